```python
import math
import jax, jax.numpy as jnp
from jax import lax
import numpy as np

D_MODEL = 1024
BATCH = 8
SEQ = 8192
DEPTH = 1

CHUNK = 64
Q_BLOCK = 128
HG_HEADS = 8
HG_DK = 128
HG_DV = D_MODEL // HG_HEADS
HG_KW = HG_HEADS * HG_DK
HG_VW = HG_HEADS * HG_DV
FOX_HEADS = 8
FOX_DH = 128
FOX_W = FOX_HEADS * FOX_DH
MEM_LEN = 256
MEM_HEADS = 4
MEM_DH = D_MODEL // MEM_HEADS
D_FF = 2816
N_BRANCH = 2
EPS = 1e-6
IN_SPLITS = [HG_KW, HG_KW, HG_VW, HG_VW, FOX_W, FOX_W, FOX_W, FOX_HEADS, N_BRANCH * D_MODEL]
IN_COLS = sum(IN_SPLITS)
IN_OFFSETS = list(np.cumsum(IN_SPLITS)[:-1])

kernel_name = "hgrn2_fox_macaron_sandwich_hybrid"


def rmsnorm(x, g):
    x32 = x.astype(jnp.float32)
    y = x32 * lax.rsqrt(jnp.mean(x32 * x32, axis=-1, keepdims=True) + EPS)
    return (y * g.astype(jnp.float32)).astype(x.dtype)


def swiglu(h, w_in, w_down):
    gate, up = jnp.split(h @ w_in, 2, axis=-1)
    return (jax.nn.silu(gate) * up) @ w_down


def hgrn2_chunkwise(q, f_logit, inp, lb):
    B, S = q.shape[0], q.shape[1]
    n = S // CHUNK
    f = lb + (1.0 - lb) * jax.nn.sigmoid(f_logit.astype(jnp.float32))
    logf = jnp.log(f)
    k = 1.0 - f
    q = jax.nn.silu(q.astype(jnp.float32))
    inp = inp.astype(jnp.float32)

    def to_chunks(t):
        return t.reshape(B, n, CHUNK, t.shape[2], t.shape[3]).transpose(1, 0, 3, 2, 4)

    qc, kc, ic = to_chunks(q), to_chunks(k), to_chunks(inp)
    bc = jnp.cumsum(to_chunks(logf), axis=3)
    mask = jnp.tril(jnp.ones((CHUNK, CHUNK), dtype=bool))[:, :, None]

    def step(state, xs):
        qt, kt, it, bt = xs
        diff = bt[:, :, :, None, :] - bt[:, :, None, :, :]
        decay = jnp.exp(jnp.where(mask, diff, -jnp.inf))
        a = jnp.einsum('bhtd,bhsd,bhtsd->bhts', qt, kt, decay)
        o_intra = jnp.einsum('bhts,bhsv->bhtv', a, it)
        o_inter = jnp.einsum('bhtd,bhdv->bhtv', qt * jnp.exp(bt), state)
        b_last = bt[:, :, -1:, :]
        new_state = jnp.exp(b_last[:, :, 0, :])[..., None] * state + jnp.einsum(
            'bhsd,bhsv->bhdv', kt * jnp.exp(b_last - bt), it)
        return new_state, o_intra + o_inter

    s0 = jnp.zeros((B, q.shape[2], HG_DK, HG_DV), jnp.float32)
    _, o = lax.scan(step, s0, (qc, kc, ic, bc))
    return o.transpose(1, 0, 3, 2, 4).reshape(B, S, HG_VW)


def fox_attention(q, k, v, f_logit):
    B, S, H, dh = q.shape
    n_blk = S // Q_BLOCK
    scale = 1.0 / math.sqrt(dh)
    c = jnp.cumsum(jax.nn.log_sigmoid(f_logit.astype(jnp.float32)), axis=1).transpose(0, 2, 1)
    k32 = k.astype(jnp.float32)
    v32 = v.astype(jnp.float32)
    qb = q.astype(jnp.float32).reshape(B, n_blk, Q_BLOCK, H, dh).transpose(1, 0, 3, 2, 4)
    cb = c.reshape(B, H, n_blk, Q_BLOCK).transpose(2, 0, 1, 3)
    kpos = jnp.arange(S)

    def block(args):
        qi, ci, blk = args
        s = jnp.einsum('bhqd,bkhd->bhqk', qi, k32) * scale
        s = s + ci[..., None] - c[:, :, None, :]
        qpos = blk * Q_BLOCK + jnp.arange(Q_BLOCK)
        s = jnp.where(kpos[None, :] <= qpos[:, None], s, -jnp.inf)
        p = jax.nn.softmax(s, axis=-1)
        return jnp.einsum('bhqk,bkhd->bqhd', p, v32)

    o = lax.map(block, (qb, cb, jnp.arange(n_blk)))
    return o.transpose(1, 0, 2, 3, 4).reshape(B, S, H * dh)


def mem_cross_attention(h, mem_n, w_mq, w_mkv, w_mo):
    B, S, _ = h.shape
    q = (h @ w_mq).reshape(B, S, MEM_HEADS, MEM_DH).astype(jnp.float32)
    k, v = jnp.split(mem_n @ w_mkv, 2, axis=-1)
    k = k.reshape(B, MEM_LEN, MEM_HEADS, MEM_DH).astype(jnp.float32)
    v = v.reshape(B, MEM_LEN, MEM_HEADS, MEM_DH).astype(jnp.float32)
    s = jnp.einsum('bqhd,bkhd->bhqk', q, k) * (1.0 / math.sqrt(MEM_DH))
    p = jax.nn.softmax(s, axis=-1)
    o = jnp.einsum('bhqk,bkhd->bqhd', p, v).reshape(B, S, D_MODEL).astype(h.dtype)
    return o @ w_mo


def _fwd_setup_inputs(seed: int = 0) -> dict:
    key = jax.random.key(seed)
    ks = iter(jax.random.split(key, 40))
    f32 = jnp.float32

    def w(shape, fan_in):
        return jax.random.normal(next(ks), shape, f32) * (fan_in ** -0.5)

    def gain(shape):
        return 1.0 + 0.05 * jax.random.normal(next(ks), shape, f32)

    L, D = DEPTH, D_MODEL
    return {
        "x": jax.random.normal(next(ks), (BATCH, SEQ, D), f32),
        "mem": jax.random.normal(next(ks), (BATCH, MEM_LEN, D), f32),
        "ffn1_pre_g": gain((L, D)),
        "ffn1_w_in": w((L, D, 2 * D_FF), D),
        "ffn1_w_down": w((L, D_FF, D), D_FF),
        "ffn1_post_g": gain((L, D)),
        "mix_pre_g": gain((L, D)),
        "w_in": w((L, D, IN_COLS), D),
        "hg_lb_logits": 0.5 * jax.random.normal(next(ks), (L + 1, HG_HEADS, HG_DK), f32),
        "hg_norm_g": gain((L, HG_VW)),
        "fox_f_bias": 1.0 + 0.5 * jax.random.normal(next(ks), (L, FOX_HEADS), f32),
        "w_branch_a": w((L, HG_VW, D), HG_VW),
        "w_branch_b": w((L, FOX_W, D), FOX_W),
        "b_gate": 0.02 * jax.random.normal(next(ks), (L, N_BRANCH * D), f32),
        "w_out": w((L, D, D), D),
        "mix_post_g": gain((L, D)),
        "mem_pre_g": gain((L, D)),
        "mem_kv_g": gain((L, D)),
        "w_mq": w((L, D, D), D),
        "w_mkv": w((L, D, 2 * D), D),
        "w_mo": w((L, D, D), D),
        "mem_post_g": gain((L, D)),
        "ffn2_pre_g": gain((L, D)),
        "ffn2_w_in": w((L, D, 2 * D_FF), D),
        "ffn2_w_down": w((L, D_FF, D), D_FF),
        "ffn2_post_g": gain((L, D)),
    }


def _fwd_reference(x, mem, ffn1_pre_g, ffn1_w_in, ffn1_w_down, ffn1_post_g,
              mix_pre_g, w_in, hg_lb_logits, hg_norm_g, fox_f_bias,
              w_branch_a, w_branch_b, b_gate, w_out, mix_post_g,
              mem_pre_g, mem_kv_g, w_mq, w_mkv, w_mo, mem_post_g,
              ffn2_pre_g, ffn2_w_in, ffn2_w_down, ffn2_post_g):
    B, S, D = x.shape
    lb_all = jnp.cumsum(jax.nn.softmax(hg_lb_logits.astype(jnp.float32), axis=0), axis=0)
    for l in range(DEPTH):
        h = rmsnorm(x, ffn1_pre_g[l])
        x = x + 0.5 * rmsnorm(swiglu(h, ffn1_w_in[l], ffn1_w_down[l]), ffn1_post_g[l])

        h = rmsnorm(x, mix_pre_g[l])
        q_a, f_a, i_a, g_a, q_b, k_b, v_b, f_b, gates = jnp.split(h @ w_in[l], IN_OFFSETS, axis=-1)

        o_a = hgrn2_chunkwise(q_a.reshape(B, S, HG_HEADS, HG_DK),
                              f_a.reshape(B, S, HG_HEADS, HG_DK),
                              i_a.reshape(B, S, HG_HEADS, HG_DV), lb_all[l])
        o_a = rmsnorm(o_a.astype(x.dtype), hg_norm_g[l]) * jax.nn.silu(g_a)
        y_a = o_a @ w_branch_a[l]

        o_b = fox_attention(q_b.reshape(B, S, FOX_HEADS, FOX_DH),
                            k_b.reshape(B, S, FOX_HEADS, FOX_DH),
                            v_b.reshape(B, S, FOX_HEADS, FOX_DH),
                            f_b + fox_f_bias[l])
        y_b = o_b.astype(x.dtype) @ w_branch_b[l]

        gate = jax.nn.sigmoid((gates + b_gate[l]).astype(jnp.float32)).reshape(B, S, N_BRANCH, D)
        y = (gate[:, :, 0] * y_a.astype(jnp.float32) + gate[:, :, 1] * y_b.astype(jnp.float32)).astype(x.dtype)
        x = x + rmsnorm(y @ w_out[l], mix_post_g[l])

        h = rmsnorm(x, mem_pre_g[l])
        mem_n = rmsnorm(mem, mem_kv_g[l])
        x = x + rmsnorm(mem_cross_attention(h, mem_n, w_mq[l], w_mkv[l], w_mo[l]), mem_post_g[l])

        h = rmsnorm(x, ffn2_pre_g[l])
        x = x + 0.5 * rmsnorm(swiglu(h, ffn2_w_in[l], ffn2_w_down[l]), ffn2_post_g[l])
    return x


import jax as _jax
import jax.numpy as _jnp

TWIN_FORMAT = 'train_step'
FWD_PARAMS = ['x', 'mem', 'ffn1_pre_g', 'ffn1_w_in', 'ffn1_w_down', 'ffn1_post_g', 'mix_pre_g', 'w_in', 'hg_lb_logits', 'hg_norm_g', 'fox_f_bias', 'w_branch_a', 'w_branch_b', 'b_gate', 'w_out', 'mix_post_g', 'mem_pre_g', 'mem_kv_g', 'w_mq', 'w_mkv', 'w_mo', 'mem_post_g', 'ffn2_pre_g', 'ffn2_w_in', 'ffn2_w_down', 'ffn2_post_g']
TWIN_WEIGHTS = ['ffn1_pre_g', 'ffn1_w_in', 'ffn1_w_down', 'ffn1_post_g', 'mix_pre_g', 'w_in', 'hg_lb_logits', 'hg_norm_g', 'fox_f_bias', 'w_branch_a', 'w_branch_b', 'b_gate', 'w_out', 'mix_post_g', 'mem_pre_g', 'mem_kv_g', 'w_mq', 'w_mkv', 'w_mo', 'mem_post_g', 'ffn2_pre_g', 'ffn2_w_in', 'ffn2_w_down', 'ffn2_post_g']
TWIN_DIFF_INPUT = 'x'
TWIN_INPUTS = ['x', 'mem', 'ffn1_pre_g', 'ffn1_w_in', 'ffn1_w_down', 'ffn1_post_g', 'mix_pre_g', 'w_in', 'hg_lb_logits', 'hg_norm_g', 'fox_f_bias', 'w_branch_a', 'w_branch_b', 'b_gate', 'w_out', 'mix_post_g', 'mem_pre_g', 'mem_kv_g', 'w_mq', 'w_mkv', 'w_mo', 'mem_post_g', 'ffn2_pre_g', 'ffn2_w_in', 'ffn2_w_down', 'ffn2_post_g', 'loss_target', 'm_ffn1_pre_g', 'm_ffn1_w_in', 'm_ffn1_w_down', 'm_ffn1_post_g', 'm_mix_pre_g', 'm_w_in', 'm_hg_lb_logits', 'm_hg_norm_g', 'm_fox_f_bias', 'm_w_branch_a', 'm_w_branch_b', 'm_b_gate', 'm_w_out', 'm_mix_post_g', 'm_mem_pre_g', 'm_mem_kv_g', 'm_w_mq', 'm_w_mkv', 'm_w_mo', 'm_mem_post_g', 'm_ffn2_pre_g', 'm_ffn2_w_in', 'm_ffn2_w_down', 'm_ffn2_post_g', 'v_ffn1_pre_g', 'v_ffn1_w_in', 'v_ffn1_w_down', 'v_ffn1_post_g', 'v_mix_pre_g', 'v_w_in', 'v_hg_lb_logits', 'v_hg_norm_g', 'v_fox_f_bias', 'v_w_branch_a', 'v_w_branch_b', 'v_b_gate', 'v_w_out', 'v_mix_post_g', 'v_mem_pre_g', 'v_mem_kv_g', 'v_w_mq', 'v_w_mkv', 'v_w_mo', 'v_mem_post_g', 'v_ffn2_pre_g', 'v_ffn2_w_in', 'v_ffn2_w_down', 'v_ffn2_post_g']
TWIN_OUTPUTS = ['loss', 'grad_x', 'grad_ffn1_pre_g', 'grad_ffn1_w_in', 'grad_ffn1_w_down', 'grad_ffn1_post_g', 'grad_mix_pre_g', 'grad_w_in', 'grad_hg_lb_logits', 'grad_hg_norm_g', 'grad_fox_f_bias', 'grad_w_branch_a', 'grad_w_branch_b', 'grad_b_gate', 'grad_w_out', 'grad_mix_post_g', 'grad_mem_pre_g', 'grad_mem_kv_g', 'grad_w_mq', 'grad_w_mkv', 'grad_w_mo', 'grad_mem_post_g', 'grad_ffn2_pre_g', 'grad_ffn2_w_in', 'grad_ffn2_w_down', 'grad_ffn2_post_g', 'delta_ffn1_pre_g', 'delta_ffn1_w_in', 'delta_ffn1_w_down', 'delta_ffn1_post_g', 'delta_mix_pre_g', 'delta_w_in', 'delta_hg_lb_logits', 'delta_hg_norm_g', 'delta_fox_f_bias', 'delta_w_branch_a', 'delta_w_branch_b', 'delta_b_gate', 'delta_w_out', 'delta_mix_post_g', 'delta_mem_pre_g', 'delta_mem_kv_g', 'delta_w_mq', 'delta_w_mkv', 'delta_w_mo', 'delta_mem_post_g', 'delta_ffn2_pre_g', 'delta_ffn2_w_in', 'delta_ffn2_w_down', 'delta_ffn2_post_g', 'new_m_ffn1_pre_g', 'new_m_ffn1_w_in', 'new_m_ffn1_w_down', 'new_m_ffn1_post_g', 'new_m_mix_pre_g', 'new_m_w_in', 'new_m_hg_lb_logits', 'new_m_hg_norm_g', 'new_m_fox_f_bias', 'new_m_w_branch_a', 'new_m_w_branch_b', 'new_m_b_gate', 'new_m_w_out', 'new_m_mix_post_g', 'new_m_mem_pre_g', 'new_m_mem_kv_g', 'new_m_w_mq', 'new_m_w_mkv', 'new_m_w_mo', 'new_m_mem_post_g', 'new_m_ffn2_pre_g', 'new_m_ffn2_w_in', 'new_m_ffn2_w_down', 'new_m_ffn2_post_g', 'new_v_ffn1_pre_g', 'new_v_ffn1_w_in', 'new_v_ffn1_w_down', 'new_v_ffn1_post_g', 'new_v_mix_pre_g', 'new_v_w_in', 'new_v_hg_lb_logits', 'new_v_hg_norm_g', 'new_v_fox_f_bias', 'new_v_w_branch_a', 'new_v_w_branch_b', 'new_v_b_gate', 'new_v_w_out', 'new_v_mix_post_g', 'new_v_mem_pre_g', 'new_v_mem_kv_g', 'new_v_w_mq', 'new_v_w_mkv', 'new_v_w_mo', 'new_v_mem_post_g', 'new_v_ffn2_pre_g', 'new_v_ffn2_w_in', 'new_v_ffn2_w_down', 'new_v_ffn2_post_g']
TWIN_LEAF_KINDS = {'loss': 'loss', 'grad_x': 'grad_x', 'grad_ffn1_pre_g': 'grad_w', 'grad_ffn1_w_in': 'grad_w', 'grad_ffn1_w_down': 'grad_w', 'grad_ffn1_post_g': 'grad_w', 'grad_mix_pre_g': 'grad_w', 'grad_w_in': 'grad_w', 'grad_hg_lb_logits': 'grad_w', 'grad_hg_norm_g': 'grad_w', 'grad_fox_f_bias': 'grad_w', 'grad_w_branch_a': 'grad_w', 'grad_w_branch_b': 'grad_w', 'grad_b_gate': 'grad_w', 'grad_w_out': 'grad_w', 'grad_mix_post_g': 'grad_w', 'grad_mem_pre_g': 'grad_w', 'grad_mem_kv_g': 'grad_w', 'grad_w_mq': 'grad_w', 'grad_w_mkv': 'grad_w', 'grad_w_mo': 'grad_w', 'grad_mem_post_g': 'grad_w', 'grad_ffn2_pre_g': 'grad_w', 'grad_ffn2_w_in': 'grad_w', 'grad_ffn2_w_down': 'grad_w', 'grad_ffn2_post_g': 'grad_w', 'delta_ffn1_pre_g': 'delta_w', 'delta_ffn1_w_in': 'delta_w', 'delta_ffn1_w_down': 'delta_w', 'delta_ffn1_post_g': 'delta_w', 'delta_mix_pre_g': 'delta_w', 'delta_w_in': 'delta_w', 'delta_hg_lb_logits': 'delta_w', 'delta_hg_norm_g': 'delta_w', 'delta_fox_f_bias': 'delta_w', 'delta_w_branch_a': 'delta_w', 'delta_w_branch_b': 'delta_w', 'delta_b_gate': 'delta_w', 'delta_w_out': 'delta_w', 'delta_mix_post_g': 'delta_w', 'delta_mem_pre_g': 'delta_w', 'delta_mem_kv_g': 'delta_w', 'delta_w_mq': 'delta_w', 'delta_w_mkv': 'delta_w', 'delta_w_mo': 'delta_w', 'delta_mem_post_g': 'delta_w', 'delta_ffn2_pre_g': 'delta_w', 'delta_ffn2_w_in': 'delta_w', 'delta_ffn2_w_down': 'delta_w', 'delta_ffn2_post_g': 'delta_w', 'new_m_ffn1_pre_g': 'new_m', 'new_m_ffn1_w_in': 'new_m', 'new_m_ffn1_w_down': 'new_m', 'new_m_ffn1_post_g': 'new_m', 'new_m_mix_pre_g': 'new_m', 'new_m_w_in': 'new_m', 'new_m_hg_lb_logits': 'new_m', 'new_m_hg_norm_g': 'new_m', 'new_m_fox_f_bias': 'new_m', 'new_m_w_branch_a': 'new_m', 'new_m_w_branch_b': 'new_m', 'new_m_b_gate': 'new_m', 'new_m_w_out': 'new_m', 'new_m_mix_post_g': 'new_m', 'new_m_mem_pre_g': 'new_m', 'new_m_mem_kv_g': 'new_m', 'new_m_w_mq': 'new_m', 'new_m_w_mkv': 'new_m', 'new_m_w_mo': 'new_m', 'new_m_mem_post_g': 'new_m', 'new_m_ffn2_pre_g': 'new_m', 'new_m_ffn2_w_in': 'new_m', 'new_m_ffn2_w_down': 'new_m', 'new_m_ffn2_post_g': 'new_m', 'new_v_ffn1_pre_g': 'new_v', 'new_v_ffn1_w_in': 'new_v', 'new_v_ffn1_w_down': 'new_v', 'new_v_ffn1_post_g': 'new_v', 'new_v_mix_pre_g': 'new_v', 'new_v_w_in': 'new_v', 'new_v_hg_lb_logits': 'new_v', 'new_v_hg_norm_g': 'new_v', 'new_v_fox_f_bias': 'new_v', 'new_v_w_branch_a': 'new_v', 'new_v_w_branch_b': 'new_v', 'new_v_b_gate': 'new_v', 'new_v_w_out': 'new_v', 'new_v_mix_post_g': 'new_v', 'new_v_mem_pre_g': 'new_v', 'new_v_mem_kv_g': 'new_v', 'new_v_w_mq': 'new_v', 'new_v_w_mkv': 'new_v', 'new_v_w_mo': 'new_v', 'new_v_mem_post_g': 'new_v', 'new_v_ffn2_pre_g': 'new_v', 'new_v_ffn2_w_in': 'new_v', 'new_v_ffn2_w_down': 'new_v', 'new_v_ffn2_post_g': 'new_v'}


def _forward(args):
    return _fwd_reference(*[args[k] for k in FWD_PARAMS])


def _output_shape():
    def fwd():
        inp = _fwd_setup_inputs(0)
        return _fwd_reference(*[inp[k] for k in FWD_PARAMS])
    out = _jax.eval_shape(fwd)
    return out.shape, out.dtype

N_MICROBATCH = 1
ADAM_LR = 0.001
ADAM_B1 = 0.9
ADAM_B2 = 0.999
ADAM_EPS = 1e-08
ADAM_WD = 0.01
ADAM_STEP = 10
PER_EXAMPLE_BATCH_AXIS = {'x': 0, 'mem': 0, 'loss_target': 0}
SHARED_INPUTS = []
_WEIGHT_DTYPES = {'ffn1_pre_g': _jnp.float32, 'ffn1_w_in': _jnp.float32, 'ffn1_w_down': _jnp.float32, 'ffn1_post_g': _jnp.float32, 'mix_pre_g': _jnp.float32, 'w_in': _jnp.float32, 'hg_lb_logits': _jnp.float32, 'hg_norm_g': _jnp.float32, 'fox_f_bias': _jnp.float32, 'w_branch_a': _jnp.float32, 'w_branch_b': _jnp.float32, 'b_gate': _jnp.float32, 'w_out': _jnp.float32, 'mix_post_g': _jnp.float32, 'mem_pre_g': _jnp.float32, 'mem_kv_g': _jnp.float32, 'w_mq': _jnp.float32, 'w_mkv': _jnp.float32, 'w_mo': _jnp.float32, 'mem_post_g': _jnp.float32, 'ffn2_pre_g': _jnp.float32, 'ffn2_w_in': _jnp.float32, 'ffn2_w_down': _jnp.float32, 'ffn2_post_g': _jnp.float32}
MOMENT_SCALE = {'ffn1_pre_g': 8.227589e-01, 'ffn1_w_in': 3.527829e-01, 'ffn1_w_down': 6.098337e-01, 'ffn1_post_g': 1.564506e+01, 'mix_pre_g': 1.197544e+00, 'w_in': 3.625209e-01, 'hg_lb_logits': 5.182953e-02, 'hg_norm_g': 5.592046e-01, 'fox_f_bias': 2.260423e+00, 'w_branch_a': 5.690666e-01, 'w_branch_b': 6.890626e-01, 'b_gate': 2.670677e-01, 'w_out': 1.005832e+00, 'mix_post_g': 6.407377e+01, 'mem_pre_g': 9.415926e-01, 'mem_kv_g': 2.194915e+00, 'w_mq': 9.283924e-01, 'w_mkv': 1.278908e+00, 'w_mo': 1.634851e+00, 'mem_post_g': 6.505182e+01, 'ffn2_pre_g': 1.131252e+00, 'ffn2_w_in': 4.524013e-01, 'ffn2_w_down': 9.545448e-01, 'ffn2_post_g': 1.611975e+01}


def _to_microbatches(a, axis):
    t = _jnp.moveaxis(a, axis, 0)
    t = t.reshape((N_MICROBATCH, t.shape[0] // N_MICROBATCH) + t.shape[1:])
    return _jnp.moveaxis(t, 1, axis + 1)


def setup_inputs(seed: int = 0) -> dict:
    inp = _fwd_setup_inputs(seed)
    key = _jax.random.fold_in(_jax.random.key(seed), 7919)
    shape, _ = _output_shape()
    out = dict(inp)
    out["loss_target"] = _jax.random.normal(_jax.random.fold_in(key, 0), shape, _jnp.float32)
    for i, name in enumerate(TWIN_WEIGHTS):
        w = inp[name].astype(_jnp.float32)
        if MOMENT_SCALE is None:
            s = _jnp.sqrt(_jnp.mean(_jnp.square(w)) + 1e-30)
        else:
            s = MOMENT_SCALE[name]
        km, kv = _jax.random.split(_jax.random.fold_in(key, i + 1))
        out[name] = w
        out["m_" + name] = s * _jax.random.normal(km, w.shape, _jnp.float32)
        out["v_" + name] = (s * s) * _jax.random.uniform(kv, w.shape, _jnp.float32, 0.5, 1.5)
    if N_MICROBATCH > 1:
        for name, axis in PER_EXAMPLE_BATCH_AXIS.items():
            out[name] = _to_microbatches(out[name], axis)
    return {'x': out['x'], 'mem': out['mem'], 'ffn1_pre_g': out['ffn1_pre_g'], 'ffn1_w_in': out['ffn1_w_in'], 'ffn1_w_down': out['ffn1_w_down'], 'ffn1_post_g': out['ffn1_post_g'], 'mix_pre_g': out['mix_pre_g'], 'w_in': out['w_in'], 'hg_lb_logits': out['hg_lb_logits'], 'hg_norm_g': out['hg_norm_g'], 'fox_f_bias': out['fox_f_bias'], 'w_branch_a': out['w_branch_a'], 'w_branch_b': out['w_branch_b'], 'b_gate': out['b_gate'], 'w_out': out['w_out'], 'mix_post_g': out['mix_post_g'], 'mem_pre_g': out['mem_pre_g'], 'mem_kv_g': out['mem_kv_g'], 'w_mq': out['w_mq'], 'w_mkv': out['w_mkv'], 'w_mo': out['w_mo'], 'mem_post_g': out['mem_post_g'], 'ffn2_pre_g': out['ffn2_pre_g'], 'ffn2_w_in': out['ffn2_w_in'], 'ffn2_w_down': out['ffn2_w_down'], 'ffn2_post_g': out['ffn2_post_g'], 'loss_target': out['loss_target'], 'm_ffn1_pre_g': out['m_ffn1_pre_g'], 'm_ffn1_w_in': out['m_ffn1_w_in'], 'm_ffn1_w_down': out['m_ffn1_w_down'], 'm_ffn1_post_g': out['m_ffn1_post_g'], 'm_mix_pre_g': out['m_mix_pre_g'], 'm_w_in': out['m_w_in'], 'm_hg_lb_logits': out['m_hg_lb_logits'], 'm_hg_norm_g': out['m_hg_norm_g'], 'm_fox_f_bias': out['m_fox_f_bias'], 'm_w_branch_a': out['m_w_branch_a'], 'm_w_branch_b': out['m_w_branch_b'], 'm_b_gate': out['m_b_gate'], 'm_w_out': out['m_w_out'], 'm_mix_post_g': out['m_mix_post_g'], 'm_mem_pre_g': out['m_mem_pre_g'], 'm_mem_kv_g': out['m_mem_kv_g'], 'm_w_mq': out['m_w_mq'], 'm_w_mkv': out['m_w_mkv'], 'm_w_mo': out['m_w_mo'], 'm_mem_post_g': out['m_mem_post_g'], 'm_ffn2_pre_g': out['m_ffn2_pre_g'], 'm_ffn2_w_in': out['m_ffn2_w_in'], 'm_ffn2_w_down': out['m_ffn2_w_down'], 'm_ffn2_post_g': out['m_ffn2_post_g'], 'v_ffn1_pre_g': out['v_ffn1_pre_g'], 'v_ffn1_w_in': out['v_ffn1_w_in'], 'v_ffn1_w_down': out['v_ffn1_w_down'], 'v_ffn1_post_g': out['v_ffn1_post_g'], 'v_mix_pre_g': out['v_mix_pre_g'], 'v_w_in': out['v_w_in'], 'v_hg_lb_logits': out['v_hg_lb_logits'], 'v_hg_norm_g': out['v_hg_norm_g'], 'v_fox_f_bias': out['v_fox_f_bias'], 'v_w_branch_a': out['v_w_branch_a'], 'v_w_branch_b': out['v_w_branch_b'], 'v_b_gate': out['v_b_gate'], 'v_w_out': out['v_w_out'], 'v_mix_post_g': out['v_mix_post_g'], 'v_mem_pre_g': out['v_mem_pre_g'], 'v_mem_kv_g': out['v_mem_kv_g'], 'v_w_mq': out['v_w_mq'], 'v_w_mkv': out['v_w_mkv'], 'v_w_mo': out['v_w_mo'], 'v_mem_post_g': out['v_mem_post_g'], 'v_ffn2_pre_g': out['v_ffn2_pre_g'], 'v_ffn2_w_in': out['v_ffn2_w_in'], 'v_ffn2_w_down': out['v_ffn2_w_down'], 'v_ffn2_post_g': out['v_ffn2_post_g']}


def _loss(weights, diff, rest, loss_target):
    with _jax.named_scope("forward"):
        args = {**rest, TWIN_DIFF_INPUT: diff, **{k: w.astype(_WEIGHT_DTYPES[k]) for k, w in weights.items()}}
        y = _forward(args)
    with _jax.named_scope("loss_head"):
        err = _jnp.square(y.astype(_jnp.float32) - loss_target)
        return 0.5 * _jnp.sum(_jnp.mean(err, axis=-1)) if err.ndim else 0.5 * err


def _adamw(w, g, m, v):
    m = ADAM_B1 * m + (1.0 - ADAM_B1) * g
    v = ADAM_B2 * v + (1.0 - ADAM_B2) * _jnp.square(g)
    m_hat = m / (1.0 - ADAM_B1 ** ADAM_STEP)
    v_hat = v / (1.0 - ADAM_B2 ** ADAM_STEP)
    delta = -ADAM_LR * (m_hat / (_jnp.sqrt(v_hat) + ADAM_EPS) + ADAM_WD * w)
    return delta, m, v


def reference(x, mem, ffn1_pre_g, ffn1_w_in, ffn1_w_down, ffn1_post_g, mix_pre_g, w_in, hg_lb_logits, hg_norm_g, fox_f_bias, w_branch_a, w_branch_b, b_gate, w_out, mix_post_g, mem_pre_g, mem_kv_g, w_mq, w_mkv, w_mo, mem_post_g, ffn2_pre_g, ffn2_w_in, ffn2_w_down, ffn2_post_g, loss_target, m_ffn1_pre_g, m_ffn1_w_in, m_ffn1_w_down, m_ffn1_post_g, m_mix_pre_g, m_w_in, m_hg_lb_logits, m_hg_norm_g, m_fox_f_bias, m_w_branch_a, m_w_branch_b, m_b_gate, m_w_out, m_mix_post_g, m_mem_pre_g, m_mem_kv_g, m_w_mq, m_w_mkv, m_w_mo, m_mem_post_g, m_ffn2_pre_g, m_ffn2_w_in, m_ffn2_w_down, m_ffn2_post_g, v_ffn1_pre_g, v_ffn1_w_in, v_ffn1_w_down, v_ffn1_post_g, v_mix_pre_g, v_w_in, v_hg_lb_logits, v_hg_norm_g, v_fox_f_bias, v_w_branch_a, v_w_branch_b, v_b_gate, v_w_out, v_mix_post_g, v_mem_pre_g, v_mem_kv_g, v_w_mq, v_w_mkv, v_w_mo, v_mem_post_g, v_ffn2_pre_g, v_ffn2_w_in, v_ffn2_w_down, v_ffn2_post_g):
    given = dict(x=x, mem=mem, ffn1_pre_g=ffn1_pre_g, ffn1_w_in=ffn1_w_in, ffn1_w_down=ffn1_w_down, ffn1_post_g=ffn1_post_g, mix_pre_g=mix_pre_g, w_in=w_in, hg_lb_logits=hg_lb_logits, hg_norm_g=hg_norm_g, fox_f_bias=fox_f_bias, w_branch_a=w_branch_a, w_branch_b=w_branch_b, b_gate=b_gate, w_out=w_out, mix_post_g=mix_post_g, mem_pre_g=mem_pre_g, mem_kv_g=mem_kv_g, w_mq=w_mq, w_mkv=w_mkv, w_mo=w_mo, mem_post_g=mem_post_g, ffn2_pre_g=ffn2_pre_g, ffn2_w_in=ffn2_w_in, ffn2_w_down=ffn2_w_down, ffn2_post_g=ffn2_post_g, loss_target=loss_target, m_ffn1_pre_g=m_ffn1_pre_g, m_ffn1_w_in=m_ffn1_w_in, m_ffn1_w_down=m_ffn1_w_down, m_ffn1_post_g=m_ffn1_post_g, m_mix_pre_g=m_mix_pre_g, m_w_in=m_w_in, m_hg_lb_logits=m_hg_lb_logits, m_hg_norm_g=m_hg_norm_g, m_fox_f_bias=m_fox_f_bias, m_w_branch_a=m_w_branch_a, m_w_branch_b=m_w_branch_b, m_b_gate=m_b_gate, m_w_out=m_w_out, m_mix_post_g=m_mix_post_g, m_mem_pre_g=m_mem_pre_g, m_mem_kv_g=m_mem_kv_g, m_w_mq=m_w_mq, m_w_mkv=m_w_mkv, m_w_mo=m_w_mo, m_mem_post_g=m_mem_post_g, m_ffn2_pre_g=m_ffn2_pre_g, m_ffn2_w_in=m_ffn2_w_in, m_ffn2_w_down=m_ffn2_w_down, m_ffn2_post_g=m_ffn2_post_g, v_ffn1_pre_g=v_ffn1_pre_g, v_ffn1_w_in=v_ffn1_w_in, v_ffn1_w_down=v_ffn1_w_down, v_ffn1_post_g=v_ffn1_post_g, v_mix_pre_g=v_mix_pre_g, v_w_in=v_w_in, v_hg_lb_logits=v_hg_lb_logits, v_hg_norm_g=v_hg_norm_g, v_fox_f_bias=v_fox_f_bias, v_w_branch_a=v_w_branch_a, v_w_branch_b=v_w_branch_b, v_b_gate=v_b_gate, v_w_out=v_w_out, v_mix_post_g=v_mix_post_g, v_mem_pre_g=v_mem_pre_g, v_mem_kv_g=v_mem_kv_g, v_w_mq=v_w_mq, v_w_mkv=v_w_mkv, v_w_mo=v_w_mo, v_mem_post_g=v_mem_post_g, v_ffn2_pre_g=v_ffn2_pre_g, v_ffn2_w_in=v_ffn2_w_in, v_ffn2_w_down=v_ffn2_w_down, v_ffn2_post_g=v_ffn2_post_g)
    weights = {n: given[n] for n in TWIN_WEIGHTS}
    shared = {n: given[n] for n in SHARED_INPUTS}
    per_example = {n: given[n] for n in ['x', 'mem']}
    grad_fn = _jax.value_and_grad(_loss, argnums=(0, 1))

    def one_microbatch(ex, loss_target):
        ex = dict(ex)
        diff = ex.pop(TWIN_DIFF_INPUT)
        return grad_fn(weights, diff, {**shared, **ex}, loss_target)

    if N_MICROBATCH == 1:
        loss, (grad_w, grad_x) = one_microbatch(per_example, given["loss_target"])
    else:
        def body(carry, xs):
            loss_sum, grad_sum = carry
            l_k, (gw_k, gx_k) = one_microbatch(xs[0], xs[1])
            with _jax.named_scope("update"):
                return (loss_sum + l_k, _jax.tree.map(_jnp.add, grad_sum, gw_k)), gx_k

        init = (_jnp.zeros((), _jnp.float32), _jax.tree.map(_jnp.zeros_like, weights))
        (loss, grad_w), grad_x = _jax.lax.scan(body, init, (per_example, given["loss_target"]))
    with _jax.named_scope("update"):
        delta_w, new_m, new_v = {}, {}, {}
        for n in TWIN_WEIGHTS:
            delta_w[n], new_m[n], new_v[n] = _adamw(weights[n], grad_w[n], given["m_" + n], given["v_" + n])
    return (loss, grad_x, *[grad_w[n] for n in TWIN_WEIGHTS], *[delta_w[n] for n in TWIN_WEIGHTS],
            *[new_m[n] for n in TWIN_WEIGHTS], *[new_v[n] for n in TWIN_WEIGHTS])
```

```python
import functools
import math

import jax
import jax.numpy as jnp
from jax import lax
from jax.experimental import pallas as pl
from jax.experimental.pallas import tpu as pltpu

F32 = jnp.float32
BF16 = jnp.bfloat16
MESH = pl.DeviceIdType.MESH

EPS = 1e-6
HEADS = 8
DH = 128
MEM_HEADS = 4
CHUNK = 128
HALF = CHUNK // 2
SWIGLU_TILE = 256
LANES = 128
PACK_COLS = 1024
PACK_ROW_TILE = 256
ROW_TILE = 512
SEQ_BLOCK = 512
ATTN_TILE = 512
EXP_CLAMP = 80.0
NEG_BIG = -1e30

ADAM_LR, ADAM_B1, ADAM_B2, ADAM_EPS, ADAM_WD, ADAM_STEP = 0.001, 0.9, 0.999, 1e-08, 0.01, 10

VMEM_LIMIT = 48 * 1024 * 1024

_DN = {
    "nn": (((1,), (0,)), ((), ())),
    "nt": (((1,), (1,)), ((), ())),
    "tn": (((0,), (0,)), ((), ())),
}

BIG = ["ffn1_w_in", "ffn1_w_down", "w_in", "w_branch_a", "w_branch_b", "w_out", "w_mq", "w_mkv", "w_mo",
       "ffn2_w_in", "ffn2_w_down"]
COL_SHARDED = {"ffn1_w_in", "w_in", "w_mkv", "ffn2_w_in"}
SMALL = ["ffn1_pre_g", "ffn1_post_g", "mix_pre_g", "hg_lb_logits", "hg_norm_g", "fox_f_bias", "b_gate",
         "mix_post_g", "mem_pre_g", "mem_kv_g", "mem_post_g", "ffn2_pre_g", "ffn2_post_g"]
WEIGHTS = ["ffn1_pre_g", "ffn1_w_in", "ffn1_w_down", "ffn1_post_g", "mix_pre_g", "w_in", "hg_lb_logits",
           "hg_norm_g", "fox_f_bias", "w_branch_a", "w_branch_b", "b_gate", "w_out", "mix_post_g", "mem_pre_g",
           "mem_kv_g", "w_mq", "w_mkv", "w_mo", "mem_post_g", "ffn2_pre_g", "ffn2_w_in", "ffn2_w_down",
           "ffn2_post_g"]


def _dot(a, b, mode="nn"):
    return lax.dot_general(a, b, _DN[mode], preferred_element_type=F32)


def _sig(x):
    return 1.0 / (1.0 + jnp.exp(-x))


def _params(*dims):
    return pltpu.CompilerParams(dimension_semantics=dims if dims else None, vmem_limit_bytes=VMEM_LIMIT)


def _tile(dim, pref):
    if dim <= pref:
        return dim
    t = (pref // LANES) * LANES
    while t >= LANES:
        if dim % t == 0:
            return t
        t -= LANES
    raise ValueError(f"no tile for {dim}")


def _colsum(x):
    return jnp.sum(x, axis=0, keepdims=True)


def _rowsum(x):
    return jnp.sum(x, axis=1, keepdims=True)


def _iota(shape, axis):
    return lax.broadcasted_iota(jnp.int32, shape, axis)


def _pick_row(x, r):
    return _colsum(jnp.where(_iota(x.shape, 0) == r, x, 0.0))


def _tri_dot(tri, x):
    hi = x.astype(BF16)
    r1 = x - hi.astype(F32)
    mid = r1.astype(BF16)
    lo = (r1 - mid.astype(F32)).astype(BF16)
    return _dot(tri, hi) + _dot(tri, mid) + _dot(tri, lo)


def _mm(a, b, mode, out_dtype, name, add=None, tm=1024, tn=512, tk=1024):
    if mode == "nn":
        (M, K), (K2, N) = a.shape, b.shape
    elif mode == "nt":
        (M, K), (N, K2) = a.shape, b.shape
    else:
        (K, M), (K2, N) = a.shape, b.shape
    assert K == K2, (name, a.shape, b.shape)
    tm, tn, tk = _tile(M, tm), _tile(N, tn), _tile(K, tk)
    nk = K // tk
    if mode == "tn":
        a_spec = pl.BlockSpec((tk, tm), lambda i, j, k: (k, i))
    else:
        a_spec = pl.BlockSpec((tm, tk), lambda i, j, k: (i, k))
    if mode == "nt":
        b_spec = pl.BlockSpec((tn, tk), lambda i, j, k: (j, k))
    else:
        b_spec = pl.BlockSpec((tk, tn), lambda i, j, k: (k, j))
    o_spec = pl.BlockSpec((tm, tn), lambda i, j, k: (i, j))
    has_add = add is not None

    def body(*refs):
        a_ref, b_ref = refs[0], refs[1]
        c_ref = refs[2] if has_add else None
        o_ref = refs[3] if has_add else refs[2]
        part = _dot(a_ref[...], b_ref[...], mode)
        if nk == 1:
            if has_add:
                part = part + c_ref[...]
            o_ref[...] = part.astype(o_ref.dtype)
            return
        acc_ref = refs[-1]
        k = pl.program_id(2)

        @pl.when(k == 0)
        def _():
            acc_ref[...] = part + c_ref[...] if has_add else part

        @pl.when(k > 0)
        def _():
            acc_ref[...] += part

        @pl.when(k == nk - 1)
        def _():
            o_ref[...] = acc_ref[...].astype(o_ref.dtype)

    in_specs = [a_spec, b_spec] + ([o_spec] if has_add else [])
    args = (a, b) + ((add,) if has_add else ())
    return pl.pallas_call(
        body, name=name, grid=(M // tm, N // tn, nk), in_specs=in_specs, out_specs=o_spec,
        out_shape=jax.ShapeDtypeStruct((M, N), out_dtype),
        scratch_shapes=[pltpu.VMEM((tm, tn), F32)] if nk > 1 else [],
        compiler_params=_params("parallel", "parallel", "arbitrary"),
    )(*args)


def _rms_fwd(x, g, name, out_dtype=BF16):
    T, D = x.shape
    tr = _tile(T, ROW_TILE)

    def body(x_ref, g_ref, o_ref):
        xv = x_ref[...]
        r = lax.rsqrt(jnp.mean(xv * xv, axis=-1, keepdims=True) + EPS)
        o_ref[...] = (xv * r * g_ref[...]).astype(o_ref.dtype)

    return pl.pallas_call(
        body, name=name, grid=(T // tr,),
        in_specs=[pl.BlockSpec((tr, D), lambda i: (i, 0)), pl.BlockSpec((1, D), lambda i: (0, 0))],
        out_specs=pl.BlockSpec((tr, D), lambda i: (i, 0)),
        out_shape=jax.ShapeDtypeStruct((T, D), out_dtype), compiler_params=_params("parallel"),
    )(x, g)


def _resid_rms(x, z, g, scale, name):
    T, D = x.shape
    tr = _tile(T, ROW_TILE)

    def body(x_ref, z_ref, g_ref, o_ref):
        zv = z_ref[...]
        r = lax.rsqrt(jnp.mean(zv * zv, axis=-1, keepdims=True) + EPS)
        o_ref[...] = x_ref[...] + scale * (zv * r * g_ref[...])

    row = pl.BlockSpec((tr, D), lambda i: (i, 0))
    return pl.pallas_call(
        body, name=name, grid=(T // tr,), in_specs=[row, row, pl.BlockSpec((1, D), lambda i: (0, 0))],
        out_specs=row, out_shape=jax.ShapeDtypeStruct((T, D), F32), compiler_params=_params("parallel"),
    )(x, z, g)


def _final_loss(x, z, g, scale, target, name):
    T, D = x.shape
    tr = _tile(T, ROW_TILE)

    def body(x_ref, z_ref, g_ref, t_ref, dx_ref, acc_ref):
        @pl.when(pl.program_id(0) == 0)
        def _():
            acc_ref[...] = jnp.zeros_like(acc_ref)

        zv = z_ref[...]
        r = lax.rsqrt(jnp.mean(zv * zv, axis=-1, keepdims=True) + EPS)
        e = x_ref[...] + scale * (zv * r * g_ref[...]) - t_ref[...]
        dx_ref[...] = e * (1.0 / D)
        acc_ref[...] += _colsum(_rowsum(e * e))

    row = pl.BlockSpec((tr, D), lambda i: (i, 0))
    return pl.pallas_call(
        body, name=name, grid=(T // tr,), in_specs=[row, row, pl.BlockSpec((1, D), lambda i: (0, 0)), row],
        out_specs=[row, pl.BlockSpec((8, LANES), lambda i: (0, 0))],
        out_shape=[jax.ShapeDtypeStruct((T, D), F32), jax.ShapeDtypeStruct((8, LANES), F32)],
        compiler_params=_params("arbitrary"),
    )(x, z, g, target)


def _rms_bwd(xin, g, dy, scale, name, out_dtype, resid=None):
    T, D = xin.shape
    tr = _tile(T, ROW_TILE)
    has_resid = resid is not None

    def body(*refs):
        x_ref, g_ref, dy_ref = refs[:3]
        r_ref = refs[3] if has_resid else None
        dx_ref, dg_ref = refs[-2], refs[-1]

        @pl.when(pl.program_id(0) == 0)
        def _():
            dg_ref[...] = jnp.zeros_like(dg_ref)

        xv = x_ref[...]
        r = lax.rsqrt(jnp.mean(xv * xv, axis=-1, keepdims=True) + EPS)
        xh = xv * r
        dyv = dy_ref[...].astype(F32) * scale
        dxh = dyv * g_ref[...]
        dx = r * (dxh - xh * jnp.mean(dxh * xh, axis=-1, keepdims=True))
        if has_resid:
            dx = dx + r_ref[...]
        dx_ref[...] = dx.astype(dx_ref.dtype)
        dg_ref[...] += _colsum(dyv * xh)

    row = pl.BlockSpec((tr, D), lambda i: (i, 0))
    vec = pl.BlockSpec((1, D), lambda i: (0, 0))
    return pl.pallas_call(
        body, name=name, grid=(T // tr,), in_specs=[row, vec, row] + ([row] if has_resid else []),
        out_specs=[row, vec],
        out_shape=[jax.ShapeDtypeStruct((T, D), out_dtype), jax.ShapeDtypeStruct((1, D), F32)],
        compiler_params=_params("arbitrary"),
    )(*((xin, g, dy) + ((resid,) if has_resid else ())))


def _swiglu_fwd(u, name):
    T, F2 = u.shape
    tf = SWIGLU_TILE
    tr = _tile(T, 1024)

    def body(u_ref, o_ref):
        gate = u_ref[:, :tf].astype(F32)
        up = u_ref[:, tf:].astype(F32)
        o_ref[...] = (gate * _sig(gate) * up).astype(o_ref.dtype)

    return pl.pallas_call(
        body, name=name, grid=(T // tr, F2 // (2 * tf)),
        in_specs=[pl.BlockSpec((tr, 2 * tf), lambda i, j: (i, j))],
        out_specs=pl.BlockSpec((tr, tf), lambda i, j: (i, j)),
        out_shape=jax.ShapeDtypeStruct((T, F2 // 2), BF16), compiler_params=_params("parallel", "parallel"),
    )(u)


def _swiglu_bwd(u, da, name):
    T, F2 = u.shape
    tf = SWIGLU_TILE
    tr = _tile(T, 1024)

    def body(u_ref, da_ref, o_ref):
        gate = u_ref[:, :tf].astype(F32)
        up = u_ref[:, tf:].astype(F32)
        d = da_ref[...].astype(F32)
        s = _sig(gate)
        o_ref[:, :tf] = (d * up * (s * (1.0 + gate * (1.0 - s)))).astype(o_ref.dtype)
        o_ref[:, tf:] = (d * gate * s).astype(o_ref.dtype)

    return pl.pallas_call(
        body, name=name, grid=(T // tr, F2 // (2 * tf)),
        in_specs=[pl.BlockSpec((tr, 2 * tf), lambda i, j: (i, j)), pl.BlockSpec((tr, tf), lambda i, j: (i, j))],
        out_specs=pl.BlockSpec((tr, 2 * tf), lambda i, j: (i, j)),
        out_shape=jax.ShapeDtypeStruct((T, F2), BF16), compiler_params=_params("parallel", "parallel"),
    )(u, da)


def _hgout_fwd(o_a, pm, g, name):
    T, D = o_a.shape
    tr = _tile(T, ROW_TILE)

    def body(o_ref, ga_ref, g_ref, out_ref):
        ov = o_ref[...]
        r = lax.rsqrt(jnp.mean(ov * ov, axis=-1, keepdims=True) + EPS)
        ga = ga_ref[...].astype(F32)
        out_ref[...] = (ov * r * g_ref[...] * (ga * _sig(ga))).astype(out_ref.dtype)

    row = pl.BlockSpec((tr, D), lambda i: (i, 0))
    return pl.pallas_call(
        body, name=name, grid=(T // tr,),
        in_specs=[row, pl.BlockSpec((tr, D), lambda i: (i, 3)), pl.BlockSpec((1, D), lambda i: (0, 0))],
        out_specs=row, out_shape=jax.ShapeDtypeStruct((T, D), BF16), compiler_params=_params("parallel"),
    )(o_a, pm, g)


def _hgout_bwd(o_a, pm, g, d_out, name):
    T, D = o_a.shape
    tr = _tile(T, ROW_TILE)

    def body(o_ref, ga_ref, g_ref, d_ref, do_ref, dga_ref, dg_ref):
        @pl.when(pl.program_id(0) == 0)
        def _():
            dg_ref[...] = jnp.zeros_like(dg_ref)

        ov = o_ref[...]
        r = lax.rsqrt(jnp.mean(ov * ov, axis=-1, keepdims=True) + EPS)
        oh = ov * r
        ga = ga_ref[...].astype(F32)
        s = _sig(ga)
        d = d_ref[...].astype(F32)
        dn = d * (ga * s)
        dga_ref[...] = (d * (oh * g_ref[...]) * (s * (1.0 + ga * (1.0 - s)))).astype(dga_ref.dtype)
        dxh = dn * g_ref[...]
        do_ref[...] = (r * (dxh - oh * jnp.mean(dxh * oh, axis=-1, keepdims=True))).astype(do_ref.dtype)
        dg_ref[...] += _colsum(dn * oh)

    row = pl.BlockSpec((tr, D), lambda i: (i, 0))
    vec = pl.BlockSpec((1, D), lambda i: (0, 0))
    return pl.pallas_call(
        body, name=name, grid=(T // tr,), in_specs=[row, pl.BlockSpec((tr, D), lambda i: (i, 3)), vec, row],
        out_specs=[row, row, vec],
        out_shape=[jax.ShapeDtypeStruct((T, D), BF16), jax.ShapeDtypeStruct((T, D), BF16),
                   jax.ShapeDtypeStruct((1, D), F32)],
        compiler_params=_params("arbitrary"),
    )(o_a, pm, g, d_out)


def _merge_fwd(ya, yb, pg, bg, name):
    T, D = ya.shape
    tr = _tile(T, 256)

    def body(ya_ref, yb_ref, pg_ref, bg_ref, o_ref):
        g0 = _sig(pg_ref[:, :D] + bg_ref[:, :D])
        g1 = _sig(pg_ref[:, D:] + bg_ref[:, D:])
        o_ref[...] = (g0 * ya_ref[...] + g1 * yb_ref[...]).astype(o_ref.dtype)

    row = pl.BlockSpec((tr, D), lambda i: (i, 0))
    return pl.pallas_call(
        body, name=name, grid=(T // tr,),
        in_specs=[row, row, pl.BlockSpec((tr, 2 * D), lambda i: (i, 0)), pl.BlockSpec((1, 2 * D), lambda i: (0, 0))],
        out_specs=row, out_shape=jax.ShapeDtypeStruct((T, D), BF16), compiler_params=_params("parallel"),
    )(ya, yb, pg, bg)


def _merge_bwd(dy, ya, yb, pg, bg, name):
    T, D = ya.shape
    tr = _tile(T, 256)

    def body(dy_ref, ya_ref, yb_ref, pg_ref, bg_ref, dya_ref, dyb_ref, dpg_ref, dbg_ref):
        @pl.when(pl.program_id(0) == 0)
        def _():
            dbg_ref[...] = jnp.zeros_like(dbg_ref)

        d = dy_ref[...]
        g0 = _sig(pg_ref[:, :D] + bg_ref[:, :D])
        g1 = _sig(pg_ref[:, D:] + bg_ref[:, D:])
        dya_ref[...] = (d * g0).astype(dya_ref.dtype)
        dyb_ref[...] = (d * g1).astype(dyb_ref.dtype)
        dg0 = d * ya_ref[...] * (g0 * (1.0 - g0))
        dg1 = d * yb_ref[...] * (g1 * (1.0 - g1))
        dpg_ref[:, :D] = dg0.astype(dpg_ref.dtype)
        dpg_ref[:, D:] = dg1.astype(dpg_ref.dtype)
        dbg_ref[:, :D] += _colsum(dg0)
        dbg_ref[:, D:] += _colsum(dg1)

    row = pl.BlockSpec((tr, D), lambda i: (i, 0))
    wide = pl.BlockSpec((tr, 2 * D), lambda i: (i, 0))
    wvec = pl.BlockSpec((1, 2 * D), lambda i: (0, 0))
    return pl.pallas_call(
        body, name=name, grid=(T // tr,), in_specs=[row, row, row, wide, wvec],
        out_specs=[row, row, wide, wvec],
        out_shape=[jax.ShapeDtypeStruct((T, D), BF16), jax.ShapeDtypeStruct((T, D), BF16),
                   jax.ShapeDtypeStruct((T, 2 * D), BF16), jax.ShapeDtypeStruct((1, 2 * D), F32)],
        compiler_params=_params("arbitrary"),
    )(dy, ya, yb, pg, bg)


def _hgrn_chunk_terms(q, fl, lb, tri):
    shape = q.shape
    row = _iota(shape, 0)
    sg = _sig(fl)
    f = lb + (1.0 - lb) * sg
    k = 1.0 - f
    b = _tri_dot(tri, jnp.log(f))
    ref1 = jnp.where(row < HALF, _pick_row(b, HALF // 2), _pick_row(b, HALF + HALF // 2))
    b_half = _pick_row(b, HALF - 1)
    b_last = _pick_row(b, CHUNK - 1)
    sq = _sig(q)
    qs = q * sq
    e_q1 = jnp.exp(jnp.minimum(b - ref1, EXP_CLAMP))
    e_k1 = jnp.exp(jnp.minimum(ref1 - b, EXP_CLAMP))
    e_q2 = jnp.exp(jnp.minimum(b - b_half, 0.0))
    e_k2 = jnp.exp(jnp.minimum(b_half - b, 0.0))
    e_b = jnp.exp(b)
    e_kd = jnp.exp(b_last - b)
    return dict(sg=sg, f=f, k=k, sq=sq, qs=qs, e_q1=e_q1, e_k1=e_k1, e_q2=e_q2, e_k2=e_k2, e_b=e_b, e_kd=e_kd,
                e_last=jnp.exp(b_last))


def _hgrn_masks():
    r = _iota((CHUNK, CHUNK), 0)
    c = _iota((CHUNK, CHUNK), 1)
    causal = r >= c
    same = (r < HALF) == (c < HALF)
    return causal, causal & same, (r >= HALF) & (c < HALF)


def _softmax_lb(lbl_ref):
    l0, l1 = lbl_ref[0, 0], lbl_ref[1, 0]
    mx = jnp.maximum(l0, l1)
    e0, e1 = jnp.exp(l0 - mx), jnp.exp(l1 - mx)
    return e0 / (e0 + e1)


def _hgrn_fwd(pm, lbl, name):
    T = pm.shape[0]
    tb = _tile(T, SEQ_BLOCK)
    nc = tb // CHUNK

    def body(q_ref, f_ref, i_ref, lbl_ref, o_ref, st_ref, s_sc):
        @pl.when(pl.program_id(1) == 0)
        def _():
            s_sc[...] = jnp.zeros_like(s_sc)

        lb = _softmax_lb(lbl_ref)
        causal, m1, m2 = _hgrn_masks()
        tri = jnp.where(causal, 1.0, 0.0).astype(BF16)
        for ci in range(nc):
            sl = pl.ds(ci * CHUNK, CHUNK)
            t = _hgrn_chunk_terms(q_ref[sl, :].astype(F32), f_ref[sl, :].astype(F32), lb, tri)
            iv = i_ref[sl, :]
            a1 = _dot((t["qs"] * t["e_q1"]).astype(BF16), (t["k"] * t["e_k1"]).astype(BF16), "nt")
            a2 = _dot((t["qs"] * t["e_q2"]).astype(BF16), (t["k"] * t["e_k2"]).astype(BF16), "nt")
            a = jnp.where(m1, a1, 0.0) + jnp.where(m2, a2, 0.0)
            st = s_sc[...]
            st_ref[0, ci] = st
            o_ref[sl, :] = _dot(a.astype(BF16), iv) + _dot((t["qs"] * t["e_b"]).astype(BF16), st.astype(BF16), "nt")
            s_sc[...] = t["e_last"] * st + _dot(iv, (t["k"] * t["e_kd"]).astype(BF16), "tn")

    blk = lambda off: pl.BlockSpec((tb, DH), lambda h, b: (b, off + h))
    return pl.pallas_call(
        body, name=name, grid=(HEADS, T // tb),
        in_specs=[blk(0), blk(HEADS), blk(2 * HEADS), pl.BlockSpec((2, 1, 1, DH), lambda h, b: (0, h, 0, 0))],
        out_specs=[pl.BlockSpec((tb, DH), lambda h, b: (b, h)),
                   pl.BlockSpec((1, nc, DH, DH), lambda h, b: (h, b, 0, 0))],
        out_shape=[jax.ShapeDtypeStruct((T, HEADS * DH), F32),
                   jax.ShapeDtypeStruct((HEADS, T // CHUNK, DH, DH), F32)],
        scratch_shapes=[pltpu.VMEM((DH, DH), F32)],
        compiler_params=_params("parallel", "arbitrary"),
    )(pm, pm, pm, lbl)


def _hgrn_bwd(pm, lbl, states, do, name):
    T = pm.shape[0]
    tb = _tile(T, SEQ_BLOCK)
    nc = tb // CHUNK
    nb = T // tb

    def body(q_ref, f_ref, i_ref, lbl_ref, st_ref, do_ref, dq_ref, df_ref, di_ref, dl_ref, ds_sc, dlb_sc):
        @pl.when(pl.program_id(1) == 0)
        def _():
            ds_sc[...] = jnp.zeros_like(ds_sc)
            dlb_sc[...] = jnp.zeros_like(dlb_sc)

        lb = _softmax_lb(lbl_ref)
        causal, m1, m2 = _hgrn_masks()
        tri = jnp.where(causal, 1.0, 0.0).astype(BF16)
        tri_rev = jnp.where(_iota((CHUNK, CHUNK), 0) <= _iota((CHUNK, CHUNK), 1), 1.0, 0.0).astype(BF16)
        last_row = _iota((CHUNK, DH), 0) == CHUNK - 1
        for ci in reversed(range(nc)):
            sl = pl.ds(ci * CHUNK, CHUNK)
            q = q_ref[sl, :].astype(F32)
            t = _hgrn_chunk_terms(q, f_ref[sl, :].astype(F32), lb, tri)
            iv = i_ref[sl, :]
            dov = do_ref[sl, :]
            qe1, ke1 = t["qs"] * t["e_q1"], t["k"] * t["e_k1"]
            qe2, ke2 = t["qs"] * t["e_q2"], t["k"] * t["e_k2"]
            qi, kd = t["qs"] * t["e_b"], t["k"] * t["e_kd"]
            qe1b, ke1b, qe2b, ke2b = qe1.astype(BF16), ke1.astype(BF16), qe2.astype(BF16), ke2.astype(BF16)
            a = jnp.where(m1, _dot(qe1b, ke1b, "nt"), 0.0) + jnp.where(m2, _dot(qe2b, ke2b, "nt"), 0.0)
            st = st_ref[0, ci]
            dsn = ds_sc[...]
            dsnb = dsn.astype(BF16)
            da = _dot(dov, iv, "nt")
            da1 = jnp.where(m1, da, 0.0).astype(BF16)
            da2 = jnp.where(m2, da, 0.0).astype(BF16)
            di_ref[sl, :] = (_dot(a.astype(BF16), dov, "tn") + _dot(kd.astype(BF16), dsnb, "nt")).astype(di_ref.dtype)
            dqe1, dke1 = _dot(da1, ke1b), _dot(da1, qe1b, "tn")
            dqe2, dke2 = _dot(da2, ke2b), _dot(da2, qe2b, "tn")
            dqi = _dot(dov, st.astype(BF16))
            dkd = _dot(iv, dsnb)
            ds_sc[...] = t["e_last"] * dsn + _dot(dov, qi.astype(BF16), "tn")
            dqs = dqe1 * t["e_q1"] + dqe2 * t["e_q2"] + dqi * t["e_b"]
            dk = dke1 * t["e_k1"] + dke2 * t["e_k2"] + dkd * t["e_kd"]
            qib, kdb = qi.astype(BF16).astype(F32), kd.astype(BF16).astype(F32)
            db = (dqe1 * qe1b.astype(F32) - dke1 * ke1b.astype(F32) + dqe2 * qe2b.astype(F32)
                  - dke2 * ke2b.astype(F32) + dqi * qib - dkd * kdb)
            extra = _colsum(dkd * kdb) + t["e_last"] * _colsum(dsn * st)
            db = db + jnp.where(last_row, extra, 0.0)
            dlf = _tri_dot(tri_rev, db)
            dfv = dlf / t["f"] - dk
            sg = t["sg"]
            df_ref[sl, :] = (dfv * (1.0 - lb) * sg * (1.0 - sg)).astype(df_ref.dtype)
            dlb_sc[...] += _colsum(dfv * (1.0 - sg))
            sq = t["sq"]
            dq_ref[sl, :] = (dqs * (sq * (1.0 + q * (1.0 - sq)))).astype(dq_ref.dtype)

        @pl.when(pl.program_id(1) == nb - 1)
        def _():
            dl0 = dlb_sc[...] * lb * (1.0 - lb)
            dl_ref[0, 0] = dl0
            dl_ref[1, 0] = -dl0

    blk = lambda off: pl.BlockSpec((tb, DH), lambda h, b: (nb - 1 - b, off + h))
    lspec = pl.BlockSpec((2, 1, 1, DH), lambda h, b: (0, h, 0, 0))
    out_blk = pl.BlockSpec((tb, DH), lambda h, b: (nb - 1 - b, h))
    D = HEADS * DH
    return pl.pallas_call(
        body, name=name, grid=(HEADS, nb),
        in_specs=[blk(0), blk(HEADS), blk(2 * HEADS), lspec,
                  pl.BlockSpec((1, nc, DH, DH), lambda h, b: (h, nb - 1 - b, 0, 0)), out_blk],
        out_specs=[out_blk, out_blk, out_blk, lspec],
        out_shape=[jax.ShapeDtypeStruct((T, D), BF16)] * 3 + [jax.ShapeDtypeStruct((2, HEADS, 1, DH), F32)],
        scratch_shapes=[pltpu.VMEM((DH, DH), F32), pltpu.VMEM((1, DH), F32)],
        compiler_params=_params("parallel", "arbitrary"),
    )(pm, pm, pm, lbl, states, do)


def _log_sigmoid(x):
    return jnp.minimum(x, 0.0) - jnp.log(1.0 + jnp.exp(-jnp.abs(x)))


def _fox_cumsum(pf, bias, name):
    T = pf.shape[0]
    tb = _tile(T, SEQ_BLOCK)

    def body(x_ref, b_ref, c_ref, carry):
        @pl.when(pl.program_id(0) == 0)
        def _():
            carry[...] = jnp.zeros_like(carry)

        tri = jnp.where(_iota((tb, tb), 0) >= _iota((tb, tb), 1), 1.0, 0.0).astype(BF16)
        c = _tri_dot(tri, _log_sigmoid(x_ref[...] + b_ref[...])) + carry[...]
        c_ref[...] = c
        carry[...] = _pick_row(c, tb - 1)

    row = pl.BlockSpec((tb, LANES), lambda i: (i, 0))
    return pl.pallas_call(
        body, name=name, grid=(T // tb,), in_specs=[row, pl.BlockSpec((1, LANES), lambda i: (0, 0))],
        out_specs=row, out_shape=jax.ShapeDtypeStruct((T, LANES), F32),
        scratch_shapes=[pltpu.VMEM((1, LANES), F32)], compiler_params=_params("arbitrary"),
    )(pf, bias)


def _fox_dcum(dc, pf, bias, name):
    T = pf.shape[0]
    tb = _tile(T, SEQ_BLOCK)
    nb = T // tb

    def body(dc_ref, x_ref, b_ref, dx_ref, db_ref, carry):
        @pl.when(pl.program_id(0) == 0)
        def _():
            carry[...] = jnp.zeros_like(carry)
            db_ref[...] = jnp.zeros_like(db_ref)

        tri_rev = jnp.where(_iota((tb, tb), 0) <= _iota((tb, tb), 1), 1.0, 0.0).astype(BF16)
        dls = _tri_dot(tri_rev, dc_ref[...]) + carry[...]
        carry[...] = _pick_row(dls, 0)
        dx = dls * (1.0 - _sig(x_ref[...] + b_ref[...]))
        dx_ref[...] = dx
        db_ref[...] += _colsum(dx)

    row = pl.BlockSpec((tb, LANES), lambda i: (nb - 1 - i, 0))
    vec = pl.BlockSpec((1, LANES), lambda i: (0, 0))
    return pl.pallas_call(
        body, name=name, grid=(nb,), in_specs=[row, row, vec], out_specs=[row, vec],
        out_shape=[jax.ShapeDtypeStruct((T, LANES), F32), jax.ShapeDtypeStruct((1, LANES), F32)],
        scratch_shapes=[pltpu.VMEM((1, LANES), F32)], compiler_params=_params("arbitrary"),
    )(dc, pf, bias)


_Q_OFF, _K_OFF, _V_OFF = 4 * HEADS, 5 * HEADS, 6 * HEADS


def _fox_logits(q_ref, k_ref, cq_ref, ck_ref, masked):
    s = _dot(q_ref[...], k_ref[...], "nt") * (1.0 / math.sqrt(DH)) + (cq_ref[0] - ck_ref[0])
    if masked:
        s = jnp.where(_iota(s.shape, 0) >= _iota(s.shape, 1), s, NEG_BIG)
    return s


def _fox_fwd(pm, c_col, c_row, name):
    T = pm.shape[0]
    tq = _tile(T, ATTN_TILE)
    nq = T // tq

    def body(q_ref, k_ref, v_ref, cq_ref, ck_ref, o_ref, lse_ref, m_sc, l_sc, acc_sc):
        i, j = pl.program_id(1), pl.program_id(2)

        @pl.when(j == 0)
        def _():
            m_sc[...] = jnp.full_like(m_sc, NEG_BIG)
            l_sc[...] = jnp.zeros_like(l_sc)
            acc_sc[...] = jnp.zeros_like(acc_sc)

        def step(masked):
            s = _fox_logits(q_ref, k_ref, cq_ref, ck_ref, masked)
            m_new = jnp.maximum(m_sc[...], jnp.max(s, axis=1, keepdims=True))
            alpha = jnp.exp(m_sc[...] - m_new)
            p = jnp.exp(s - m_new)
            l_sc[...] = alpha * l_sc[...] + _rowsum(p)
            acc_sc[...] = alpha * acc_sc[...] + _dot(p.astype(BF16), v_ref[...])
            m_sc[...] = m_new

        @pl.when(j < i)
        def _():
            step(False)

        @pl.when(j == i)
        def _():
            step(True)
            o_ref[...] = (acc_sc[...] / l_sc[...]).astype(o_ref.dtype)
            lse_ref[0] = m_sc[...] + jnp.log(l_sc[...])

    kv = lambda off: pl.BlockSpec((tq, DH), lambda h, i, j: (jnp.minimum(j, i), off + h))
    return pl.pallas_call(
        body, name=name, grid=(HEADS, nq, nq),
        in_specs=[pl.BlockSpec((tq, DH), lambda h, i, j: (i, _Q_OFF + h)), kv(_K_OFF), kv(_V_OFF),
                  pl.BlockSpec((1, tq, 1), lambda h, i, j: (h, i, 0)),
                  pl.BlockSpec((1, 1, tq), lambda h, i, j: (h, 0, jnp.minimum(j, i)))],
        out_specs=[pl.BlockSpec((tq, DH), lambda h, i, j: (i, h)), pl.BlockSpec((1, tq, 1), lambda h, i, j: (h, i, 0))],
        out_shape=[jax.ShapeDtypeStruct((T, HEADS * DH), BF16), jax.ShapeDtypeStruct((HEADS, T, 1), F32)],
        scratch_shapes=[pltpu.VMEM((tq, 1), F32), pltpu.VMEM((tq, 1), F32), pltpu.VMEM((tq, DH), F32)],
        compiler_params=_params("parallel", "parallel", "arbitrary"),
    )(pm, pm, pm, c_col, c_row)


def _fox_delta(do, o, name):
    T, D = o.shape
    tr = _tile(T, ROW_TILE)

    def body(do_ref, o_ref, d_ref):
        prod = do_ref[...].astype(F32) * o_ref[...].astype(F32)
        for h in range(HEADS):
            d_ref[h] = _rowsum(prod[:, h * DH:(h + 1) * DH])

    row = pl.BlockSpec((tr, D), lambda i: (i, 0))
    return pl.pallas_call(
        body, name=name, grid=(T // tr,), in_specs=[row, row],
        out_specs=pl.BlockSpec((HEADS, tr, 1), lambda i: (0, i, 0)),
        out_shape=jax.ShapeDtypeStruct((HEADS, T, 1), F32), compiler_params=_params("parallel"),
    )(do, o)


def _fox_bwd_dq(pm, c_col, c_row, do, lse, delta, name):
    T = pm.shape[0]
    tq = _tile(T, ATTN_TILE)
    nq = T // tq
    scale = 1.0 / math.sqrt(DH)

    def body(q_ref, k_ref, v_ref, cq_ref, ck_ref, do_ref, lse_ref, dl_ref, dq_ref, dc_ref, acc_sc, dc_sc):
        i, j = pl.program_id(1), pl.program_id(2)

        @pl.when(j == 0)
        def _():
            acc_sc[...] = jnp.zeros_like(acc_sc)
            dc_sc[...] = jnp.zeros_like(dc_sc)

        def step(masked):
            s = _fox_logits(q_ref, k_ref, cq_ref, ck_ref, masked)
            p = jnp.exp(s - lse_ref[0])
            dp = _dot(do_ref[...], v_ref[...], "nt")
            ds = p * (dp - dl_ref[0])
            acc_sc[...] += _dot(ds.astype(BF16), k_ref[...])
            dc_sc[...] += _rowsum(ds)

        @pl.when(j < i)
        def _():
            step(False)

        @pl.when(j == i)
        def _():
            step(True)
            dq_ref[...] = (acc_sc[...] * scale).astype(dq_ref.dtype)
            dc_ref[0] = dc_sc[...]

    kv = lambda off: pl.BlockSpec((tq, DH), lambda h, i, j: (jnp.minimum(j, i), off + h))
    col = pl.BlockSpec((1, tq, 1), lambda h, i, j: (h, i, 0))
    return pl.pallas_call(
        body, name=name, grid=(HEADS, nq, nq),
        in_specs=[pl.BlockSpec((tq, DH), lambda h, i, j: (i, _Q_OFF + h)), kv(_K_OFF), kv(_V_OFF), col,
                  pl.BlockSpec((1, 1, tq), lambda h, i, j: (h, 0, jnp.minimum(j, i))),
                  pl.BlockSpec((tq, DH), lambda h, i, j: (i, h)), col, col],
        out_specs=[pl.BlockSpec((tq, DH), lambda h, i, j: (i, h)), col],
        out_shape=[jax.ShapeDtypeStruct((T, HEADS * DH), BF16), jax.ShapeDtypeStruct((HEADS, T, 1), F32)],
        scratch_shapes=[pltpu.VMEM((tq, DH), F32), pltpu.VMEM((tq, 1), F32)],
        compiler_params=_params("parallel", "parallel", "arbitrary"),
    )(pm, pm, pm, c_col, c_row, do, lse, delta)


def _fox_bwd_dkv(pm, c_col, c_row, do, lse, delta, name):
    T = pm.shape[0]
    tq = _tile(T, ATTN_TILE)
    nq = T // tq
    scale = 1.0 / math.sqrt(DH)

    def body(q_ref, k_ref, v_ref, cq_ref, ck_ref, do_ref, lse_ref, dl_ref, dk_ref, dv_ref, dc_ref,
             dk_sc, dv_sc, dc_sc):
        j, i = pl.program_id(1), pl.program_id(2)

        @pl.when(i == 0)
        def _():
            dk_sc[...] = jnp.zeros_like(dk_sc)
            dv_sc[...] = jnp.zeros_like(dv_sc)
            dc_sc[...] = jnp.zeros_like(dc_sc)

        def step(masked):
            s = _fox_logits(q_ref, k_ref, cq_ref, ck_ref, masked)
            p = jnp.exp(s - lse_ref[0])
            dp = _dot(do_ref[...], v_ref[...], "nt")
            ds = p * (dp - dl_ref[0])
            dv_sc[...] += _dot(p.astype(BF16), do_ref[...], "tn")
            dk_sc[...] += _dot(ds.astype(BF16), q_ref[...], "tn")
            dc_sc[...] -= _colsum(ds)

        @pl.when(i > j)
        def _():
            step(False)

        @pl.when(i == j)
        def _():
            step(True)

        @pl.when(i == nq - 1)
        def _():
            dk_ref[...] = (dk_sc[...] * scale).astype(dk_ref.dtype)
            dv_ref[...] = dv_sc[...].astype(dv_ref.dtype)
            dc_ref[0] = dc_sc[...]

    qi = lambda h, j, i: jnp.maximum(i, j)
    col = pl.BlockSpec((1, tq, 1), lambda h, j, i: (h, qi(h, j, i), 0))
    kv = lambda off: pl.BlockSpec((tq, DH), lambda h, j, i: (j, off + h))
    out = pl.BlockSpec((tq, DH), lambda h, j, i: (j, h))
    return pl.pallas_call(
        body, name=name, grid=(HEADS, nq, nq),
        in_specs=[pl.BlockSpec((tq, DH), lambda h, j, i: (qi(h, j, i), _Q_OFF + h)), kv(_K_OFF), kv(_V_OFF), col,
                  pl.BlockSpec((1, 1, tq), lambda h, j, i: (h, 0, j)),
                  pl.BlockSpec((tq, DH), lambda h, j, i: (qi(h, j, i), h)), col, col],
        out_specs=[out, out, pl.BlockSpec((1, 1, tq), lambda h, j, i: (h, 0, j))],
        out_shape=[jax.ShapeDtypeStruct((T, HEADS * DH), BF16)] * 2 + [jax.ShapeDtypeStruct((HEADS, 1, T), F32)],
        scratch_shapes=[pltpu.VMEM((tq, DH), F32), pltpu.VMEM((tq, DH), F32), pltpu.VMEM((1, tq), F32)],
        compiler_params=_params("parallel", "parallel", "arbitrary"),
    )(pm, pm, pm, c_col, c_row, do, lse, delta)


def _xattn_fwd(q, kv, name):
    T, D = q.shape
    M = kv.shape[0]
    dh = D // MEM_HEADS
    tq = _tile(T, ATTN_TILE)
    scale = 1.0 / math.sqrt(dh)

    def body(q_ref, kv_ref, o_ref):
        for h in range(MEM_HEADS):
            cs = slice(h * dh, (h + 1) * dh)
            s = _dot(q_ref[:, cs], kv_ref[:, cs], "nt") * scale
            p = jnp.exp(s - jnp.max(s, axis=1, keepdims=True))
            p = p / _rowsum(p)
            o_ref[:, cs] = _dot(p.astype(BF16), kv_ref[:, D + h * dh:D + (h + 1) * dh]).astype(o_ref.dtype)

    row = pl.BlockSpec((tq, D), lambda i: (i, 0))
    return pl.pallas_call(
        body, name=name, grid=(T // tq,), in_specs=[row, pl.BlockSpec((M, 2 * D), lambda i: (0, 0))],
        out_specs=row, out_shape=jax.ShapeDtypeStruct((T, D), BF16), compiler_params=_params("parallel"),
    )(q, kv)


def _xattn_bwd(q, kv, do, name):
    T, D = q.shape
    M = kv.shape[0]
    dh = D // MEM_HEADS
    tq = _tile(T, ATTN_TILE)
    scale = 1.0 / math.sqrt(dh)

    def body(q_ref, kv_ref, do_ref, dq_ref, dkv_ref):
        @pl.when(pl.program_id(0) == 0)
        def _():
            dkv_ref[...] = jnp.zeros_like(dkv_ref)

        for h in range(MEM_HEADS):
            cs = slice(h * dh, (h + 1) * dh)
            vs = slice(D + h * dh, D + (h + 1) * dh)
            s = _dot(q_ref[:, cs], kv_ref[:, cs], "nt") * scale
            p = jnp.exp(s - jnp.max(s, axis=1, keepdims=True))
            p = p / _rowsum(p)
            dp = _dot(do_ref[:, cs], kv_ref[:, vs], "nt")
            ds = (p * (dp - _rowsum(p * dp)) * scale).astype(BF16)
            dq_ref[:, cs] = _dot(ds, kv_ref[:, cs]).astype(dq_ref.dtype)
            dkv_ref[:, cs] += _dot(ds, q_ref[:, cs], "tn")
            dkv_ref[:, vs] += _dot(p.astype(BF16), do_ref[:, cs], "tn")

    row = pl.BlockSpec((tq, D), lambda i: (i, 0))
    full = pl.BlockSpec((M, 2 * D), lambda i: (0, 0))
    return pl.pallas_call(
        body, name=name, grid=(T // tq,), in_specs=[row, full, row], out_specs=[row, full],
        out_shape=[jax.ShapeDtypeStruct((T, D), BF16), jax.ShapeDtypeStruct((M, 2 * D), F32)],
        compiler_params=_params("arbitrary"),
    )(q, kv, do)


_HBM = pl.BlockSpec(memory_space=pltpu.HBM)


def _position():
    return lax.axis_index("x"), lax.axis_index("y"), lax.axis_index("c")


def _other_chips(x, y):
    return [(1 - x, y), (x, 1 - y), (1 - x, 1 - y)]


def _ag_chips(blk, name):
    R, C = blk.shape

    def body(x_ref, a_ref, send, recv, local):
        x, y, c = _position()
        q = 2 * x + y
        mine = pltpu.make_async_copy(x_ref, a_ref.at[c, q], local)
        mine.start()
        sends = []
        for j, (px, py) in enumerate(_other_chips(x, y)):
            cp = pltpu.make_async_remote_copy(src_ref=x_ref, dst_ref=a_ref.at[c, q], send_sem=send.at[j],
                                              recv_sem=recv.at[j], device_id=(px, py, c), device_id_type=MESH)
            cp.start()
            sends.append(cp)
        for j, (px, py) in enumerate(_other_chips(x, y)):
            pltpu.make_async_remote_copy(src_ref=x_ref, dst_ref=a_ref.at[c, 2 * px + py], send_sem=send.at[j],
                                         recv_sem=recv.at[j], device_id=(px, py, c), device_id_type=MESH).wait_recv()
        for cp in sends:
            cp.wait_send()
        mine.wait()

    return pl.pallas_call(
        body, name=name, in_specs=[_HBM], out_specs=_HBM, out_shape=jax.ShapeDtypeStruct((2, 4, R, C), blk.dtype),
        scratch_shapes=[pltpu.SemaphoreType.DMA((3,)), pltpu.SemaphoreType.DMA((3,)), pltpu.SemaphoreType.DMA],
    )(blk)


def _ag_sibling(a, name):
    def body(a_in, a_ref, send, recv):
        del a_in
        x, y, c = _position()
        cp = pltpu.make_async_remote_copy(src_ref=a_ref.at[c], dst_ref=a_ref.at[c], send_sem=send, recv_sem=recv,
                                          device_id=(x, y, 1 - c), device_id_type=MESH)
        cp.start()
        pltpu.make_async_remote_copy(src_ref=a_ref.at[c], dst_ref=a_ref.at[1 - c], send_sem=send, recv_sem=recv,
                                     device_id=(x, y, 1 - c), device_id_type=MESH).wait_recv()
        cp.wait_send()

    return pl.pallas_call(
        body, name=name, in_specs=[_HBM], out_specs=_HBM, out_shape=jax.ShapeDtypeStruct(a.shape, a.dtype),
        input_output_aliases={0: 0},
        scratch_shapes=[pltpu.SemaphoreType.DMA, pltpu.SemaphoreType.DMA],
    )(a)


def _rs_sibling(g_send, name):
    def body(g_ref, l_ref, send, recv):
        x, y, c = _position()
        cp = pltpu.make_async_remote_copy(src_ref=g_ref, dst_ref=l_ref, send_sem=send, recv_sem=recv,
                                          device_id=(x, y, 1 - c), device_id_type=MESH)
        cp.start()
        cp.wait()

    return pl.pallas_call(
        body, name=name, in_specs=[_HBM], out_specs=_HBM, out_shape=jax.ShapeDtypeStruct(g_send.shape, g_send.dtype),
        scratch_shapes=[pltpu.SemaphoreType.DMA, pltpu.SemaphoreType.DMA],
    )(g_send)


def _rs_chips(h, name):
    def body(h_ref, l_ref, send, recv, local):
        x, y, c = _position()
        q = 2 * x + y
        mine = pltpu.make_async_copy(h_ref.at[q], l_ref.at[q], local)
        mine.start()
        sends = []
        for j, (px, py) in enumerate(_other_chips(x, y)):
            cp = pltpu.make_async_remote_copy(src_ref=h_ref.at[2 * px + py], dst_ref=l_ref.at[q], send_sem=send.at[j],
                                              recv_sem=recv.at[j], device_id=(px, py, c), device_id_type=MESH)
            cp.start()
            sends.append(cp)
        for j, (px, py) in enumerate(_other_chips(x, y)):
            pltpu.make_async_remote_copy(src_ref=h_ref.at[q], dst_ref=l_ref.at[2 * px + py], send_sem=send.at[j],
                                         recv_sem=recv.at[j], device_id=(px, py, c), device_id_type=MESH).wait_recv()
        for cp in sends:
            cp.wait_send()
        mine.wait()

    return pl.pallas_call(
        body, name=name, in_specs=[_HBM], out_specs=_HBM, out_shape=jax.ShapeDtypeStruct(h.shape, h.dtype),
        scratch_shapes=[pltpu.SemaphoreType.DMA((3,)), pltpu.SemaphoreType.DMA((3,)), pltpu.SemaphoreType.DMA],
    )(h)


def _pair_add(a, b, name):
    n, R, C = a.shape
    tr = _tile(R, PACK_ROW_TILE)

    def body(a_ref, b_ref, o_ref):
        o_ref[...] = a_ref[...] + b_ref[...]

    blk = pl.BlockSpec((1, tr, C), lambda p, i: (p, i, 0))
    return pl.pallas_call(
        body, name=name, grid=(n, R // tr), in_specs=[blk, blk], out_specs=blk,
        out_shape=jax.ShapeDtypeStruct(a.shape, a.dtype), compiler_params=_params("parallel", "parallel"),
    )(a, b)


def _adamw_math(w, g, m, v):
    m = ADAM_B1 * m + (1.0 - ADAM_B1) * g
    v = ADAM_B2 * v + (1.0 - ADAM_B2) * (g * g)
    m_hat = m / (1.0 - ADAM_B1 ** ADAM_STEP)
    v_hat = v / (1.0 - ADAM_B2 ** ADAM_STEP)
    delta = -ADAM_LR * (m_hat / (jnp.sqrt(v_hat) + ADAM_EPS) + ADAM_WD * w)
    return delta, m, v


def _adamw_reduce(slots, w, m, v, name):
    n, R, C = slots.shape
    tr = _tile(R, PACK_ROW_TILE)

    def body(s_ref, w_ref, m_ref, v_ref, g_ref, d_ref, nm_ref, nv_ref):
        g = s_ref[0]
        for p in range(1, n):
            g = g + s_ref[p]
        g_ref[...] = g
        d_ref[...], nm_ref[...], nv_ref[...] = _adamw_math(w_ref[...], g, m_ref[...], v_ref[...])

    row = pl.BlockSpec((tr, C), lambda i: (i, 0))
    return pl.pallas_call(
        body, name=name, grid=(R // tr,), in_specs=[pl.BlockSpec((n, tr, C), lambda i: (0, i, 0)), row, row, row],
        out_specs=[row] * 4, out_shape=[jax.ShapeDtypeStruct((R, C), F32)] * 4, compiler_params=_params("parallel"),
    )(slots, w, m, v)


def _pad_rows(a, rows):
    return jnp.pad(a, ((0, rows - a.shape[0]), (0, 0)))


def _pack_plan(shards):
    plan, r = {}, 0
    for n in BIG:
        rows = shards[n].size // PACK_COLS
        plan[n] = (r, rows)
        r += rows
    total = -(-r // PACK_ROW_TILE) * PACK_ROW_TILE
    return plan, total


def _pack_shards(shards, total, dtype):
    flat = jnp.concatenate([shards[n].reshape(-1, PACK_COLS).astype(dtype) for n in BIG], axis=0)
    return _pad_rows(flat, total)


def _full_from_gathered(a, plan, shards):
    full = {}
    for n in BIG:
        r0, rows = plan[n]
        blk = a[:, :, r0:r0 + rows, :].transpose(1, 0, 2, 3).reshape(8, rows, PACK_COLS)
        s0, s1 = shards[n].shape
        if n in COL_SHARDED:
            full[n] = blk.reshape(8, s0, s1).transpose(1, 0, 2).reshape(s0, 8 * s1)
        else:
            full[n] = blk.reshape(8 * s0, s1)
    return full


def _blocks_from_full(g, n, shards):
    s0, s1 = shards[n].shape
    if n in COL_SHARDED:
        blk = g.reshape(s0, 8, s1).transpose(1, 0, 2)
    else:
        blk = g.reshape(8, s0, s1)
    return blk.reshape(4, 2, -1, PACK_COLS)


def _swiglu_interleave(w):
    d, f2 = w.shape
    return w.reshape(d, 2, f2 // (2 * SWIGLU_TILE), SWIGLU_TILE).transpose(0, 2, 1, 3).reshape(d, f2)


def _swiglu_deinterleave(w):
    d, f2 = w.shape
    return w.reshape(d, f2 // (2 * SWIGLU_TILE), 2, SWIGLU_TILE).transpose(0, 2, 1, 3).reshape(d, f2)


SMALL_ROWS = 16


def _pack_small(vals, loss_row):
    rows = []
    for n in SMALL:
        flat = vals[n].reshape(-1)
        pad = (-flat.shape[0]) % PACK_COLS
        rows.append(jnp.pad(flat, (0, pad)).reshape(-1, PACK_COLS))
    rows.append(loss_row)
    out = jnp.concatenate(rows, axis=0)
    assert out.shape[0] == SMALL_ROWS, out.shape
    return out


def _unpack_small(packed, like):
    out, r = {}, 0
    for n in SMALL:
        size = like[n].size
        rows = -(-size // PACK_COLS)
        out[n] = packed[r:r + rows].reshape(-1)[:size].reshape(like[n].shape)
        r += rows
    return out


def _ffn_fwd(x, g_pre, w_in, w_down, tag):
    h = _rms_fwd(x, g_pre, f"{tag}_pre")
    u = _mm(h, w_in, "nn", BF16, f"{tag}_up")
    a = _swiglu_fwd(u, f"{tag}_act")
    z = _mm(a, w_down, "nn", F32, f"{tag}_down", tk=1408)
    return h, u, a, z


def _ffn_bwd(saved, x, g_pre, w_in, w_down, g_post, dx_out, tag):
    h, u, a, z = saved
    dz, dg_post = _rms_bwd(z, g_post, dx_out, 0.5, f"{tag}_post_bwd", BF16)
    da = _mm(dz, w_down, "nt", BF16, f"{tag}_down_dx", tn=1408)
    dw_down = _mm(a, dz, "tn", F32, f"{tag}_down_dw", tm=1408, tn=1024)
    du = _swiglu_bwd(u, da, f"{tag}_act_bwd")
    dh = _mm(du, w_in, "nt", F32, f"{tag}_up_dx", tn=1024, tk=512)
    dw_in = _mm(h, du, "tn", F32, f"{tag}_up_dw")
    dx, dg_pre = _rms_bwd(x, g_pre, dh, 1.0, f"{tag}_pre_bwd", F32, resid=dx_out)
    return dx, dg_pre, dg_post, dw_in, dw_down


def _step_local(x, mem, target, W, S):
    T, D = x.shape
    gW, gS = {}, {}

    f1 = _ffn_fwd(x, S["ffn1_pre_g"], W["ffn1_w_in"], W["ffn1_w_down"], "ffn1")
    x1 = _resid_rms(x, f1[3], S["ffn1_post_g"], 0.5, "ffn1_post")

    h2 = _rms_fwd(x1, S["mix_pre_g"], "mix_pre")
    pm = _mm(h2, W["w_main"], "nn", BF16, "mix_proj_main")
    pf = _mm(h2, W["w_f"], "nn", F32, "mix_proj_f")
    pg = _mm(h2, W["w_gates"], "nn", F32, "mix_proj_gates")
    lbl = S["hg_lb_logits"].reshape(2, HEADS, 1, DH)
    o_a, states = _hgrn_fwd(pm, lbl, "hgrn_fwd")
    oan = _hgout_fwd(o_a, pm, S["hg_norm_g"], "hgrn_out")
    bias = jnp.pad(S["fox_f_bias"], ((0, 0), (0, LANES - HEADS)))
    c = _fox_cumsum(pf, bias, "fox_cumsum")
    c_heads = c[:, :HEADS].T
    c_col, c_row = c_heads[:, :, None], c_heads[:, None, :]
    o_b, lse = _fox_fwd(pm, c_col, c_row, "fox_fwd")
    ya = _mm(oan, W["w_branch_a"], "nn", F32, "branch_a")
    yb = _mm(o_b, W["w_branch_b"], "nn", F32, "branch_b")
    y = _merge_fwd(ya, yb, pg, S["b_gate"], "merge")
    z2 = _mm(y, W["w_out"], "nn", F32, "mix_out")
    x2 = _resid_rms(x1, z2, S["mix_post_g"], 1.0, "mix_post")

    h3 = _rms_fwd(x2, S["mem_pre_g"], "mem_pre")
    memn = _rms_fwd(mem, S["mem_kv_g"], "mem_kv_norm")
    qm = _mm(h3, W["w_mq"], "nn", BF16, "mem_q")
    kv = _mm(memn, W["w_mkv"], "nn", BF16, "mem_kv")
    om = _xattn_fwd(qm, kv, "mem_attn")
    z3 = _mm(om, W["w_mo"], "nn", F32, "mem_o")
    x3 = _resid_rms(x2, z3, S["mem_post_g"], 1.0, "mem_post")

    f2 = _ffn_fwd(x3, S["ffn2_pre_g"], W["ffn2_w_in"], W["ffn2_w_down"], "ffn2")
    dx4, sq = _final_loss(x3, f2[3], S["ffn2_post_g"], 0.5, target, "loss")

    dx3, gS["ffn2_pre_g"], gS["ffn2_post_g"], gW["ffn2_w_in"], gW["ffn2_w_down"] = _ffn_bwd(
        f2, x3, S["ffn2_pre_g"], W["ffn2_w_in"], W["ffn2_w_down"], S["ffn2_post_g"], dx4, "ffn2")

    dz3, gS["mem_post_g"] = _rms_bwd(z3, S["mem_post_g"], dx3, 1.0, "mem_post_bwd", BF16)
    dom = _mm(dz3, W["w_mo"], "nt", BF16, "mem_o_dx")
    gW["w_mo"] = _mm(om, dz3, "tn", F32, "mem_o_dw")
    dqm, dkv = _xattn_bwd(qm, kv, dom, "mem_attn_bwd")
    dh3 = _mm(dqm, W["w_mq"], "nt", F32, "mem_q_dx")
    gW["w_mq"] = _mm(h3, dqm, "tn", F32, "mem_q_dw")
    dkvb = dkv.astype(BF16)
    gW["w_mkv"] = _mm(memn, dkvb, "tn", F32, "mem_kv_dw")
    dmemn = _mm(dkvb, W["w_mkv"], "nt", F32, "mem_kv_dx")
    _, gS["mem_kv_g"] = _rms_bwd(mem, S["mem_kv_g"], dmemn, 1.0, "mem_kv_norm_bwd", BF16)
    dx2, gS["mem_pre_g"] = _rms_bwd(x2, S["mem_pre_g"], dh3, 1.0, "mem_pre_bwd", F32, resid=dx3)

    dz2, gS["mix_post_g"] = _rms_bwd(z2, S["mix_post_g"], dx2, 1.0, "mix_post_bwd", BF16)
    dy = _mm(dz2, W["w_out"], "nt", F32, "mix_out_dx")
    gW["w_out"] = _mm(y, dz2, "tn", F32, "mix_out_dw")
    dya, dyb, dpg, gS["b_gate"] = _merge_bwd(dy, ya, yb, pg, S["b_gate"], "merge_bwd")
    doan = _mm(dya, W["w_branch_a"], "nt", F32, "branch_a_dx")
    gW["w_branch_a"] = _mm(oan, dya, "tn", F32, "branch_a_dw")
    dob = _mm(dyb, W["w_branch_b"], "nt", BF16, "branch_b_dx")
    gW["w_branch_b"] = _mm(o_b, dyb, "tn", F32, "branch_b_dw")

    delta = _fox_delta(dob, o_b, "fox_delta")
    dq_b, dc_col = _fox_bwd_dq(pm, c_col, c_row, dob, lse, delta, "fox_bwd_dq")
    dk_b, dv_b, dc_row = _fox_bwd_dkv(pm, c_col, c_row, dob, lse, delta, "fox_bwd_dkv")
    dc = jnp.pad((dc_col.reshape(HEADS, T) + dc_row.reshape(HEADS, T)).T, ((0, 0), (0, LANES - HEADS)))
    dpf, dbias = _fox_dcum(dc, pf, bias, "fox_cumsum_bwd")
    gS["fox_f_bias"] = dbias[:, :HEADS]

    do_a, dg_a, gS["hg_norm_g"] = _hgout_bwd(o_a, pm, S["hg_norm_g"], doan, "hgrn_out_bwd")
    dq_a, df_a, di_a, dlbl = _hgrn_bwd(pm, lbl, states, do_a, "hgrn_bwd")
    gS["hg_lb_logits"] = dlbl.reshape(2, HEADS, DH)

    dpm = jnp.concatenate([dq_a, df_a, di_a, dg_a, dq_b, dk_b, dv_b], axis=1)
    dpf16 = dpf.astype(BF16)
    dh2 = _mm(dpm, W["w_main"], "nt", F32, "mix_proj_main_dx", tn=1024, tk=512)
    dh2 = _mm(dpg, W["w_gates"], "nt", F32, "mix_proj_gates_dx", add=dh2, tn=1024, tk=512)
    dh2 = _mm(dpf16, W["w_f"], "nt", F32, "mix_proj_f_dx", add=dh2, tn=1024)
    gW["w_main"] = _mm(h2, dpm, "tn", F32, "mix_proj_main_dw")
    gW["w_gates"] = _mm(h2, dpg, "tn", F32, "mix_proj_gates_dw")
    gW["w_f"] = _mm(h2, dpf16, "tn", F32, "mix_proj_f_dw")
    dx1, gS["mix_pre_g"] = _rms_bwd(x1, S["mix_pre_g"], dh2, 1.0, "mix_pre_bwd", F32, resid=dx2)

    dx0, gS["ffn1_pre_g"], gS["ffn1_post_g"], gW["ffn1_w_in"], gW["ffn1_w_down"] = _ffn_bwd(
        f1, x, S["ffn1_pre_g"], W["ffn1_w_in"], W["ffn1_w_down"], S["ffn1_post_g"], dx1, "ffn1")
    return sq, dx0, gW, gS


def _train_step(a):
    c_idx = lax.axis_index("c")
    x, mem, target = a["x"][0], a["mem"][0], a["loss_target"][0]
    D = x.shape[1]
    shards = {n: a[n][0] for n in BIG}
    plan, total = _pack_plan(shards)

    gathered = _ag_sibling(_ag_chips(_pack_shards(shards, total, BF16), "ag_chips"), "ag_sibling")
    W = _full_from_gathered(gathered, plan, shards)
    n_main = 7 * D
    w_in = W.pop("w_in")
    W["w_main"] = w_in[:, :n_main]
    W["w_f"] = jnp.pad(w_in[:, n_main:n_main + HEADS], ((0, 0), (0, LANES - HEADS)))
    W["w_gates"] = w_in[:, n_main + HEADS:]
    W["ffn1_w_in"] = _swiglu_interleave(W["ffn1_w_in"])
    W["ffn2_w_in"] = _swiglu_interleave(W["ffn2_w_in"])
    S = {n: a[n] for n in SMALL}

    sq, grad_x, gW, gS = _step_local(x, mem, target, W, S)

    gW["w_in"] = jnp.concatenate([gW.pop("w_main"), gW.pop("w_f")[:, :HEADS], gW.pop("w_gates")], axis=1)
    gW["ffn1_w_in"] = _swiglu_deinterleave(gW["ffn1_w_in"])
    gW["ffn2_w_in"] = _swiglu_deinterleave(gW["ffn2_w_in"])
    blocks = jnp.concatenate([_blocks_from_full(gW[n], n, shards) for n in BIG], axis=2)
    blocks = jnp.pad(blocks, ((0, 0), (0, 0), (0, total - blocks.shape[2]), (0, 0)))
    keep = lax.dynamic_index_in_dim(blocks, c_idx, axis=1, keepdims=False)
    send = lax.dynamic_index_in_dim(blocks, 1 - c_idx, axis=1, keepdims=False)
    pair = _pair_add(keep, _rs_sibling(send, "rs_sibling"), "rs_pair_add")
    slots = _rs_chips(pair, "rs_chips")
    g_big, d_big, m_big, v_big = _adamw_reduce(
        slots, _pack_shards(shards, total, F32), _pack_shards({n: a["m_" + n][0] for n in BIG}, total, F32),
        _pack_shards({n: a["v_" + n][0] for n in BIG}, total, F32), "adamw_big")

    loss_row = jnp.pad(sq[:1, :1] * (0.5 / D), ((0, 0), (0, PACK_COLS - 1)))
    small_all = _ag_sibling(_ag_chips(_pack_small(gS, loss_row), "small_ag_chips"), "small_ag_sibling")
    small_slots = small_all.transpose(1, 0, 2, 3).reshape(8, SMALL_ROWS, PACK_COLS)
    zero_row = jnp.zeros((1, PACK_COLS), F32)
    g_sm, d_sm, m_sm, v_sm = _adamw_reduce(
        small_slots, _pack_small({n: a[n] for n in SMALL}, zero_row),
        _pack_small({n: a["m_" + n] for n in SMALL}, zero_row),
        _pack_small({n: a["v_" + n] for n in SMALL}, zero_row), "adamw_small")

    def unpack(big, small):
        out = _unpack_small(small, {n: a[n] for n in SMALL})
        for n in BIG:
            r0, rows = plan[n]
            out[n] = big[r0:r0 + rows].reshape(a[n].shape)
        return [out[n] for n in WEIGHTS]

    loss = g_sm[SMALL_ROWS - 1, 0]
    return (loss, grad_x[None], *unpack(g_big, g_sm), *unpack(d_big, d_sm), *unpack(m_big, m_sm),
            *unpack(v_big, v_sm))


def kernel(x, mem, ffn1_pre_g, ffn1_w_in, ffn1_w_down, ffn1_post_g, mix_pre_g, w_in, hg_lb_logits, hg_norm_g, fox_f_bias, w_branch_a, w_branch_b, b_gate, w_out, mix_post_g, mem_pre_g, mem_kv_g, w_mq, w_mkv, w_mo, mem_post_g, ffn2_pre_g, ffn2_w_in, ffn2_w_down, ffn2_post_g, loss_target, m_ffn1_pre_g, m_ffn1_w_in, m_ffn1_w_down, m_ffn1_post_g, m_mix_pre_g, m_w_in, m_hg_lb_logits, m_hg_norm_g, m_fox_f_bias, m_w_branch_a, m_w_branch_b, m_b_gate, m_w_out, m_mix_post_g, m_mem_pre_g, m_mem_kv_g, m_w_mq, m_w_mkv, m_w_mo, m_mem_post_g, m_ffn2_pre_g, m_ffn2_w_in, m_ffn2_w_down, m_ffn2_post_g, v_ffn1_pre_g, v_ffn1_w_in, v_ffn1_w_down, v_ffn1_post_g, v_mix_pre_g, v_w_in, v_hg_lb_logits, v_hg_norm_g, v_fox_f_bias, v_w_branch_a, v_w_branch_b, v_b_gate, v_w_out, v_mix_post_g, v_mem_pre_g, v_mem_kv_g, v_w_mq, v_w_mkv, v_w_mo, v_mem_post_g, v_ffn2_pre_g, v_ffn2_w_in, v_ffn2_w_down, v_ffn2_post_g):
    return _train_step(dict(locals()))
```

```python
import functools
import math

import jax
import jax.numpy as jnp
from jax import lax
from jax.experimental import pallas as pl
from jax.experimental.pallas import tpu as pltpu

F32 = jnp.float32
BF16 = jnp.bfloat16
MESH = pl.DeviceIdType.MESH

EPS = 1e-6
HEADS = 8
DH = 128
MEM_HEADS = 4
CHUNK = 128
HALF = CHUNK // 2
SWIGLU_TILE = 256
LANES = 128
PACK_COLS = 1024
PACK_ROW_TILE = 256
ROW_TILE = 512
SEQ_BLOCK = 512
ATTN_TILE = 512
ATTN_ROWS = 128
EXP_CLAMP = 80.0
NEG_BIG = -1e30

ADAM_LR, ADAM_B1, ADAM_B2, ADAM_EPS, ADAM_WD, ADAM_STEP = 0.001, 0.9, 0.999, 1e-08, 0.01, 10

VMEM_LIMIT = 48 * 1024 * 1024

_DN = {
    "nn": (((1,), (0,)), ((), ())),
    "nt": (((1,), (1,)), ((), ())),
    "tn": (((0,), (0,)), ((), ())),
}

BIG = ["ffn1_w_in", "ffn1_w_down", "w_in", "w_branch_a", "w_branch_b", "w_out", "w_mq", "w_mkv", "w_mo",
       "ffn2_w_in", "ffn2_w_down"]
COL_SHARDED = {"ffn1_w_in", "w_in", "w_mkv", "ffn2_w_in"}
SMALL = ["ffn1_pre_g", "ffn1_post_g", "mix_pre_g", "hg_lb_logits", "hg_norm_g", "fox_f_bias", "b_gate",
         "mix_post_g", "mem_pre_g", "mem_kv_g", "mem_post_g", "ffn2_pre_g", "ffn2_post_g"]
WEIGHTS = ["ffn1_pre_g", "ffn1_w_in", "ffn1_w_down", "ffn1_post_g", "mix_pre_g", "w_in", "hg_lb_logits",
           "hg_norm_g", "fox_f_bias", "w_branch_a", "w_branch_b", "b_gate", "w_out", "mix_post_g", "mem_pre_g",
           "mem_kv_g", "w_mq", "w_mkv", "w_mo", "mem_post_g", "ffn2_pre_g", "ffn2_w_in", "ffn2_w_down",
           "ffn2_post_g"]


def _dot(a, b, mode="nn"):
    return lax.dot_general(a, b, _DN[mode], preferred_element_type=F32)


def _sig(x):
    return 1.0 / (1.0 + jnp.exp(-x))


def _params(*dims):
    return pltpu.CompilerParams(dimension_semantics=dims if dims else None, vmem_limit_bytes=VMEM_LIMIT)


def _tile(dim, pref):
    if dim <= pref:
        return dim
    t = (pref // LANES) * LANES
    while t >= LANES:
        if dim % t == 0:
            return t
        t -= LANES
    raise ValueError(f"no tile for {dim}")


def _colsum(x):
    return jnp.sum(x, axis=0, keepdims=True)


def _rowsum(x):
    return jnp.sum(x, axis=1, keepdims=True)


def _iota(shape, axis):
    return lax.broadcasted_iota(jnp.int32, shape, axis)


def _pick_row(x, r):
    return _colsum(jnp.where(_iota(x.shape, 0) == r, x, 0.0))


def _tri_dot(tri, x):
    hi = x.astype(BF16)
    r1 = x - hi.astype(F32)
    mid = r1.astype(BF16)
    lo = (r1 - mid.astype(F32)).astype(BF16)
    return _dot(tri, hi) + _dot(tri, mid) + _dot(tri, lo)


def _mm(a, b, mode, out_dtype, name, add=None, tm=1024, tn=512, tk=1024):
    if mode == "nn":
        (M, K), (K2, N) = a.shape, b.shape
    elif mode == "nt":
        (M, K), (N, K2) = a.shape, b.shape
    else:
        (K, M), (K2, N) = a.shape, b.shape
    assert K == K2, (name, a.shape, b.shape)
    tm, tn, tk = _tile(M, tm), _tile(N, tn), _tile(K, tk)
    nk = K // tk
    if mode == "tn":
        a_spec = pl.BlockSpec((tk, tm), lambda i, j, k: (k, i))
    else:
        a_spec = pl.BlockSpec((tm, tk), lambda i, j, k: (i, k))
    if mode == "nt":
        b_spec = pl.BlockSpec((tn, tk), lambda i, j, k: (j, k))
    else:
        b_spec = pl.BlockSpec((tk, tn), lambda i, j, k: (k, j))
    o_spec = pl.BlockSpec((tm, tn), lambda i, j, k: (i, j))
    has_add = add is not None

    def body(*refs):
        a_ref, b_ref = refs[0], refs[1]
        c_ref = refs[2] if has_add else None
        o_ref = refs[3] if has_add else refs[2]
        part = _dot(a_ref[...], b_ref[...], mode)
        if nk == 1:
            if has_add:
                part = part + c_ref[...]
            o_ref[...] = part.astype(o_ref.dtype)
            return
        acc_ref = refs[-1]
        k = pl.program_id(2)

        @pl.when(k == 0)
        def _():
            acc_ref[...] = part + c_ref[...] if has_add else part

        @pl.when(k > 0)
        def _():
            acc_ref[...] += part

        @pl.when(k == nk - 1)
        def _():
            o_ref[...] = acc_ref[...].astype(o_ref.dtype)

    in_specs = [a_spec, b_spec] + ([o_spec] if has_add else [])
    args = (a, b) + ((add,) if has_add else ())
    return pl.pallas_call(
        body, name=name, grid=(M // tm, N // tn, nk), in_specs=in_specs, out_specs=o_spec,
        out_shape=jax.ShapeDtypeStruct((M, N), out_dtype),
        scratch_shapes=[pltpu.VMEM((tm, tn), F32)] if nk > 1 else [],
        compiler_params=_params("parallel", "parallel", "arbitrary"),
    )(*args)


def _rms_fwd(x, g, name, out_dtype=BF16):
    T, D = x.shape
    tr = _tile(T, ROW_TILE)

    def body(x_ref, g_ref, o_ref):
        xv = x_ref[...]
        r = lax.rsqrt(jnp.mean(xv * xv, axis=-1, keepdims=True) + EPS)
        o_ref[...] = (xv * r * g_ref[...]).astype(o_ref.dtype)

    return pl.pallas_call(
        body, name=name, grid=(T // tr,),
        in_specs=[pl.BlockSpec((tr, D), lambda i: (i, 0)), pl.BlockSpec((1, D), lambda i: (0, 0))],
        out_specs=pl.BlockSpec((tr, D), lambda i: (i, 0)),
        out_shape=jax.ShapeDtypeStruct((T, D), out_dtype), compiler_params=_params("parallel"),
    )(x, g)


def _resid_rms(x, z, g, scale, name):
    T, D = x.shape
    tr = _tile(T, ROW_TILE)

    def body(x_ref, z_ref, g_ref, o_ref):
        zv = z_ref[...]
        r = lax.rsqrt(jnp.mean(zv * zv, axis=-1, keepdims=True) + EPS)
        o_ref[...] = x_ref[...] + scale * (zv * r * g_ref[...])

    row = pl.BlockSpec((tr, D), lambda i: (i, 0))
    return pl.pallas_call(
        body, name=name, grid=(T // tr,), in_specs=[row, row, pl.BlockSpec((1, D), lambda i: (0, 0))],
        out_specs=row, out_shape=jax.ShapeDtypeStruct((T, D), F32), compiler_params=_params("parallel"),
    )(x, z, g)


def _final_loss(x, z, g, scale, target, name):
    T, D = x.shape
    tr = _tile(T, ROW_TILE)

    def body(x_ref, z_ref, g_ref, t_ref, dx_ref, acc_ref):
        @pl.when(pl.program_id(0) == 0)
        def _():
            acc_ref[...] = jnp.zeros_like(acc_ref)

        zv = z_ref[...]
        r = lax.rsqrt(jnp.mean(zv * zv, axis=-1, keepdims=True) + EPS)
        e = x_ref[...] + scale * (zv * r * g_ref[...]) - t_ref[...]
        dx_ref[...] = e * (1.0 / D)
        acc_ref[...] += _colsum(_rowsum(e * e))

    row = pl.BlockSpec((tr, D), lambda i: (i, 0))
    return pl.pallas_call(
        body, name=name, grid=(T // tr,), in_specs=[row, row, pl.BlockSpec((1, D), lambda i: (0, 0)), row],
        out_specs=[row, pl.BlockSpec((8, LANES), lambda i: (0, 0))],
        out_shape=[jax.ShapeDtypeStruct((T, D), F32), jax.ShapeDtypeStruct((8, LANES), F32)],
        compiler_params=_params("arbitrary"),
    )(x, z, g, target)


def _rms_bwd(xin, g, dy, scale, name, out_dtype, resid=None):
    T, D = xin.shape
    tr = _tile(T, ROW_TILE)
    has_resid = resid is not None

    def body(*refs):
        x_ref, g_ref, dy_ref = refs[:3]
        r_ref = refs[3] if has_resid else None
        dx_ref, dg_ref = refs[-2], refs[-1]

        @pl.when(pl.program_id(0) == 0)
        def _():
            dg_ref[...] = jnp.zeros_like(dg_ref)

        xv = x_ref[...]
        r = lax.rsqrt(jnp.mean(xv * xv, axis=-1, keepdims=True) + EPS)
        xh = xv * r
        dyv = dy_ref[...].astype(F32) * scale
        dxh = dyv * g_ref[...]
        dx = r * (dxh - xh * jnp.mean(dxh * xh, axis=-1, keepdims=True))
        if has_resid:
            dx = dx + r_ref[...]
        dx_ref[...] = dx.astype(dx_ref.dtype)
        dg_ref[...] += _colsum(dyv * xh)

    row = pl.BlockSpec((tr, D), lambda i: (i, 0))
    vec = pl.BlockSpec((1, D), lambda i: (0, 0))
    return pl.pallas_call(
        body, name=name, grid=(T // tr,), in_specs=[row, vec, row] + ([row] if has_resid else []),
        out_specs=[row, vec],
        out_shape=[jax.ShapeDtypeStruct((T, D), out_dtype), jax.ShapeDtypeStruct((1, D), F32)],
        compiler_params=_params("arbitrary"),
    )(*((xin, g, dy) + ((resid,) if has_resid else ())))


def _swiglu_fwd(u, name):
    T, F2 = u.shape
    tf = SWIGLU_TILE
    tr = _tile(T, 1024)

    def body(u_ref, o_ref):
        gate = u_ref[:, :tf].astype(F32)
        up = u_ref[:, tf:].astype(F32)
        o_ref[...] = (gate * _sig(gate) * up).astype(o_ref.dtype)

    return pl.pallas_call(
        body, name=name, grid=(T // tr, F2 // (2 * tf)),
        in_specs=[pl.BlockSpec((tr, 2 * tf), lambda i, j: (i, j))],
        out_specs=pl.BlockSpec((tr, tf), lambda i, j: (i, j)),
        out_shape=jax.ShapeDtypeStruct((T, F2 // 2), BF16), compiler_params=_params("parallel", "parallel"),
    )(u)


def _swiglu_bwd(u, da, name):
    T, F2 = u.shape
    tf = SWIGLU_TILE
    tr = _tile(T, 1024)

    def body(u_ref, da_ref, o_ref):
        gate = u_ref[:, :tf].astype(F32)
        up = u_ref[:, tf:].astype(F32)
        d = da_ref[...].astype(F32)
        s = _sig(gate)
        o_ref[:, :tf] = (d * up * (s * (1.0 + gate * (1.0 - s)))).astype(o_ref.dtype)
        o_ref[:, tf:] = (d * gate * s).astype(o_ref.dtype)

    return pl.pallas_call(
        body, name=name, grid=(T // tr, F2 // (2 * tf)),
        in_specs=[pl.BlockSpec((tr, 2 * tf), lambda i, j: (i, j)), pl.BlockSpec((tr, tf), lambda i, j: (i, j))],
        out_specs=pl.BlockSpec((tr, 2 * tf), lambda i, j: (i, j)),
        out_shape=jax.ShapeDtypeStruct((T, F2), BF16), compiler_params=_params("parallel", "parallel"),
    )(u, da)


def _hgout_fwd(o_a, pm, g, name):
    T, D = o_a.shape
    tr = _tile(T, ROW_TILE)

    def body(o_ref, ga_ref, g_ref, out_ref):
        ov = o_ref[...]
        r = lax.rsqrt(jnp.mean(ov * ov, axis=-1, keepdims=True) + EPS)
        ga = ga_ref[...].astype(F32)
        out_ref[...] = (ov * r * g_ref[...] * (ga * _sig(ga))).astype(out_ref.dtype)

    row = pl.BlockSpec((tr, D), lambda i: (i, 0))
    return pl.pallas_call(
        body, name=name, grid=(T // tr,),
        in_specs=[row, pl.BlockSpec((tr, D), lambda i: (i, 3)), pl.BlockSpec((1, D), lambda i: (0, 0))],
        out_specs=row, out_shape=jax.ShapeDtypeStruct((T, D), BF16), compiler_params=_params("parallel"),
    )(o_a, pm, g)


def _hgout_bwd(o_a, pm, g, d_out, name):
    T, D = o_a.shape
    tr = _tile(T, ROW_TILE)

    def body(o_ref, ga_ref, g_ref, d_ref, do_ref, dga_ref, dg_ref):
        @pl.when(pl.program_id(0) == 0)
        def _():
            dg_ref[...] = jnp.zeros_like(dg_ref)

        ov = o_ref[...]
        r = lax.rsqrt(jnp.mean(ov * ov, axis=-1, keepdims=True) + EPS)
        oh = ov * r
        ga = ga_ref[...].astype(F32)
        s = _sig(ga)
        d = d_ref[...].astype(F32)
        dn = d * (ga * s)
        dga_ref[...] = (d * (oh * g_ref[...]) * (s * (1.0 + ga * (1.0 - s)))).astype(dga_ref.dtype)
        dxh = dn * g_ref[...]
        do_ref[...] = (r * (dxh - oh * jnp.mean(dxh * oh, axis=-1, keepdims=True))).astype(do_ref.dtype)
        dg_ref[...] += _colsum(dn * oh)

    row = pl.BlockSpec((tr, D), lambda i: (i, 0))
    vec = pl.BlockSpec((1, D), lambda i: (0, 0))
    return pl.pallas_call(
        body, name=name, grid=(T // tr,), in_specs=[row, pl.BlockSpec((tr, D), lambda i: (i, 3)), vec, row],
        out_specs=[row, row, vec],
        out_shape=[jax.ShapeDtypeStruct((T, D), BF16), jax.ShapeDtypeStruct((T, D), BF16),
                   jax.ShapeDtypeStruct((1, D), F32)],
        compiler_params=_params("arbitrary"),
    )(o_a, pm, g, d_out)


def _merge_fwd(ya, yb, pg, bg, name):
    T, D = ya.shape
    tr = _tile(T, 256)

    def body(ya_ref, yb_ref, pg_ref, bg_ref, o_ref):
        g0 = _sig(pg_ref[:, :D] + bg_ref[:, :D])
        g1 = _sig(pg_ref[:, D:] + bg_ref[:, D:])
        o_ref[...] = (g0 * ya_ref[...] + g1 * yb_ref[...]).astype(o_ref.dtype)

    row = pl.BlockSpec((tr, D), lambda i: (i, 0))
    return pl.pallas_call(
        body, name=name, grid=(T // tr,),
        in_specs=[row, row, pl.BlockSpec((tr, 2 * D), lambda i: (i, 0)), pl.BlockSpec((1, 2 * D), lambda i: (0, 0))],
        out_specs=row, out_shape=jax.ShapeDtypeStruct((T, D), BF16), compiler_params=_params("parallel"),
    )(ya, yb, pg, bg)


def _merge_bwd(dy, ya, yb, pg, bg, name):
    T, D = ya.shape
    tr = _tile(T, 256)

    def body(dy_ref, ya_ref, yb_ref, pg_ref, bg_ref, dya_ref, dyb_ref, dpg_ref, dbg_ref):
        @pl.when(pl.program_id(0) == 0)
        def _():
            dbg_ref[...] = jnp.zeros_like(dbg_ref)

        d = dy_ref[...]
        g0 = _sig(pg_ref[:, :D] + bg_ref[:, :D])
        g1 = _sig(pg_ref[:, D:] + bg_ref[:, D:])
        dya_ref[...] = (d * g0).astype(dya_ref.dtype)
        dyb_ref[...] = (d * g1).astype(dyb_ref.dtype)
        dg0 = d * ya_ref[...] * (g0 * (1.0 - g0))
        dg1 = d * yb_ref[...] * (g1 * (1.0 - g1))
        dpg_ref[:, :D] = dg0.astype(dpg_ref.dtype)
        dpg_ref[:, D:] = dg1.astype(dpg_ref.dtype)
        dbg_ref[:, :D] += _colsum(dg0)
        dbg_ref[:, D:] += _colsum(dg1)

    row = pl.BlockSpec((tr, D), lambda i: (i, 0))
    wide = pl.BlockSpec((tr, 2 * D), lambda i: (i, 0))
    wvec = pl.BlockSpec((1, 2 * D), lambda i: (0, 0))
    return pl.pallas_call(
        body, name=name, grid=(T // tr,), in_specs=[row, row, row, wide, wvec],
        out_specs=[row, row, wide, wvec],
        out_shape=[jax.ShapeDtypeStruct((T, D), BF16), jax.ShapeDtypeStruct((T, D), BF16),
                   jax.ShapeDtypeStruct((T, 2 * D), BF16), jax.ShapeDtypeStruct((1, 2 * D), F32)],
        compiler_params=_params("arbitrary"),
    )(dy, ya, yb, pg, bg)


def _hgrn_chunk_terms(q, fl, lb, tri):
    shape = q.shape
    row = _iota(shape, 0)
    sg = _sig(fl)
    f = lb + (1.0 - lb) * sg
    k = 1.0 - f
    b = _tri_dot(tri, jnp.log(f))
    ref1 = jnp.where(row < HALF, _pick_row(b, HALF // 2), _pick_row(b, HALF + HALF // 2))
    b_half = _pick_row(b, HALF - 1)
    b_last = _pick_row(b, CHUNK - 1)
    sq = _sig(q)
    qs = q * sq
    e_q1 = jnp.exp(jnp.minimum(b - ref1, EXP_CLAMP))
    e_k1 = jnp.exp(jnp.minimum(ref1 - b, EXP_CLAMP))
    e_q2 = jnp.exp(jnp.minimum(b - b_half, 0.0))
    e_k2 = jnp.exp(jnp.minimum(b_half - b, 0.0))
    e_b = jnp.exp(b)
    e_kd = jnp.exp(b_last - b)
    return dict(sg=sg, f=f, k=k, sq=sq, qs=qs, e_q1=e_q1, e_k1=e_k1, e_q2=e_q2, e_k2=e_k2, e_b=e_b, e_kd=e_kd,
                e_last=jnp.exp(b_last))


def _hgrn_masks():
    r = _iota((CHUNK, CHUNK), 0)
    c = _iota((CHUNK, CHUNK), 1)
    causal = r >= c
    same = (r < HALF) == (c < HALF)
    return causal, causal & same, (r >= HALF) & (c < HALF)


def _softmax_lb(lbl_ref):
    l0, l1 = lbl_ref[0, 0], lbl_ref[1, 0]
    mx = jnp.maximum(l0, l1)
    e0, e1 = jnp.exp(l0 - mx), jnp.exp(l1 - mx)
    return e0 / (e0 + e1)


def _hgrn_fwd(pm, lbl, name):
    T = pm.shape[0]
    tb = _tile(T, SEQ_BLOCK)
    nc = tb // CHUNK

    def body(q_ref, f_ref, i_ref, lbl_ref, o_ref, st_ref, s_sc):
        @pl.when(pl.program_id(1) == 0)
        def _():
            s_sc[...] = jnp.zeros_like(s_sc)

        lb = _softmax_lb(lbl_ref)
        causal, m1, m2 = _hgrn_masks()
        tri = jnp.where(causal, 1.0, 0.0).astype(BF16)
        for ci in range(nc):
            sl = pl.ds(ci * CHUNK, CHUNK)
            t = _hgrn_chunk_terms(q_ref[sl, :].astype(F32), f_ref[sl, :].astype(F32), lb, tri)
            iv = i_ref[sl, :]
            a1 = _dot((t["qs"] * t["e_q1"]).astype(BF16), (t["k"] * t["e_k1"]).astype(BF16), "nt")
            a2 = _dot((t["qs"] * t["e_q2"]).astype(BF16), (t["k"] * t["e_k2"]).astype(BF16), "nt")
            a = jnp.where(m1, a1, 0.0) + jnp.where(m2, a2, 0.0)
            st = s_sc[...]
            st_ref[0, ci] = st
            o_ref[sl, :] = _dot(a.astype(BF16), iv) + _dot((t["qs"] * t["e_b"]).astype(BF16), st.astype(BF16), "nt")
            s_sc[...] = t["e_last"] * st + _dot(iv, (t["k"] * t["e_kd"]).astype(BF16), "tn")

    blk = lambda off: pl.BlockSpec((tb, DH), lambda h, b: (b, off + h))
    return pl.pallas_call(
        body, name=name, grid=(HEADS, T // tb),
        in_specs=[blk(0), blk(HEADS), blk(2 * HEADS), pl.BlockSpec((2, 1, 1, DH), lambda h, b: (0, h, 0, 0))],
        out_specs=[pl.BlockSpec((tb, DH), lambda h, b: (b, h)),
                   pl.BlockSpec((1, nc, DH, DH), lambda h, b: (h, b, 0, 0))],
        out_shape=[jax.ShapeDtypeStruct((T, HEADS * DH), F32),
                   jax.ShapeDtypeStruct((HEADS, T // CHUNK, DH, DH), F32)],
        scratch_shapes=[pltpu.VMEM((DH, DH), F32)],
        compiler_params=_params("parallel", "arbitrary"),
    )(pm, pm, pm, lbl)


def _hgrn_bwd(pm, lbl, states, do, name):
    T = pm.shape[0]
    tb = _tile(T, SEQ_BLOCK)
    nc = tb // CHUNK
    nb = T // tb

    def body(q_ref, f_ref, i_ref, lbl_ref, st_ref, do_ref, dq_ref, df_ref, di_ref, dl_ref, ds_sc, dlb_sc):
        @pl.when(pl.program_id(1) == 0)
        def _():
            ds_sc[...] = jnp.zeros_like(ds_sc)
            dlb_sc[...] = jnp.zeros_like(dlb_sc)

        lb = _softmax_lb(lbl_ref)
        causal, m1, m2 = _hgrn_masks()
        tri = jnp.where(causal, 1.0, 0.0).astype(BF16)
        tri_rev = jnp.where(_iota((CHUNK, CHUNK), 0) <= _iota((CHUNK, CHUNK), 1), 1.0, 0.0).astype(BF16)
        last_row = _iota((CHUNK, DH), 0) == CHUNK - 1
        for ci in reversed(range(nc)):
            sl = pl.ds(ci * CHUNK, CHUNK)
            q = q_ref[sl, :].astype(F32)
            t = _hgrn_chunk_terms(q, f_ref[sl, :].astype(F32), lb, tri)
            iv = i_ref[sl, :]
            dov = do_ref[sl, :]
            qe1, ke1 = t["qs"] * t["e_q1"], t["k"] * t["e_k1"]
            qe2, ke2 = t["qs"] * t["e_q2"], t["k"] * t["e_k2"]
            qi, kd = t["qs"] * t["e_b"], t["k"] * t["e_kd"]
            qe1b, ke1b, qe2b, ke2b = qe1.astype(BF16), ke1.astype(BF16), qe2.astype(BF16), ke2.astype(BF16)
            a = jnp.where(m1, _dot(qe1b, ke1b, "nt"), 0.0) + jnp.where(m2, _dot(qe2b, ke2b, "nt"), 0.0)
            st = st_ref[0, ci]
            dsn = ds_sc[...]
            dsnb = dsn.astype(BF16)
            da = _dot(dov, iv, "nt")
            da1 = jnp.where(m1, da, 0.0).astype(BF16)
            da2 = jnp.where(m2, da, 0.0).astype(BF16)
            di_ref[sl, :] = (_dot(a.astype(BF16), dov, "tn") + _dot(kd.astype(BF16), dsnb, "nt")).astype(di_ref.dtype)
            dqe1, dke1 = _dot(da1, ke1b), _dot(da1, qe1b, "tn")
            dqe2, dke2 = _dot(da2, ke2b), _dot(da2, qe2b, "tn")
            dqi = _dot(dov, st.astype(BF16))
            dkd = _dot(iv, dsnb)
            ds_sc[...] = t["e_last"] * dsn + _dot(dov, qi.astype(BF16), "tn")
            dqs = dqe1 * t["e_q1"] + dqe2 * t["e_q2"] + dqi * t["e_b"]
            dk = dke1 * t["e_k1"] + dke2 * t["e_k2"] + dkd * t["e_kd"]
            qib, kdb = qi.astype(BF16).astype(F32), kd.astype(BF16).astype(F32)
            db = (dqe1 * qe1b.astype(F32) - dke1 * ke1b.astype(F32) + dqe2 * qe2b.astype(F32)
                  - dke2 * ke2b.astype(F32) + dqi * qib - dkd * kdb)
            extra = _colsum(dkd * kdb) + t["e_last"] * _colsum(dsn * st)
            db = db + jnp.where(last_row, extra, 0.0)
            dlf = _tri_dot(tri_rev, db)
            dfv = dlf / t["f"] - dk
            sg = t["sg"]
            df_ref[sl, :] = (dfv * (1.0 - lb) * sg * (1.0 - sg)).astype(df_ref.dtype)
            dlb_sc[...] += _colsum(dfv * (1.0 - sg))
            sq = t["sq"]
            dq_ref[sl, :] = (dqs * (sq * (1.0 + q * (1.0 - sq)))).astype(dq_ref.dtype)

        @pl.when(pl.program_id(1) == nb - 1)
        def _():
            dl0 = dlb_sc[...] * lb * (1.0 - lb)
            dl_ref[0, 0] = dl0
            dl_ref[1, 0] = -dl0

    blk = lambda off: pl.BlockSpec((tb, DH), lambda h, b: (nb - 1 - b, off + h))
    lspec = pl.BlockSpec((2, 1, 1, DH), lambda h, b: (0, h, 0, 0))
    out_blk = pl.BlockSpec((tb, DH), lambda h, b: (nb - 1 - b, h))
    D = HEADS * DH
    return pl.pallas_call(
        body, name=name, grid=(HEADS, nb),
        in_specs=[blk(0), blk(HEADS), blk(2 * HEADS), lspec,
                  pl.BlockSpec((1, nc, DH, DH), lambda h, b: (h, nb - 1 - b, 0, 0)), out_blk],
        out_specs=[out_blk, out_blk, out_blk, lspec],
        out_shape=[jax.ShapeDtypeStruct((T, D), BF16)] * 3 + [jax.ShapeDtypeStruct((2, HEADS, 1, DH), F32)],
        scratch_shapes=[pltpu.VMEM((DH, DH), F32), pltpu.VMEM((1, DH), F32)],
        compiler_params=_params("parallel", "arbitrary"),
    )(pm, pm, pm, lbl, states, do)


def _log_sigmoid(x):
    return jnp.minimum(x, 0.0) - jnp.log(1.0 + jnp.exp(-jnp.abs(x)))


def _fox_cumsum(pf, bias, name):
    T = pf.shape[0]
    tb = _tile(T, SEQ_BLOCK)

    def body(x_ref, b_ref, c_ref, carry):
        @pl.when(pl.program_id(0) == 0)
        def _():
            carry[...] = jnp.zeros_like(carry)

        tri = jnp.where(_iota((tb, tb), 0) >= _iota((tb, tb), 1), 1.0, 0.0).astype(BF16)
        c = _tri_dot(tri, _log_sigmoid(x_ref[...] + b_ref[...])) + carry[...]
        c_ref[...] = c
        carry[...] = _pick_row(c, tb - 1)

    row = pl.BlockSpec((tb, LANES), lambda i: (i, 0))
    return pl.pallas_call(
        body, name=name, grid=(T // tb,), in_specs=[row, pl.BlockSpec((1, LANES), lambda i: (0, 0))],
        out_specs=row, out_shape=jax.ShapeDtypeStruct((T, LANES), F32),
        scratch_shapes=[pltpu.VMEM((1, LANES), F32)], compiler_params=_params("arbitrary"),
    )(pf, bias)


def _fox_dcum(dc, pf, bias, name):
    T = pf.shape[0]
    tb = _tile(T, SEQ_BLOCK)
    nb = T // tb

    def body(dc_ref, x_ref, b_ref, dx_ref, db_ref, carry):
        @pl.when(pl.program_id(0) == 0)
        def _():
            carry[...] = jnp.zeros_like(carry)
            db_ref[...] = jnp.zeros_like(db_ref)

        tri_rev = jnp.where(_iota((tb, tb), 0) <= _iota((tb, tb), 1), 1.0, 0.0).astype(BF16)
        dls = _tri_dot(tri_rev, dc_ref[...]) + carry[...]
        carry[...] = _pick_row(dls, 0)
        dx = dls * (1.0 - _sig(x_ref[...] + b_ref[...]))
        dx_ref[...] = dx
        db_ref[...] += _colsum(dx)

    row = pl.BlockSpec((tb, LANES), lambda i: (nb - 1 - i, 0))
    vec = pl.BlockSpec((1, LANES), lambda i: (0, 0))
    return pl.pallas_call(
        body, name=name, grid=(nb,), in_specs=[row, row, vec], out_specs=[row, vec],
        out_shape=[jax.ShapeDtypeStruct((T, LANES), F32), jax.ShapeDtypeStruct((1, LANES), F32)],
        scratch_shapes=[pltpu.VMEM((1, LANES), F32)], compiler_params=_params("arbitrary"),
    )(dc, pf, bias)


_Q_OFF, _K_OFF, _V_OFF = 4 * HEADS, 5 * HEADS, 6 * HEADS


def _causal_pairs(nq, by_key):
    if by_key:
        pairs = [(i, j) for j in range(nq) for i in range(j, nq)]
    else:
        pairs = [(i, j) for i in range(nq) for j in range(i + 1)]
    return jnp.asarray([p[0] for p in pairs], jnp.int32), jnp.asarray([p[1] for p in pairs], jnp.int32)


def _fox_logits(q, k, cq, ck, row0, masked):
    s = _dot(q, k, "nt") + (cq - ck)
    if masked:
        s = jnp.where(_iota(s.shape, 0) + row0 >= _iota(s.shape, 1), s, NEG_BIG)
    return s


def _fox_fwd(pm, c_col, c_row, name):
    T = pm.shape[0]
    tq = _tile(T, ATTN_TILE)
    nq = T // tq
    rg = min(ATTN_ROWS, tq)
    qi_tab, kj_tab = _causal_pairs(nq, by_key=False)

    def body(qi_ref, kj_ref, q_ref, k_ref, v_ref, cq_ref, ck_ref, o_ref, lse_ref, m_sc, l_sc, acc_sc):
        t = pl.program_id(1)
        i, j = qi_ref[t], kj_ref[t]

        @pl.when(j == 0)
        def _():
            m_sc[...] = jnp.full_like(m_sc, NEG_BIG)
            l_sc[...] = jnp.zeros_like(l_sc)
            acc_sc[...] = jnp.zeros_like(acc_sc)

        def step(diag):
            m_all, l_all, acc_all = m_sc[...], l_sc[...], acc_sc[...]
            ms, ls, accs = [], [], []
            for r in range(tq // rg):
                rows = slice(r * rg, (r + 1) * rg)
                w = (r + 1) * rg if diag else tq
                s = _fox_logits(q_ref[rows, :], k_ref[:w, :], cq_ref[0, rows, :], ck_ref[0, :, :w], r * rg, diag)
                m_old = m_all[rows, :]
                m_new = jnp.maximum(m_old, jnp.max(s, axis=1, keepdims=True))
                alpha = jnp.exp(m_old - m_new)
                p = jnp.exp(s - m_new)
                ms.append(m_new)
                ls.append(alpha * l_all[rows, :] + _rowsum(p))
                accs.append(alpha * acc_all[rows, :] + _dot(p.astype(BF16), v_ref[:w, :]))
            m_sc[...] = jnp.concatenate(ms, axis=0)
            l_sc[...] = jnp.concatenate(ls, axis=0)
            acc_sc[...] = jnp.concatenate(accs, axis=0)

        @pl.when(j < i)
        def _():
            step(False)

        @pl.when(j == i)
        def _():
            step(True)
            o_ref[...] = (acc_sc[...] / l_sc[...]).astype(o_ref.dtype)
            lse_ref[0] = m_sc[...] + jnp.log(l_sc[...])

    kv = lambda off: pl.BlockSpec((tq, DH), lambda h, t, qi, kj: (kj[t], off + h))
    col = pl.BlockSpec((1, tq, 1), lambda h, t, qi, kj: (h, qi[t], 0))
    grid_spec = pltpu.PrefetchScalarGridSpec(
        num_scalar_prefetch=2, grid=(HEADS, qi_tab.shape[0]),
        in_specs=[pl.BlockSpec((tq, DH), lambda h, t, qi, kj: (qi[t], _Q_OFF + h)), kv(_K_OFF), kv(_V_OFF), col,
                  pl.BlockSpec((1, 1, tq), lambda h, t, qi, kj: (h, 0, kj[t]))],
        out_specs=[pl.BlockSpec((tq, DH), lambda h, t, qi, kj: (qi[t], h)), col],
        scratch_shapes=[pltpu.VMEM((tq, 1), F32), pltpu.VMEM((tq, 1), F32), pltpu.VMEM((tq, DH), F32)])
    return pl.pallas_call(
        body, name=name, grid_spec=grid_spec,
        out_shape=[jax.ShapeDtypeStruct((T, HEADS * DH), BF16), jax.ShapeDtypeStruct((HEADS, T, 1), F32)],
        compiler_params=_params("parallel", "arbitrary"),
    )(qi_tab, kj_tab, pm, pm, pm, c_col, c_row)


def _fox_delta(do, o, name):
    T, D = o.shape
    tr = _tile(T, ROW_TILE)

    def body(do_ref, o_ref, d_ref):
        prod = do_ref[...].astype(F32) * o_ref[...].astype(F32)
        for h in range(HEADS):
            d_ref[h] = _rowsum(prod[:, h * DH:(h + 1) * DH])

    row = pl.BlockSpec((tr, D), lambda i: (i, 0))
    return pl.pallas_call(
        body, name=name, grid=(T // tr,), in_specs=[row, row],
        out_specs=pl.BlockSpec((HEADS, tr, 1), lambda i: (0, i, 0)),
        out_shape=jax.ShapeDtypeStruct((HEADS, T, 1), F32), compiler_params=_params("parallel"),
    )(do, o)


def _fox_bwd(pm, c_col, c_row, do, lse, delta, name):
    T = pm.shape[0]
    tq = _tile(T, ATTN_TILE)
    nq = T // tq
    rg = min(ATTN_ROWS, tq)
    qi_tab, kj_tab = _causal_pairs(nq, by_key=True)
    npairs = qi_tab.shape[0]

    def body(qi_ref, kj_ref, q_ref, k_ref, v_ref, cq_ref, ck_ref, do_ref, lse_ref, dl_ref,
             dq_ref, dk_ref, dv_ref, dcq_ref, dck_ref, dq_sc, dk_sc, dv_sc, dck_sc):
        t = pl.program_id(1)
        i, j = qi_ref[t], kj_ref[t]

        @pl.when(t == 0)
        def _():
            dq_sc[...] = jnp.zeros_like(dq_sc)
            dcq_ref[...] = jnp.zeros_like(dcq_ref)

        @pl.when(i == j)
        def _():
            dk_sc[...] = jnp.zeros_like(dk_sc)
            dv_sc[...] = jnp.zeros_like(dv_sc)
            dck_sc[...] = jnp.zeros_like(dck_sc)

        base = pl.multiple_of(i * tq, tq)

        def step(diag):
            for r in range(tq // rg):
                rows = slice(r * rg, (r + 1) * rg)
                w = (r + 1) * rg if diag else tq
                qr, dor = q_ref[rows, :], do_ref[rows, :]
                s = _fox_logits(qr, k_ref[:w, :], cq_ref[0, rows, :], ck_ref[0, :, :w], r * rg, diag)
                p = jnp.exp(s - lse_ref[0, rows, :])
                dp = _dot(dor, v_ref[:w, :], "nt")
                ds = p * (dp - dl_ref[0, rows, :])
                dsb = ds.astype(BF16)
                dv_sc[:w, :] += _dot(p.astype(BF16), dor, "tn")
                dk_sc[:w, :] += _dot(dsb, qr, "tn")
                dck_sc[:, :w] -= _colsum(ds)
                tgt = pl.ds(base + r * rg, rg)
                dq_sc[tgt, :] += _dot(dsb, k_ref[:w, :])
                dcq_ref[0, tgt, :] += _rowsum(ds)

        @pl.when(i > j)
        def _():
            step(False)

        @pl.when(i == j)
        def _():
            step(True)

        @pl.when(i == nq - 1)
        def _():
            dk_ref[...] = dk_sc[...].astype(dk_ref.dtype)
            dv_ref[...] = dv_sc[...].astype(dv_ref.dtype)
            dck_ref[0] = dck_sc[...]

        @pl.when(t == npairs - 1)
        def _():
            dq_ref[...] = dq_sc[...].astype(dq_ref.dtype)

    col = pl.BlockSpec((1, tq, 1), lambda h, t, qi, kj: (h, qi[t], 0))
    kv = lambda off: pl.BlockSpec((tq, DH), lambda h, t, qi, kj: (kj[t], off + h))
    kv_out = pl.BlockSpec((tq, DH), lambda h, t, qi, kj: (kj[t], h))
    key_row = pl.BlockSpec((1, 1, tq), lambda h, t, qi, kj: (h, 0, kj[t]))
    grid_spec = pltpu.PrefetchScalarGridSpec(
        num_scalar_prefetch=2, grid=(HEADS, npairs),
        in_specs=[pl.BlockSpec((tq, DH), lambda h, t, qi, kj: (qi[t], _Q_OFF + h)), kv(_K_OFF), kv(_V_OFF), col,
                  key_row, pl.BlockSpec((tq, DH), lambda h, t, qi, kj: (qi[t], h)), col, col],
        out_specs=[pl.BlockSpec((T, DH), lambda h, t, qi, kj: (0, h)), kv_out, kv_out,
                   pl.BlockSpec((1, T, 1), lambda h, t, qi, kj: (h, 0, 0)), key_row],
        scratch_shapes=[pltpu.VMEM((T, DH), F32), pltpu.VMEM((tq, DH), F32), pltpu.VMEM((tq, DH), F32),
                        pltpu.VMEM((1, tq), F32)])
    D = HEADS * DH
    return pl.pallas_call(
        body, name=name, grid_spec=grid_spec,
        out_shape=[jax.ShapeDtypeStruct((T, D), BF16)] * 3
        + [jax.ShapeDtypeStruct((HEADS, T, 1), F32), jax.ShapeDtypeStruct((HEADS, 1, T), F32)],
        compiler_params=_params("parallel", "arbitrary"),
    )(qi_tab, kj_tab, pm, pm, pm, c_col, c_row, do, lse, delta)


def _xattn_fwd(q, kv, name):
    T, D = q.shape
    M = kv.shape[0]
    dh = D // MEM_HEADS
    tq = _tile(T, ATTN_TILE)
    scale = 1.0 / math.sqrt(dh)

    def body(q_ref, kv_ref, o_ref):
        for h in range(MEM_HEADS):
            cs = slice(h * dh, (h + 1) * dh)
            s = _dot(q_ref[:, cs], kv_ref[:, cs], "nt") * scale
            p = jnp.exp(s - jnp.max(s, axis=1, keepdims=True))
            p = p / _rowsum(p)
            o_ref[:, cs] = _dot(p.astype(BF16), kv_ref[:, D + h * dh:D + (h + 1) * dh]).astype(o_ref.dtype)

    row = pl.BlockSpec((tq, D), lambda i: (i, 0))
    return pl.pallas_call(
        body, name=name, grid=(T // tq,), in_specs=[row, pl.BlockSpec((M, 2 * D), lambda i: (0, 0))],
        out_specs=row, out_shape=jax.ShapeDtypeStruct((T, D), BF16), compiler_params=_params("parallel"),
    )(q, kv)


def _xattn_bwd(q, kv, do, name):
    T, D = q.shape
    M = kv.shape[0]
    dh = D // MEM_HEADS
    tq = _tile(T, ATTN_TILE)
    scale = 1.0 / math.sqrt(dh)

    def body(q_ref, kv_ref, do_ref, dq_ref, dkv_ref):
        @pl.when(pl.program_id(0) == 0)
        def _():
            dkv_ref[...] = jnp.zeros_like(dkv_ref)

        for h in range(MEM_HEADS):
            cs = slice(h * dh, (h + 1) * dh)
            vs = slice(D + h * dh, D + (h + 1) * dh)
            s = _dot(q_ref[:, cs], kv_ref[:, cs], "nt") * scale
            p = jnp.exp(s - jnp.max(s, axis=1, keepdims=True))
            p = p / _rowsum(p)
            dp = _dot(do_ref[:, cs], kv_ref[:, vs], "nt")
            ds = (p * (dp - _rowsum(p * dp)) * scale).astype(BF16)
            dq_ref[:, cs] = _dot(ds, kv_ref[:, cs]).astype(dq_ref.dtype)
            dkv_ref[:, cs] += _dot(ds, q_ref[:, cs], "tn")
            dkv_ref[:, vs] += _dot(p.astype(BF16), do_ref[:, cs], "tn")

    row = pl.BlockSpec((tq, D), lambda i: (i, 0))
    full = pl.BlockSpec((M, 2 * D), lambda i: (0, 0))
    return pl.pallas_call(
        body, name=name, grid=(T // tq,), in_specs=[row, full, row], out_specs=[row, full],
        out_shape=[jax.ShapeDtypeStruct((T, D), BF16), jax.ShapeDtypeStruct((M, 2 * D), F32)],
        compiler_params=_params("arbitrary"),
    )(q, kv, do)


_HBM = pl.BlockSpec(memory_space=pltpu.HBM)


def _position():
    return lax.axis_index("x"), lax.axis_index("y"), lax.axis_index("c")


def _other_chips(x, y):
    return [(1 - x, y), (x, 1 - y), (1 - x, 1 - y)]


def _ag_chips(blk, name):
    R, C = blk.shape

    def body(x_ref, a_ref, send, recv, local):
        x, y, c = _position()
        q = 2 * x + y
        mine = pltpu.make_async_copy(x_ref, a_ref.at[c, q], local)
        mine.start()
        sends = []
        for j, (px, py) in enumerate(_other_chips(x, y)):
            cp = pltpu.make_async_remote_copy(src_ref=x_ref, dst_ref=a_ref.at[c, q], send_sem=send.at[j],
                                              recv_sem=recv.at[j], device_id=(px, py, c), device_id_type=MESH)
            cp.start()
            sends.append(cp)
        for j, (px, py) in enumerate(_other_chips(x, y)):
            pltpu.make_async_remote_copy(src_ref=x_ref, dst_ref=a_ref.at[c, 2 * px + py], send_sem=send.at[j],
                                         recv_sem=recv.at[j], device_id=(px, py, c), device_id_type=MESH).wait_recv()
        for cp in sends:
            cp.wait_send()
        mine.wait()

    return pl.pallas_call(
        body, name=name, in_specs=[_HBM], out_specs=_HBM, out_shape=jax.ShapeDtypeStruct((2, 4, R, C), blk.dtype),
        scratch_shapes=[pltpu.SemaphoreType.DMA((3,)), pltpu.SemaphoreType.DMA((3,)), pltpu.SemaphoreType.DMA],
    )(blk)


def _ag_sibling(a, name):
    def body(a_in, a_ref, send, recv):
        del a_in
        x, y, c = _position()
        cp = pltpu.make_async_remote_copy(src_ref=a_ref.at[c], dst_ref=a_ref.at[c], send_sem=send, recv_sem=recv,
                                          device_id=(x, y, 1 - c), device_id_type=MESH)
        cp.start()
        pltpu.make_async_remote_copy(src_ref=a_ref.at[c], dst_ref=a_ref.at[1 - c], send_sem=send, recv_sem=recv,
                                     device_id=(x, y, 1 - c), device_id_type=MESH).wait_recv()
        cp.wait_send()

    return pl.pallas_call(
        body, name=name, in_specs=[_HBM], out_specs=_HBM, out_shape=jax.ShapeDtypeStruct(a.shape, a.dtype),
        input_output_aliases={0: 0},
        scratch_shapes=[pltpu.SemaphoreType.DMA, pltpu.SemaphoreType.DMA],
    )(a)


def _rs_sibling(g_send, name):
    def body(g_ref, l_ref, send, recv):
        x, y, c = _position()
        cp = pltpu.make_async_remote_copy(src_ref=g_ref, dst_ref=l_ref, send_sem=send, recv_sem=recv,
                                          device_id=(x, y, 1 - c), device_id_type=MESH)
        cp.start()
        cp.wait()

    return pl.pallas_call(
        body, name=name, in_specs=[_HBM], out_specs=_HBM, out_shape=jax.ShapeDtypeStruct(g_send.shape, g_send.dtype),
        scratch_shapes=[pltpu.SemaphoreType.DMA, pltpu.SemaphoreType.DMA],
    )(g_send)


def _rs_chips(h, name):
    def body(h_ref, l_ref, send, recv, local):
        x, y, c = _position()
        q = 2 * x + y
        mine = pltpu.make_async_copy(h_ref.at[q], l_ref.at[q], local)
        mine.start()
        sends = []
        for j, (px, py) in enumerate(_other_chips(x, y)):
            cp = pltpu.make_async_remote_copy(src_ref=h_ref.at[2 * px + py], dst_ref=l_ref.at[q], send_sem=send.at[j],
                                              recv_sem=recv.at[j], device_id=(px, py, c), device_id_type=MESH)
            cp.start()
            sends.append(cp)
        for j, (px, py) in enumerate(_other_chips(x, y)):
            pltpu.make_async_remote_copy(src_ref=h_ref.at[q], dst_ref=l_ref.at[2 * px + py], send_sem=send.at[j],
                                         recv_sem=recv.at[j], device_id=(px, py, c), device_id_type=MESH).wait_recv()
        for cp in sends:
            cp.wait_send()
        mine.wait()

    return pl.pallas_call(
        body, name=name, in_specs=[_HBM], out_specs=_HBM, out_shape=jax.ShapeDtypeStruct(h.shape, h.dtype),
        scratch_shapes=[pltpu.SemaphoreType.DMA((3,)), pltpu.SemaphoreType.DMA((3,)), pltpu.SemaphoreType.DMA],
    )(h)


def _pair_add(a, b, out_dtype, name):
    n, R, C = a.shape
    tr = _tile(R, PACK_ROW_TILE)

    def body(a_ref, b_ref, o_ref):
        o_ref[...] = (a_ref[...] + b_ref[...]).astype(o_ref.dtype)

    blk = pl.BlockSpec((1, tr, C), lambda p, i: (p, i, 0))
    return pl.pallas_call(
        body, name=name, grid=(n, R // tr), in_specs=[blk, blk], out_specs=blk,
        out_shape=jax.ShapeDtypeStruct(a.shape, out_dtype), compiler_params=_params("parallel", "parallel"),
    )(a, b)


def _adamw_math(w, g, m, v):
    m = ADAM_B1 * m + (1.0 - ADAM_B1) * g
    v = ADAM_B2 * v + (1.0 - ADAM_B2) * (g * g)
    m_hat = m / (1.0 - ADAM_B1 ** ADAM_STEP)
    v_hat = v / (1.0 - ADAM_B2 ** ADAM_STEP)
    delta = -ADAM_LR * (m_hat / (jnp.sqrt(v_hat) + ADAM_EPS) + ADAM_WD * w)
    return delta, m, v


def _adamw_reduce(slots, w, m, v, name):
    n, R, C = slots.shape
    tr = _tile(R, PACK_ROW_TILE)

    def body(s_ref, w_ref, m_ref, v_ref, g_ref, d_ref, nm_ref, nv_ref):
        g = s_ref[0].astype(F32)
        for p in range(1, n):
            g = g + s_ref[p].astype(F32)
        g_ref[...] = g
        d_ref[...], nm_ref[...], nv_ref[...] = _adamw_math(w_ref[...], g, m_ref[...], v_ref[...])

    row = pl.BlockSpec((tr, C), lambda i: (i, 0))
    return pl.pallas_call(
        body, name=name, grid=(R // tr,), in_specs=[pl.BlockSpec((n, tr, C), lambda i: (0, i, 0)), row, row, row],
        out_specs=[row] * 4, out_shape=[jax.ShapeDtypeStruct((R, C), F32)] * 4, compiler_params=_params("parallel"),
    )(slots, w, m, v)


def _pad_rows(a, rows):
    return jnp.pad(a, ((0, rows - a.shape[0]), (0, 0)))


def _pack_plan(shards):
    plan, r = {}, 0
    for n in BIG:
        rows = shards[n].size // PACK_COLS
        plan[n] = (r, rows)
        r += rows
    total = -(-r // PACK_ROW_TILE) * PACK_ROW_TILE
    return plan, total


def _pack_shards(shards, total, dtype):
    flat = jnp.concatenate([shards[n].reshape(-1, PACK_COLS).astype(dtype) for n in BIG], axis=0)
    return _pad_rows(flat, total)


def _full_from_gathered(a, plan, shards):
    full = {}
    for n in BIG:
        r0, rows = plan[n]
        blk = a[:, :, r0:r0 + rows, :].transpose(1, 0, 2, 3).reshape(8, rows, PACK_COLS)
        s0, s1 = shards[n].shape
        if n in COL_SHARDED:
            full[n] = blk.reshape(8, s0, s1).transpose(1, 0, 2).reshape(s0, 8 * s1)
        else:
            full[n] = blk.reshape(8 * s0, s1)
    return full


def _blocks_from_full(g, n, shards):
    s0, s1 = shards[n].shape
    if n in COL_SHARDED:
        blk = g.reshape(s0, 8, s1).transpose(1, 0, 2)
    else:
        blk = g.reshape(8, s0, s1)
    return blk.reshape(4, 2, -1, PACK_COLS)


def _swiglu_interleave(w):
    d, f2 = w.shape
    return w.reshape(d, 2, f2 // (2 * SWIGLU_TILE), SWIGLU_TILE).transpose(0, 2, 1, 3).reshape(d, f2)


def _swiglu_deinterleave(w):
    d, f2 = w.shape
    return w.reshape(d, f2 // (2 * SWIGLU_TILE), 2, SWIGLU_TILE).transpose(0, 2, 1, 3).reshape(d, f2)


SMALL_ROWS = 16


def _pack_small(vals, loss_row):
    rows = []
    for n in SMALL:
        flat = vals[n].reshape(-1)
        pad = (-flat.shape[0]) % PACK_COLS
        rows.append(jnp.pad(flat, (0, pad)).reshape(-1, PACK_COLS))
    rows.append(loss_row)
    out = jnp.concatenate(rows, axis=0)
    assert out.shape[0] == SMALL_ROWS, out.shape
    return out


def _unpack_small(packed, like):
    out, r = {}, 0
    for n in SMALL:
        size = like[n].size
        rows = -(-size // PACK_COLS)
        out[n] = packed[r:r + rows].reshape(-1)[:size].reshape(like[n].shape)
        r += rows
    return out


def _ffn_fwd(x, g_pre, w_in, w_down, tag):
    h = _rms_fwd(x, g_pre, f"{tag}_pre")
    u = _mm(h, w_in, "nn", BF16, f"{tag}_up")
    a = _swiglu_fwd(u, f"{tag}_act")
    z = _mm(a, w_down, "nn", F32, f"{tag}_down", tk=1408)
    return h, u, a, z


def _ffn_bwd(saved, x, g_pre, w_in, w_down, g_post, dx_out, tag):
    h, u, a, z = saved
    dz, dg_post = _rms_bwd(z, g_post, dx_out, 0.5, f"{tag}_post_bwd", BF16)
    da = _mm(dz, w_down, "nt", BF16, f"{tag}_down_dx", tn=1408)
    dw_down = _mm(a, dz, "tn", F32, f"{tag}_down_dw", tm=1408, tn=1024)
    du = _swiglu_bwd(u, da, f"{tag}_act_bwd")
    dh = _mm(du, w_in, "nt", F32, f"{tag}_up_dx", tn=1024, tk=512)
    dw_in = _mm(h, du, "tn", F32, f"{tag}_up_dw")
    dx, dg_pre = _rms_bwd(x, g_pre, dh, 1.0, f"{tag}_pre_bwd", F32, resid=dx_out)
    return dx, dg_pre, dg_post, dw_in, dw_down


def _step_local(x, mem, target, W, S):
    T, D = x.shape
    gW, gS = {}, {}

    f1 = _ffn_fwd(x, S["ffn1_pre_g"], W["ffn1_w_in"], W["ffn1_w_down"], "ffn1")
    x1 = _resid_rms(x, f1[3], S["ffn1_post_g"], 0.5, "ffn1_post")

    h2 = _rms_fwd(x1, S["mix_pre_g"], "mix_pre")
    pm = _mm(h2, W["w_main"], "nn", BF16, "mix_proj_main")
    pf = _mm(h2, W["w_f"], "nn", F32, "mix_proj_f")
    pg = _mm(h2, W["w_gates"], "nn", F32, "mix_proj_gates")
    lbl = S["hg_lb_logits"].reshape(2, HEADS, 1, DH)
    o_a, states = _hgrn_fwd(pm, lbl, "hgrn_fwd")
    oan = _hgout_fwd(o_a, pm, S["hg_norm_g"], "hgrn_out")
    bias = jnp.pad(S["fox_f_bias"], ((0, 0), (0, LANES - HEADS)))
    c = _fox_cumsum(pf, bias, "fox_cumsum")
    c_heads = c[:, :HEADS].T
    c_col, c_row = c_heads[:, :, None], c_heads[:, None, :]
    o_b, lse = _fox_fwd(pm, c_col, c_row, "fox_fwd")
    ya = _mm(oan, W["w_branch_a"], "nn", F32, "branch_a")
    yb = _mm(o_b, W["w_branch_b"], "nn", F32, "branch_b")
    y = _merge_fwd(ya, yb, pg, S["b_gate"], "merge")
    z2 = _mm(y, W["w_out"], "nn", F32, "mix_out")
    x2 = _resid_rms(x1, z2, S["mix_post_g"], 1.0, "mix_post")

    h3 = _rms_fwd(x2, S["mem_pre_g"], "mem_pre")
    memn = _rms_fwd(mem, S["mem_kv_g"], "mem_kv_norm")
    qm = _mm(h3, W["w_mq"], "nn", BF16, "mem_q")
    kv = _mm(memn, W["w_mkv"], "nn", BF16, "mem_kv")
    om = _xattn_fwd(qm, kv, "mem_attn")
    z3 = _mm(om, W["w_mo"], "nn", F32, "mem_o")
    x3 = _resid_rms(x2, z3, S["mem_post_g"], 1.0, "mem_post")

    f2 = _ffn_fwd(x3, S["ffn2_pre_g"], W["ffn2_w_in"], W["ffn2_w_down"], "ffn2")
    dx4, sq = _final_loss(x3, f2[3], S["ffn2_post_g"], 0.5, target, "loss")

    dx3, gS["ffn2_pre_g"], gS["ffn2_post_g"], gW["ffn2_w_in"], gW["ffn2_w_down"] = _ffn_bwd(
        f2, x3, S["ffn2_pre_g"], W["ffn2_w_in"], W["ffn2_w_down"], S["ffn2_post_g"], dx4, "ffn2")

    dz3, gS["mem_post_g"] = _rms_bwd(z3, S["mem_post_g"], dx3, 1.0, "mem_post_bwd", BF16)
    dom = _mm(dz3, W["w_mo"], "nt", BF16, "mem_o_dx")
    gW["w_mo"] = _mm(om, dz3, "tn", F32, "mem_o_dw")
    dqm, dkv = _xattn_bwd(qm, kv, dom, "mem_attn_bwd")
    dh3 = _mm(dqm, W["w_mq"], "nt", F32, "mem_q_dx")
    gW["w_mq"] = _mm(h3, dqm, "tn", F32, "mem_q_dw")
    dkvb = dkv.astype(BF16)
    gW["w_mkv"] = _mm(memn, dkvb, "tn", F32, "mem_kv_dw")
    dmemn = _mm(dkvb, W["w_mkv"], "nt", F32, "mem_kv_dx")
    _, gS["mem_kv_g"] = _rms_bwd(mem, S["mem_kv_g"], dmemn, 1.0, "mem_kv_norm_bwd", BF16)
    dx2, gS["mem_pre_g"] = _rms_bwd(x2, S["mem_pre_g"], dh3, 1.0, "mem_pre_bwd", F32, resid=dx3)

    dz2, gS["mix_post_g"] = _rms_bwd(z2, S["mix_post_g"], dx2, 1.0, "mix_post_bwd", BF16)
    dy = _mm(dz2, W["w_out"], "nt", F32, "mix_out_dx")
    gW["w_out"] = _mm(y, dz2, "tn", F32, "mix_out_dw")
    dya, dyb, dpg, gS["b_gate"] = _merge_bwd(dy, ya, yb, pg, S["b_gate"], "merge_bwd")
    doan = _mm(dya, W["w_branch_a"], "nt", F32, "branch_a_dx")
    gW["w_branch_a"] = _mm(oan, dya, "tn", F32, "branch_a_dw")
    dob = _mm(dyb, W["w_branch_b"], "nt", BF16, "branch_b_dx")
    gW["w_branch_b"] = _mm(o_b, dyb, "tn", F32, "branch_b_dw")

    delta = _fox_delta(dob, o_b, "fox_delta")
    dq_b, dk_b, dv_b, dc_col, dc_row = _fox_bwd(pm, c_col, c_row, dob, lse, delta, "fox_bwd")
    dc = jnp.pad((dc_col.reshape(HEADS, T) + dc_row.reshape(HEADS, T)).T, ((0, 0), (0, LANES - HEADS)))
    dpf, dbias = _fox_dcum(dc, pf, bias, "fox_cumsum_bwd")
    gS["fox_f_bias"] = dbias[:, :HEADS]

    do_a, dg_a, gS["hg_norm_g"] = _hgout_bwd(o_a, pm, S["hg_norm_g"], doan, "hgrn_out_bwd")
    dq_a, df_a, di_a, dlbl = _hgrn_bwd(pm, lbl, states, do_a, "hgrn_bwd")
    gS["hg_lb_logits"] = dlbl.reshape(2, HEADS, DH)

    dpm = jnp.concatenate([dq_a, df_a, di_a, dg_a, dq_b, dk_b, dv_b], axis=1)
    dpf16 = dpf.astype(BF16)
    dh2 = _mm(dpm, W["w_main"], "nt", F32, "mix_proj_main_dx", tn=1024, tk=512)
    dh2 = _mm(dpg, W["w_gates"], "nt", F32, "mix_proj_gates_dx", add=dh2, tn=1024, tk=512)
    dh2 = _mm(dpf16, W["w_f"], "nt", F32, "mix_proj_f_dx", add=dh2, tn=1024)
    gW["w_main"] = _mm(h2, dpm, "tn", F32, "mix_proj_main_dw")
    gW["w_gates"] = _mm(h2, dpg, "tn", F32, "mix_proj_gates_dw")
    gW["w_f"] = _mm(h2, dpf16, "tn", F32, "mix_proj_f_dw")
    dx1, gS["mix_pre_g"] = _rms_bwd(x1, S["mix_pre_g"], dh2, 1.0, "mix_pre_bwd", F32, resid=dx2)

    dx0, gS["ffn1_pre_g"], gS["ffn1_post_g"], gW["ffn1_w_in"], gW["ffn1_w_down"] = _ffn_bwd(
        f1, x, S["ffn1_pre_g"], W["ffn1_w_in"], W["ffn1_w_down"], S["ffn1_post_g"], dx1, "ffn1")
    return sq, dx0, gW, gS


def _train_step(a):
    c_idx = lax.axis_index("c")
    x, mem, target = a["x"][0], a["mem"][0], a["loss_target"][0]
    D = x.shape[1]
    shards = {n: a[n][0] for n in BIG}
    plan, total = _pack_plan(shards)

    fox_scale = 1.0 / math.sqrt(DH)
    n_mine = shards["w_in"].shape[1]
    dev = 4 * lax.axis_index("x") + 2 * lax.axis_index("y") + c_idx
    cols = dev * n_mine + jnp.arange(n_mine)
    is_fox_q = (cols >= 4 * D) & (cols < 5 * D)
    sent = dict(shards, w_in=shards["w_in"] * jnp.where(is_fox_q, fox_scale, 1.0)[None, :])
    gathered = _ag_sibling(_ag_chips(_pack_shards(sent, total, BF16), "ag_chips"), "ag_sibling")
    W = _full_from_gathered(gathered, plan, shards)
    n_main = 7 * D
    w_in = W.pop("w_in")
    W["w_main"] = w_in[:, :n_main]
    W["w_f"] = jnp.pad(w_in[:, n_main:n_main + HEADS], ((0, 0), (0, LANES - HEADS)))
    W["w_gates"] = w_in[:, n_main + HEADS:]
    W["ffn1_w_in"] = _swiglu_interleave(W["ffn1_w_in"])
    W["ffn2_w_in"] = _swiglu_interleave(W["ffn2_w_in"])
    S = {n: a[n] for n in SMALL}

    sq, grad_x, gW, gS = _step_local(x, mem, target, W, S)

    g_main = gW.pop("w_main")
    gW["w_in"] = jnp.concatenate([g_main[:, :4 * D], g_main[:, 4 * D:5 * D] * fox_scale, g_main[:, 5 * D:],
                                  gW.pop("w_f")[:, :HEADS], gW.pop("w_gates")], axis=1)
    gW["ffn1_w_in"] = _swiglu_deinterleave(gW["ffn1_w_in"])
    gW["ffn2_w_in"] = _swiglu_deinterleave(gW["ffn2_w_in"])
    blocks = [_blocks_from_full(gW[n], n, shards) for n in BIG]
    tail = [jnp.zeros((4, total - sum(b.shape[2] for b in blocks), PACK_COLS), F32)]

    def core_half(core):
        return jnp.concatenate([lax.dynamic_index_in_dim(b, core, axis=1, keepdims=False) for b in blocks] + tail,
                               axis=1)

    keep, send = core_half(c_idx), core_half(1 - c_idx)
    pair = _pair_add(keep, _rs_sibling(send, "rs_sibling"), BF16, "rs_pair_add")
    slots = _rs_chips(pair, "rs_chips")
    g_big, d_big, m_big, v_big = _adamw_reduce(
        slots, _pack_shards(shards, total, F32), _pack_shards({n: a["m_" + n][0] for n in BIG}, total, F32),
        _pack_shards({n: a["v_" + n][0] for n in BIG}, total, F32), "adamw_big")

    loss_row = jnp.pad(sq[:1, :1] * (0.5 / D), ((0, 0), (0, PACK_COLS - 1)))
    small_all = _ag_sibling(_ag_chips(_pack_small(gS, loss_row), "small_ag_chips"), "small_ag_sibling")
    small_slots = small_all.transpose(1, 0, 2, 3).reshape(8, SMALL_ROWS, PACK_COLS)
    zero_row = jnp.zeros((1, PACK_COLS), F32)
    g_sm, d_sm, m_sm, v_sm = _adamw_reduce(
        small_slots, _pack_small({n: a[n] for n in SMALL}, zero_row),
        _pack_small({n: a["m_" + n] for n in SMALL}, zero_row),
        _pack_small({n: a["v_" + n] for n in SMALL}, zero_row), "adamw_small")

    def unpack(big, small):
        out = _unpack_small(small, {n: a[n] for n in SMALL})
        for n in BIG:
            r0, rows = plan[n]
            out[n] = big[r0:r0 + rows].reshape(a[n].shape)
        return [out[n] for n in WEIGHTS]

    loss = g_sm[SMALL_ROWS - 1, 0]
    return (loss, grad_x[None], *unpack(g_big, g_sm), *unpack(d_big, d_sm), *unpack(m_big, m_sm),
            *unpack(v_big, v_sm))


def kernel(x, mem, ffn1_pre_g, ffn1_w_in, ffn1_w_down, ffn1_post_g, mix_pre_g, w_in, hg_lb_logits, hg_norm_g, fox_f_bias, w_branch_a, w_branch_b, b_gate, w_out, mix_post_g, mem_pre_g, mem_kv_g, w_mq, w_mkv, w_mo, mem_post_g, ffn2_pre_g, ffn2_w_in, ffn2_w_down, ffn2_post_g, loss_target, m_ffn1_pre_g, m_ffn1_w_in, m_ffn1_w_down, m_ffn1_post_g, m_mix_pre_g, m_w_in, m_hg_lb_logits, m_hg_norm_g, m_fox_f_bias, m_w_branch_a, m_w_branch_b, m_b_gate, m_w_out, m_mix_post_g, m_mem_pre_g, m_mem_kv_g, m_w_mq, m_w_mkv, m_w_mo, m_mem_post_g, m_ffn2_pre_g, m_ffn2_w_in, m_ffn2_w_down, m_ffn2_post_g, v_ffn1_pre_g, v_ffn1_w_in, v_ffn1_w_down, v_ffn1_post_g, v_mix_pre_g, v_w_in, v_hg_lb_logits, v_hg_norm_g, v_fox_f_bias, v_w_branch_a, v_w_branch_b, v_b_gate, v_w_out, v_mix_post_g, v_mem_pre_g, v_mem_kv_g, v_w_mq, v_w_mkv, v_w_mo, v_mem_post_g, v_ffn2_pre_g, v_ffn2_w_in, v_ffn2_w_down, v_ffn2_post_g):
    return _train_step(dict(locals()))
```

```python
import functools
import math

import jax
import jax.numpy as jnp
from jax import lax
from jax.experimental import pallas as pl
from jax.experimental.pallas import tpu as pltpu

F32 = jnp.float32
BF16 = jnp.bfloat16
MESH = pl.DeviceIdType.MESH

EPS = 1e-6
HEADS = 8
DH = 128
MEM_HEADS = 4
CHUNK = 128
HALF = CHUNK // 2
SWIGLU_TILE = 256
LANES = 128
PACK_COLS = 1024
PACK_ROW_TILE = 256
ROW_TILE = 512
SEQ_BLOCK = 512
ATTN_TILE = 1024
ATTN_ROWS = 256
EXP_CLAMP = 80.0
NEG_BIG = -1e30

ADAM_LR, ADAM_B1, ADAM_B2, ADAM_EPS, ADAM_WD, ADAM_STEP = 0.001, 0.9, 0.999, 1e-08, 0.01, 10

VMEM_LIMIT = 48 * 1024 * 1024

_DN = {
    "nn": (((1,), (0,)), ((), ())),
    "nt": (((1,), (1,)), ((), ())),
    "tn": (((0,), (0,)), ((), ())),
}

BIG = ["ffn1_w_in", "ffn1_w_down", "w_in", "w_branch_a", "w_branch_b", "w_out", "w_mq", "w_mkv", "w_mo",
       "ffn2_w_in", "ffn2_w_down"]
COL_SHARDED = {"ffn1_w_in", "w_in", "w_mkv", "ffn2_w_in"}
SMALL = ["ffn1_pre_g", "ffn1_post_g", "mix_pre_g", "hg_lb_logits", "hg_norm_g", "fox_f_bias", "b_gate",
         "mix_post_g", "mem_pre_g", "mem_kv_g", "mem_post_g", "ffn2_pre_g", "ffn2_post_g"]
WEIGHTS = ["ffn1_pre_g", "ffn1_w_in", "ffn1_w_down", "ffn1_post_g", "mix_pre_g", "w_in", "hg_lb_logits",
           "hg_norm_g", "fox_f_bias", "w_branch_a", "w_branch_b", "b_gate", "w_out", "mix_post_g", "mem_pre_g",
           "mem_kv_g", "w_mq", "w_mkv", "w_mo", "mem_post_g", "ffn2_pre_g", "ffn2_w_in", "ffn2_w_down",
           "ffn2_post_g"]


def _dot(a, b, mode="nn"):
    return lax.dot_general(a, b, _DN[mode], preferred_element_type=F32)


def _sig(x):
    return 1.0 / (1.0 + jnp.exp(-x))


def _params(*dims):
    return pltpu.CompilerParams(dimension_semantics=dims if dims else None, vmem_limit_bytes=VMEM_LIMIT)


def _tile(dim, pref):
    if dim <= pref:
        return dim
    t = (pref // LANES) * LANES
    while t >= LANES:
        if dim % t == 0:
            return t
        t -= LANES
    raise ValueError(f"no tile for {dim}")


def _colsum(x):
    return jnp.sum(x, axis=0, keepdims=True)


def _rowsum(x):
    return jnp.sum(x, axis=1, keepdims=True)


def _iota(shape, axis):
    return lax.broadcasted_iota(jnp.int32, shape, axis)


def _pick_row(x, r):
    return _colsum(jnp.where(_iota(x.shape, 0) == r, x, 0.0))


def _tri_dot(tri, x):
    hi = x.astype(BF16)
    r1 = x - hi.astype(F32)
    mid = r1.astype(BF16)
    lo = (r1 - mid.astype(F32)).astype(BF16)
    return _dot(tri, hi) + _dot(tri, mid) + _dot(tri, lo)


def _mm(a, b, mode, out_dtype, name, add=None, tm=1024, tn=512, tk=1024):
    if mode == "nn":
        (M, K), (K2, N) = a.shape, b.shape
    elif mode == "nt":
        (M, K), (N, K2) = a.shape, b.shape
    else:
        (K, M), (K2, N) = a.shape, b.shape
    assert K == K2, (name, a.shape, b.shape)
    tm, tn, tk = _tile(M, tm), _tile(N, tn), _tile(K, tk)
    nk = K // tk
    if mode == "tn":
        a_spec = pl.BlockSpec((tk, tm), lambda i, j, k: (k, i))
    else:
        a_spec = pl.BlockSpec((tm, tk), lambda i, j, k: (i, k))
    if mode == "nt":
        b_spec = pl.BlockSpec((tn, tk), lambda i, j, k: (j, k))
    else:
        b_spec = pl.BlockSpec((tk, tn), lambda i, j, k: (k, j))
    o_spec = pl.BlockSpec((tm, tn), lambda i, j, k: (i, j))
    has_add = add is not None

    def body(*refs):
        a_ref, b_ref = refs[0], refs[1]
        c_ref = refs[2] if has_add else None
        o_ref = refs[3] if has_add else refs[2]
        part = _dot(a_ref[...], b_ref[...], mode)
        if nk == 1:
            if has_add:
                part = part + c_ref[...]
            o_ref[...] = part.astype(o_ref.dtype)
            return
        acc_ref = refs[-1]
        k = pl.program_id(2)

        @pl.when(k == 0)
        def _():
            acc_ref[...] = part + c_ref[...] if has_add else part

        @pl.when(k > 0)
        def _():
            acc_ref[...] += part

        @pl.when(k == nk - 1)
        def _():
            o_ref[...] = acc_ref[...].astype(o_ref.dtype)

    in_specs = [a_spec, b_spec] + ([o_spec] if has_add else [])
    args = (a, b) + ((add,) if has_add else ())
    return pl.pallas_call(
        body, name=name, grid=(M // tm, N // tn, nk), in_specs=in_specs, out_specs=o_spec,
        out_shape=jax.ShapeDtypeStruct((M, N), out_dtype),
        scratch_shapes=[pltpu.VMEM((tm, tn), F32)] if nk > 1 else [],
        compiler_params=_params("parallel", "parallel", "arbitrary"),
    )(*args)


def _rms_fwd(x, g, name, out_dtype=BF16):
    T, D = x.shape
    tr = _tile(T, ROW_TILE)

    def body(x_ref, g_ref, o_ref):
        xv = x_ref[...]
        r = lax.rsqrt(jnp.mean(xv * xv, axis=-1, keepdims=True) + EPS)
        o_ref[...] = (xv * r * g_ref[...]).astype(o_ref.dtype)

    return pl.pallas_call(
        body, name=name, grid=(T // tr,),
        in_specs=[pl.BlockSpec((tr, D), lambda i: (i, 0)), pl.BlockSpec((1, D), lambda i: (0, 0))],
        out_specs=pl.BlockSpec((tr, D), lambda i: (i, 0)),
        out_shape=jax.ShapeDtypeStruct((T, D), out_dtype), compiler_params=_params("parallel"),
    )(x, g)


def _resid_rms(x, z, g, scale, name):
    T, D = x.shape
    tr = _tile(T, ROW_TILE)

    def body(x_ref, z_ref, g_ref, o_ref):
        zv = z_ref[...]
        r = lax.rsqrt(jnp.mean(zv * zv, axis=-1, keepdims=True) + EPS)
        o_ref[...] = x_ref[...] + scale * (zv * r * g_ref[...])

    row = pl.BlockSpec((tr, D), lambda i: (i, 0))
    return pl.pallas_call(
        body, name=name, grid=(T // tr,), in_specs=[row, row, pl.BlockSpec((1, D), lambda i: (0, 0))],
        out_specs=row, out_shape=jax.ShapeDtypeStruct((T, D), F32), compiler_params=_params("parallel"),
    )(x, z, g)


def _final_loss(x, z, g, scale, target, name):
    T, D = x.shape
    tr = _tile(T, ROW_TILE)

    def body(x_ref, z_ref, g_ref, t_ref, dx_ref, acc_ref):
        @pl.when(pl.program_id(0) == 0)
        def _():
            acc_ref[...] = jnp.zeros_like(acc_ref)

        zv = z_ref[...]
        r = lax.rsqrt(jnp.mean(zv * zv, axis=-1, keepdims=True) + EPS)
        e = x_ref[...] + scale * (zv * r * g_ref[...]) - t_ref[...]
        dx_ref[...] = e * (1.0 / D)
        acc_ref[...] += _colsum(_rowsum(e * e))

    row = pl.BlockSpec((tr, D), lambda i: (i, 0))
    return pl.pallas_call(
        body, name=name, grid=(T // tr,), in_specs=[row, row, pl.BlockSpec((1, D), lambda i: (0, 0)), row],
        out_specs=[row, pl.BlockSpec((8, LANES), lambda i: (0, 0))],
        out_shape=[jax.ShapeDtypeStruct((T, D), F32), jax.ShapeDtypeStruct((8, LANES), F32)],
        compiler_params=_params("arbitrary"),
    )(x, z, g, target)


def _rms_bwd(xin, g, dy, scale, name, out_dtype, resid=None):
    T, D = xin.shape
    tr = _tile(T, ROW_TILE)
    has_resid = resid is not None

    def body(*refs):
        x_ref, g_ref, dy_ref = refs[:3]
        r_ref = refs[3] if has_resid else None
        dx_ref, dg_ref = refs[-2], refs[-1]

        @pl.when(pl.program_id(0) == 0)
        def _():
            dg_ref[...] = jnp.zeros_like(dg_ref)

        xv = x_ref[...]
        r = lax.rsqrt(jnp.mean(xv * xv, axis=-1, keepdims=True) + EPS)
        xh = xv * r
        dyv = dy_ref[...].astype(F32) * scale
        dxh = dyv * g_ref[...]
        dx = r * (dxh - xh * jnp.mean(dxh * xh, axis=-1, keepdims=True))
        if has_resid:
            dx = dx + r_ref[...]
        dx_ref[...] = dx.astype(dx_ref.dtype)
        dg_ref[...] += _colsum(dyv * xh)

    row = pl.BlockSpec((tr, D), lambda i: (i, 0))
    vec = pl.BlockSpec((1, D), lambda i: (0, 0))
    return pl.pallas_call(
        body, name=name, grid=(T // tr,), in_specs=[row, vec, row] + ([row] if has_resid else []),
        out_specs=[row, vec],
        out_shape=[jax.ShapeDtypeStruct((T, D), out_dtype), jax.ShapeDtypeStruct((1, D), F32)],
        compiler_params=_params("arbitrary"),
    )(*((xin, g, dy) + ((resid,) if has_resid else ())))


def _swiglu_fwd(u, name):
    T, F2 = u.shape
    tf = SWIGLU_TILE
    tr = _tile(T, 1024)

    def body(u_ref, o_ref):
        gate = u_ref[:, :tf].astype(F32)
        up = u_ref[:, tf:].astype(F32)
        o_ref[...] = (gate * _sig(gate) * up).astype(o_ref.dtype)

    return pl.pallas_call(
        body, name=name, grid=(T // tr, F2 // (2 * tf)),
        in_specs=[pl.BlockSpec((tr, 2 * tf), lambda i, j: (i, j))],
        out_specs=pl.BlockSpec((tr, tf), lambda i, j: (i, j)),
        out_shape=jax.ShapeDtypeStruct((T, F2 // 2), BF16), compiler_params=_params("parallel", "parallel"),
    )(u)


def _swiglu_bwd(u, da, name):
    T, F2 = u.shape
    tf = SWIGLU_TILE
    tr = _tile(T, 1024)

    def body(u_ref, da_ref, o_ref):
        gate = u_ref[:, :tf].astype(F32)
        up = u_ref[:, tf:].astype(F32)
        d = da_ref[...].astype(F32)
        s = _sig(gate)
        o_ref[:, :tf] = (d * up * (s * (1.0 + gate * (1.0 - s)))).astype(o_ref.dtype)
        o_ref[:, tf:] = (d * gate * s).astype(o_ref.dtype)

    return pl.pallas_call(
        body, name=name, grid=(T // tr, F2 // (2 * tf)),
        in_specs=[pl.BlockSpec((tr, 2 * tf), lambda i, j: (i, j)), pl.BlockSpec((tr, tf), lambda i, j: (i, j))],
        out_specs=pl.BlockSpec((tr, 2 * tf), lambda i, j: (i, j)),
        out_shape=jax.ShapeDtypeStruct((T, F2), BF16), compiler_params=_params("parallel", "parallel"),
    )(u, da)


def _hgout_fwd(o_a, pm, g, name):
    T, D = o_a.shape
    tr = _tile(T, ROW_TILE)

    def body(o_ref, ga_ref, g_ref, out_ref):
        ov = o_ref[...]
        r = lax.rsqrt(jnp.mean(ov * ov, axis=-1, keepdims=True) + EPS)
        ga = ga_ref[...].astype(F32)
        out_ref[...] = (ov * r * g_ref[...] * (ga * _sig(ga))).astype(out_ref.dtype)

    row = pl.BlockSpec((tr, D), lambda i: (i, 0))
    return pl.pallas_call(
        body, name=name, grid=(T // tr,),
        in_specs=[row, pl.BlockSpec((tr, D), lambda i: (i, 3)), pl.BlockSpec((1, D), lambda i: (0, 0))],
        out_specs=row, out_shape=jax.ShapeDtypeStruct((T, D), BF16), compiler_params=_params("parallel"),
    )(o_a, pm, g)


def _hgout_bwd(o_a, pm, g, d_out, name):
    T, D = o_a.shape
    tr = _tile(T, ROW_TILE)

    def body(o_ref, ga_ref, g_ref, d_ref, do_ref, dga_ref, dg_ref):
        @pl.when(pl.program_id(0) == 0)
        def _():
            dg_ref[...] = jnp.zeros_like(dg_ref)

        ov = o_ref[...]
        r = lax.rsqrt(jnp.mean(ov * ov, axis=-1, keepdims=True) + EPS)
        oh = ov * r
        ga = ga_ref[...].astype(F32)
        s = _sig(ga)
        d = d_ref[...].astype(F32)
        dn = d * (ga * s)
        dga_ref[...] = (d * (oh * g_ref[...]) * (s * (1.0 + ga * (1.0 - s)))).astype(dga_ref.dtype)
        dxh = dn * g_ref[...]
        do_ref[...] = (r * (dxh - oh * jnp.mean(dxh * oh, axis=-1, keepdims=True))).astype(do_ref.dtype)
        dg_ref[...] += _colsum(dn * oh)

    row = pl.BlockSpec((tr, D), lambda i: (i, 0))
    vec = pl.BlockSpec((1, D), lambda i: (0, 0))
    return pl.pallas_call(
        body, name=name, grid=(T // tr,), in_specs=[row, pl.BlockSpec((tr, D), lambda i: (i, 3)), vec, row],
        out_specs=[row, row, vec],
        out_shape=[jax.ShapeDtypeStruct((T, D), BF16), jax.ShapeDtypeStruct((T, D), BF16),
                   jax.ShapeDtypeStruct((1, D), F32)],
        compiler_params=_params("arbitrary"),
    )(o_a, pm, g, d_out)


def _merge_fwd(ya, yb, pg, bg, name):
    T, D = ya.shape
    tr = _tile(T, 256)

    def body(ya_ref, yb_ref, pg_ref, bg_ref, o_ref):
        g0 = _sig(pg_ref[:, :D] + bg_ref[:, :D])
        g1 = _sig(pg_ref[:, D:] + bg_ref[:, D:])
        o_ref[...] = (g0 * ya_ref[...] + g1 * yb_ref[...]).astype(o_ref.dtype)

    row = pl.BlockSpec((tr, D), lambda i: (i, 0))
    return pl.pallas_call(
        body, name=name, grid=(T // tr,),
        in_specs=[row, row, pl.BlockSpec((tr, 2 * D), lambda i: (i, 0)), pl.BlockSpec((1, 2 * D), lambda i: (0, 0))],
        out_specs=row, out_shape=jax.ShapeDtypeStruct((T, D), BF16), compiler_params=_params("parallel"),
    )(ya, yb, pg, bg)


def _merge_bwd(dy, ya, yb, pg, bg, name):
    T, D = ya.shape
    tr = _tile(T, 256)

    def body(dy_ref, ya_ref, yb_ref, pg_ref, bg_ref, dya_ref, dyb_ref, dpg_ref, dbg_ref):
        @pl.when(pl.program_id(0) == 0)
        def _():
            dbg_ref[...] = jnp.zeros_like(dbg_ref)

        d = dy_ref[...]
        g0 = _sig(pg_ref[:, :D] + bg_ref[:, :D])
        g1 = _sig(pg_ref[:, D:] + bg_ref[:, D:])
        dya_ref[...] = (d * g0).astype(dya_ref.dtype)
        dyb_ref[...] = (d * g1).astype(dyb_ref.dtype)
        dg0 = d * ya_ref[...] * (g0 * (1.0 - g0))
        dg1 = d * yb_ref[...] * (g1 * (1.0 - g1))
        dpg_ref[:, :D] = dg0.astype(dpg_ref.dtype)
        dpg_ref[:, D:] = dg1.astype(dpg_ref.dtype)
        dbg_ref[:, :D] += _colsum(dg0)
        dbg_ref[:, D:] += _colsum(dg1)

    row = pl.BlockSpec((tr, D), lambda i: (i, 0))
    wide = pl.BlockSpec((tr, 2 * D), lambda i: (i, 0))
    wvec = pl.BlockSpec((1, 2 * D), lambda i: (0, 0))
    return pl.pallas_call(
        body, name=name, grid=(T // tr,), in_specs=[row, row, row, wide, wvec],
        out_specs=[row, row, wide, wvec],
        out_shape=[jax.ShapeDtypeStruct((T, D), BF16), jax.ShapeDtypeStruct((T, D), BF16),
                   jax.ShapeDtypeStruct((T, 2 * D), BF16), jax.ShapeDtypeStruct((1, 2 * D), F32)],
        compiler_params=_params("arbitrary"),
    )(dy, ya, yb, pg, bg)


def _hgrn_chunk_terms(q, fl, lb, tri):
    shape = q.shape
    row = _iota(shape, 0)
    sg = _sig(fl)
    f = lb + (1.0 - lb) * sg
    k = 1.0 - f
    b = _tri_dot(tri, jnp.log(f))
    ref1 = jnp.where(row < HALF, _pick_row(b, HALF // 2), _pick_row(b, HALF + HALF // 2))
    b_half = _pick_row(b, HALF - 1)
    b_last = _pick_row(b, CHUNK - 1)
    sq = _sig(q)
    qs = q * sq
    e_q1 = jnp.exp(jnp.minimum(b - ref1, EXP_CLAMP))
    e_k1 = jnp.exp(jnp.minimum(ref1 - b, EXP_CLAMP))
    e_q2 = jnp.exp(jnp.minimum(b - b_half, 0.0))
    e_k2 = jnp.exp(jnp.minimum(b_half - b, 0.0))
    e_b = jnp.exp(b)
    e_kd = jnp.exp(b_last - b)
    return dict(sg=sg, f=f, k=k, sq=sq, qs=qs, e_q1=e_q1, e_k1=e_k1, e_q2=e_q2, e_k2=e_k2, e_b=e_b, e_kd=e_kd,
                e_last=jnp.exp(b_last))


def _hgrn_masks():
    r = _iota((CHUNK, CHUNK), 0)
    c = _iota((CHUNK, CHUNK), 1)
    causal = r >= c
    same = (r < HALF) == (c < HALF)
    return causal, causal & same, (r >= HALF) & (c < HALF)


def _softmax_lb(lbl_ref):
    l0, l1 = lbl_ref[0, 0], lbl_ref[1, 0]
    mx = jnp.maximum(l0, l1)
    e0, e1 = jnp.exp(l0 - mx), jnp.exp(l1 - mx)
    return e0 / (e0 + e1)


def _hgrn_fwd(pm, lbl, name):
    T = pm.shape[0]
    tb = _tile(T, SEQ_BLOCK)
    nc = tb // CHUNK

    def body(q_ref, f_ref, i_ref, lbl_ref, o_ref, st_ref, s_sc):
        @pl.when(pl.program_id(1) == 0)
        def _():
            s_sc[...] = jnp.zeros_like(s_sc)

        lb = _softmax_lb(lbl_ref)
        causal, m1, m2 = _hgrn_masks()
        tri = jnp.where(causal, 1.0, 0.0).astype(BF16)
        for ci in range(nc):
            sl = pl.ds(ci * CHUNK, CHUNK)
            t = _hgrn_chunk_terms(q_ref[sl, :].astype(F32), f_ref[sl, :].astype(F32), lb, tri)
            iv = i_ref[sl, :]
            a1 = _dot((t["qs"] * t["e_q1"]).astype(BF16), (t["k"] * t["e_k1"]).astype(BF16), "nt")
            a2 = _dot((t["qs"] * t["e_q2"]).astype(BF16), (t["k"] * t["e_k2"]).astype(BF16), "nt")
            a = jnp.where(m1, a1, 0.0) + jnp.where(m2, a2, 0.0)
            st = s_sc[...]
            st_ref[0, ci] = st
            o_ref[sl, :] = _dot(a.astype(BF16), iv) + _dot((t["qs"] * t["e_b"]).astype(BF16), st.astype(BF16), "nt")
            s_sc[...] = t["e_last"] * st + _dot(iv, (t["k"] * t["e_kd"]).astype(BF16), "tn")

    blk = lambda off: pl.BlockSpec((tb, DH), lambda h, b: (b, off + h))
    return pl.pallas_call(
        body, name=name, grid=(HEADS, T // tb),
        in_specs=[blk(0), blk(HEADS), blk(2 * HEADS), pl.BlockSpec((2, 1, 1, DH), lambda h, b: (0, h, 0, 0))],
        out_specs=[pl.BlockSpec((tb, DH), lambda h, b: (b, h)),
                   pl.BlockSpec((1, nc, DH, DH), lambda h, b: (h, b, 0, 0))],
        out_shape=[jax.ShapeDtypeStruct((T, HEADS * DH), F32),
                   jax.ShapeDtypeStruct((HEADS, T // CHUNK, DH, DH), F32)],
        scratch_shapes=[pltpu.VMEM((DH, DH), F32)],
        compiler_params=_params("parallel", "arbitrary"),
    )(pm, pm, pm, lbl)


def _hgrn_bwd(pm, lbl, states, do, name):
    T = pm.shape[0]
    tb = _tile(T, SEQ_BLOCK)
    nc = tb // CHUNK
    nb = T // tb

    def body(q_ref, f_ref, i_ref, lbl_ref, st_ref, do_ref, dq_ref, df_ref, di_ref, dl_ref, ds_sc, dlb_sc):
        @pl.when(pl.program_id(1) == 0)
        def _():
            ds_sc[...] = jnp.zeros_like(ds_sc)
            dlb_sc[...] = jnp.zeros_like(dlb_sc)

        lb = _softmax_lb(lbl_ref)
        causal, m1, m2 = _hgrn_masks()
        tri = jnp.where(causal, 1.0, 0.0).astype(BF16)
        tri_rev = jnp.where(_iota((CHUNK, CHUNK), 0) <= _iota((CHUNK, CHUNK), 1), 1.0, 0.0).astype(BF16)
        last_row = _iota((CHUNK, DH), 0) == CHUNK - 1
        for ci in reversed(range(nc)):
            sl = pl.ds(ci * CHUNK, CHUNK)
            q = q_ref[sl, :].astype(F32)
            t = _hgrn_chunk_terms(q, f_ref[sl, :].astype(F32), lb, tri)
            iv = i_ref[sl, :]
            dov = do_ref[sl, :]
            qe1, ke1 = t["qs"] * t["e_q1"], t["k"] * t["e_k1"]
            qe2, ke2 = t["qs"] * t["e_q2"], t["k"] * t["e_k2"]
            qi, kd = t["qs"] * t["e_b"], t["k"] * t["e_kd"]
            qe1b, ke1b, qe2b, ke2b = qe1.astype(BF16), ke1.astype(BF16), qe2.astype(BF16), ke2.astype(BF16)
            a = jnp.where(m1, _dot(qe1b, ke1b, "nt"), 0.0) + jnp.where(m2, _dot(qe2b, ke2b, "nt"), 0.0)
            st = st_ref[0, ci]
            dsn = ds_sc[...]
            dsnb = dsn.astype(BF16)
            da = _dot(dov, iv, "nt")
            da1 = jnp.where(m1, da, 0.0).astype(BF16)
            da2 = jnp.where(m2, da, 0.0).astype(BF16)
            di_ref[sl, :] = (_dot(a.astype(BF16), dov, "tn") + _dot(kd.astype(BF16), dsnb, "nt")).astype(di_ref.dtype)
            dqe1, dke1 = _dot(da1, ke1b), _dot(da1, qe1b, "tn")
            dqe2, dke2 = _dot(da2, ke2b), _dot(da2, qe2b, "tn")
            dqi = _dot(dov, st.astype(BF16))
            dkd = _dot(iv, dsnb)
            ds_sc[...] = t["e_last"] * dsn + _dot(dov, qi.astype(BF16), "tn")
            dqs = dqe1 * t["e_q1"] + dqe2 * t["e_q2"] + dqi * t["e_b"]
            dk = dke1 * t["e_k1"] + dke2 * t["e_k2"] + dkd * t["e_kd"]
            qib, kdb = qi.astype(BF16).astype(F32), kd.astype(BF16).astype(F32)
            db = (dqe1 * qe1b.astype(F32) - dke1 * ke1b.astype(F32) + dqe2 * qe2b.astype(F32)
                  - dke2 * ke2b.astype(F32) + dqi * qib - dkd * kdb)
            extra = _colsum(dkd * kdb) + t["e_last"] * _colsum(dsn * st)
            db = db + jnp.where(last_row, extra, 0.0)
            dlf = _tri_dot(tri_rev, db)
            dfv = dlf / t["f"] - dk
            sg = t["sg"]
            df_ref[sl, :] = (dfv * (1.0 - lb) * sg * (1.0 - sg)).astype(df_ref.dtype)
            dlb_sc[...] += _colsum(dfv * (1.0 - sg))
            sq = t["sq"]
            dq_ref[sl, :] = (dqs * (sq * (1.0 + q * (1.0 - sq)))).astype(dq_ref.dtype)

        @pl.when(pl.program_id(1) == nb - 1)
        def _():
            dl0 = dlb_sc[...] * lb * (1.0 - lb)
            dl_ref[0, 0] = dl0
            dl_ref[1, 0] = -dl0

    blk = lambda off: pl.BlockSpec((tb, DH), lambda h, b: (nb - 1 - b, off + h))
    lspec = pl.BlockSpec((2, 1, 1, DH), lambda h, b: (0, h, 0, 0))
    out_blk = pl.BlockSpec((tb, DH), lambda h, b: (nb - 1 - b, h))
    D = HEADS * DH
    return pl.pallas_call(
        body, name=name, grid=(HEADS, nb),
        in_specs=[blk(0), blk(HEADS), blk(2 * HEADS), lspec,
                  pl.BlockSpec((1, nc, DH, DH), lambda h, b: (h, nb - 1 - b, 0, 0)), out_blk],
        out_specs=[out_blk, out_blk, out_blk, lspec],
        out_shape=[jax.ShapeDtypeStruct((T, D), BF16)] * 3 + [jax.ShapeDtypeStruct((2, HEADS, 1, DH), F32)],
        scratch_shapes=[pltpu.VMEM((DH, DH), F32), pltpu.VMEM((1, DH), F32)],
        compiler_params=_params("parallel", "arbitrary"),
    )(pm, pm, pm, lbl, states, do)


def _log_sigmoid(x):
    return jnp.minimum(x, 0.0) - jnp.log(1.0 + jnp.exp(-jnp.abs(x)))


def _fox_cumsum(pf, bias, name):
    T = pf.shape[0]
    tb = _tile(T, SEQ_BLOCK)

    def body(x_ref, b_ref, c_ref, carry):
        @pl.when(pl.program_id(0) == 0)
        def _():
            carry[...] = jnp.zeros_like(carry)

        tri = jnp.where(_iota((tb, tb), 0) >= _iota((tb, tb), 1), 1.0, 0.0).astype(BF16)
        c = _tri_dot(tri, _log_sigmoid(x_ref[...] + b_ref[...])) + carry[...]
        c_ref[...] = c
        carry[...] = _pick_row(c, tb - 1)

    row = pl.BlockSpec((tb, LANES), lambda i: (i, 0))
    return pl.pallas_call(
        body, name=name, grid=(T // tb,), in_specs=[row, pl.BlockSpec((1, LANES), lambda i: (0, 0))],
        out_specs=row, out_shape=jax.ShapeDtypeStruct((T, LANES), F32),
        scratch_shapes=[pltpu.VMEM((1, LANES), F32)], compiler_params=_params("arbitrary"),
    )(pf, bias)


def _fox_dcum(dc, pf, bias, name):
    T = pf.shape[0]
    tb = _tile(T, SEQ_BLOCK)
    nb = T // tb

    def body(dc_ref, x_ref, b_ref, dx_ref, db_ref, carry):
        @pl.when(pl.program_id(0) == 0)
        def _():
            carry[...] = jnp.zeros_like(carry)
            db_ref[...] = jnp.zeros_like(db_ref)

        tri_rev = jnp.where(_iota((tb, tb), 0) <= _iota((tb, tb), 1), 1.0, 0.0).astype(BF16)
        dls = _tri_dot(tri_rev, dc_ref[...]) + carry[...]
        carry[...] = _pick_row(dls, 0)
        dx = dls * (1.0 - _sig(x_ref[...] + b_ref[...]))
        dx_ref[...] = dx
        db_ref[...] += _colsum(dx)

    row = pl.BlockSpec((tb, LANES), lambda i: (nb - 1 - i, 0))
    vec = pl.BlockSpec((1, LANES), lambda i: (0, 0))
    return pl.pallas_call(
        body, name=name, grid=(nb,), in_specs=[row, row, vec], out_specs=[row, vec],
        out_shape=[jax.ShapeDtypeStruct((T, LANES), F32), jax.ShapeDtypeStruct((1, LANES), F32)],
        scratch_shapes=[pltpu.VMEM((1, LANES), F32)], compiler_params=_params("arbitrary"),
    )(dc, pf, bias)


_Q_OFF, _K_OFF, _V_OFF = 4 * HEADS, 5 * HEADS, 6 * HEADS


def _causal_pairs(nq, by_key):
    if by_key:
        pairs = [(i, j) for j in range(nq) for i in range(j, nq)]
    else:
        pairs = [(i, j) for i in range(nq) for j in range(i + 1)]
    return jnp.asarray([p[0] for p in pairs], jnp.int32), jnp.asarray([p[1] for p in pairs], jnp.int32)


def _fox_logits(q, k, cq, ck, row0, masked):
    s = _dot(q, k, "nt") + (cq - ck)
    if masked:
        s = jnp.where(_iota(s.shape, 0) + row0 >= _iota(s.shape, 1), s, NEG_BIG)
    return s


def _fox_fwd(pm, c_col, c_row, name):
    T = pm.shape[0]
    tq = _tile(T, ATTN_TILE)
    nq = T // tq
    rg = min(ATTN_ROWS, tq)
    qi_tab, kj_tab = _causal_pairs(nq, by_key=False)

    def body(qi_ref, kj_ref, q_ref, k_ref, v_ref, cq_ref, ck_ref, o_ref, lse_ref, m_sc, l_sc, acc_sc):
        t = pl.program_id(1)
        i, j = qi_ref[t], kj_ref[t]

        @pl.when(j == 0)
        def _():
            m_sc[...] = jnp.full_like(m_sc, NEG_BIG)
            l_sc[...] = jnp.zeros_like(l_sc)
            acc_sc[...] = jnp.zeros_like(acc_sc)

        def step(diag):
            m_all, l_all, acc_all = m_sc[...], l_sc[...], acc_sc[...]
            ms, ls, accs = [], [], []
            for r in range(tq // rg):
                rows = slice(r * rg, (r + 1) * rg)
                w = (r + 1) * rg if diag else tq
                s = _fox_logits(q_ref[rows, :], k_ref[:w, :], cq_ref[0, rows, :], ck_ref[0, :, :w], r * rg, diag)
                m_old = m_all[rows, :]
                m_new = jnp.maximum(m_old, jnp.max(s, axis=1, keepdims=True))
                alpha = jnp.exp(m_old - m_new)
                p = jnp.exp(s - m_new)
                ms.append(m_new)
                ls.append(alpha * l_all[rows, :] + _rowsum(p))
                accs.append(alpha * acc_all[rows, :] + _dot(p.astype(BF16), v_ref[:w, :]))
            m_sc[...] = jnp.concatenate(ms, axis=0)
            l_sc[...] = jnp.concatenate(ls, axis=0)
            acc_sc[...] = jnp.concatenate(accs, axis=0)

        @pl.when(j < i)
        def _():
            step(False)

        @pl.when(j == i)
        def _():
            step(True)
            o_ref[...] = (acc_sc[...] / l_sc[...]).astype(o_ref.dtype)
            lse_ref[0] = m_sc[...] + jnp.log(l_sc[...])

    kv = lambda off: pl.BlockSpec((tq, DH), lambda h, t, qi, kj: (kj[t], off + h))
    col = pl.BlockSpec((1, tq, 1), lambda h, t, qi, kj: (h, qi[t], 0))
    grid_spec = pltpu.PrefetchScalarGridSpec(
        num_scalar_prefetch=2, grid=(HEADS, qi_tab.shape[0]),
        in_specs=[pl.BlockSpec((tq, DH), lambda h, t, qi, kj: (qi[t], _Q_OFF + h)), kv(_K_OFF), kv(_V_OFF), col,
                  pl.BlockSpec((1, 1, tq), lambda h, t, qi, kj: (h, 0, kj[t]))],
        out_specs=[pl.BlockSpec((tq, DH), lambda h, t, qi, kj: (qi[t], h)), col],
        scratch_shapes=[pltpu.VMEM((tq, 1), F32), pltpu.VMEM((tq, 1), F32), pltpu.VMEM((tq, DH), F32)])
    return pl.pallas_call(
        body, name=name, grid_spec=grid_spec,
        out_shape=[jax.ShapeDtypeStruct((T, HEADS * DH), BF16), jax.ShapeDtypeStruct((HEADS, T, 1), F32)],
        compiler_params=_params("parallel", "arbitrary"),
    )(qi_tab, kj_tab, pm, pm, pm, c_col, c_row)


def _fox_delta(do, o, name):
    T, D = o.shape
    tr = _tile(T, ROW_TILE)

    def body(do_ref, o_ref, d_ref):
        prod = do_ref[...].astype(F32) * o_ref[...].astype(F32)
        for h in range(HEADS):
            d_ref[h] = _rowsum(prod[:, h * DH:(h + 1) * DH])

    row = pl.BlockSpec((tr, D), lambda i: (i, 0))
    return pl.pallas_call(
        body, name=name, grid=(T // tr,), in_specs=[row, row],
        out_specs=pl.BlockSpec((HEADS, tr, 1), lambda i: (0, i, 0)),
        out_shape=jax.ShapeDtypeStruct((HEADS, T, 1), F32), compiler_params=_params("parallel"),
    )(do, o)


def _fox_bwd(pm, c_col, c_row, do, lse, delta, name):
    T = pm.shape[0]
    tq = _tile(T, ATTN_TILE)
    nq = T // tq
    rg = min(ATTN_ROWS, tq)
    qi_tab, kj_tab = _causal_pairs(nq, by_key=True)
    npairs = qi_tab.shape[0]

    def body(qi_ref, kj_ref, q_ref, k_ref, v_ref, cq_ref, ck_ref, do_ref, lse_ref, dl_ref,
             dq_ref, dk_ref, dv_ref, dcq_ref, dck_ref, dq_sc, dk_sc, dv_sc, dck_sc):
        t = pl.program_id(1)
        i, j = qi_ref[t], kj_ref[t]

        @pl.when(t == 0)
        def _():
            dq_sc[...] = jnp.zeros_like(dq_sc)
            dcq_ref[...] = jnp.zeros_like(dcq_ref)

        @pl.when(i == j)
        def _():
            dk_sc[...] = jnp.zeros_like(dk_sc)
            dv_sc[...] = jnp.zeros_like(dv_sc)
            dck_sc[...] = jnp.zeros_like(dck_sc)

        base = pl.multiple_of(i * tq, tq)

        def step(diag):
            for r in range(tq // rg):
                rows = slice(r * rg, (r + 1) * rg)
                w = (r + 1) * rg if diag else tq
                qr, dor = q_ref[rows, :], do_ref[rows, :]
                s = _fox_logits(qr, k_ref[:w, :], cq_ref[0, rows, :], ck_ref[0, :, :w], r * rg, diag)
                p = jnp.exp(s - lse_ref[0, rows, :])
                dp = _dot(dor, v_ref[:w, :], "nt")
                ds = p * (dp - dl_ref[0, rows, :])
                dsb = ds.astype(BF16)
                dv_sc[:w, :] += _dot(p.astype(BF16), dor, "tn")
                dk_sc[:w, :] += _dot(dsb, qr, "tn")
                dck_sc[:, :w] -= _colsum(ds)
                tgt = pl.ds(base + r * rg, rg)
                dq_sc[tgt, :] += _dot(dsb, k_ref[:w, :])
                dcq_ref[0, tgt, :] += _rowsum(ds)

        @pl.when(i > j)
        def _():
            step(False)

        @pl.when(i == j)
        def _():
            step(True)

        @pl.when(i == nq - 1)
        def _():
            dk_ref[...] = dk_sc[...].astype(dk_ref.dtype)
            dv_ref[...] = dv_sc[...].astype(dv_ref.dtype)
            dck_ref[0] = dck_sc[...]

        @pl.when(t == npairs - 1)
        def _():
            dq_ref[...] = dq_sc[...].astype(dq_ref.dtype)

    col = pl.BlockSpec((1, tq, 1), lambda h, t, qi, kj: (h, qi[t], 0))
    kv = lambda off: pl.BlockSpec((tq, DH), lambda h, t, qi, kj: (kj[t], off + h))
    kv_out = pl.BlockSpec((tq, DH), lambda h, t, qi, kj: (kj[t], h))
    key_row = pl.BlockSpec((1, 1, tq), lambda h, t, qi, kj: (h, 0, kj[t]))
    grid_spec = pltpu.PrefetchScalarGridSpec(
        num_scalar_prefetch=2, grid=(HEADS, npairs),
        in_specs=[pl.BlockSpec((tq, DH), lambda h, t, qi, kj: (qi[t], _Q_OFF + h)), kv(_K_OFF), kv(_V_OFF), col,
                  key_row, pl.BlockSpec((tq, DH), lambda h, t, qi, kj: (qi[t], h)), col, col],
        out_specs=[pl.BlockSpec((T, DH), lambda h, t, qi, kj: (0, h)), kv_out, kv_out,
                   pl.BlockSpec((1, T, 1), lambda h, t, qi, kj: (h, 0, 0)), key_row],
        scratch_shapes=[pltpu.VMEM((T, DH), F32), pltpu.VMEM((tq, DH), F32), pltpu.VMEM((tq, DH), F32),
                        pltpu.VMEM((1, tq), F32)])
    D = HEADS * DH
    return pl.pallas_call(
        body, name=name, grid_spec=grid_spec,
        out_shape=[jax.ShapeDtypeStruct((T, D), BF16)] * 3
        + [jax.ShapeDtypeStruct((HEADS, T, 1), F32), jax.ShapeDtypeStruct((HEADS, 1, T), F32)],
        compiler_params=_params("parallel", "arbitrary"),
    )(qi_tab, kj_tab, pm, pm, pm, c_col, c_row, do, lse, delta)


def _xattn_fwd(q, kv, name):
    T, D = q.shape
    M = kv.shape[0]
    dh = D // MEM_HEADS
    tq = _tile(T, ATTN_TILE)
    scale = 1.0 / math.sqrt(dh)

    def body(q_ref, kv_ref, o_ref):
        for h in range(MEM_HEADS):
            cs = slice(h * dh, (h + 1) * dh)
            s = _dot(q_ref[:, cs], kv_ref[:, cs], "nt") * scale
            p = jnp.exp(s - jnp.max(s, axis=1, keepdims=True))
            p = p / _rowsum(p)
            o_ref[:, cs] = _dot(p.astype(BF16), kv_ref[:, D + h * dh:D + (h + 1) * dh]).astype(o_ref.dtype)

    row = pl.BlockSpec((tq, D), lambda i: (i, 0))
    return pl.pallas_call(
        body, name=name, grid=(T // tq,), in_specs=[row, pl.BlockSpec((M, 2 * D), lambda i: (0, 0))],
        out_specs=row, out_shape=jax.ShapeDtypeStruct((T, D), BF16), compiler_params=_params("parallel"),
    )(q, kv)


def _xattn_bwd(q, kv, do, name):
    T, D = q.shape
    M = kv.shape[0]
    dh = D // MEM_HEADS
    tq = _tile(T, ATTN_TILE)
    scale = 1.0 / math.sqrt(dh)

    def body(q_ref, kv_ref, do_ref, dq_ref, dkv_ref):
        @pl.when(pl.program_id(0) == 0)
        def _():
            dkv_ref[...] = jnp.zeros_like(dkv_ref)

        for h in range(MEM_HEADS):
            cs = slice(h * dh, (h + 1) * dh)
            vs = slice(D + h * dh, D + (h + 1) * dh)
            s = _dot(q_ref[:, cs], kv_ref[:, cs], "nt") * scale
            p = jnp.exp(s - jnp.max(s, axis=1, keepdims=True))
            p = p / _rowsum(p)
            dp = _dot(do_ref[:, cs], kv_ref[:, vs], "nt")
            ds = (p * (dp - _rowsum(p * dp)) * scale).astype(BF16)
            dq_ref[:, cs] = _dot(ds, kv_ref[:, cs]).astype(dq_ref.dtype)
            dkv_ref[:, cs] += _dot(ds, q_ref[:, cs], "tn")
            dkv_ref[:, vs] += _dot(p.astype(BF16), do_ref[:, cs], "tn")

    row = pl.BlockSpec((tq, D), lambda i: (i, 0))
    full = pl.BlockSpec((M, 2 * D), lambda i: (0, 0))
    return pl.pallas_call(
        body, name=name, grid=(T // tq,), in_specs=[row, full, row], out_specs=[row, full],
        out_shape=[jax.ShapeDtypeStruct((T, D), BF16), jax.ShapeDtypeStruct((M, 2 * D), F32)],
        compiler_params=_params("arbitrary"),
    )(q, kv, do)


_HBM = pl.BlockSpec(memory_space=pltpu.HBM)


def _position():
    return lax.axis_index("x"), lax.axis_index("y"), lax.axis_index("c")


def _other_chips(x, y):
    return [(1 - x, y), (x, 1 - y), (1 - x, 1 - y)]


def _ag_chips(blk, name):
    R, C = blk.shape

    def body(x_ref, a_ref, send, recv, local):
        x, y, c = _position()
        q = 2 * x + y
        mine = pltpu.make_async_copy(x_ref, a_ref.at[c, q], local)
        mine.start()
        sends = []
        for j, (px, py) in enumerate(_other_chips(x, y)):
            cp = pltpu.make_async_remote_copy(src_ref=x_ref, dst_ref=a_ref.at[c, q], send_sem=send.at[j],
                                              recv_sem=recv.at[j], device_id=(px, py, c), device_id_type=MESH)
            cp.start()
            sends.append(cp)
        for j, (px, py) in enumerate(_other_chips(x, y)):
            pltpu.make_async_remote_copy(src_ref=x_ref, dst_ref=a_ref.at[c, 2 * px + py], send_sem=send.at[j],
                                         recv_sem=recv.at[j], device_id=(px, py, c), device_id_type=MESH).wait_recv()
        for cp in sends:
            cp.wait_send()
        mine.wait()

    return pl.pallas_call(
        body, name=name, in_specs=[_HBM], out_specs=_HBM, out_shape=jax.ShapeDtypeStruct((2, 4, R, C), blk.dtype),
        scratch_shapes=[pltpu.SemaphoreType.DMA((3,)), pltpu.SemaphoreType.DMA((3,)), pltpu.SemaphoreType.DMA],
    )(blk)


def _ag_sibling(a, name):
    def body(a_in, a_ref, send, recv):
        del a_in
        x, y, c = _position()
        cp = pltpu.make_async_remote_copy(src_ref=a_ref.at[c], dst_ref=a_ref.at[c], send_sem=send, recv_sem=recv,
                                          device_id=(x, y, 1 - c), device_id_type=MESH)
        cp.start()
        pltpu.make_async_remote_copy(src_ref=a_ref.at[c], dst_ref=a_ref.at[1 - c], send_sem=send, recv_sem=recv,
                                     device_id=(x, y, 1 - c), device_id_type=MESH).wait_recv()
        cp.wait_send()

    return pl.pallas_call(
        body, name=name, in_specs=[_HBM], out_specs=_HBM, out_shape=jax.ShapeDtypeStruct(a.shape, a.dtype),
        input_output_aliases={0: 0},
        scratch_shapes=[pltpu.SemaphoreType.DMA, pltpu.SemaphoreType.DMA],
    )(a)


def _rs_sibling(g_send, name):
    def body(g_ref, l_ref, send, recv):
        x, y, c = _position()
        cp = pltpu.make_async_remote_copy(src_ref=g_ref, dst_ref=l_ref, send_sem=send, recv_sem=recv,
                                          device_id=(x, y, 1 - c), device_id_type=MESH)
        cp.start()
        cp.wait()

    return pl.pallas_call(
        body, name=name, in_specs=[_HBM], out_specs=_HBM, out_shape=jax.ShapeDtypeStruct(g_send.shape, g_send.dtype),
        scratch_shapes=[pltpu.SemaphoreType.DMA, pltpu.SemaphoreType.DMA],
    )(g_send)


def _rs_chips(h, name):
    def body(h_ref, l_ref, send, recv, local):
        x, y, c = _position()
        q = 2 * x + y
        mine = pltpu.make_async_copy(h_ref.at[q], l_ref.at[q], local)
        mine.start()
        sends = []
        for j, (px, py) in enumerate(_other_chips(x, y)):
            cp = pltpu.make_async_remote_copy(src_ref=h_ref.at[2 * px + py], dst_ref=l_ref.at[q], send_sem=send.at[j],
                                              recv_sem=recv.at[j], device_id=(px, py, c), device_id_type=MESH)
            cp.start()
            sends.append(cp)
        for j, (px, py) in enumerate(_other_chips(x, y)):
            pltpu.make_async_remote_copy(src_ref=h_ref.at[q], dst_ref=l_ref.at[2 * px + py], send_sem=send.at[j],
                                         recv_sem=recv.at[j], device_id=(px, py, c), device_id_type=MESH).wait_recv()
        for cp in sends:
            cp.wait_send()
        mine.wait()

    return pl.pallas_call(
        body, name=name, in_specs=[_HBM], out_specs=_HBM, out_shape=jax.ShapeDtypeStruct(h.shape, h.dtype),
        scratch_shapes=[pltpu.SemaphoreType.DMA((3,)), pltpu.SemaphoreType.DMA((3,)), pltpu.SemaphoreType.DMA],
    )(h)


def _pair_add(a, b, out_dtype, name):
    n, R, C = a.shape
    tr = _tile(R, PACK_ROW_TILE)

    def body(a_ref, b_ref, o_ref):
        o_ref[...] = (a_ref[...] + b_ref[...]).astype(o_ref.dtype)

    blk = pl.BlockSpec((1, tr, C), lambda p, i: (p, i, 0))
    return pl.pallas_call(
        body, name=name, grid=(n, R // tr), in_specs=[blk, blk], out_specs=blk,
        out_shape=jax.ShapeDtypeStruct(a.shape, out_dtype), compiler_params=_params("parallel", "parallel"),
    )(a, b)


def _adamw_math(w, g, m, v):
    m = ADAM_B1 * m + (1.0 - ADAM_B1) * g
    v = ADAM_B2 * v + (1.0 - ADAM_B2) * (g * g)
    m_hat = m / (1.0 - ADAM_B1 ** ADAM_STEP)
    v_hat = v / (1.0 - ADAM_B2 ** ADAM_STEP)
    delta = -ADAM_LR * (m_hat / (jnp.sqrt(v_hat) + ADAM_EPS) + ADAM_WD * w)
    return delta, m, v


def _adamw_reduce(slots, w, m, v, name):
    n, R, C = slots.shape
    tr = _tile(R, PACK_ROW_TILE)

    def body(s_ref, w_ref, m_ref, v_ref, g_ref, d_ref, nm_ref, nv_ref):
        g = s_ref[0].astype(F32)
        for p in range(1, n):
            g = g + s_ref[p].astype(F32)
        g_ref[...] = g
        d_ref[...], nm_ref[...], nv_ref[...] = _adamw_math(w_ref[...], g, m_ref[...], v_ref[...])

    row = pl.BlockSpec((tr, C), lambda i: (i, 0))
    return pl.pallas_call(
        body, name=name, grid=(R // tr,), in_specs=[pl.BlockSpec((n, tr, C), lambda i: (0, i, 0)), row, row, row],
        out_specs=[row] * 4, out_shape=[jax.ShapeDtypeStruct((R, C), F32)] * 4, compiler_params=_params("parallel"),
    )(slots, w, m, v)


def _pad_rows(a, rows):
    return jnp.pad(a, ((0, rows - a.shape[0]), (0, 0)))


def _pack_plan(shards):
    plan, r = {}, 0
    for n in BIG:
        rows = shards[n].size // PACK_COLS
        plan[n] = (r, rows)
        r += rows
    total = -(-r // PACK_ROW_TILE) * PACK_ROW_TILE
    return plan, total


def _pack_shards(shards, total, dtype):
    flat = jnp.concatenate([shards[n].reshape(-1, PACK_COLS).astype(dtype) for n in BIG], axis=0)
    return _pad_rows(flat, total)


def _full_from_gathered(a, plan, shards):
    full = {}
    for n in BIG:
        r0, rows = plan[n]
        blk = a[:, :, r0:r0 + rows, :].transpose(1, 0, 2, 3).reshape(8, rows, PACK_COLS)
        s0, s1 = shards[n].shape
        if n in COL_SHARDED:
            full[n] = blk.reshape(8, s0, s1).transpose(1, 0, 2).reshape(s0, 8 * s1)
        else:
            full[n] = blk.reshape(8 * s0, s1)
    return full


def _blocks_from_full(g, n, shards):
    s0, s1 = shards[n].shape
    if n in COL_SHARDED:
        blk = g.reshape(s0, 8, s1).transpose(1, 0, 2)
    else:
        blk = g.reshape(8, s0, s1)
    return blk.reshape(4, 2, -1, PACK_COLS)


def _swiglu_interleave(w):
    d, f2 = w.shape
    return w.reshape(d, 2, f2 // (2 * SWIGLU_TILE), SWIGLU_TILE).transpose(0, 2, 1, 3).reshape(d, f2)


def _swiglu_deinterleave(w):
    d, f2 = w.shape
    return w.reshape(d, f2 // (2 * SWIGLU_TILE), 2, SWIGLU_TILE).transpose(0, 2, 1, 3).reshape(d, f2)


SMALL_ROWS = 16


def _pack_small(vals, loss_row):
    rows = []
    for n in SMALL:
        flat = vals[n].reshape(-1)
        pad = (-flat.shape[0]) % PACK_COLS
        rows.append(jnp.pad(flat, (0, pad)).reshape(-1, PACK_COLS))
    rows.append(loss_row)
    out = jnp.concatenate(rows, axis=0)
    assert out.shape[0] == SMALL_ROWS, out.shape
    return out


def _unpack_small(packed, like):
    out, r = {}, 0
    for n in SMALL:
        size = like[n].size
        rows = -(-size // PACK_COLS)
        out[n] = packed[r:r + rows].reshape(-1)[:size].reshape(like[n].shape)
        r += rows
    return out


def _ffn_fwd(x, g_pre, w_in, w_down, tag):
    h = _rms_fwd(x, g_pre, f"{tag}_pre")
    u = _mm(h, w_in, "nn", BF16, f"{tag}_up")
    a = _swiglu_fwd(u, f"{tag}_act")
    z = _mm(a, w_down, "nn", F32, f"{tag}_down", tk=1408)
    return h, u, a, z


def _ffn_bwd(saved, x, g_pre, w_in, w_down, g_post, dx_out, tag):
    h, u, a, z = saved
    dz, dg_post = _rms_bwd(z, g_post, dx_out, 0.5, f"{tag}_post_bwd", BF16)
    da = _mm(dz, w_down, "nt", BF16, f"{tag}_down_dx", tn=1408)
    dw_down = _mm(a, dz, "tn", F32, f"{tag}_down_dw", tm=1408, tn=1024)
    du = _swiglu_bwd(u, da, f"{tag}_act_bwd")
    dh = _mm(du, w_in, "nt", F32, f"{tag}_up_dx", tn=1024, tk=512)
    dw_in = _mm(h, du, "tn", F32, f"{tag}_up_dw")
    dx, dg_pre = _rms_bwd(x, g_pre, dh, 1.0, f"{tag}_pre_bwd", F32, resid=dx_out)
    return dx, dg_pre, dg_post, dw_in, dw_down


def _step_local(x, mem, target, W, S):
    T, D = x.shape
    gW, gS = {}, {}

    f1 = _ffn_fwd(x, S["ffn1_pre_g"], W["ffn1_w_in"], W["ffn1_w_down"], "ffn1")
    x1 = _resid_rms(x, f1[3], S["ffn1_post_g"], 0.5, "ffn1_post")

    h2 = _rms_fwd(x1, S["mix_pre_g"], "mix_pre")
    pm = _mm(h2, W["w_main"], "nn", BF16, "mix_proj_main")
    pf = _mm(h2, W["w_f"], "nn", F32, "mix_proj_f")
    pg = _mm(h2, W["w_gates"], "nn", F32, "mix_proj_gates")
    lbl = S["hg_lb_logits"].reshape(2, HEADS, 1, DH)
    o_a, states = _hgrn_fwd(pm, lbl, "hgrn_fwd")
    oan = _hgout_fwd(o_a, pm, S["hg_norm_g"], "hgrn_out")
    bias = jnp.pad(S["fox_f_bias"], ((0, 0), (0, LANES - HEADS)))
    c = _fox_cumsum(pf, bias, "fox_cumsum")
    c_heads = c[:, :HEADS].T
    c_col, c_row = c_heads[:, :, None], c_heads[:, None, :]
    o_b, lse = _fox_fwd(pm, c_col, c_row, "fox_fwd")
    ya = _mm(oan, W["w_branch_a"], "nn", F32, "branch_a")
    yb = _mm(o_b, W["w_branch_b"], "nn", F32, "branch_b")
    y = _merge_fwd(ya, yb, pg, S["b_gate"], "merge")
    z2 = _mm(y, W["w_out"], "nn", F32, "mix_out")
    x2 = _resid_rms(x1, z2, S["mix_post_g"], 1.0, "mix_post")

    h3 = _rms_fwd(x2, S["mem_pre_g"], "mem_pre")
    memn = _rms_fwd(mem, S["mem_kv_g"], "mem_kv_norm")
    qm = _mm(h3, W["w_mq"], "nn", BF16, "mem_q")
    kv = _mm(memn, W["w_mkv"], "nn", BF16, "mem_kv")
    om = _xattn_fwd(qm, kv, "mem_attn")
    z3 = _mm(om, W["w_mo"], "nn", F32, "mem_o")
    x3 = _resid_rms(x2, z3, S["mem_post_g"], 1.0, "mem_post")

    f2 = _ffn_fwd(x3, S["ffn2_pre_g"], W["ffn2_w_in"], W["ffn2_w_down"], "ffn2")
    dx4, sq = _final_loss(x3, f2[3], S["ffn2_post_g"], 0.5, target, "loss")

    dx3, gS["ffn2_pre_g"], gS["ffn2_post_g"], gW["ffn2_w_in"], gW["ffn2_w_down"] = _ffn_bwd(
        f2, x3, S["ffn2_pre_g"], W["ffn2_w_in"], W["ffn2_w_down"], S["ffn2_post_g"], dx4, "ffn2")

    dz3, gS["mem_post_g"] = _rms_bwd(z3, S["mem_post_g"], dx3, 1.0, "mem_post_bwd", BF16)
    dom = _mm(dz3, W["w_mo"], "nt", BF16, "mem_o_dx")
    gW["w_mo"] = _mm(om, dz3, "tn", F32, "mem_o_dw")
    dqm, dkv = _xattn_bwd(qm, kv, dom, "mem_attn_bwd")
    dh3 = _mm(dqm, W["w_mq"], "nt", F32, "mem_q_dx")
    gW["w_mq"] = _mm(h3, dqm, "tn", F32, "mem_q_dw")
    dkvb = dkv.astype(BF16)
    gW["w_mkv"] = _mm(memn, dkvb, "tn", F32, "mem_kv_dw")
    dmemn = _mm(dkvb, W["w_mkv"], "nt", F32, "mem_kv_dx")
    _, gS["mem_kv_g"] = _rms_bwd(mem, S["mem_kv_g"], dmemn, 1.0, "mem_kv_norm_bwd", BF16)
    dx2, gS["mem_pre_g"] = _rms_bwd(x2, S["mem_pre_g"], dh3, 1.0, "mem_pre_bwd", F32, resid=dx3)

    dz2, gS["mix_post_g"] = _rms_bwd(z2, S["mix_post_g"], dx2, 1.0, "mix_post_bwd", BF16)
    dy = _mm(dz2, W["w_out"], "nt", F32, "mix_out_dx")
    gW["w_out"] = _mm(y, dz2, "tn", F32, "mix_out_dw")
    dya, dyb, dpg, gS["b_gate"] = _merge_bwd(dy, ya, yb, pg, S["b_gate"], "merge_bwd")
    doan = _mm(dya, W["w_branch_a"], "nt", F32, "branch_a_dx")
    gW["w_branch_a"] = _mm(oan, dya, "tn", F32, "branch_a_dw")
    dob = _mm(dyb, W["w_branch_b"], "nt", BF16, "branch_b_dx")
    gW["w_branch_b"] = _mm(o_b, dyb, "tn", F32, "branch_b_dw")

    delta = _fox_delta(dob, o_b, "fox_delta")
    dq_b, dk_b, dv_b, dc_col, dc_row = _fox_bwd(pm, c_col, c_row, dob, lse, delta, "fox_bwd")
    dc = jnp.pad((dc_col.reshape(HEADS, T) + dc_row.reshape(HEADS, T)).T, ((0, 0), (0, LANES - HEADS)))
    dpf, dbias = _fox_dcum(dc, pf, bias, "fox_cumsum_bwd")
    gS["fox_f_bias"] = dbias[:, :HEADS]

    do_a, dg_a, gS["hg_norm_g"] = _hgout_bwd(o_a, pm, S["hg_norm_g"], doan, "hgrn_out_bwd")
    dq_a, df_a, di_a, dlbl = _hgrn_bwd(pm, lbl, states, do_a, "hgrn_bwd")
    gS["hg_lb_logits"] = dlbl.reshape(2, HEADS, DH)

    dpm = jnp.concatenate([dq_a, df_a, di_a, dg_a, dq_b, dk_b, dv_b], axis=1)
    dpf16 = dpf.astype(BF16)
    dh2 = _mm(dpm, W["w_main"], "nt", F32, "mix_proj_main_dx", tn=1024, tk=512)
    dh2 = _mm(dpg, W["w_gates"], "nt", F32, "mix_proj_gates_dx", add=dh2, tn=1024, tk=512)
    dh2 = _mm(dpf16, W["w_f"], "nt", F32, "mix_proj_f_dx", add=dh2, tn=1024)
    gW["w_main"] = _mm(h2, dpm, "tn", F32, "mix_proj_main_dw")
    gW["w_gates"] = _mm(h2, dpg, "tn", F32, "mix_proj_gates_dw")
    gW["w_f"] = _mm(h2, dpf16, "tn", F32, "mix_proj_f_dw")
    dx1, gS["mix_pre_g"] = _rms_bwd(x1, S["mix_pre_g"], dh2, 1.0, "mix_pre_bwd", F32, resid=dx2)

    dx0, gS["ffn1_pre_g"], gS["ffn1_post_g"], gW["ffn1_w_in"], gW["ffn1_w_down"] = _ffn_bwd(
        f1, x, S["ffn1_pre_g"], W["ffn1_w_in"], W["ffn1_w_down"], S["ffn1_post_g"], dx1, "ffn1")
    return sq, dx0, gW, gS


def _train_step(a):
    c_idx = lax.axis_index("c")
    x, mem, target = a["x"][0], a["mem"][0], a["loss_target"][0]
    D = x.shape[1]
    shards = {n: a[n][0] for n in BIG}
    plan, total = _pack_plan(shards)

    fox_scale = 1.0 / math.sqrt(DH)
    n_mine = shards["w_in"].shape[1]
    dev = 4 * lax.axis_index("x") + 2 * lax.axis_index("y") + c_idx
    cols = dev * n_mine + jnp.arange(n_mine)
    is_fox_q = (cols >= 4 * D) & (cols < 5 * D)
    sent = dict(shards, w_in=shards["w_in"] * jnp.where(is_fox_q, fox_scale, 1.0)[None, :])
    gathered = _ag_sibling(_ag_chips(_pack_shards(sent, total, BF16), "ag_chips"), "ag_sibling")
    W = _full_from_gathered(gathered, plan, shards)
    n_main = 7 * D
    w_in = W.pop("w_in")
    W["w_main"] = w_in[:, :n_main]
    W["w_f"] = jnp.pad(w_in[:, n_main:n_main + HEADS], ((0, 0), (0, LANES - HEADS)))
    W["w_gates"] = w_in[:, n_main + HEADS:]
    W["ffn1_w_in"] = _swiglu_interleave(W["ffn1_w_in"])
    W["ffn2_w_in"] = _swiglu_interleave(W["ffn2_w_in"])
    S = {n: a[n] for n in SMALL}

    sq, grad_x, gW, gS = _step_local(x, mem, target, W, S)

    g_main = gW.pop("w_main")
    gW["w_in"] = jnp.concatenate([g_main[:, :4 * D], g_main[:, 4 * D:5 * D] * fox_scale, g_main[:, 5 * D:],
                                  gW.pop("w_f")[:, :HEADS], gW.pop("w_gates")], axis=1)
    gW["ffn1_w_in"] = _swiglu_deinterleave(gW["ffn1_w_in"])
    gW["ffn2_w_in"] = _swiglu_deinterleave(gW["ffn2_w_in"])
    blocks = [_blocks_from_full(gW[n], n, shards) for n in BIG]
    tail = [jnp.zeros((4, total - sum(b.shape[2] for b in blocks), PACK_COLS), F32)]

    def core_half(core):
        return jnp.concatenate([lax.dynamic_index_in_dim(b, core, axis=1, keepdims=False) for b in blocks] + tail,
                               axis=1)

    keep, send = core_half(c_idx), core_half(1 - c_idx)
    pair = _pair_add(keep, _rs_sibling(send, "rs_sibling"), BF16, "rs_pair_add")
    slots = _rs_chips(pair, "rs_chips")
    g_big, d_big, m_big, v_big = _adamw_reduce(
        slots, _pack_shards(shards, total, F32), _pack_shards({n: a["m_" + n][0] for n in BIG}, total, F32),
        _pack_shards({n: a["v_" + n][0] for n in BIG}, total, F32), "adamw_big")

    loss_row = jnp.pad(sq[:1, :1] * (0.5 / D), ((0, 0), (0, PACK_COLS - 1)))
    small_all = _ag_sibling(_ag_chips(_pack_small(gS, loss_row), "small_ag_chips"), "small_ag_sibling")
    small_slots = small_all.transpose(1, 0, 2, 3).reshape(8, SMALL_ROWS, PACK_COLS)
    zero_row = jnp.zeros((1, PACK_COLS), F32)
    g_sm, d_sm, m_sm, v_sm = _adamw_reduce(
        small_slots, _pack_small({n: a[n] for n in SMALL}, zero_row),
        _pack_small({n: a["m_" + n] for n in SMALL}, zero_row),
        _pack_small({n: a["v_" + n] for n in SMALL}, zero_row), "adamw_small")

    def unpack(big, small):
        out = _unpack_small(small, {n: a[n] for n in SMALL})
        for n in BIG:
            r0, rows = plan[n]
            out[n] = big[r0:r0 + rows].reshape(a[n].shape)
        return [out[n] for n in WEIGHTS]

    loss = g_sm[SMALL_ROWS - 1, 0]
    return (loss, grad_x[None], *unpack(g_big, g_sm), *unpack(d_big, d_sm), *unpack(m_big, m_sm),
            *unpack(v_big, v_sm))


def kernel(x, mem, ffn1_pre_g, ffn1_w_in, ffn1_w_down, ffn1_post_g, mix_pre_g, w_in, hg_lb_logits, hg_norm_g, fox_f_bias, w_branch_a, w_branch_b, b_gate, w_out, mix_post_g, mem_pre_g, mem_kv_g, w_mq, w_mkv, w_mo, mem_post_g, ffn2_pre_g, ffn2_w_in, ffn2_w_down, ffn2_post_g, loss_target, m_ffn1_pre_g, m_ffn1_w_in, m_ffn1_w_down, m_ffn1_post_g, m_mix_pre_g, m_w_in, m_hg_lb_logits, m_hg_norm_g, m_fox_f_bias, m_w_branch_a, m_w_branch_b, m_b_gate, m_w_out, m_mix_post_g, m_mem_pre_g, m_mem_kv_g, m_w_mq, m_w_mkv, m_w_mo, m_mem_post_g, m_ffn2_pre_g, m_ffn2_w_in, m_ffn2_w_down, m_ffn2_post_g, v_ffn1_pre_g, v_ffn1_w_in, v_ffn1_w_down, v_ffn1_post_g, v_mix_pre_g, v_w_in, v_hg_lb_logits, v_hg_norm_g, v_fox_f_bias, v_w_branch_a, v_w_branch_b, v_b_gate, v_w_out, v_mix_post_g, v_mem_pre_g, v_mem_kv_g, v_w_mq, v_w_mkv, v_w_mo, v_mem_post_g, v_ffn2_pre_g, v_ffn2_w_in, v_ffn2_w_down, v_ffn2_post_g):
    return _train_step(dict(locals()))
```

```python
import functools
import math

import jax
import jax.numpy as jnp
from jax import lax
from jax.experimental import pallas as pl
from jax.experimental.pallas import tpu as pltpu

F32 = jnp.float32
BF16 = jnp.bfloat16
MESH = pl.DeviceIdType.MESH

EPS = 1e-6
HEADS = 8
DH = 128
MEM_HEADS = 4
CHUNK = 128
HALF = CHUNK // 2
SWIGLU_TILE = 256
LANES = 128
PACK_COLS = 1024
ROW_TILE = 512
SEQ_BLOCK = 512
ATTN_TILE = 1024
ATTN_ROWS = 256
EXP_CLAMP = 80.0
NEG_BIG = -1e30

ADAM_LR, ADAM_B1, ADAM_B2, ADAM_EPS, ADAM_WD, ADAM_STEP = 0.001, 0.9, 0.999, 1e-08, 0.01, 10

VMEM_LIMIT = 48 * 1024 * 1024

_DN = {
    "nn": (((1,), (0,)), ((), ())),
    "nt": (((1,), (1,)), ((), ())),
    "tn": (((0,), (0,)), ((), ())),
}

BIG = ["ffn1_w_in", "ffn1_w_down", "w_in", "w_branch_a", "w_branch_b", "w_out", "w_mq", "w_mkv", "w_mo",
       "ffn2_w_in", "ffn2_w_down"]
COL_SHARDED = {"ffn1_w_in", "w_in", "w_mkv", "ffn2_w_in"}
SMALL = ["ffn1_pre_g", "ffn1_post_g", "mix_pre_g", "hg_lb_logits", "hg_norm_g", "fox_f_bias", "b_gate",
         "mix_post_g", "mem_pre_g", "mem_kv_g", "mem_post_g", "ffn2_pre_g", "ffn2_post_g"]
WEIGHTS = ["ffn1_pre_g", "ffn1_w_in", "ffn1_w_down", "ffn1_post_g", "mix_pre_g", "w_in", "hg_lb_logits",
           "hg_norm_g", "fox_f_bias", "w_branch_a", "w_branch_b", "b_gate", "w_out", "mix_post_g", "mem_pre_g",
           "mem_kv_g", "w_mq", "w_mkv", "w_mo", "mem_post_g", "ffn2_pre_g", "ffn2_w_in", "ffn2_w_down",
           "ffn2_post_g"]


def _dot(a, b, mode="nn"):
    return lax.dot_general(a, b, _DN[mode], preferred_element_type=F32)


def _sig(x):
    return 1.0 / (1.0 + jnp.exp(-x))


def _params(*dims):
    return pltpu.CompilerParams(dimension_semantics=dims if dims else None, vmem_limit_bytes=VMEM_LIMIT)


def _tile(dim, pref):
    if dim <= pref:
        return dim
    t = (pref // LANES) * LANES
    while t >= LANES:
        if dim % t == 0:
            return t
        t -= LANES
    raise ValueError(f"no tile for {dim}")


def _colsum(x):
    return jnp.sum(x, axis=0, keepdims=True)


def _rowsum(x):
    return jnp.sum(x, axis=1, keepdims=True)


def _iota(shape, axis):
    return lax.broadcasted_iota(jnp.int32, shape, axis)


def _pick_row(x, r):
    return _colsum(jnp.where(_iota(x.shape, 0) == r, x, 0.0))


def _tri_dot(tri, x):
    hi = x.astype(BF16)
    r1 = x - hi.astype(F32)
    mid = r1.astype(BF16)
    lo = (r1 - mid.astype(F32)).astype(BF16)
    return _dot(tri, hi) + _dot(tri, mid) + _dot(tri, lo)


def _mm(a, b, mode, out_dtype, name, add=None, tm=1024, tn=512, tk=1024):
    if mode == "nn":
        (M, K), (K2, N) = a.shape, b.shape
    elif mode == "nt":
        (M, K), (N, K2) = a.shape, b.shape
    else:
        (K, M), (K2, N) = a.shape, b.shape
    assert K == K2, (name, a.shape, b.shape)
    tm, tn, tk = _tile(M, tm), _tile(N, tn), _tile(K, tk)
    nk = K // tk
    if mode == "tn":
        a_spec = pl.BlockSpec((tk, tm), lambda i, j, k: (k, i))
    else:
        a_spec = pl.BlockSpec((tm, tk), lambda i, j, k: (i, k))
    if mode == "nt":
        b_spec = pl.BlockSpec((tn, tk), lambda i, j, k: (j, k))
    else:
        b_spec = pl.BlockSpec((tk, tn), lambda i, j, k: (k, j))
    o_spec = pl.BlockSpec((tm, tn), lambda i, j, k: (i, j))
    has_add = add is not None

    def body(*refs):
        a_ref, b_ref = refs[0], refs[1]
        c_ref = refs[2] if has_add else None
        o_ref = refs[3] if has_add else refs[2]
        part = _dot(a_ref[...], b_ref[...], mode)
        if nk == 1:
            if has_add:
                part = part + c_ref[...]
            o_ref[...] = part.astype(o_ref.dtype)
            return
        acc_ref = refs[-1]
        k = pl.program_id(2)

        @pl.when(k == 0)
        def _():
            acc_ref[...] = part + c_ref[...] if has_add else part

        @pl.when(k > 0)
        def _():
            acc_ref[...] += part

        @pl.when(k == nk - 1)
        def _():
            o_ref[...] = acc_ref[...].astype(o_ref.dtype)

    in_specs = [a_spec, b_spec] + ([o_spec] if has_add else [])
    args = (a, b) + ((add,) if has_add else ())
    return pl.pallas_call(
        body, name=name, grid=(M // tm, N // tn, nk), in_specs=in_specs, out_specs=o_spec,
        out_shape=jax.ShapeDtypeStruct((M, N), out_dtype),
        scratch_shapes=[pltpu.VMEM((tm, tn), F32)] if nk > 1 else [],
        compiler_params=_params("parallel", "parallel", "arbitrary"),
    )(*args)


def _rms_fwd(x, g, name, out_dtype=BF16):
    T, D = x.shape
    tr = _tile(T, ROW_TILE)

    def body(x_ref, g_ref, o_ref):
        xv = x_ref[...]
        r = lax.rsqrt(jnp.mean(xv * xv, axis=-1, keepdims=True) + EPS)
        o_ref[...] = (xv * r * g_ref[...]).astype(o_ref.dtype)

    return pl.pallas_call(
        body, name=name, grid=(T // tr,),
        in_specs=[pl.BlockSpec((tr, D), lambda i: (i, 0)), pl.BlockSpec((1, D), lambda i: (0, 0))],
        out_specs=pl.BlockSpec((tr, D), lambda i: (i, 0)),
        out_shape=jax.ShapeDtypeStruct((T, D), out_dtype), compiler_params=_params("parallel"),
    )(x, g)


def _resid_rms(x, z, g, scale, name):
    T, D = x.shape
    tr = _tile(T, ROW_TILE)

    def body(x_ref, z_ref, g_ref, o_ref):
        zv = z_ref[...]
        r = lax.rsqrt(jnp.mean(zv * zv, axis=-1, keepdims=True) + EPS)
        o_ref[...] = x_ref[...] + scale * (zv * r * g_ref[...])

    row = pl.BlockSpec((tr, D), lambda i: (i, 0))
    return pl.pallas_call(
        body, name=name, grid=(T // tr,), in_specs=[row, row, pl.BlockSpec((1, D), lambda i: (0, 0))],
        out_specs=row, out_shape=jax.ShapeDtypeStruct((T, D), F32), compiler_params=_params("parallel"),
    )(x, z, g)


def _final_loss(x, z, g, scale, target, name):
    T, D = x.shape
    tr = _tile(T, ROW_TILE)

    def body(x_ref, z_ref, g_ref, t_ref, dx_ref, acc_ref):
        @pl.when(pl.program_id(0) == 0)
        def _():
            acc_ref[...] = jnp.zeros_like(acc_ref)

        zv = z_ref[...]
        r = lax.rsqrt(jnp.mean(zv * zv, axis=-1, keepdims=True) + EPS)
        e = x_ref[...] + scale * (zv * r * g_ref[...]) - t_ref[...]
        dx_ref[...] = e * (1.0 / D)
        acc_ref[...] += _colsum(_rowsum(e * e))

    row = pl.BlockSpec((tr, D), lambda i: (i, 0))
    return pl.pallas_call(
        body, name=name, grid=(T // tr,), in_specs=[row, row, pl.BlockSpec((1, D), lambda i: (0, 0)), row],
        out_specs=[row, pl.BlockSpec((8, LANES), lambda i: (0, 0))],
        out_shape=[jax.ShapeDtypeStruct((T, D), F32), jax.ShapeDtypeStruct((8, LANES), F32)],
        compiler_params=_params("arbitrary"),
    )(x, z, g, target)


def _rms_bwd(xin, g, dy, scale, name, out_dtype, resid=None):
    T, D = xin.shape
    tr = _tile(T, ROW_TILE)
    has_resid = resid is not None

    def body(*refs):
        x_ref, g_ref, dy_ref = refs[:3]
        r_ref = refs[3] if has_resid else None
        dx_ref, dg_ref = refs[-2], refs[-1]

        @pl.when(pl.program_id(0) == 0)
        def _():
            dg_ref[...] = jnp.zeros_like(dg_ref)

        xv = x_ref[...]
        r = lax.rsqrt(jnp.mean(xv * xv, axis=-1, keepdims=True) + EPS)
        xh = xv * r
        dyv = dy_ref[...].astype(F32) * scale
        dxh = dyv * g_ref[...]
        dx = r * (dxh - xh * jnp.mean(dxh * xh, axis=-1, keepdims=True))
        if has_resid:
            dx = dx + r_ref[...]
        dx_ref[...] = dx.astype(dx_ref.dtype)
        dg_ref[...] += _colsum(dyv * xh)

    row = pl.BlockSpec((tr, D), lambda i: (i, 0))
    vec = pl.BlockSpec((1, D), lambda i: (0, 0))
    return pl.pallas_call(
        body, name=name, grid=(T // tr,), in_specs=[row, vec, row] + ([row] if has_resid else []),
        out_specs=[row, vec],
        out_shape=[jax.ShapeDtypeStruct((T, D), out_dtype), jax.ShapeDtypeStruct((1, D), F32)],
        compiler_params=_params("arbitrary"),
    )(*((xin, g, dy) + ((resid,) if has_resid else ())))


def _swiglu_fwd(u, name):
    T, F2 = u.shape
    tf = SWIGLU_TILE
    tr = _tile(T, 1024)

    def body(u_ref, o_ref):
        gate = u_ref[:, :tf].astype(F32)
        up = u_ref[:, tf:].astype(F32)
        o_ref[...] = (gate * _sig(gate) * up).astype(o_ref.dtype)

    return pl.pallas_call(
        body, name=name, grid=(T // tr, F2 // (2 * tf)),
        in_specs=[pl.BlockSpec((tr, 2 * tf), lambda i, j: (i, j))],
        out_specs=pl.BlockSpec((tr, tf), lambda i, j: (i, j)),
        out_shape=jax.ShapeDtypeStruct((T, F2 // 2), BF16), compiler_params=_params("parallel", "parallel"),
    )(u)


def _swiglu_bwd(u, da, name):
    T, F2 = u.shape
    tf = SWIGLU_TILE
    tr = _tile(T, 1024)

    def body(u_ref, da_ref, o_ref):
        gate = u_ref[:, :tf].astype(F32)
        up = u_ref[:, tf:].astype(F32)
        d = da_ref[...].astype(F32)
        s = _sig(gate)
        o_ref[:, :tf] = (d * up * (s * (1.0 + gate * (1.0 - s)))).astype(o_ref.dtype)
        o_ref[:, tf:] = (d * gate * s).astype(o_ref.dtype)

    return pl.pallas_call(
        body, name=name, grid=(T // tr, F2 // (2 * tf)),
        in_specs=[pl.BlockSpec((tr, 2 * tf), lambda i, j: (i, j)), pl.BlockSpec((tr, tf), lambda i, j: (i, j))],
        out_specs=pl.BlockSpec((tr, 2 * tf), lambda i, j: (i, j)),
        out_shape=jax.ShapeDtypeStruct((T, F2), BF16), compiler_params=_params("parallel", "parallel"),
    )(u, da)


def _hgout_fwd(o_a, pm, g, name):
    T, D = o_a.shape
    tr = _tile(T, ROW_TILE)

    def body(o_ref, ga_ref, g_ref, out_ref):
        ov = o_ref[...]
        r = lax.rsqrt(jnp.mean(ov * ov, axis=-1, keepdims=True) + EPS)
        ga = ga_ref[...].astype(F32)
        out_ref[...] = (ov * r * g_ref[...] * (ga * _sig(ga))).astype(out_ref.dtype)

    row = pl.BlockSpec((tr, D), lambda i: (i, 0))
    return pl.pallas_call(
        body, name=name, grid=(T // tr,),
        in_specs=[row, pl.BlockSpec((tr, D), lambda i: (i, 3)), pl.BlockSpec((1, D), lambda i: (0, 0))],
        out_specs=row, out_shape=jax.ShapeDtypeStruct((T, D), BF16), compiler_params=_params("parallel"),
    )(o_a, pm, g)


def _hgout_bwd(o_a, pm, g, d_out, name):
    T, D = o_a.shape
    tr = _tile(T, ROW_TILE)

    def body(o_ref, ga_ref, g_ref, d_ref, do_ref, dga_ref, dg_ref):
        @pl.when(pl.program_id(0) == 0)
        def _():
            dg_ref[...] = jnp.zeros_like(dg_ref)

        ov = o_ref[...]
        r = lax.rsqrt(jnp.mean(ov * ov, axis=-1, keepdims=True) + EPS)
        oh = ov * r
        ga = ga_ref[...].astype(F32)
        s = _sig(ga)
        d = d_ref[...].astype(F32)
        dn = d * (ga * s)
        dga_ref[...] = (d * (oh * g_ref[...]) * (s * (1.0 + ga * (1.0 - s)))).astype(dga_ref.dtype)
        dxh = dn * g_ref[...]
        do_ref[...] = (r * (dxh - oh * jnp.mean(dxh * oh, axis=-1, keepdims=True))).astype(do_ref.dtype)
        dg_ref[...] += _colsum(dn * oh)

    row = pl.BlockSpec((tr, D), lambda i: (i, 0))
    vec = pl.BlockSpec((1, D), lambda i: (0, 0))
    return pl.pallas_call(
        body, name=name, grid=(T // tr,), in_specs=[row, pl.BlockSpec((tr, D), lambda i: (i, 3)), vec, row],
        out_specs=[row, row, vec],
        out_shape=[jax.ShapeDtypeStruct((T, D), BF16), jax.ShapeDtypeStruct((T, D), BF16),
                   jax.ShapeDtypeStruct((1, D), F32)],
        compiler_params=_params("arbitrary"),
    )(o_a, pm, g, d_out)


def _merge_fwd(ya, yb, pg, bg, name):
    T, D = ya.shape
    tr = _tile(T, 256)

    def body(ya_ref, yb_ref, pg_ref, bg_ref, o_ref):
        g0 = _sig(pg_ref[:, :D] + bg_ref[:, :D])
        g1 = _sig(pg_ref[:, D:] + bg_ref[:, D:])
        o_ref[...] = (g0 * ya_ref[...] + g1 * yb_ref[...]).astype(o_ref.dtype)

    row = pl.BlockSpec((tr, D), lambda i: (i, 0))
    return pl.pallas_call(
        body, name=name, grid=(T // tr,),
        in_specs=[row, row, pl.BlockSpec((tr, 2 * D), lambda i: (i, 0)), pl.BlockSpec((1, 2 * D), lambda i: (0, 0))],
        out_specs=row, out_shape=jax.ShapeDtypeStruct((T, D), BF16), compiler_params=_params("parallel"),
    )(ya, yb, pg, bg)


def _merge_bwd(dy, ya, yb, pg, bg, name):
    T, D = ya.shape
    tr = _tile(T, 256)

    def body(dy_ref, ya_ref, yb_ref, pg_ref, bg_ref, dya_ref, dyb_ref, dpg_ref, dbg_ref):
        @pl.when(pl.program_id(0) == 0)
        def _():
            dbg_ref[...] = jnp.zeros_like(dbg_ref)

        d = dy_ref[...]
        g0 = _sig(pg_ref[:, :D] + bg_ref[:, :D])
        g1 = _sig(pg_ref[:, D:] + bg_ref[:, D:])
        dya_ref[...] = (d * g0).astype(dya_ref.dtype)
        dyb_ref[...] = (d * g1).astype(dyb_ref.dtype)
        dg0 = d * ya_ref[...] * (g0 * (1.0 - g0))
        dg1 = d * yb_ref[...] * (g1 * (1.0 - g1))
        dpg_ref[:, :D] = dg0.astype(dpg_ref.dtype)
        dpg_ref[:, D:] = dg1.astype(dpg_ref.dtype)
        dbg_ref[:, :D] += _colsum(dg0)
        dbg_ref[:, D:] += _colsum(dg1)

    row = pl.BlockSpec((tr, D), lambda i: (i, 0))
    wide = pl.BlockSpec((tr, 2 * D), lambda i: (i, 0))
    wvec = pl.BlockSpec((1, 2 * D), lambda i: (0, 0))
    return pl.pallas_call(
        body, name=name, grid=(T // tr,), in_specs=[row, row, row, wide, wvec],
        out_specs=[row, row, wide, wvec],
        out_shape=[jax.ShapeDtypeStruct((T, D), BF16), jax.ShapeDtypeStruct((T, D), BF16),
                   jax.ShapeDtypeStruct((T, 2 * D), BF16), jax.ShapeDtypeStruct((1, 2 * D), F32)],
        compiler_params=_params("arbitrary"),
    )(dy, ya, yb, pg, bg)


def _hgrn_chunk_terms(q, fl, lb, tri):
    shape = q.shape
    row = _iota(shape, 0)
    sg = _sig(fl)
    f = lb + (1.0 - lb) * sg
    k = 1.0 - f
    b = _tri_dot(tri, jnp.log(f))
    ref1 = jnp.where(row < HALF, _pick_row(b, HALF // 2), _pick_row(b, HALF + HALF // 2))
    b_half = _pick_row(b, HALF - 1)
    b_last = _pick_row(b, CHUNK - 1)
    sq = _sig(q)
    qs = q * sq
    e_q1 = jnp.exp(jnp.minimum(b - ref1, EXP_CLAMP))
    e_k1 = jnp.exp(jnp.minimum(ref1 - b, EXP_CLAMP))
    e_q2 = jnp.exp(jnp.minimum(b - b_half, 0.0))
    e_k2 = jnp.exp(jnp.minimum(b_half - b, 0.0))
    e_b = jnp.exp(b)
    e_kd = jnp.exp(b_last - b)
    return dict(sg=sg, f=f, k=k, sq=sq, qs=qs, e_q1=e_q1, e_k1=e_k1, e_q2=e_q2, e_k2=e_k2, e_b=e_b, e_kd=e_kd,
                e_last=jnp.exp(b_last))


def _hgrn_masks():
    r = _iota((CHUNK, CHUNK), 0)
    c = _iota((CHUNK, CHUNK), 1)
    causal = r >= c
    same = (r < HALF) == (c < HALF)
    return causal, causal & same, (r >= HALF) & (c < HALF)


def _softmax_lb(lbl_ref):
    l0, l1 = lbl_ref[0, 0], lbl_ref[1, 0]
    mx = jnp.maximum(l0, l1)
    e0, e1 = jnp.exp(l0 - mx), jnp.exp(l1 - mx)
    return e0 / (e0 + e1)


def _hgrn_fwd(pm, lbl, name):
    T = pm.shape[0]
    tb = _tile(T, SEQ_BLOCK)
    nc = tb // CHUNK

    def body(q_ref, f_ref, i_ref, lbl_ref, o_ref, st_ref, s_sc):
        @pl.when(pl.program_id(1) == 0)
        def _():
            s_sc[...] = jnp.zeros_like(s_sc)

        lb = _softmax_lb(lbl_ref)
        causal, m1, m2 = _hgrn_masks()
        tri = jnp.where(causal, 1.0, 0.0).astype(BF16)
        for ci in range(nc):
            sl = pl.ds(ci * CHUNK, CHUNK)
            t = _hgrn_chunk_terms(q_ref[sl, :].astype(F32), f_ref[sl, :].astype(F32), lb, tri)
            iv = i_ref[sl, :]
            a1 = _dot((t["qs"] * t["e_q1"]).astype(BF16), (t["k"] * t["e_k1"]).astype(BF16), "nt")
            a2 = _dot((t["qs"] * t["e_q2"]).astype(BF16), (t["k"] * t["e_k2"]).astype(BF16), "nt")
            a = jnp.where(m1, a1, 0.0) + jnp.where(m2, a2, 0.0)
            st = s_sc[...]
            st_ref[0, ci] = st
            o_ref[sl, :] = _dot(a.astype(BF16), iv) + _dot((t["qs"] * t["e_b"]).astype(BF16), st.astype(BF16), "nt")
            s_sc[...] = t["e_last"] * st + _dot(iv, (t["k"] * t["e_kd"]).astype(BF16), "tn")

    blk = lambda off: pl.BlockSpec((tb, DH), lambda h, b: (b, off + h))
    return pl.pallas_call(
        body, name=name, grid=(HEADS, T // tb),
        in_specs=[blk(0), blk(HEADS), blk(2 * HEADS), pl.BlockSpec((2, 1, 1, DH), lambda h, b: (0, h, 0, 0))],
        out_specs=[pl.BlockSpec((tb, DH), lambda h, b: (b, h)),
                   pl.BlockSpec((1, nc, DH, DH), lambda h, b: (h, b, 0, 0))],
        out_shape=[jax.ShapeDtypeStruct((T, HEADS * DH), F32),
                   jax.ShapeDtypeStruct((HEADS, T // CHUNK, DH, DH), F32)],
        scratch_shapes=[pltpu.VMEM((DH, DH), F32)],
        compiler_params=_params("parallel", "arbitrary"),
    )(pm, pm, pm, lbl)


def _hgrn_bwd(pm, lbl, states, do, name):
    T = pm.shape[0]
    tb = _tile(T, SEQ_BLOCK)
    nc = tb // CHUNK
    nb = T // tb

    def body(q_ref, f_ref, i_ref, lbl_ref, st_ref, do_ref, dq_ref, df_ref, di_ref, dl_ref, ds_sc, dlb_sc):
        @pl.when(pl.program_id(1) == 0)
        def _():
            ds_sc[...] = jnp.zeros_like(ds_sc)
            dlb_sc[...] = jnp.zeros_like(dlb_sc)

        lb = _softmax_lb(lbl_ref)
        causal, m1, m2 = _hgrn_masks()
        tri = jnp.where(causal, 1.0, 0.0).astype(BF16)
        tri_rev = jnp.where(_iota((CHUNK, CHUNK), 0) <= _iota((CHUNK, CHUNK), 1), 1.0, 0.0).astype(BF16)
        last_row = _iota((CHUNK, DH), 0) == CHUNK - 1
        for ci in reversed(range(nc)):
            sl = pl.ds(ci * CHUNK, CHUNK)
            q = q_ref[sl, :].astype(F32)
            t = _hgrn_chunk_terms(q, f_ref[sl, :].astype(F32), lb, tri)
            iv = i_ref[sl, :]
            dov = do_ref[sl, :]
            qe1, ke1 = t["qs"] * t["e_q1"], t["k"] * t["e_k1"]
            qe2, ke2 = t["qs"] * t["e_q2"], t["k"] * t["e_k2"]
            qi, kd = t["qs"] * t["e_b"], t["k"] * t["e_kd"]
            qe1b, ke1b, qe2b, ke2b = qe1.astype(BF16), ke1.astype(BF16), qe2.astype(BF16), ke2.astype(BF16)
            a = jnp.where(m1, _dot(qe1b, ke1b, "nt"), 0.0) + jnp.where(m2, _dot(qe2b, ke2b, "nt"), 0.0)
            st = st_ref[0, ci]
            dsn = ds_sc[...]
            dsnb = dsn.astype(BF16)
            da = _dot(dov, iv, "nt")
            da1 = jnp.where(m1, da, 0.0).astype(BF16)
            da2 = jnp.where(m2, da, 0.0).astype(BF16)
            di_ref[sl, :] = (_dot(a.astype(BF16), dov, "tn") + _dot(kd.astype(BF16), dsnb, "nt")).astype(di_ref.dtype)
            dqe1, dke1 = _dot(da1, ke1b), _dot(da1, qe1b, "tn")
            dqe2, dke2 = _dot(da2, ke2b), _dot(da2, qe2b, "tn")
            dqi = _dot(dov, st.astype(BF16))
            dkd = _dot(iv, dsnb)
            ds_sc[...] = t["e_last"] * dsn + _dot(dov, qi.astype(BF16), "tn")
            dqs = dqe1 * t["e_q1"] + dqe2 * t["e_q2"] + dqi * t["e_b"]
            dk = dke1 * t["e_k1"] + dke2 * t["e_k2"] + dkd * t["e_kd"]
            qib, kdb = qi.astype(BF16).astype(F32), kd.astype(BF16).astype(F32)
            db = (dqe1 * qe1b.astype(F32) - dke1 * ke1b.astype(F32) + dqe2 * qe2b.astype(F32)
                  - dke2 * ke2b.astype(F32) + dqi * qib - dkd * kdb)
            extra = _colsum(dkd * kdb) + t["e_last"] * _colsum(dsn * st)
            db = db + jnp.where(last_row, extra, 0.0)
            dlf = _tri_dot(tri_rev, db)
            dfv = dlf / t["f"] - dk
            sg = t["sg"]
            df_ref[sl, :] = (dfv * (1.0 - lb) * sg * (1.0 - sg)).astype(df_ref.dtype)
            dlb_sc[...] += _colsum(dfv * (1.0 - sg))
            sq = t["sq"]
            dq_ref[sl, :] = (dqs * (sq * (1.0 + q * (1.0 - sq)))).astype(dq_ref.dtype)

        @pl.when(pl.program_id(1) == nb - 1)
        def _():
            dl0 = dlb_sc[...] * lb * (1.0 - lb)
            dl_ref[0, 0] = dl0
            dl_ref[1, 0] = -dl0

    blk = lambda off: pl.BlockSpec((tb, DH), lambda h, b: (nb - 1 - b, off + h))
    lspec = pl.BlockSpec((2, 1, 1, DH), lambda h, b: (0, h, 0, 0))
    out_blk = pl.BlockSpec((tb, DH), lambda h, b: (nb - 1 - b, h))
    D = HEADS * DH
    return pl.pallas_call(
        body, name=name, grid=(HEADS, nb),
        in_specs=[blk(0), blk(HEADS), blk(2 * HEADS), lspec,
                  pl.BlockSpec((1, nc, DH, DH), lambda h, b: (h, nb - 1 - b, 0, 0)), out_blk],
        out_specs=[out_blk, out_blk, out_blk, lspec],
        out_shape=[jax.ShapeDtypeStruct((T, D), BF16)] * 3 + [jax.ShapeDtypeStruct((2, HEADS, 1, DH), F32)],
        scratch_shapes=[pltpu.VMEM((DH, DH), F32), pltpu.VMEM((1, DH), F32)],
        compiler_params=_params("parallel", "arbitrary"),
    )(pm, pm, pm, lbl, states, do)


def _log_sigmoid(x):
    return jnp.minimum(x, 0.0) - jnp.log(1.0 + jnp.exp(-jnp.abs(x)))


def _fox_cumsum(pf, bias, name):
    T = pf.shape[0]
    tb = _tile(T, SEQ_BLOCK)

    def body(x_ref, b_ref, c_ref, carry):
        @pl.when(pl.program_id(0) == 0)
        def _():
            carry[...] = jnp.zeros_like(carry)

        tri = jnp.where(_iota((tb, tb), 0) >= _iota((tb, tb), 1), 1.0, 0.0).astype(BF16)
        c = _tri_dot(tri, _log_sigmoid(x_ref[...] + b_ref[...])) + carry[...]
        c_ref[...] = c
        carry[...] = _pick_row(c, tb - 1)

    row = pl.BlockSpec((tb, LANES), lambda i: (i, 0))
    return pl.pallas_call(
        body, name=name, grid=(T // tb,), in_specs=[row, pl.BlockSpec((1, LANES), lambda i: (0, 0))],
        out_specs=row, out_shape=jax.ShapeDtypeStruct((T, LANES), F32),
        scratch_shapes=[pltpu.VMEM((1, LANES), F32)], compiler_params=_params("arbitrary"),
    )(pf, bias)


def _fox_dcum(dc, pf, bias, name):
    T = pf.shape[0]
    tb = _tile(T, SEQ_BLOCK)
    nb = T // tb

    def body(dc_ref, x_ref, b_ref, dx_ref, db_ref, carry):
        @pl.when(pl.program_id(0) == 0)
        def _():
            carry[...] = jnp.zeros_like(carry)
            db_ref[...] = jnp.zeros_like(db_ref)

        tri_rev = jnp.where(_iota((tb, tb), 0) <= _iota((tb, tb), 1), 1.0, 0.0).astype(BF16)
        dls = _tri_dot(tri_rev, dc_ref[...]) + carry[...]
        carry[...] = _pick_row(dls, 0)
        dx = dls * (1.0 - _sig(x_ref[...] + b_ref[...]))
        dx_ref[...] = dx
        db_ref[...] += _colsum(dx)

    row = pl.BlockSpec((tb, LANES), lambda i: (nb - 1 - i, 0))
    vec = pl.BlockSpec((1, LANES), lambda i: (0, 0))
    return pl.pallas_call(
        body, name=name, grid=(nb,), in_specs=[row, row, vec], out_specs=[row, vec],
        out_shape=[jax.ShapeDtypeStruct((T, LANES), F32), jax.ShapeDtypeStruct((1, LANES), F32)],
        scratch_shapes=[pltpu.VMEM((1, LANES), F32)], compiler_params=_params("arbitrary"),
    )(dc, pf, bias)


_Q_OFF, _K_OFF, _V_OFF = 4 * HEADS, 5 * HEADS, 6 * HEADS


def _causal_pairs(nq, by_key):
    if by_key:
        pairs = [(i, j) for j in range(nq) for i in range(j, nq)]
    else:
        pairs = [(i, j) for i in range(nq) for j in range(i + 1)]
    return jnp.asarray([p[0] for p in pairs], jnp.int32), jnp.asarray([p[1] for p in pairs], jnp.int32)


def _fox_logits(q, k, cq, ck, row0, masked):
    s = _dot(q, k, "nt") + (cq - ck)
    if masked:
        s = jnp.where(_iota(s.shape, 0) + row0 >= _iota(s.shape, 1), s, NEG_BIG)
    return s


def _fox_fwd(pm, c_col, c_row, name):
    T = pm.shape[0]
    tq = _tile(T, ATTN_TILE)
    nq = T // tq
    rg = min(ATTN_ROWS, tq)
    qi_tab, kj_tab = _causal_pairs(nq, by_key=False)

    def body(qi_ref, kj_ref, q_ref, k_ref, v_ref, cq_ref, ck_ref, o_ref, lse_ref, m_sc, l_sc, acc_sc):
        t = pl.program_id(1)
        i, j = qi_ref[t], kj_ref[t]

        @pl.when(j == 0)
        def _():
            m_sc[...] = jnp.full_like(m_sc, NEG_BIG)
            l_sc[...] = jnp.zeros_like(l_sc)
            acc_sc[...] = jnp.zeros_like(acc_sc)

        def step(diag):
            m_all, l_all, acc_all = m_sc[...], l_sc[...], acc_sc[...]
            ms, ls, accs = [], [], []
            for r in range(tq // rg):
                rows = slice(r * rg, (r + 1) * rg)
                w = (r + 1) * rg if diag else tq
                s = _fox_logits(q_ref[rows, :], k_ref[:w, :], cq_ref[0, rows, :], ck_ref[0, :, :w], r * rg, diag)
                m_old = m_all[rows, :]
                m_new = jnp.maximum(m_old, jnp.max(s, axis=1, keepdims=True))
                alpha = jnp.exp(m_old - m_new)
                p = jnp.exp(s - m_new)
                ms.append(m_new)
                ls.append(alpha * l_all[rows, :] + _rowsum(p))
                accs.append(alpha * acc_all[rows, :] + _dot(p.astype(BF16), v_ref[:w, :]))
            m_sc[...] = jnp.concatenate(ms, axis=0)
            l_sc[...] = jnp.concatenate(ls, axis=0)
            acc_sc[...] = jnp.concatenate(accs, axis=0)

        @pl.when(j < i)
        def _():
            step(False)

        @pl.when(j == i)
        def _():
            step(True)
            o_ref[...] = (acc_sc[...] / l_sc[...]).astype(o_ref.dtype)
            lse_ref[0] = m_sc[...] + jnp.log(l_sc[...])

    kv = lambda off: pl.BlockSpec((tq, DH), lambda h, t, qi, kj: (kj[t], off + h))
    col = pl.BlockSpec((1, tq, 1), lambda h, t, qi, kj: (h, qi[t], 0))
    grid_spec = pltpu.PrefetchScalarGridSpec(
        num_scalar_prefetch=2, grid=(HEADS, qi_tab.shape[0]),
        in_specs=[pl.BlockSpec((tq, DH), lambda h, t, qi, kj: (qi[t], _Q_OFF + h)), kv(_K_OFF), kv(_V_OFF), col,
                  pl.BlockSpec((1, 1, tq), lambda h, t, qi, kj: (h, 0, kj[t]))],
        out_specs=[pl.BlockSpec((tq, DH), lambda h, t, qi, kj: (qi[t], h)), col],
        scratch_shapes=[pltpu.VMEM((tq, 1), F32), pltpu.VMEM((tq, 1), F32), pltpu.VMEM((tq, DH), F32)])
    return pl.pallas_call(
        body, name=name, grid_spec=grid_spec,
        out_shape=[jax.ShapeDtypeStruct((T, HEADS * DH), BF16), jax.ShapeDtypeStruct((HEADS, T, 1), F32)],
        compiler_params=_params("parallel", "arbitrary"),
    )(qi_tab, kj_tab, pm, pm, pm, c_col, c_row)


def _fox_delta(do, o, name):
    T, D = o.shape
    tr = _tile(T, ROW_TILE)

    def body(do_ref, o_ref, d_ref):
        prod = do_ref[...].astype(F32) * o_ref[...].astype(F32)
        for h in range(HEADS):
            d_ref[h] = _rowsum(prod[:, h * DH:(h + 1) * DH])

    row = pl.BlockSpec((tr, D), lambda i: (i, 0))
    return pl.pallas_call(
        body, name=name, grid=(T // tr,), in_specs=[row, row],
        out_specs=pl.BlockSpec((HEADS, tr, 1), lambda i: (0, i, 0)),
        out_shape=jax.ShapeDtypeStruct((HEADS, T, 1), F32), compiler_params=_params("parallel"),
    )(do, o)


def _fox_bwd(pm, c_col, c_row, do, lse, delta, name):
    T = pm.shape[0]
    tq = _tile(T, ATTN_TILE)
    nq = T // tq
    rg = min(ATTN_ROWS, tq)
    qi_tab, kj_tab = _causal_pairs(nq, by_key=True)
    npairs = qi_tab.shape[0]

    def body(qi_ref, kj_ref, q_ref, k_ref, v_ref, cq_ref, ck_ref, do_ref, lse_ref, dl_ref,
             dq_ref, dk_ref, dv_ref, dcq_ref, dck_ref, dq_sc, dk_sc, dv_sc, dck_sc):
        t = pl.program_id(1)
        i, j = qi_ref[t], kj_ref[t]

        @pl.when(t == 0)
        def _():
            dq_sc[...] = jnp.zeros_like(dq_sc)
            dcq_ref[...] = jnp.zeros_like(dcq_ref)

        @pl.when(i == j)
        def _():
            dk_sc[...] = jnp.zeros_like(dk_sc)
            dv_sc[...] = jnp.zeros_like(dv_sc)
            dck_sc[...] = jnp.zeros_like(dck_sc)

        base = pl.multiple_of(i * tq, tq)

        def step(diag):
            for r in range(tq // rg):
                rows = slice(r * rg, (r + 1) * rg)
                w = (r + 1) * rg if diag else tq
                qr, dor = q_ref[rows, :], do_ref[rows, :]
                s = _fox_logits(qr, k_ref[:w, :], cq_ref[0, rows, :], ck_ref[0, :, :w], r * rg, diag)
                p = jnp.exp(s - lse_ref[0, rows, :])
                dp = _dot(dor, v_ref[:w, :], "nt")
                ds = p * (dp - dl_ref[0, rows, :])
                dsb = ds.astype(BF16)
                dv_sc[:w, :] += _dot(p.astype(BF16), dor, "tn")
                dk_sc[:w, :] += _dot(dsb, qr, "tn")
                dck_sc[:, :w] -= _colsum(ds)
                tgt = pl.ds(base + r * rg, rg)
                dq_sc[tgt, :] += _dot(dsb, k_ref[:w, :])
                dcq_ref[0, tgt, :] += _rowsum(ds)

        @pl.when(i > j)
        def _():
            step(False)

        @pl.when(i == j)
        def _():
            step(True)

        @pl.when(i == nq - 1)
        def _():
            dk_ref[...] = dk_sc[...].astype(dk_ref.dtype)
            dv_ref[...] = dv_sc[...].astype(dv_ref.dtype)
            dck_ref[0] = dck_sc[...]

        @pl.when(t == npairs - 1)
        def _():
            dq_ref[...] = dq_sc[...].astype(dq_ref.dtype)

    col = pl.BlockSpec((1, tq, 1), lambda h, t, qi, kj: (h, qi[t], 0))
    kv = lambda off: pl.BlockSpec((tq, DH), lambda h, t, qi, kj: (kj[t], off + h))
    kv_out = pl.BlockSpec((tq, DH), lambda h, t, qi, kj: (kj[t], h))
    key_row = pl.BlockSpec((1, 1, tq), lambda h, t, qi, kj: (h, 0, kj[t]))
    grid_spec = pltpu.PrefetchScalarGridSpec(
        num_scalar_prefetch=2, grid=(HEADS, npairs),
        in_specs=[pl.BlockSpec((tq, DH), lambda h, t, qi, kj: (qi[t], _Q_OFF + h)), kv(_K_OFF), kv(_V_OFF), col,
                  key_row, pl.BlockSpec((tq, DH), lambda h, t, qi, kj: (qi[t], h)), col, col],
        out_specs=[pl.BlockSpec((T, DH), lambda h, t, qi, kj: (0, h)), kv_out, kv_out,
                   pl.BlockSpec((1, T, 1), lambda h, t, qi, kj: (h, 0, 0)), key_row],
        scratch_shapes=[pltpu.VMEM((T, DH), F32), pltpu.VMEM((tq, DH), F32), pltpu.VMEM((tq, DH), F32),
                        pltpu.VMEM((1, tq), F32)])
    D = HEADS * DH
    return pl.pallas_call(
        body, name=name, grid_spec=grid_spec,
        out_shape=[jax.ShapeDtypeStruct((T, D), BF16)] * 3
        + [jax.ShapeDtypeStruct((HEADS, T, 1), F32), jax.ShapeDtypeStruct((HEADS, 1, T), F32)],
        compiler_params=_params("parallel", "arbitrary"),
    )(qi_tab, kj_tab, pm, pm, pm, c_col, c_row, do, lse, delta)


def _xattn_fwd(q, kv, name):
    T, D = q.shape
    M = kv.shape[0]
    dh = D // MEM_HEADS
    tq = _tile(T, ATTN_TILE)
    scale = 1.0 / math.sqrt(dh)

    def body(q_ref, kv_ref, o_ref):
        for h in range(MEM_HEADS):
            cs = slice(h * dh, (h + 1) * dh)
            s = _dot(q_ref[:, cs], kv_ref[:, cs], "nt") * scale
            p = jnp.exp(s - jnp.max(s, axis=1, keepdims=True))
            p = p / _rowsum(p)
            o_ref[:, cs] = _dot(p.astype(BF16), kv_ref[:, D + h * dh:D + (h + 1) * dh]).astype(o_ref.dtype)

    row = pl.BlockSpec((tq, D), lambda i: (i, 0))
    return pl.pallas_call(
        body, name=name, grid=(T // tq,), in_specs=[row, pl.BlockSpec((M, 2 * D), lambda i: (0, 0))],
        out_specs=row, out_shape=jax.ShapeDtypeStruct((T, D), BF16), compiler_params=_params("parallel"),
    )(q, kv)


def _xattn_bwd(q, kv, do, name):
    T, D = q.shape
    M = kv.shape[0]
    dh = D // MEM_HEADS
    tq = _tile(T, ATTN_TILE)
    scale = 1.0 / math.sqrt(dh)

    def body(q_ref, kv_ref, do_ref, dq_ref, dkv_ref):
        @pl.when(pl.program_id(0) == 0)
        def _():
            dkv_ref[...] = jnp.zeros_like(dkv_ref)

        for h in range(MEM_HEADS):
            cs = slice(h * dh, (h + 1) * dh)
            vs = slice(D + h * dh, D + (h + 1) * dh)
            s = _dot(q_ref[:, cs], kv_ref[:, cs], "nt") * scale
            p = jnp.exp(s - jnp.max(s, axis=1, keepdims=True))
            p = p / _rowsum(p)
            dp = _dot(do_ref[:, cs], kv_ref[:, vs], "nt")
            ds = (p * (dp - _rowsum(p * dp)) * scale).astype(BF16)
            dq_ref[:, cs] = _dot(ds, kv_ref[:, cs]).astype(dq_ref.dtype)
            dkv_ref[:, cs] += _dot(ds, q_ref[:, cs], "tn")
            dkv_ref[:, vs] += _dot(p.astype(BF16), do_ref[:, cs], "tn")

    row = pl.BlockSpec((tq, D), lambda i: (i, 0))
    full = pl.BlockSpec((M, 2 * D), lambda i: (0, 0))
    return pl.pallas_call(
        body, name=name, grid=(T // tq,), in_specs=[row, full, row], out_specs=[row, full],
        out_shape=[jax.ShapeDtypeStruct((T, D), BF16), jax.ShapeDtypeStruct((M, 2 * D), F32)],
        compiler_params=_params("arbitrary"),
    )(q, kv, do)


_HBM = pl.BlockSpec(memory_space=pltpu.HBM)


def _position():
    return lax.axis_index("x"), lax.axis_index("y"), lax.axis_index("c")


def _other_chips(x, y):
    return [(1 - x, y), (x, 1 - y), (1 - x, 1 - y)]


def _exchange_chips(srcs, outs, send, recv, local, src_of, dst_of):
    x, y, c = _position()
    q = 2 * x + y
    kept, sent = [], []
    for w, (s_ref, o_ref) in enumerate(zip(srcs, outs)):
        mine = pltpu.make_async_copy(src_of(s_ref, q), dst_of(o_ref, q), local.at[w])
        mine.start()
        kept.append(mine)
        for j, (px, py) in enumerate(_other_chips(x, y)):
            cp = pltpu.make_async_remote_copy(
                src_ref=src_of(s_ref, 2 * px + py), dst_ref=dst_of(o_ref, q), send_sem=send.at[3 * w + j],
                recv_sem=recv.at[3 * w + j], device_id=(px, py, c), device_id_type=MESH)
            cp.start()
            sent.append(cp)
    for w, (s_ref, o_ref) in enumerate(zip(srcs, outs)):
        for j, (px, py) in enumerate(_other_chips(x, y)):
            pltpu.make_async_remote_copy(
                src_ref=src_of(s_ref, q), dst_ref=dst_of(o_ref, 2 * px + py), send_sem=send.at[3 * w + j],
                recv_sem=recv.at[3 * w + j], device_id=(px, py, c), device_id_type=MESH).wait_recv()
    for cp in sent:
        cp.wait_send()
    for cp in kept:
        cp.wait()


def _chip_sems(n):
    return [pltpu.SemaphoreType.DMA((3 * n,)), pltpu.SemaphoreType.DMA((3 * n,)), pltpu.SemaphoreType.DMA((n,))]


def _ag_chips(blks, name):
    n = len(blks)

    def body(*refs):
        c = lax.axis_index("c")
        _exchange_chips(refs[:n], refs[n:2 * n], *refs[2 * n:], src_of=lambda r, chip: r,
                        dst_of=lambda r, chip: r.at[chip, c])

    return pl.pallas_call(
        body, name=name, in_specs=[_HBM] * n, out_specs=[_HBM] * n,
        out_shape=[jax.ShapeDtypeStruct((4, 2) + b.shape, b.dtype) for b in blks], scratch_shapes=_chip_sems(n),
    )(*blks)


def _ag_sibling(arrs, name):
    n = len(arrs)

    def body(*refs):
        outs, send, recv = refs[n:2 * n], refs[2 * n], refs[2 * n + 1]
        x, y, c = _position()
        cps = []
        for w, a_ref in enumerate(outs):
            cp = pltpu.make_async_remote_copy(src_ref=a_ref.at[:, c], dst_ref=a_ref.at[:, c], send_sem=send.at[w],
                                              recv_sem=recv.at[w], device_id=(x, y, 1 - c), device_id_type=MESH)
            cp.start()
            cps.append(cp)
        for w, a_ref in enumerate(outs):
            pltpu.make_async_remote_copy(src_ref=a_ref.at[:, c], dst_ref=a_ref.at[:, 1 - c], send_sem=send.at[w],
                                         recv_sem=recv.at[w], device_id=(x, y, 1 - c), device_id_type=MESH).wait_recv()
        for cp in cps:
            cp.wait_send()

    return pl.pallas_call(
        body, name=name, in_specs=[_HBM] * n, out_specs=[_HBM] * n,
        out_shape=[jax.ShapeDtypeStruct(a.shape, a.dtype) for a in arrs],
        input_output_aliases={i: i for i in range(n)},
        scratch_shapes=[pltpu.SemaphoreType.DMA((n,)), pltpu.SemaphoreType.DMA((n,))],
    )(*arrs)


def _rs_sibling(blocks, name):
    n = len(blocks)

    def body(*refs):
        srcs, outs, send, recv = refs[:n], refs[n:2 * n], refs[2 * n], refs[2 * n + 1]
        x, y, c = _position()
        cps = []
        for w, (b_ref, l_ref) in enumerate(zip(srcs, outs)):
            cp = pltpu.make_async_remote_copy(src_ref=b_ref.at[:, 1 - c], dst_ref=l_ref, send_sem=send.at[w],
                                              recv_sem=recv.at[w], device_id=(x, y, 1 - c), device_id_type=MESH)
            cp.start()
            cps.append(cp)
        for cp in cps:
            cp.wait()

    return pl.pallas_call(
        body, name=name, in_specs=[_HBM] * n, out_specs=[_HBM] * n,
        out_shape=[jax.ShapeDtypeStruct((4,) + b.shape[2:], b.dtype) for b in blocks],
        scratch_shapes=[pltpu.SemaphoreType.DMA((n,)), pltpu.SemaphoreType.DMA((n,))],
    )(*blocks)


def _rs_chips(parts, name):
    n = len(parts)

    def body(*refs):
        _exchange_chips(refs[:n], refs[n:2 * n], *refs[2 * n:], src_of=lambda r, chip: r.at[chip],
                        dst_of=lambda r, chip: r.at[chip])

    return pl.pallas_call(
        body, name=name, in_specs=[_HBM] * n, out_specs=[_HBM] * n,
        out_shape=[jax.ShapeDtypeStruct(h.shape, h.dtype) for h in parts], scratch_shapes=_chip_sems(n),
    )(*parts)


def _row_tile(rows, pref=256):
    for t in range(min(pref, rows) // 16 * 16, 0, -16):
        if rows % t == 0:
            return t
    raise ValueError(f"no row tile for {rows}")


def _pair_add(blocks, landed, core, out_dtype, name):
    n, _, s0, s1 = blocks.shape
    tr = _row_tile(s0)

    def body(core_ref, a_ref, b_ref, o_ref):
        del core_ref
        o_ref[...] = (a_ref[...] + b_ref[...]).astype(o_ref.dtype)

    grid_spec = pltpu.PrefetchScalarGridSpec(
        num_scalar_prefetch=1, grid=(n, s0 // tr),
        in_specs=[pl.BlockSpec((1, None, tr, s1), lambda p, i, core: (p, core[0], i, 0)),
                  pl.BlockSpec((1, tr, s1), lambda p, i, core: (p, i, 0))],
        out_specs=pl.BlockSpec((1, tr, s1), lambda p, i, core: (p, i, 0)))
    return pl.pallas_call(
        body, name=name, grid_spec=grid_spec, out_shape=jax.ShapeDtypeStruct(landed.shape, out_dtype),
        compiler_params=_params("parallel", "parallel"),
    )(core, blocks, landed)


def _adamw_math(w, g, m, v):
    m = ADAM_B1 * m + (1.0 - ADAM_B1) * g
    v = ADAM_B2 * v + (1.0 - ADAM_B2) * (g * g)
    m_hat = m / (1.0 - ADAM_B1 ** ADAM_STEP)
    v_hat = v / (1.0 - ADAM_B2 ** ADAM_STEP)
    delta = -ADAM_LR * (m_hat / (jnp.sqrt(v_hat) + ADAM_EPS) + ADAM_WD * w)
    return delta, m, v


def _adamw_reduce(slots, w, m, v, name):
    n, R, C = slots.shape
    tr = _row_tile(R)

    def body(s_ref, w_ref, m_ref, v_ref, g_ref, d_ref, nm_ref, nv_ref):
        g = s_ref[0].astype(F32)
        for p in range(1, n):
            g = g + s_ref[p].astype(F32)
        g_ref[...] = g
        d_ref[...], nm_ref[...], nv_ref[...] = _adamw_math(w_ref[...], g, m_ref[...], v_ref[...])

    row = pl.BlockSpec((tr, C), lambda i: (i, 0))
    return pl.pallas_call(
        body, name=name, grid=(R // tr,), in_specs=[pl.BlockSpec((n, tr, C), lambda i: (0, i, 0)), row, row, row],
        out_specs=[row] * 4, out_shape=[jax.ShapeDtypeStruct((R, C), F32)] * 4, compiler_params=_params("parallel"),
    )(slots, w, m, v)


def _full_from_gathered(a, n):
    s0, s1 = a.shape[2:]
    blk = a.reshape(8, s0, s1)
    if n in COL_SHARDED:
        return blk.transpose(1, 0, 2).reshape(s0, 8 * s1)
    return blk.reshape(8 * s0, s1)


def _blocks_from_full(g, n, shard_shape):
    s0, s1 = shard_shape
    if n in COL_SHARDED:
        blk = g.reshape(s0, 8, s1).transpose(1, 0, 2)
    else:
        blk = g.reshape(8, s0, s1)
    return blk.reshape(4, 2, s0, s1)


def _swiglu_interleave(w):
    d, f2 = w.shape
    return w.reshape(d, 2, f2 // (2 * SWIGLU_TILE), SWIGLU_TILE).transpose(0, 2, 1, 3).reshape(d, f2)


def _swiglu_deinterleave(w):
    d, f2 = w.shape
    return w.reshape(d, f2 // (2 * SWIGLU_TILE), 2, SWIGLU_TILE).transpose(0, 2, 1, 3).reshape(d, f2)


SMALL_ROWS = 16


def _pack_small(vals, loss_row):
    rows = []
    for n in SMALL:
        flat = vals[n].reshape(-1)
        pad = (-flat.shape[0]) % PACK_COLS
        rows.append(jnp.pad(flat, (0, pad)).reshape(-1, PACK_COLS))
    rows.append(loss_row)
    out = jnp.concatenate(rows, axis=0)
    assert out.shape[0] == SMALL_ROWS, out.shape
    return out


def _unpack_small(packed, like):
    out, r = {}, 0
    for n in SMALL:
        size = like[n].size
        rows = -(-size // PACK_COLS)
        out[n] = packed[r:r + rows].reshape(-1)[:size].reshape(like[n].shape)
        r += rows
    return out


def _ffn_fwd(x, g_pre, w_in, w_down, tag):
    h = _rms_fwd(x, g_pre, f"{tag}_pre")
    u = _mm(h, w_in, "nn", BF16, f"{tag}_up")
    a = _swiglu_fwd(u, f"{tag}_act")
    z = _mm(a, w_down, "nn", F32, f"{tag}_down", tk=1408)
    return h, u, a, z


def _ffn_bwd(saved, x, g_pre, w_in, w_down, g_post, dx_out, tag):
    h, u, a, z = saved
    dz, dg_post = _rms_bwd(z, g_post, dx_out, 0.5, f"{tag}_post_bwd", BF16)
    da = _mm(dz, w_down, "nt", BF16, f"{tag}_down_dx", tn=1408)
    dw_down = _mm(a, dz, "tn", F32, f"{tag}_down_dw", tm=1408, tn=1024)
    du = _swiglu_bwd(u, da, f"{tag}_act_bwd")
    dh = _mm(du, w_in, "nt", F32, f"{tag}_up_dx", tn=1024, tk=512)
    dw_in = _mm(h, du, "tn", F32, f"{tag}_up_dw")
    dx, dg_pre = _rms_bwd(x, g_pre, dh, 1.0, f"{tag}_pre_bwd", F32, resid=dx_out)
    return dx, dg_pre, dg_post, dw_in, dw_down


def _step_local(x, mem, target, W, S):
    T, D = x.shape
    gW, gS = {}, {}

    f1 = _ffn_fwd(x, S["ffn1_pre_g"], W["ffn1_w_in"], W["ffn1_w_down"], "ffn1")
    x1 = _resid_rms(x, f1[3], S["ffn1_post_g"], 0.5, "ffn1_post")

    h2 = _rms_fwd(x1, S["mix_pre_g"], "mix_pre")
    pm = _mm(h2, W["w_main"], "nn", BF16, "mix_proj_main")
    pf = _mm(h2, W["w_f"], "nn", F32, "mix_proj_f")
    pg = _mm(h2, W["w_gates"], "nn", F32, "mix_proj_gates")
    lbl = S["hg_lb_logits"].reshape(2, HEADS, 1, DH)
    o_a, states = _hgrn_fwd(pm, lbl, "hgrn_fwd")
    oan = _hgout_fwd(o_a, pm, S["hg_norm_g"], "hgrn_out")
    bias = jnp.pad(S["fox_f_bias"], ((0, 0), (0, LANES - HEADS)))
    c = _fox_cumsum(pf, bias, "fox_cumsum")
    c_heads = c[:, :HEADS].T
    c_col, c_row = c_heads[:, :, None], c_heads[:, None, :]
    o_b, lse = _fox_fwd(pm, c_col, c_row, "fox_fwd")
    ya = _mm(oan, W["w_branch_a"], "nn", F32, "branch_a")
    yb = _mm(o_b, W["w_branch_b"], "nn", F32, "branch_b")
    y = _merge_fwd(ya, yb, pg, S["b_gate"], "merge")
    z2 = _mm(y, W["w_out"], "nn", F32, "mix_out")
    x2 = _resid_rms(x1, z2, S["mix_post_g"], 1.0, "mix_post")

    h3 = _rms_fwd(x2, S["mem_pre_g"], "mem_pre")
    memn = _rms_fwd(mem, S["mem_kv_g"], "mem_kv_norm")
    qm = _mm(h3, W["w_mq"], "nn", BF16, "mem_q")
    kv = _mm(memn, W["w_mkv"], "nn", BF16, "mem_kv")
    om = _xattn_fwd(qm, kv, "mem_attn")
    z3 = _mm(om, W["w_mo"], "nn", F32, "mem_o")
    x3 = _resid_rms(x2, z3, S["mem_post_g"], 1.0, "mem_post")

    f2 = _ffn_fwd(x3, S["ffn2_pre_g"], W["ffn2_w_in"], W["ffn2_w_down"], "ffn2")
    dx4, sq = _final_loss(x3, f2[3], S["ffn2_post_g"], 0.5, target, "loss")

    dx3, gS["ffn2_pre_g"], gS["ffn2_post_g"], gW["ffn2_w_in"], gW["ffn2_w_down"] = _ffn_bwd(
        f2, x3, S["ffn2_pre_g"], W["ffn2_w_in"], W["ffn2_w_down"], S["ffn2_post_g"], dx4, "ffn2")

    dz3, gS["mem_post_g"] = _rms_bwd(z3, S["mem_post_g"], dx3, 1.0, "mem_post_bwd", BF16)
    dom = _mm(dz3, W["w_mo"], "nt", BF16, "mem_o_dx")
    gW["w_mo"] = _mm(om, dz3, "tn", F32, "mem_o_dw")
    dqm, dkv = _xattn_bwd(qm, kv, dom, "mem_attn_bwd")
    dh3 = _mm(dqm, W["w_mq"], "nt", F32, "mem_q_dx")
    gW["w_mq"] = _mm(h3, dqm, "tn", F32, "mem_q_dw")
    dkvb = dkv.astype(BF16)
    gW["w_mkv"] = _mm(memn, dkvb, "tn", F32, "mem_kv_dw")
    dmemn = _mm(dkvb, W["w_mkv"], "nt", F32, "mem_kv_dx")
    _, gS["mem_kv_g"] = _rms_bwd(mem, S["mem_kv_g"], dmemn, 1.0, "mem_kv_norm_bwd", BF16)
    dx2, gS["mem_pre_g"] = _rms_bwd(x2, S["mem_pre_g"], dh3, 1.0, "mem_pre_bwd", F32, resid=dx3)

    dz2, gS["mix_post_g"] = _rms_bwd(z2, S["mix_post_g"], dx2, 1.0, "mix_post_bwd", BF16)
    dy = _mm(dz2, W["w_out"], "nt", F32, "mix_out_dx")
    gW["w_out"] = _mm(y, dz2, "tn", F32, "mix_out_dw")
    dya, dyb, dpg, gS["b_gate"] = _merge_bwd(dy, ya, yb, pg, S["b_gate"], "merge_bwd")
    doan = _mm(dya, W["w_branch_a"], "nt", F32, "branch_a_dx")
    gW["w_branch_a"] = _mm(oan, dya, "tn", F32, "branch_a_dw")
    dob = _mm(dyb, W["w_branch_b"], "nt", BF16, "branch_b_dx")
    gW["w_branch_b"] = _mm(o_b, dyb, "tn", F32, "branch_b_dw")

    delta = _fox_delta(dob, o_b, "fox_delta")
    dq_b, dk_b, dv_b, dc_col, dc_row = _fox_bwd(pm, c_col, c_row, dob, lse, delta, "fox_bwd")
    dc = jnp.pad((dc_col.reshape(HEADS, T) + dc_row.reshape(HEADS, T)).T, ((0, 0), (0, LANES - HEADS)))
    dpf, dbias = _fox_dcum(dc, pf, bias, "fox_cumsum_bwd")
    gS["fox_f_bias"] = dbias[:, :HEADS]

    do_a, dg_a, gS["hg_norm_g"] = _hgout_bwd(o_a, pm, S["hg_norm_g"], doan, "hgrn_out_bwd")
    dq_a, df_a, di_a, dlbl = _hgrn_bwd(pm, lbl, states, do_a, "hgrn_bwd")
    gS["hg_lb_logits"] = dlbl.reshape(2, HEADS, DH)

    dpm = jnp.concatenate([dq_a, df_a, di_a, dg_a, dq_b, dk_b, dv_b], axis=1)
    dpf16 = dpf.astype(BF16)
    dh2 = _mm(dpm, W["w_main"], "nt", F32, "mix_proj_main_dx", tn=1024, tk=512)
    dh2 = _mm(dpg, W["w_gates"], "nt", F32, "mix_proj_gates_dx", add=dh2, tn=1024, tk=512)
    dh2 = _mm(dpf16, W["w_f"], "nt", F32, "mix_proj_f_dx", add=dh2, tn=1024)
    gW["w_main"] = _mm(h2, dpm, "tn", F32, "mix_proj_main_dw")
    gW["w_gates"] = _mm(h2, dpg, "tn", F32, "mix_proj_gates_dw")
    gW["w_f"] = _mm(h2, dpf16, "tn", F32, "mix_proj_f_dw")
    dx1, gS["mix_pre_g"] = _rms_bwd(x1, S["mix_pre_g"], dh2, 1.0, "mix_pre_bwd", F32, resid=dx2)

    dx0, gS["ffn1_pre_g"], gS["ffn1_post_g"], gW["ffn1_w_in"], gW["ffn1_w_down"] = _ffn_bwd(
        f1, x, S["ffn1_pre_g"], W["ffn1_w_in"], W["ffn1_w_down"], S["ffn1_post_g"], dx1, "ffn1")
    return sq, dx0, gW, gS


def _train_step(a):
    c_idx = lax.axis_index("c")
    x, mem, target = a["x"][0], a["mem"][0], a["loss_target"][0]
    D = x.shape[1]
    shards = {n: a[n][0] for n in BIG}

    fox_scale = 1.0 / math.sqrt(DH)
    n_mine = shards["w_in"].shape[1]
    dev = 4 * lax.axis_index("x") + 2 * lax.axis_index("y") + c_idx
    cols = dev * n_mine + jnp.arange(n_mine)
    is_fox_q = (cols >= 4 * D) & (cols < 5 * D)
    sent = dict(shards, w_in=shards["w_in"] * jnp.where(is_fox_q, fox_scale, 1.0)[None, :])
    gathered = _ag_sibling(_ag_chips([sent[n].astype(BF16) for n in BIG], "ag_chips"), "ag_sibling")
    W = {n: _full_from_gathered(g, n) for n, g in zip(BIG, gathered)}
    n_main = 7 * D
    w_in = W.pop("w_in")
    W["w_main"] = w_in[:, :n_main]
    W["w_f"] = jnp.pad(w_in[:, n_main:n_main + HEADS], ((0, 0), (0, LANES - HEADS)))
    W["w_gates"] = w_in[:, n_main + HEADS:]
    W["ffn1_w_in"] = _swiglu_interleave(W["ffn1_w_in"])
    W["ffn2_w_in"] = _swiglu_interleave(W["ffn2_w_in"])
    S = {n: a[n] for n in SMALL}

    sq, grad_x, gW, gS = _step_local(x, mem, target, W, S)

    g_main = gW.pop("w_main")
    gW["w_in"] = jnp.concatenate([g_main[:, :4 * D], g_main[:, 4 * D:5 * D] * fox_scale, g_main[:, 5 * D:],
                                  gW.pop("w_f")[:, :HEADS], gW.pop("w_gates")], axis=1)
    gW["ffn1_w_in"] = _swiglu_deinterleave(gW["ffn1_w_in"])
    gW["ffn2_w_in"] = _swiglu_deinterleave(gW["ffn2_w_in"])
    blocks = [_blocks_from_full(gW[n], n, shards[n].shape) for n in BIG]
    landed = _rs_sibling(blocks, "rs_sibling")
    core = c_idx.astype(jnp.int32).reshape(1)
    pairs = [_pair_add(b, l, core, BF16, f"rs_pair_add_{n}") for n, b, l in zip(BIG, blocks, landed)]
    slots = _rs_chips(pairs, "rs_chips")
    big = {n: _adamw_reduce(s, shards[n], a["m_" + n][0], a["v_" + n][0], f"adamw_{n}") for n, s in zip(BIG, slots)}

    loss_row = jnp.pad(sq[:1, :1] * (0.5 / D), ((0, 0), (0, PACK_COLS - 1)))
    small_all = _ag_sibling(_ag_chips([_pack_small(gS, loss_row)], "small_ag_chips"), "small_ag_sibling")[0]
    small_slots = small_all.reshape(8, SMALL_ROWS, PACK_COLS)
    zero_row = jnp.zeros((1, PACK_COLS), F32)
    g_sm, d_sm, m_sm, v_sm = _adamw_reduce(
        small_slots, _pack_small({n: a[n] for n in SMALL}, zero_row),
        _pack_small({n: a["m_" + n] for n in SMALL}, zero_row),
        _pack_small({n: a["v_" + n] for n in SMALL}, zero_row), "adamw_small")

    def unpack(which, small):
        out = _unpack_small(small, {n: a[n] for n in SMALL})
        for n in BIG:
            out[n] = big[n][which][None]
        return [out[n] for n in WEIGHTS]

    loss = g_sm[SMALL_ROWS - 1, 0]
    return (loss, grad_x[None], *unpack(0, g_sm), *unpack(1, d_sm), *unpack(2, m_sm), *unpack(3, v_sm))


def kernel(x, mem, ffn1_pre_g, ffn1_w_in, ffn1_w_down, ffn1_post_g, mix_pre_g, w_in, hg_lb_logits, hg_norm_g, fox_f_bias, w_branch_a, w_branch_b, b_gate, w_out, mix_post_g, mem_pre_g, mem_kv_g, w_mq, w_mkv, w_mo, mem_post_g, ffn2_pre_g, ffn2_w_in, ffn2_w_down, ffn2_post_g, loss_target, m_ffn1_pre_g, m_ffn1_w_in, m_ffn1_w_down, m_ffn1_post_g, m_mix_pre_g, m_w_in, m_hg_lb_logits, m_hg_norm_g, m_fox_f_bias, m_w_branch_a, m_w_branch_b, m_b_gate, m_w_out, m_mix_post_g, m_mem_pre_g, m_mem_kv_g, m_w_mq, m_w_mkv, m_w_mo, m_mem_post_g, m_ffn2_pre_g, m_ffn2_w_in, m_ffn2_w_down, m_ffn2_post_g, v_ffn1_pre_g, v_ffn1_w_in, v_ffn1_w_down, v_ffn1_post_g, v_mix_pre_g, v_w_in, v_hg_lb_logits, v_hg_norm_g, v_fox_f_bias, v_w_branch_a, v_w_branch_b, v_b_gate, v_w_out, v_mix_post_g, v_mem_pre_g, v_mem_kv_g, v_w_mq, v_w_mkv, v_w_mo, v_mem_post_g, v_ffn2_pre_g, v_ffn2_w_in, v_ffn2_w_down, v_ffn2_post_g):
    return _train_step(dict(locals()))
```

```python
import functools
import math

import jax
import jax.numpy as jnp
from jax import lax
from jax.experimental import pallas as pl
from jax.experimental.pallas import tpu as pltpu

F32 = jnp.float32
BF16 = jnp.bfloat16
MESH = pl.DeviceIdType.MESH

EPS = 1e-6
HEADS = 8
DH = 128
MEM_HEADS = 4
CHUNK = 128
HALF = CHUNK // 2
SWIGLU_TILE = 256
LANES = 128
PACK_COLS = 1024
ROW_TILE = 512
SEQ_BLOCK = 512
ATTN_TILE = 1024
ATTN_ROWS = 256
EXP_CLAMP = 80.0
NEG_BIG = -1e30

ADAM_LR, ADAM_B1, ADAM_B2, ADAM_EPS, ADAM_WD, ADAM_STEP = 0.001, 0.9, 0.999, 1e-08, 0.01, 10

VMEM_LIMIT = 48 * 1024 * 1024

_DN = {
    "nn": (((1,), (0,)), ((), ())),
    "nt": (((1,), (1,)), ((), ())),
    "tn": (((0,), (0,)), ((), ())),
}

BIG = ["ffn1_w_in", "ffn1_w_down", "w_in", "w_branch_a", "w_branch_b", "w_out", "w_mq", "w_mkv", "w_mo",
       "ffn2_w_in", "ffn2_w_down"]
COL_SHARDED = {"ffn1_w_in", "w_in", "w_mkv", "ffn2_w_in"}
SMALL = ["ffn1_pre_g", "ffn1_post_g", "mix_pre_g", "hg_lb_logits", "hg_norm_g", "fox_f_bias", "b_gate",
         "mix_post_g", "mem_pre_g", "mem_kv_g", "mem_post_g", "ffn2_pre_g", "ffn2_post_g"]
WEIGHTS = ["ffn1_pre_g", "ffn1_w_in", "ffn1_w_down", "ffn1_post_g", "mix_pre_g", "w_in", "hg_lb_logits",
           "hg_norm_g", "fox_f_bias", "w_branch_a", "w_branch_b", "b_gate", "w_out", "mix_post_g", "mem_pre_g",
           "mem_kv_g", "w_mq", "w_mkv", "w_mo", "mem_post_g", "ffn2_pre_g", "ffn2_w_in", "ffn2_w_down",
           "ffn2_post_g"]


def _dot(a, b, mode="nn"):
    return lax.dot_general(a, b, _DN[mode], preferred_element_type=F32)


def _sig(x):
    return 1.0 / (1.0 + jnp.exp(-x))


def _params(*dims):
    return pltpu.CompilerParams(dimension_semantics=dims if dims else None, vmem_limit_bytes=VMEM_LIMIT)


def _tile(dim, pref):
    if dim <= pref:
        return dim
    t = (pref // LANES) * LANES
    while t >= LANES:
        if dim % t == 0:
            return t
        t -= LANES
    raise ValueError(f"no tile for {dim}")


def _colsum(x):
    return jnp.sum(x, axis=0, keepdims=True)


def _rowsum(x):
    return jnp.sum(x, axis=1, keepdims=True)


def _iota(shape, axis):
    return lax.broadcasted_iota(jnp.int32, shape, axis)


def _pick_row(x, r):
    return _colsum(jnp.where(_iota(x.shape, 0) == r, x, 0.0))


def _tri_dot(tri, x):
    hi = x.astype(BF16)
    r1 = x - hi.astype(F32)
    mid = r1.astype(BF16)
    lo = (r1 - mid.astype(F32)).astype(BF16)
    return _dot(tri, hi) + _dot(tri, mid) + _dot(tri, lo)


_MM_TILES = {"nn": (2048, 512, 1024), "nt": (512, 1024, 4096), "tn": (1024, 1024, 2048)}


def _mm(a, b, mode, out_dtype, name, add=None, tm=None, tn=None, tk=None):
    tm, tn, tk = (given or pref for given, pref in zip((tm, tn, tk), _MM_TILES[mode]))
    if mode == "nn":
        (M, K), (K2, N) = a.shape, b.shape
    elif mode == "nt":
        (M, K), (N, K2) = a.shape, b.shape
    else:
        (K, M), (K2, N) = a.shape, b.shape
    assert K == K2, (name, a.shape, b.shape)
    tm, tn, tk = _tile(M, tm), _tile(N, tn), _tile(K, tk)
    nk = K // tk
    if mode == "tn":
        a_spec = pl.BlockSpec((tk, tm), lambda i, j, k: (k, i))
    else:
        a_spec = pl.BlockSpec((tm, tk), lambda i, j, k: (i, k))
    if mode == "nt":
        b_spec = pl.BlockSpec((tn, tk), lambda i, j, k: (j, k))
    else:
        b_spec = pl.BlockSpec((tk, tn), lambda i, j, k: (k, j))
    o_spec = pl.BlockSpec((tm, tn), lambda i, j, k: (i, j))
    has_add = add is not None

    def body(*refs):
        a_ref, b_ref = refs[0], refs[1]
        c_ref = refs[2] if has_add else None
        o_ref = refs[3] if has_add else refs[2]
        part = _dot(a_ref[...], b_ref[...], mode)
        if nk == 1:
            if has_add:
                part = part + c_ref[...]
            o_ref[...] = part.astype(o_ref.dtype)
            return
        acc_ref = refs[-1]
        k = pl.program_id(2)

        @pl.when(k == 0)
        def _():
            acc_ref[...] = part + c_ref[...] if has_add else part

        @pl.when(k > 0)
        def _():
            acc_ref[...] += part

        @pl.when(k == nk - 1)
        def _():
            o_ref[...] = acc_ref[...].astype(o_ref.dtype)

    in_specs = [a_spec, b_spec] + ([o_spec] if has_add else [])
    args = (a, b) + ((add,) if has_add else ())
    return pl.pallas_call(
        body, name=name, grid=(M // tm, N // tn, nk), in_specs=in_specs, out_specs=o_spec,
        out_shape=jax.ShapeDtypeStruct((M, N), out_dtype),
        scratch_shapes=[pltpu.VMEM((tm, tn), F32)] if nk > 1 else [],
        compiler_params=_params("parallel", "parallel", "arbitrary"),
    )(*args)


def _rms_fwd(x, g, name, out_dtype=BF16):
    T, D = x.shape
    tr = _tile(T, ROW_TILE)

    def body(x_ref, g_ref, o_ref):
        xv = x_ref[...]
        r = lax.rsqrt(jnp.mean(xv * xv, axis=-1, keepdims=True) + EPS)
        o_ref[...] = (xv * r * g_ref[...]).astype(o_ref.dtype)

    return pl.pallas_call(
        body, name=name, grid=(T // tr,),
        in_specs=[pl.BlockSpec((tr, D), lambda i: (i, 0)), pl.BlockSpec((1, D), lambda i: (0, 0))],
        out_specs=pl.BlockSpec((tr, D), lambda i: (i, 0)),
        out_shape=jax.ShapeDtypeStruct((T, D), out_dtype), compiler_params=_params("parallel"),
    )(x, g)


def _resid_rms(x, z, g, scale, name):
    T, D = x.shape
    tr = _tile(T, ROW_TILE)

    def body(x_ref, z_ref, g_ref, o_ref):
        zv = z_ref[...]
        r = lax.rsqrt(jnp.mean(zv * zv, axis=-1, keepdims=True) + EPS)
        o_ref[...] = x_ref[...] + scale * (zv * r * g_ref[...])

    row = pl.BlockSpec((tr, D), lambda i: (i, 0))
    return pl.pallas_call(
        body, name=name, grid=(T // tr,), in_specs=[row, row, pl.BlockSpec((1, D), lambda i: (0, 0))],
        out_specs=row, out_shape=jax.ShapeDtypeStruct((T, D), F32), compiler_params=_params("parallel"),
    )(x, z, g)


def _final_loss(x, z, g, scale, target, name):
    T, D = x.shape
    tr = _tile(T, ROW_TILE)

    def body(x_ref, z_ref, g_ref, t_ref, dx_ref, acc_ref):
        @pl.when(pl.program_id(0) == 0)
        def _():
            acc_ref[...] = jnp.zeros_like(acc_ref)

        zv = z_ref[...]
        r = lax.rsqrt(jnp.mean(zv * zv, axis=-1, keepdims=True) + EPS)
        e = x_ref[...] + scale * (zv * r * g_ref[...]) - t_ref[...]
        dx_ref[...] = e * (1.0 / D)
        acc_ref[...] += _colsum(_rowsum(e * e))

    row = pl.BlockSpec((tr, D), lambda i: (i, 0))
    return pl.pallas_call(
        body, name=name, grid=(T // tr,), in_specs=[row, row, pl.BlockSpec((1, D), lambda i: (0, 0)), row],
        out_specs=[row, pl.BlockSpec((8, LANES), lambda i: (0, 0))],
        out_shape=[jax.ShapeDtypeStruct((T, D), F32), jax.ShapeDtypeStruct((8, LANES), F32)],
        compiler_params=_params("arbitrary"),
    )(x, z, g, target)


def _rms_bwd(xin, g, dy, scale, name, out_dtype, resid=None):
    T, D = xin.shape
    tr = _tile(T, ROW_TILE)
    has_resid = resid is not None

    def body(*refs):
        x_ref, g_ref, dy_ref = refs[:3]
        r_ref = refs[3] if has_resid else None
        dx_ref, dg_ref = refs[-2], refs[-1]

        @pl.when(pl.program_id(0) == 0)
        def _():
            dg_ref[...] = jnp.zeros_like(dg_ref)

        xv = x_ref[...]
        r = lax.rsqrt(jnp.mean(xv * xv, axis=-1, keepdims=True) + EPS)
        xh = xv * r
        dyv = dy_ref[...].astype(F32) * scale
        dxh = dyv * g_ref[...]
        dx = r * (dxh - xh * jnp.mean(dxh * xh, axis=-1, keepdims=True))
        if has_resid:
            dx = dx + r_ref[...]
        dx_ref[...] = dx.astype(dx_ref.dtype)
        dg_ref[...] += _colsum(dyv * xh)

    row = pl.BlockSpec((tr, D), lambda i: (i, 0))
    vec = pl.BlockSpec((1, D), lambda i: (0, 0))
    return pl.pallas_call(
        body, name=name, grid=(T // tr,), in_specs=[row, vec, row] + ([row] if has_resid else []),
        out_specs=[row, vec],
        out_shape=[jax.ShapeDtypeStruct((T, D), out_dtype), jax.ShapeDtypeStruct((1, D), F32)],
        compiler_params=_params("arbitrary"),
    )(*((xin, g, dy) + ((resid,) if has_resid else ())))


def _mm_swiglu(h, w_in, name):
    T, K = h.shape
    F2 = w_in.shape[1]
    tf = SWIGLU_TILE
    tm = _tile(T, _MM_TILES["nn"][0])

    def body(h_ref, w_ref, u_ref, a_ref):
        u = _dot(h_ref[...], w_ref[...])
        u_ref[...] = u.astype(u_ref.dtype)
        gate, up = u[:, :tf], u[:, tf:]
        a_ref[...] = (gate * _sig(gate) * up).astype(a_ref.dtype)

    return pl.pallas_call(
        body, name=name, grid=(T // tm, F2 // (2 * tf)),
        in_specs=[pl.BlockSpec((tm, K), lambda i, j: (i, 0)), pl.BlockSpec((K, 2 * tf), lambda i, j: (0, j))],
        out_specs=[pl.BlockSpec((tm, 2 * tf), lambda i, j: (i, j)), pl.BlockSpec((tm, tf), lambda i, j: (i, j))],
        out_shape=[jax.ShapeDtypeStruct((T, F2), BF16), jax.ShapeDtypeStruct((T, F2 // 2), BF16)],
        compiler_params=_params("parallel", "parallel"),
    )(h, w_in)


def _mm_swiglu_bwd(dz, w_down, u, name):
    T, D = dz.shape
    F = w_down.shape[0]
    tf = SWIGLU_TILE
    tm = _tile(T, _MM_TILES["nn"][0])

    def body(dz_ref, w_ref, u_ref, o_ref):
        d = _dot(dz_ref[...], w_ref[...], "nt")
        gate = u_ref[:, :tf].astype(F32)
        up = u_ref[:, tf:].astype(F32)
        s = _sig(gate)
        o_ref[:, :tf] = (d * up * (s * (1.0 + gate * (1.0 - s)))).astype(o_ref.dtype)
        o_ref[:, tf:] = (d * gate * s).astype(o_ref.dtype)

    return pl.pallas_call(
        body, name=name, grid=(T // tm, F // tf),
        in_specs=[pl.BlockSpec((tm, D), lambda i, j: (i, 0)), pl.BlockSpec((tf, D), lambda i, j: (j, 0)),
                  pl.BlockSpec((tm, 2 * tf), lambda i, j: (i, j))],
        out_specs=pl.BlockSpec((tm, 2 * tf), lambda i, j: (i, j)),
        out_shape=jax.ShapeDtypeStruct((T, 2 * F), BF16), compiler_params=_params("parallel", "parallel"),
    )(dz, w_down, u)


def _hgout_fwd(o_a, pm, g, name):
    T, D = o_a.shape
    tr = _tile(T, ROW_TILE)

    def body(o_ref, ga_ref, g_ref, out_ref):
        ov = o_ref[...]
        r = lax.rsqrt(jnp.mean(ov * ov, axis=-1, keepdims=True) + EPS)
        ga = ga_ref[...].astype(F32)
        out_ref[...] = (ov * r * g_ref[...] * (ga * _sig(ga))).astype(out_ref.dtype)

    row = pl.BlockSpec((tr, D), lambda i: (i, 0))
    return pl.pallas_call(
        body, name=name, grid=(T // tr,),
        in_specs=[row, pl.BlockSpec((tr, D), lambda i: (i, 3)), pl.BlockSpec((1, D), lambda i: (0, 0))],
        out_specs=row, out_shape=jax.ShapeDtypeStruct((T, D), BF16), compiler_params=_params("parallel"),
    )(o_a, pm, g)


def _hgout_bwd(o_a, pm, g, d_out, name):
    T, D = o_a.shape
    tr = _tile(T, ROW_TILE)

    def body(o_ref, ga_ref, g_ref, d_ref, do_ref, dga_ref, dg_ref):
        @pl.when(pl.program_id(0) == 0)
        def _():
            dg_ref[...] = jnp.zeros_like(dg_ref)

        ov = o_ref[...]
        r = lax.rsqrt(jnp.mean(ov * ov, axis=-1, keepdims=True) + EPS)
        oh = ov * r
        ga = ga_ref[...].astype(F32)
        s = _sig(ga)
        d = d_ref[...].astype(F32)
        dn = d * (ga * s)
        dga_ref[...] = (d * (oh * g_ref[...]) * (s * (1.0 + ga * (1.0 - s)))).astype(dga_ref.dtype)
        dxh = dn * g_ref[...]
        do_ref[...] = (r * (dxh - oh * jnp.mean(dxh * oh, axis=-1, keepdims=True))).astype(do_ref.dtype)
        dg_ref[...] += _colsum(dn * oh)

    row = pl.BlockSpec((tr, D), lambda i: (i, 0))
    vec = pl.BlockSpec((1, D), lambda i: (0, 0))
    return pl.pallas_call(
        body, name=name, grid=(T // tr,), in_specs=[row, pl.BlockSpec((tr, D), lambda i: (i, 3)), vec, row],
        out_specs=[row, row, vec],
        out_shape=[jax.ShapeDtypeStruct((T, D), BF16), jax.ShapeDtypeStruct((T, D), BF16),
                   jax.ShapeDtypeStruct((1, D), F32)],
        compiler_params=_params("arbitrary"),
    )(o_a, pm, g, d_out)


def _merge_fwd(ya, yb, pg, bg, name):
    T, D = ya.shape
    tr = _tile(T, 256)

    def body(ya_ref, yb_ref, pg_ref, bg_ref, o_ref):
        g0 = _sig(pg_ref[:, :D] + bg_ref[:, :D])
        g1 = _sig(pg_ref[:, D:] + bg_ref[:, D:])
        o_ref[...] = (g0 * ya_ref[...] + g1 * yb_ref[...]).astype(o_ref.dtype)

    row = pl.BlockSpec((tr, D), lambda i: (i, 0))
    return pl.pallas_call(
        body, name=name, grid=(T // tr,),
        in_specs=[row, row, pl.BlockSpec((tr, 2 * D), lambda i: (i, 0)), pl.BlockSpec((1, 2 * D), lambda i: (0, 0))],
        out_specs=row, out_shape=jax.ShapeDtypeStruct((T, D), BF16), compiler_params=_params("parallel"),
    )(ya, yb, pg, bg)


def _merge_bwd(dy, ya, yb, pg, bg, name):
    T, D = ya.shape
    tr = _tile(T, 256)

    def body(dy_ref, ya_ref, yb_ref, pg_ref, bg_ref, dya_ref, dyb_ref, dpg_ref, dbg_ref):
        @pl.when(pl.program_id(0) == 0)
        def _():
            dbg_ref[...] = jnp.zeros_like(dbg_ref)

        d = dy_ref[...]
        g0 = _sig(pg_ref[:, :D] + bg_ref[:, :D])
        g1 = _sig(pg_ref[:, D:] + bg_ref[:, D:])
        dya_ref[...] = (d * g0).astype(dya_ref.dtype)
        dyb_ref[...] = (d * g1).astype(dyb_ref.dtype)
        dg0 = d * ya_ref[...] * (g0 * (1.0 - g0))
        dg1 = d * yb_ref[...] * (g1 * (1.0 - g1))
        dpg_ref[:, :D] = dg0.astype(dpg_ref.dtype)
        dpg_ref[:, D:] = dg1.astype(dpg_ref.dtype)
        dbg_ref[:, :D] += _colsum(dg0)
        dbg_ref[:, D:] += _colsum(dg1)

    row = pl.BlockSpec((tr, D), lambda i: (i, 0))
    wide = pl.BlockSpec((tr, 2 * D), lambda i: (i, 0))
    wvec = pl.BlockSpec((1, 2 * D), lambda i: (0, 0))
    return pl.pallas_call(
        body, name=name, grid=(T // tr,), in_specs=[row, row, row, wide, wvec],
        out_specs=[row, row, wide, wvec],
        out_shape=[jax.ShapeDtypeStruct((T, D), BF16), jax.ShapeDtypeStruct((T, D), BF16),
                   jax.ShapeDtypeStruct((T, 2 * D), BF16), jax.ShapeDtypeStruct((1, 2 * D), F32)],
        compiler_params=_params("arbitrary"),
    )(dy, ya, yb, pg, bg)


def _hgrn_chunk_terms(q, fl, lb, tri):
    shape = q.shape
    row = _iota(shape, 0)
    sg = _sig(fl)
    f = lb + (1.0 - lb) * sg
    k = 1.0 - f
    b = _tri_dot(tri, jnp.log(f))
    ref1 = jnp.where(row < HALF, _pick_row(b, HALF // 2), _pick_row(b, HALF + HALF // 2))
    b_half = _pick_row(b, HALF - 1)
    b_last = _pick_row(b, CHUNK - 1)
    sq = _sig(q)
    qs = q * sq
    e_q1 = jnp.exp(jnp.minimum(b - ref1, EXP_CLAMP))
    e_k1 = jnp.exp(jnp.minimum(ref1 - b, EXP_CLAMP))
    e_q2 = jnp.exp(jnp.minimum(b - b_half, 0.0))
    e_k2 = jnp.exp(jnp.minimum(b_half - b, 0.0))
    e_b = jnp.exp(b)
    e_kd = jnp.exp(b_last - b)
    return dict(sg=sg, f=f, k=k, sq=sq, qs=qs, e_q1=e_q1, e_k1=e_k1, e_q2=e_q2, e_k2=e_k2, e_b=e_b, e_kd=e_kd,
                e_last=jnp.exp(b_last))


def _hgrn_masks():
    r = _iota((CHUNK, CHUNK), 0)
    c = _iota((CHUNK, CHUNK), 1)
    causal = r >= c
    same = (r < HALF) == (c < HALF)
    return causal, causal & same, (r >= HALF) & (c < HALF)


def _softmax_lb(lbl_ref):
    l0, l1 = lbl_ref[0, 0], lbl_ref[1, 0]
    mx = jnp.maximum(l0, l1)
    e0, e1 = jnp.exp(l0 - mx), jnp.exp(l1 - mx)
    return e0 / (e0 + e1)


def _hgrn_fwd(pm, lbl, name):
    T = pm.shape[0]
    tb = _tile(T, SEQ_BLOCK)
    nc = tb // CHUNK

    def body(q_ref, f_ref, i_ref, lbl_ref, o_ref, st_ref, s_sc):
        @pl.when(pl.program_id(1) == 0)
        def _():
            s_sc[...] = jnp.zeros_like(s_sc)

        lb = _softmax_lb(lbl_ref)
        causal, m1, m2 = _hgrn_masks()
        tri = jnp.where(causal, 1.0, 0.0).astype(BF16)
        for ci in range(nc):
            sl = pl.ds(ci * CHUNK, CHUNK)
            t = _hgrn_chunk_terms(q_ref[sl, :].astype(F32), f_ref[sl, :].astype(F32), lb, tri)
            iv = i_ref[sl, :]
            a1 = _dot((t["qs"] * t["e_q1"]).astype(BF16), (t["k"] * t["e_k1"]).astype(BF16), "nt")
            a2 = _dot((t["qs"] * t["e_q2"]).astype(BF16), (t["k"] * t["e_k2"]).astype(BF16), "nt")
            a = jnp.where(m1, a1, 0.0) + jnp.where(m2, a2, 0.0)
            st = s_sc[...]
            st_ref[0, ci] = st
            o_ref[sl, :] = _dot(a.astype(BF16), iv) + _dot((t["qs"] * t["e_b"]).astype(BF16), st.astype(BF16), "nt")
            s_sc[...] = t["e_last"] * st + _dot(iv, (t["k"] * t["e_kd"]).astype(BF16), "tn")

    blk = lambda off: pl.BlockSpec((tb, DH), lambda h, b: (b, off + h))
    return pl.pallas_call(
        body, name=name, grid=(HEADS, T // tb),
        in_specs=[blk(0), blk(HEADS), blk(2 * HEADS), pl.BlockSpec((2, 1, 1, DH), lambda h, b: (0, h, 0, 0))],
        out_specs=[pl.BlockSpec((tb, DH), lambda h, b: (b, h)),
                   pl.BlockSpec((1, nc, DH, DH), lambda h, b: (h, b, 0, 0))],
        out_shape=[jax.ShapeDtypeStruct((T, HEADS * DH), F32),
                   jax.ShapeDtypeStruct((HEADS, T // CHUNK, DH, DH), F32)],
        scratch_shapes=[pltpu.VMEM((DH, DH), F32)],
        compiler_params=_params("parallel", "arbitrary"),
    )(pm, pm, pm, lbl)


def _hgrn_bwd(pm, lbl, states, do, name):
    T = pm.shape[0]
    tb = _tile(T, SEQ_BLOCK)
    nc = tb // CHUNK
    nb = T // tb

    def body(q_ref, f_ref, i_ref, lbl_ref, st_ref, do_ref, dq_ref, df_ref, di_ref, dl_ref, ds_sc, dlb_sc):
        @pl.when(pl.program_id(1) == 0)
        def _():
            ds_sc[...] = jnp.zeros_like(ds_sc)
            dlb_sc[...] = jnp.zeros_like(dlb_sc)

        lb = _softmax_lb(lbl_ref)
        causal, m1, m2 = _hgrn_masks()
        tri = jnp.where(causal, 1.0, 0.0).astype(BF16)
        tri_rev = jnp.where(_iota((CHUNK, CHUNK), 0) <= _iota((CHUNK, CHUNK), 1), 1.0, 0.0).astype(BF16)
        last_row = _iota((CHUNK, DH), 0) == CHUNK - 1
        for ci in reversed(range(nc)):
            sl = pl.ds(ci * CHUNK, CHUNK)
            q = q_ref[sl, :].astype(F32)
            t = _hgrn_chunk_terms(q, f_ref[sl, :].astype(F32), lb, tri)
            iv = i_ref[sl, :]
            dov = do_ref[sl, :]
            qe1, ke1 = t["qs"] * t["e_q1"], t["k"] * t["e_k1"]
            qe2, ke2 = t["qs"] * t["e_q2"], t["k"] * t["e_k2"]
            qi, kd = t["qs"] * t["e_b"], t["k"] * t["e_kd"]
            qe1b, ke1b, qe2b, ke2b = qe1.astype(BF16), ke1.astype(BF16), qe2.astype(BF16), ke2.astype(BF16)
            a = jnp.where(m1, _dot(qe1b, ke1b, "nt"), 0.0) + jnp.where(m2, _dot(qe2b, ke2b, "nt"), 0.0)
            st = st_ref[0, ci]
            dsn = ds_sc[...]
            dsnb = dsn.astype(BF16)
            da = _dot(dov, iv, "nt")
            da1 = jnp.where(m1, da, 0.0).astype(BF16)
            da2 = jnp.where(m2, da, 0.0).astype(BF16)
            di_ref[sl, :] = (_dot(a.astype(BF16), dov, "tn") + _dot(kd.astype(BF16), dsnb, "nt")).astype(di_ref.dtype)
            dqe1, dke1 = _dot(da1, ke1b), _dot(da1, qe1b, "tn")
            dqe2, dke2 = _dot(da2, ke2b), _dot(da2, qe2b, "tn")
            dqi = _dot(dov, st.astype(BF16))
            dkd = _dot(iv, dsnb)
            ds_sc[...] = t["e_last"] * dsn + _dot(dov, qi.astype(BF16), "tn")
            dqs = dqe1 * t["e_q1"] + dqe2 * t["e_q2"] + dqi * t["e_b"]
            dk = dke1 * t["e_k1"] + dke2 * t["e_k2"] + dkd * t["e_kd"]
            qib, kdb = qi.astype(BF16).astype(F32), kd.astype(BF16).astype(F32)
            db = (dqe1 * qe1b.astype(F32) - dke1 * ke1b.astype(F32) + dqe2 * qe2b.astype(F32)
                  - dke2 * ke2b.astype(F32) + dqi * qib - dkd * kdb)
            extra = _colsum(dkd * kdb) + t["e_last"] * _colsum(dsn * st)
            db = db + jnp.where(last_row, extra, 0.0)
            dlf = _tri_dot(tri_rev, db)
            dfv = dlf / t["f"] - dk
            sg = t["sg"]
            df_ref[sl, :] = (dfv * (1.0 - lb) * sg * (1.0 - sg)).astype(df_ref.dtype)
            dlb_sc[...] += _colsum(dfv * (1.0 - sg))
            sq = t["sq"]
            dq_ref[sl, :] = (dqs * (sq * (1.0 + q * (1.0 - sq)))).astype(dq_ref.dtype)

        @pl.when(pl.program_id(1) == nb - 1)
        def _():
            dl0 = dlb_sc[...] * lb * (1.0 - lb)
            dl_ref[0, 0] = dl0
            dl_ref[1, 0] = -dl0

    blk = lambda off: pl.BlockSpec((tb, DH), lambda h, b: (nb - 1 - b, off + h))
    lspec = pl.BlockSpec((2, 1, 1, DH), lambda h, b: (0, h, 0, 0))
    out_blk = pl.BlockSpec((tb, DH), lambda h, b: (nb - 1 - b, h))
    D = HEADS * DH
    return pl.pallas_call(
        body, name=name, grid=(HEADS, nb),
        in_specs=[blk(0), blk(HEADS), blk(2 * HEADS), lspec,
                  pl.BlockSpec((1, nc, DH, DH), lambda h, b: (h, nb - 1 - b, 0, 0)), out_blk],
        out_specs=[out_blk, out_blk, out_blk, lspec],
        out_shape=[jax.ShapeDtypeStruct((T, D), BF16)] * 3 + [jax.ShapeDtypeStruct((2, HEADS, 1, DH), F32)],
        scratch_shapes=[pltpu.VMEM((DH, DH), F32), pltpu.VMEM((1, DH), F32)],
        compiler_params=_params("parallel", "arbitrary"),
    )(pm, pm, pm, lbl, states, do)


def _log_sigmoid(x):
    return jnp.minimum(x, 0.0) - jnp.log(1.0 + jnp.exp(-jnp.abs(x)))


def _fox_cumsum(pf, bias, name):
    T = pf.shape[0]
    tb = _tile(T, SEQ_BLOCK)

    def body(x_ref, b_ref, c_ref, carry):
        @pl.when(pl.program_id(0) == 0)
        def _():
            carry[...] = jnp.zeros_like(carry)

        tri = jnp.where(_iota((tb, tb), 0) >= _iota((tb, tb), 1), 1.0, 0.0).astype(BF16)
        c = _tri_dot(tri, _log_sigmoid(x_ref[...] + b_ref[...])) + carry[...]
        c_ref[...] = c
        carry[...] = _pick_row(c, tb - 1)

    row = pl.BlockSpec((tb, LANES), lambda i: (i, 0))
    return pl.pallas_call(
        body, name=name, grid=(T // tb,), in_specs=[row, pl.BlockSpec((1, LANES), lambda i: (0, 0))],
        out_specs=row, out_shape=jax.ShapeDtypeStruct((T, LANES), F32),
        scratch_shapes=[pltpu.VMEM((1, LANES), F32)], compiler_params=_params("arbitrary"),
    )(pf, bias)


def _fox_dcum(dc, pf, bias, name):
    T = pf.shape[0]
    tb = _tile(T, SEQ_BLOCK)
    nb = T // tb

    def body(dc_ref, x_ref, b_ref, dx_ref, db_ref, carry):
        @pl.when(pl.program_id(0) == 0)
        def _():
            carry[...] = jnp.zeros_like(carry)
            db_ref[...] = jnp.zeros_like(db_ref)

        tri_rev = jnp.where(_iota((tb, tb), 0) <= _iota((tb, tb), 1), 1.0, 0.0).astype(BF16)
        dls = _tri_dot(tri_rev, dc_ref[...]) + carry[...]
        carry[...] = _pick_row(dls, 0)
        dx = dls * (1.0 - _sig(x_ref[...] + b_ref[...]))
        dx_ref[...] = dx
        db_ref[...] += _colsum(dx)

    row = pl.BlockSpec((tb, LANES), lambda i: (nb - 1 - i, 0))
    vec = pl.BlockSpec((1, LANES), lambda i: (0, 0))
    return pl.pallas_call(
        body, name=name, grid=(nb,), in_specs=[row, row, vec], out_specs=[row, vec],
        out_shape=[jax.ShapeDtypeStruct((T, LANES), F32), jax.ShapeDtypeStruct((1, LANES), F32)],
        scratch_shapes=[pltpu.VMEM((1, LANES), F32)], compiler_params=_params("arbitrary"),
    )(dc, pf, bias)


_Q_OFF, _K_OFF, _V_OFF = 4 * HEADS, 5 * HEADS, 6 * HEADS


def _causal_pairs(nq, by_key):
    if by_key:
        pairs = [(i, j) for j in range(nq) for i in range(j, nq)]
    else:
        pairs = [(i, j) for i in range(nq) for j in range(i + 1)]
    return jnp.asarray([p[0] for p in pairs], jnp.int32), jnp.asarray([p[1] for p in pairs], jnp.int32)


def _fox_logits(q, k, cq, ck, row0, masked):
    s = _dot(q, k, "nt") + (cq - ck)
    if masked:
        s = jnp.where(_iota(s.shape, 0) + row0 >= _iota(s.shape, 1), s, NEG_BIG)
    return s


def _fox_fwd(pm, c_col, c_row, name):
    T = pm.shape[0]
    tq = _tile(T, ATTN_TILE)
    nq = T // tq
    rg = min(ATTN_ROWS, tq)
    qi_tab, kj_tab = _causal_pairs(nq, by_key=False)

    def body(qi_ref, kj_ref, q_ref, k_ref, v_ref, cq_ref, ck_ref, o_ref, lse_ref, m_sc, l_sc, acc_sc):
        t = pl.program_id(1)
        i, j = qi_ref[t], kj_ref[t]

        @pl.when(j == 0)
        def _():
            m_sc[...] = jnp.full_like(m_sc, NEG_BIG)
            l_sc[...] = jnp.zeros_like(l_sc)
            acc_sc[...] = jnp.zeros_like(acc_sc)

        def step(diag):
            m_all, l_all, acc_all = m_sc[...], l_sc[...], acc_sc[...]
            ms, ls, accs = [], [], []
            for r in range(tq // rg):
                rows = slice(r * rg, (r + 1) * rg)
                w = (r + 1) * rg if diag else tq
                s = _fox_logits(q_ref[rows, :], k_ref[:w, :], cq_ref[0, rows, :], ck_ref[0, :, :w], r * rg, diag)
                m_old = m_all[rows, :]
                m_new = jnp.maximum(m_old, jnp.max(s, axis=1, keepdims=True))
                alpha = jnp.exp(m_old - m_new)
                p = jnp.exp(s - m_new)
                ms.append(m_new)
                ls.append(alpha * l_all[rows, :] + _rowsum(p))
                accs.append(alpha * acc_all[rows, :] + _dot(p.astype(BF16), v_ref[:w, :]))
            m_sc[...] = jnp.concatenate(ms, axis=0)
            l_sc[...] = jnp.concatenate(ls, axis=0)
            acc_sc[...] = jnp.concatenate(accs, axis=0)

        @pl.when(j < i)
        def _():
            step(False)

        @pl.when(j == i)
        def _():
            step(True)
            o_ref[...] = (acc_sc[...] / l_sc[...]).astype(o_ref.dtype)
            lse_ref[0] = m_sc[...] + jnp.log(l_sc[...])

    kv = lambda off: pl.BlockSpec((tq, DH), lambda h, t, qi, kj: (kj[t], off + h))
    col = pl.BlockSpec((1, tq, 1), lambda h, t, qi, kj: (h, qi[t], 0))
    grid_spec = pltpu.PrefetchScalarGridSpec(
        num_scalar_prefetch=2, grid=(HEADS, qi_tab.shape[0]),
        in_specs=[pl.BlockSpec((tq, DH), lambda h, t, qi, kj: (qi[t], _Q_OFF + h)), kv(_K_OFF), kv(_V_OFF), col,
                  pl.BlockSpec((1, 1, tq), lambda h, t, qi, kj: (h, 0, kj[t]))],
        out_specs=[pl.BlockSpec((tq, DH), lambda h, t, qi, kj: (qi[t], h)), col],
        scratch_shapes=[pltpu.VMEM((tq, 1), F32), pltpu.VMEM((tq, 1), F32), pltpu.VMEM((tq, DH), F32)])
    return pl.pallas_call(
        body, name=name, grid_spec=grid_spec,
        out_shape=[jax.ShapeDtypeStruct((T, HEADS * DH), BF16), jax.ShapeDtypeStruct((HEADS, T, 1), F32)],
        compiler_params=_params("parallel", "arbitrary"),
    )(qi_tab, kj_tab, pm, pm, pm, c_col, c_row)


def _fox_delta(do, o, name):
    T, D = o.shape
    tr = _tile(T, ROW_TILE)

    def body(do_ref, o_ref, d_ref):
        prod = do_ref[...].astype(F32) * o_ref[...].astype(F32)
        for h in range(HEADS):
            d_ref[h] = _rowsum(prod[:, h * DH:(h + 1) * DH])

    row = pl.BlockSpec((tr, D), lambda i: (i, 0))
    return pl.pallas_call(
        body, name=name, grid=(T // tr,), in_specs=[row, row],
        out_specs=pl.BlockSpec((HEADS, tr, 1), lambda i: (0, i, 0)),
        out_shape=jax.ShapeDtypeStruct((HEADS, T, 1), F32), compiler_params=_params("parallel"),
    )(do, o)


def _fox_bwd(pm, c_col, c_row, do, lse, delta, name):
    T = pm.shape[0]
    tq = _tile(T, ATTN_TILE)
    nq = T // tq
    rg = min(ATTN_ROWS, tq)
    qi_tab, kj_tab = _causal_pairs(nq, by_key=True)
    npairs = qi_tab.shape[0]

    def body(qi_ref, kj_ref, q_ref, k_ref, v_ref, cq_ref, ck_ref, do_ref, lse_ref, dl_ref,
             dq_ref, dk_ref, dv_ref, dcq_ref, dck_ref, dq_sc, dk_sc, dv_sc, dck_sc):
        t = pl.program_id(1)
        i, j = qi_ref[t], kj_ref[t]

        @pl.when(t == 0)
        def _():
            dq_sc[...] = jnp.zeros_like(dq_sc)
            dcq_ref[...] = jnp.zeros_like(dcq_ref)

        @pl.when(i == j)
        def _():
            dk_sc[...] = jnp.zeros_like(dk_sc)
            dv_sc[...] = jnp.zeros_like(dv_sc)
            dck_sc[...] = jnp.zeros_like(dck_sc)

        base = pl.multiple_of(i * tq, tq)

        def step(diag):
            for r in range(tq // rg):
                rows = slice(r * rg, (r + 1) * rg)
                w = (r + 1) * rg if diag else tq
                qr, dor = q_ref[rows, :], do_ref[rows, :]
                s = _fox_logits(qr, k_ref[:w, :], cq_ref[0, rows, :], ck_ref[0, :, :w], r * rg, diag)
                p = jnp.exp(s - lse_ref[0, rows, :])
                dp = _dot(dor, v_ref[:w, :], "nt")
                ds = p * (dp - dl_ref[0, rows, :])
                dsb = ds.astype(BF16)
                dv_sc[:w, :] += _dot(p.astype(BF16), dor, "tn")
                dk_sc[:w, :] += _dot(dsb, qr, "tn")
                dck_sc[:, :w] -= _colsum(ds)
                tgt = pl.ds(base + r * rg, rg)
                dq_sc[tgt, :] += _dot(dsb, k_ref[:w, :])
                dcq_ref[0, tgt, :] += _rowsum(ds)

        @pl.when(i > j)
        def _():
            step(False)

        @pl.when(i == j)
        def _():
            step(True)

        @pl.when(i == nq - 1)
        def _():
            dk_ref[...] = dk_sc[...].astype(dk_ref.dtype)
            dv_ref[...] = dv_sc[...].astype(dv_ref.dtype)
            dck_ref[0] = dck_sc[...]

        @pl.when(t == npairs - 1)
        def _():
            dq_ref[...] = dq_sc[...].astype(dq_ref.dtype)

    col = pl.BlockSpec((1, tq, 1), lambda h, t, qi, kj: (h, qi[t], 0))
    kv = lambda off: pl.BlockSpec((tq, DH), lambda h, t, qi, kj: (kj[t], off + h))
    kv_out = pl.BlockSpec((tq, DH), lambda h, t, qi, kj: (kj[t], h))
    key_row = pl.BlockSpec((1, 1, tq), lambda h, t, qi, kj: (h, 0, kj[t]))
    grid_spec = pltpu.PrefetchScalarGridSpec(
        num_scalar_prefetch=2, grid=(HEADS, npairs),
        in_specs=[pl.BlockSpec((tq, DH), lambda h, t, qi, kj: (qi[t], _Q_OFF + h)), kv(_K_OFF), kv(_V_OFF), col,
                  key_row, pl.BlockSpec((tq, DH), lambda h, t, qi, kj: (qi[t], h)), col, col],
        out_specs=[pl.BlockSpec((T, DH), lambda h, t, qi, kj: (0, h)), kv_out, kv_out,
                   pl.BlockSpec((1, T, 1), lambda h, t, qi, kj: (h, 0, 0)), key_row],
        scratch_shapes=[pltpu.VMEM((T, DH), F32), pltpu.VMEM((tq, DH), F32), pltpu.VMEM((tq, DH), F32),
                        pltpu.VMEM((1, tq), F32)])
    D = HEADS * DH
    return pl.pallas_call(
        body, name=name, grid_spec=grid_spec,
        out_shape=[jax.ShapeDtypeStruct((T, D), BF16)] * 3
        + [jax.ShapeDtypeStruct((HEADS, T, 1), F32), jax.ShapeDtypeStruct((HEADS, 1, T), F32)],
        compiler_params=_params("parallel", "arbitrary"),
    )(qi_tab, kj_tab, pm, pm, pm, c_col, c_row, do, lse, delta)


def _xattn_fwd(q, kv, name):
    T, D = q.shape
    M = kv.shape[0]
    dh = D // MEM_HEADS
    tq = _tile(T, ATTN_TILE)
    scale = 1.0 / math.sqrt(dh)

    def body(q_ref, kv_ref, o_ref):
        for h in range(MEM_HEADS):
            cs = slice(h * dh, (h + 1) * dh)
            s = _dot(q_ref[:, cs], kv_ref[:, cs], "nt") * scale
            p = jnp.exp(s - jnp.max(s, axis=1, keepdims=True))
            p = p / _rowsum(p)
            o_ref[:, cs] = _dot(p.astype(BF16), kv_ref[:, D + h * dh:D + (h + 1) * dh]).astype(o_ref.dtype)

    row = pl.BlockSpec((tq, D), lambda i: (i, 0))
    return pl.pallas_call(
        body, name=name, grid=(T // tq,), in_specs=[row, pl.BlockSpec((M, 2 * D), lambda i: (0, 0))],
        out_specs=row, out_shape=jax.ShapeDtypeStruct((T, D), BF16), compiler_params=_params("parallel"),
    )(q, kv)


def _xattn_bwd(q, kv, do, name):
    T, D = q.shape
    M = kv.shape[0]
    dh = D // MEM_HEADS
    tq = _tile(T, ATTN_TILE)
    scale = 1.0 / math.sqrt(dh)

    def body(q_ref, kv_ref, do_ref, dq_ref, dkv_ref):
        @pl.when(pl.program_id(0) == 0)
        def _():
            dkv_ref[...] = jnp.zeros_like(dkv_ref)

        for h in range(MEM_HEADS):
            cs = slice(h * dh, (h + 1) * dh)
            vs = slice(D + h * dh, D + (h + 1) * dh)
            s = _dot(q_ref[:, cs], kv_ref[:, cs], "nt") * scale
            p = jnp.exp(s - jnp.max(s, axis=1, keepdims=True))
            p = p / _rowsum(p)
            dp = _dot(do_ref[:, cs], kv_ref[:, vs], "nt")
            ds = (p * (dp - _rowsum(p * dp)) * scale).astype(BF16)
            dq_ref[:, cs] = _dot(ds, kv_ref[:, cs]).astype(dq_ref.dtype)
            dkv_ref[:, cs] += _dot(ds, q_ref[:, cs], "tn")
            dkv_ref[:, vs] += _dot(p.astype(BF16), do_ref[:, cs], "tn")

    row = pl.BlockSpec((tq, D), lambda i: (i, 0))
    full = pl.BlockSpec((M, 2 * D), lambda i: (0, 0))
    return pl.pallas_call(
        body, name=name, grid=(T // tq,), in_specs=[row, full, row], out_specs=[row, full],
        out_shape=[jax.ShapeDtypeStruct((T, D), BF16), jax.ShapeDtypeStruct((M, 2 * D), F32)],
        compiler_params=_params("arbitrary"),
    )(q, kv, do)


_HBM = pl.BlockSpec(memory_space=pltpu.HBM)


def _position():
    return lax.axis_index("x"), lax.axis_index("y"), lax.axis_index("c")


def _other_chips(x, y):
    return [(1 - x, y), (x, 1 - y), (1 - x, 1 - y)]


def _exchange_chips(srcs, outs, send, recv, local, src_of, dst_of):
    x, y, c = _position()
    q = 2 * x + y
    kept, sent = [], []
    for w, (s_ref, o_ref) in enumerate(zip(srcs, outs)):
        mine = pltpu.make_async_copy(src_of(s_ref, q), dst_of(o_ref, q), local.at[w])
        mine.start()
        kept.append(mine)
        for j, (px, py) in enumerate(_other_chips(x, y)):
            cp = pltpu.make_async_remote_copy(
                src_ref=src_of(s_ref, 2 * px + py), dst_ref=dst_of(o_ref, q), send_sem=send.at[3 * w + j],
                recv_sem=recv.at[3 * w + j], device_id=(px, py, c), device_id_type=MESH)
            cp.start()
            sent.append(cp)
    for w, (s_ref, o_ref) in enumerate(zip(srcs, outs)):
        for j, (px, py) in enumerate(_other_chips(x, y)):
            pltpu.make_async_remote_copy(
                src_ref=src_of(s_ref, q), dst_ref=dst_of(o_ref, 2 * px + py), send_sem=send.at[3 * w + j],
                recv_sem=recv.at[3 * w + j], device_id=(px, py, c), device_id_type=MESH).wait_recv()
    for cp in sent:
        cp.wait_send()
    for cp in kept:
        cp.wait()


def _chip_sems(n):
    return [pltpu.SemaphoreType.DMA((3 * n,)), pltpu.SemaphoreType.DMA((3 * n,)), pltpu.SemaphoreType.DMA((n,))]


def _ag_chips(blks, name):
    n = len(blks)

    def body(*refs):
        c = lax.axis_index("c")
        _exchange_chips(refs[:n], refs[n:2 * n], *refs[2 * n:], src_of=lambda r, chip: r,
                        dst_of=lambda r, chip: r.at[chip, c])

    return pl.pallas_call(
        body, name=name, in_specs=[_HBM] * n, out_specs=[_HBM] * n,
        out_shape=[jax.ShapeDtypeStruct((4, 2) + b.shape, b.dtype) for b in blks], scratch_shapes=_chip_sems(n),
    )(*blks)


def _ag_sibling(arrs, name):
    n = len(arrs)

    def body(*refs):
        outs, send, recv = refs[n:2 * n], refs[2 * n], refs[2 * n + 1]
        x, y, c = _position()
        cps = []
        for w, a_ref in enumerate(outs):
            cp = pltpu.make_async_remote_copy(src_ref=a_ref.at[:, c], dst_ref=a_ref.at[:, c], send_sem=send.at[w],
                                              recv_sem=recv.at[w], device_id=(x, y, 1 - c), device_id_type=MESH)
            cp.start()
            cps.append(cp)
        for w, a_ref in enumerate(outs):
            pltpu.make_async_remote_copy(src_ref=a_ref.at[:, c], dst_ref=a_ref.at[:, 1 - c], send_sem=send.at[w],
                                         recv_sem=recv.at[w], device_id=(x, y, 1 - c), device_id_type=MESH).wait_recv()
        for cp in cps:
            cp.wait_send()

    return pl.pallas_call(
        body, name=name, in_specs=[_HBM] * n, out_specs=[_HBM] * n,
        out_shape=[jax.ShapeDtypeStruct(a.shape, a.dtype) for a in arrs],
        input_output_aliases={i: i for i in range(n)},
        scratch_shapes=[pltpu.SemaphoreType.DMA((n,)), pltpu.SemaphoreType.DMA((n,))],
    )(*arrs)


def _rs_sibling(blocks, name):
    n = len(blocks)

    def body(*refs):
        srcs, outs, send, recv = refs[:n], refs[n:2 * n], refs[2 * n], refs[2 * n + 1]
        x, y, c = _position()
        cps = []
        for w, (b_ref, l_ref) in enumerate(zip(srcs, outs)):
            cp = pltpu.make_async_remote_copy(src_ref=b_ref.at[:, 1 - c], dst_ref=l_ref, send_sem=send.at[w],
                                              recv_sem=recv.at[w], device_id=(x, y, 1 - c), device_id_type=MESH)
            cp.start()
            cps.append(cp)
        for cp in cps:
            cp.wait()

    return pl.pallas_call(
        body, name=name, in_specs=[_HBM] * n, out_specs=[_HBM] * n,
        out_shape=[jax.ShapeDtypeStruct((4,) + b.shape[2:], b.dtype) for b in blocks],
        scratch_shapes=[pltpu.SemaphoreType.DMA((n,)), pltpu.SemaphoreType.DMA((n,))],
    )(*blocks)


def _rs_chips(parts, name):
    n = len(parts)

    def body(*refs):
        _exchange_chips(refs[:n], refs[n:2 * n], *refs[2 * n:], src_of=lambda r, chip: r.at[chip],
                        dst_of=lambda r, chip: r.at[chip])

    return pl.pallas_call(
        body, name=name, in_specs=[_HBM] * n, out_specs=[_HBM] * n,
        out_shape=[jax.ShapeDtypeStruct(h.shape, h.dtype) for h in parts], scratch_shapes=_chip_sems(n),
    )(*parts)


def _row_tile(rows, pref=256):
    for t in range(min(pref, rows) // 16 * 16, 0, -16):
        if rows % t == 0:
            return t
    raise ValueError(f"no row tile for {rows}")


def _pair_add(blocks, landed, core, out_dtype, name):
    n, _, s0, s1 = blocks.shape
    tr = _row_tile(s0)

    def body(core_ref, a_ref, b_ref, o_ref):
        del core_ref
        o_ref[...] = (a_ref[...] + b_ref[...]).astype(o_ref.dtype)

    grid_spec = pltpu.PrefetchScalarGridSpec(
        num_scalar_prefetch=1, grid=(n, s0 // tr),
        in_specs=[pl.BlockSpec((1, None, tr, s1), lambda p, i, core: (p, core[0], i, 0)),
                  pl.BlockSpec((1, tr, s1), lambda p, i, core: (p, i, 0))],
        out_specs=pl.BlockSpec((1, tr, s1), lambda p, i, core: (p, i, 0)))
    return pl.pallas_call(
        body, name=name, grid_spec=grid_spec, out_shape=jax.ShapeDtypeStruct(landed.shape, out_dtype),
        compiler_params=_params("parallel", "parallel"),
    )(core, blocks, landed)


def _adamw_math(w, g, m, v):
    m = ADAM_B1 * m + (1.0 - ADAM_B1) * g
    v = ADAM_B2 * v + (1.0 - ADAM_B2) * (g * g)
    m_hat = m / (1.0 - ADAM_B1 ** ADAM_STEP)
    v_hat = v / (1.0 - ADAM_B2 ** ADAM_STEP)
    delta = -ADAM_LR * (m_hat / (jnp.sqrt(v_hat) + ADAM_EPS) + ADAM_WD * w)
    return delta, m, v


def _adamw_reduce(slots, w, m, v, name):
    n, R, C = slots.shape
    tr = _row_tile(R)

    def body(s_ref, w_ref, m_ref, v_ref, g_ref, d_ref, nm_ref, nv_ref):
        g = s_ref[0].astype(F32)
        for p in range(1, n):
            g = g + s_ref[p].astype(F32)
        g_ref[...] = g
        d_ref[...], nm_ref[...], nv_ref[...] = _adamw_math(w_ref[...], g, m_ref[...], v_ref[...])

    row = pl.BlockSpec((tr, C), lambda i: (i, 0))
    return pl.pallas_call(
        body, name=name, grid=(R // tr,), in_specs=[pl.BlockSpec((n, tr, C), lambda i: (0, i, 0)), row, row, row],
        out_specs=[row] * 4, out_shape=[jax.ShapeDtypeStruct((R, C), F32)] * 4, compiler_params=_params("parallel"),
    )(slots, w, m, v)


def _full_from_gathered(a, n):
    s0, s1 = a.shape[2:]
    blk = a.reshape(8, s0, s1)
    if n in COL_SHARDED:
        return blk.transpose(1, 0, 2).reshape(s0, 8 * s1)
    return blk.reshape(8 * s0, s1)


def _blocks_from_full(g, n, shard_shape):
    s0, s1 = shard_shape
    if n in COL_SHARDED:
        blk = g.reshape(s0, 8, s1).transpose(1, 0, 2)
    else:
        blk = g.reshape(8, s0, s1)
    return blk.reshape(4, 2, s0, s1)


def _swiglu_interleave(w):
    d, f2 = w.shape
    return w.reshape(d, 2, f2 // (2 * SWIGLU_TILE), SWIGLU_TILE).transpose(0, 2, 1, 3).reshape(d, f2)


def _swiglu_deinterleave(w):
    d, f2 = w.shape
    return w.reshape(d, f2 // (2 * SWIGLU_TILE), 2, SWIGLU_TILE).transpose(0, 2, 1, 3).reshape(d, f2)


SMALL_ROWS = 16


def _pack_small(vals, loss_row):
    rows = []
    for n in SMALL:
        flat = vals[n].reshape(-1)
        pad = (-flat.shape[0]) % PACK_COLS
        rows.append(jnp.pad(flat, (0, pad)).reshape(-1, PACK_COLS))
    rows.append(loss_row)
    out = jnp.concatenate(rows, axis=0)
    assert out.shape[0] == SMALL_ROWS, out.shape
    return out


def _unpack_small(packed, like):
    out, r = {}, 0
    for n in SMALL:
        size = like[n].size
        rows = -(-size // PACK_COLS)
        out[n] = packed[r:r + rows].reshape(-1)[:size].reshape(like[n].shape)
        r += rows
    return out


def _ffn_fwd(x, g_pre, w_in, w_down, tag):
    h = _rms_fwd(x, g_pre, f"{tag}_pre")
    u, a = _mm_swiglu(h, w_in, f"{tag}_up")
    z = _mm(a, w_down, "nn", F32, f"{tag}_down", tk=1408)
    return h, u, a, z


def _ffn_bwd(saved, x, g_pre, w_in, w_down, g_post, dx_out, tag):
    h, u, a, z = saved
    dz, dg_post = _rms_bwd(z, g_post, dx_out, 0.5, f"{tag}_post_bwd", BF16)
    dw_down = _mm(a, dz, "tn", F32, f"{tag}_down_dw", tm=1408)
    du = _mm_swiglu_bwd(dz, w_down, u, f"{tag}_down_dx")
    dh = _mm(du, w_in, "nt", F32, f"{tag}_up_dx", tk=5632)
    dw_in = _mm(h, du, "tn", F32, f"{tag}_up_dw")
    dx, dg_pre = _rms_bwd(x, g_pre, dh, 1.0, f"{tag}_pre_bwd", F32, resid=dx_out)
    return dx, dg_pre, dg_post, dw_in, dw_down


def _step_local(x, mem, target, W, S):
    T, D = x.shape
    gW, gS = {}, {}

    f1 = _ffn_fwd(x, S["ffn1_pre_g"], W["ffn1_w_in"], W["ffn1_w_down"], "ffn1")
    x1 = _resid_rms(x, f1[3], S["ffn1_post_g"], 0.5, "ffn1_post")

    h2 = _rms_fwd(x1, S["mix_pre_g"], "mix_pre")
    pm = _mm(h2, W["w_main"], "nn", BF16, "mix_proj_main")
    pf = _mm(h2, W["w_f"], "nn", F32, "mix_proj_f")
    pg = _mm(h2, W["w_gates"], "nn", F32, "mix_proj_gates")
    lbl = S["hg_lb_logits"].reshape(2, HEADS, 1, DH)
    o_a, states = _hgrn_fwd(pm, lbl, "hgrn_fwd")
    oan = _hgout_fwd(o_a, pm, S["hg_norm_g"], "hgrn_out")
    bias = jnp.pad(S["fox_f_bias"], ((0, 0), (0, LANES - HEADS)))
    c = _fox_cumsum(pf, bias, "fox_cumsum")
    c_heads = c[:, :HEADS].T
    c_col, c_row = c_heads[:, :, None], c_heads[:, None, :]
    o_b, lse = _fox_fwd(pm, c_col, c_row, "fox_fwd")
    ya = _mm(oan, W["w_branch_a"], "nn", F32, "branch_a")
    yb = _mm(o_b, W["w_branch_b"], "nn", F32, "branch_b")
    y = _merge_fwd(ya, yb, pg, S["b_gate"], "merge")
    z2 = _mm(y, W["w_out"], "nn", F32, "mix_out")
    x2 = _resid_rms(x1, z2, S["mix_post_g"], 1.0, "mix_post")

    h3 = _rms_fwd(x2, S["mem_pre_g"], "mem_pre")
    memn = _rms_fwd(mem, S["mem_kv_g"], "mem_kv_norm")
    qm = _mm(h3, W["w_mq"], "nn", BF16, "mem_q")
    kv = _mm(memn, W["w_mkv"], "nn", BF16, "mem_kv")
    om = _xattn_fwd(qm, kv, "mem_attn")
    z3 = _mm(om, W["w_mo"], "nn", F32, "mem_o")
    x3 = _resid_rms(x2, z3, S["mem_post_g"], 1.0, "mem_post")

    f2 = _ffn_fwd(x3, S["ffn2_pre_g"], W["ffn2_w_in"], W["ffn2_w_down"], "ffn2")
    dx4, sq = _final_loss(x3, f2[3], S["ffn2_post_g"], 0.5, target, "loss")

    dx3, gS["ffn2_pre_g"], gS["ffn2_post_g"], gW["ffn2_w_in"], gW["ffn2_w_down"] = _ffn_bwd(
        f2, x3, S["ffn2_pre_g"], W["ffn2_w_in"], W["ffn2_w_down"], S["ffn2_post_g"], dx4, "ffn2")

    dz3, gS["mem_post_g"] = _rms_bwd(z3, S["mem_post_g"], dx3, 1.0, "mem_post_bwd", BF16)
    dom = _mm(dz3, W["w_mo"], "nt", BF16, "mem_o_dx")
    gW["w_mo"] = _mm(om, dz3, "tn", F32, "mem_o_dw")
    dqm, dkv = _xattn_bwd(qm, kv, dom, "mem_attn_bwd")
    dh3 = _mm(dqm, W["w_mq"], "nt", F32, "mem_q_dx")
    gW["w_mq"] = _mm(h3, dqm, "tn", F32, "mem_q_dw")
    dkvb = dkv.astype(BF16)
    gW["w_mkv"] = _mm(memn, dkvb, "tn", F32, "mem_kv_dw")
    dmemn = _mm(dkvb, W["w_mkv"], "nt", F32, "mem_kv_dx")
    _, gS["mem_kv_g"] = _rms_bwd(mem, S["mem_kv_g"], dmemn, 1.0, "mem_kv_norm_bwd", BF16)
    dx2, gS["mem_pre_g"] = _rms_bwd(x2, S["mem_pre_g"], dh3, 1.0, "mem_pre_bwd", F32, resid=dx3)

    dz2, gS["mix_post_g"] = _rms_bwd(z2, S["mix_post_g"], dx2, 1.0, "mix_post_bwd", BF16)
    dy = _mm(dz2, W["w_out"], "nt", F32, "mix_out_dx")
    gW["w_out"] = _mm(y, dz2, "tn", F32, "mix_out_dw")
    dya, dyb, dpg, gS["b_gate"] = _merge_bwd(dy, ya, yb, pg, S["b_gate"], "merge_bwd")
    doan = _mm(dya, W["w_branch_a"], "nt", F32, "branch_a_dx")
    gW["w_branch_a"] = _mm(oan, dya, "tn", F32, "branch_a_dw")
    dob = _mm(dyb, W["w_branch_b"], "nt", BF16, "branch_b_dx")
    gW["w_branch_b"] = _mm(o_b, dyb, "tn", F32, "branch_b_dw")

    delta = _fox_delta(dob, o_b, "fox_delta")
    dq_b, dk_b, dv_b, dc_col, dc_row = _fox_bwd(pm, c_col, c_row, dob, lse, delta, "fox_bwd")
    dc = jnp.pad((dc_col.reshape(HEADS, T) + dc_row.reshape(HEADS, T)).T, ((0, 0), (0, LANES - HEADS)))
    dpf, dbias = _fox_dcum(dc, pf, bias, "fox_cumsum_bwd")
    gS["fox_f_bias"] = dbias[:, :HEADS]

    do_a, dg_a, gS["hg_norm_g"] = _hgout_bwd(o_a, pm, S["hg_norm_g"], doan, "hgrn_out_bwd")
    dq_a, df_a, di_a, dlbl = _hgrn_bwd(pm, lbl, states, do_a, "hgrn_bwd")
    gS["hg_lb_logits"] = dlbl.reshape(2, HEADS, DH)

    dpm = jnp.concatenate([dq_a, df_a, di_a, dg_a, dq_b, dk_b, dv_b], axis=1)
    dpf16 = dpf.astype(BF16)
    dh2 = _mm(dpm, W["w_main"], "nt", F32, "mix_proj_main_dx")
    dh2 = _mm(dpg, W["w_gates"], "nt", F32, "mix_proj_gates_dx", add=dh2)
    dh2 = _mm(dpf16, W["w_f"], "nt", F32, "mix_proj_f_dx", add=dh2)
    gW["w_main"] = _mm(h2, dpm, "tn", F32, "mix_proj_main_dw")
    gW["w_gates"] = _mm(h2, dpg, "tn", F32, "mix_proj_gates_dw")
    gW["w_f"] = _mm(h2, dpf16, "tn", F32, "mix_proj_f_dw")
    dx1, gS["mix_pre_g"] = _rms_bwd(x1, S["mix_pre_g"], dh2, 1.0, "mix_pre_bwd", F32, resid=dx2)

    dx0, gS["ffn1_pre_g"], gS["ffn1_post_g"], gW["ffn1_w_in"], gW["ffn1_w_down"] = _ffn_bwd(
        f1, x, S["ffn1_pre_g"], W["ffn1_w_in"], W["ffn1_w_down"], S["ffn1_post_g"], dx1, "ffn1")
    return sq, dx0, gW, gS


def _train_step(a):
    c_idx = lax.axis_index("c")
    x, mem, target = a["x"][0], a["mem"][0], a["loss_target"][0]
    D = x.shape[1]
    shards = {n: a[n][0] for n in BIG}

    fox_scale = 1.0 / math.sqrt(DH)
    n_mine = shards["w_in"].shape[1]
    dev = 4 * lax.axis_index("x") + 2 * lax.axis_index("y") + c_idx
    cols = dev * n_mine + jnp.arange(n_mine)
    is_fox_q = (cols >= 4 * D) & (cols < 5 * D)
    sent = dict(shards, w_in=shards["w_in"] * jnp.where(is_fox_q, fox_scale, 1.0)[None, :])
    gathered = _ag_sibling(_ag_chips([sent[n].astype(BF16) for n in BIG], "ag_chips"), "ag_sibling")
    W = {n: _full_from_gathered(g, n) for n, g in zip(BIG, gathered)}
    n_main = 7 * D
    w_in = W.pop("w_in")
    W["w_main"] = w_in[:, :n_main]
    W["w_f"] = jnp.pad(w_in[:, n_main:n_main + HEADS], ((0, 0), (0, LANES - HEADS)))
    W["w_gates"] = w_in[:, n_main + HEADS:]
    W["ffn1_w_in"] = _swiglu_interleave(W["ffn1_w_in"])
    W["ffn2_w_in"] = _swiglu_interleave(W["ffn2_w_in"])
    S = {n: a[n] for n in SMALL}

    sq, grad_x, gW, gS = _step_local(x, mem, target, W, S)

    g_main = gW.pop("w_main")
    gW["w_in"] = jnp.concatenate([g_main[:, :4 * D], g_main[:, 4 * D:5 * D] * fox_scale, g_main[:, 5 * D:],
                                  gW.pop("w_f")[:, :HEADS], gW.pop("w_gates")], axis=1)
    gW["ffn1_w_in"] = _swiglu_deinterleave(gW["ffn1_w_in"])
    gW["ffn2_w_in"] = _swiglu_deinterleave(gW["ffn2_w_in"])
    blocks = [_blocks_from_full(gW[n], n, shards[n].shape) for n in BIG]
    landed = _rs_sibling(blocks, "rs_sibling")
    core = c_idx.astype(jnp.int32).reshape(1)
    pairs = [_pair_add(b, l, core, BF16, f"rs_pair_add_{n}") for n, b, l in zip(BIG, blocks, landed)]
    slots = _rs_chips(pairs, "rs_chips")
    big = {n: _adamw_reduce(s, shards[n], a["m_" + n][0], a["v_" + n][0], f"adamw_{n}") for n, s in zip(BIG, slots)}

    loss_row = jnp.pad(sq[:1, :1] * (0.5 / D), ((0, 0), (0, PACK_COLS - 1)))
    small_all = _ag_sibling(_ag_chips([_pack_small(gS, loss_row)], "small_ag_chips"), "small_ag_sibling")[0]
    small_slots = small_all.reshape(8, SMALL_ROWS, PACK_COLS)
    zero_row = jnp.zeros((1, PACK_COLS), F32)
    g_sm, d_sm, m_sm, v_sm = _adamw_reduce(
        small_slots, _pack_small({n: a[n] for n in SMALL}, zero_row),
        _pack_small({n: a["m_" + n] for n in SMALL}, zero_row),
        _pack_small({n: a["v_" + n] for n in SMALL}, zero_row), "adamw_small")

    def unpack(which, small):
        out = _unpack_small(small, {n: a[n] for n in SMALL})
        for n in BIG:
            out[n] = big[n][which][None]
        return [out[n] for n in WEIGHTS]

    loss = g_sm[SMALL_ROWS - 1, 0]
    return (loss, grad_x[None], *unpack(0, g_sm), *unpack(1, d_sm), *unpack(2, m_sm), *unpack(3, v_sm))


def kernel(x, mem, ffn1_pre_g, ffn1_w_in, ffn1_w_down, ffn1_post_g, mix_pre_g, w_in, hg_lb_logits, hg_norm_g, fox_f_bias, w_branch_a, w_branch_b, b_gate, w_out, mix_post_g, mem_pre_g, mem_kv_g, w_mq, w_mkv, w_mo, mem_post_g, ffn2_pre_g, ffn2_w_in, ffn2_w_down, ffn2_post_g, loss_target, m_ffn1_pre_g, m_ffn1_w_in, m_ffn1_w_down, m_ffn1_post_g, m_mix_pre_g, m_w_in, m_hg_lb_logits, m_hg_norm_g, m_fox_f_bias, m_w_branch_a, m_w_branch_b, m_b_gate, m_w_out, m_mix_post_g, m_mem_pre_g, m_mem_kv_g, m_w_mq, m_w_mkv, m_w_mo, m_mem_post_g, m_ffn2_pre_g, m_ffn2_w_in, m_ffn2_w_down, m_ffn2_post_g, v_ffn1_pre_g, v_ffn1_w_in, v_ffn1_w_down, v_ffn1_post_g, v_mix_pre_g, v_w_in, v_hg_lb_logits, v_hg_norm_g, v_fox_f_bias, v_w_branch_a, v_w_branch_b, v_b_gate, v_w_out, v_mix_post_g, v_mem_pre_g, v_mem_kv_g, v_w_mq, v_w_mkv, v_w_mo, v_mem_post_g, v_ffn2_pre_g, v_ffn2_w_in, v_ffn2_w_down, v_ffn2_post_g):
    return _train_step(dict(locals()))
```

```python
import functools
import math

import jax
import jax.numpy as jnp
from jax import lax
from jax.experimental import pallas as pl
from jax.experimental.pallas import tpu as pltpu

F32 = jnp.float32
BF16 = jnp.bfloat16
MESH = pl.DeviceIdType.MESH

EPS = 1e-6
HEADS = 8
DH = 128
MEM_HEADS = 4
CHUNK = 128
HALF = CHUNK // 2
SWIGLU_TILE = 256
LANES = 128
PACK_COLS = 1024
ROW_TILE = 512
SEQ_BLOCK = 512
ATTN_TILE = 1024
ATTN_ROWS = 256
EXP_CLAMP = 80.0
NEG_BIG = -1e30

ADAM_LR, ADAM_B1, ADAM_B2, ADAM_EPS, ADAM_WD, ADAM_STEP = 0.001, 0.9, 0.999, 1e-08, 0.01, 10

VMEM_LIMIT = 48 * 1024 * 1024

_DN = {
    "nn": (((1,), (0,)), ((), ())),
    "nt": (((1,), (1,)), ((), ())),
    "tn": (((0,), (0,)), ((), ())),
}

BIG = ["ffn1_w_in", "ffn1_w_down", "w_in", "w_branch_a", "w_branch_b", "w_out", "w_mq", "w_mkv", "w_mo",
       "ffn2_w_in", "ffn2_w_down"]
COL_SHARDED = {"ffn1_w_in", "w_in", "w_mkv", "ffn2_w_in"}
SMALL = ["ffn1_pre_g", "ffn1_post_g", "mix_pre_g", "hg_lb_logits", "hg_norm_g", "fox_f_bias", "b_gate",
         "mix_post_g", "mem_pre_g", "mem_kv_g", "mem_post_g", "ffn2_pre_g", "ffn2_post_g"]
WEIGHTS = ["ffn1_pre_g", "ffn1_w_in", "ffn1_w_down", "ffn1_post_g", "mix_pre_g", "w_in", "hg_lb_logits",
           "hg_norm_g", "fox_f_bias", "w_branch_a", "w_branch_b", "b_gate", "w_out", "mix_post_g", "mem_pre_g",
           "mem_kv_g", "w_mq", "w_mkv", "w_mo", "mem_post_g", "ffn2_pre_g", "ffn2_w_in", "ffn2_w_down",
           "ffn2_post_g"]


def _dot(a, b, mode="nn"):
    return lax.dot_general(a, b, _DN[mode], preferred_element_type=F32)


def _sig(x):
    return 1.0 / (1.0 + jnp.exp(-x))


def _params(*dims):
    return pltpu.CompilerParams(dimension_semantics=dims if dims else None, vmem_limit_bytes=VMEM_LIMIT)


def _tile(dim, pref):
    if dim <= pref:
        return dim
    t = (pref // LANES) * LANES
    while t >= LANES:
        if dim % t == 0:
            return t
        t -= LANES
    raise ValueError(f"no tile for {dim}")


def _colsum(x):
    return jnp.sum(x, axis=0, keepdims=True)


def _rowsum(x):
    return jnp.sum(x, axis=1, keepdims=True)


def _iota(shape, axis):
    return lax.broadcasted_iota(jnp.int32, shape, axis)


def _pick_row(x, r):
    return _colsum(jnp.where(_iota(x.shape, 0) == r, x, 0.0))


def _tri_dot(tri, x):
    hi = x.astype(BF16)
    r1 = x - hi.astype(F32)
    mid = r1.astype(BF16)
    lo = (r1 - mid.astype(F32)).astype(BF16)
    return _dot(tri, hi) + _dot(tri, mid) + _dot(tri, lo)


_MM_TILES = {"nn": (2048, 512, 1024), "nt": (512, 1024, 4096), "tn": (1024, 1024, 2048)}


def _mm(a, b, mode, out_dtype, name, add=None, tm=None, tn=None, tk=None):
    tm, tn, tk = (given or pref for given, pref in zip((tm, tn, tk), _MM_TILES[mode]))
    if mode == "nn":
        (M, K), (K2, N) = a.shape, b.shape
    elif mode == "nt":
        (M, K), (N, K2) = a.shape, b.shape
    else:
        (K, M), (K2, N) = a.shape, b.shape
    assert K == K2, (name, a.shape, b.shape)
    tm, tn, tk = _tile(M, tm), _tile(N, tn), _tile(K, tk)
    nk = K // tk
    if mode == "tn":
        a_spec = pl.BlockSpec((tk, tm), lambda i, j, k: (k, i))
    else:
        a_spec = pl.BlockSpec((tm, tk), lambda i, j, k: (i, k))
    if mode == "nt":
        b_spec = pl.BlockSpec((tn, tk), lambda i, j, k: (j, k))
    else:
        b_spec = pl.BlockSpec((tk, tn), lambda i, j, k: (k, j))
    o_spec = pl.BlockSpec((tm, tn), lambda i, j, k: (i, j))
    has_add = add is not None

    def body(*refs):
        a_ref, b_ref = refs[0], refs[1]
        c_ref = refs[2] if has_add else None
        o_ref = refs[3] if has_add else refs[2]
        part = _dot(a_ref[...], b_ref[...], mode)
        if nk == 1:
            if has_add:
                part = part + c_ref[...]
            o_ref[...] = part.astype(o_ref.dtype)
            return
        acc_ref = refs[-1]
        k = pl.program_id(2)

        @pl.when(k == 0)
        def _():
            acc_ref[...] = part + c_ref[...] if has_add else part

        @pl.when(k > 0)
        def _():
            acc_ref[...] += part

        @pl.when(k == nk - 1)
        def _():
            o_ref[...] = acc_ref[...].astype(o_ref.dtype)

    in_specs = [a_spec, b_spec] + ([o_spec] if has_add else [])
    args = (a, b) + ((add,) if has_add else ())
    return pl.pallas_call(
        body, name=name, grid=(M // tm, N // tn, nk), in_specs=in_specs, out_specs=o_spec,
        out_shape=jax.ShapeDtypeStruct((M, N), out_dtype),
        scratch_shapes=[pltpu.VMEM((tm, tn), F32)] if nk > 1 else [],
        compiler_params=_params("parallel", "parallel", "arbitrary"),
    )(*args)


def _rms_fwd(x, g, name, out_dtype=BF16):
    T, D = x.shape
    tr = _tile(T, ROW_TILE)

    def body(x_ref, g_ref, o_ref):
        xv = x_ref[...]
        r = lax.rsqrt(jnp.mean(xv * xv, axis=-1, keepdims=True) + EPS)
        o_ref[...] = (xv * r * g_ref[...]).astype(o_ref.dtype)

    return pl.pallas_call(
        body, name=name, grid=(T // tr,),
        in_specs=[pl.BlockSpec((tr, D), lambda i: (i, 0)), pl.BlockSpec((1, D), lambda i: (0, 0))],
        out_specs=pl.BlockSpec((tr, D), lambda i: (i, 0)),
        out_shape=jax.ShapeDtypeStruct((T, D), out_dtype), compiler_params=_params("parallel"),
    )(x, g)


def _resid_rms(x, z, g, scale, name):
    T, D = x.shape
    tr = _tile(T, ROW_TILE)

    def body(x_ref, z_ref, g_ref, o_ref):
        zv = z_ref[...]
        r = lax.rsqrt(jnp.mean(zv * zv, axis=-1, keepdims=True) + EPS)
        o_ref[...] = x_ref[...] + scale * (zv * r * g_ref[...])

    row = pl.BlockSpec((tr, D), lambda i: (i, 0))
    return pl.pallas_call(
        body, name=name, grid=(T // tr,), in_specs=[row, row, pl.BlockSpec((1, D), lambda i: (0, 0))],
        out_specs=row, out_shape=jax.ShapeDtypeStruct((T, D), F32), compiler_params=_params("parallel"),
    )(x, z, g)


def _final_loss(x, z, g, scale, target, name):
    T, D = x.shape
    tr = _tile(T, ROW_TILE)

    def body(x_ref, z_ref, g_ref, t_ref, dx_ref, acc_ref):
        @pl.when(pl.program_id(0) == 0)
        def _():
            acc_ref[...] = jnp.zeros_like(acc_ref)

        zv = z_ref[...]
        r = lax.rsqrt(jnp.mean(zv * zv, axis=-1, keepdims=True) + EPS)
        e = x_ref[...] + scale * (zv * r * g_ref[...]) - t_ref[...]
        dx_ref[...] = e * (1.0 / D)
        acc_ref[...] += _colsum(_rowsum(e * e))

    row = pl.BlockSpec((tr, D), lambda i: (i, 0))
    return pl.pallas_call(
        body, name=name, grid=(T // tr,), in_specs=[row, row, pl.BlockSpec((1, D), lambda i: (0, 0)), row],
        out_specs=[row, pl.BlockSpec((8, LANES), lambda i: (0, 0))],
        out_shape=[jax.ShapeDtypeStruct((T, D), F32), jax.ShapeDtypeStruct((8, LANES), F32)],
        compiler_params=_params("arbitrary"),
    )(x, z, g, target)


def _rms_bwd(xin, g, dy, scale, name, out_dtype, resid=None):
    T, D = xin.shape
    tr = _tile(T, ROW_TILE)
    has_resid = resid is not None

    def body(*refs):
        x_ref, g_ref, dy_ref = refs[:3]
        r_ref = refs[3] if has_resid else None
        dx_ref, dg_ref = refs[-2], refs[-1]

        @pl.when(pl.program_id(0) == 0)
        def _():
            dg_ref[...] = jnp.zeros_like(dg_ref)

        xv = x_ref[...]
        r = lax.rsqrt(jnp.mean(xv * xv, axis=-1, keepdims=True) + EPS)
        xh = xv * r
        dyv = dy_ref[...].astype(F32) * scale
        dxh = dyv * g_ref[...]
        dx = r * (dxh - xh * jnp.mean(dxh * xh, axis=-1, keepdims=True))
        if has_resid:
            dx = dx + r_ref[...]
        dx_ref[...] = dx.astype(dx_ref.dtype)
        dg_ref[...] += _colsum(dyv * xh)

    row = pl.BlockSpec((tr, D), lambda i: (i, 0))
    vec = pl.BlockSpec((1, D), lambda i: (0, 0))
    return pl.pallas_call(
        body, name=name, grid=(T // tr,), in_specs=[row, vec, row] + ([row] if has_resid else []),
        out_specs=[row, vec],
        out_shape=[jax.ShapeDtypeStruct((T, D), out_dtype), jax.ShapeDtypeStruct((1, D), F32)],
        compiler_params=_params("arbitrary"),
    )(*((xin, g, dy) + ((resid,) if has_resid else ())))


def _mm_swiglu(h, w_in, name):
    T, K = h.shape
    F2 = w_in.shape[1]
    tf = SWIGLU_TILE
    tm = _tile(T, _MM_TILES["nn"][0])

    def body(h_ref, w_ref, u_ref, a_ref):
        u = _dot(h_ref[...], w_ref[...])
        u_ref[...] = u.astype(u_ref.dtype)
        gate, up = u[:, :tf], u[:, tf:]
        a_ref[...] = (gate * _sig(gate) * up).astype(a_ref.dtype)

    return pl.pallas_call(
        body, name=name, grid=(T // tm, F2 // (2 * tf)),
        in_specs=[pl.BlockSpec((tm, K), lambda i, j: (i, 0)), pl.BlockSpec((K, 2 * tf), lambda i, j: (0, j))],
        out_specs=[pl.BlockSpec((tm, 2 * tf), lambda i, j: (i, j)), pl.BlockSpec((tm, tf), lambda i, j: (i, j))],
        out_shape=[jax.ShapeDtypeStruct((T, F2), BF16), jax.ShapeDtypeStruct((T, F2 // 2), BF16)],
        compiler_params=_params("parallel", "parallel"),
    )(h, w_in)


def _mm_swiglu_bwd(dz, w_down, u, name):
    T, D = dz.shape
    F = w_down.shape[0]
    tf = SWIGLU_TILE
    tm = _tile(T, _MM_TILES["nn"][0])

    def body(dz_ref, w_ref, u_ref, o_ref):
        d = _dot(dz_ref[...], w_ref[...], "nt")
        gate = u_ref[:, :tf].astype(F32)
        up = u_ref[:, tf:].astype(F32)
        s = _sig(gate)
        o_ref[:, :tf] = (d * up * (s * (1.0 + gate * (1.0 - s)))).astype(o_ref.dtype)
        o_ref[:, tf:] = (d * gate * s).astype(o_ref.dtype)

    return pl.pallas_call(
        body, name=name, grid=(T // tm, F // tf),
        in_specs=[pl.BlockSpec((tm, D), lambda i, j: (i, 0)), pl.BlockSpec((tf, D), lambda i, j: (j, 0)),
                  pl.BlockSpec((tm, 2 * tf), lambda i, j: (i, j))],
        out_specs=pl.BlockSpec((tm, 2 * tf), lambda i, j: (i, j)),
        out_shape=jax.ShapeDtypeStruct((T, 2 * F), BF16), compiler_params=_params("parallel", "parallel"),
    )(dz, w_down, u)


def _hgout_fwd(o_a, pm, g, name):
    T, D = o_a.shape
    tr = _tile(T, ROW_TILE)

    def body(o_ref, ga_ref, g_ref, out_ref):
        ov = o_ref[...]
        r = lax.rsqrt(jnp.mean(ov * ov, axis=-1, keepdims=True) + EPS)
        ga = ga_ref[...].astype(F32)
        out_ref[...] = (ov * r * g_ref[...] * (ga * _sig(ga))).astype(out_ref.dtype)

    row = pl.BlockSpec((tr, D), lambda i: (i, 0))
    return pl.pallas_call(
        body, name=name, grid=(T // tr,),
        in_specs=[row, pl.BlockSpec((tr, D), lambda i: (i, 3)), pl.BlockSpec((1, D), lambda i: (0, 0))],
        out_specs=row, out_shape=jax.ShapeDtypeStruct((T, D), BF16), compiler_params=_params("parallel"),
    )(o_a, pm, g)


def _hgout_bwd(o_a, pm, g, d_out, name):
    T, D = o_a.shape
    tr = _tile(T, ROW_TILE)

    def body(o_ref, ga_ref, g_ref, d_ref, do_ref, dga_ref, dg_ref):
        @pl.when(pl.program_id(0) == 0)
        def _():
            dg_ref[...] = jnp.zeros_like(dg_ref)

        ov = o_ref[...]
        r = lax.rsqrt(jnp.mean(ov * ov, axis=-1, keepdims=True) + EPS)
        oh = ov * r
        ga = ga_ref[...].astype(F32)
        s = _sig(ga)
        d = d_ref[...].astype(F32)
        dn = d * (ga * s)
        dga_ref[...] = (d * (oh * g_ref[...]) * (s * (1.0 + ga * (1.0 - s)))).astype(dga_ref.dtype)
        dxh = dn * g_ref[...]
        do_ref[...] = (r * (dxh - oh * jnp.mean(dxh * oh, axis=-1, keepdims=True))).astype(do_ref.dtype)
        dg_ref[...] += _colsum(dn * oh)

    row = pl.BlockSpec((tr, D), lambda i: (i, 0))
    vec = pl.BlockSpec((1, D), lambda i: (0, 0))
    return pl.pallas_call(
        body, name=name, grid=(T // tr,), in_specs=[row, pl.BlockSpec((tr, D), lambda i: (i, 3)), vec, row],
        out_specs=[row, row, vec],
        out_shape=[jax.ShapeDtypeStruct((T, D), BF16), jax.ShapeDtypeStruct((T, D), BF16),
                   jax.ShapeDtypeStruct((1, D), F32)],
        compiler_params=_params("arbitrary"),
    )(o_a, pm, g, d_out)


def _merge_fwd(ya, yb, pg, bg, name):
    T, D = ya.shape
    tr = _tile(T, 256)

    def body(ya_ref, yb_ref, pg_ref, bg_ref, o_ref):
        g0 = _sig(pg_ref[:, :D] + bg_ref[:, :D])
        g1 = _sig(pg_ref[:, D:] + bg_ref[:, D:])
        o_ref[...] = (g0 * ya_ref[...] + g1 * yb_ref[...]).astype(o_ref.dtype)

    row = pl.BlockSpec((tr, D), lambda i: (i, 0))
    return pl.pallas_call(
        body, name=name, grid=(T // tr,),
        in_specs=[row, row, pl.BlockSpec((tr, 2 * D), lambda i: (i, 0)), pl.BlockSpec((1, 2 * D), lambda i: (0, 0))],
        out_specs=row, out_shape=jax.ShapeDtypeStruct((T, D), BF16), compiler_params=_params("parallel"),
    )(ya, yb, pg, bg)


def _merge_bwd(dy, ya, yb, pg, bg, name):
    T, D = ya.shape
    tr = _tile(T, 256)

    def body(dy_ref, ya_ref, yb_ref, pg_ref, bg_ref, dya_ref, dyb_ref, dpg_ref, dbg_ref):
        @pl.when(pl.program_id(0) == 0)
        def _():
            dbg_ref[...] = jnp.zeros_like(dbg_ref)

        d = dy_ref[...].astype(F32)
        g0 = _sig(pg_ref[:, :D] + bg_ref[:, :D])
        g1 = _sig(pg_ref[:, D:] + bg_ref[:, D:])
        dya_ref[...] = (d * g0).astype(dya_ref.dtype)
        dyb_ref[...] = (d * g1).astype(dyb_ref.dtype)
        dg0 = d * ya_ref[...] * (g0 * (1.0 - g0))
        dg1 = d * yb_ref[...] * (g1 * (1.0 - g1))
        dpg_ref[:, :D] = dg0.astype(dpg_ref.dtype)
        dpg_ref[:, D:] = dg1.astype(dpg_ref.dtype)
        dbg_ref[:, :D] += _colsum(dg0)
        dbg_ref[:, D:] += _colsum(dg1)

    row = pl.BlockSpec((tr, D), lambda i: (i, 0))
    wide = pl.BlockSpec((tr, 2 * D), lambda i: (i, 0))
    wvec = pl.BlockSpec((1, 2 * D), lambda i: (0, 0))
    return pl.pallas_call(
        body, name=name, grid=(T // tr,), in_specs=[row, row, row, wide, wvec],
        out_specs=[row, row, wide, wvec],
        out_shape=[jax.ShapeDtypeStruct((T, D), BF16), jax.ShapeDtypeStruct((T, D), BF16),
                   jax.ShapeDtypeStruct((T, 2 * D), BF16), jax.ShapeDtypeStruct((1, 2 * D), F32)],
        compiler_params=_params("arbitrary"),
    )(dy, ya, yb, pg, bg)


def _hgrn_chunk_terms(q, fl, lb, tri):
    shape = q.shape
    row = _iota(shape, 0)
    sg = _sig(fl)
    f = lb + (1.0 - lb) * sg
    k = 1.0 - f
    b = _tri_dot(tri, jnp.log(f))
    ref1 = jnp.where(row < HALF, _pick_row(b, HALF // 2), _pick_row(b, HALF + HALF // 2))
    b_half = _pick_row(b, HALF - 1)
    b_last = _pick_row(b, CHUNK - 1)
    sq = _sig(q)
    qs = q * sq
    e_q1 = jnp.exp(jnp.minimum(b - ref1, EXP_CLAMP))
    e_k1 = jnp.exp(jnp.minimum(ref1 - b, EXP_CLAMP))
    e_q2 = jnp.exp(jnp.minimum(b - b_half, 0.0))
    e_k2 = jnp.exp(jnp.minimum(b_half - b, 0.0))
    e_b = jnp.exp(b)
    e_kd = jnp.exp(b_last - b)
    return dict(sg=sg, f=f, k=k, sq=sq, qs=qs, e_q1=e_q1, e_k1=e_k1, e_q2=e_q2, e_k2=e_k2, e_b=e_b, e_kd=e_kd,
                e_last=jnp.exp(b_last))


def _hgrn_masks():
    r = _iota((CHUNK, CHUNK), 0)
    c = _iota((CHUNK, CHUNK), 1)
    causal = r >= c
    same = (r < HALF) == (c < HALF)
    return causal, causal & same, (r >= HALF) & (c < HALF)


def _softmax_lb(lbl_ref):
    l0, l1 = lbl_ref[0, 0], lbl_ref[1, 0]
    mx = jnp.maximum(l0, l1)
    e0, e1 = jnp.exp(l0 - mx), jnp.exp(l1 - mx)
    return e0 / (e0 + e1)


def _hgrn_fwd(pm, lbl, name):
    T = pm.shape[0]
    tb = _tile(T, SEQ_BLOCK)
    nc = tb // CHUNK

    def body(q_ref, f_ref, i_ref, lbl_ref, o_ref, st_ref, s_sc):
        @pl.when(pl.program_id(1) == 0)
        def _():
            s_sc[...] = jnp.zeros_like(s_sc)

        lb = _softmax_lb(lbl_ref)
        causal, m1, m2 = _hgrn_masks()
        tri = jnp.where(causal, 1.0, 0.0).astype(BF16)
        parts = []
        for ci in range(nc):
            sl = pl.ds(ci * CHUNK, CHUNK)
            t = _hgrn_chunk_terms(q_ref[sl, :].astype(F32), f_ref[sl, :].astype(F32), lb, tri)
            iv = i_ref[sl, :]
            a1 = _dot((t["qs"] * t["e_q1"]).astype(BF16), (t["k"] * t["e_k1"]).astype(BF16), "nt")
            a2 = _dot((t["qs"] * t["e_q2"]).astype(BF16), (t["k"] * t["e_k2"]).astype(BF16), "nt")
            a = jnp.where(m1, a1, 0.0) + jnp.where(m2, a2, 0.0)
            parts.append((_dot(a.astype(BF16), iv), (t["qs"] * t["e_b"]).astype(BF16),
                          _dot(iv, (t["k"] * t["e_kd"]).astype(BF16), "tn"), t["e_last"]))
        st = s_sc[...]
        for ci, (o_intra, qi, grow, e_last) in enumerate(parts):
            st_ref[0, ci] = st
            o_ref[pl.ds(ci * CHUNK, CHUNK), :] = o_intra + _dot(qi, st.astype(BF16), "nt")
            st = e_last * st + grow
        s_sc[...] = st

    blk = lambda off: pl.BlockSpec((tb, DH), lambda h, b: (b, off + h))
    return pl.pallas_call(
        body, name=name, grid=(HEADS, T // tb),
        in_specs=[blk(0), blk(HEADS), blk(2 * HEADS), pl.BlockSpec((2, 1, 1, DH), lambda h, b: (0, h, 0, 0))],
        out_specs=[pl.BlockSpec((tb, DH), lambda h, b: (b, h)),
                   pl.BlockSpec((1, nc, DH, DH), lambda h, b: (h, b, 0, 0))],
        out_shape=[jax.ShapeDtypeStruct((T, HEADS * DH), F32),
                   jax.ShapeDtypeStruct((HEADS, T // CHUNK, DH, DH), F32)],
        scratch_shapes=[pltpu.VMEM((DH, DH), F32)],
        compiler_params=_params("parallel", "arbitrary"),
    )(pm, pm, pm, lbl)


def _hgrn_bwd(pm, lbl, states, do, name):
    T = pm.shape[0]
    tb = _tile(T, SEQ_BLOCK)
    nc = tb // CHUNK
    nb = T // tb

    def body(q_ref, f_ref, i_ref, lbl_ref, st_ref, do_ref, dq_ref, df_ref, di_ref, dl_ref, ds_sc, dlb_sc):
        @pl.when(pl.program_id(1) == 0)
        def _():
            ds_sc[...] = jnp.zeros_like(ds_sc)
            dlb_sc[...] = jnp.zeros_like(dlb_sc)

        lb = _softmax_lb(lbl_ref)
        causal, m1, m2 = _hgrn_masks()
        tri = jnp.where(causal, 1.0, 0.0).astype(BF16)
        tri_rev = jnp.where(_iota((CHUNK, CHUNK), 0) <= _iota((CHUNK, CHUNK), 1), 1.0, 0.0).astype(BF16)
        last_row = _iota((CHUNK, DH), 0) == CHUNK - 1
        for ci in reversed(range(nc)):
            sl = pl.ds(ci * CHUNK, CHUNK)
            q = q_ref[sl, :].astype(F32)
            t = _hgrn_chunk_terms(q, f_ref[sl, :].astype(F32), lb, tri)
            iv = i_ref[sl, :]
            dov = do_ref[sl, :]
            qe1, ke1 = t["qs"] * t["e_q1"], t["k"] * t["e_k1"]
            qe2, ke2 = t["qs"] * t["e_q2"], t["k"] * t["e_k2"]
            qi, kd = t["qs"] * t["e_b"], t["k"] * t["e_kd"]
            qe1b, ke1b, qe2b, ke2b = qe1.astype(BF16), ke1.astype(BF16), qe2.astype(BF16), ke2.astype(BF16)
            a = jnp.where(m1, _dot(qe1b, ke1b, "nt"), 0.0) + jnp.where(m2, _dot(qe2b, ke2b, "nt"), 0.0)
            st = st_ref[0, ci]
            dsn = ds_sc[...]
            dsnb = dsn.astype(BF16)
            da = _dot(dov, iv, "nt")
            da1 = jnp.where(m1, da, 0.0).astype(BF16)
            da2 = jnp.where(m2, da, 0.0).astype(BF16)
            di_ref[sl, :] = (_dot(a.astype(BF16), dov, "tn") + _dot(kd.astype(BF16), dsnb, "nt")).astype(di_ref.dtype)
            dqe1, dke1 = _dot(da1, ke1b), _dot(da1, qe1b, "tn")
            dqe2, dke2 = _dot(da2, ke2b), _dot(da2, qe2b, "tn")
            dqi = _dot(dov, st.astype(BF16))
            dkd = _dot(iv, dsnb)
            ds_sc[...] = t["e_last"] * dsn + _dot(dov, qi.astype(BF16), "tn")
            dqs = dqe1 * t["e_q1"] + dqe2 * t["e_q2"] + dqi * t["e_b"]
            dk = dke1 * t["e_k1"] + dke2 * t["e_k2"] + dkd * t["e_kd"]
            qib, kdb = qi.astype(BF16).astype(F32), kd.astype(BF16).astype(F32)
            db = (dqe1 * qe1b.astype(F32) - dke1 * ke1b.astype(F32) + dqe2 * qe2b.astype(F32)
                  - dke2 * ke2b.astype(F32) + dqi * qib - dkd * kdb)
            extra = _colsum(dkd * kdb) + t["e_last"] * _colsum(dsn * st)
            db = db + jnp.where(last_row, extra, 0.0)
            dlf = _tri_dot(tri_rev, db)
            dfv = dlf / t["f"] - dk
            sg = t["sg"]
            df_ref[sl, :] = (dfv * (1.0 - lb) * sg * (1.0 - sg)).astype(df_ref.dtype)
            dlb_sc[...] += _colsum(dfv * (1.0 - sg))
            sq = t["sq"]
            dq_ref[sl, :] = (dqs * (sq * (1.0 + q * (1.0 - sq)))).astype(dq_ref.dtype)

        @pl.when(pl.program_id(1) == nb - 1)
        def _():
            dl0 = dlb_sc[...] * lb * (1.0 - lb)
            dl_ref[0, 0] = dl0
            dl_ref[1, 0] = -dl0

    blk = lambda off: pl.BlockSpec((tb, DH), lambda h, b: (nb - 1 - b, off + h))
    lspec = pl.BlockSpec((2, 1, 1, DH), lambda h, b: (0, h, 0, 0))
    out_blk = pl.BlockSpec((tb, DH), lambda h, b: (nb - 1 - b, h))
    D = HEADS * DH
    return pl.pallas_call(
        body, name=name, grid=(HEADS, nb),
        in_specs=[blk(0), blk(HEADS), blk(2 * HEADS), lspec,
                  pl.BlockSpec((1, nc, DH, DH), lambda h, b: (h, nb - 1 - b, 0, 0)), out_blk],
        out_specs=[out_blk, out_blk, out_blk, lspec],
        out_shape=[jax.ShapeDtypeStruct((T, D), BF16)] * 3 + [jax.ShapeDtypeStruct((2, HEADS, 1, DH), F32)],
        scratch_shapes=[pltpu.VMEM((DH, DH), F32), pltpu.VMEM((1, DH), F32)],
        compiler_params=_params("parallel", "arbitrary"),
    )(pm, pm, pm, lbl, states, do)


def _log_sigmoid(x):
    return jnp.minimum(x, 0.0) - jnp.log(1.0 + jnp.exp(-jnp.abs(x)))


def _fox_cumsum(pf, bias, name):
    T = pf.shape[0]
    tb = _tile(T, SEQ_BLOCK)

    def body(x_ref, b_ref, c_ref, carry):
        @pl.when(pl.program_id(0) == 0)
        def _():
            carry[...] = jnp.zeros_like(carry)

        tri = jnp.where(_iota((tb, tb), 0) >= _iota((tb, tb), 1), 1.0, 0.0).astype(BF16)
        c = _tri_dot(tri, _log_sigmoid(x_ref[...] + b_ref[...])) + carry[...]
        c_ref[...] = c
        carry[...] = _pick_row(c, tb - 1)

    row = pl.BlockSpec((tb, LANES), lambda i: (i, 0))
    return pl.pallas_call(
        body, name=name, grid=(T // tb,), in_specs=[row, pl.BlockSpec((1, LANES), lambda i: (0, 0))],
        out_specs=row, out_shape=jax.ShapeDtypeStruct((T, LANES), F32),
        scratch_shapes=[pltpu.VMEM((1, LANES), F32)], compiler_params=_params("arbitrary"),
    )(pf, bias)


def _fox_dcum(dc, pf, bias, name):
    T = pf.shape[0]
    tb = _tile(T, SEQ_BLOCK)
    nb = T // tb

    def body(dc_ref, x_ref, b_ref, dx_ref, db_ref, carry):
        @pl.when(pl.program_id(0) == 0)
        def _():
            carry[...] = jnp.zeros_like(carry)
            db_ref[...] = jnp.zeros_like(db_ref)

        tri_rev = jnp.where(_iota((tb, tb), 0) <= _iota((tb, tb), 1), 1.0, 0.0).astype(BF16)
        dls = _tri_dot(tri_rev, dc_ref[...]) + carry[...]
        carry[...] = _pick_row(dls, 0)
        dx = dls * (1.0 - _sig(x_ref[...] + b_ref[...]))
        dx_ref[...] = dx
        db_ref[...] += _colsum(dx)

    row = pl.BlockSpec((tb, LANES), lambda i: (nb - 1 - i, 0))
    vec = pl.BlockSpec((1, LANES), lambda i: (0, 0))
    return pl.pallas_call(
        body, name=name, grid=(nb,), in_specs=[row, row, vec], out_specs=[row, vec],
        out_shape=[jax.ShapeDtypeStruct((T, LANES), F32), jax.ShapeDtypeStruct((1, LANES), F32)],
        scratch_shapes=[pltpu.VMEM((1, LANES), F32)], compiler_params=_params("arbitrary"),
    )(dc, pf, bias)


_Q_OFF, _K_OFF, _V_OFF = 4 * HEADS, 5 * HEADS, 6 * HEADS


def _causal_pairs(nq, by_key):
    if by_key:
        pairs = [(i, j) for j in range(nq) for i in range(j, nq)]
    else:
        pairs = [(i, j) for i in range(nq) for j in range(i + 1)]
    return jnp.asarray([p[0] for p in pairs], jnp.int32), jnp.asarray([p[1] for p in pairs], jnp.int32)


def _fox_logits(q, k, ck, row0, masked):
    s = _dot(q, k, "nt") - ck
    if masked:
        s = jnp.where(_iota(s.shape, 0) + row0 >= _iota(s.shape, 1), s, NEG_BIG)
    return s


def _ones_column(rows):
    return jnp.where(_iota((rows, DH), 1) == 0, 1.0, 0.0).astype(BF16)


def _fox_fwd(pm, c_col, c_row, name):
    T = pm.shape[0]
    tq = _tile(T, ATTN_TILE)
    nq = T // tq
    rg = min(ATTN_ROWS, tq)
    qi_tab, kj_tab = _causal_pairs(nq, by_key=False)

    def body(qi_ref, kj_ref, q_ref, k_ref, v_ref, cq_ref, ck_ref, o_ref, lse_ref, m_sc, acc_sc):
        t = pl.program_id(1)
        i, j = qi_ref[t], kj_ref[t]

        @pl.when(j == 0)
        def _():
            m_sc[...] = jnp.full_like(m_sc, NEG_BIG)
            acc_sc[...] = jnp.zeros_like(acc_sc)

        def step(diag):
            m_all, acc_all = m_sc[...], acc_sc[...]
            ones = _ones_column(tq)
            ms, accs = [], []
            for r in range(tq // rg):
                rows = slice(r * rg, (r + 1) * rg)
                w = (r + 1) * rg if diag else tq
                cq = cq_ref[0, rows, :]
                s = _fox_logits(q_ref[rows, :], k_ref[:w, :], ck_ref[0, :, :w], r * rg, diag)
                m_old = m_all[rows, :]
                m_new = jnp.maximum(m_old, jnp.max(s, axis=1, keepdims=True) + cq)
                alpha = jnp.exp(m_old - m_new)
                p = jnp.exp(s - (m_new - cq)).astype(BF16)
                v_one = jnp.concatenate([v_ref[:w, :], ones[:w, :]], axis=1)
                ms.append(m_new)
                accs.append(alpha * acc_all[rows, :] + _dot(p, v_one))
            m_sc[...] = jnp.concatenate(ms, axis=0)
            acc_sc[...] = jnp.concatenate(accs, axis=0)

        @pl.when(j < i)
        def _():
            step(False)

        @pl.when(j == i)
        def _():
            step(True)
            acc = acc_sc[...]
            denom = acc[:, DH:DH + 1]
            o_ref[...] = (acc[:, :DH] / denom).astype(o_ref.dtype)
            lse_ref[0] = m_sc[...] + jnp.log(denom)

    kv = lambda off: pl.BlockSpec((tq, DH), lambda h, t, qi, kj: (kj[t], off + h))
    col = pl.BlockSpec((1, tq, 1), lambda h, t, qi, kj: (h, qi[t], 0))
    grid_spec = pltpu.PrefetchScalarGridSpec(
        num_scalar_prefetch=2, grid=(HEADS, qi_tab.shape[0]),
        in_specs=[pl.BlockSpec((tq, DH), lambda h, t, qi, kj: (qi[t], _Q_OFF + h)), kv(_K_OFF), kv(_V_OFF), col,
                  pl.BlockSpec((1, 1, tq), lambda h, t, qi, kj: (h, 0, kj[t]))],
        out_specs=[pl.BlockSpec((tq, DH), lambda h, t, qi, kj: (qi[t], h)), col],
        scratch_shapes=[pltpu.VMEM((tq, 1), F32), pltpu.VMEM((tq, 2 * DH), F32)])
    return pl.pallas_call(
        body, name=name, grid_spec=grid_spec,
        out_shape=[jax.ShapeDtypeStruct((T, HEADS * DH), BF16), jax.ShapeDtypeStruct((HEADS, T, 1), F32)],
        compiler_params=_params("parallel", "arbitrary"),
    )(qi_tab, kj_tab, pm, pm, pm, c_col, c_row)


def _fox_delta(do, o, name):
    T, D = o.shape
    tr = _tile(T, ROW_TILE)

    def body(do_ref, o_ref, d_ref):
        prod = do_ref[...].astype(F32) * o_ref[...].astype(F32)
        for h in range(HEADS):
            d_ref[h] = _rowsum(prod[:, h * DH:(h + 1) * DH])

    row = pl.BlockSpec((tr, D), lambda i: (i, 0))
    return pl.pallas_call(
        body, name=name, grid=(T // tr,), in_specs=[row, row],
        out_specs=pl.BlockSpec((HEADS, tr, 1), lambda i: (0, i, 0)),
        out_shape=jax.ShapeDtypeStruct((HEADS, T, 1), F32), compiler_params=_params("parallel"),
    )(do, o)


def _fox_bwd(pm, c_col, c_row, do, lse, delta, name):
    T = pm.shape[0]
    tq = _tile(T, ATTN_TILE)
    nq = T // tq
    rg = min(ATTN_ROWS, tq)
    qi_tab, kj_tab = _causal_pairs(nq, by_key=True)
    npairs = qi_tab.shape[0]

    def body(qi_ref, kj_ref, q_ref, k_ref, v_ref, cq_ref, ck_ref, do_ref, lse_ref, dl_ref,
             dq_ref, dk_ref, dv_ref, rsum_ref, csum_ref, dq_sc, dk_sc, dv_sc):
        t = pl.program_id(1)
        i, j = qi_ref[t], kj_ref[t]

        @pl.when(t == 0)
        def _():
            dq_sc[...] = jnp.zeros_like(dq_sc)

        @pl.when(i == j)
        def _():
            dk_sc[...] = jnp.zeros_like(dk_sc)
            dv_sc[...] = jnp.zeros_like(dv_sc)

        base = pl.multiple_of(i * tq, tq)

        def step(diag):
            ones = _ones_column(tq)
            for r in range(tq // rg):
                rows = slice(r * rg, (r + 1) * rg)
                w = (r + 1) * rg if diag else tq
                qr, dor = q_ref[rows, :], do_ref[rows, :]
                s = _fox_logits(qr, k_ref[:w, :], ck_ref[0, :, :w], r * rg, diag)
                p = jnp.exp(s - (lse_ref[0, rows, :] - cq_ref[0, rows, :]))
                dp = _dot(dor, v_ref[:w, :], "nt")
                dsb = (p * (dp - dl_ref[0, rows, :])).astype(BF16)
                dv_sc[:w, :] += _dot(p.astype(BF16), dor, "tn")
                dk_sc[:w, :] += _dot(dsb, jnp.concatenate([qr, ones[rows, :]], axis=1), "tn")
                dq_sc[pl.ds(base + r * rg, rg), :] += _dot(dsb, jnp.concatenate([k_ref[:w, :], ones[:w, :]], axis=1))

        @pl.when(i > j)
        def _():
            step(False)

        @pl.when(i == j)
        def _():
            step(True)

        @pl.when(i == nq - 1)
        def _():
            dk_ref[...] = dk_sc[:, :DH].astype(dk_ref.dtype)
            dv_ref[...] = dv_sc[...].astype(dv_ref.dtype)
            csum_ref[0] = dk_sc[:, DH:DH + 1]

        @pl.when(t == npairs - 1)
        def _():
            dq_ref[...] = dq_sc[:, :DH].astype(dq_ref.dtype)
            rsum_ref[0] = dq_sc[:, DH:DH + 1]

    col = pl.BlockSpec((1, tq, 1), lambda h, t, qi, kj: (h, qi[t], 0))
    kv = lambda off: pl.BlockSpec((tq, DH), lambda h, t, qi, kj: (kj[t], off + h))
    kv_out = pl.BlockSpec((tq, DH), lambda h, t, qi, kj: (kj[t], h))
    grid_spec = pltpu.PrefetchScalarGridSpec(
        num_scalar_prefetch=2, grid=(HEADS, npairs),
        in_specs=[pl.BlockSpec((tq, DH), lambda h, t, qi, kj: (qi[t], _Q_OFF + h)), kv(_K_OFF), kv(_V_OFF), col,
                  pl.BlockSpec((1, 1, tq), lambda h, t, qi, kj: (h, 0, kj[t])),
                  pl.BlockSpec((tq, DH), lambda h, t, qi, kj: (qi[t], h)), col, col],
        out_specs=[pl.BlockSpec((T, DH), lambda h, t, qi, kj: (0, h)), kv_out, kv_out,
                   pl.BlockSpec((1, T, 1), lambda h, t, qi, kj: (h, 0, 0)),
                   pl.BlockSpec((1, tq, 1), lambda h, t, qi, kj: (h, kj[t], 0))],
        scratch_shapes=[pltpu.VMEM((T, 2 * DH), F32), pltpu.VMEM((tq, 2 * DH), F32), pltpu.VMEM((tq, DH), F32)])
    D = HEADS * DH
    return pl.pallas_call(
        body, name=name, grid_spec=grid_spec,
        out_shape=[jax.ShapeDtypeStruct((T, D), BF16)] * 3 + [jax.ShapeDtypeStruct((HEADS, T, 1), F32)] * 2,
        compiler_params=_params("parallel", "arbitrary"),
    )(qi_tab, kj_tab, pm, pm, pm, c_col, c_row, do, lse, delta)


def _xattn_fwd(q, kv, name):
    T, D = q.shape
    M = kv.shape[0]
    dh = D // MEM_HEADS
    tq = _tile(T, ATTN_TILE)
    scale = 1.0 / math.sqrt(dh)

    def body(q_ref, kv_ref, o_ref):
        for h in range(MEM_HEADS):
            cs = slice(h * dh, (h + 1) * dh)
            s = _dot(q_ref[:, cs], kv_ref[:, cs], "nt") * scale
            p = jnp.exp(s - jnp.max(s, axis=1, keepdims=True))
            p = p / _rowsum(p)
            o_ref[:, cs] = _dot(p.astype(BF16), kv_ref[:, D + h * dh:D + (h + 1) * dh]).astype(o_ref.dtype)

    row = pl.BlockSpec((tq, D), lambda i: (i, 0))
    return pl.pallas_call(
        body, name=name, grid=(T // tq,), in_specs=[row, pl.BlockSpec((M, 2 * D), lambda i: (0, 0))],
        out_specs=row, out_shape=jax.ShapeDtypeStruct((T, D), BF16), compiler_params=_params("parallel"),
    )(q, kv)


def _xattn_bwd(q, kv, do, name):
    T, D = q.shape
    M = kv.shape[0]
    dh = D // MEM_HEADS
    tq = _tile(T, ATTN_TILE)
    scale = 1.0 / math.sqrt(dh)

    def body(q_ref, kv_ref, do_ref, dq_ref, dkv_ref):
        @pl.when(pl.program_id(0) == 0)
        def _():
            dkv_ref[...] = jnp.zeros_like(dkv_ref)

        for h in range(MEM_HEADS):
            cs = slice(h * dh, (h + 1) * dh)
            vs = slice(D + h * dh, D + (h + 1) * dh)
            s = _dot(q_ref[:, cs], kv_ref[:, cs], "nt") * scale
            p = jnp.exp(s - jnp.max(s, axis=1, keepdims=True))
            p = p / _rowsum(p)
            dp = _dot(do_ref[:, cs], kv_ref[:, vs], "nt")
            ds = (p * (dp - _rowsum(p * dp)) * scale).astype(BF16)
            dq_ref[:, cs] = _dot(ds, kv_ref[:, cs]).astype(dq_ref.dtype)
            dkv_ref[:, cs] += _dot(ds, q_ref[:, cs], "tn")
            dkv_ref[:, vs] += _dot(p.astype(BF16), do_ref[:, cs], "tn")

    row = pl.BlockSpec((tq, D), lambda i: (i, 0))
    full = pl.BlockSpec((M, 2 * D), lambda i: (0, 0))
    return pl.pallas_call(
        body, name=name, grid=(T // tq,), in_specs=[row, full, row], out_specs=[row, full],
        out_shape=[jax.ShapeDtypeStruct((T, D), BF16), jax.ShapeDtypeStruct((M, 2 * D), F32)],
        compiler_params=_params("arbitrary"),
    )(q, kv, do)


_HBM = pl.BlockSpec(memory_space=pltpu.HBM)


def _position():
    return lax.axis_index("x"), lax.axis_index("y"), lax.axis_index("c")


def _other_chips(x, y):
    return [(1 - x, y), (x, 1 - y), (1 - x, 1 - y)]


def _exchange_chips(srcs, outs, send, recv, local, src_of, dst_of):
    x, y, c = _position()
    q = 2 * x + y
    kept, sent = [], []
    for w, (s_ref, o_ref) in enumerate(zip(srcs, outs)):
        mine = pltpu.make_async_copy(src_of(s_ref, q), dst_of(o_ref, q), local.at[w])
        mine.start()
        kept.append(mine)
        for j, (px, py) in enumerate(_other_chips(x, y)):
            cp = pltpu.make_async_remote_copy(
                src_ref=src_of(s_ref, 2 * px + py), dst_ref=dst_of(o_ref, q), send_sem=send.at[3 * w + j],
                recv_sem=recv.at[3 * w + j], device_id=(px, py, c), device_id_type=MESH)
            cp.start()
            sent.append(cp)
    for w, (s_ref, o_ref) in enumerate(zip(srcs, outs)):
        for j, (px, py) in enumerate(_other_chips(x, y)):
            pltpu.make_async_remote_copy(
                src_ref=src_of(s_ref, q), dst_ref=dst_of(o_ref, 2 * px + py), send_sem=send.at[3 * w + j],
                recv_sem=recv.at[3 * w + j], device_id=(px, py, c), device_id_type=MESH).wait_recv()
    for cp in sent:
        cp.wait_send()
    for cp in kept:
        cp.wait()


def _chip_sems(n):
    return [pltpu.SemaphoreType.DMA((3 * n,)), pltpu.SemaphoreType.DMA((3 * n,)), pltpu.SemaphoreType.DMA((n,))]


def _ag_chips(blks, name):
    n = len(blks)

    def body(*refs):
        c = lax.axis_index("c")
        _exchange_chips(refs[:n], refs[n:2 * n], *refs[2 * n:], src_of=lambda r, chip: r,
                        dst_of=lambda r, chip: r.at[chip, c])

    return pl.pallas_call(
        body, name=name, in_specs=[_HBM] * n, out_specs=[_HBM] * n,
        out_shape=[jax.ShapeDtypeStruct((4, 2) + b.shape, b.dtype) for b in blks], scratch_shapes=_chip_sems(n),
    )(*blks)


def _ag_sibling(arrs, name):
    n = len(arrs)

    def body(*refs):
        outs, send, recv = refs[n:2 * n], refs[2 * n], refs[2 * n + 1]
        x, y, c = _position()
        cps = []
        for w, a_ref in enumerate(outs):
            cp = pltpu.make_async_remote_copy(src_ref=a_ref.at[:, c], dst_ref=a_ref.at[:, c], send_sem=send.at[w],
                                              recv_sem=recv.at[w], device_id=(x, y, 1 - c), device_id_type=MESH)
            cp.start()
            cps.append(cp)
        for w, a_ref in enumerate(outs):
            pltpu.make_async_remote_copy(src_ref=a_ref.at[:, c], dst_ref=a_ref.at[:, 1 - c], send_sem=send.at[w],
                                         recv_sem=recv.at[w], device_id=(x, y, 1 - c), device_id_type=MESH).wait_recv()
        for cp in cps:
            cp.wait_send()

    return pl.pallas_call(
        body, name=name, in_specs=[_HBM] * n, out_specs=[_HBM] * n,
        out_shape=[jax.ShapeDtypeStruct(a.shape, a.dtype) for a in arrs],
        input_output_aliases={i: i for i in range(n)},
        scratch_shapes=[pltpu.SemaphoreType.DMA((n,)), pltpu.SemaphoreType.DMA((n,))],
    )(*arrs)


def _rs_sibling(blocks, name):
    n = len(blocks)

    def body(*refs):
        srcs, outs, send, recv = refs[:n], refs[n:2 * n], refs[2 * n], refs[2 * n + 1]
        x, y, c = _position()
        cps = []
        for w, (b_ref, l_ref) in enumerate(zip(srcs, outs)):
            cp = pltpu.make_async_remote_copy(src_ref=b_ref.at[:, 1 - c], dst_ref=l_ref, send_sem=send.at[w],
                                              recv_sem=recv.at[w], device_id=(x, y, 1 - c), device_id_type=MESH)
            cp.start()
            cps.append(cp)
        for cp in cps:
            cp.wait()

    return pl.pallas_call(
        body, name=name, in_specs=[_HBM] * n, out_specs=[_HBM] * n,
        out_shape=[jax.ShapeDtypeStruct((4,) + b.shape[2:], b.dtype) for b in blocks],
        scratch_shapes=[pltpu.SemaphoreType.DMA((n,)), pltpu.SemaphoreType.DMA((n,))],
    )(*blocks)


def _rs_chips(parts, name):
    n = len(parts)

    def body(*refs):
        _exchange_chips(refs[:n], refs[n:2 * n], *refs[2 * n:], src_of=lambda r, chip: r.at[chip],
                        dst_of=lambda r, chip: r.at[chip])

    return pl.pallas_call(
        body, name=name, in_specs=[_HBM] * n, out_specs=[_HBM] * n,
        out_shape=[jax.ShapeDtypeStruct(h.shape, h.dtype) for h in parts], scratch_shapes=_chip_sems(n),
    )(*parts)


def _row_tile(rows, pref=256):
    for t in range(min(pref, rows) // 16 * 16, 0, -16):
        if rows % t == 0:
            return t
    raise ValueError(f"no row tile for {rows}")


def _pair_add(blocks, landed, core, out_dtype, name):
    n, _, s0, s1 = blocks.shape
    tr = _row_tile(s0)

    def body(core_ref, a_ref, b_ref, o_ref):
        del core_ref
        o_ref[...] = (a_ref[...] + b_ref[...]).astype(o_ref.dtype)

    grid_spec = pltpu.PrefetchScalarGridSpec(
        num_scalar_prefetch=1, grid=(n, s0 // tr),
        in_specs=[pl.BlockSpec((1, None, tr, s1), lambda p, i, core: (p, core[0], i, 0)),
                  pl.BlockSpec((1, tr, s1), lambda p, i, core: (p, i, 0))],
        out_specs=pl.BlockSpec((1, tr, s1), lambda p, i, core: (p, i, 0)))
    return pl.pallas_call(
        body, name=name, grid_spec=grid_spec, out_shape=jax.ShapeDtypeStruct(landed.shape, out_dtype),
        compiler_params=_params("parallel", "parallel"),
    )(core, blocks, landed)


def _adamw_math(w, g, m, v):
    m = ADAM_B1 * m + (1.0 - ADAM_B1) * g
    v = ADAM_B2 * v + (1.0 - ADAM_B2) * (g * g)
    m_hat = m / (1.0 - ADAM_B1 ** ADAM_STEP)
    v_hat = v / (1.0 - ADAM_B2 ** ADAM_STEP)
    delta = -ADAM_LR * (m_hat / (jnp.sqrt(v_hat) + ADAM_EPS) + ADAM_WD * w)
    return delta, m, v


def _adamw_reduce(slots, w, m, v, name):
    n, R, C = slots.shape
    tr = _row_tile(R)

    def body(s_ref, w_ref, m_ref, v_ref, g_ref, d_ref, nm_ref, nv_ref):
        g = s_ref[0].astype(F32)
        for p in range(1, n):
            g = g + s_ref[p].astype(F32)
        g_ref[...] = g
        d_ref[...], nm_ref[...], nv_ref[...] = _adamw_math(w_ref[...], g, m_ref[...], v_ref[...])

    row = pl.BlockSpec((tr, C), lambda i: (i, 0))
    return pl.pallas_call(
        body, name=name, grid=(R // tr,), in_specs=[pl.BlockSpec((n, tr, C), lambda i: (0, i, 0)), row, row, row],
        out_specs=[row] * 4, out_shape=[jax.ShapeDtypeStruct((R, C), F32)] * 4, compiler_params=_params("parallel"),
    )(slots, w, m, v)


def _full_from_gathered(a, n):
    s0, s1 = a.shape[2:]
    blk = a.reshape(8, s0, s1)
    if n in COL_SHARDED:
        return blk.transpose(1, 0, 2).reshape(s0, 8 * s1)
    return blk.reshape(8 * s0, s1)


def _blocks_from_full(g, n, shard_shape):
    s0, s1 = shard_shape
    if n in COL_SHARDED:
        blk = g.reshape(s0, 8, s1).transpose(1, 0, 2)
    else:
        blk = g.reshape(8, s0, s1)
    return blk.reshape(4, 2, s0, s1)


def _swiglu_interleave(w):
    d, f2 = w.shape
    return w.reshape(d, 2, f2 // (2 * SWIGLU_TILE), SWIGLU_TILE).transpose(0, 2, 1, 3).reshape(d, f2)


def _swiglu_deinterleave(w):
    d, f2 = w.shape
    return w.reshape(d, f2 // (2 * SWIGLU_TILE), 2, SWIGLU_TILE).transpose(0, 2, 1, 3).reshape(d, f2)


SMALL_ROWS = 16


def _pack_small(vals, loss_row):
    rows = []
    for n in SMALL:
        flat = vals[n].reshape(-1)
        pad = (-flat.shape[0]) % PACK_COLS
        rows.append(jnp.pad(flat, (0, pad)).reshape(-1, PACK_COLS))
    rows.append(loss_row)
    out = jnp.concatenate(rows, axis=0)
    assert out.shape[0] == SMALL_ROWS, out.shape
    return out


def _unpack_small(packed, like):
    out, r = {}, 0
    for n in SMALL:
        size = like[n].size
        rows = -(-size // PACK_COLS)
        out[n] = packed[r:r + rows].reshape(-1)[:size].reshape(like[n].shape)
        r += rows
    return out


def _ffn_fwd(x, g_pre, w_in, w_down, tag):
    h = _rms_fwd(x, g_pre, f"{tag}_pre")
    u, a = _mm_swiglu(h, w_in, f"{tag}_up")
    z = _mm(a, w_down, "nn", F32, f"{tag}_down", tk=1408)
    return h, u, a, z


def _ffn_bwd(saved, x, g_pre, w_in, w_down, g_post, dx_out, tag):
    h, u, a, z = saved
    dz, dg_post = _rms_bwd(z, g_post, dx_out, 0.5, f"{tag}_post_bwd", BF16)
    dw_down = _mm(a, dz, "tn", F32, f"{tag}_down_dw", tm=1408)
    du = _mm_swiglu_bwd(dz, w_down, u, f"{tag}_down_dx")
    dh = _mm(du, w_in, "nt", BF16, f"{tag}_up_dx", tk=5632)
    dw_in = _mm(h, du, "tn", F32, f"{tag}_up_dw")
    dx, dg_pre = _rms_bwd(x, g_pre, dh, 1.0, f"{tag}_pre_bwd", F32, resid=dx_out)
    return dx, dg_pre, dg_post, dw_in, dw_down


def _step_local(x, mem, target, W, S):
    T, D = x.shape
    gW, gS = {}, {}

    f1 = _ffn_fwd(x, S["ffn1_pre_g"], W["ffn1_w_in"], W["ffn1_w_down"], "ffn1")
    x1 = _resid_rms(x, f1[3], S["ffn1_post_g"], 0.5, "ffn1_post")

    h2 = _rms_fwd(x1, S["mix_pre_g"], "mix_pre")
    pm = _mm(h2, W["w_main"], "nn", BF16, "mix_proj_main")
    pf = _mm(h2, W["w_f"], "nn", F32, "mix_proj_f")
    pg = _mm(h2, W["w_gates"], "nn", F32, "mix_proj_gates")
    lbl = S["hg_lb_logits"].reshape(2, HEADS, 1, DH)
    o_a, states = _hgrn_fwd(pm, lbl, "hgrn_fwd")
    oan = _hgout_fwd(o_a, pm, S["hg_norm_g"], "hgrn_out")
    bias = jnp.pad(S["fox_f_bias"], ((0, 0), (0, LANES - HEADS)))
    c = _fox_cumsum(pf, bias, "fox_cumsum")
    c_heads = c[:, :HEADS].T
    c_col, c_row = c_heads[:, :, None], c_heads[:, None, :]
    o_b, lse = _fox_fwd(pm, c_col, c_row, "fox_fwd")
    ya = _mm(oan, W["w_branch_a"], "nn", F32, "branch_a")
    yb = _mm(o_b, W["w_branch_b"], "nn", F32, "branch_b")
    y = _merge_fwd(ya, yb, pg, S["b_gate"], "merge")
    z2 = _mm(y, W["w_out"], "nn", F32, "mix_out")
    x2 = _resid_rms(x1, z2, S["mix_post_g"], 1.0, "mix_post")

    h3 = _rms_fwd(x2, S["mem_pre_g"], "mem_pre")
    memn = _rms_fwd(mem, S["mem_kv_g"], "mem_kv_norm")
    qm = _mm(h3, W["w_mq"], "nn", BF16, "mem_q")
    kv = _mm(memn, W["w_mkv"], "nn", BF16, "mem_kv")
    om = _xattn_fwd(qm, kv, "mem_attn")
    z3 = _mm(om, W["w_mo"], "nn", F32, "mem_o")
    x3 = _resid_rms(x2, z3, S["mem_post_g"], 1.0, "mem_post")

    f2 = _ffn_fwd(x3, S["ffn2_pre_g"], W["ffn2_w_in"], W["ffn2_w_down"], "ffn2")
    dx4, sq = _final_loss(x3, f2[3], S["ffn2_post_g"], 0.5, target, "loss")

    dx3, gS["ffn2_pre_g"], gS["ffn2_post_g"], gW["ffn2_w_in"], gW["ffn2_w_down"] = _ffn_bwd(
        f2, x3, S["ffn2_pre_g"], W["ffn2_w_in"], W["ffn2_w_down"], S["ffn2_post_g"], dx4, "ffn2")

    dz3, gS["mem_post_g"] = _rms_bwd(z3, S["mem_post_g"], dx3, 1.0, "mem_post_bwd", BF16)
    dom = _mm(dz3, W["w_mo"], "nt", BF16, "mem_o_dx")
    gW["w_mo"] = _mm(om, dz3, "tn", F32, "mem_o_dw")
    dqm, dkv = _xattn_bwd(qm, kv, dom, "mem_attn_bwd")
    dh3 = _mm(dqm, W["w_mq"], "nt", BF16, "mem_q_dx")
    gW["w_mq"] = _mm(h3, dqm, "tn", F32, "mem_q_dw")
    dkvb = dkv.astype(BF16)
    gW["w_mkv"] = _mm(memn, dkvb, "tn", F32, "mem_kv_dw")
    dmemn = _mm(dkvb, W["w_mkv"], "nt", F32, "mem_kv_dx")
    _, gS["mem_kv_g"] = _rms_bwd(mem, S["mem_kv_g"], dmemn, 1.0, "mem_kv_norm_bwd", BF16)
    dx2, gS["mem_pre_g"] = _rms_bwd(x2, S["mem_pre_g"], dh3, 1.0, "mem_pre_bwd", F32, resid=dx3)

    dz2, gS["mix_post_g"] = _rms_bwd(z2, S["mix_post_g"], dx2, 1.0, "mix_post_bwd", BF16)
    dy = _mm(dz2, W["w_out"], "nt", BF16, "mix_out_dx")
    gW["w_out"] = _mm(y, dz2, "tn", F32, "mix_out_dw")
    dya, dyb, dpg, gS["b_gate"] = _merge_bwd(dy, ya, yb, pg, S["b_gate"], "merge_bwd")
    doan = _mm(dya, W["w_branch_a"], "nt", BF16, "branch_a_dx")
    gW["w_branch_a"] = _mm(oan, dya, "tn", F32, "branch_a_dw")
    dob = _mm(dyb, W["w_branch_b"], "nt", BF16, "branch_b_dx")
    gW["w_branch_b"] = _mm(o_b, dyb, "tn", F32, "branch_b_dw")

    delta = _fox_delta(dob, o_b, "fox_delta")
    dq_b, dk_b, dv_b, ds_rows, ds_cols = _fox_bwd(pm, c_col, c_row, dob, lse, delta, "fox_bwd")
    dc = jnp.pad((ds_rows - ds_cols).reshape(HEADS, T).T, ((0, 0), (0, LANES - HEADS)))
    dpf, dbias = _fox_dcum(dc, pf, bias, "fox_cumsum_bwd")
    gS["fox_f_bias"] = dbias[:, :HEADS]

    do_a, dg_a, gS["hg_norm_g"] = _hgout_bwd(o_a, pm, S["hg_norm_g"], doan, "hgrn_out_bwd")
    dq_a, df_a, di_a, dlbl = _hgrn_bwd(pm, lbl, states, do_a, "hgrn_bwd")
    gS["hg_lb_logits"] = dlbl.reshape(2, HEADS, DH)

    dpm = jnp.concatenate([dq_a, df_a, di_a, dg_a, dq_b, dk_b, dv_b], axis=1)
    dpf16 = dpf.astype(BF16)
    dh2 = _mm(dpm, W["w_main"], "nt", F32, "mix_proj_main_dx")
    dh2 = _mm(dpg, W["w_gates"], "nt", F32, "mix_proj_gates_dx", add=dh2)
    dh2 = _mm(dpf16, W["w_f"], "nt", F32, "mix_proj_f_dx", add=dh2)
    gW["w_main"] = _mm(h2, dpm, "tn", F32, "mix_proj_main_dw")
    gW["w_gates"] = _mm(h2, dpg, "tn", F32, "mix_proj_gates_dw")
    gW["w_f"] = _mm(h2, dpf16, "tn", F32, "mix_proj_f_dw")
    dx1, gS["mix_pre_g"] = _rms_bwd(x1, S["mix_pre_g"], dh2, 1.0, "mix_pre_bwd", F32, resid=dx2)

    dx0, gS["ffn1_pre_g"], gS["ffn1_post_g"], gW["ffn1_w_in"], gW["ffn1_w_down"] = _ffn_bwd(
        f1, x, S["ffn1_pre_g"], W["ffn1_w_in"], W["ffn1_w_down"], S["ffn1_post_g"], dx1, "ffn1")
    return sq, dx0, gW, gS


def _train_step(a):
    c_idx = lax.axis_index("c")
    x, mem, target = a["x"][0], a["mem"][0], a["loss_target"][0]
    D = x.shape[1]
    shards = {n: a[n][0] for n in BIG}

    fox_scale = 1.0 / math.sqrt(DH)
    n_mine = shards["w_in"].shape[1]
    dev = 4 * lax.axis_index("x") + 2 * lax.axis_index("y") + c_idx
    cols = dev * n_mine + jnp.arange(n_mine)
    is_fox_q = (cols >= 4 * D) & (cols < 5 * D)
    sent = dict(shards, w_in=shards["w_in"] * jnp.where(is_fox_q, fox_scale, 1.0)[None, :])
    gathered = _ag_sibling(_ag_chips([sent[n].astype(BF16) for n in BIG], "ag_chips"), "ag_sibling")
    W = {n: _full_from_gathered(g, n) for n, g in zip(BIG, gathered)}
    n_main = 7 * D
    w_in = W.pop("w_in")
    W["w_main"] = w_in[:, :n_main]
    W["w_f"] = jnp.pad(w_in[:, n_main:n_main + HEADS], ((0, 0), (0, LANES - HEADS)))
    W["w_gates"] = w_in[:, n_main + HEADS:]
    W["ffn1_w_in"] = _swiglu_interleave(W["ffn1_w_in"])
    W["ffn2_w_in"] = _swiglu_interleave(W["ffn2_w_in"])
    S = {n: a[n] for n in SMALL}

    sq, grad_x, gW, gS = _step_local(x, mem, target, W, S)

    g_main = gW.pop("w_main")
    gW["w_in"] = jnp.concatenate([g_main[:, :4 * D], g_main[:, 4 * D:5 * D] * fox_scale, g_main[:, 5 * D:],
                                  gW.pop("w_f")[:, :HEADS], gW.pop("w_gates")], axis=1)
    gW["ffn1_w_in"] = _swiglu_deinterleave(gW["ffn1_w_in"])
    gW["ffn2_w_in"] = _swiglu_deinterleave(gW["ffn2_w_in"])
    blocks = [_blocks_from_full(gW[n], n, shards[n].shape) for n in BIG]
    landed = _rs_sibling(blocks, "rs_sibling")
    core = c_idx.astype(jnp.int32).reshape(1)
    pairs = [_pair_add(b, l, core, BF16, f"rs_pair_add_{n}") for n, b, l in zip(BIG, blocks, landed)]
    slots = _rs_chips(pairs, "rs_chips")
    big = {n: _adamw_reduce(s, shards[n], a["m_" + n][0], a["v_" + n][0], f"adamw_{n}") for n, s in zip(BIG, slots)}

    loss_row = jnp.pad(sq[:1, :1] * (0.5 / D), ((0, 0), (0, PACK_COLS - 1)))
    small_all = _ag_sibling(_ag_chips([_pack_small(gS, loss_row)], "small_ag_chips"), "small_ag_sibling")[0]
    small_slots = small_all.reshape(8, SMALL_ROWS, PACK_COLS)
    zero_row = jnp.zeros((1, PACK_COLS), F32)
    g_sm, d_sm, m_sm, v_sm = _adamw_reduce(
        small_slots, _pack_small({n: a[n] for n in SMALL}, zero_row),
        _pack_small({n: a["m_" + n] for n in SMALL}, zero_row),
        _pack_small({n: a["v_" + n] for n in SMALL}, zero_row), "adamw_small")

    def unpack(which, small):
        out = _unpack_small(small, {n: a[n] for n in SMALL})
        for n in BIG:
            out[n] = big[n][which][None]
        return [out[n] for n in WEIGHTS]

    loss = g_sm[SMALL_ROWS - 1, 0]
    return (loss, grad_x[None], *unpack(0, g_sm), *unpack(1, d_sm), *unpack(2, m_sm), *unpack(3, v_sm))


def kernel(x, mem, ffn1_pre_g, ffn1_w_in, ffn1_w_down, ffn1_post_g, mix_pre_g, w_in, hg_lb_logits, hg_norm_g, fox_f_bias, w_branch_a, w_branch_b, b_gate, w_out, mix_post_g, mem_pre_g, mem_kv_g, w_mq, w_mkv, w_mo, mem_post_g, ffn2_pre_g, ffn2_w_in, ffn2_w_down, ffn2_post_g, loss_target, m_ffn1_pre_g, m_ffn1_w_in, m_ffn1_w_down, m_ffn1_post_g, m_mix_pre_g, m_w_in, m_hg_lb_logits, m_hg_norm_g, m_fox_f_bias, m_w_branch_a, m_w_branch_b, m_b_gate, m_w_out, m_mix_post_g, m_mem_pre_g, m_mem_kv_g, m_w_mq, m_w_mkv, m_w_mo, m_mem_post_g, m_ffn2_pre_g, m_ffn2_w_in, m_ffn2_w_down, m_ffn2_post_g, v_ffn1_pre_g, v_ffn1_w_in, v_ffn1_w_down, v_ffn1_post_g, v_mix_pre_g, v_w_in, v_hg_lb_logits, v_hg_norm_g, v_fox_f_bias, v_w_branch_a, v_w_branch_b, v_b_gate, v_w_out, v_mix_post_g, v_mem_pre_g, v_mem_kv_g, v_w_mq, v_w_mkv, v_w_mo, v_mem_post_g, v_ffn2_pre_g, v_ffn2_w_in, v_ffn2_w_down, v_ffn2_post_g):
    return _train_step(dict(locals()))
```

```python
import functools
import math

import jax
import jax.numpy as jnp
from jax import lax
from jax.experimental import pallas as pl
from jax.experimental.pallas import tpu as pltpu

F32 = jnp.float32
BF16 = jnp.bfloat16
MESH = pl.DeviceIdType.MESH

EPS = 1e-6
HEADS = 8
DH = 128
MEM_HEADS = 4
CHUNK = 128
HALF = CHUNK // 2
SWIGLU_TILE = 256
LANES = 128
PACK_COLS = 1024
ROW_TILE = 512
SEQ_BLOCK = 512
ATTN_TILE = 1024
ATTN_ROWS = 256
EXP_CLAMP = 80.0
NEG_BIG = -1e30

ADAM_LR, ADAM_B1, ADAM_B2, ADAM_EPS, ADAM_WD, ADAM_STEP = 0.001, 0.9, 0.999, 1e-08, 0.01, 10

VMEM_LIMIT = 48 * 1024 * 1024

_DN = {
    "nn": (((1,), (0,)), ((), ())),
    "nt": (((1,), (1,)), ((), ())),
    "tn": (((0,), (0,)), ((), ())),
}

BIG = ["ffn1_w_in", "ffn1_w_down", "w_in", "w_branch_a", "w_branch_b", "w_out", "w_mq", "w_mkv", "w_mo",
       "ffn2_w_in", "ffn2_w_down"]
COL_SHARDED = {"ffn1_w_in", "w_in", "w_mkv", "ffn2_w_in"}
SMALL = ["ffn1_pre_g", "ffn1_post_g", "mix_pre_g", "hg_lb_logits", "hg_norm_g", "fox_f_bias", "b_gate",
         "mix_post_g", "mem_pre_g", "mem_kv_g", "mem_post_g", "ffn2_pre_g", "ffn2_post_g"]
WEIGHTS = ["ffn1_pre_g", "ffn1_w_in", "ffn1_w_down", "ffn1_post_g", "mix_pre_g", "w_in", "hg_lb_logits",
           "hg_norm_g", "fox_f_bias", "w_branch_a", "w_branch_b", "b_gate", "w_out", "mix_post_g", "mem_pre_g",
           "mem_kv_g", "w_mq", "w_mkv", "w_mo", "mem_post_g", "ffn2_pre_g", "ffn2_w_in", "ffn2_w_down",
           "ffn2_post_g"]


def _dot(a, b, mode="nn"):
    return lax.dot_general(a, b, _DN[mode], preferred_element_type=F32)


def _sig(x):
    return 1.0 / (1.0 + jnp.exp(-x))


def _params(*dims):
    return pltpu.CompilerParams(dimension_semantics=dims if dims else None, vmem_limit_bytes=VMEM_LIMIT)


def _tile(dim, pref):
    if dim <= pref:
        return dim
    t = (pref // LANES) * LANES
    while t >= LANES:
        if dim % t == 0:
            return t
        t -= LANES
    raise ValueError(f"no tile for {dim}")


def _colsum(x):
    return jnp.sum(x, axis=0, keepdims=True)


def _rowsum(x):
    return jnp.sum(x, axis=1, keepdims=True)


def _iota(shape, axis):
    return lax.broadcasted_iota(jnp.int32, shape, axis)


def _pick_row(x, r):
    return _colsum(jnp.where(_iota(x.shape, 0) == r, x, 0.0))


def _tri_dot(tri, x):
    hi = x.astype(BF16)
    r1 = x - hi.astype(F32)
    mid = r1.astype(BF16)
    lo = (r1 - mid.astype(F32)).astype(BF16)
    return _dot(tri, hi) + _dot(tri, mid) + _dot(tri, lo)


_MM_TILES = {"nn": (2048, 512, 1024), "nt": (512, 1024, 4096), "tn": (1024, 1024, 2048)}


def _host_call(body, name, grid, in_specs, out_specs, out_shape, scratch_shapes, dims, args, hosted=None):
    if hosted is None:
        results = pl.pallas_call(body, name=name, grid=grid, in_specs=in_specs, out_specs=out_specs, out_shape=out_shape,
                                 scratch_shapes=scratch_shapes, compiler_params=_params(*dims))(*args)
        return list(results), []
    n_in, n_out, n_sc = len(in_specs), len(out_specs), len(scratch_shapes)
    h_in, h_out = len(hosted.inputs), len(hosted.out_shapes)

    def wrapped(*refs):
        cut = [n_in, h_in, n_out, h_out, n_sc]
        at = [sum(cut[:i]) for i in range(len(cut) + 1)]
        ins, hin, outs, hout, scr = (refs[at[i]:at[i + 1]] for i in range(len(cut)))
        hsems = refs[at[-1]:]
        ids = [pl.program_id(d) for d in range(len(grid))]
        first = functools.reduce(jnp.logical_and, [i == 0 for i in ids])
        last = functools.reduce(jnp.logical_and, [i == g - 1 for i, g in zip(ids, grid)])

        @pl.when(first)
        def _():
            hosted.start(hin, hout, hsems)

        body(*ins, *outs, *scr)

        @pl.when(last)
        def _():
            hosted.wait(hin, hout, hsems)

    results = pl.pallas_call(
        wrapped, name=name, grid=grid, in_specs=list(in_specs) + [_HBM] * h_in,
        out_specs=list(out_specs) + [_HBM] * h_out, out_shape=list(out_shape) + list(hosted.out_shapes),
        scratch_shapes=list(scratch_shapes) + list(hosted.scratch), input_output_aliases=hosted.aliases(n_in, n_out),
        compiler_params=_params(*dims))(*args, *hosted.inputs)
    return list(results[:n_out]), list(results[n_out:])


def _mm(a, b, mode, out_dtype, name, add=None, tm=None, tn=None, tk=None, hosted=None):
    tm, tn, tk = (given or pref for given, pref in zip((tm, tn, tk), _MM_TILES[mode]))
    if mode == "nn":
        (M, K), (K2, N) = a.shape, b.shape
    elif mode == "nt":
        (M, K), (N, K2) = a.shape, b.shape
    else:
        (K, M), (K2, N) = a.shape, b.shape
    assert K == K2, (name, a.shape, b.shape)
    tm, tn, tk = _tile(M, tm), _tile(N, tn), _tile(K, tk)
    nk = K // tk
    if mode == "tn":
        a_spec = pl.BlockSpec((tk, tm), lambda i, j, k: (k, i))
    else:
        a_spec = pl.BlockSpec((tm, tk), lambda i, j, k: (i, k))
    if mode == "nt":
        b_spec = pl.BlockSpec((tn, tk), lambda i, j, k: (j, k))
    else:
        b_spec = pl.BlockSpec((tk, tn), lambda i, j, k: (k, j))
    o_spec = pl.BlockSpec((tm, tn), lambda i, j, k: (i, j))
    has_add = add is not None

    def body(*refs):
        a_ref, b_ref = refs[0], refs[1]
        c_ref = refs[2] if has_add else None
        o_ref = refs[3] if has_add else refs[2]
        part = _dot(a_ref[...], b_ref[...], mode)
        if nk == 1:
            if has_add:
                part = part + c_ref[...]
            o_ref[...] = part.astype(o_ref.dtype)
            return
        acc_ref = refs[-1]
        k = pl.program_id(2)

        @pl.when(k == 0)
        def _():
            acc_ref[...] = part + c_ref[...] if has_add else part

        @pl.when(k > 0)
        def _():
            acc_ref[...] += part

        @pl.when(k == nk - 1)
        def _():
            o_ref[...] = acc_ref[...].astype(o_ref.dtype)

    in_specs = [a_spec, b_spec] + ([o_spec] if has_add else [])
    args = (a, b) + ((add,) if has_add else ())
    (out,), landed = _host_call(
        body, name, (M // tm, N // tn, nk), in_specs, [o_spec], [jax.ShapeDtypeStruct((M, N), out_dtype)],
        [pltpu.VMEM((tm, tn), F32)] if nk > 1 else [], ("parallel", "parallel", "arbitrary"), args, hosted)
    return out if hosted is None else (out, landed)


def _rms_fwd(x, g, name, out_dtype=BF16):
    T, D = x.shape
    tr = _tile(T, ROW_TILE)

    def body(x_ref, g_ref, o_ref):
        xv = x_ref[...]
        r = lax.rsqrt(jnp.mean(xv * xv, axis=-1, keepdims=True) + EPS)
        o_ref[...] = (xv * r * g_ref[...]).astype(o_ref.dtype)

    return pl.pallas_call(
        body, name=name, grid=(T // tr,),
        in_specs=[pl.BlockSpec((tr, D), lambda i: (i, 0)), pl.BlockSpec((1, D), lambda i: (0, 0))],
        out_specs=pl.BlockSpec((tr, D), lambda i: (i, 0)),
        out_shape=jax.ShapeDtypeStruct((T, D), out_dtype), compiler_params=_params("parallel"),
    )(x, g)


def _resid_rms(x, z, g, scale, name):
    T, D = x.shape
    tr = _tile(T, ROW_TILE)

    def body(x_ref, z_ref, g_ref, o_ref):
        zv = z_ref[...]
        r = lax.rsqrt(jnp.mean(zv * zv, axis=-1, keepdims=True) + EPS)
        o_ref[...] = x_ref[...] + scale * (zv * r * g_ref[...])

    row = pl.BlockSpec((tr, D), lambda i: (i, 0))
    return pl.pallas_call(
        body, name=name, grid=(T // tr,), in_specs=[row, row, pl.BlockSpec((1, D), lambda i: (0, 0))],
        out_specs=row, out_shape=jax.ShapeDtypeStruct((T, D), F32), compiler_params=_params("parallel"),
    )(x, z, g)


def _final_loss(x, z, g, scale, target, name):
    T, D = x.shape
    tr = _tile(T, ROW_TILE)

    def body(x_ref, z_ref, g_ref, t_ref, dx_ref, acc_ref):
        @pl.when(pl.program_id(0) == 0)
        def _():
            acc_ref[...] = jnp.zeros_like(acc_ref)

        zv = z_ref[...]
        r = lax.rsqrt(jnp.mean(zv * zv, axis=-1, keepdims=True) + EPS)
        e = x_ref[...] + scale * (zv * r * g_ref[...]) - t_ref[...]
        dx_ref[...] = e * (1.0 / D)
        acc_ref[...] += _colsum(_rowsum(e * e))

    row = pl.BlockSpec((tr, D), lambda i: (i, 0))
    return pl.pallas_call(
        body, name=name, grid=(T // tr,), in_specs=[row, row, pl.BlockSpec((1, D), lambda i: (0, 0)), row],
        out_specs=[row, pl.BlockSpec((8, LANES), lambda i: (0, 0))],
        out_shape=[jax.ShapeDtypeStruct((T, D), F32), jax.ShapeDtypeStruct((8, LANES), F32)],
        compiler_params=_params("arbitrary"),
    )(x, z, g, target)


def _rms_bwd(xin, g, dy, scale, name, out_dtype, resid=None):
    T, D = xin.shape
    tr = _tile(T, ROW_TILE)
    has_resid = resid is not None

    def body(*refs):
        x_ref, g_ref, dy_ref = refs[:3]
        r_ref = refs[3] if has_resid else None
        dx_ref, dg_ref = refs[-2], refs[-1]

        @pl.when(pl.program_id(0) == 0)
        def _():
            dg_ref[...] = jnp.zeros_like(dg_ref)

        xv = x_ref[...]
        r = lax.rsqrt(jnp.mean(xv * xv, axis=-1, keepdims=True) + EPS)
        xh = xv * r
        dyv = dy_ref[...].astype(F32) * scale
        dxh = dyv * g_ref[...]
        dx = r * (dxh - xh * jnp.mean(dxh * xh, axis=-1, keepdims=True))
        if has_resid:
            dx = dx + r_ref[...]
        dx_ref[...] = dx.astype(dx_ref.dtype)
        dg_ref[...] += _colsum(dyv * xh)

    row = pl.BlockSpec((tr, D), lambda i: (i, 0))
    vec = pl.BlockSpec((1, D), lambda i: (0, 0))
    return pl.pallas_call(
        body, name=name, grid=(T // tr,), in_specs=[row, vec, row] + ([row] if has_resid else []),
        out_specs=[row, vec],
        out_shape=[jax.ShapeDtypeStruct((T, D), out_dtype), jax.ShapeDtypeStruct((1, D), F32)],
        compiler_params=_params("arbitrary"),
    )(*((xin, g, dy) + ((resid,) if has_resid else ())))


def _mm_swiglu(h, w_in, name, hosted=None):
    T, K = h.shape
    F2 = w_in.shape[1]
    tf = SWIGLU_TILE
    tm = _tile(T, _MM_TILES["nn"][0])

    def body(h_ref, w_ref, u_ref, a_ref):
        u = _dot(h_ref[...], w_ref[...])
        u_ref[...] = u.astype(u_ref.dtype)
        gate, up = u[:, :tf], u[:, tf:]
        a_ref[...] = (gate * _sig(gate) * up).astype(a_ref.dtype)

    (u, a), landed = _host_call(
        body, name, (T // tm, F2 // (2 * tf)),
        [pl.BlockSpec((tm, K), lambda i, j: (i, 0)), pl.BlockSpec((K, 2 * tf), lambda i, j: (0, j))],
        [pl.BlockSpec((tm, 2 * tf), lambda i, j: (i, j)), pl.BlockSpec((tm, tf), lambda i, j: (i, j))],
        [jax.ShapeDtypeStruct((T, F2), BF16), jax.ShapeDtypeStruct((T, F2 // 2), BF16)], [],
        ("parallel", "parallel"), (h, w_in), hosted)
    return u, a, landed


def _mm_swiglu_bwd(dz, w_down, u, name):
    T, D = dz.shape
    F = w_down.shape[0]
    tf = SWIGLU_TILE
    tm = _tile(T, _MM_TILES["nn"][0])

    def body(dz_ref, w_ref, u_ref, o_ref):
        d = _dot(dz_ref[...], w_ref[...], "nt")
        gate = u_ref[:, :tf].astype(F32)
        up = u_ref[:, tf:].astype(F32)
        s = _sig(gate)
        o_ref[:, :tf] = (d * up * (s * (1.0 + gate * (1.0 - s)))).astype(o_ref.dtype)
        o_ref[:, tf:] = (d * gate * s).astype(o_ref.dtype)

    return pl.pallas_call(
        body, name=name, grid=(T // tm, F // tf),
        in_specs=[pl.BlockSpec((tm, D), lambda i, j: (i, 0)), pl.BlockSpec((tf, D), lambda i, j: (j, 0)),
                  pl.BlockSpec((tm, 2 * tf), lambda i, j: (i, j))],
        out_specs=pl.BlockSpec((tm, 2 * tf), lambda i, j: (i, j)),
        out_shape=jax.ShapeDtypeStruct((T, 2 * F), BF16), compiler_params=_params("parallel", "parallel"),
    )(dz, w_down, u)


def _hgout_fwd(o_a, pm, g, name):
    T, D = o_a.shape
    tr = _tile(T, ROW_TILE)

    def body(o_ref, ga_ref, g_ref, out_ref):
        ov = o_ref[...]
        r = lax.rsqrt(jnp.mean(ov * ov, axis=-1, keepdims=True) + EPS)
        ga = ga_ref[...].astype(F32)
        out_ref[...] = (ov * r * g_ref[...] * (ga * _sig(ga))).astype(out_ref.dtype)

    row = pl.BlockSpec((tr, D), lambda i: (i, 0))
    return pl.pallas_call(
        body, name=name, grid=(T // tr,),
        in_specs=[row, pl.BlockSpec((tr, D), lambda i: (i, 3)), pl.BlockSpec((1, D), lambda i: (0, 0))],
        out_specs=row, out_shape=jax.ShapeDtypeStruct((T, D), BF16), compiler_params=_params("parallel"),
    )(o_a, pm, g)


def _hgout_bwd(o_a, pm, g, d_out, name):
    T, D = o_a.shape
    tr = _tile(T, ROW_TILE)

    def body(o_ref, ga_ref, g_ref, d_ref, do_ref, dga_ref, dg_ref):
        @pl.when(pl.program_id(0) == 0)
        def _():
            dg_ref[...] = jnp.zeros_like(dg_ref)

        ov = o_ref[...]
        r = lax.rsqrt(jnp.mean(ov * ov, axis=-1, keepdims=True) + EPS)
        oh = ov * r
        ga = ga_ref[...].astype(F32)
        s = _sig(ga)
        d = d_ref[...].astype(F32)
        dn = d * (ga * s)
        dga_ref[...] = (d * (oh * g_ref[...]) * (s * (1.0 + ga * (1.0 - s)))).astype(dga_ref.dtype)
        dxh = dn * g_ref[...]
        do_ref[...] = (r * (dxh - oh * jnp.mean(dxh * oh, axis=-1, keepdims=True))).astype(do_ref.dtype)
        dg_ref[...] += _colsum(dn * oh)

    row = pl.BlockSpec((tr, D), lambda i: (i, 0))
    vec = pl.BlockSpec((1, D), lambda i: (0, 0))
    return pl.pallas_call(
        body, name=name, grid=(T // tr,), in_specs=[row, pl.BlockSpec((tr, D), lambda i: (i, 3)), vec, row],
        out_specs=[row, row, vec],
        out_shape=[jax.ShapeDtypeStruct((T, D), BF16), jax.ShapeDtypeStruct((T, D), BF16),
                   jax.ShapeDtypeStruct((1, D), F32)],
        compiler_params=_params("arbitrary"),
    )(o_a, pm, g, d_out)


def _merge_fwd(ya, yb, pg, bg, name):
    T, D = ya.shape
    tr = _tile(T, 256)

    def body(ya_ref, yb_ref, pg_ref, bg_ref, o_ref):
        g0 = _sig(pg_ref[:, :D] + bg_ref[:, :D])
        g1 = _sig(pg_ref[:, D:] + bg_ref[:, D:])
        o_ref[...] = (g0 * ya_ref[...] + g1 * yb_ref[...]).astype(o_ref.dtype)

    row = pl.BlockSpec((tr, D), lambda i: (i, 0))
    return pl.pallas_call(
        body, name=name, grid=(T // tr,),
        in_specs=[row, row, pl.BlockSpec((tr, 2 * D), lambda i: (i, 0)), pl.BlockSpec((1, 2 * D), lambda i: (0, 0))],
        out_specs=row, out_shape=jax.ShapeDtypeStruct((T, D), BF16), compiler_params=_params("parallel"),
    )(ya, yb, pg, bg)


def _merge_bwd(dy, ya, yb, pg, bg, name):
    T, D = ya.shape
    tr = _tile(T, 256)

    def body(dy_ref, ya_ref, yb_ref, pg_ref, bg_ref, dya_ref, dyb_ref, dpg_ref, dbg_ref):
        @pl.when(pl.program_id(0) == 0)
        def _():
            dbg_ref[...] = jnp.zeros_like(dbg_ref)

        d = dy_ref[...].astype(F32)
        g0 = _sig(pg_ref[:, :D] + bg_ref[:, :D])
        g1 = _sig(pg_ref[:, D:] + bg_ref[:, D:])
        dya_ref[...] = (d * g0).astype(dya_ref.dtype)
        dyb_ref[...] = (d * g1).astype(dyb_ref.dtype)
        dg0 = d * ya_ref[...] * (g0 * (1.0 - g0))
        dg1 = d * yb_ref[...] * (g1 * (1.0 - g1))
        dpg_ref[:, :D] = dg0.astype(dpg_ref.dtype)
        dpg_ref[:, D:] = dg1.astype(dpg_ref.dtype)
        dbg_ref[:, :D] += _colsum(dg0)
        dbg_ref[:, D:] += _colsum(dg1)

    row = pl.BlockSpec((tr, D), lambda i: (i, 0))
    wide = pl.BlockSpec((tr, 2 * D), lambda i: (i, 0))
    wvec = pl.BlockSpec((1, 2 * D), lambda i: (0, 0))
    return pl.pallas_call(
        body, name=name, grid=(T // tr,), in_specs=[row, row, row, wide, wvec],
        out_specs=[row, row, wide, wvec],
        out_shape=[jax.ShapeDtypeStruct((T, D), BF16), jax.ShapeDtypeStruct((T, D), BF16),
                   jax.ShapeDtypeStruct((T, 2 * D), BF16), jax.ShapeDtypeStruct((1, 2 * D), F32)],
        compiler_params=_params("arbitrary"),
    )(dy, ya, yb, pg, bg)


def _hgrn_chunk_terms(q, fl, lb, tri):
    shape = q.shape
    row = _iota(shape, 0)
    sg = _sig(fl)
    f = lb + (1.0 - lb) * sg
    k = 1.0 - f
    b = _tri_dot(tri, jnp.log(f))
    ref1 = jnp.where(row < HALF, _pick_row(b, HALF // 2), _pick_row(b, HALF + HALF // 2))
    b_half = _pick_row(b, HALF - 1)
    b_last = _pick_row(b, CHUNK - 1)
    sq = _sig(q)
    qs = q * sq
    e_q1 = jnp.exp(jnp.minimum(b - ref1, EXP_CLAMP))
    e_k1 = jnp.exp(jnp.minimum(ref1 - b, EXP_CLAMP))
    e_q2 = jnp.exp(jnp.minimum(b - b_half, 0.0))
    e_k2 = jnp.exp(jnp.minimum(b_half - b, 0.0))
    e_b = jnp.exp(b)
    e_kd = jnp.exp(b_last - b)
    return dict(sg=sg, f=f, k=k, sq=sq, qs=qs, e_q1=e_q1, e_k1=e_k1, e_q2=e_q2, e_k2=e_k2, e_b=e_b, e_kd=e_kd,
                e_last=jnp.exp(b_last))


def _hgrn_masks():
    r = _iota((CHUNK, CHUNK), 0)
    c = _iota((CHUNK, CHUNK), 1)
    causal = r >= c
    same = (r < HALF) == (c < HALF)
    return causal, causal & same, (r >= HALF) & (c < HALF)


def _softmax_lb(lbl_ref):
    l0, l1 = lbl_ref[0, 0], lbl_ref[1, 0]
    mx = jnp.maximum(l0, l1)
    e0, e1 = jnp.exp(l0 - mx), jnp.exp(l1 - mx)
    return e0 / (e0 + e1)


def _hgrn_fwd(pm, lbl, name):
    T = pm.shape[0]
    tb = _tile(T, SEQ_BLOCK)
    nc = tb // CHUNK

    def body(q_ref, f_ref, i_ref, lbl_ref, o_ref, st_ref, s_sc):
        @pl.when(pl.program_id(1) == 0)
        def _():
            s_sc[...] = jnp.zeros_like(s_sc)

        lb = _softmax_lb(lbl_ref)
        causal, m1, m2 = _hgrn_masks()
        tri = jnp.where(causal, 1.0, 0.0).astype(BF16)
        parts = []
        for ci in range(nc):
            sl = pl.ds(ci * CHUNK, CHUNK)
            t = _hgrn_chunk_terms(q_ref[sl, :].astype(F32), f_ref[sl, :].astype(F32), lb, tri)
            iv = i_ref[sl, :]
            a1 = _dot((t["qs"] * t["e_q1"]).astype(BF16), (t["k"] * t["e_k1"]).astype(BF16), "nt")
            a2 = _dot((t["qs"] * t["e_q2"]).astype(BF16), (t["k"] * t["e_k2"]).astype(BF16), "nt")
            a = jnp.where(m1, a1, 0.0) + jnp.where(m2, a2, 0.0)
            parts.append((_dot(a.astype(BF16), iv), (t["qs"] * t["e_b"]).astype(BF16),
                          _dot(iv, (t["k"] * t["e_kd"]).astype(BF16), "tn"), t["e_last"]))
        st = s_sc[...]
        for ci, (o_intra, qi, grow, e_last) in enumerate(parts):
            st_ref[0, ci] = st
            o_ref[pl.ds(ci * CHUNK, CHUNK), :] = o_intra + _dot(qi, st.astype(BF16), "nt")
            st = e_last * st + grow
        s_sc[...] = st

    blk = lambda off: pl.BlockSpec((tb, DH), lambda h, b: (b, off + h))
    return pl.pallas_call(
        body, name=name, grid=(HEADS, T // tb),
        in_specs=[blk(0), blk(HEADS), blk(2 * HEADS), pl.BlockSpec((2, 1, 1, DH), lambda h, b: (0, h, 0, 0))],
        out_specs=[pl.BlockSpec((tb, DH), lambda h, b: (b, h)),
                   pl.BlockSpec((1, nc, DH, DH), lambda h, b: (h, b, 0, 0))],
        out_shape=[jax.ShapeDtypeStruct((T, HEADS * DH), F32),
                   jax.ShapeDtypeStruct((HEADS, T // CHUNK, DH, DH), F32)],
        scratch_shapes=[pltpu.VMEM((DH, DH), F32)],
        compiler_params=_params("parallel", "arbitrary"),
    )(pm, pm, pm, lbl)


def _hgrn_bwd(pm, lbl, states, do, name):
    T = pm.shape[0]
    tb = _tile(T, SEQ_BLOCK)
    nc = tb // CHUNK
    nb = T // tb

    def body(q_ref, f_ref, i_ref, lbl_ref, st_ref, do_ref, dq_ref, df_ref, di_ref, dl_ref, ds_sc, dlb_sc):
        @pl.when(pl.program_id(1) == 0)
        def _():
            ds_sc[...] = jnp.zeros_like(ds_sc)
            dlb_sc[...] = jnp.zeros_like(dlb_sc)

        lb = _softmax_lb(lbl_ref)
        causal, m1, m2 = _hgrn_masks()
        tri = jnp.where(causal, 1.0, 0.0).astype(BF16)
        tri_rev = jnp.where(_iota((CHUNK, CHUNK), 0) <= _iota((CHUNK, CHUNK), 1), 1.0, 0.0).astype(BF16)
        last_row = _iota((CHUNK, DH), 0) == CHUNK - 1
        for ci in reversed(range(nc)):
            sl = pl.ds(ci * CHUNK, CHUNK)
            q = q_ref[sl, :].astype(F32)
            t = _hgrn_chunk_terms(q, f_ref[sl, :].astype(F32), lb, tri)
            iv = i_ref[sl, :]
            dov = do_ref[sl, :]
            qe1, ke1 = t["qs"] * t["e_q1"], t["k"] * t["e_k1"]
            qe2, ke2 = t["qs"] * t["e_q2"], t["k"] * t["e_k2"]
            qi, kd = t["qs"] * t["e_b"], t["k"] * t["e_kd"]
            qe1b, ke1b, qe2b, ke2b = qe1.astype(BF16), ke1.astype(BF16), qe2.astype(BF16), ke2.astype(BF16)
            a = jnp.where(m1, _dot(qe1b, ke1b, "nt"), 0.0) + jnp.where(m2, _dot(qe2b, ke2b, "nt"), 0.0)
            st = st_ref[0, ci]
            dsn = ds_sc[...]
            dsnb = dsn.astype(BF16)
            da = _dot(dov, iv, "nt")
            da1 = jnp.where(m1, da, 0.0).astype(BF16)
            da2 = jnp.where(m2, da, 0.0).astype(BF16)
            di_ref[sl, :] = (_dot(a.astype(BF16), dov, "tn") + _dot(kd.astype(BF16), dsnb, "nt")).astype(di_ref.dtype)
            dqe1, dke1 = _dot(da1, ke1b), _dot(da1, qe1b, "tn")
            dqe2, dke2 = _dot(da2, ke2b), _dot(da2, qe2b, "tn")
            dqi = _dot(dov, st.astype(BF16))
            dkd = _dot(iv, dsnb)
            ds_sc[...] = t["e_last"] * dsn + _dot(dov, qi.astype(BF16), "tn")
            dqs = dqe1 * t["e_q1"] + dqe2 * t["e_q2"] + dqi * t["e_b"]
            dk = dke1 * t["e_k1"] + dke2 * t["e_k2"] + dkd * t["e_kd"]
            qib, kdb = qi.astype(BF16).astype(F32), kd.astype(BF16).astype(F32)
            db = (dqe1 * qe1b.astype(F32) - dke1 * ke1b.astype(F32) + dqe2 * qe2b.astype(F32)
                  - dke2 * ke2b.astype(F32) + dqi * qib - dkd * kdb)
            extra = _colsum(dkd * kdb) + t["e_last"] * _colsum(dsn * st)
            db = db + jnp.where(last_row, extra, 0.0)
            dlf = _tri_dot(tri_rev, db)
            dfv = dlf / t["f"] - dk
            sg = t["sg"]
            df_ref[sl, :] = (dfv * (1.0 - lb) * sg * (1.0 - sg)).astype(df_ref.dtype)
            dlb_sc[...] += _colsum(dfv * (1.0 - sg))
            sq = t["sq"]
            dq_ref[sl, :] = (dqs * (sq * (1.0 + q * (1.0 - sq)))).astype(dq_ref.dtype)

        @pl.when(pl.program_id(1) == nb - 1)
        def _():
            dl0 = dlb_sc[...] * lb * (1.0 - lb)
            dl_ref[0, 0] = dl0
            dl_ref[1, 0] = -dl0

    blk = lambda off: pl.BlockSpec((tb, DH), lambda h, b: (nb - 1 - b, off + h))
    lspec = pl.BlockSpec((2, 1, 1, DH), lambda h, b: (0, h, 0, 0))
    out_blk = pl.BlockSpec((tb, DH), lambda h, b: (nb - 1 - b, h))
    D = HEADS * DH
    return pl.pallas_call(
        body, name=name, grid=(HEADS, nb),
        in_specs=[blk(0), blk(HEADS), blk(2 * HEADS), lspec,
                  pl.BlockSpec((1, nc, DH, DH), lambda h, b: (h, nb - 1 - b, 0, 0)), out_blk],
        out_specs=[out_blk, out_blk, out_blk, lspec],
        out_shape=[jax.ShapeDtypeStruct((T, D), BF16)] * 3 + [jax.ShapeDtypeStruct((2, HEADS, 1, DH), F32)],
        scratch_shapes=[pltpu.VMEM((DH, DH), F32), pltpu.VMEM((1, DH), F32)],
        compiler_params=_params("parallel", "arbitrary"),
    )(pm, pm, pm, lbl, states, do)


def _log_sigmoid(x):
    return jnp.minimum(x, 0.0) - jnp.log(1.0 + jnp.exp(-jnp.abs(x)))


def _fox_cumsum(pf, bias, name):
    T = pf.shape[0]
    tb = _tile(T, SEQ_BLOCK)

    def body(x_ref, b_ref, c_ref, carry):
        @pl.when(pl.program_id(0) == 0)
        def _():
            carry[...] = jnp.zeros_like(carry)

        tri = jnp.where(_iota((tb, tb), 0) >= _iota((tb, tb), 1), 1.0, 0.0).astype(BF16)
        c = _tri_dot(tri, _log_sigmoid(x_ref[...] + b_ref[...])) + carry[...]
        c_ref[...] = c
        carry[...] = _pick_row(c, tb - 1)

    row = pl.BlockSpec((tb, LANES), lambda i: (i, 0))
    return pl.pallas_call(
        body, name=name, grid=(T // tb,), in_specs=[row, pl.BlockSpec((1, LANES), lambda i: (0, 0))],
        out_specs=row, out_shape=jax.ShapeDtypeStruct((T, LANES), F32),
        scratch_shapes=[pltpu.VMEM((1, LANES), F32)], compiler_params=_params("arbitrary"),
    )(pf, bias)


def _fox_dcum(dc, pf, bias, name):
    T = pf.shape[0]
    tb = _tile(T, SEQ_BLOCK)
    nb = T // tb

    def body(dc_ref, x_ref, b_ref, dx_ref, db_ref, carry):
        @pl.when(pl.program_id(0) == 0)
        def _():
            carry[...] = jnp.zeros_like(carry)
            db_ref[...] = jnp.zeros_like(db_ref)

        tri_rev = jnp.where(_iota((tb, tb), 0) <= _iota((tb, tb), 1), 1.0, 0.0).astype(BF16)
        dls = _tri_dot(tri_rev, dc_ref[...]) + carry[...]
        carry[...] = _pick_row(dls, 0)
        dx = dls * (1.0 - _sig(x_ref[...] + b_ref[...]))
        dx_ref[...] = dx
        db_ref[...] += _colsum(dx)

    row = pl.BlockSpec((tb, LANES), lambda i: (nb - 1 - i, 0))
    vec = pl.BlockSpec((1, LANES), lambda i: (0, 0))
    return pl.pallas_call(
        body, name=name, grid=(nb,), in_specs=[row, row, vec], out_specs=[row, vec],
        out_shape=[jax.ShapeDtypeStruct((T, LANES), F32), jax.ShapeDtypeStruct((1, LANES), F32)],
        scratch_shapes=[pltpu.VMEM((1, LANES), F32)], compiler_params=_params("arbitrary"),
    )(dc, pf, bias)


_Q_OFF, _K_OFF, _V_OFF = 4 * HEADS, 5 * HEADS, 6 * HEADS


def _causal_pairs(nq, by_key):
    if by_key:
        pairs = [(i, j) for j in range(nq) for i in range(j, nq)]
    else:
        pairs = [(i, j) for i in range(nq) for j in range(i + 1)]
    return jnp.asarray([p[0] for p in pairs], jnp.int32), jnp.asarray([p[1] for p in pairs], jnp.int32)


def _fox_logits(q, k, ck, row0, masked):
    s = _dot(q, k, "nt") - ck
    if masked:
        s = jnp.where(_iota(s.shape, 0) + row0 >= _iota(s.shape, 1), s, NEG_BIG)
    return s


def _ones_column(rows):
    return jnp.where(_iota((rows, DH), 1) == 0, 1.0, 0.0).astype(BF16)


def _fox_fwd(pm, c_col, c_row, name):
    T = pm.shape[0]
    tq = _tile(T, ATTN_TILE)
    nq = T // tq
    rg = min(ATTN_ROWS, tq)
    qi_tab, kj_tab = _causal_pairs(nq, by_key=False)

    def body(qi_ref, kj_ref, q_ref, k_ref, v_ref, cq_ref, ck_ref, o_ref, lse_ref, m_sc, acc_sc):
        t = pl.program_id(1)
        i, j = qi_ref[t], kj_ref[t]

        @pl.when(j == 0)
        def _():
            m_sc[...] = jnp.full_like(m_sc, NEG_BIG)
            acc_sc[...] = jnp.zeros_like(acc_sc)

        def step(diag):
            m_all, acc_all = m_sc[...], acc_sc[...]
            ones = _ones_column(tq)
            ms, accs = [], []
            for r in range(tq // rg):
                rows = slice(r * rg, (r + 1) * rg)
                w = (r + 1) * rg if diag else tq
                cq = cq_ref[0, rows, :]
                s = _fox_logits(q_ref[rows, :], k_ref[:w, :], ck_ref[0, :, :w], r * rg, diag)
                m_old = m_all[rows, :]
                m_new = jnp.maximum(m_old, jnp.max(s, axis=1, keepdims=True) + cq)
                alpha = jnp.exp(m_old - m_new)
                p = jnp.exp(s - (m_new - cq)).astype(BF16)
                v_one = jnp.concatenate([v_ref[:w, :], ones[:w, :]], axis=1)
                ms.append(m_new)
                accs.append(alpha * acc_all[rows, :] + _dot(p, v_one))
            m_sc[...] = jnp.concatenate(ms, axis=0)
            acc_sc[...] = jnp.concatenate(accs, axis=0)

        @pl.when(j < i)
        def _():
            step(False)

        @pl.when(j == i)
        def _():
            step(True)
            acc = acc_sc[...]
            denom = acc[:, DH:DH + 1]
            o_ref[...] = (acc[:, :DH] / denom).astype(o_ref.dtype)
            lse_ref[0] = m_sc[...] + jnp.log(denom)

    kv = lambda off: pl.BlockSpec((tq, DH), lambda h, t, qi, kj: (kj[t], off + h))
    col = pl.BlockSpec((1, tq, 1), lambda h, t, qi, kj: (h, qi[t], 0))
    grid_spec = pltpu.PrefetchScalarGridSpec(
        num_scalar_prefetch=2, grid=(HEADS, qi_tab.shape[0]),
        in_specs=[pl.BlockSpec((tq, DH), lambda h, t, qi, kj: (qi[t], _Q_OFF + h)), kv(_K_OFF), kv(_V_OFF), col,
                  pl.BlockSpec((1, 1, tq), lambda h, t, qi, kj: (h, 0, kj[t]))],
        out_specs=[pl.BlockSpec((tq, DH), lambda h, t, qi, kj: (qi[t], h)), col],
        scratch_shapes=[pltpu.VMEM((tq, 1), F32), pltpu.VMEM((tq, 2 * DH), F32)])
    return pl.pallas_call(
        body, name=name, grid_spec=grid_spec,
        out_shape=[jax.ShapeDtypeStruct((T, HEADS * DH), BF16), jax.ShapeDtypeStruct((HEADS, T, 1), F32)],
        compiler_params=_params("parallel", "arbitrary"),
    )(qi_tab, kj_tab, pm, pm, pm, c_col, c_row)


def _fox_delta(do, o, name):
    T, D = o.shape
    tr = _tile(T, ROW_TILE)

    def body(do_ref, o_ref, d_ref):
        prod = do_ref[...].astype(F32) * o_ref[...].astype(F32)
        for h in range(HEADS):
            d_ref[h] = _rowsum(prod[:, h * DH:(h + 1) * DH])

    row = pl.BlockSpec((tr, D), lambda i: (i, 0))
    return pl.pallas_call(
        body, name=name, grid=(T // tr,), in_specs=[row, row],
        out_specs=pl.BlockSpec((HEADS, tr, 1), lambda i: (0, i, 0)),
        out_shape=jax.ShapeDtypeStruct((HEADS, T, 1), F32), compiler_params=_params("parallel"),
    )(do, o)


def _fox_bwd(pm, c_col, c_row, do, lse, delta, name):
    T = pm.shape[0]
    tq = _tile(T, ATTN_TILE)
    nq = T // tq
    rg = min(ATTN_ROWS, tq)
    qi_tab, kj_tab = _causal_pairs(nq, by_key=True)
    npairs = qi_tab.shape[0]

    def body(qi_ref, kj_ref, q_ref, k_ref, v_ref, cq_ref, ck_ref, do_ref, lse_ref, dl_ref,
             dq_ref, dk_ref, dv_ref, rsum_ref, csum_ref, dq_sc, dk_sc, dv_sc):
        t = pl.program_id(1)
        i, j = qi_ref[t], kj_ref[t]

        @pl.when(t == 0)
        def _():
            dq_sc[...] = jnp.zeros_like(dq_sc)

        @pl.when(i == j)
        def _():
            dk_sc[...] = jnp.zeros_like(dk_sc)
            dv_sc[...] = jnp.zeros_like(dv_sc)

        base = pl.multiple_of(i * tq, tq)

        def step(diag):
            ones = _ones_column(tq)
            for r in range(tq // rg):
                rows = slice(r * rg, (r + 1) * rg)
                w = (r + 1) * rg if diag else tq
                qr, dor = q_ref[rows, :], do_ref[rows, :]
                s = _fox_logits(qr, k_ref[:w, :], ck_ref[0, :, :w], r * rg, diag)
                p = jnp.exp(s - (lse_ref[0, rows, :] - cq_ref[0, rows, :]))
                dp = _dot(dor, v_ref[:w, :], "nt")
                dsb = (p * (dp - dl_ref[0, rows, :])).astype(BF16)
                dv_sc[:w, :] += _dot(p.astype(BF16), dor, "tn")
                dk_sc[:w, :] += _dot(dsb, jnp.concatenate([qr, ones[rows, :]], axis=1), "tn")
                dq_sc[pl.ds(base + r * rg, rg), :] += _dot(dsb, jnp.concatenate([k_ref[:w, :], ones[:w, :]], axis=1))

        @pl.when(i > j)
        def _():
            step(False)

        @pl.when(i == j)
        def _():
            step(True)

        @pl.when(i == nq - 1)
        def _():
            dk_ref[...] = dk_sc[:, :DH].astype(dk_ref.dtype)
            dv_ref[...] = dv_sc[...].astype(dv_ref.dtype)
            csum_ref[0] = dk_sc[:, DH:DH + 1]

        @pl.when(t == npairs - 1)
        def _():
            dq_ref[...] = dq_sc[:, :DH].astype(dq_ref.dtype)
            rsum_ref[0] = dq_sc[:, DH:DH + 1]

    col = pl.BlockSpec((1, tq, 1), lambda h, t, qi, kj: (h, qi[t], 0))
    kv = lambda off: pl.BlockSpec((tq, DH), lambda h, t, qi, kj: (kj[t], off + h))
    kv_out = pl.BlockSpec((tq, DH), lambda h, t, qi, kj: (kj[t], h))
    grid_spec = pltpu.PrefetchScalarGridSpec(
        num_scalar_prefetch=2, grid=(HEADS, npairs),
        in_specs=[pl.BlockSpec((tq, DH), lambda h, t, qi, kj: (qi[t], _Q_OFF + h)), kv(_K_OFF), kv(_V_OFF), col,
                  pl.BlockSpec((1, 1, tq), lambda h, t, qi, kj: (h, 0, kj[t])),
                  pl.BlockSpec((tq, DH), lambda h, t, qi, kj: (qi[t], h)), col, col],
        out_specs=[pl.BlockSpec((T, DH), lambda h, t, qi, kj: (0, h)), kv_out, kv_out,
                   pl.BlockSpec((1, T, 1), lambda h, t, qi, kj: (h, 0, 0)),
                   pl.BlockSpec((1, tq, 1), lambda h, t, qi, kj: (h, kj[t], 0))],
        scratch_shapes=[pltpu.VMEM((T, 2 * DH), F32), pltpu.VMEM((tq, 2 * DH), F32), pltpu.VMEM((tq, DH), F32)])
    D = HEADS * DH
    return pl.pallas_call(
        body, name=name, grid_spec=grid_spec,
        out_shape=[jax.ShapeDtypeStruct((T, D), BF16)] * 3 + [jax.ShapeDtypeStruct((HEADS, T, 1), F32)] * 2,
        compiler_params=_params("parallel", "arbitrary"),
    )(qi_tab, kj_tab, pm, pm, pm, c_col, c_row, do, lse, delta)


def _xattn_fwd(q, kv, name):
    T, D = q.shape
    M = kv.shape[0]
    dh = D // MEM_HEADS
    tq = _tile(T, ATTN_TILE)
    scale = 1.0 / math.sqrt(dh)

    def body(q_ref, kv_ref, o_ref):
        for h in range(MEM_HEADS):
            cs = slice(h * dh, (h + 1) * dh)
            s = _dot(q_ref[:, cs], kv_ref[:, cs], "nt") * scale
            p = jnp.exp(s - jnp.max(s, axis=1, keepdims=True))
            p = p / _rowsum(p)
            o_ref[:, cs] = _dot(p.astype(BF16), kv_ref[:, D + h * dh:D + (h + 1) * dh]).astype(o_ref.dtype)

    row = pl.BlockSpec((tq, D), lambda i: (i, 0))
    return pl.pallas_call(
        body, name=name, grid=(T // tq,), in_specs=[row, pl.BlockSpec((M, 2 * D), lambda i: (0, 0))],
        out_specs=row, out_shape=jax.ShapeDtypeStruct((T, D), BF16), compiler_params=_params("parallel"),
    )(q, kv)


def _xattn_bwd(q, kv, do, name):
    T, D = q.shape
    M = kv.shape[0]
    dh = D // MEM_HEADS
    tq = _tile(T, ATTN_TILE)
    scale = 1.0 / math.sqrt(dh)

    def body(q_ref, kv_ref, do_ref, dq_ref, dkv_ref):
        @pl.when(pl.program_id(0) == 0)
        def _():
            dkv_ref[...] = jnp.zeros_like(dkv_ref)

        for h in range(MEM_HEADS):
            cs = slice(h * dh, (h + 1) * dh)
            vs = slice(D + h * dh, D + (h + 1) * dh)
            s = _dot(q_ref[:, cs], kv_ref[:, cs], "nt") * scale
            p = jnp.exp(s - jnp.max(s, axis=1, keepdims=True))
            p = p / _rowsum(p)
            dp = _dot(do_ref[:, cs], kv_ref[:, vs], "nt")
            ds = (p * (dp - _rowsum(p * dp)) * scale).astype(BF16)
            dq_ref[:, cs] = _dot(ds, kv_ref[:, cs]).astype(dq_ref.dtype)
            dkv_ref[:, cs] += _dot(ds, q_ref[:, cs], "tn")
            dkv_ref[:, vs] += _dot(p.astype(BF16), do_ref[:, cs], "tn")

    row = pl.BlockSpec((tq, D), lambda i: (i, 0))
    full = pl.BlockSpec((M, 2 * D), lambda i: (0, 0))
    return pl.pallas_call(
        body, name=name, grid=(T // tq,), in_specs=[row, full, row], out_specs=[row, full],
        out_shape=[jax.ShapeDtypeStruct((T, D), BF16), jax.ShapeDtypeStruct((M, 2 * D), F32)],
        compiler_params=_params("arbitrary"),
    )(q, kv, do)


_HBM = pl.BlockSpec(memory_space=pltpu.HBM)


def _position():
    return lax.axis_index("x"), lax.axis_index("y"), lax.axis_index("c")


def _other_chips(x, y):
    return [(1 - x, y), (x, 1 - y), (1 - x, 1 - y)]


class _Exchange:
    def __init__(self, inputs, out_shapes, scratch, copies, inplace=False):
        self.inputs, self.out_shapes, self.scratch, self.copies, self.inplace = inputs, out_shapes, scratch, copies, inplace

    def start(self, in_refs, out_refs, sems):
        for cp in self.copies(in_refs, out_refs, sems, False)[0]:
            cp.start()

    def wait(self, in_refs, out_refs, sems):
        for cp, how in self.copies(in_refs, out_refs, sems, True)[1]:
            getattr(cp, how)()

    def aliases(self, first_input, first_output):
        return {first_input + w: first_output + w for w in range(len(self.inputs))} if self.inplace else {}


def _run_exchange(ex, name):
    n_in, n_out = len(ex.inputs), len(ex.out_shapes)

    def body(*refs):
        parts = refs[:n_in], refs[n_in:n_in + n_out], refs[n_in + n_out:]
        ex.start(*parts)
        ex.wait(*parts)

    return pl.pallas_call(
        body, name=name, in_specs=[_HBM] * n_in, out_specs=[_HBM] * n_out, out_shape=ex.out_shapes,
        input_output_aliases=ex.aliases(0, 0), scratch_shapes=ex.scratch,
    )(*ex.inputs)


def _chip_exchange(arrays, out_shapes, src_of, dst_of):
    n = len(arrays)

    def copies(srcs, outs, sems, waiting):
        send, recv, local = sems
        x, y, c = _position()
        q = 2 * x + y
        kept, sent, arriving = [], [], []
        for w, (s_ref, o_ref) in enumerate(zip(srcs, outs)):
            kept.append(pltpu.make_async_copy(src_of(s_ref, q, c), dst_of(o_ref, q, c), local.at[w]))
            for j, (px, py) in enumerate(_other_chips(x, y)):
                sems_j = dict(send_sem=send.at[3 * w + j], recv_sem=recv.at[3 * w + j], device_id=(px, py, c),
                              device_id_type=MESH)
                sent.append(pltpu.make_async_remote_copy(src_ref=src_of(s_ref, 2 * px + py, c),
                                                         dst_ref=dst_of(o_ref, q, c), **sems_j))
                if waiting:
                    arriving.append(pltpu.make_async_remote_copy(src_ref=src_of(s_ref, q, c),
                                                                 dst_ref=dst_of(o_ref, 2 * px + py, c), **sems_j))
        return kept + sent, ([(cp, "wait_recv") for cp in arriving] + [(cp, "wait_send") for cp in sent]
                             + [(cp, "wait") for cp in kept])

    scratch = [pltpu.SemaphoreType.DMA((3 * n,)), pltpu.SemaphoreType.DMA((3 * n,)), pltpu.SemaphoreType.DMA((n,))]
    return _Exchange(arrays, out_shapes, scratch, copies)


def _ex_ag_chips(blks):
    return _chip_exchange(blks, [jax.ShapeDtypeStruct((4, 2) + b.shape, b.dtype) for b in blks],
                          src_of=lambda r, chip, c: r, dst_of=lambda r, chip, c: r.at[chip, c])


def _ex_rs_chips(parts):
    return _chip_exchange(parts, [jax.ShapeDtypeStruct(h.shape, h.dtype) for h in parts],
                          src_of=lambda r, chip, c: r.at[chip], dst_of=lambda r, chip, c: r.at[chip])


def _ex_ag_sibling(arrs):
    n = len(arrs)

    def copies(ins, outs, sems, waiting):
        send, recv = sems
        x, y, c = _position()
        to = dict(device_id=(x, y, 1 - c), device_id_type=MESH)
        mine = [pltpu.make_async_remote_copy(src_ref=a.at[:, c], dst_ref=a.at[:, c], send_sem=send.at[w],
                                             recv_sem=recv.at[w], **to) for w, a in enumerate(outs)]
        theirs = [pltpu.make_async_remote_copy(src_ref=a.at[:, c], dst_ref=a.at[:, 1 - c], send_sem=send.at[w],
                                               recv_sem=recv.at[w], **to) for w, a in enumerate(outs if waiting else [])]
        return mine, [(cp, "wait_recv") for cp in theirs] + [(cp, "wait_send") for cp in mine]

    return _Exchange(arrs, [jax.ShapeDtypeStruct(a.shape, a.dtype) for a in arrs],
                     [pltpu.SemaphoreType.DMA((n,)), pltpu.SemaphoreType.DMA((n,))], copies, inplace=True)


def _ex_rs_sibling(blocks):
    n = len(blocks)

    def copies(srcs, outs, sems, waiting):
        send, recv = sems
        x, y, c = _position()
        cps = [pltpu.make_async_remote_copy(src_ref=b.at[:, 1 - c], dst_ref=l, send_sem=send.at[w], recv_sem=recv.at[w],
                                            device_id=(x, y, 1 - c), device_id_type=MESH)
               for w, (b, l) in enumerate(zip(srcs, outs))]
        return cps, [(cp, "wait") for cp in cps]

    return _Exchange(blocks, [jax.ShapeDtypeStruct((4,) + b.shape[2:], b.dtype) for b in blocks],
                     [pltpu.SemaphoreType.DMA((n,)), pltpu.SemaphoreType.DMA((n,))], copies)


def _row_tile(rows, pref=256):
    for t in range(min(pref, rows) // 16 * 16, 0, -16):
        if rows % t == 0:
            return t
    raise ValueError(f"no row tile for {rows}")


def _pair_add(blocks, landed, core, out_dtype, name):
    n, _, s0, s1 = blocks.shape
    tr = _row_tile(s0)

    def body(core_ref, a_ref, b_ref, o_ref):
        del core_ref
        o_ref[...] = (a_ref[...] + b_ref[...]).astype(o_ref.dtype)

    grid_spec = pltpu.PrefetchScalarGridSpec(
        num_scalar_prefetch=1, grid=(n, s0 // tr),
        in_specs=[pl.BlockSpec((1, None, tr, s1), lambda p, i, core: (p, core[0], i, 0)),
                  pl.BlockSpec((1, tr, s1), lambda p, i, core: (p, i, 0))],
        out_specs=pl.BlockSpec((1, tr, s1), lambda p, i, core: (p, i, 0)))
    return pl.pallas_call(
        body, name=name, grid_spec=grid_spec, out_shape=jax.ShapeDtypeStruct(landed.shape, out_dtype),
        compiler_params=_params("parallel", "parallel"),
    )(core, blocks, landed)


def _adamw_math(w, g, m, v):
    m = ADAM_B1 * m + (1.0 - ADAM_B1) * g
    v = ADAM_B2 * v + (1.0 - ADAM_B2) * (g * g)
    m_hat = m / (1.0 - ADAM_B1 ** ADAM_STEP)
    v_hat = v / (1.0 - ADAM_B2 ** ADAM_STEP)
    delta = -ADAM_LR * (m_hat / (jnp.sqrt(v_hat) + ADAM_EPS) + ADAM_WD * w)
    return delta, m, v


def _adamw_reduce(slots, w, m, v, name):
    n, R, C = slots.shape
    tr = _row_tile(R)

    def body(s_ref, w_ref, m_ref, v_ref, g_ref, d_ref, nm_ref, nv_ref):
        g = s_ref[0].astype(F32)
        for p in range(1, n):
            g = g + s_ref[p].astype(F32)
        g_ref[...] = g
        d_ref[...], nm_ref[...], nv_ref[...] = _adamw_math(w_ref[...], g, m_ref[...], v_ref[...])

    row = pl.BlockSpec((tr, C), lambda i: (i, 0))
    return pl.pallas_call(
        body, name=name, grid=(R // tr,), in_specs=[pl.BlockSpec((n, tr, C), lambda i: (0, i, 0)), row, row, row],
        out_specs=[row] * 4, out_shape=[jax.ShapeDtypeStruct((R, C), F32)] * 4, compiler_params=_params("parallel"),
    )(slots, w, m, v)


def _full_from_gathered(a, n):
    s0, s1 = a.shape[2:]
    blk = a.reshape(8, s0, s1)
    if n in COL_SHARDED:
        return blk.transpose(1, 0, 2).reshape(s0, 8 * s1)
    return blk.reshape(8 * s0, s1)


def _blocks_from_full(g, n, shard_shape):
    s0, s1 = shard_shape
    if n in COL_SHARDED:
        blk = g.reshape(s0, 8, s1).transpose(1, 0, 2)
    else:
        blk = g.reshape(8, s0, s1)
    return blk.reshape(4, 2, s0, s1)


def _swiglu_interleave(w):
    d, f2 = w.shape
    return w.reshape(d, 2, f2 // (2 * SWIGLU_TILE), SWIGLU_TILE).transpose(0, 2, 1, 3).reshape(d, f2)


def _swiglu_deinterleave(w):
    d, f2 = w.shape
    return w.reshape(d, f2 // (2 * SWIGLU_TILE), 2, SWIGLU_TILE).transpose(0, 2, 1, 3).reshape(d, f2)


SMALL_ROWS = 16


def _pack_small(vals, loss_row):
    rows = []
    for n in SMALL:
        flat = vals[n].reshape(-1)
        pad = (-flat.shape[0]) % PACK_COLS
        rows.append(jnp.pad(flat, (0, pad)).reshape(-1, PACK_COLS))
    rows.append(loss_row)
    out = jnp.concatenate(rows, axis=0)
    assert out.shape[0] == SMALL_ROWS, out.shape
    return out


def _unpack_small(packed, like):
    out, r = {}, 0
    for n in SMALL:
        size = like[n].size
        rows = -(-size // PACK_COLS)
        out[n] = packed[r:r + rows].reshape(-1)[:size].reshape(like[n].shape)
        r += rows
    return out


class _NoTraffic:
    def host(self, stage):
        return None

    def landed(self, stage, arrays):
        pass

    def grads_ready(self, names, gW):
        pass


def _mm_behind(traffic, stage, *args, **kwargs):
    ex = traffic.host(stage)
    if ex is None:
        return _mm(*args, **kwargs)
    out, arrays = _mm(*args, hosted=ex, **kwargs)
    traffic.landed(stage, arrays)
    return out


def _ffn_fwd(x, g_pre, W, tag, traffic, up_stage=None, down_stage=None):
    h = _rms_fwd(x, g_pre, f"{tag}_pre")
    ex = traffic.host(up_stage) if up_stage else None
    u, a, arrays = _mm_swiglu(h, W[f"{tag}_w_in"], f"{tag}_up", hosted=ex)
    if ex is not None:
        traffic.landed(up_stage, arrays)
    z = _mm_behind(traffic, down_stage, a, W[f"{tag}_w_down"], "nn", F32, f"{tag}_down", tk=1408)
    return h, u, a, z


def _ffn_bwd(saved, x, g_pre, w_in, w_down, g_post, dx_out, tag, traffic, up_dx_stage=None):
    h, u, a, z = saved
    dz, dg_post = _rms_bwd(z, g_post, dx_out, 0.5, f"{tag}_post_bwd", BF16)
    dw_down = _mm(a, dz, "tn", F32, f"{tag}_down_dw", tm=1408)
    du = _mm_swiglu_bwd(dz, w_down, u, f"{tag}_down_dx")
    dh = _mm_behind(traffic, up_dx_stage, du, w_in, "nt", BF16, f"{tag}_up_dx", tk=5632)
    dw_in = _mm(h, du, "tn", F32, f"{tag}_up_dw")
    dx, dg_pre = _rms_bwd(x, g_pre, dh, 1.0, f"{tag}_pre_bwd", F32, resid=dx_out)
    return dx, dg_pre, dg_post, dw_in, dw_down


def _step_local(x, mem, target, W, S, traffic=_NoTraffic()):
    T, D = x.shape
    gW, gS = {}, {}

    f1 = _ffn_fwd(x, S["ffn1_pre_g"], W, "ffn1", traffic, "gather_mixer_chips", "gather_mixer_sibling")
    x1 = _resid_rms(x, f1[3], S["ffn1_post_g"], 0.5, "ffn1_post")

    h2 = _rms_fwd(x1, S["mix_pre_g"], "mix_pre")
    pm = _mm_behind(traffic, "gather_late_chips", h2, W["w_main"], "nn", BF16, "mix_proj_main")
    pf = _mm(h2, W["w_f"], "nn", F32, "mix_proj_f")
    pg = _mm_behind(traffic, "gather_late_sibling", h2, W["w_gates"], "nn", F32, "mix_proj_gates")
    lbl = S["hg_lb_logits"].reshape(2, HEADS, 1, DH)
    o_a, states = _hgrn_fwd(pm, lbl, "hgrn_fwd")
    oan = _hgout_fwd(o_a, pm, S["hg_norm_g"], "hgrn_out")
    bias = jnp.pad(S["fox_f_bias"], ((0, 0), (0, LANES - HEADS)))
    c = _fox_cumsum(pf, bias, "fox_cumsum")
    c_heads = c[:, :HEADS].T
    c_col, c_row = c_heads[:, :, None], c_heads[:, None, :]
    o_b, lse = _fox_fwd(pm, c_col, c_row, "fox_fwd")
    ya = _mm(oan, W["w_branch_a"], "nn", F32, "branch_a")
    yb = _mm(o_b, W["w_branch_b"], "nn", F32, "branch_b")
    y = _merge_fwd(ya, yb, pg, S["b_gate"], "merge")
    z2 = _mm(y, W["w_out"], "nn", F32, "mix_out")
    x2 = _resid_rms(x1, z2, S["mix_post_g"], 1.0, "mix_post")

    h3 = _rms_fwd(x2, S["mem_pre_g"], "mem_pre")
    memn = _rms_fwd(mem, S["mem_kv_g"], "mem_kv_norm")
    qm = _mm(h3, W["w_mq"], "nn", BF16, "mem_q")
    kv = _mm(memn, W["w_mkv"], "nn", BF16, "mem_kv")
    om = _xattn_fwd(qm, kv, "mem_attn")
    z3 = _mm(om, W["w_mo"], "nn", F32, "mem_o")
    x3 = _resid_rms(x2, z3, S["mem_post_g"], 1.0, "mem_post")

    f2 = _ffn_fwd(x3, S["ffn2_pre_g"], W, "ffn2", traffic)
    dx4, sq = _final_loss(x3, f2[3], S["ffn2_post_g"], 0.5, target, "loss")

    dx3, gS["ffn2_pre_g"], gS["ffn2_post_g"], gW["ffn2_w_in"], gW["ffn2_w_down"] = _ffn_bwd(
        f2, x3, S["ffn2_pre_g"], W["ffn2_w_in"], W["ffn2_w_down"], S["ffn2_post_g"], dx4, "ffn2", traffic)
    traffic.grads_ready(["ffn2_w_in", "ffn2_w_down"], gW)

    dz3, gS["mem_post_g"] = _rms_bwd(z3, S["mem_post_g"], dx3, 1.0, "mem_post_bwd", BF16)
    dom = _mm(dz3, W["w_mo"], "nt", BF16, "mem_o_dx")
    gW["w_mo"] = _mm(om, dz3, "tn", F32, "mem_o_dw")
    dqm, dkv = _xattn_bwd(qm, kv, dom, "mem_attn_bwd")
    dh3 = _mm(dqm, W["w_mq"], "nt", BF16, "mem_q_dx")
    gW["w_mq"] = _mm(h3, dqm, "tn", F32, "mem_q_dw")
    dkvb = dkv.astype(BF16)
    gW["w_mkv"] = _mm(memn, dkvb, "tn", F32, "mem_kv_dw")
    dmemn = _mm(dkvb, W["w_mkv"], "nt", F32, "mem_kv_dx")
    _, gS["mem_kv_g"] = _rms_bwd(mem, S["mem_kv_g"], dmemn, 1.0, "mem_kv_norm_bwd", BF16)
    dx2, gS["mem_pre_g"] = _rms_bwd(x2, S["mem_pre_g"], dh3, 1.0, "mem_pre_bwd", F32, resid=dx3)

    dz2, gS["mix_post_g"] = _rms_bwd(z2, S["mix_post_g"], dx2, 1.0, "mix_post_bwd", BF16)
    dy = _mm(dz2, W["w_out"], "nt", BF16, "mix_out_dx")
    gW["w_out"] = _mm(y, dz2, "tn", F32, "mix_out_dw")
    dya, dyb, dpg, gS["b_gate"] = _merge_bwd(dy, ya, yb, pg, S["b_gate"], "merge_bwd")
    doan = _mm(dya, W["w_branch_a"], "nt", BF16, "branch_a_dx")
    gW["w_branch_a"] = _mm(oan, dya, "tn", F32, "branch_a_dw")
    dob = _mm(dyb, W["w_branch_b"], "nt", BF16, "branch_b_dx")
    gW["w_branch_b"] = _mm(o_b, dyb, "tn", F32, "branch_b_dw")
    traffic.grads_ready(["w_mo", "w_mq", "w_mkv", "w_out", "w_branch_a", "w_branch_b"], gW)

    delta = _fox_delta(dob, o_b, "fox_delta")
    dq_b, dk_b, dv_b, ds_rows, ds_cols = _fox_bwd(pm, c_col, c_row, dob, lse, delta, "fox_bwd")
    dc = jnp.pad((ds_rows - ds_cols).reshape(HEADS, T).T, ((0, 0), (0, LANES - HEADS)))
    dpf, dbias = _fox_dcum(dc, pf, bias, "fox_cumsum_bwd")
    gS["fox_f_bias"] = dbias[:, :HEADS]

    do_a, dg_a, gS["hg_norm_g"] = _hgout_bwd(o_a, pm, S["hg_norm_g"], doan, "hgrn_out_bwd")
    dq_a, df_a, di_a, dlbl = _hgrn_bwd(pm, lbl, states, do_a, "hgrn_bwd")
    gS["hg_lb_logits"] = dlbl.reshape(2, HEADS, DH)

    dpm = jnp.concatenate([dq_a, df_a, di_a, dg_a, dq_b, dk_b, dv_b], axis=1)
    dpf16 = dpf.astype(BF16)
    dh2 = _mm_behind(traffic, "scatter_ffn2_chips", dpm, W["w_main"], "nt", F32, "mix_proj_main_dx")
    dh2 = _mm(dpg, W["w_gates"], "nt", F32, "mix_proj_gates_dx", add=dh2)
    dh2 = _mm(dpf16, W["w_f"], "nt", F32, "mix_proj_f_dx", add=dh2)
    gW["w_main"] = _mm_behind(traffic, "scatter_mid_chips", h2, dpm, "tn", F32, "mix_proj_main_dw")
    gW["w_gates"] = _mm(h2, dpg, "tn", F32, "mix_proj_gates_dw")
    gW["w_f"] = _mm(h2, dpf16, "tn", F32, "mix_proj_f_dw")
    traffic.grads_ready(["w_in"], gW)
    dx1, gS["mix_pre_g"] = _rms_bwd(x1, S["mix_pre_g"], dh2, 1.0, "mix_pre_bwd", F32, resid=dx2)

    dx0, gS["ffn1_pre_g"], gS["ffn1_post_g"], gW["ffn1_w_in"], gW["ffn1_w_down"] = _ffn_bwd(
        f1, x, S["ffn1_pre_g"], W["ffn1_w_in"], W["ffn1_w_down"], S["ffn1_post_g"], dx1, "ffn1", traffic,
        "scatter_w_in_chips")
    traffic.grads_ready(["ffn1_w_in", "ffn1_w_down"], gW)
    return sq, dx0, gW, gS


GATHER_FIRST = ["ffn1_w_in", "ffn1_w_down"]
GATHER_MIXER = ["w_in", "w_branch_a", "w_branch_b", "w_out"]
GATHER_LATE = ["w_mq", "w_mkv", "w_mo", "ffn2_w_in", "ffn2_w_down"]
SCATTER_BEHIND = {
    "scatter_ffn2_chips": ["ffn2_w_in", "ffn2_w_down"],
    "scatter_mid_chips": ["w_mo", "w_mq", "w_mkv", "w_out", "w_branch_a", "w_branch_b"],
    "scatter_w_in_chips": ["w_in"],
}


class _Traffic:
    def __init__(self, sent, W, shapes, core, D):
        self.sent, self.W, self.shapes, self.core, self.D = sent, W, shapes, core, D
        self.half, self.pairs, self.slots = {}, {}, {}

    def install(self, names, gathered):
        D = self.D
        for n, g in zip(names, gathered):
            full = _full_from_gathered(g, n)
            if n == "w_in":
                self.W["w_main"] = full[:, :7 * D]
                self.W["w_f"] = jnp.pad(full[:, 7 * D:7 * D + HEADS], ((0, 0), (0, LANES - HEADS)))
                self.W["w_gates"] = full[:, 7 * D + HEADS:]
            elif n in ("ffn1_w_in", "ffn2_w_in"):
                self.W[n] = _swiglu_interleave(full)
            else:
                self.W[n] = full

    def host(self, stage):
        if stage == "gather_mixer_chips":
            return _ex_ag_chips([self.sent[n] for n in GATHER_MIXER])
        if stage == "gather_late_chips":
            return _ex_ag_chips([self.sent[n] for n in GATHER_LATE])
        if stage in ("gather_mixer_sibling", "gather_late_sibling"):
            return _ex_ag_sibling(self.half[stage])
        if stage in SCATTER_BEHIND:
            return _ex_rs_chips([self.pairs[n] for n in SCATTER_BEHIND[stage]])
        return None

    def landed(self, stage, arrays):
        if stage == "gather_mixer_chips":
            self.half["gather_mixer_sibling"] = arrays
        elif stage == "gather_late_chips":
            self.half["gather_late_sibling"] = arrays
        elif stage == "gather_mixer_sibling":
            self.install(GATHER_MIXER, arrays)
        elif stage == "gather_late_sibling":
            self.install(GATHER_LATE, arrays)
        else:
            self.slots.update(zip(SCATTER_BEHIND[stage], arrays))

    def _final_grad(self, n, gW):
        D = self.D
        if n == "w_in":
            g = gW["w_main"]
            return jnp.concatenate([g[:, :4 * D], g[:, 4 * D:5 * D] * (1.0 / math.sqrt(DH)), g[:, 5 * D:],
                                    gW["w_f"][:, :HEADS], gW["w_gates"]], axis=1)
        if n in ("ffn1_w_in", "ffn2_w_in"):
            return _swiglu_deinterleave(gW[n])
        return gW[n]

    def grads_ready(self, names, gW):
        blocks = [_blocks_from_full(self._final_grad(n, gW), n, self.shapes[n]) for n in names]
        got = _run_exchange(_ex_rs_sibling(blocks), f"rs_sibling_{names[0]}")
        for n, b, l in zip(names, blocks, got):
            self.pairs[n] = _pair_add(b, l, self.core, BF16, f"rs_pair_add_{n}")

    def finish(self):
        rest = [n for n in BIG if n not in self.slots]
        got = _run_exchange(_ex_rs_chips([self.pairs[n] for n in rest]), "rs_chips_last")
        self.slots.update(zip(rest, got))
        return self.slots


def _train_step(a):
    c_idx = lax.axis_index("c")
    x, mem, target = a["x"][0], a["mem"][0], a["loss_target"][0]
    D = x.shape[1]
    shards = {n: a[n][0] for n in BIG}

    fox_scale = 1.0 / math.sqrt(DH)
    n_mine = shards["w_in"].shape[1]
    dev = 4 * lax.axis_index("x") + 2 * lax.axis_index("y") + c_idx
    cols = dev * n_mine + jnp.arange(n_mine)
    is_fox_q = (cols >= 4 * D) & (cols < 5 * D)
    sent = dict(shards, w_in=shards["w_in"] * jnp.where(is_fox_q, fox_scale, 1.0)[None, :])
    sent = {n: v.astype(BF16) for n, v in sent.items()}
    W = {}
    traffic = _Traffic(sent, W, {n: shards[n].shape for n in BIG}, c_idx.astype(jnp.int32).reshape(1), D)
    first = _run_exchange(_ex_ag_chips([sent[n] for n in GATHER_FIRST]), "ag_first_chips")
    traffic.install(GATHER_FIRST, _run_exchange(_ex_ag_sibling(first), "ag_first_sibling"))
    S = {n: a[n] for n in SMALL}

    sq, grad_x, gW, gS = _step_local(x, mem, target, W, S, traffic)
    slots = traffic.finish()
    big = {n: _adamw_reduce(slots[n], shards[n], a["m_" + n][0], a["v_" + n][0], f"adamw_{n}") for n in BIG}

    loss_row = jnp.pad(sq[:1, :1] * (0.5 / D), ((0, 0), (0, PACK_COLS - 1)))
    small_half = _run_exchange(_ex_ag_chips([_pack_small(gS, loss_row)]), "small_ag_chips")
    small_all = _run_exchange(_ex_ag_sibling(small_half), "small_ag_sibling")[0]
    small_slots = small_all.reshape(8, SMALL_ROWS, PACK_COLS)
    zero_row = jnp.zeros((1, PACK_COLS), F32)
    g_sm, d_sm, m_sm, v_sm = _adamw_reduce(
        small_slots, _pack_small({n: a[n] for n in SMALL}, zero_row),
        _pack_small({n: a["m_" + n] for n in SMALL}, zero_row),
        _pack_small({n: a["v_" + n] for n in SMALL}, zero_row), "adamw_small")

    def unpack(which, small):
        out = _unpack_small(small, {n: a[n] for n in SMALL})
        for n in BIG:
            out[n] = big[n][which][None]
        return [out[n] for n in WEIGHTS]

    loss = g_sm[SMALL_ROWS - 1, 0]
    return (loss, grad_x[None], *unpack(0, g_sm), *unpack(1, d_sm), *unpack(2, m_sm), *unpack(3, v_sm))


def kernel(x, mem, ffn1_pre_g, ffn1_w_in, ffn1_w_down, ffn1_post_g, mix_pre_g, w_in, hg_lb_logits, hg_norm_g, fox_f_bias, w_branch_a, w_branch_b, b_gate, w_out, mix_post_g, mem_pre_g, mem_kv_g, w_mq, w_mkv, w_mo, mem_post_g, ffn2_pre_g, ffn2_w_in, ffn2_w_down, ffn2_post_g, loss_target, m_ffn1_pre_g, m_ffn1_w_in, m_ffn1_w_down, m_ffn1_post_g, m_mix_pre_g, m_w_in, m_hg_lb_logits, m_hg_norm_g, m_fox_f_bias, m_w_branch_a, m_w_branch_b, m_b_gate, m_w_out, m_mix_post_g, m_mem_pre_g, m_mem_kv_g, m_w_mq, m_w_mkv, m_w_mo, m_mem_post_g, m_ffn2_pre_g, m_ffn2_w_in, m_ffn2_w_down, m_ffn2_post_g, v_ffn1_pre_g, v_ffn1_w_in, v_ffn1_w_down, v_ffn1_post_g, v_mix_pre_g, v_w_in, v_hg_lb_logits, v_hg_norm_g, v_fox_f_bias, v_w_branch_a, v_w_branch_b, v_b_gate, v_w_out, v_mix_post_g, v_mem_pre_g, v_mem_kv_g, v_w_mq, v_w_mkv, v_w_mo, v_mem_post_g, v_ffn2_pre_g, v_ffn2_w_in, v_ffn2_w_down, v_ffn2_post_g):
    return _train_step(dict(locals()))
```

```python
import functools
import math

import jax
import jax.numpy as jnp
from jax import lax
from jax.experimental import pallas as pl
from jax.experimental.pallas import tpu as pltpu

F32 = jnp.float32
BF16 = jnp.bfloat16
MESH = pl.DeviceIdType.MESH

EPS = 1e-6
HEADS = 8
DH = 128
MEM_HEADS = 4
CHUNK = 128
HALF = CHUNK // 2
SWIGLU_TILE = 256
LANES = 128
PACK_COLS = 1024
ROW_TILE = 512
SEQ_BLOCK = 512
ATTN_TILE = 2048
ATTN_ROWS = 256
EXP_CLAMP = 80.0
NEG_BIG = -1e30

ADAM_LR, ADAM_B1, ADAM_B2, ADAM_EPS, ADAM_WD, ADAM_STEP = 0.001, 0.9, 0.999, 1e-08, 0.01, 10

VMEM_LIMIT = 48 * 1024 * 1024

_DN = {
    "nn": (((1,), (0,)), ((), ())),
    "nt": (((1,), (1,)), ((), ())),
    "tn": (((0,), (0,)), ((), ())),
}

BIG = ["ffn1_w_in", "ffn1_w_down", "w_in", "w_branch_a", "w_branch_b", "w_out", "w_mq", "w_mkv", "w_mo",
       "ffn2_w_in", "ffn2_w_down"]
COL_SHARDED = {"ffn1_w_in", "w_in", "w_mkv", "ffn2_w_in"}
SMALL = ["ffn1_pre_g", "ffn1_post_g", "mix_pre_g", "hg_lb_logits", "hg_norm_g", "fox_f_bias", "b_gate",
         "mix_post_g", "mem_pre_g", "mem_kv_g", "mem_post_g", "ffn2_pre_g", "ffn2_post_g"]
WEIGHTS = ["ffn1_pre_g", "ffn1_w_in", "ffn1_w_down", "ffn1_post_g", "mix_pre_g", "w_in", "hg_lb_logits",
           "hg_norm_g", "fox_f_bias", "w_branch_a", "w_branch_b", "b_gate", "w_out", "mix_post_g", "mem_pre_g",
           "mem_kv_g", "w_mq", "w_mkv", "w_mo", "mem_post_g", "ffn2_pre_g", "ffn2_w_in", "ffn2_w_down",
           "ffn2_post_g"]


def _dot(a, b, mode="nn"):
    return lax.dot_general(a, b, _DN[mode], preferred_element_type=F32)


def _sig(x):
    return 1.0 / (1.0 + jnp.exp(-x))


def _sig_approx(x):
    return pl.reciprocal(1.0 + jnp.exp(-x), approx=True)


def _params(*dims):
    return pltpu.CompilerParams(dimension_semantics=dims if dims else None, vmem_limit_bytes=VMEM_LIMIT)


def _tile(dim, pref):
    if dim <= pref:
        return dim
    t = (pref // LANES) * LANES
    while t >= LANES:
        if dim % t == 0:
            return t
        t -= LANES
    raise ValueError(f"no tile for {dim}")


def _colsum(x):
    return jnp.sum(x, axis=0, keepdims=True)


def _rowsum(x):
    return jnp.sum(x, axis=1, keepdims=True)


def _iota(shape, axis):
    return lax.broadcasted_iota(jnp.int32, shape, axis)


def _pick_row(x, r):
    return _colsum(jnp.where(_iota(x.shape, 0) == r, x, 0.0))


def _tri_dot(tri, x):
    hi = x.astype(BF16)
    r1 = x - hi.astype(F32)
    mid = r1.astype(BF16)
    lo = (r1 - mid.astype(F32)).astype(BF16)
    return _dot(tri, hi) + _dot(tri, mid) + _dot(tri, lo)


_MM_TILES = {"nn": (2048, 512, 1024), "nt": (512, 1024, 4096), "tn": (1024, 1024, 2048)}


def _host_call(body, name, grid, in_specs, out_specs, out_shape, scratch_shapes, dims, args, hosted=None):
    if hosted is None:
        results = pl.pallas_call(body, name=name, grid=grid, in_specs=in_specs, out_specs=out_specs, out_shape=out_shape,
                                 scratch_shapes=scratch_shapes, compiler_params=_params(*dims))(*args)
        return list(results), []
    n_in, n_out, n_sc = len(in_specs), len(out_specs), len(scratch_shapes)
    h_in, h_out = len(hosted.inputs), len(hosted.out_shapes)

    def wrapped(*refs):
        cut = [n_in, h_in, n_out, h_out, n_sc]
        at = [sum(cut[:i]) for i in range(len(cut) + 1)]
        ins, hin, outs, hout, scr = (refs[at[i]:at[i + 1]] for i in range(len(cut)))
        hsems = refs[at[-1]:]
        ids = [pl.program_id(d) for d in range(len(grid))]
        first = functools.reduce(jnp.logical_and, [i == 0 for i in ids])
        last = functools.reduce(jnp.logical_and, [i == g - 1 for i, g in zip(ids, grid)])

        @pl.when(first)
        def _():
            hosted.start(hin, hout, hsems)

        body(*ins, *outs, *scr)

        @pl.when(last)
        def _():
            hosted.wait(hin, hout, hsems)

    results = pl.pallas_call(
        wrapped, name=name, grid=grid, in_specs=list(in_specs) + [_HBM] * h_in,
        out_specs=list(out_specs) + [_HBM] * h_out, out_shape=list(out_shape) + list(hosted.out_shapes),
        scratch_shapes=list(scratch_shapes) + list(hosted.scratch), input_output_aliases=hosted.aliases(n_in, n_out),
        compiler_params=_params(*dims))(*args, *hosted.inputs)
    return list(results[:n_out]), list(results[n_out:])


def _mm(a, b, mode, out_dtype, name, add=None, tm=None, tn=None, tk=None, hosted=None):
    tm, tn, tk = (given or pref for given, pref in zip((tm, tn, tk), _MM_TILES[mode]))
    if mode == "nn":
        (M, K), (K2, N) = a.shape, b.shape
    elif mode == "nt":
        (M, K), (N, K2) = a.shape, b.shape
    else:
        (K, M), (K2, N) = a.shape, b.shape
    assert K == K2, (name, a.shape, b.shape)
    tm, tn, tk = _tile(M, tm), _tile(N, tn), _tile(K, tk)
    nk = K // tk
    if mode == "tn":
        a_spec = pl.BlockSpec((tk, tm), lambda i, j, k: (k, i))
    else:
        a_spec = pl.BlockSpec((tm, tk), lambda i, j, k: (i, k))
    if mode == "nt":
        b_spec = pl.BlockSpec((tn, tk), lambda i, j, k: (j, k))
    else:
        b_spec = pl.BlockSpec((tk, tn), lambda i, j, k: (k, j))
    o_spec = pl.BlockSpec((tm, tn), lambda i, j, k: (i, j))
    has_add = add is not None

    def body(*refs):
        a_ref, b_ref = refs[0], refs[1]
        c_ref = refs[2] if has_add else None
        o_ref = refs[3] if has_add else refs[2]
        part = _dot(a_ref[...], b_ref[...], mode)
        if nk == 1:
            if has_add:
                part = part + c_ref[...]
            o_ref[...] = part.astype(o_ref.dtype)
            return
        acc_ref = refs[-1]
        k = pl.program_id(2)

        @pl.when(k == 0)
        def _():
            acc_ref[...] = part + c_ref[...] if has_add else part

        @pl.when(k > 0)
        def _():
            acc_ref[...] += part

        @pl.when(k == nk - 1)
        def _():
            o_ref[...] = acc_ref[...].astype(o_ref.dtype)

    in_specs = [a_spec, b_spec] + ([o_spec] if has_add else [])
    args = (a, b) + ((add,) if has_add else ())
    (out,), landed = _host_call(
        body, name, (M // tm, N // tn, nk), in_specs, [o_spec], [jax.ShapeDtypeStruct((M, N), out_dtype)],
        [pltpu.VMEM((tm, tn), F32)] if nk > 1 else [], ("parallel", "parallel", "arbitrary"), args, hosted)
    return out if hosted is None else (out, landed)


def _rms_fwd(x, g, name, out_dtype=BF16):
    T, D = x.shape
    tr = _tile(T, ROW_TILE)

    def body(x_ref, g_ref, o_ref):
        xv = x_ref[...]
        r = lax.rsqrt(jnp.mean(xv * xv, axis=-1, keepdims=True) + EPS)
        o_ref[...] = (xv * r * g_ref[...]).astype(o_ref.dtype)

    return pl.pallas_call(
        body, name=name, grid=(T // tr,),
        in_specs=[pl.BlockSpec((tr, D), lambda i: (i, 0)), pl.BlockSpec((1, D), lambda i: (0, 0))],
        out_specs=pl.BlockSpec((tr, D), lambda i: (i, 0)),
        out_shape=jax.ShapeDtypeStruct((T, D), out_dtype), compiler_params=_params("parallel"),
    )(x, g)


def _resid_rms(x, z, g, scale, name):
    T, D = x.shape
    tr = _tile(T, ROW_TILE)

    def body(x_ref, z_ref, g_ref, o_ref):
        zv = z_ref[...]
        r = lax.rsqrt(jnp.mean(zv * zv, axis=-1, keepdims=True) + EPS)
        o_ref[...] = x_ref[...] + scale * (zv * r * g_ref[...])

    row = pl.BlockSpec((tr, D), lambda i: (i, 0))
    return pl.pallas_call(
        body, name=name, grid=(T // tr,), in_specs=[row, row, pl.BlockSpec((1, D), lambda i: (0, 0))],
        out_specs=row, out_shape=jax.ShapeDtypeStruct((T, D), F32), compiler_params=_params("parallel"),
    )(x, z, g)


def _final_loss(x, z, g, scale, target, name):
    T, D = x.shape
    tr = _tile(T, ROW_TILE)

    def body(x_ref, z_ref, g_ref, t_ref, dx_ref, acc_ref):
        @pl.when(pl.program_id(0) == 0)
        def _():
            acc_ref[...] = jnp.zeros_like(acc_ref)

        zv = z_ref[...]
        r = lax.rsqrt(jnp.mean(zv * zv, axis=-1, keepdims=True) + EPS)
        e = x_ref[...] + scale * (zv * r * g_ref[...]) - t_ref[...]
        dx_ref[...] = e * (1.0 / D)
        acc_ref[...] += _colsum(_rowsum(e * e))

    row = pl.BlockSpec((tr, D), lambda i: (i, 0))
    return pl.pallas_call(
        body, name=name, grid=(T // tr,), in_specs=[row, row, pl.BlockSpec((1, D), lambda i: (0, 0)), row],
        out_specs=[row, pl.BlockSpec((8, LANES), lambda i: (0, 0))],
        out_shape=[jax.ShapeDtypeStruct((T, D), F32), jax.ShapeDtypeStruct((8, LANES), F32)],
        compiler_params=_params("arbitrary"),
    )(x, z, g, target)


def _rms_bwd(xin, g, dy, scale, name, out_dtype, resid=None):
    T, D = xin.shape
    tr = _tile(T, ROW_TILE)
    has_resid = resid is not None

    def body(*refs):
        x_ref, g_ref, dy_ref = refs[:3]
        r_ref = refs[3] if has_resid else None
        dx_ref, dg_ref = refs[-2], refs[-1]

        @pl.when(pl.program_id(0) == 0)
        def _():
            dg_ref[...] = jnp.zeros_like(dg_ref)

        xv = x_ref[...]
        r = lax.rsqrt(jnp.mean(xv * xv, axis=-1, keepdims=True) + EPS)
        xh = xv * r
        dyv = dy_ref[...].astype(F32) * scale
        dxh = dyv * g_ref[...]
        dx = r * (dxh - xh * jnp.mean(dxh * xh, axis=-1, keepdims=True))
        if has_resid:
            dx = dx + r_ref[...]
        dx_ref[...] = dx.astype(dx_ref.dtype)
        dg_ref[...] += _colsum(dyv * xh)

    row = pl.BlockSpec((tr, D), lambda i: (i, 0))
    vec = pl.BlockSpec((1, D), lambda i: (0, 0))
    return pl.pallas_call(
        body, name=name, grid=(T // tr,), in_specs=[row, vec, row] + ([row] if has_resid else []),
        out_specs=[row, vec],
        out_shape=[jax.ShapeDtypeStruct((T, D), out_dtype), jax.ShapeDtypeStruct((1, D), F32)],
        compiler_params=_params("arbitrary"),
    )(*((xin, g, dy) + ((resid,) if has_resid else ())))


def _mm_swiglu(h, w_in, name, hosted=None):
    T, K = h.shape
    F2 = w_in.shape[1]
    tf = SWIGLU_TILE
    tm = _tile(T, _MM_TILES["nn"][0])

    def body(h_ref, w_ref, u_ref, a_ref):
        u = _dot(h_ref[...], w_ref[...])
        u_ref[...] = u.astype(u_ref.dtype)
        gate, up = u[:, :tf], u[:, tf:]
        a_ref[...] = (gate * _sig_approx(gate) * up).astype(a_ref.dtype)

    (u, a), landed = _host_call(
        body, name, (T // tm, F2 // (2 * tf)),
        [pl.BlockSpec((tm, K), lambda i, j: (i, 0)), pl.BlockSpec((K, 2 * tf), lambda i, j: (0, j))],
        [pl.BlockSpec((tm, 2 * tf), lambda i, j: (i, j)), pl.BlockSpec((tm, tf), lambda i, j: (i, j))],
        [jax.ShapeDtypeStruct((T, F2), BF16), jax.ShapeDtypeStruct((T, F2 // 2), BF16)], [],
        ("parallel", "parallel"), (h, w_in), hosted)
    return u, a, landed


def _mm_swiglu_bwd(dz, w_down, u, name):
    T, D = dz.shape
    F = w_down.shape[0]
    tf = SWIGLU_TILE
    tm = _tile(T, _MM_TILES["nn"][0])

    def body(dz_ref, w_ref, u_ref, o_ref):
        d = _dot(dz_ref[...], w_ref[...], "nt")
        gate = u_ref[:, :tf].astype(F32)
        up = u_ref[:, tf:].astype(F32)
        s = _sig_approx(gate)
        o_ref[:, :tf] = (d * up * (s * (1.0 + gate * (1.0 - s)))).astype(o_ref.dtype)
        o_ref[:, tf:] = (d * gate * s).astype(o_ref.dtype)

    return pl.pallas_call(
        body, name=name, grid=(T // tm, F // tf),
        in_specs=[pl.BlockSpec((tm, D), lambda i, j: (i, 0)), pl.BlockSpec((tf, D), lambda i, j: (j, 0)),
                  pl.BlockSpec((tm, 2 * tf), lambda i, j: (i, j))],
        out_specs=pl.BlockSpec((tm, 2 * tf), lambda i, j: (i, j)),
        out_shape=jax.ShapeDtypeStruct((T, 2 * F), BF16), compiler_params=_params("parallel", "parallel"),
    )(dz, w_down, u)


def _hgout_fwd(o_a, pm, g, name):
    T, D = o_a.shape
    tr = _tile(T, ROW_TILE)

    def body(o_ref, ga_ref, g_ref, out_ref):
        ov = o_ref[...]
        r = lax.rsqrt(jnp.mean(ov * ov, axis=-1, keepdims=True) + EPS)
        ga = ga_ref[...].astype(F32)
        out_ref[...] = (ov * r * g_ref[...] * (ga * _sig(ga))).astype(out_ref.dtype)

    row = pl.BlockSpec((tr, D), lambda i: (i, 0))
    return pl.pallas_call(
        body, name=name, grid=(T // tr,),
        in_specs=[row, pl.BlockSpec((tr, D), lambda i: (i, 3)), pl.BlockSpec((1, D), lambda i: (0, 0))],
        out_specs=row, out_shape=jax.ShapeDtypeStruct((T, D), BF16), compiler_params=_params("parallel"),
    )(o_a, pm, g)


def _hgout_bwd(o_a, pm, g, d_out, name):
    T, D = o_a.shape
    tr = _tile(T, ROW_TILE)

    def body(o_ref, ga_ref, g_ref, d_ref, do_ref, dga_ref, dg_ref):
        @pl.when(pl.program_id(0) == 0)
        def _():
            dg_ref[...] = jnp.zeros_like(dg_ref)

        ov = o_ref[...]
        r = lax.rsqrt(jnp.mean(ov * ov, axis=-1, keepdims=True) + EPS)
        oh = ov * r
        ga = ga_ref[...].astype(F32)
        s = _sig(ga)
        d = d_ref[...].astype(F32)
        dn = d * (ga * s)
        dga_ref[...] = (d * (oh * g_ref[...]) * (s * (1.0 + ga * (1.0 - s)))).astype(dga_ref.dtype)
        dxh = dn * g_ref[...]
        do_ref[...] = (r * (dxh - oh * jnp.mean(dxh * oh, axis=-1, keepdims=True))).astype(do_ref.dtype)
        dg_ref[...] += _colsum(dn * oh)

    row = pl.BlockSpec((tr, D), lambda i: (i, 0))
    vec = pl.BlockSpec((1, D), lambda i: (0, 0))
    return pl.pallas_call(
        body, name=name, grid=(T // tr,), in_specs=[row, pl.BlockSpec((tr, D), lambda i: (i, 3)), vec, row],
        out_specs=[row, row, vec],
        out_shape=[jax.ShapeDtypeStruct((T, D), BF16), jax.ShapeDtypeStruct((T, D), BF16),
                   jax.ShapeDtypeStruct((1, D), F32)],
        compiler_params=_params("arbitrary"),
    )(o_a, pm, g, d_out)


def _merge_fwd(ya, yb, pg, bg, name):
    T, D = ya.shape
    tr = _tile(T, 256)

    def body(ya_ref, yb_ref, pg_ref, bg_ref, o_ref):
        g0 = _sig(pg_ref[:, :D] + bg_ref[:, :D])
        g1 = _sig(pg_ref[:, D:] + bg_ref[:, D:])
        o_ref[...] = (g0 * ya_ref[...] + g1 * yb_ref[...]).astype(o_ref.dtype)

    row = pl.BlockSpec((tr, D), lambda i: (i, 0))
    return pl.pallas_call(
        body, name=name, grid=(T // tr,),
        in_specs=[row, row, pl.BlockSpec((tr, 2 * D), lambda i: (i, 0)), pl.BlockSpec((1, 2 * D), lambda i: (0, 0))],
        out_specs=row, out_shape=jax.ShapeDtypeStruct((T, D), BF16), compiler_params=_params("parallel"),
    )(ya, yb, pg, bg)


def _merge_bwd(dy, ya, yb, pg, bg, name):
    T, D = ya.shape
    tr = _tile(T, 256)

    def body(dy_ref, ya_ref, yb_ref, pg_ref, bg_ref, dya_ref, dyb_ref, dpg_ref, dbg_ref):
        @pl.when(pl.program_id(0) == 0)
        def _():
            dbg_ref[...] = jnp.zeros_like(dbg_ref)

        d = dy_ref[...].astype(F32)
        g0 = _sig(pg_ref[:, :D] + bg_ref[:, :D])
        g1 = _sig(pg_ref[:, D:] + bg_ref[:, D:])
        dya_ref[...] = (d * g0).astype(dya_ref.dtype)
        dyb_ref[...] = (d * g1).astype(dyb_ref.dtype)
        dg0 = d * ya_ref[...] * (g0 * (1.0 - g0))
        dg1 = d * yb_ref[...] * (g1 * (1.0 - g1))
        dpg_ref[:, :D] = dg0.astype(dpg_ref.dtype)
        dpg_ref[:, D:] = dg1.astype(dpg_ref.dtype)
        dbg_ref[:, :D] += _colsum(dg0)
        dbg_ref[:, D:] += _colsum(dg1)

    row = pl.BlockSpec((tr, D), lambda i: (i, 0))
    wide = pl.BlockSpec((tr, 2 * D), lambda i: (i, 0))
    wvec = pl.BlockSpec((1, 2 * D), lambda i: (0, 0))
    return pl.pallas_call(
        body, name=name, grid=(T // tr,), in_specs=[row, row, row, wide, wvec],
        out_specs=[row, row, wide, wvec],
        out_shape=[jax.ShapeDtypeStruct((T, D), BF16), jax.ShapeDtypeStruct((T, D), BF16),
                   jax.ShapeDtypeStruct((T, 2 * D), BF16), jax.ShapeDtypeStruct((1, 2 * D), F32)],
        compiler_params=_params("arbitrary"),
    )(dy, ya, yb, pg, bg)


def _hgrn_chunk_terms(q, fl, lb, tri):
    shape = q.shape
    row = _iota(shape, 0)
    sg = _sig(fl)
    f = lb + (1.0 - lb) * sg
    k = 1.0 - f
    b = _tri_dot(tri, jnp.log(f))
    ref1 = jnp.where(row < HALF, _pick_row(b, HALF // 2), _pick_row(b, HALF + HALF // 2))
    b_half = _pick_row(b, HALF - 1)
    b_last = _pick_row(b, CHUNK - 1)
    sq = _sig(q)
    qs = q * sq
    e_q1 = jnp.exp(jnp.minimum(b - ref1, EXP_CLAMP))
    e_k1 = jnp.exp(jnp.minimum(ref1 - b, EXP_CLAMP))
    e_q2 = jnp.exp(jnp.minimum(b - b_half, 0.0))
    e_k2 = jnp.exp(jnp.minimum(b_half - b, 0.0))
    e_b = jnp.exp(b)
    e_kd = jnp.exp(b_last - b)
    return dict(sg=sg, f=f, k=k, sq=sq, qs=qs, e_q1=e_q1, e_k1=e_k1, e_q2=e_q2, e_k2=e_k2, e_b=e_b, e_kd=e_kd,
                e_last=jnp.exp(b_last))


def _hgrn_masks():
    r = _iota((CHUNK, CHUNK), 0)
    c = _iota((CHUNK, CHUNK), 1)
    causal = r >= c
    same = (r < HALF) == (c < HALF)
    return causal, causal & same, (r >= HALF) & (c < HALF)


def _softmax_lb(lbl_ref):
    l0, l1 = lbl_ref[0, 0], lbl_ref[1, 0]
    mx = jnp.maximum(l0, l1)
    e0, e1 = jnp.exp(l0 - mx), jnp.exp(l1 - mx)
    return e0 / (e0 + e1)


def _hgrn_fwd(pm, lbl, name):
    T = pm.shape[0]
    tb = _tile(T, SEQ_BLOCK)
    nc = tb // CHUNK

    def body(q_ref, f_ref, i_ref, lbl_ref, o_ref, st_ref, s_sc):
        @pl.when(pl.program_id(1) == 0)
        def _():
            s_sc[...] = jnp.zeros_like(s_sc)

        lb = _softmax_lb(lbl_ref)
        causal, m1, m2 = _hgrn_masks()
        tri = jnp.where(causal, 1.0, 0.0).astype(BF16)
        parts = []
        for ci in range(nc):
            sl = pl.ds(ci * CHUNK, CHUNK)
            t = _hgrn_chunk_terms(q_ref[sl, :].astype(F32), f_ref[sl, :].astype(F32), lb, tri)
            iv = i_ref[sl, :]
            a1 = _dot((t["qs"] * t["e_q1"]).astype(BF16), (t["k"] * t["e_k1"]).astype(BF16), "nt")
            a2 = _dot((t["qs"] * t["e_q2"]).astype(BF16), (t["k"] * t["e_k2"]).astype(BF16), "nt")
            a = jnp.where(m1, a1, 0.0) + jnp.where(m2, a2, 0.0)
            parts.append((_dot(a.astype(BF16), iv), (t["qs"] * t["e_b"]).astype(BF16),
                          _dot(iv, (t["k"] * t["e_kd"]).astype(BF16), "tn"), t["e_last"]))
        st = s_sc[...]
        for ci, (o_intra, qi, grow, e_last) in enumerate(parts):
            st_ref[0, ci] = st
            o_ref[pl.ds(ci * CHUNK, CHUNK), :] = o_intra + _dot(qi, st.astype(BF16), "nt")
            st = e_last * st + grow
        s_sc[...] = st

    blk = lambda off: pl.BlockSpec((tb, DH), lambda h, b: (b, off + h))
    return pl.pallas_call(
        body, name=name, grid=(HEADS, T // tb),
        in_specs=[blk(0), blk(HEADS), blk(2 * HEADS), pl.BlockSpec((2, 1, 1, DH), lambda h, b: (0, h, 0, 0))],
        out_specs=[pl.BlockSpec((tb, DH), lambda h, b: (b, h)),
                   pl.BlockSpec((1, nc, DH, DH), lambda h, b: (h, b, 0, 0))],
        out_shape=[jax.ShapeDtypeStruct((T, HEADS * DH), F32),
                   jax.ShapeDtypeStruct((HEADS, T // CHUNK, DH, DH), F32)],
        scratch_shapes=[pltpu.VMEM((DH, DH), F32)],
        compiler_params=_params("parallel", "arbitrary"),
    )(pm, pm, pm, lbl)


def _hgrn_bwd(pm, lbl, states, do, name):
    T = pm.shape[0]
    tb = _tile(T, SEQ_BLOCK)
    nc = tb // CHUNK
    nb = T // tb

    def body(q_ref, f_ref, i_ref, lbl_ref, st_ref, do_ref, dq_ref, df_ref, di_ref, dl_ref, ds_sc, dlb_sc):
        @pl.when(pl.program_id(1) == 0)
        def _():
            ds_sc[...] = jnp.zeros_like(ds_sc)
            dlb_sc[...] = jnp.zeros_like(dlb_sc)

        lb = _softmax_lb(lbl_ref)
        causal, m1, m2 = _hgrn_masks()
        tri = jnp.where(causal, 1.0, 0.0).astype(BF16)
        tri_rev = jnp.where(_iota((CHUNK, CHUNK), 0) <= _iota((CHUNK, CHUNK), 1), 1.0, 0.0).astype(BF16)
        last_row = _iota((CHUNK, DH), 0) == CHUNK - 1
        for ci in reversed(range(nc)):
            sl = pl.ds(ci * CHUNK, CHUNK)
            q = q_ref[sl, :].astype(F32)
            t = _hgrn_chunk_terms(q, f_ref[sl, :].astype(F32), lb, tri)
            iv = i_ref[sl, :]
            dov = do_ref[sl, :]
            qe1, ke1 = t["qs"] * t["e_q1"], t["k"] * t["e_k1"]
            qe2, ke2 = t["qs"] * t["e_q2"], t["k"] * t["e_k2"]
            qi, kd = t["qs"] * t["e_b"], t["k"] * t["e_kd"]
            qe1b, ke1b, qe2b, ke2b = qe1.astype(BF16), ke1.astype(BF16), qe2.astype(BF16), ke2.astype(BF16)
            a = jnp.where(m1, _dot(qe1b, ke1b, "nt"), 0.0) + jnp.where(m2, _dot(qe2b, ke2b, "nt"), 0.0)
            st = st_ref[0, ci]
            dsn = ds_sc[...]
            dsnb = dsn.astype(BF16)
            da = _dot(dov, iv, "nt")
            da1 = jnp.where(m1, da, 0.0).astype(BF16)
            da2 = jnp.where(m2, da, 0.0).astype(BF16)
            di_ref[sl, :] = (_dot(a.astype(BF16), dov, "tn") + _dot(kd.astype(BF16), dsnb, "nt")).astype(di_ref.dtype)
            dqe1, dke1 = _dot(da1, ke1b), _dot(da1, qe1b, "tn")
            dqe2, dke2 = _dot(da2, ke2b), _dot(da2, qe2b, "tn")
            dqi = _dot(dov, st.astype(BF16))
            dkd = _dot(iv, dsnb)
            ds_sc[...] = t["e_last"] * dsn + _dot(dov, qi.astype(BF16), "tn")
            dqs = dqe1 * t["e_q1"] + dqe2 * t["e_q2"] + dqi * t["e_b"]
            dk = dke1 * t["e_k1"] + dke2 * t["e_k2"] + dkd * t["e_kd"]
            qib, kdb = qi.astype(BF16).astype(F32), kd.astype(BF16).astype(F32)
            db = (dqe1 * qe1b.astype(F32) - dke1 * ke1b.astype(F32) + dqe2 * qe2b.astype(F32)
                  - dke2 * ke2b.astype(F32) + dqi * qib - dkd * kdb)
            extra = _colsum(dkd * kdb) + t["e_last"] * _colsum(dsn * st)
            db = db + jnp.where(last_row, extra, 0.0)
            dlf = _tri_dot(tri_rev, db)
            dfv = dlf / t["f"] - dk
            sg = t["sg"]
            df_ref[sl, :] = (dfv * (1.0 - lb) * sg * (1.0 - sg)).astype(df_ref.dtype)
            dlb_sc[...] += _colsum(dfv * (1.0 - sg))
            sq = t["sq"]
            dq_ref[sl, :] = (dqs * (sq * (1.0 + q * (1.0 - sq)))).astype(dq_ref.dtype)

        @pl.when(pl.program_id(1) == nb - 1)
        def _():
            dl0 = dlb_sc[...] * lb * (1.0 - lb)
            dl_ref[0, 0] = dl0
            dl_ref[1, 0] = -dl0

    blk = lambda off: pl.BlockSpec((tb, DH), lambda h, b: (nb - 1 - b, off + h))
    lspec = pl.BlockSpec((2, 1, 1, DH), lambda h, b: (0, h, 0, 0))
    out_blk = pl.BlockSpec((tb, DH), lambda h, b: (nb - 1 - b, h))
    D = HEADS * DH
    return pl.pallas_call(
        body, name=name, grid=(HEADS, nb),
        in_specs=[blk(0), blk(HEADS), blk(2 * HEADS), lspec,
                  pl.BlockSpec((1, nc, DH, DH), lambda h, b: (h, nb - 1 - b, 0, 0)), out_blk],
        out_specs=[out_blk, out_blk, out_blk, lspec],
        out_shape=[jax.ShapeDtypeStruct((T, D), BF16)] * 3 + [jax.ShapeDtypeStruct((2, HEADS, 1, DH), F32)],
        scratch_shapes=[pltpu.VMEM((DH, DH), F32), pltpu.VMEM((1, DH), F32)],
        compiler_params=_params("parallel", "arbitrary"),
    )(pm, pm, pm, lbl, states, do)


def _log_sigmoid(x):
    return jnp.minimum(x, 0.0) - jnp.log(1.0 + jnp.exp(-jnp.abs(x)))


def _fox_cumsum(pf, bias, name):
    T = pf.shape[0]
    tb = _tile(T, SEQ_BLOCK)

    def body(x_ref, b_ref, c_ref, carry):
        @pl.when(pl.program_id(0) == 0)
        def _():
            carry[...] = jnp.zeros_like(carry)

        tri = jnp.where(_iota((tb, tb), 0) >= _iota((tb, tb), 1), 1.0, 0.0).astype(BF16)
        c = _tri_dot(tri, _log_sigmoid(x_ref[...] + b_ref[...])) + carry[...]
        c_ref[...] = c
        carry[...] = _pick_row(c, tb - 1)

    row = pl.BlockSpec((tb, LANES), lambda i: (i, 0))
    return pl.pallas_call(
        body, name=name, grid=(T // tb,), in_specs=[row, pl.BlockSpec((1, LANES), lambda i: (0, 0))],
        out_specs=row, out_shape=jax.ShapeDtypeStruct((T, LANES), F32),
        scratch_shapes=[pltpu.VMEM((1, LANES), F32)], compiler_params=_params("arbitrary"),
    )(pf, bias)


def _fox_dcum(dc, pf, bias, name):
    T = pf.shape[0]
    tb = _tile(T, SEQ_BLOCK)
    nb = T // tb

    def body(dc_ref, x_ref, b_ref, dx_ref, db_ref, carry):
        @pl.when(pl.program_id(0) == 0)
        def _():
            carry[...] = jnp.zeros_like(carry)
            db_ref[...] = jnp.zeros_like(db_ref)

        tri_rev = jnp.where(_iota((tb, tb), 0) <= _iota((tb, tb), 1), 1.0, 0.0).astype(BF16)
        dls = _tri_dot(tri_rev, dc_ref[...]) + carry[...]
        carry[...] = _pick_row(dls, 0)
        dx = dls * (1.0 - _sig(x_ref[...] + b_ref[...]))
        dx_ref[...] = dx
        db_ref[...] += _colsum(dx)

    row = pl.BlockSpec((tb, LANES), lambda i: (nb - 1 - i, 0))
    vec = pl.BlockSpec((1, LANES), lambda i: (0, 0))
    return pl.pallas_call(
        body, name=name, grid=(nb,), in_specs=[row, row, vec], out_specs=[row, vec],
        out_shape=[jax.ShapeDtypeStruct((T, LANES), F32), jax.ShapeDtypeStruct((1, LANES), F32)],
        scratch_shapes=[pltpu.VMEM((1, LANES), F32)], compiler_params=_params("arbitrary"),
    )(dc, pf, bias)


_Q_OFF, _K_OFF, _V_OFF = 4 * HEADS, 5 * HEADS, 6 * HEADS


def _causal_pairs(nq, by_key):
    if by_key:
        pairs = [(i, j) for j in range(nq) for i in range(j, nq)]
    else:
        pairs = [(i, j) for i in range(nq) for j in range(i + 1)]
    return jnp.asarray([p[0] for p in pairs], jnp.int32), jnp.asarray([p[1] for p in pairs], jnp.int32)


def _fox_logits(q, k, ck, row0, masked):
    s = _dot(q, k, "nt") - ck
    if masked:
        s = jnp.where(_iota(s.shape, 0) + row0 >= _iota(s.shape, 1), s, NEG_BIG)
    return s


def _ones_column(rows):
    return jnp.where(_iota((rows, DH), 1) == 0, 1.0, 0.0).astype(BF16)


def _fox_fwd(pm, c_col, c_row, name):
    T = pm.shape[0]
    tq = _tile(T, ATTN_TILE)
    nq = T // tq
    rg = min(ATTN_ROWS, tq)
    qi_tab, kj_tab = _causal_pairs(nq, by_key=False)

    def body(qi_ref, kj_ref, q_ref, k_ref, v_ref, cq_ref, ck_ref, o_ref, lse_ref, m_sc, acc_sc):
        t = pl.program_id(1)
        i, j = qi_ref[t], kj_ref[t]

        @pl.when(j == 0)
        def _():
            m_sc[...] = jnp.full_like(m_sc, NEG_BIG)
            acc_sc[...] = jnp.zeros_like(acc_sc)

        def step(diag):
            m_all, acc_all = m_sc[...], acc_sc[...]
            ones = _ones_column(tq)
            ms, accs = [], []
            for r in range(tq // rg):
                rows = slice(r * rg, (r + 1) * rg)
                w = (r + 1) * rg if diag else tq
                cq = cq_ref[0, rows, :]
                s = _fox_logits(q_ref[rows, :], k_ref[:w, :], ck_ref[0, :, :w], r * rg, diag)
                m_old = m_all[rows, :]
                m_new = jnp.maximum(m_old, jnp.max(s, axis=1, keepdims=True) + cq)
                alpha = jnp.exp(m_old - m_new)
                p = jnp.exp(s - (m_new - cq)).astype(BF16)
                v_one = jnp.concatenate([v_ref[:w, :], ones[:w, :]], axis=1)
                ms.append(m_new)
                accs.append(alpha * acc_all[rows, :] + _dot(p, v_one))
            m_sc[...] = jnp.concatenate(ms, axis=0)
            acc_sc[...] = jnp.concatenate(accs, axis=0)

        @pl.when(j < i)
        def _():
            step(False)

        @pl.when(j == i)
        def _():
            step(True)
            acc = acc_sc[...]
            denom = acc[:, DH:DH + 1]
            o_ref[...] = (acc[:, :DH] / denom).astype(o_ref.dtype)
            lse_ref[0] = m_sc[...] + jnp.log(denom)

    kv = lambda off: pl.BlockSpec((tq, DH), lambda h, t, qi, kj: (kj[t], off + h))
    col = pl.BlockSpec((1, tq, 1), lambda h, t, qi, kj: (h, qi[t], 0))
    grid_spec = pltpu.PrefetchScalarGridSpec(
        num_scalar_prefetch=2, grid=(HEADS, qi_tab.shape[0]),
        in_specs=[pl.BlockSpec((tq, DH), lambda h, t, qi, kj: (qi[t], _Q_OFF + h)), kv(_K_OFF), kv(_V_OFF), col,
                  pl.BlockSpec((1, 1, tq), lambda h, t, qi, kj: (h, 0, kj[t]))],
        out_specs=[pl.BlockSpec((tq, DH), lambda h, t, qi, kj: (qi[t], h)), col],
        scratch_shapes=[pltpu.VMEM((tq, 1), F32), pltpu.VMEM((tq, 2 * DH), F32)])
    return pl.pallas_call(
        body, name=name, grid_spec=grid_spec,
        out_shape=[jax.ShapeDtypeStruct((T, HEADS * DH), BF16), jax.ShapeDtypeStruct((HEADS, T, 1), F32)],
        compiler_params=_params("parallel", "arbitrary"),
    )(qi_tab, kj_tab, pm, pm, pm, c_col, c_row)


def _fox_delta(do, o, name):
    T, D = o.shape
    tr = _tile(T, ROW_TILE)

    def body(do_ref, o_ref, d_ref):
        prod = do_ref[...].astype(F32) * o_ref[...].astype(F32)
        for h in range(HEADS):
            d_ref[h] = _rowsum(prod[:, h * DH:(h + 1) * DH])

    row = pl.BlockSpec((tr, D), lambda i: (i, 0))
    return pl.pallas_call(
        body, name=name, grid=(T // tr,), in_specs=[row, row],
        out_specs=pl.BlockSpec((HEADS, tr, 1), lambda i: (0, i, 0)),
        out_shape=jax.ShapeDtypeStruct((HEADS, T, 1), F32), compiler_params=_params("parallel"),
    )(do, o)


def _fox_bwd(pm, c_col, c_row, do, lse, delta, name):
    T = pm.shape[0]
    tq = _tile(T, ATTN_TILE)
    nq = T // tq
    rg = min(ATTN_ROWS, tq)
    qi_tab, kj_tab = _causal_pairs(nq, by_key=True)
    npairs = qi_tab.shape[0]

    def body(qi_ref, kj_ref, q_ref, k_ref, v_ref, cq_ref, ck_ref, do_ref, lse_ref, dl_ref,
             dq_ref, dk_ref, dv_ref, rsum_ref, csum_ref, dq_sc, dk_sc, dv_sc):
        t = pl.program_id(1)
        i, j = qi_ref[t], kj_ref[t]

        @pl.when(t == 0)
        def _():
            dq_sc[...] = jnp.zeros_like(dq_sc)

        @pl.when(i == j)
        def _():
            dk_sc[...] = jnp.zeros_like(dk_sc)
            dv_sc[...] = jnp.zeros_like(dv_sc)

        base = pl.multiple_of(i * tq, tq)

        def step(diag):
            ones = _ones_column(tq)
            for r in range(tq // rg):
                rows = slice(r * rg, (r + 1) * rg)
                w = (r + 1) * rg if diag else tq
                qr, dor = q_ref[rows, :], do_ref[rows, :]
                s = _fox_logits(qr, k_ref[:w, :], ck_ref[0, :, :w], r * rg, diag)
                p = jnp.exp(s - (lse_ref[0, rows, :] - cq_ref[0, rows, :]))
                dp = _dot(dor, v_ref[:w, :], "nt")
                dsb = (p * (dp - dl_ref[0, rows, :])).astype(BF16)
                dv_sc[:w, :] += _dot(p.astype(BF16), dor, "tn")
                dk_sc[:w, :] += _dot(dsb, jnp.concatenate([qr, ones[rows, :]], axis=1), "tn")
                dq_sc[pl.ds(base + r * rg, rg), :] += _dot(dsb, jnp.concatenate([k_ref[:w, :], ones[:w, :]], axis=1))

        @pl.when(i > j)
        def _():
            step(False)

        @pl.when(i == j)
        def _():
            step(True)

        @pl.when(i == nq - 1)
        def _():
            dk_ref[...] = dk_sc[:, :DH].astype(dk_ref.dtype)
            dv_ref[...] = dv_sc[...].astype(dv_ref.dtype)
            csum_ref[0] = dk_sc[:, DH:DH + 1]

        @pl.when(t == npairs - 1)
        def _():
            dq_ref[...] = dq_sc[:, :DH].astype(dq_ref.dtype)
            rsum_ref[0] = dq_sc[:, DH:DH + 1]

    col = pl.BlockSpec((1, tq, 1), lambda h, t, qi, kj: (h, qi[t], 0))
    kv = lambda off: pl.BlockSpec((tq, DH), lambda h, t, qi, kj: (kj[t], off + h))
    kv_out = pl.BlockSpec((tq, DH), lambda h, t, qi, kj: (kj[t], h))
    grid_spec = pltpu.PrefetchScalarGridSpec(
        num_scalar_prefetch=2, grid=(HEADS, npairs),
        in_specs=[pl.BlockSpec((tq, DH), lambda h, t, qi, kj: (qi[t], _Q_OFF + h)), kv(_K_OFF), kv(_V_OFF), col,
                  pl.BlockSpec((1, 1, tq), lambda h, t, qi, kj: (h, 0, kj[t])),
                  pl.BlockSpec((tq, DH), lambda h, t, qi, kj: (qi[t], h)), col, col],
        out_specs=[pl.BlockSpec((T, DH), lambda h, t, qi, kj: (0, h)), kv_out, kv_out,
                   pl.BlockSpec((1, T, 1), lambda h, t, qi, kj: (h, 0, 0)),
                   pl.BlockSpec((1, tq, 1), lambda h, t, qi, kj: (h, kj[t], 0))],
        scratch_shapes=[pltpu.VMEM((T, 2 * DH), F32), pltpu.VMEM((tq, 2 * DH), F32), pltpu.VMEM((tq, DH), F32)])
    D = HEADS * DH
    return pl.pallas_call(
        body, name=name, grid_spec=grid_spec,
        out_shape=[jax.ShapeDtypeStruct((T, D), BF16)] * 3 + [jax.ShapeDtypeStruct((HEADS, T, 1), F32)] * 2,
        compiler_params=_params("parallel", "arbitrary"),
    )(qi_tab, kj_tab, pm, pm, pm, c_col, c_row, do, lse, delta)


def _xattn_fwd(q, kv, name):
    T, D = q.shape
    M = kv.shape[0]
    dh = D // MEM_HEADS
    tq = _tile(T, ATTN_TILE)
    scale = 1.0 / math.sqrt(dh)

    def body(q_ref, kv_ref, o_ref):
        for h in range(MEM_HEADS):
            cs = slice(h * dh, (h + 1) * dh)
            s = _dot(q_ref[:, cs], kv_ref[:, cs], "nt") * scale
            p = jnp.exp(s - jnp.max(s, axis=1, keepdims=True))
            p = p / _rowsum(p)
            o_ref[:, cs] = _dot(p.astype(BF16), kv_ref[:, D + h * dh:D + (h + 1) * dh]).astype(o_ref.dtype)

    row = pl.BlockSpec((tq, D), lambda i: (i, 0))
    return pl.pallas_call(
        body, name=name, grid=(T // tq,), in_specs=[row, pl.BlockSpec((M, 2 * D), lambda i: (0, 0))],
        out_specs=row, out_shape=jax.ShapeDtypeStruct((T, D), BF16), compiler_params=_params("parallel"),
    )(q, kv)


def _xattn_bwd(q, kv, do, name):
    T, D = q.shape
    M = kv.shape[0]
    dh = D // MEM_HEADS
    tq = _tile(T, ATTN_TILE)
    scale = 1.0 / math.sqrt(dh)

    def body(q_ref, kv_ref, do_ref, dq_ref, dkv_ref):
        @pl.when(pl.program_id(0) == 0)
        def _():
            dkv_ref[...] = jnp.zeros_like(dkv_ref)

        for h in range(MEM_HEADS):
            cs = slice(h * dh, (h + 1) * dh)
            vs = slice(D + h * dh, D + (h + 1) * dh)
            s = _dot(q_ref[:, cs], kv_ref[:, cs], "nt") * scale
            p = jnp.exp(s - jnp.max(s, axis=1, keepdims=True))
            p = p / _rowsum(p)
            dp = _dot(do_ref[:, cs], kv_ref[:, vs], "nt")
            ds = (p * (dp - _rowsum(p * dp)) * scale).astype(BF16)
            dq_ref[:, cs] = _dot(ds, kv_ref[:, cs]).astype(dq_ref.dtype)
            dkv_ref[:, cs] += _dot(ds, q_ref[:, cs], "tn")
            dkv_ref[:, vs] += _dot(p.astype(BF16), do_ref[:, cs], "tn")

    row = pl.BlockSpec((tq, D), lambda i: (i, 0))
    full = pl.BlockSpec((M, 2 * D), lambda i: (0, 0))
    return pl.pallas_call(
        body, name=name, grid=(T // tq,), in_specs=[row, full, row], out_specs=[row, full],
        out_shape=[jax.ShapeDtypeStruct((T, D), BF16), jax.ShapeDtypeStruct((M, 2 * D), F32)],
        compiler_params=_params("arbitrary"),
    )(q, kv, do)


_HBM = pl.BlockSpec(memory_space=pltpu.HBM)


def _position():
    return lax.axis_index("x"), lax.axis_index("y"), lax.axis_index("c")


def _other_chips(x, y):
    return [(1 - x, y), (x, 1 - y), (1 - x, 1 - y)]


class _Exchange:
    def __init__(self, inputs, out_shapes, scratch, copies, inplace=False):
        self.inputs, self.out_shapes, self.scratch, self.copies, self.inplace = inputs, out_shapes, scratch, copies, inplace

    def start(self, in_refs, out_refs, sems):
        for cp in self.copies(in_refs, out_refs, sems, False)[0]:
            cp.start()

    def wait(self, in_refs, out_refs, sems):
        for cp, how in self.copies(in_refs, out_refs, sems, True)[1]:
            getattr(cp, how)()

    def aliases(self, first_input, first_output):
        return {first_input + w: first_output + w for w in range(len(self.inputs))} if self.inplace else {}


def _run_exchange(ex, name):
    n_in, n_out = len(ex.inputs), len(ex.out_shapes)

    def body(*refs):
        parts = refs[:n_in], refs[n_in:n_in + n_out], refs[n_in + n_out:]
        ex.start(*parts)
        ex.wait(*parts)

    return pl.pallas_call(
        body, name=name, in_specs=[_HBM] * n_in, out_specs=[_HBM] * n_out, out_shape=ex.out_shapes,
        input_output_aliases=ex.aliases(0, 0), scratch_shapes=ex.scratch,
    )(*ex.inputs)


def _chip_exchange(arrays, out_shapes, src_of, dst_of):
    n = len(arrays)

    def copies(srcs, outs, sems, waiting):
        send, recv, local = sems
        x, y, c = _position()
        q = 2 * x + y
        kept, sent, arriving = [], [], []
        for w, (s_ref, o_ref) in enumerate(zip(srcs, outs)):
            kept.append(pltpu.make_async_copy(src_of(s_ref, q, c), dst_of(o_ref, q, c), local.at[w]))
            for j, (px, py) in enumerate(_other_chips(x, y)):
                sems_j = dict(send_sem=send.at[3 * w + j], recv_sem=recv.at[3 * w + j], device_id=(px, py, c),
                              device_id_type=MESH)
                sent.append(pltpu.make_async_remote_copy(src_ref=src_of(s_ref, 2 * px + py, c),
                                                         dst_ref=dst_of(o_ref, q, c), **sems_j))
                if waiting:
                    arriving.append(pltpu.make_async_remote_copy(src_ref=src_of(s_ref, q, c),
                                                                 dst_ref=dst_of(o_ref, 2 * px + py, c), **sems_j))
        return kept + sent, ([(cp, "wait_recv") for cp in arriving] + [(cp, "wait_send") for cp in sent]
                             + [(cp, "wait") for cp in kept])

    scratch = [pltpu.SemaphoreType.DMA((3 * n,)), pltpu.SemaphoreType.DMA((3 * n,)), pltpu.SemaphoreType.DMA((n,))]
    return _Exchange(arrays, out_shapes, scratch, copies)


def _ex_ag_chips(blks):
    return _chip_exchange(blks, [jax.ShapeDtypeStruct((4, 2) + b.shape, b.dtype) for b in blks],
                          src_of=lambda r, chip, c: r, dst_of=lambda r, chip, c: r.at[chip, c])


def _ex_rs_chips(parts):
    return _chip_exchange(parts, [jax.ShapeDtypeStruct(h.shape, h.dtype) for h in parts],
                          src_of=lambda r, chip, c: r.at[chip], dst_of=lambda r, chip, c: r.at[chip])


def _ex_ag_sibling(arrs):
    n = len(arrs)

    def copies(ins, outs, sems, waiting):
        send, recv = sems
        x, y, c = _position()
        to = dict(device_id=(x, y, 1 - c), device_id_type=MESH)
        mine = [pltpu.make_async_remote_copy(src_ref=a.at[:, c], dst_ref=a.at[:, c], send_sem=send.at[w],
                                             recv_sem=recv.at[w], **to) for w, a in enumerate(outs)]
        theirs = [pltpu.make_async_remote_copy(src_ref=a.at[:, c], dst_ref=a.at[:, 1 - c], send_sem=send.at[w],
                                               recv_sem=recv.at[w], **to) for w, a in enumerate(outs if waiting else [])]
        return mine, [(cp, "wait_recv") for cp in theirs] + [(cp, "wait_send") for cp in mine]

    return _Exchange(arrs, [jax.ShapeDtypeStruct(a.shape, a.dtype) for a in arrs],
                     [pltpu.SemaphoreType.DMA((n,)), pltpu.SemaphoreType.DMA((n,))], copies, inplace=True)


def _ex_rs_sibling(blocks):
    n = len(blocks)

    def copies(srcs, outs, sems, waiting):
        send, recv = sems
        x, y, c = _position()
        cps = [pltpu.make_async_remote_copy(src_ref=b.at[:, 1 - c], dst_ref=l, send_sem=send.at[w], recv_sem=recv.at[w],
                                            device_id=(x, y, 1 - c), device_id_type=MESH)
               for w, (b, l) in enumerate(zip(srcs, outs))]
        return cps, [(cp, "wait") for cp in cps]

    return _Exchange(blocks, [jax.ShapeDtypeStruct((4,) + b.shape[2:], b.dtype) for b in blocks],
                     [pltpu.SemaphoreType.DMA((n,)), pltpu.SemaphoreType.DMA((n,))], copies)


def _row_tile(rows, pref=256):
    for t in range(min(pref, rows) // 16 * 16, 0, -16):
        if rows % t == 0:
            return t
    raise ValueError(f"no row tile for {rows}")


def _pair_add(blocks, landed, core, out_dtype, name):
    n, _, s0, s1 = blocks.shape
    tr = _row_tile(s0)

    def body(core_ref, a_ref, b_ref, o_ref):
        del core_ref
        o_ref[...] = (a_ref[...] + b_ref[...]).astype(o_ref.dtype)

    grid_spec = pltpu.PrefetchScalarGridSpec(
        num_scalar_prefetch=1, grid=(n, s0 // tr),
        in_specs=[pl.BlockSpec((1, None, tr, s1), lambda p, i, core: (p, core[0], i, 0)),
                  pl.BlockSpec((1, tr, s1), lambda p, i, core: (p, i, 0))],
        out_specs=pl.BlockSpec((1, tr, s1), lambda p, i, core: (p, i, 0)))
    return pl.pallas_call(
        body, name=name, grid_spec=grid_spec, out_shape=jax.ShapeDtypeStruct(landed.shape, out_dtype),
        compiler_params=_params("parallel", "parallel"),
    )(core, blocks, landed)


def _adamw_math(w, g, m, v):
    m = ADAM_B1 * m + (1.0 - ADAM_B1) * g
    v = ADAM_B2 * v + (1.0 - ADAM_B2) * (g * g)
    m_hat = m / (1.0 - ADAM_B1 ** ADAM_STEP)
    v_hat = v / (1.0 - ADAM_B2 ** ADAM_STEP)
    delta = -ADAM_LR * (m_hat / (jnp.sqrt(v_hat) + ADAM_EPS) + ADAM_WD * w)
    return delta, m, v


def _adamw_reduce(slots, w, m, v, name):
    n, R, C = slots.shape
    tr = _row_tile(R)

    def body(s_ref, w_ref, m_ref, v_ref, g_ref, d_ref, nm_ref, nv_ref):
        g = s_ref[0].astype(F32)
        for p in range(1, n):
            g = g + s_ref[p].astype(F32)
        g_ref[...] = g
        d_ref[...], nm_ref[...], nv_ref[...] = _adamw_math(w_ref[...], g, m_ref[...], v_ref[...])

    row = pl.BlockSpec((tr, C), lambda i: (i, 0))
    return pl.pallas_call(
        body, name=name, grid=(R // tr,), in_specs=[pl.BlockSpec((n, tr, C), lambda i: (0, i, 0)), row, row, row],
        out_specs=[row] * 4, out_shape=[jax.ShapeDtypeStruct((R, C), F32)] * 4, compiler_params=_params("parallel"),
    )(slots, w, m, v)


def _full_from_gathered(a, n):
    s0, s1 = a.shape[2:]
    blk = a.reshape(8, s0, s1)
    if n in COL_SHARDED:
        return blk.transpose(1, 0, 2).reshape(s0, 8 * s1)
    return blk.reshape(8 * s0, s1)


def _blocks_from_full(g, n, shard_shape):
    s0, s1 = shard_shape
    if n in COL_SHARDED:
        blk = g.reshape(s0, 8, s1).transpose(1, 0, 2)
    else:
        blk = g.reshape(8, s0, s1)
    return blk.reshape(4, 2, s0, s1)


def _swiglu_interleave(w):
    d, f2 = w.shape
    return w.reshape(d, 2, f2 // (2 * SWIGLU_TILE), SWIGLU_TILE).transpose(0, 2, 1, 3).reshape(d, f2)


def _swiglu_deinterleave(w):
    d, f2 = w.shape
    return w.reshape(d, f2 // (2 * SWIGLU_TILE), 2, SWIGLU_TILE).transpose(0, 2, 1, 3).reshape(d, f2)


SMALL_ROWS = 16


def _pack_small(vals, loss_row):
    rows = []
    for n in SMALL:
        flat = vals[n].reshape(-1)
        pad = (-flat.shape[0]) % PACK_COLS
        rows.append(jnp.pad(flat, (0, pad)).reshape(-1, PACK_COLS))
    rows.append(loss_row)
    out = jnp.concatenate(rows, axis=0)
    assert out.shape[0] == SMALL_ROWS, out.shape
    return out


def _unpack_small(packed, like):
    out, r = {}, 0
    for n in SMALL:
        size = like[n].size
        rows = -(-size // PACK_COLS)
        out[n] = packed[r:r + rows].reshape(-1)[:size].reshape(like[n].shape)
        r += rows
    return out


class _NoTraffic:
    def host(self, stage):
        return None

    def landed(self, stage, arrays):
        pass

    def grads_ready(self, names, gW):
        pass


def _mm_behind(traffic, stage, *args, **kwargs):
    ex = traffic.host(stage)
    if ex is None:
        return _mm(*args, **kwargs)
    out, arrays = _mm(*args, hosted=ex, **kwargs)
    traffic.landed(stage, arrays)
    return out


def _ffn_fwd(x, g_pre, W, tag, traffic, up_stage=None, down_stage=None):
    h = _rms_fwd(x, g_pre, f"{tag}_pre")
    ex = traffic.host(up_stage) if up_stage else None
    u, a, arrays = _mm_swiglu(h, W[f"{tag}_w_in"], f"{tag}_up", hosted=ex)
    if ex is not None:
        traffic.landed(up_stage, arrays)
    z = _mm_behind(traffic, down_stage, a, W[f"{tag}_w_down"], "nn", F32, f"{tag}_down", tk=1408)
    return h, u, a, z


def _ffn_bwd(saved, x, g_pre, w_in, w_down, g_post, dx_out, tag, traffic, up_dx_stage=None):
    h, u, a, z = saved
    dz, dg_post = _rms_bwd(z, g_post, dx_out, 0.5, f"{tag}_post_bwd", BF16)
    dw_down = _mm(a, dz, "tn", F32, f"{tag}_down_dw", tm=1408)
    du = _mm_swiglu_bwd(dz, w_down, u, f"{tag}_down_dx")
    dh = _mm_behind(traffic, up_dx_stage, du, w_in, "nt", BF16, f"{tag}_up_dx", tk=5632)
    dw_in = _mm(h, du, "tn", F32, f"{tag}_up_dw")
    dx, dg_pre = _rms_bwd(x, g_pre, dh, 1.0, f"{tag}_pre_bwd", F32, resid=dx_out)
    return dx, dg_pre, dg_post, dw_in, dw_down


def _step_local(x, mem, target, W, S, traffic=_NoTraffic()):
    T, D = x.shape
    gW, gS = {}, {}

    f1 = _ffn_fwd(x, S["ffn1_pre_g"], W, "ffn1", traffic, "gather_mixer_chips", "gather_mixer_sibling")
    x1 = _resid_rms(x, f1[3], S["ffn1_post_g"], 0.5, "ffn1_post")

    h2 = _rms_fwd(x1, S["mix_pre_g"], "mix_pre")
    pm = _mm_behind(traffic, "gather_late_chips", h2, W["w_main"], "nn", BF16, "mix_proj_main")
    pf = _mm(h2, W["w_f"], "nn", F32, "mix_proj_f")
    pg = _mm_behind(traffic, "gather_late_sibling", h2, W["w_gates"], "nn", F32, "mix_proj_gates")
    lbl = S["hg_lb_logits"].reshape(2, HEADS, 1, DH)
    o_a, states = _hgrn_fwd(pm, lbl, "hgrn_fwd")
    oan = _hgout_fwd(o_a, pm, S["hg_norm_g"], "hgrn_out")
    bias = jnp.pad(S["fox_f_bias"], ((0, 0), (0, LANES - HEADS)))
    c = _fox_cumsum(pf, bias, "fox_cumsum")
    c_heads = c[:, :HEADS].T
    c_col, c_row = c_heads[:, :, None], c_heads[:, None, :]
    o_b, lse = _fox_fwd(pm, c_col, c_row, "fox_fwd")
    ya = _mm(oan, W["w_branch_a"], "nn", F32, "branch_a")
    yb = _mm(o_b, W["w_branch_b"], "nn", F32, "branch_b")
    y = _merge_fwd(ya, yb, pg, S["b_gate"], "merge")
    z2 = _mm(y, W["w_out"], "nn", F32, "mix_out")
    x2 = _resid_rms(x1, z2, S["mix_post_g"], 1.0, "mix_post")

    h3 = _rms_fwd(x2, S["mem_pre_g"], "mem_pre")
    memn = _rms_fwd(mem, S["mem_kv_g"], "mem_kv_norm")
    qm = _mm(h3, W["w_mq"], "nn", BF16, "mem_q")
    kv = _mm(memn, W["w_mkv"], "nn", BF16, "mem_kv")
    om = _xattn_fwd(qm, kv, "mem_attn")
    z3 = _mm(om, W["w_mo"], "nn", F32, "mem_o")
    x3 = _resid_rms(x2, z3, S["mem_post_g"], 1.0, "mem_post")

    f2 = _ffn_fwd(x3, S["ffn2_pre_g"], W, "ffn2", traffic)
    dx4, sq = _final_loss(x3, f2[3], S["ffn2_post_g"], 0.5, target, "loss")

    dx3, gS["ffn2_pre_g"], gS["ffn2_post_g"], gW["ffn2_w_in"], gW["ffn2_w_down"] = _ffn_bwd(
        f2, x3, S["ffn2_pre_g"], W["ffn2_w_in"], W["ffn2_w_down"], S["ffn2_post_g"], dx4, "ffn2", traffic)
    traffic.grads_ready(["ffn2_w_in", "ffn2_w_down"], gW)

    dz3, gS["mem_post_g"] = _rms_bwd(z3, S["mem_post_g"], dx3, 1.0, "mem_post_bwd", BF16)
    dom = _mm(dz3, W["w_mo"], "nt", BF16, "mem_o_dx")
    gW["w_mo"] = _mm(om, dz3, "tn", F32, "mem_o_dw")
    dqm, dkv = _xattn_bwd(qm, kv, dom, "mem_attn_bwd")
    dh3 = _mm(dqm, W["w_mq"], "nt", BF16, "mem_q_dx")
    gW["w_mq"] = _mm(h3, dqm, "tn", F32, "mem_q_dw")
    dkvb = dkv.astype(BF16)
    gW["w_mkv"] = _mm(memn, dkvb, "tn", F32, "mem_kv_dw")
    dmemn = _mm(dkvb, W["w_mkv"], "nt", F32, "mem_kv_dx")
    _, gS["mem_kv_g"] = _rms_bwd(mem, S["mem_kv_g"], dmemn, 1.0, "mem_kv_norm_bwd", BF16)
    dx2, gS["mem_pre_g"] = _rms_bwd(x2, S["mem_pre_g"], dh3, 1.0, "mem_pre_bwd", F32, resid=dx3)

    dz2, gS["mix_post_g"] = _rms_bwd(z2, S["mix_post_g"], dx2, 1.0, "mix_post_bwd", BF16)
    dy = _mm(dz2, W["w_out"], "nt", BF16, "mix_out_dx")
    gW["w_out"] = _mm(y, dz2, "tn", F32, "mix_out_dw")
    dya, dyb, dpg, gS["b_gate"] = _merge_bwd(dy, ya, yb, pg, S["b_gate"], "merge_bwd")
    doan = _mm(dya, W["w_branch_a"], "nt", BF16, "branch_a_dx")
    gW["w_branch_a"] = _mm(oan, dya, "tn", F32, "branch_a_dw")
    dob = _mm(dyb, W["w_branch_b"], "nt", BF16, "branch_b_dx")
    gW["w_branch_b"] = _mm(o_b, dyb, "tn", F32, "branch_b_dw")
    traffic.grads_ready(["w_mo", "w_mq", "w_mkv", "w_out", "w_branch_a", "w_branch_b"], gW)

    delta = _fox_delta(dob, o_b, "fox_delta")
    dq_b, dk_b, dv_b, ds_rows, ds_cols = _fox_bwd(pm, c_col, c_row, dob, lse, delta, "fox_bwd")
    dc = jnp.pad((ds_rows - ds_cols).reshape(HEADS, T).T, ((0, 0), (0, LANES - HEADS)))
    dpf, dbias = _fox_dcum(dc, pf, bias, "fox_cumsum_bwd")
    gS["fox_f_bias"] = dbias[:, :HEADS]

    do_a, dg_a, gS["hg_norm_g"] = _hgout_bwd(o_a, pm, S["hg_norm_g"], doan, "hgrn_out_bwd")
    dq_a, df_a, di_a, dlbl = _hgrn_bwd(pm, lbl, states, do_a, "hgrn_bwd")
    gS["hg_lb_logits"] = dlbl.reshape(2, HEADS, DH)

    dpm = jnp.concatenate([dq_a, df_a, di_a, dg_a, dq_b, dk_b, dv_b], axis=1)
    dpf16 = dpf.astype(BF16)
    dh2 = _mm_behind(traffic, "scatter_ffn2_chips", dpm, W["w_main"], "nt", F32, "mix_proj_main_dx")
    dh2 = _mm(dpg, W["w_gates"], "nt", F32, "mix_proj_gates_dx", add=dh2)
    dh2 = _mm(dpf16, W["w_f"], "nt", F32, "mix_proj_f_dx", add=dh2)
    gW["w_main"] = _mm_behind(traffic, "scatter_mid_chips", h2, dpm, "tn", F32, "mix_proj_main_dw")
    gW["w_gates"] = _mm(h2, dpg, "tn", F32, "mix_proj_gates_dw")
    gW["w_f"] = _mm(h2, dpf16, "tn", F32, "mix_proj_f_dw")
    traffic.grads_ready(["w_in"], gW)
    dx1, gS["mix_pre_g"] = _rms_bwd(x1, S["mix_pre_g"], dh2, 1.0, "mix_pre_bwd", F32, resid=dx2)

    dx0, gS["ffn1_pre_g"], gS["ffn1_post_g"], gW["ffn1_w_in"], gW["ffn1_w_down"] = _ffn_bwd(
        f1, x, S["ffn1_pre_g"], W["ffn1_w_in"], W["ffn1_w_down"], S["ffn1_post_g"], dx1, "ffn1", traffic,
        "scatter_w_in_chips")
    traffic.grads_ready(["ffn1_w_in", "ffn1_w_down"], gW)
    return sq, dx0, gW, gS


GATHER_FIRST = ["ffn1_w_in", "ffn1_w_down"]
GATHER_MIXER = ["w_in", "w_branch_a", "w_branch_b", "w_out"]
GATHER_LATE = ["w_mq", "w_mkv", "w_mo", "ffn2_w_in", "ffn2_w_down"]
SCATTER_BEHIND = {
    "scatter_ffn2_chips": ["ffn2_w_in", "ffn2_w_down"],
    "scatter_mid_chips": ["w_mo", "w_mq", "w_mkv", "w_out", "w_branch_a", "w_branch_b"],
    "scatter_w_in_chips": ["w_in"],
}


class _Traffic:
    def __init__(self, sent, W, shapes, core, D):
        self.sent, self.W, self.shapes, self.core, self.D = sent, W, shapes, core, D
        self.half, self.pairs, self.slots = {}, {}, {}

    def install(self, names, gathered):
        D = self.D
        for n, g in zip(names, gathered):
            full = _full_from_gathered(g, n)
            if n == "w_in":
                self.W["w_main"] = full[:, :7 * D]
                self.W["w_f"] = jnp.pad(full[:, 7 * D:7 * D + HEADS], ((0, 0), (0, LANES - HEADS)))
                self.W["w_gates"] = full[:, 7 * D + HEADS:]
            elif n in ("ffn1_w_in", "ffn2_w_in"):
                self.W[n] = _swiglu_interleave(full)
            else:
                self.W[n] = full

    def host(self, stage):
        if stage == "gather_mixer_chips":
            return _ex_ag_chips([self.sent[n] for n in GATHER_MIXER])
        if stage == "gather_late_chips":
            return _ex_ag_chips([self.sent[n] for n in GATHER_LATE])
        if stage in ("gather_mixer_sibling", "gather_late_sibling"):
            return _ex_ag_sibling(self.half[stage])
        if stage in SCATTER_BEHIND:
            return _ex_rs_chips([self.pairs[n] for n in SCATTER_BEHIND[stage]])
        return None

    def landed(self, stage, arrays):
        if stage == "gather_mixer_chips":
            self.half["gather_mixer_sibling"] = arrays
        elif stage == "gather_late_chips":
            self.half["gather_late_sibling"] = arrays
        elif stage == "gather_mixer_sibling":
            self.install(GATHER_MIXER, arrays)
        elif stage == "gather_late_sibling":
            self.install(GATHER_LATE, arrays)
        else:
            self.slots.update(zip(SCATTER_BEHIND[stage], arrays))

    def _final_grad(self, n, gW):
        D = self.D
        if n == "w_in":
            g = gW["w_main"]
            return jnp.concatenate([g[:, :4 * D], g[:, 4 * D:5 * D] * (1.0 / math.sqrt(DH)), g[:, 5 * D:],
                                    gW["w_f"][:, :HEADS], gW["w_gates"]], axis=1)
        if n in ("ffn1_w_in", "ffn2_w_in"):
            return _swiglu_deinterleave(gW[n])
        return gW[n]

    def grads_ready(self, names, gW):
        blocks = [_blocks_from_full(self._final_grad(n, gW), n, self.shapes[n]) for n in names]
        got = _run_exchange(_ex_rs_sibling(blocks), f"rs_sibling_{names[0]}")
        for n, b, l in zip(names, blocks, got):
            self.pairs[n] = _pair_add(b, l, self.core, BF16, f"rs_pair_add_{n}")

    def finish(self):
        rest = [n for n in BIG if n not in self.slots]
        got = _run_exchange(_ex_rs_chips([self.pairs[n] for n in rest]), "rs_chips_last")
        self.slots.update(zip(rest, got))
        return self.slots


def _train_step(a):
    c_idx = lax.axis_index("c")
    x, mem, target = a["x"][0], a["mem"][0], a["loss_target"][0]
    D = x.shape[1]
    shards = {n: a[n][0] for n in BIG}

    fox_scale = 1.0 / math.sqrt(DH)
    n_mine = shards["w_in"].shape[1]
    dev = 4 * lax.axis_index("x") + 2 * lax.axis_index("y") + c_idx
    cols = dev * n_mine + jnp.arange(n_mine)
    is_fox_q = (cols >= 4 * D) & (cols < 5 * D)
    sent = dict(shards, w_in=shards["w_in"] * jnp.where(is_fox_q, fox_scale, 1.0)[None, :])
    sent = {n: v.astype(BF16) for n, v in sent.items()}
    W = {}
    traffic = _Traffic(sent, W, {n: shards[n].shape for n in BIG}, c_idx.astype(jnp.int32).reshape(1), D)
    first = _run_exchange(_ex_ag_chips([sent[n] for n in GATHER_FIRST]), "ag_first_chips")
    traffic.install(GATHER_FIRST, _run_exchange(_ex_ag_sibling(first), "ag_first_sibling"))
    S = {n: a[n] for n in SMALL}

    sq, grad_x, gW, gS = _step_local(x, mem, target, W, S, traffic)
    slots = traffic.finish()
    big = {n: _adamw_reduce(slots[n], shards[n], a["m_" + n][0], a["v_" + n][0], f"adamw_{n}") for n in BIG}

    loss_row = jnp.pad(sq[:1, :1] * (0.5 / D), ((0, 0), (0, PACK_COLS - 1)))
    small_half = _run_exchange(_ex_ag_chips([_pack_small(gS, loss_row)]), "small_ag_chips")
    small_all = _run_exchange(_ex_ag_sibling(small_half), "small_ag_sibling")[0]
    small_slots = small_all.reshape(8, SMALL_ROWS, PACK_COLS)
    zero_row = jnp.zeros((1, PACK_COLS), F32)
    g_sm, d_sm, m_sm, v_sm = _adamw_reduce(
        small_slots, _pack_small({n: a[n] for n in SMALL}, zero_row),
        _pack_small({n: a["m_" + n] for n in SMALL}, zero_row),
        _pack_small({n: a["v_" + n] for n in SMALL}, zero_row), "adamw_small")

    def unpack(which, small):
        out = _unpack_small(small, {n: a[n] for n in SMALL})
        for n in BIG:
            out[n] = big[n][which][None]
        return [out[n] for n in WEIGHTS]

    loss = g_sm[SMALL_ROWS - 1, 0]
    return (loss, grad_x[None], *unpack(0, g_sm), *unpack(1, d_sm), *unpack(2, m_sm), *unpack(3, v_sm))


def kernel(x, mem, ffn1_pre_g, ffn1_w_in, ffn1_w_down, ffn1_post_g, mix_pre_g, w_in, hg_lb_logits, hg_norm_g, fox_f_bias, w_branch_a, w_branch_b, b_gate, w_out, mix_post_g, mem_pre_g, mem_kv_g, w_mq, w_mkv, w_mo, mem_post_g, ffn2_pre_g, ffn2_w_in, ffn2_w_down, ffn2_post_g, loss_target, m_ffn1_pre_g, m_ffn1_w_in, m_ffn1_w_down, m_ffn1_post_g, m_mix_pre_g, m_w_in, m_hg_lb_logits, m_hg_norm_g, m_fox_f_bias, m_w_branch_a, m_w_branch_b, m_b_gate, m_w_out, m_mix_post_g, m_mem_pre_g, m_mem_kv_g, m_w_mq, m_w_mkv, m_w_mo, m_mem_post_g, m_ffn2_pre_g, m_ffn2_w_in, m_ffn2_w_down, m_ffn2_post_g, v_ffn1_pre_g, v_ffn1_w_in, v_ffn1_w_down, v_ffn1_post_g, v_mix_pre_g, v_w_in, v_hg_lb_logits, v_hg_norm_g, v_fox_f_bias, v_w_branch_a, v_w_branch_b, v_b_gate, v_w_out, v_mix_post_g, v_mem_pre_g, v_mem_kv_g, v_w_mq, v_w_mkv, v_w_mo, v_mem_post_g, v_ffn2_pre_g, v_ffn2_w_in, v_ffn2_w_down, v_ffn2_post_g):
    return _train_step(dict(locals()))
```

```python
import functools
import math

import jax
import jax.numpy as jnp
from jax import lax
from jax.experimental import pallas as pl
from jax.experimental.pallas import tpu as pltpu

F32 = jnp.float32
BF16 = jnp.bfloat16
MESH = pl.DeviceIdType.MESH

EPS = 1e-6
HEADS = 8
DH = 128
MEM_HEADS = 4
CHUNK = 128
HALF = CHUNK // 2
SWIGLU_TILE = 256
LANES = 128
PACK_COLS = 1024
ROW_TILE = 512
SEQ_BLOCK = 2048
CUMSUM_BLOCK = 512
XATTN_TILE = 2048
ATTN_TILE = 2048
ATTN_ROWS = 256
EXP_CLAMP = 80.0
NEG_BIG = -1e30

ADAM_LR, ADAM_B1, ADAM_B2, ADAM_EPS, ADAM_WD, ADAM_STEP = 0.001, 0.9, 0.999, 1e-08, 0.01, 10

VMEM_LIMIT = 48 * 1024 * 1024

_DN = {
    "nn": (((1,), (0,)), ((), ())),
    "nt": (((1,), (1,)), ((), ())),
    "tn": (((0,), (0,)), ((), ())),
}

BIG = ["ffn1_w_in", "ffn1_w_down", "w_in", "w_branch_a", "w_branch_b", "w_out", "w_mq", "w_mkv", "w_mo",
       "ffn2_w_in", "ffn2_w_down"]
COL_SHARDED = {"ffn1_w_in", "w_in", "w_mkv", "ffn2_w_in"}
SMALL = ["ffn1_pre_g", "ffn1_post_g", "mix_pre_g", "hg_lb_logits", "hg_norm_g", "fox_f_bias", "b_gate",
         "mix_post_g", "mem_pre_g", "mem_kv_g", "mem_post_g", "ffn2_pre_g", "ffn2_post_g"]
WEIGHTS = ["ffn1_pre_g", "ffn1_w_in", "ffn1_w_down", "ffn1_post_g", "mix_pre_g", "w_in", "hg_lb_logits",
           "hg_norm_g", "fox_f_bias", "w_branch_a", "w_branch_b", "b_gate", "w_out", "mix_post_g", "mem_pre_g",
           "mem_kv_g", "w_mq", "w_mkv", "w_mo", "mem_post_g", "ffn2_pre_g", "ffn2_w_in", "ffn2_w_down",
           "ffn2_post_g"]


def _dot(a, b, mode="nn"):
    return lax.dot_general(a, b, _DN[mode], preferred_element_type=F32)


def _sig(x):
    return 1.0 / (1.0 + jnp.exp(-x))


def _sig_approx(x):
    return pl.reciprocal(1.0 + jnp.exp(-x), approx=True)


def _params(*dims):
    return pltpu.CompilerParams(dimension_semantics=dims if dims else None, vmem_limit_bytes=VMEM_LIMIT)


def _tile(dim, pref):
    if dim <= pref:
        return dim
    t = (pref // LANES) * LANES
    while t >= LANES:
        if dim % t == 0:
            return t
        t -= LANES
    raise ValueError(f"no tile for {dim}")


def _colsum(x):
    return jnp.sum(x, axis=0, keepdims=True)


def _rowsum(x):
    return jnp.sum(x, axis=1, keepdims=True)


def _iota(shape, axis):
    return lax.broadcasted_iota(jnp.int32, shape, axis)


def _pick_row(x, r):
    return _colsum(jnp.where(_iota(x.shape, 0) == r, x, 0.0))


def _tri_dot(tri, x):
    hi = x.astype(BF16)
    r1 = x - hi.astype(F32)
    mid = r1.astype(BF16)
    lo = (r1 - mid.astype(F32)).astype(BF16)
    return _dot(tri, hi) + _dot(tri, mid) + _dot(tri, lo)


_MM_TILES = {"nn": (2048, 512, 1024), "nt": (512, 1024, 4096), "tn": (1024, 1024, 2048)}


def _host_call(body, name, grid, in_specs, out_specs, out_shape, scratch_shapes, dims, args, hosted=None):
    if hosted is None:
        results = pl.pallas_call(body, name=name, grid=grid, in_specs=in_specs, out_specs=out_specs, out_shape=out_shape,
                                 scratch_shapes=scratch_shapes, compiler_params=_params(*dims))(*args)
        return list(results), []
    n_in, n_out, n_sc = len(in_specs), len(out_specs), len(scratch_shapes)
    h_in, h_out = len(hosted.inputs), len(hosted.out_shapes)

    def wrapped(*refs):
        cut = [n_in, h_in, n_out, h_out, n_sc]
        at = [sum(cut[:i]) for i in range(len(cut) + 1)]
        ins, hin, outs, hout, scr = (refs[at[i]:at[i + 1]] for i in range(len(cut)))
        hsems = refs[at[-1]:]
        ids = [pl.program_id(d) for d in range(len(grid))]
        first = functools.reduce(jnp.logical_and, [i == 0 for i in ids])
        last = functools.reduce(jnp.logical_and, [i == g - 1 for i, g in zip(ids, grid)])

        @pl.when(first)
        def _():
            hosted.start(hin, hout, hsems)

        body(*ins, *outs, *scr)

        @pl.when(last)
        def _():
            hosted.wait(hin, hout, hsems)

    results = pl.pallas_call(
        wrapped, name=name, grid=grid, in_specs=list(in_specs) + [_HBM] * h_in,
        out_specs=list(out_specs) + [_HBM] * h_out, out_shape=list(out_shape) + list(hosted.out_shapes),
        scratch_shapes=list(scratch_shapes) + list(hosted.scratch), input_output_aliases=hosted.aliases(n_in, n_out),
        compiler_params=_params(*dims))(*args, *hosted.inputs)
    return list(results[:n_out]), list(results[n_out:])


def _mm(a, b, mode, out_dtype, name, add=None, tm=None, tn=None, tk=None, hosted=None):
    tm, tn, tk = (given or pref for given, pref in zip((tm, tn, tk), _MM_TILES[mode]))
    if mode == "nn":
        (M, K), (K2, N) = a.shape, b.shape
    elif mode == "nt":
        (M, K), (N, K2) = a.shape, b.shape
    else:
        (K, M), (K2, N) = a.shape, b.shape
    assert K == K2, (name, a.shape, b.shape)
    tm, tn, tk = _tile(M, tm), _tile(N, tn), _tile(K, tk)
    nk = K // tk
    if mode == "tn":
        a_spec = pl.BlockSpec((tk, tm), lambda i, j, k: (k, i))
    else:
        a_spec = pl.BlockSpec((tm, tk), lambda i, j, k: (i, k))
    if mode == "nt":
        b_spec = pl.BlockSpec((tn, tk), lambda i, j, k: (j, k))
    else:
        b_spec = pl.BlockSpec((tk, tn), lambda i, j, k: (k, j))
    o_spec = pl.BlockSpec((tm, tn), lambda i, j, k: (i, j))
    has_add = add is not None

    def body(*refs):
        a_ref, b_ref = refs[0], refs[1]
        c_ref = refs[2] if has_add else None
        o_ref = refs[3] if has_add else refs[2]
        part = _dot(a_ref[...], b_ref[...], mode)
        if nk == 1:
            if has_add:
                part = part + c_ref[...]
            o_ref[...] = part.astype(o_ref.dtype)
            return
        acc_ref = refs[-1]
        k = pl.program_id(2)

        @pl.when(k == 0)
        def _():
            acc_ref[...] = part + c_ref[...] if has_add else part

        @pl.when(k > 0)
        def _():
            acc_ref[...] += part

        @pl.when(k == nk - 1)
        def _():
            o_ref[...] = acc_ref[...].astype(o_ref.dtype)

    in_specs = [a_spec, b_spec] + ([o_spec] if has_add else [])
    args = (a, b) + ((add,) if has_add else ())
    (out,), landed = _host_call(
        body, name, (M // tm, N // tn, nk), in_specs, [o_spec], [jax.ShapeDtypeStruct((M, N), out_dtype)],
        [pltpu.VMEM((tm, tn), F32)] if nk > 1 else [], ("parallel", "parallel", "arbitrary"), args, hosted)
    return out if hosted is None else (out, landed)


def _rms_fwd(x, g, name, out_dtype=BF16):
    T, D = x.shape
    tr = _tile(T, ROW_TILE)

    def body(x_ref, g_ref, o_ref):
        xv = x_ref[...]
        r = lax.rsqrt(jnp.mean(xv * xv, axis=-1, keepdims=True) + EPS)
        o_ref[...] = (xv * r * g_ref[...]).astype(o_ref.dtype)

    return pl.pallas_call(
        body, name=name, grid=(T // tr,),
        in_specs=[pl.BlockSpec((tr, D), lambda i: (i, 0)), pl.BlockSpec((1, D), lambda i: (0, 0))],
        out_specs=pl.BlockSpec((tr, D), lambda i: (i, 0)),
        out_shape=jax.ShapeDtypeStruct((T, D), out_dtype), compiler_params=_params("parallel"),
    )(x, g)


def _resid_rms(x, z, g, scale, name):
    T, D = x.shape
    tr = _tile(T, ROW_TILE)

    def body(x_ref, z_ref, g_ref, o_ref):
        zv = z_ref[...]
        r = lax.rsqrt(jnp.mean(zv * zv, axis=-1, keepdims=True) + EPS)
        o_ref[...] = x_ref[...] + scale * (zv * r * g_ref[...])

    row = pl.BlockSpec((tr, D), lambda i: (i, 0))
    return pl.pallas_call(
        body, name=name, grid=(T // tr,), in_specs=[row, row, pl.BlockSpec((1, D), lambda i: (0, 0))],
        out_specs=row, out_shape=jax.ShapeDtypeStruct((T, D), F32), compiler_params=_params("parallel"),
    )(x, z, g)


def _final_loss(x, z, g, scale, target, name):
    T, D = x.shape
    tr = _tile(T, ROW_TILE)

    def body(x_ref, z_ref, g_ref, t_ref, dx_ref, acc_ref):
        @pl.when(pl.program_id(0) == 0)
        def _():
            acc_ref[...] = jnp.zeros_like(acc_ref)

        zv = z_ref[...]
        r = lax.rsqrt(jnp.mean(zv * zv, axis=-1, keepdims=True) + EPS)
        e = x_ref[...] + scale * (zv * r * g_ref[...]) - t_ref[...]
        dx_ref[...] = e * (1.0 / D)
        acc_ref[...] += _colsum(_rowsum(e * e))

    row = pl.BlockSpec((tr, D), lambda i: (i, 0))
    return pl.pallas_call(
        body, name=name, grid=(T // tr,), in_specs=[row, row, pl.BlockSpec((1, D), lambda i: (0, 0)), row],
        out_specs=[row, pl.BlockSpec((8, LANES), lambda i: (0, 0))],
        out_shape=[jax.ShapeDtypeStruct((T, D), F32), jax.ShapeDtypeStruct((8, LANES), F32)],
        compiler_params=_params("arbitrary"),
    )(x, z, g, target)


def _rms_bwd(xin, g, dy, scale, name, out_dtype, resid=None):
    T, D = xin.shape
    tr = _tile(T, ROW_TILE)
    has_resid = resid is not None

    def body(*refs):
        x_ref, g_ref, dy_ref = refs[:3]
        r_ref = refs[3] if has_resid else None
        dx_ref, dg_ref = refs[-2], refs[-1]

        @pl.when(pl.program_id(0) == 0)
        def _():
            dg_ref[...] = jnp.zeros_like(dg_ref)

        xv = x_ref[...]
        r = lax.rsqrt(jnp.mean(xv * xv, axis=-1, keepdims=True) + EPS)
        xh = xv * r
        dyv = dy_ref[...].astype(F32) * scale
        dxh = dyv * g_ref[...]
        dx = r * (dxh - xh * jnp.mean(dxh * xh, axis=-1, keepdims=True))
        if has_resid:
            dx = dx + r_ref[...]
        dx_ref[...] = dx.astype(dx_ref.dtype)
        dg_ref[...] += _colsum(dyv * xh)

    row = pl.BlockSpec((tr, D), lambda i: (i, 0))
    vec = pl.BlockSpec((1, D), lambda i: (0, 0))
    return pl.pallas_call(
        body, name=name, grid=(T // tr,), in_specs=[row, vec, row] + ([row] if has_resid else []),
        out_specs=[row, vec],
        out_shape=[jax.ShapeDtypeStruct((T, D), out_dtype), jax.ShapeDtypeStruct((1, D), F32)],
        compiler_params=_params("arbitrary"),
    )(*((xin, g, dy) + ((resid,) if has_resid else ())))


def _mm_swiglu(h, w_in, name, hosted=None):
    T, K = h.shape
    F2 = w_in.shape[1]
    tf = SWIGLU_TILE
    tm = _tile(T, _MM_TILES["nn"][0])

    def body(h_ref, w_ref, u_ref, a_ref):
        u = _dot(h_ref[...], w_ref[...])
        u_ref[...] = u.astype(u_ref.dtype)
        gate, up = u[:, :tf], u[:, tf:]
        a_ref[...] = (gate * _sig_approx(gate) * up).astype(a_ref.dtype)

    (u, a), landed = _host_call(
        body, name, (T // tm, F2 // (2 * tf)),
        [pl.BlockSpec((tm, K), lambda i, j: (i, 0)), pl.BlockSpec((K, 2 * tf), lambda i, j: (0, j))],
        [pl.BlockSpec((tm, 2 * tf), lambda i, j: (i, j)), pl.BlockSpec((tm, tf), lambda i, j: (i, j))],
        [jax.ShapeDtypeStruct((T, F2), BF16), jax.ShapeDtypeStruct((T, F2 // 2), BF16)], [],
        ("parallel", "parallel"), (h, w_in), hosted)
    return u, a, landed


def _mm_swiglu_bwd(dz, w_down, u, name):
    T, D = dz.shape
    F = w_down.shape[0]
    tf = SWIGLU_TILE
    tm = _tile(T, _MM_TILES["nn"][0])

    def body(dz_ref, w_ref, u_ref, o_ref):
        d = _dot(dz_ref[...], w_ref[...], "nt")
        gate = u_ref[:, :tf].astype(F32)
        up = u_ref[:, tf:].astype(F32)
        s = _sig_approx(gate)
        o_ref[:, :tf] = (d * up * (s * (1.0 + gate * (1.0 - s)))).astype(o_ref.dtype)
        o_ref[:, tf:] = (d * gate * s).astype(o_ref.dtype)

    return pl.pallas_call(
        body, name=name, grid=(T // tm, F // tf),
        in_specs=[pl.BlockSpec((tm, D), lambda i, j: (i, 0)), pl.BlockSpec((tf, D), lambda i, j: (j, 0)),
                  pl.BlockSpec((tm, 2 * tf), lambda i, j: (i, j))],
        out_specs=pl.BlockSpec((tm, 2 * tf), lambda i, j: (i, j)),
        out_shape=jax.ShapeDtypeStruct((T, 2 * F), BF16), compiler_params=_params("parallel", "parallel"),
    )(dz, w_down, u)


def _hgout_fwd(o_a, pm, g, name):
    T, D = o_a.shape
    tr = _tile(T, ROW_TILE)

    def body(o_ref, ga_ref, g_ref, out_ref):
        ov = o_ref[...]
        r = lax.rsqrt(jnp.mean(ov * ov, axis=-1, keepdims=True) + EPS)
        ga = ga_ref[...].astype(F32)
        out_ref[...] = (ov * r * g_ref[...] * (ga * _sig(ga))).astype(out_ref.dtype)

    row = pl.BlockSpec((tr, D), lambda i: (i, 0))
    return pl.pallas_call(
        body, name=name, grid=(T // tr,),
        in_specs=[row, pl.BlockSpec((tr, D), lambda i: (i, 3)), pl.BlockSpec((1, D), lambda i: (0, 0))],
        out_specs=row, out_shape=jax.ShapeDtypeStruct((T, D), BF16), compiler_params=_params("parallel"),
    )(o_a, pm, g)


def _hgout_bwd(o_a, pm, g, d_out, name):
    T, D = o_a.shape
    tr = _tile(T, ROW_TILE)

    def body(o_ref, ga_ref, g_ref, d_ref, do_ref, dga_ref, dg_ref):
        @pl.when(pl.program_id(0) == 0)
        def _():
            dg_ref[...] = jnp.zeros_like(dg_ref)

        ov = o_ref[...]
        r = lax.rsqrt(jnp.mean(ov * ov, axis=-1, keepdims=True) + EPS)
        oh = ov * r
        ga = ga_ref[...].astype(F32)
        s = _sig(ga)
        d = d_ref[...].astype(F32)
        dn = d * (ga * s)
        dga_ref[...] = (d * (oh * g_ref[...]) * (s * (1.0 + ga * (1.0 - s)))).astype(dga_ref.dtype)
        dxh = dn * g_ref[...]
        do_ref[...] = (r * (dxh - oh * jnp.mean(dxh * oh, axis=-1, keepdims=True))).astype(do_ref.dtype)
        dg_ref[...] += _colsum(dn * oh)

    row = pl.BlockSpec((tr, D), lambda i: (i, 0))
    vec = pl.BlockSpec((1, D), lambda i: (0, 0))
    return pl.pallas_call(
        body, name=name, grid=(T // tr,), in_specs=[row, pl.BlockSpec((tr, D), lambda i: (i, 3)), vec, row],
        out_specs=[row, row, vec],
        out_shape=[jax.ShapeDtypeStruct((T, D), BF16), jax.ShapeDtypeStruct((T, D), BF16),
                   jax.ShapeDtypeStruct((1, D), F32)],
        compiler_params=_params("arbitrary"),
    )(o_a, pm, g, d_out)


def _merge_fwd(ya, yb, pg, bg, name):
    T, D = ya.shape
    tr = _tile(T, 256)

    def body(ya_ref, yb_ref, pg_ref, bg_ref, o_ref):
        g0 = _sig(pg_ref[:, :D] + bg_ref[:, :D])
        g1 = _sig(pg_ref[:, D:] + bg_ref[:, D:])
        o_ref[...] = (g0 * ya_ref[...] + g1 * yb_ref[...]).astype(o_ref.dtype)

    row = pl.BlockSpec((tr, D), lambda i: (i, 0))
    return pl.pallas_call(
        body, name=name, grid=(T // tr,),
        in_specs=[row, row, pl.BlockSpec((tr, 2 * D), lambda i: (i, 0)), pl.BlockSpec((1, 2 * D), lambda i: (0, 0))],
        out_specs=row, out_shape=jax.ShapeDtypeStruct((T, D), BF16), compiler_params=_params("parallel"),
    )(ya, yb, pg, bg)


def _merge_bwd(dy, ya, yb, pg, bg, name):
    T, D = ya.shape
    tr = _tile(T, 256)

    def body(dy_ref, ya_ref, yb_ref, pg_ref, bg_ref, dya_ref, dyb_ref, dpg_ref, dbg_ref):
        @pl.when(pl.program_id(0) == 0)
        def _():
            dbg_ref[...] = jnp.zeros_like(dbg_ref)

        d = dy_ref[...].astype(F32)
        g0 = _sig(pg_ref[:, :D] + bg_ref[:, :D])
        g1 = _sig(pg_ref[:, D:] + bg_ref[:, D:])
        dya_ref[...] = (d * g0).astype(dya_ref.dtype)
        dyb_ref[...] = (d * g1).astype(dyb_ref.dtype)
        dg0 = d * ya_ref[...] * (g0 * (1.0 - g0))
        dg1 = d * yb_ref[...] * (g1 * (1.0 - g1))
        dpg_ref[:, :D] = dg0.astype(dpg_ref.dtype)
        dpg_ref[:, D:] = dg1.astype(dpg_ref.dtype)
        dbg_ref[:, :D] += _colsum(dg0)
        dbg_ref[:, D:] += _colsum(dg1)

    row = pl.BlockSpec((tr, D), lambda i: (i, 0))
    wide = pl.BlockSpec((tr, 2 * D), lambda i: (i, 0))
    wvec = pl.BlockSpec((1, 2 * D), lambda i: (0, 0))
    return pl.pallas_call(
        body, name=name, grid=(T // tr,), in_specs=[row, row, row, wide, wvec],
        out_specs=[row, row, wide, wvec],
        out_shape=[jax.ShapeDtypeStruct((T, D), BF16), jax.ShapeDtypeStruct((T, D), BF16),
                   jax.ShapeDtypeStruct((T, 2 * D), BF16), jax.ShapeDtypeStruct((1, 2 * D), F32)],
        compiler_params=_params("arbitrary"),
    )(dy, ya, yb, pg, bg)


def _hgrn_chunk_terms(q, fl, lb, tri):
    shape = q.shape
    row = _iota(shape, 0)
    sg = _sig(fl)
    f = lb + (1.0 - lb) * sg
    k = 1.0 - f
    b = _tri_dot(tri, jnp.log(f))
    ref1 = jnp.where(row < HALF, _pick_row(b, HALF // 2), _pick_row(b, HALF + HALF // 2))
    b_half = _pick_row(b, HALF - 1)
    b_last = _pick_row(b, CHUNK - 1)
    sq = _sig(q)
    qs = q * sq
    e_q1 = jnp.exp(jnp.minimum(b - ref1, EXP_CLAMP))
    e_k1 = jnp.exp(jnp.minimum(ref1 - b, EXP_CLAMP))
    e_q2 = jnp.exp(jnp.minimum(b - b_half, 0.0))
    e_k2 = jnp.exp(jnp.minimum(b_half - b, 0.0))
    e_b = jnp.exp(b)
    e_kd = jnp.exp(b_last - b)
    return dict(sg=sg, f=f, k=k, sq=sq, qs=qs, e_q1=e_q1, e_k1=e_k1, e_q2=e_q2, e_k2=e_k2, e_b=e_b, e_kd=e_kd,
                e_last=jnp.exp(b_last))


def _hgrn_masks():
    r = _iota((CHUNK, CHUNK), 0)
    c = _iota((CHUNK, CHUNK), 1)
    causal = r >= c
    same = (r < HALF) == (c < HALF)
    return causal, causal & same, (r >= HALF) & (c < HALF)


def _softmax_lb(lbl_ref):
    l0, l1 = lbl_ref[0, 0], lbl_ref[1, 0]
    mx = jnp.maximum(l0, l1)
    e0, e1 = jnp.exp(l0 - mx), jnp.exp(l1 - mx)
    return e0 / (e0 + e1)


def _hgrn_fwd(pm, lbl, name):
    T = pm.shape[0]
    tb = _tile(T, SEQ_BLOCK)
    nc = tb // CHUNK

    def body(q_ref, f_ref, i_ref, lbl_ref, o_ref, st_ref, s_sc):
        @pl.when(pl.program_id(1) == 0)
        def _():
            s_sc[...] = jnp.zeros_like(s_sc)

        lb = _softmax_lb(lbl_ref)
        causal, m1, m2 = _hgrn_masks()
        tri = jnp.where(causal, 1.0, 0.0).astype(BF16)
        parts = []
        for ci in range(nc):
            sl = pl.ds(ci * CHUNK, CHUNK)
            t = _hgrn_chunk_terms(q_ref[sl, :].astype(F32), f_ref[sl, :].astype(F32), lb, tri)
            iv = i_ref[sl, :]
            a1 = _dot((t["qs"] * t["e_q1"]).astype(BF16), (t["k"] * t["e_k1"]).astype(BF16), "nt")
            a2 = _dot((t["qs"] * t["e_q2"]).astype(BF16), (t["k"] * t["e_k2"]).astype(BF16), "nt")
            a = jnp.where(m1, a1, 0.0) + jnp.where(m2, a2, 0.0)
            parts.append((_dot(a.astype(BF16), iv), (t["qs"] * t["e_b"]).astype(BF16),
                          _dot(iv, (t["k"] * t["e_kd"]).astype(BF16), "tn"), t["e_last"]))
        st = s_sc[...]
        for ci, (o_intra, qi, grow, e_last) in enumerate(parts):
            st_ref[0, ci] = st
            o_ref[pl.ds(ci * CHUNK, CHUNK), :] = o_intra + _dot(qi, st.astype(BF16), "nt")
            st = e_last * st + grow
        s_sc[...] = st

    blk = lambda off: pl.BlockSpec((tb, DH), lambda h, b: (b, off + h))
    return pl.pallas_call(
        body, name=name, grid=(HEADS, T // tb),
        in_specs=[blk(0), blk(HEADS), blk(2 * HEADS), pl.BlockSpec((2, 1, 1, DH), lambda h, b: (0, h, 0, 0))],
        out_specs=[pl.BlockSpec((tb, DH), lambda h, b: (b, h)),
                   pl.BlockSpec((1, nc, DH, DH), lambda h, b: (h, b, 0, 0))],
        out_shape=[jax.ShapeDtypeStruct((T, HEADS * DH), F32),
                   jax.ShapeDtypeStruct((HEADS, T // CHUNK, DH, DH), F32)],
        scratch_shapes=[pltpu.VMEM((DH, DH), F32)],
        compiler_params=_params("parallel", "arbitrary"),
    )(pm, pm, pm, lbl)


def _hgrn_bwd(pm, lbl, states, do, name):
    T = pm.shape[0]
    tb = _tile(T, SEQ_BLOCK)
    nc = tb // CHUNK
    nb = T // tb

    def body(q_ref, f_ref, i_ref, lbl_ref, st_ref, do_ref, dq_ref, df_ref, di_ref, dl_ref, ds_sc, dlb_sc):
        @pl.when(pl.program_id(1) == 0)
        def _():
            ds_sc[...] = jnp.zeros_like(ds_sc)
            dlb_sc[...] = jnp.zeros_like(dlb_sc)

        lb = _softmax_lb(lbl_ref)
        causal, m1, m2 = _hgrn_masks()
        tri = jnp.where(causal, 1.0, 0.0).astype(BF16)
        tri_rev = jnp.where(_iota((CHUNK, CHUNK), 0) <= _iota((CHUNK, CHUNK), 1), 1.0, 0.0).astype(BF16)
        last_row = _iota((CHUNK, DH), 0) == CHUNK - 1
        dsn = ds_sc[...]
        dlb = jnp.zeros((1, DH), F32)
        for ci in reversed(range(nc)):
            sl = pl.ds(ci * CHUNK, CHUNK)
            q = q_ref[sl, :].astype(F32)
            t = _hgrn_chunk_terms(q, f_ref[sl, :].astype(F32), lb, tri)
            iv = i_ref[sl, :]
            dov = do_ref[sl, :]
            qe1, ke1 = t["qs"] * t["e_q1"], t["k"] * t["e_k1"]
            qe2, ke2 = t["qs"] * t["e_q2"], t["k"] * t["e_k2"]
            qi, kd = t["qs"] * t["e_b"], t["k"] * t["e_kd"]
            qe1b, ke1b, qe2b, ke2b = qe1.astype(BF16), ke1.astype(BF16), qe2.astype(BF16), ke2.astype(BF16)
            a = jnp.where(m1, _dot(qe1b, ke1b, "nt"), 0.0) + jnp.where(m2, _dot(qe2b, ke2b, "nt"), 0.0)
            st = st_ref[0, ci]
            dsnb = dsn.astype(BF16)
            da = _dot(dov, iv, "nt")
            da1 = jnp.where(m1, da, 0.0).astype(BF16)
            da2 = jnp.where(m2, da, 0.0).astype(BF16)
            di_ref[sl, :] = (_dot(a.astype(BF16), dov, "tn") + _dot(kd.astype(BF16), dsnb, "nt")).astype(di_ref.dtype)
            dqe1, dke1 = _dot(da1, ke1b), _dot(da1, qe1b, "tn")
            dqe2, dke2 = _dot(da2, ke2b), _dot(da2, qe2b, "tn")
            dqi = _dot(dov, st.astype(BF16))
            dkd = _dot(iv, dsnb)
            ds_before = t["e_last"] * dsn + _dot(dov, qi.astype(BF16), "tn")
            dqs =dqe1 * t["e_q1"] + dqe2 * t["e_q2"] + dqi * t["e_b"]
            dk = dke1 * t["e_k1"] + dke2 * t["e_k2"] + dkd * t["e_kd"]
            qib, kdb = qi.astype(BF16).astype(F32), kd.astype(BF16).astype(F32)
            db = (dqe1 * qe1b.astype(F32) - dke1 * ke1b.astype(F32) + dqe2 * qe2b.astype(F32)
                  - dke2 * ke2b.astype(F32) + dqi * qib - dkd * kdb)
            extra = _colsum(dkd * kdb) + t["e_last"] * _colsum(dsn * st)
            db = db + jnp.where(last_row, extra, 0.0)
            dlf = _tri_dot(tri_rev, db)
            dfv = dlf / t["f"] - dk
            sg = t["sg"]
            df_ref[sl, :] = (dfv * (1.0 - lb) * sg * (1.0 - sg)).astype(df_ref.dtype)
            dlb = dlb + _colsum(dfv * (1.0 - sg))
            sq = t["sq"]
            dq_ref[sl, :] = (dqs * (sq * (1.0 + q * (1.0 - sq)))).astype(dq_ref.dtype)
            dsn = ds_before
        ds_sc[...] = dsn
        dlb_sc[...] += dlb

        @pl.when(pl.program_id(1) == nb - 1)
        def _():
            dl0 = dlb_sc[...] * lb * (1.0 - lb)
            dl_ref[0, 0] = dl0
            dl_ref[1, 0] = -dl0

    blk = lambda off: pl.BlockSpec((tb, DH), lambda h, b: (nb - 1 - b, off + h))
    lspec = pl.BlockSpec((2, 1, 1, DH), lambda h, b: (0, h, 0, 0))
    out_blk = pl.BlockSpec((tb, DH), lambda h, b: (nb - 1 - b, h))
    D = HEADS * DH
    return pl.pallas_call(
        body, name=name, grid=(HEADS, nb),
        in_specs=[blk(0), blk(HEADS), blk(2 * HEADS), lspec,
                  pl.BlockSpec((1, nc, DH, DH), lambda h, b: (h, nb - 1 - b, 0, 0)), out_blk],
        out_specs=[out_blk, out_blk, out_blk, lspec],
        out_shape=[jax.ShapeDtypeStruct((T, D), BF16)] * 3 + [jax.ShapeDtypeStruct((2, HEADS, 1, DH), F32)],
        scratch_shapes=[pltpu.VMEM((DH, DH), F32), pltpu.VMEM((1, DH), F32)],
        compiler_params=_params("parallel", "arbitrary"),
    )(pm, pm, pm, lbl, states, do)


def _log_sigmoid(x):
    return jnp.minimum(x, 0.0) - jnp.log(1.0 + jnp.exp(-jnp.abs(x)))


def _fox_cumsum(pf, bias, name):
    T = pf.shape[0]
    tb = _tile(T, CUMSUM_BLOCK)

    def body(x_ref, b_ref, c_ref, carry):
        @pl.when(pl.program_id(0) == 0)
        def _():
            carry[...] = jnp.zeros_like(carry)

        tri = jnp.where(_iota((tb, tb), 0) >= _iota((tb, tb), 1), 1.0, 0.0).astype(BF16)
        c = _tri_dot(tri, _log_sigmoid(x_ref[...] + b_ref[...])) + carry[...]
        c_ref[...] = c
        carry[...] = _pick_row(c, tb - 1)

    row = pl.BlockSpec((tb, LANES), lambda i: (i, 0))
    return pl.pallas_call(
        body, name=name, grid=(T // tb,), in_specs=[row, pl.BlockSpec((1, LANES), lambda i: (0, 0))],
        out_specs=row, out_shape=jax.ShapeDtypeStruct((T, LANES), F32),
        scratch_shapes=[pltpu.VMEM((1, LANES), F32)], compiler_params=_params("arbitrary"),
    )(pf, bias)


def _fox_dcum(dc, pf, bias, name):
    T = pf.shape[0]
    tb = _tile(T, CUMSUM_BLOCK)
    nb = T // tb

    def body(dc_ref, x_ref, b_ref, dx_ref, db_ref, carry):
        @pl.when(pl.program_id(0) == 0)
        def _():
            carry[...] = jnp.zeros_like(carry)
            db_ref[...] = jnp.zeros_like(db_ref)

        tri_rev = jnp.where(_iota((tb, tb), 0) <= _iota((tb, tb), 1), 1.0, 0.0).astype(BF16)
        dls = _tri_dot(tri_rev, dc_ref[...]) + carry[...]
        carry[...] = _pick_row(dls, 0)
        dx = dls * (1.0 - _sig(x_ref[...] + b_ref[...]))
        dx_ref[...] = dx
        db_ref[...] += _colsum(dx)

    row = pl.BlockSpec((tb, LANES), lambda i: (nb - 1 - i, 0))
    vec = pl.BlockSpec((1, LANES), lambda i: (0, 0))
    return pl.pallas_call(
        body, name=name, grid=(nb,), in_specs=[row, row, vec], out_specs=[row, vec],
        out_shape=[jax.ShapeDtypeStruct((T, LANES), F32), jax.ShapeDtypeStruct((1, LANES), F32)],
        scratch_shapes=[pltpu.VMEM((1, LANES), F32)], compiler_params=_params("arbitrary"),
    )(dc, pf, bias)


_Q_OFF, _K_OFF, _V_OFF = 4 * HEADS, 5 * HEADS, 6 * HEADS


def _causal_pairs(nq, by_key):
    if by_key:
        pairs = [(i, j) for j in range(nq) for i in range(j, nq)]
    else:
        pairs = [(i, j) for i in range(nq) for j in range(i + 1)]
    return jnp.asarray([p[0] for p in pairs], jnp.int32), jnp.asarray([p[1] for p in pairs], jnp.int32)


def _fox_logits(q, k, ck, row0, masked):
    s = _dot(q, k, "nt") - ck
    if masked:
        s = jnp.where(_iota(s.shape, 0) + row0 >= _iota(s.shape, 1), s, NEG_BIG)
    return s


def _ones_column(rows):
    return jnp.where(_iota((rows, DH), 1) == 0, 1.0, 0.0).astype(BF16)


def _fox_fwd(pm, c_col, c_row, name):
    T = pm.shape[0]
    tq = _tile(T, ATTN_TILE)
    nq = T // tq
    rg = min(ATTN_ROWS, tq)
    qi_tab, kj_tab = _causal_pairs(nq, by_key=False)

    def body(qi_ref, kj_ref, q_ref, k_ref, v_ref, cq_ref, ck_ref, o_ref, lse_ref, m_sc, acc_sc):
        t = pl.program_id(1)
        i, j = qi_ref[t], kj_ref[t]

        @pl.when(j == 0)
        def _():
            m_sc[...] = jnp.full_like(m_sc, NEG_BIG)
            acc_sc[...] = jnp.zeros_like(acc_sc)

        def step(diag):
            m_all, acc_all = m_sc[...], acc_sc[...]
            ones = _ones_column(tq)
            ms, accs = [], []
            for r in range(tq // rg):
                rows = slice(r * rg, (r + 1) * rg)
                w = (r + 1) * rg if diag else tq
                cq = cq_ref[0, rows, :]
                s = _fox_logits(q_ref[rows, :], k_ref[:w, :], ck_ref[0, :, :w], r * rg, diag)
                m_old = m_all[rows, :]
                m_new = jnp.maximum(m_old, jnp.max(s, axis=1, keepdims=True) + cq)
                alpha = jnp.exp(m_old - m_new)
                p = jnp.exp(s - (m_new - cq)).astype(BF16)
                v_one = jnp.concatenate([v_ref[:w, :], ones[:w, :]], axis=1)
                ms.append(m_new)
                accs.append(alpha * acc_all[rows, :] + _dot(p, v_one))
            m_sc[...] = jnp.concatenate(ms, axis=0)
            acc_sc[...] = jnp.concatenate(accs, axis=0)

        @pl.when(j < i)
        def _():
            step(False)

        @pl.when(j == i)
        def _():
            step(True)
            acc = acc_sc[...]
            denom = acc[:, DH:DH + 1]
            o_ref[...] = (acc[:, :DH] / denom).astype(o_ref.dtype)
            lse_ref[0] = m_sc[...] + jnp.log(denom)

    kv = lambda off: pl.BlockSpec((tq, DH), lambda h, t, qi, kj: (kj[t], off + h))
    col = pl.BlockSpec((1, tq, 1), lambda h, t, qi, kj: (h, qi[t], 0))
    grid_spec = pltpu.PrefetchScalarGridSpec(
        num_scalar_prefetch=2, grid=(HEADS, qi_tab.shape[0]),
        in_specs=[pl.BlockSpec((tq, DH), lambda h, t, qi, kj: (qi[t], _Q_OFF + h)), kv(_K_OFF), kv(_V_OFF), col,
                  pl.BlockSpec((1, 1, tq), lambda h, t, qi, kj: (h, 0, kj[t]))],
        out_specs=[pl.BlockSpec((tq, DH), lambda h, t, qi, kj: (qi[t], h)), col],
        scratch_shapes=[pltpu.VMEM((tq, 1), F32), pltpu.VMEM((tq, 2 * DH), F32)])
    return pl.pallas_call(
        body, name=name, grid_spec=grid_spec,
        out_shape=[jax.ShapeDtypeStruct((T, HEADS * DH), BF16), jax.ShapeDtypeStruct((HEADS, T, 1), F32)],
        compiler_params=_params("parallel", "arbitrary"),
    )(qi_tab, kj_tab, pm, pm, pm, c_col, c_row)


def _fox_delta(do, o, name):
    T, D = o.shape
    tr = _tile(T, ROW_TILE)

    def body(do_ref, o_ref, d_ref):
        prod = do_ref[...].astype(F32) * o_ref[...].astype(F32)
        for h in range(HEADS):
            d_ref[h] = _rowsum(prod[:, h * DH:(h + 1) * DH])

    row = pl.BlockSpec((tr, D), lambda i: (i, 0))
    return pl.pallas_call(
        body, name=name, grid=(T // tr,), in_specs=[row, row],
        out_specs=pl.BlockSpec((HEADS, tr, 1), lambda i: (0, i, 0)),
        out_shape=jax.ShapeDtypeStruct((HEADS, T, 1), F32), compiler_params=_params("parallel"),
    )(do, o)


def _fox_bwd(pm, c_col, c_row, do, lse, delta, name):
    T = pm.shape[0]
    tq = _tile(T, ATTN_TILE)
    nq = T // tq
    rg = min(ATTN_ROWS, tq)
    qi_tab, kj_tab = _causal_pairs(nq, by_key=True)
    npairs = qi_tab.shape[0]

    def body(qi_ref, kj_ref, q_ref, k_ref, v_ref, cq_ref, ck_ref, do_ref, lse_ref, dl_ref,
             dq_ref, dk_ref, dv_ref, rsum_ref, csum_ref, dq_sc, dk_sc, dv_sc):
        t = pl.program_id(1)
        i, j = qi_ref[t], kj_ref[t]

        @pl.when(t == 0)
        def _():
            dq_sc[...] = jnp.zeros_like(dq_sc)

        @pl.when(i == j)
        def _():
            dk_sc[...] = jnp.zeros_like(dk_sc)
            dv_sc[...] = jnp.zeros_like(dv_sc)

        base = pl.multiple_of(i * tq, tq)

        def step(diag):
            ones = _ones_column(tq)
            for r in range(tq // rg):
                rows = slice(r * rg, (r + 1) * rg)
                w = (r + 1) * rg if diag else tq
                qr, dor = q_ref[rows, :], do_ref[rows, :]
                s = _fox_logits(qr, k_ref[:w, :], ck_ref[0, :, :w], r * rg, diag)
                p = jnp.exp(s - (lse_ref[0, rows, :] - cq_ref[0, rows, :]))
                dp = _dot(dor, v_ref[:w, :], "nt")
                dsb = (p * (dp - dl_ref[0, rows, :])).astype(BF16)
                dv_sc[:w, :] += _dot(p.astype(BF16), dor, "tn")
                dk_sc[:w, :] += _dot(dsb, jnp.concatenate([qr, ones[rows, :]], axis=1), "tn")
                dq_sc[pl.ds(base + r * rg, rg), :] += _dot(dsb, jnp.concatenate([k_ref[:w, :], ones[:w, :]], axis=1))

        @pl.when(i > j)
        def _():
            step(False)

        @pl.when(i == j)
        def _():
            step(True)

        @pl.when(i == nq - 1)
        def _():
            dk_ref[...] = dk_sc[:, :DH].astype(dk_ref.dtype)
            dv_ref[...] = dv_sc[...].astype(dv_ref.dtype)
            csum_ref[0] = dk_sc[:, DH:DH + 1]

        @pl.when(t == npairs - 1)
        def _():
            dq_ref[...] = dq_sc[:, :DH].astype(dq_ref.dtype)
            rsum_ref[0] = dq_sc[:, DH:DH + 1]

    col = pl.BlockSpec((1, tq, 1), lambda h, t, qi, kj: (h, qi[t], 0))
    kv = lambda off: pl.BlockSpec((tq, DH), lambda h, t, qi, kj: (kj[t], off + h))
    kv_out = pl.BlockSpec((tq, DH), lambda h, t, qi, kj: (kj[t], h))
    grid_spec = pltpu.PrefetchScalarGridSpec(
        num_scalar_prefetch=2, grid=(HEADS, npairs),
        in_specs=[pl.BlockSpec((tq, DH), lambda h, t, qi, kj: (qi[t], _Q_OFF + h)), kv(_K_OFF), kv(_V_OFF), col,
                  pl.BlockSpec((1, 1, tq), lambda h, t, qi, kj: (h, 0, kj[t])),
                  pl.BlockSpec((tq, DH), lambda h, t, qi, kj: (qi[t], h)), col, col],
        out_specs=[pl.BlockSpec((T, DH), lambda h, t, qi, kj: (0, h)), kv_out, kv_out,
                   pl.BlockSpec((1, T, 1), lambda h, t, qi, kj: (h, 0, 0)),
                   pl.BlockSpec((1, tq, 1), lambda h, t, qi, kj: (h, kj[t], 0))],
        scratch_shapes=[pltpu.VMEM((T, 2 * DH), F32), pltpu.VMEM((tq, 2 * DH), F32), pltpu.VMEM((tq, DH), F32)])
    D = HEADS * DH
    return pl.pallas_call(
        body, name=name, grid_spec=grid_spec,
        out_shape=[jax.ShapeDtypeStruct((T, D), BF16)] * 3 + [jax.ShapeDtypeStruct((HEADS, T, 1), F32)] * 2,
        compiler_params=_params("parallel", "arbitrary"),
    )(qi_tab, kj_tab, pm, pm, pm, c_col, c_row, do, lse, delta)


def _xattn_fwd(q, kv, name):
    T, D = q.shape
    M = kv.shape[0]
    dh = D // MEM_HEADS
    tq = _tile(T, XATTN_TILE)
    scale = 1.0 / math.sqrt(dh)

    def body(q_ref, kv_ref, o_ref):
        for h in range(MEM_HEADS):
            cs = slice(h * dh, (h + 1) * dh)
            s = _dot(q_ref[:, cs], kv_ref[:, cs], "nt") * scale
            p = jnp.exp(s - jnp.max(s, axis=1, keepdims=True))
            p = p / _rowsum(p)
            o_ref[:, cs] = _dot(p.astype(BF16), kv_ref[:, D + h * dh:D + (h + 1) * dh]).astype(o_ref.dtype)

    row = pl.BlockSpec((tq, D), lambda i: (i, 0))
    return pl.pallas_call(
        body, name=name, grid=(T // tq,), in_specs=[row, pl.BlockSpec((M, 2 * D), lambda i: (0, 0))],
        out_specs=row, out_shape=jax.ShapeDtypeStruct((T, D), BF16), compiler_params=_params("parallel"),
    )(q, kv)


def _xattn_bwd(q, kv, do, name):
    T, D = q.shape
    M = kv.shape[0]
    dh = D // MEM_HEADS
    tq = _tile(T, XATTN_TILE)
    scale = 1.0 / math.sqrt(dh)

    def body(q_ref, kv_ref, do_ref, dq_ref, dkv_ref):
        @pl.when(pl.program_id(0) == 0)
        def _():
            dkv_ref[...] = jnp.zeros_like(dkv_ref)

        for h in range(MEM_HEADS):
            cs = slice(h * dh, (h + 1) * dh)
            vs = slice(D + h * dh, D + (h + 1) * dh)
            s = _dot(q_ref[:, cs], kv_ref[:, cs], "nt") * scale
            p = jnp.exp(s - jnp.max(s, axis=1, keepdims=True))
            p = p / _rowsum(p)
            dp = _dot(do_ref[:, cs], kv_ref[:, vs], "nt")
            ds = (p * (dp - _rowsum(p * dp)) * scale).astype(BF16)
            dq_ref[:, cs] = _dot(ds, kv_ref[:, cs]).astype(dq_ref.dtype)
            dkv_ref[:, cs] += _dot(ds, q_ref[:, cs], "tn")
            dkv_ref[:, vs] += _dot(p.astype(BF16), do_ref[:, cs], "tn")

    row = pl.BlockSpec((tq, D), lambda i: (i, 0))
    full = pl.BlockSpec((M, 2 * D), lambda i: (0, 0))
    return pl.pallas_call(
        body, name=name, grid=(T // tq,), in_specs=[row, full, row], out_specs=[row, full],
        out_shape=[jax.ShapeDtypeStruct((T, D), BF16), jax.ShapeDtypeStruct((M, 2 * D), F32)],
        compiler_params=_params("arbitrary"),
    )(q, kv, do)


_HBM = pl.BlockSpec(memory_space=pltpu.HBM)


def _position():
    return lax.axis_index("x"), lax.axis_index("y"), lax.axis_index("c")


def _other_chips(x, y):
    return [(1 - x, y), (x, 1 - y), (1 - x, 1 - y)]


class _Exchange:
    def __init__(self, inputs, out_shapes, scratch, copies, inplace=False):
        self.inputs, self.out_shapes, self.scratch, self.copies, self.inplace = inputs, out_shapes, scratch, copies, inplace

    def start(self, in_refs, out_refs, sems):
        for cp in self.copies(in_refs, out_refs, sems, False)[0]:
            cp.start()

    def wait(self, in_refs, out_refs, sems):
        for cp, how in self.copies(in_refs, out_refs, sems, True)[1]:
            getattr(cp, how)()

    def aliases(self, first_input, first_output):
        return {first_input + w: first_output + w for w in range(len(self.inputs))} if self.inplace else {}


def _run_exchange(ex, name):
    n_in, n_out = len(ex.inputs), len(ex.out_shapes)

    def body(*refs):
        parts = refs[:n_in], refs[n_in:n_in + n_out], refs[n_in + n_out:]
        ex.start(*parts)
        ex.wait(*parts)

    return pl.pallas_call(
        body, name=name, in_specs=[_HBM] * n_in, out_specs=[_HBM] * n_out, out_shape=ex.out_shapes,
        input_output_aliases=ex.aliases(0, 0), scratch_shapes=ex.scratch,
    )(*ex.inputs)


def _chip_exchange(arrays, out_shapes, src_of, dst_of):
    n = len(arrays)

    def copies(srcs, outs, sems, waiting):
        send, recv, local = sems
        x, y, c = _position()
        q = 2 * x + y
        kept, sent, arriving = [], [], []
        for w, (s_ref, o_ref) in enumerate(zip(srcs, outs)):
            kept.append(pltpu.make_async_copy(src_of(s_ref, q, c), dst_of(o_ref, q, c), local.at[w]))
            for j, (px, py) in enumerate(_other_chips(x, y)):
                sems_j = dict(send_sem=send.at[3 * w + j], recv_sem=recv.at[3 * w + j], device_id=(px, py, c),
                              device_id_type=MESH)
                sent.append(pltpu.make_async_remote_copy(src_ref=src_of(s_ref, 2 * px + py, c),
                                                         dst_ref=dst_of(o_ref, q, c), **sems_j))
                if waiting:
                    arriving.append(pltpu.make_async_remote_copy(src_ref=src_of(s_ref, q, c),
                                                                 dst_ref=dst_of(o_ref, 2 * px + py, c), **sems_j))
        return kept + sent, ([(cp, "wait_recv") for cp in arriving] + [(cp, "wait_send") for cp in sent]
                             + [(cp, "wait") for cp in kept])

    scratch = [pltpu.SemaphoreType.DMA((3 * n,)), pltpu.SemaphoreType.DMA((3 * n,)), pltpu.SemaphoreType.DMA((n,))]
    return _Exchange(arrays, out_shapes, scratch, copies)


def _ex_ag_chips(blks):
    return _chip_exchange(blks, [jax.ShapeDtypeStruct((4, 2) + b.shape, b.dtype) for b in blks],
                          src_of=lambda r, chip, c: r, dst_of=lambda r, chip, c: r.at[chip, c])


def _ex_rs_chips(parts):
    return _chip_exchange(parts, [jax.ShapeDtypeStruct(h.shape, h.dtype) for h in parts],
                          src_of=lambda r, chip, c: r.at[chip], dst_of=lambda r, chip, c: r.at[chip])


def _ex_ag_sibling(arrs):
    n = len(arrs)

    def copies(ins, outs, sems, waiting):
        send, recv = sems
        x, y, c = _position()
        to = dict(device_id=(x, y, 1 - c), device_id_type=MESH)
        mine = [pltpu.make_async_remote_copy(src_ref=a.at[:, c], dst_ref=a.at[:, c], send_sem=send.at[w],
                                             recv_sem=recv.at[w], **to) for w, a in enumerate(outs)]
        theirs = [pltpu.make_async_remote_copy(src_ref=a.at[:, c], dst_ref=a.at[:, 1 - c], send_sem=send.at[w],
                                               recv_sem=recv.at[w], **to) for w, a in enumerate(outs if waiting else [])]
        return mine, [(cp, "wait_recv") for cp in theirs] + [(cp, "wait_send") for cp in mine]

    return _Exchange(arrs, [jax.ShapeDtypeStruct(a.shape, a.dtype) for a in arrs],
                     [pltpu.SemaphoreType.DMA((n,)), pltpu.SemaphoreType.DMA((n,))], copies, inplace=True)


def _ex_rs_sibling(blocks):
    n = len(blocks)

    def copies(srcs, outs, sems, waiting):
        send, recv = sems
        x, y, c = _position()
        cps = [pltpu.make_async_remote_copy(src_ref=b.at[:, 1 - c], dst_ref=l, send_sem=send.at[w], recv_sem=recv.at[w],
                                            device_id=(x, y, 1 - c), device_id_type=MESH)
               for w, (b, l) in enumerate(zip(srcs, outs))]
        return cps, [(cp, "wait") for cp in cps]

    return _Exchange(blocks, [jax.ShapeDtypeStruct((4,) + b.shape[2:], b.dtype) for b in blocks],
                     [pltpu.SemaphoreType.DMA((n,)), pltpu.SemaphoreType.DMA((n,))], copies)


def _row_tile(rows, pref=256):
    for t in range(min(pref, rows) // 16 * 16, 0, -16):
        if rows % t == 0:
            return t
    raise ValueError(f"no row tile for {rows}")


def _pair_add(blocks, landed, core, out_dtype, name):
    n, _, s0, s1 = blocks.shape
    tr = _row_tile(s0)

    def body(core_ref, a_ref, b_ref, o_ref):
        del core_ref
        o_ref[...] = (a_ref[...] + b_ref[...]).astype(o_ref.dtype)

    grid_spec = pltpu.PrefetchScalarGridSpec(
        num_scalar_prefetch=1, grid=(n, s0 // tr),
        in_specs=[pl.BlockSpec((1, None, tr, s1), lambda p, i, core: (p, core[0], i, 0)),
                  pl.BlockSpec((1, tr, s1), lambda p, i, core: (p, i, 0))],
        out_specs=pl.BlockSpec((1, tr, s1), lambda p, i, core: (p, i, 0)))
    return pl.pallas_call(
        body, name=name, grid_spec=grid_spec, out_shape=jax.ShapeDtypeStruct(landed.shape, out_dtype),
        compiler_params=_params("parallel", "parallel"),
    )(core, blocks, landed)


def _adamw_math(w, g, m, v):
    m = ADAM_B1 * m + (1.0 - ADAM_B1) * g
    v = ADAM_B2 * v + (1.0 - ADAM_B2) * (g * g)
    m_hat = m / (1.0 - ADAM_B1 ** ADAM_STEP)
    v_hat = v / (1.0 - ADAM_B2 ** ADAM_STEP)
    delta = -ADAM_LR * (m_hat / (jnp.sqrt(v_hat) + ADAM_EPS) + ADAM_WD * w)
    return delta, m, v


def _adamw_reduce(slots, w, m, v, name):
    n, R, C = slots.shape
    tr = _row_tile(R)

    def body(s_ref, w_ref, m_ref, v_ref, g_ref, d_ref, nm_ref, nv_ref):
        g = s_ref[0].astype(F32)
        for p in range(1, n):
            g = g + s_ref[p].astype(F32)
        g_ref[...] = g
        d_ref[...], nm_ref[...], nv_ref[...] = _adamw_math(w_ref[...], g, m_ref[...], v_ref[...])

    row = pl.BlockSpec((tr, C), lambda i: (i, 0))
    return pl.pallas_call(
        body, name=name, grid=(R // tr,), in_specs=[pl.BlockSpec((n, tr, C), lambda i: (0, i, 0)), row, row, row],
        out_specs=[row] * 4, out_shape=[jax.ShapeDtypeStruct((R, C), F32)] * 4, compiler_params=_params("parallel"),
    )(slots, w, m, v)


def _full_from_gathered(a, n):
    s0, s1 = a.shape[2:]
    blk = a.reshape(8, s0, s1)
    if n in COL_SHARDED:
        return blk.transpose(1, 0, 2).reshape(s0, 8 * s1)
    return blk.reshape(8 * s0, s1)


def _blocks_from_full(g, n, shard_shape):
    s0, s1 = shard_shape
    if n in COL_SHARDED:
        blk = g.reshape(s0, 8, s1).transpose(1, 0, 2)
    else:
        blk = g.reshape(8, s0, s1)
    return blk.reshape(4, 2, s0, s1)


def _swiglu_interleave(w):
    d, f2 = w.shape
    return w.reshape(d, 2, f2 // (2 * SWIGLU_TILE), SWIGLU_TILE).transpose(0, 2, 1, 3).reshape(d, f2)


def _swiglu_deinterleave(w):
    d, f2 = w.shape
    return w.reshape(d, f2 // (2 * SWIGLU_TILE), 2, SWIGLU_TILE).transpose(0, 2, 1, 3).reshape(d, f2)


SMALL_ROWS = 16


def _pack_small(vals, loss_row):
    rows = []
    for n in SMALL:
        flat = vals[n].reshape(-1)
        pad = (-flat.shape[0]) % PACK_COLS
        rows.append(jnp.pad(flat, (0, pad)).reshape(-1, PACK_COLS))
    rows.append(loss_row)
    out = jnp.concatenate(rows, axis=0)
    assert out.shape[0] == SMALL_ROWS, out.shape
    return out


def _unpack_small(packed, like):
    out, r = {}, 0
    for n in SMALL:
        size = like[n].size
        rows = -(-size // PACK_COLS)
        out[n] = packed[r:r + rows].reshape(-1)[:size].reshape(like[n].shape)
        r += rows
    return out


class _NoTraffic:
    def host(self, stage):
        return None

    def landed(self, stage, arrays):
        pass

    def grads_ready(self, names, gW):
        pass


def _mm_behind(traffic, stage, *args, **kwargs):
    ex = traffic.host(stage)
    if ex is None:
        return _mm(*args, **kwargs)
    out, arrays = _mm(*args, hosted=ex, **kwargs)
    traffic.landed(stage, arrays)
    return out


def _ffn_fwd(x, g_pre, W, tag, traffic, up_stage=None, down_stage=None):
    h = _rms_fwd(x, g_pre, f"{tag}_pre")
    ex = traffic.host(up_stage) if up_stage else None
    u, a, arrays = _mm_swiglu(h, W[f"{tag}_w_in"], f"{tag}_up", hosted=ex)
    if ex is not None:
        traffic.landed(up_stage, arrays)
    z = _mm_behind(traffic, down_stage, a, W[f"{tag}_w_down"], "nn", F32, f"{tag}_down", tk=1408)
    return h, u, a, z


def _ffn_bwd(saved, x, g_pre, w_in, w_down, g_post, dx_out, tag, traffic, up_dx_stage=None):
    h, u, a, z = saved
    dz, dg_post = _rms_bwd(z, g_post, dx_out, 0.5, f"{tag}_post_bwd", BF16)
    dw_down = _mm(a, dz, "tn", F32, f"{tag}_down_dw", tm=1408)
    du = _mm_swiglu_bwd(dz, w_down, u, f"{tag}_down_dx")
    dh = _mm_behind(traffic, up_dx_stage, du, w_in, "nt", BF16, f"{tag}_up_dx", tk=5632)
    dw_in = _mm(h, du, "tn", F32, f"{tag}_up_dw")
    dx, dg_pre = _rms_bwd(x, g_pre, dh, 1.0, f"{tag}_pre_bwd", F32, resid=dx_out)
    return dx, dg_pre, dg_post, dw_in, dw_down


def _step_local(x, mem, target, W, S, traffic=_NoTraffic()):
    T, D = x.shape
    gW, gS = {}, {}

    f1 = _ffn_fwd(x, S["ffn1_pre_g"], W, "ffn1", traffic, "gather_mixer_chips", "gather_mixer_sibling")
    x1 = _resid_rms(x, f1[3], S["ffn1_post_g"], 0.5, "ffn1_post")

    h2 = _rms_fwd(x1, S["mix_pre_g"], "mix_pre")
    pm = _mm_behind(traffic, "gather_late_chips", h2, W["w_main"], "nn", BF16, "mix_proj_main")
    pf = _mm(h2, W["w_f"], "nn", F32, "mix_proj_f")
    pg = _mm_behind(traffic, "gather_late_sibling", h2, W["w_gates"], "nn", F32, "mix_proj_gates")
    lbl = S["hg_lb_logits"].reshape(2, HEADS, 1, DH)
    o_a, states = _hgrn_fwd(pm, lbl, "hgrn_fwd")
    oan = _hgout_fwd(o_a, pm, S["hg_norm_g"], "hgrn_out")
    bias = jnp.pad(S["fox_f_bias"], ((0, 0), (0, LANES - HEADS)))
    c = _fox_cumsum(pf, bias, "fox_cumsum")
    c_heads = c[:, :HEADS].T
    c_col, c_row = c_heads[:, :, None], c_heads[:, None, :]
    o_b, lse = _fox_fwd(pm, c_col, c_row, "fox_fwd")
    ya = _mm(oan, W["w_branch_a"], "nn", F32, "branch_a")
    yb = _mm(o_b, W["w_branch_b"], "nn", F32, "branch_b")
    y = _merge_fwd(ya, yb, pg, S["b_gate"], "merge")
    z2 = _mm(y, W["w_out"], "nn", F32, "mix_out")
    x2 = _resid_rms(x1, z2, S["mix_post_g"], 1.0, "mix_post")

    h3 = _rms_fwd(x2, S["mem_pre_g"], "mem_pre")
    memn = _rms_fwd(mem, S["mem_kv_g"], "mem_kv_norm")
    qm = _mm(h3, W["w_mq"], "nn", BF16, "mem_q")
    kv = _mm(memn, W["w_mkv"], "nn", BF16, "mem_kv")
    om = _xattn_fwd(qm, kv, "mem_attn")
    z3 = _mm(om, W["w_mo"], "nn", F32, "mem_o")
    x3 = _resid_rms(x2, z3, S["mem_post_g"], 1.0, "mem_post")

    f2 = _ffn_fwd(x3, S["ffn2_pre_g"], W, "ffn2", traffic)
    dx4, sq = _final_loss(x3, f2[3], S["ffn2_post_g"], 0.5, target, "loss")

    dx3, gS["ffn2_pre_g"], gS["ffn2_post_g"], gW["ffn2_w_in"], gW["ffn2_w_down"] = _ffn_bwd(
        f2, x3, S["ffn2_pre_g"], W["ffn2_w_in"], W["ffn2_w_down"], S["ffn2_post_g"], dx4, "ffn2", traffic)
    traffic.grads_ready(["ffn2_w_in", "ffn2_w_down"], gW)

    dz3, gS["mem_post_g"] = _rms_bwd(z3, S["mem_post_g"], dx3, 1.0, "mem_post_bwd", BF16)
    dom = _mm(dz3, W["w_mo"], "nt", BF16, "mem_o_dx")
    gW["w_mo"] = _mm(om, dz3, "tn", F32, "mem_o_dw")
    dqm, dkv = _xattn_bwd(qm, kv, dom, "mem_attn_bwd")
    dh3 = _mm(dqm, W["w_mq"], "nt", BF16, "mem_q_dx")
    gW["w_mq"] = _mm(h3, dqm, "tn", F32, "mem_q_dw")
    dkvb = dkv.astype(BF16)
    gW["w_mkv"] = _mm(memn, dkvb, "tn", F32, "mem_kv_dw")
    dmemn = _mm(dkvb, W["w_mkv"], "nt", F32, "mem_kv_dx")
    _, gS["mem_kv_g"] = _rms_bwd(mem, S["mem_kv_g"], dmemn, 1.0, "mem_kv_norm_bwd", BF16)
    dx2, gS["mem_pre_g"] = _rms_bwd(x2, S["mem_pre_g"], dh3, 1.0, "mem_pre_bwd", F32, resid=dx3)

    dz2, gS["mix_post_g"] = _rms_bwd(z2, S["mix_post_g"], dx2, 1.0, "mix_post_bwd", BF16)
    dy = _mm(dz2, W["w_out"], "nt", BF16, "mix_out_dx")
    gW["w_out"] = _mm(y, dz2, "tn", F32, "mix_out_dw")
    dya, dyb, dpg, gS["b_gate"] = _merge_bwd(dy, ya, yb, pg, S["b_gate"], "merge_bwd")
    doan = _mm(dya, W["w_branch_a"], "nt", BF16, "branch_a_dx")
    gW["w_branch_a"] = _mm(oan, dya, "tn", F32, "branch_a_dw")
    dob = _mm(dyb, W["w_branch_b"], "nt", BF16, "branch_b_dx")
    gW["w_branch_b"] = _mm(o_b, dyb, "tn", F32, "branch_b_dw")
    traffic.grads_ready(["w_mo", "w_mq", "w_mkv", "w_out", "w_branch_a", "w_branch_b"], gW)

    delta = _fox_delta(dob, o_b, "fox_delta")
    dq_b, dk_b, dv_b, ds_rows, ds_cols = _fox_bwd(pm, c_col, c_row, dob, lse, delta, "fox_bwd")
    dc = jnp.pad((ds_rows - ds_cols).reshape(HEADS, T).T, ((0, 0), (0, LANES - HEADS)))
    dpf, dbias = _fox_dcum(dc, pf, bias, "fox_cumsum_bwd")
    gS["fox_f_bias"] = dbias[:, :HEADS]

    do_a, dg_a, gS["hg_norm_g"] = _hgout_bwd(o_a, pm, S["hg_norm_g"], doan, "hgrn_out_bwd")
    dq_a, df_a, di_a, dlbl = _hgrn_bwd(pm, lbl, states, do_a, "hgrn_bwd")
    gS["hg_lb_logits"] = dlbl.reshape(2, HEADS, DH)

    dpm = jnp.concatenate([dq_a, df_a, di_a, dg_a, dq_b, dk_b, dv_b], axis=1)
    dpf16 = dpf.astype(BF16)
    dh2 = _mm_behind(traffic, "scatter_ffn2_chips", dpm, W["w_main"], "nt", F32, "mix_proj_main_dx")
    dh2 = _mm(dpg, W["w_gates"], "nt", F32, "mix_proj_gates_dx", add=dh2)
    dh2 = _mm(dpf16, W["w_f"], "nt", F32, "mix_proj_f_dx", add=dh2)
    gW["w_main"] = _mm_behind(traffic, "scatter_mid_chips", h2, dpm, "tn", F32, "mix_proj_main_dw")
    gW["w_gates"] = _mm(h2, dpg, "tn", F32, "mix_proj_gates_dw")
    gW["w_f"] = _mm(h2, dpf16, "tn", F32, "mix_proj_f_dw")
    traffic.grads_ready(["w_in"], gW)
    dx1, gS["mix_pre_g"] = _rms_bwd(x1, S["mix_pre_g"], dh2, 1.0, "mix_pre_bwd", F32, resid=dx2)

    dx0, gS["ffn1_pre_g"], gS["ffn1_post_g"], gW["ffn1_w_in"], gW["ffn1_w_down"] = _ffn_bwd(
        f1, x, S["ffn1_pre_g"], W["ffn1_w_in"], W["ffn1_w_down"], S["ffn1_post_g"], dx1, "ffn1", traffic,
        "scatter_w_in_chips")
    traffic.grads_ready(["ffn1_w_in", "ffn1_w_down"], gW)
    return sq, dx0, gW, gS


GATHER_FIRST = ["ffn1_w_in", "ffn1_w_down"]
GATHER_MIXER = ["w_in", "w_branch_a", "w_branch_b", "w_out"]
GATHER_LATE = ["w_mq", "w_mkv", "w_mo", "ffn2_w_in", "ffn2_w_down"]
SCATTER_BEHIND = {
    "scatter_ffn2_chips": ["ffn2_w_in", "ffn2_w_down"],
    "scatter_mid_chips": ["w_mo", "w_mq", "w_mkv", "w_out", "w_branch_a", "w_branch_b"],
    "scatter_w_in_chips": ["w_in"],
}


class _Traffic:
    def __init__(self, sent, W, shapes, core, D):
        self.sent, self.W, self.shapes, self.core, self.D = sent, W, shapes, core, D
        self.half, self.pairs, self.slots = {}, {}, {}

    def install(self, names, gathered):
        D = self.D
        for n, g in zip(names, gathered):
            full = _full_from_gathered(g, n)
            if n == "w_in":
                self.W["w_main"] = full[:, :7 * D]
                self.W["w_f"] = jnp.pad(full[:, 7 * D:7 * D + HEADS], ((0, 0), (0, LANES - HEADS)))
                self.W["w_gates"] = full[:, 7 * D + HEADS:]
            elif n in ("ffn1_w_in", "ffn2_w_in"):
                self.W[n] = _swiglu_interleave(full)
            else:
                self.W[n] = full

    def host(self, stage):
        if stage == "gather_mixer_chips":
            return _ex_ag_chips([self.sent[n] for n in GATHER_MIXER])
        if stage == "gather_late_chips":
            return _ex_ag_chips([self.sent[n] for n in GATHER_LATE])
        if stage in ("gather_mixer_sibling", "gather_late_sibling"):
            return _ex_ag_sibling(self.half[stage])
        if stage in SCATTER_BEHIND:
            return _ex_rs_chips([self.pairs[n] for n in SCATTER_BEHIND[stage]])
        return None

    def landed(self, stage, arrays):
        if stage == "gather_mixer_chips":
            self.half["gather_mixer_sibling"] = arrays
        elif stage == "gather_late_chips":
            self.half["gather_late_sibling"] = arrays
        elif stage == "gather_mixer_sibling":
            self.install(GATHER_MIXER, arrays)
        elif stage == "gather_late_sibling":
            self.install(GATHER_LATE, arrays)
        else:
            self.slots.update(zip(SCATTER_BEHIND[stage], arrays))

    def _final_grad(self, n, gW):
        D = self.D
        if n == "w_in":
            g = gW["w_main"]
            return jnp.concatenate([g[:, :4 * D], g[:, 4 * D:5 * D] * (1.0 / math.sqrt(DH)), g[:, 5 * D:],
                                    gW["w_f"][:, :HEADS], gW["w_gates"]], axis=1)
        if n in ("ffn1_w_in", "ffn2_w_in"):
            return _swiglu_deinterleave(gW[n])
        return gW[n]

    def grads_ready(self, names, gW):
        blocks = [_blocks_from_full(self._final_grad(n, gW), n, self.shapes[n]) for n in names]
        got = _run_exchange(_ex_rs_sibling(blocks), f"rs_sibling_{names[0]}")
        for n, b, l in zip(names, blocks, got):
            self.pairs[n] = _pair_add(b, l, self.core, BF16, f"rs_pair_add_{n}")

    def finish(self):
        rest = [n for n in BIG if n not in self.slots]
        got = _run_exchange(_ex_rs_chips([self.pairs[n] for n in rest]), "rs_chips_last")
        self.slots.update(zip(rest, got))
        return self.slots


def _train_step(a):
    c_idx = lax.axis_index("c")
    x, mem, target = a["x"][0], a["mem"][0], a["loss_target"][0]
    D = x.shape[1]
    shards = {n: a[n][0] for n in BIG}

    fox_scale = 1.0 / math.sqrt(DH)
    n_mine = shards["w_in"].shape[1]
    dev = 4 * lax.axis_index("x") + 2 * lax.axis_index("y") + c_idx
    cols = dev * n_mine + jnp.arange(n_mine)
    is_fox_q = (cols >= 4 * D) & (cols < 5 * D)
    sent = dict(shards, w_in=shards["w_in"] * jnp.where(is_fox_q, fox_scale, 1.0)[None, :])
    sent = {n: v.astype(BF16) for n, v in sent.items()}
    W = {}
    traffic = _Traffic(sent, W, {n: shards[n].shape for n in BIG}, c_idx.astype(jnp.int32).reshape(1), D)
    first = _run_exchange(_ex_ag_chips([sent[n] for n in GATHER_FIRST]), "ag_first_chips")
    traffic.install(GATHER_FIRST, _run_exchange(_ex_ag_sibling(first), "ag_first_sibling"))
    S = {n: a[n] for n in SMALL}

    sq, grad_x, gW, gS = _step_local(x, mem, target, W, S, traffic)
    slots = traffic.finish()
    big = {n: _adamw_reduce(slots[n], shards[n], a["m_" + n][0], a["v_" + n][0], f"adamw_{n}") for n in BIG}

    loss_row = jnp.pad(sq[:1, :1] * (0.5 / D), ((0, 0), (0, PACK_COLS - 1)))
    small_half = _run_exchange(_ex_ag_chips([_pack_small(gS, loss_row)]), "small_ag_chips")
    small_all = _run_exchange(_ex_ag_sibling(small_half), "small_ag_sibling")[0]
    small_slots = small_all.reshape(8, SMALL_ROWS, PACK_COLS)
    zero_row = jnp.zeros((1, PACK_COLS), F32)
    g_sm, d_sm, m_sm, v_sm = _adamw_reduce(
        small_slots, _pack_small({n: a[n] for n in SMALL}, zero_row),
        _pack_small({n: a["m_" + n] for n in SMALL}, zero_row),
        _pack_small({n: a["v_" + n] for n in SMALL}, zero_row), "adamw_small")

    def unpack(which, small):
        out = _unpack_small(small, {n: a[n] for n in SMALL})
        for n in BIG:
            out[n] = big[n][which][None]
        return [out[n] for n in WEIGHTS]

    loss = g_sm[SMALL_ROWS - 1, 0]
    return (loss, grad_x[None], *unpack(0, g_sm), *unpack(1, d_sm), *unpack(2, m_sm), *unpack(3, v_sm))


def kernel(x, mem, ffn1_pre_g, ffn1_w_in, ffn1_w_down, ffn1_post_g, mix_pre_g, w_in, hg_lb_logits, hg_norm_g, fox_f_bias, w_branch_a, w_branch_b, b_gate, w_out, mix_post_g, mem_pre_g, mem_kv_g, w_mq, w_mkv, w_mo, mem_post_g, ffn2_pre_g, ffn2_w_in, ffn2_w_down, ffn2_post_g, loss_target, m_ffn1_pre_g, m_ffn1_w_in, m_ffn1_w_down, m_ffn1_post_g, m_mix_pre_g, m_w_in, m_hg_lb_logits, m_hg_norm_g, m_fox_f_bias, m_w_branch_a, m_w_branch_b, m_b_gate, m_w_out, m_mix_post_g, m_mem_pre_g, m_mem_kv_g, m_w_mq, m_w_mkv, m_w_mo, m_mem_post_g, m_ffn2_pre_g, m_ffn2_w_in, m_ffn2_w_down, m_ffn2_post_g, v_ffn1_pre_g, v_ffn1_w_in, v_ffn1_w_down, v_ffn1_post_g, v_mix_pre_g, v_w_in, v_hg_lb_logits, v_hg_norm_g, v_fox_f_bias, v_w_branch_a, v_w_branch_b, v_b_gate, v_w_out, v_mix_post_g, v_mem_pre_g, v_mem_kv_g, v_w_mq, v_w_mkv, v_w_mo, v_mem_post_g, v_ffn2_pre_g, v_ffn2_w_in, v_ffn2_w_down, v_ffn2_post_g):
    return _train_step(dict(locals()))
```

```python
import functools
import math

import jax
import jax.numpy as jnp
from jax import lax
from jax.experimental import pallas as pl
from jax.experimental.pallas import tpu as pltpu

F32 = jnp.float32
BF16 = jnp.bfloat16
MESH = pl.DeviceIdType.MESH

EPS = 1e-6
HEADS = 8
DH = 128
MEM_HEADS = 4
CHUNK = 128
HALF = CHUNK // 2
SWIGLU_TILE = 256
LANES = 128
PACK_COLS = 1024
ROW_TILE = 512
SEQ_BLOCK = 2048
CUMSUM_BLOCK = 512
XATTN_TILE = 2048
ATTN_TILE = 2048
ATTN_ROWS = 256
EXP_CLAMP = 80.0
NEG_BIG = -1e30

ADAM_LR, ADAM_B1, ADAM_B2, ADAM_EPS, ADAM_WD, ADAM_STEP = 0.001, 0.9, 0.999, 1e-08, 0.01, 10

VMEM_LIMIT = 48 * 1024 * 1024

_DN = {
    "nn": (((1,), (0,)), ((), ())),
    "nt": (((1,), (1,)), ((), ())),
    "tn": (((0,), (0,)), ((), ())),
}

BIG = ["ffn1_w_in", "ffn1_w_down", "w_in", "w_branch_a", "w_branch_b", "w_out", "w_mq", "w_mkv", "w_mo",
       "ffn2_w_in", "ffn2_w_down"]
COL_SHARDED = {"ffn1_w_in", "w_in", "w_mkv", "ffn2_w_in"}
SMALL = ["ffn1_pre_g", "ffn1_post_g", "mix_pre_g", "hg_lb_logits", "hg_norm_g", "fox_f_bias", "b_gate",
         "mix_post_g", "mem_pre_g", "mem_kv_g", "mem_post_g", "ffn2_pre_g", "ffn2_post_g"]
WEIGHTS = ["ffn1_pre_g", "ffn1_w_in", "ffn1_w_down", "ffn1_post_g", "mix_pre_g", "w_in", "hg_lb_logits",
           "hg_norm_g", "fox_f_bias", "w_branch_a", "w_branch_b", "b_gate", "w_out", "mix_post_g", "mem_pre_g",
           "mem_kv_g", "w_mq", "w_mkv", "w_mo", "mem_post_g", "ffn2_pre_g", "ffn2_w_in", "ffn2_w_down",
           "ffn2_post_g"]


def _dot(a, b, mode="nn"):
    return lax.dot_general(a, b, _DN[mode], preferred_element_type=F32)


def _sig(x):
    return 1.0 / (1.0 + jnp.exp(-x))


def _sig_approx(x):
    return pl.reciprocal(1.0 + jnp.exp(-x), approx=True)


def _params(*dims):
    return pltpu.CompilerParams(dimension_semantics=dims if dims else None, vmem_limit_bytes=VMEM_LIMIT)


def _tile(dim, pref):
    if dim <= pref:
        return dim
    t = (pref // LANES) * LANES
    while t >= LANES:
        if dim % t == 0:
            return t
        t -= LANES
    raise ValueError(f"no tile for {dim}")


def _colsum(x):
    return jnp.sum(x, axis=0, keepdims=True)


def _rowsum(x):
    return jnp.sum(x, axis=1, keepdims=True)


def _iota(shape, axis):
    return lax.broadcasted_iota(jnp.int32, shape, axis)


def _pick_row(x, r):
    return _colsum(jnp.where(_iota(x.shape, 0) == r, x, 0.0))


def _tri_dot(tri, x):
    hi = x.astype(BF16)
    r1 = x - hi.astype(F32)
    mid = r1.astype(BF16)
    lo = (r1 - mid.astype(F32)).astype(BF16)
    return _dot(tri, hi) + _dot(tri, mid) + _dot(tri, lo)


_MM_TILES = {"nn": (2048, 512, 1024), "nt": (512, 1024, 4096), "tn": (1024, 1024, 2048)}


def _host_call(body, name, grid, in_specs, out_specs, out_shape, scratch_shapes, dims, args, hosted=None):
    if hosted is None:
        results = pl.pallas_call(body, name=name, grid=grid, in_specs=in_specs, out_specs=out_specs, out_shape=out_shape,
                                 scratch_shapes=scratch_shapes, compiler_params=_params(*dims))(*args)
        return list(results), []
    n_in, n_out, n_sc = len(in_specs), len(out_specs), len(scratch_shapes)
    h_in, h_out = len(hosted.inputs), len(hosted.out_shapes)

    def wrapped(*refs):
        cut = [n_in, h_in, n_out, h_out, n_sc]
        at = [sum(cut[:i]) for i in range(len(cut) + 1)]
        ins, hin, outs, hout, scr = (refs[at[i]:at[i + 1]] for i in range(len(cut)))
        hsems = refs[at[-1]:]
        ids = [pl.program_id(d) for d in range(len(grid))]
        first = functools.reduce(jnp.logical_and, [i == 0 for i in ids])
        last = functools.reduce(jnp.logical_and, [i == g - 1 for i, g in zip(ids, grid)])

        @pl.when(first)
        def _():
            hosted.start(hin, hout, hsems)

        body(*ins, *outs, *scr)

        @pl.when(last)
        def _():
            hosted.wait(hin, hout, hsems)

    results = pl.pallas_call(
        wrapped, name=name, grid=grid, in_specs=list(in_specs) + [_HBM] * h_in,
        out_specs=list(out_specs) + [_HBM] * h_out, out_shape=list(out_shape) + list(hosted.out_shapes),
        scratch_shapes=list(scratch_shapes) + list(hosted.scratch), input_output_aliases=hosted.aliases(n_in, n_out),
        compiler_params=_params(*dims))(*args, *hosted.inputs)
    return list(results[:n_out]), list(results[n_out:])


def _mm(a, b, mode, out_dtype, name, add=None, tm=None, tn=None, tk=None, hosted=None):
    tm, tn, tk = (given or pref for given, pref in zip((tm, tn, tk), _MM_TILES[mode]))
    if mode == "nn":
        (M, K), (K2, N) = a.shape, b.shape
    elif mode == "nt":
        (M, K), (N, K2) = a.shape, b.shape
    else:
        (K, M), (K2, N) = a.shape, b.shape
    assert K == K2, (name, a.shape, b.shape)
    tm, tn, tk = _tile(M, tm), _tile(N, tn), _tile(K, tk)
    nk = K // tk
    if mode == "tn":
        a_spec = pl.BlockSpec((tk, tm), lambda i, j, k: (k, i))
    else:
        a_spec = pl.BlockSpec((tm, tk), lambda i, j, k: (i, k))
    if mode == "nt":
        b_spec = pl.BlockSpec((tn, tk), lambda i, j, k: (j, k))
    else:
        b_spec = pl.BlockSpec((tk, tn), lambda i, j, k: (k, j))
    o_spec = pl.BlockSpec((tm, tn), lambda i, j, k: (i, j))
    has_add = add is not None

    def body(*refs):
        a_ref, b_ref = refs[0], refs[1]
        c_ref = refs[2] if has_add else None
        o_ref = refs[3] if has_add else refs[2]
        part = _dot(a_ref[...], b_ref[...], mode)
        if nk == 1:
            if has_add:
                part = part + c_ref[...]
            o_ref[...] = part.astype(o_ref.dtype)
            return
        acc_ref = refs[-1]
        k = pl.program_id(2)

        @pl.when(k == 0)
        def _():
            acc_ref[...] = part + c_ref[...] if has_add else part

        @pl.when(k > 0)
        def _():
            acc_ref[...] += part

        @pl.when(k == nk - 1)
        def _():
            o_ref[...] = acc_ref[...].astype(o_ref.dtype)

    in_specs = [a_spec, b_spec] + ([o_spec] if has_add else [])
    args = (a, b) + ((add,) if has_add else ())
    (out,), landed = _host_call(
        body, name, (M // tm, N // tn, nk), in_specs, [o_spec], [jax.ShapeDtypeStruct((M, N), out_dtype)],
        [pltpu.VMEM((tm, tn), F32)] if nk > 1 else [], ("parallel", "parallel", "arbitrary"), args, hosted)
    return out if hosted is None else (out, landed)


def _rms_fwd(x, g, name, out_dtype=BF16):
    T, D = x.shape
    tr = _tile(T, ROW_TILE)

    def body(x_ref, g_ref, o_ref):
        xv = x_ref[...]
        r = lax.rsqrt(jnp.mean(xv * xv, axis=-1, keepdims=True) + EPS)
        o_ref[...] = (xv * r * g_ref[...]).astype(o_ref.dtype)

    return pl.pallas_call(
        body, name=name, grid=(T // tr,),
        in_specs=[pl.BlockSpec((tr, D), lambda i: (i, 0)), pl.BlockSpec((1, D), lambda i: (0, 0))],
        out_specs=pl.BlockSpec((tr, D), lambda i: (i, 0)),
        out_shape=jax.ShapeDtypeStruct((T, D), out_dtype), compiler_params=_params("parallel"),
    )(x, g)


def _resid_rms(x, z, g, scale, name):
    T, D = x.shape
    tr = _tile(T, ROW_TILE)

    def body(x_ref, z_ref, g_ref, o_ref):
        zv = z_ref[...]
        r = lax.rsqrt(jnp.mean(zv * zv, axis=-1, keepdims=True) + EPS)
        o_ref[...] = x_ref[...] + scale * (zv * r * g_ref[...])

    row = pl.BlockSpec((tr, D), lambda i: (i, 0))
    return pl.pallas_call(
        body, name=name, grid=(T // tr,), in_specs=[row, row, pl.BlockSpec((1, D), lambda i: (0, 0))],
        out_specs=row, out_shape=jax.ShapeDtypeStruct((T, D), F32), compiler_params=_params("parallel"),
    )(x, z, g)


def _final_loss(x, z, g, scale, target, name):
    T, D = x.shape
    tr = _tile(T, ROW_TILE)

    def body(x_ref, z_ref, g_ref, t_ref, dx_ref, acc_ref):
        @pl.when(pl.program_id(0) == 0)
        def _():
            acc_ref[...] = jnp.zeros_like(acc_ref)

        zv = z_ref[...]
        r = lax.rsqrt(jnp.mean(zv * zv, axis=-1, keepdims=True) + EPS)
        e = x_ref[...] + scale * (zv * r * g_ref[...]) - t_ref[...]
        dx_ref[...] = e * (1.0 / D)
        acc_ref[...] += _colsum(_rowsum(e * e))

    row = pl.BlockSpec((tr, D), lambda i: (i, 0))
    return pl.pallas_call(
        body, name=name, grid=(T // tr,), in_specs=[row, row, pl.BlockSpec((1, D), lambda i: (0, 0)), row],
        out_specs=[row, pl.BlockSpec((8, LANES), lambda i: (0, 0))],
        out_shape=[jax.ShapeDtypeStruct((T, D), F32), jax.ShapeDtypeStruct((8, LANES), F32)],
        compiler_params=_params("arbitrary"),
    )(x, z, g, target)


def _rms_bwd(xin, g, dy, scale, name, out_dtype, resid=None):
    T, D = xin.shape
    tr = _tile(T, ROW_TILE)
    has_resid = resid is not None

    def body(*refs):
        x_ref, g_ref, dy_ref = refs[:3]
        r_ref = refs[3] if has_resid else None
        dx_ref, dg_ref = refs[-2], refs[-1]

        @pl.when(pl.program_id(0) == 0)
        def _():
            dg_ref[...] = jnp.zeros_like(dg_ref)

        xv = x_ref[...]
        r = lax.rsqrt(jnp.mean(xv * xv, axis=-1, keepdims=True) + EPS)
        xh = xv * r
        dyv = dy_ref[...].astype(F32) * scale
        dxh = dyv * g_ref[...]
        dx = r * (dxh - xh * jnp.mean(dxh * xh, axis=-1, keepdims=True))
        if has_resid:
            dx = dx + r_ref[...]
        dx_ref[...] = dx.astype(dx_ref.dtype)
        dg_ref[...] += _colsum(dyv * xh)

    row = pl.BlockSpec((tr, D), lambda i: (i, 0))
    vec = pl.BlockSpec((1, D), lambda i: (0, 0))
    return pl.pallas_call(
        body, name=name, grid=(T // tr,), in_specs=[row, vec, row] + ([row] if has_resid else []),
        out_specs=[row, vec],
        out_shape=[jax.ShapeDtypeStruct((T, D), out_dtype), jax.ShapeDtypeStruct((1, D), F32)],
        compiler_params=_params("arbitrary"),
    )(*((xin, g, dy) + ((resid,) if has_resid else ())))


def _mm_swiglu(h, w_in, name, hosted=None):
    T, K = h.shape
    F2 = w_in.shape[1]
    tf = SWIGLU_TILE
    tm = _tile(T, _MM_TILES["nn"][0])

    def body(h_ref, w_ref, u_ref, a_ref):
        u = _dot(h_ref[...], w_ref[...])
        u_ref[...] = u.astype(u_ref.dtype)
        gate, up = u[:, :tf], u[:, tf:]
        a_ref[...] = (gate * _sig_approx(gate) * up).astype(a_ref.dtype)

    (u, a), landed = _host_call(
        body, name, (T // tm, F2 // (2 * tf)),
        [pl.BlockSpec((tm, K), lambda i, j: (i, 0)), pl.BlockSpec((K, 2 * tf), lambda i, j: (0, j))],
        [pl.BlockSpec((tm, 2 * tf), lambda i, j: (i, j)), pl.BlockSpec((tm, tf), lambda i, j: (i, j))],
        [jax.ShapeDtypeStruct((T, F2), BF16), jax.ShapeDtypeStruct((T, F2 // 2), BF16)], [],
        ("parallel", "parallel"), (h, w_in), hosted)
    return u, a, landed


def _mm_swiglu_bwd(dz, w_down, u, name):
    T, D = dz.shape
    F = w_down.shape[0]
    tf = SWIGLU_TILE
    tm = _tile(T, _MM_TILES["nn"][0])

    def body(dz_ref, w_ref, u_ref, o_ref):
        d = _dot(dz_ref[...], w_ref[...], "nt")
        gate = u_ref[:, :tf].astype(F32)
        up = u_ref[:, tf:].astype(F32)
        s = _sig_approx(gate)
        o_ref[:, :tf] = (d * up * (s * (1.0 + gate * (1.0 - s)))).astype(o_ref.dtype)
        o_ref[:, tf:] = (d * gate * s).astype(o_ref.dtype)

    return pl.pallas_call(
        body, name=name, grid=(T // tm, F // tf),
        in_specs=[pl.BlockSpec((tm, D), lambda i, j: (i, 0)), pl.BlockSpec((tf, D), lambda i, j: (j, 0)),
                  pl.BlockSpec((tm, 2 * tf), lambda i, j: (i, j))],
        out_specs=pl.BlockSpec((tm, 2 * tf), lambda i, j: (i, j)),
        out_shape=jax.ShapeDtypeStruct((T, 2 * F), BF16), compiler_params=_params("parallel", "parallel"),
    )(dz, w_down, u)


def _hgout_fwd(o_a, pm, g, name):
    T, D = o_a.shape
    tr = _tile(T, ROW_TILE)

    def body(o_ref, ga_ref, g_ref, out_ref):
        ov = o_ref[...]
        r = lax.rsqrt(jnp.mean(ov * ov, axis=-1, keepdims=True) + EPS)
        ga = ga_ref[...].astype(F32)
        out_ref[...] = (ov * r * g_ref[...] * (ga * _sig(ga))).astype(out_ref.dtype)

    row = pl.BlockSpec((tr, D), lambda i: (i, 0))
    return pl.pallas_call(
        body, name=name, grid=(T // tr,),
        in_specs=[row, pl.BlockSpec((tr, D), lambda i: (i, 3)), pl.BlockSpec((1, D), lambda i: (0, 0))],
        out_specs=row, out_shape=jax.ShapeDtypeStruct((T, D), BF16), compiler_params=_params("parallel"),
    )(o_a, pm, g)


def _hgout_bwd(o_a, pm, g, d_out, name):
    T, D = o_a.shape
    tr = _tile(T, ROW_TILE)

    def body(o_ref, ga_ref, g_ref, d_ref, do_ref, dga_ref, dg_ref):
        @pl.when(pl.program_id(0) == 0)
        def _():
            dg_ref[...] = jnp.zeros_like(dg_ref)

        ov = o_ref[...]
        r = lax.rsqrt(jnp.mean(ov * ov, axis=-1, keepdims=True) + EPS)
        oh = ov * r
        ga = ga_ref[...].astype(F32)
        s = _sig(ga)
        d = d_ref[...].astype(F32)
        dn = d * (ga * s)
        dga_ref[...] = (d * (oh * g_ref[...]) * (s * (1.0 + ga * (1.0 - s)))).astype(dga_ref.dtype)
        dxh = dn * g_ref[...]
        do_ref[...] = (r * (dxh - oh * jnp.mean(dxh * oh, axis=-1, keepdims=True))).astype(do_ref.dtype)
        dg_ref[...] += _colsum(dn * oh)

    row = pl.BlockSpec((tr, D), lambda i: (i, 0))
    vec = pl.BlockSpec((1, D), lambda i: (0, 0))
    return pl.pallas_call(
        body, name=name, grid=(T // tr,), in_specs=[row, pl.BlockSpec((tr, D), lambda i: (i, 3)), vec, row],
        out_specs=[row, row, vec],
        out_shape=[jax.ShapeDtypeStruct((T, D), BF16), jax.ShapeDtypeStruct((T, D), BF16),
                   jax.ShapeDtypeStruct((1, D), F32)],
        compiler_params=_params("arbitrary"),
    )(o_a, pm, g, d_out)


def _merge_fwd(ya, yb, pg, bg, name):
    T, D = ya.shape
    tr = _tile(T, 256)

    def body(ya_ref, yb_ref, pg_ref, bg_ref, o_ref):
        g0 = _sig(pg_ref[:, :D].astype(F32) + bg_ref[:, :D])
        g1 = _sig(pg_ref[:, D:].astype(F32) + bg_ref[:, D:])
        o_ref[...] = (g0 * ya_ref[...].astype(F32) + g1 * yb_ref[...].astype(F32)).astype(o_ref.dtype)

    row = pl.BlockSpec((tr, D), lambda i: (i, 0))
    return pl.pallas_call(
        body, name=name, grid=(T // tr,),
        in_specs=[row, row, pl.BlockSpec((tr, 2 * D), lambda i: (i, 0)), pl.BlockSpec((1, 2 * D), lambda i: (0, 0))],
        out_specs=row, out_shape=jax.ShapeDtypeStruct((T, D), BF16), compiler_params=_params("parallel"),
    )(ya, yb, pg, bg)


def _merge_bwd(dy, ya, yb, pg, bg, name):
    T, D = ya.shape
    tr = _tile(T, 256)

    def body(dy_ref, ya_ref, yb_ref, pg_ref, bg_ref, dya_ref, dyb_ref, dpg_ref, dbg_ref):
        @pl.when(pl.program_id(0) == 0)
        def _():
            dbg_ref[...] = jnp.zeros_like(dbg_ref)

        d = dy_ref[...].astype(F32)
        g0 = _sig(pg_ref[:, :D].astype(F32) + bg_ref[:, :D])
        g1 = _sig(pg_ref[:, D:].astype(F32) + bg_ref[:, D:])
        dya_ref[...] = (d * g0).astype(dya_ref.dtype)
        dyb_ref[...] = (d * g1).astype(dyb_ref.dtype)
        dg0 = d * ya_ref[...].astype(F32) * (g0 * (1.0 - g0))
        dg1 = d * yb_ref[...].astype(F32) * (g1 * (1.0 - g1))
        dpg_ref[:, :D] = dg0.astype(dpg_ref.dtype)
        dpg_ref[:, D:] = dg1.astype(dpg_ref.dtype)
        dbg_ref[:, :D] += _colsum(dg0)
        dbg_ref[:, D:] += _colsum(dg1)

    row = pl.BlockSpec((tr, D), lambda i: (i, 0))
    wide = pl.BlockSpec((tr, 2 * D), lambda i: (i, 0))
    wvec = pl.BlockSpec((1, 2 * D), lambda i: (0, 0))
    return pl.pallas_call(
        body, name=name, grid=(T // tr,), in_specs=[row, row, row, wide, wvec],
        out_specs=[row, row, wide, wvec],
        out_shape=[jax.ShapeDtypeStruct((T, D), BF16), jax.ShapeDtypeStruct((T, D), BF16),
                   jax.ShapeDtypeStruct((T, 2 * D), BF16), jax.ShapeDtypeStruct((1, 2 * D), F32)],
        compiler_params=_params("arbitrary"),
    )(dy, ya, yb, pg, bg)


def _hgrn_chunk_terms(q, fl, lb, tri):
    shape = q.shape
    row = _iota(shape, 0)
    sg = _sig(fl)
    f = lb + (1.0 - lb) * sg
    k = 1.0 - f
    b = _tri_dot(tri, jnp.log(f))
    ref1 = jnp.where(row < HALF, _pick_row(b, HALF // 2), _pick_row(b, HALF + HALF // 2))
    b_half = _pick_row(b, HALF - 1)
    b_last = _pick_row(b, CHUNK - 1)
    sq = _sig(q)
    qs = q * sq
    e_q1 = jnp.exp(jnp.minimum(b - ref1, EXP_CLAMP))
    e_k1 = jnp.exp(jnp.minimum(ref1 - b, EXP_CLAMP))
    e_q2 = jnp.exp(jnp.minimum(b - b_half, 0.0))
    e_k2 = jnp.exp(jnp.minimum(b_half - b, 0.0))
    e_b = jnp.exp(b)
    e_kd = jnp.exp(b_last - b)
    return dict(sg=sg, f=f, k=k, sq=sq, qs=qs, e_q1=e_q1, e_k1=e_k1, e_q2=e_q2, e_k2=e_k2, e_b=e_b, e_kd=e_kd,
                e_last=jnp.exp(b_last))


def _hgrn_masks():
    r = _iota((CHUNK, CHUNK), 0)
    c = _iota((CHUNK, CHUNK), 1)
    causal = r >= c
    same = (r < HALF) == (c < HALF)
    return causal, causal & same, (r >= HALF) & (c < HALF)


def _softmax_lb(lbl_ref):
    l0, l1 = lbl_ref[0, 0], lbl_ref[1, 0]
    mx = jnp.maximum(l0, l1)
    e0, e1 = jnp.exp(l0 - mx), jnp.exp(l1 - mx)
    return e0 / (e0 + e1)


def _hgrn_fwd(pm, lbl, name):
    T = pm.shape[0]
    tb = _tile(T, SEQ_BLOCK)
    nc = tb // CHUNK

    def body(q_ref, f_ref, i_ref, lbl_ref, o_ref, st_ref, s_sc):
        @pl.when(pl.program_id(1) == 0)
        def _():
            s_sc[...] = jnp.zeros_like(s_sc)

        lb = _softmax_lb(lbl_ref)
        causal, m1, m2 = _hgrn_masks()
        tri = jnp.where(causal, 1.0, 0.0).astype(BF16)
        parts = []
        for ci in range(nc):
            sl = pl.ds(ci * CHUNK, CHUNK)
            t = _hgrn_chunk_terms(q_ref[sl, :].astype(F32), f_ref[sl, :].astype(F32), lb, tri)
            iv = i_ref[sl, :]
            a1 = _dot((t["qs"] * t["e_q1"]).astype(BF16), (t["k"] * t["e_k1"]).astype(BF16), "nt")
            a2 = _dot((t["qs"] * t["e_q2"]).astype(BF16), (t["k"] * t["e_k2"]).astype(BF16), "nt")
            a = jnp.where(m1, a1, 0.0) + jnp.where(m2, a2, 0.0)
            parts.append((_dot(a.astype(BF16), iv), (t["qs"] * t["e_b"]).astype(BF16),
                          _dot(iv, (t["k"] * t["e_kd"]).astype(BF16), "tn"), t["e_last"]))
        st = s_sc[...]
        for ci, (o_intra, qi, grow, e_last) in enumerate(parts):
            st_ref[0, ci] = st
            o_ref[pl.ds(ci * CHUNK, CHUNK), :] = o_intra + _dot(qi, st.astype(BF16), "nt")
            st = e_last * st + grow
        s_sc[...] = st

    blk = lambda off: pl.BlockSpec((tb, DH), lambda h, b: (b, off + h))
    return pl.pallas_call(
        body, name=name, grid=(HEADS, T // tb),
        in_specs=[blk(0), blk(HEADS), blk(2 * HEADS), pl.BlockSpec((2, 1, 1, DH), lambda h, b: (0, h, 0, 0))],
        out_specs=[pl.BlockSpec((tb, DH), lambda h, b: (b, h)),
                   pl.BlockSpec((1, nc, DH, DH), lambda h, b: (h, b, 0, 0))],
        out_shape=[jax.ShapeDtypeStruct((T, HEADS * DH), F32),
                   jax.ShapeDtypeStruct((HEADS, T // CHUNK, DH, DH), F32)],
        scratch_shapes=[pltpu.VMEM((DH, DH), F32)],
        compiler_params=_params("parallel", "arbitrary"),
    )(pm, pm, pm, lbl)


def _hgrn_bwd(pm, lbl, states, do, name):
    T = pm.shape[0]
    tb = _tile(T, SEQ_BLOCK)
    nc = tb // CHUNK
    nb = T // tb

    def body(q_ref, f_ref, i_ref, lbl_ref, st_ref, do_ref, dq_ref, df_ref, di_ref, dl_ref, ds_sc, dlb_sc):
        @pl.when(pl.program_id(1) == 0)
        def _():
            ds_sc[...] = jnp.zeros_like(ds_sc)
            dlb_sc[...] = jnp.zeros_like(dlb_sc)

        lb = _softmax_lb(lbl_ref)
        causal, m1, m2 = _hgrn_masks()
        tri = jnp.where(causal, 1.0, 0.0).astype(BF16)
        tri_rev = jnp.where(_iota((CHUNK, CHUNK), 0) <= _iota((CHUNK, CHUNK), 1), 1.0, 0.0).astype(BF16)
        last_row = _iota((CHUNK, DH), 0) == CHUNK - 1
        dsn = ds_sc[...]
        dlb = jnp.zeros((1, DH), F32)
        for ci in reversed(range(nc)):
            sl = pl.ds(ci * CHUNK, CHUNK)
            q = q_ref[sl, :].astype(F32)
            t = _hgrn_chunk_terms(q, f_ref[sl, :].astype(F32), lb, tri)
            iv = i_ref[sl, :]
            dov = do_ref[sl, :]
            qe1, ke1 = t["qs"] * t["e_q1"], t["k"] * t["e_k1"]
            qe2, ke2 = t["qs"] * t["e_q2"], t["k"] * t["e_k2"]
            qi, kd = t["qs"] * t["e_b"], t["k"] * t["e_kd"]
            qe1b, ke1b, qe2b, ke2b = qe1.astype(BF16), ke1.astype(BF16), qe2.astype(BF16), ke2.astype(BF16)
            a = jnp.where(m1, _dot(qe1b, ke1b, "nt"), 0.0) + jnp.where(m2, _dot(qe2b, ke2b, "nt"), 0.0)
            st = st_ref[0, ci]
            dsnb = dsn.astype(BF16)
            da = _dot(dov, iv, "nt")
            da1 = jnp.where(m1, da, 0.0).astype(BF16)
            da2 = jnp.where(m2, da, 0.0).astype(BF16)
            di_ref[sl, :] = (_dot(a.astype(BF16), dov, "tn") + _dot(kd.astype(BF16), dsnb, "nt")).astype(di_ref.dtype)
            dqe1, dke1 = _dot(da1, ke1b), _dot(da1, qe1b, "tn")
            dqe2, dke2 = _dot(da2, ke2b), _dot(da2, qe2b, "tn")
            dqi = _dot(dov, st.astype(BF16))
            dkd = _dot(iv, dsnb)
            ds_before = t["e_last"] * dsn + _dot(dov, qi.astype(BF16), "tn")
            dqs =dqe1 * t["e_q1"] + dqe2 * t["e_q2"] + dqi * t["e_b"]
            dk = dke1 * t["e_k1"] + dke2 * t["e_k2"] + dkd * t["e_kd"]
            qib, kdb = qi.astype(BF16).astype(F32), kd.astype(BF16).astype(F32)
            db = (dqe1 * qe1b.astype(F32) - dke1 * ke1b.astype(F32) + dqe2 * qe2b.astype(F32)
                  - dke2 * ke2b.astype(F32) + dqi * qib - dkd * kdb)
            extra = _colsum(dkd * kdb) + t["e_last"] * _colsum(dsn * st)
            db = db + jnp.where(last_row, extra, 0.0)
            dlf = _tri_dot(tri_rev, db)
            dfv = dlf / t["f"] - dk
            sg = t["sg"]
            df_ref[sl, :] = (dfv * (1.0 - lb) * sg * (1.0 - sg)).astype(df_ref.dtype)
            dlb = dlb + _colsum(dfv * (1.0 - sg))
            sq = t["sq"]
            dq_ref[sl, :] = (dqs * (sq * (1.0 + q * (1.0 - sq)))).astype(dq_ref.dtype)
            dsn = ds_before
        ds_sc[...] = dsn
        dlb_sc[...] += dlb

        @pl.when(pl.program_id(1) == nb - 1)
        def _():
            dl0 = dlb_sc[...] * lb * (1.0 - lb)
            dl_ref[0, 0] = dl0
            dl_ref[1, 0] = -dl0

    blk = lambda off: pl.BlockSpec((tb, DH), lambda h, b: (nb - 1 - b, off + h))
    lspec = pl.BlockSpec((2, 1, 1, DH), lambda h, b: (0, h, 0, 0))
    out_blk = pl.BlockSpec((tb, DH), lambda h, b: (nb - 1 - b, h))
    D = HEADS * DH
    return pl.pallas_call(
        body, name=name, grid=(HEADS, nb),
        in_specs=[blk(0), blk(HEADS), blk(2 * HEADS), lspec,
                  pl.BlockSpec((1, nc, DH, DH), lambda h, b: (h, nb - 1 - b, 0, 0)), out_blk],
        out_specs=[out_blk, out_blk, out_blk, lspec],
        out_shape=[jax.ShapeDtypeStruct((T, D), BF16)] * 3 + [jax.ShapeDtypeStruct((2, HEADS, 1, DH), F32)],
        scratch_shapes=[pltpu.VMEM((DH, DH), F32), pltpu.VMEM((1, DH), F32)],
        compiler_params=_params("parallel", "arbitrary"),
    )(pm, pm, pm, lbl, states, do)


def _log_sigmoid(x):
    return jnp.minimum(x, 0.0) - jnp.log(1.0 + jnp.exp(-jnp.abs(x)))


def _fox_cumsum(pf, bias, name):
    T = pf.shape[0]
    tb = _tile(T, CUMSUM_BLOCK)

    def body(x_ref, b_ref, c_ref, carry):
        @pl.when(pl.program_id(0) == 0)
        def _():
            carry[...] = jnp.zeros_like(carry)

        tri = jnp.where(_iota((tb, tb), 0) >= _iota((tb, tb), 1), 1.0, 0.0).astype(BF16)
        c = _tri_dot(tri, _log_sigmoid(x_ref[...] + b_ref[...])) + carry[...]
        c_ref[...] = c
        carry[...] = _pick_row(c, tb - 1)

    row = pl.BlockSpec((tb, LANES), lambda i: (i, 0))
    return pl.pallas_call(
        body, name=name, grid=(T // tb,), in_specs=[row, pl.BlockSpec((1, LANES), lambda i: (0, 0))],
        out_specs=row, out_shape=jax.ShapeDtypeStruct((T, LANES), F32),
        scratch_shapes=[pltpu.VMEM((1, LANES), F32)], compiler_params=_params("arbitrary"),
    )(pf, bias)


def _fox_dcum(dc, pf, bias, name):
    T = pf.shape[0]
    tb = _tile(T, CUMSUM_BLOCK)
    nb = T // tb

    def body(dc_ref, x_ref, b_ref, dx_ref, db_ref, carry):
        @pl.when(pl.program_id(0) == 0)
        def _():
            carry[...] = jnp.zeros_like(carry)
            db_ref[...] = jnp.zeros_like(db_ref)

        tri_rev = jnp.where(_iota((tb, tb), 0) <= _iota((tb, tb), 1), 1.0, 0.0).astype(BF16)
        dls = _tri_dot(tri_rev, dc_ref[...]) + carry[...]
        carry[...] = _pick_row(dls, 0)
        dx = dls * (1.0 - _sig(x_ref[...] + b_ref[...]))
        dx_ref[...] = dx
        db_ref[...] += _colsum(dx)

    row = pl.BlockSpec((tb, LANES), lambda i: (nb - 1 - i, 0))
    vec = pl.BlockSpec((1, LANES), lambda i: (0, 0))
    return pl.pallas_call(
        body, name=name, grid=(nb,), in_specs=[row, row, vec], out_specs=[row, vec],
        out_shape=[jax.ShapeDtypeStruct((T, LANES), F32), jax.ShapeDtypeStruct((1, LANES), F32)],
        scratch_shapes=[pltpu.VMEM((1, LANES), F32)], compiler_params=_params("arbitrary"),
    )(dc, pf, bias)


_Q_OFF, _K_OFF, _V_OFF = 4 * HEADS, 5 * HEADS, 6 * HEADS


def _causal_pairs(nq, by_key):
    if by_key:
        pairs = [(i, j) for j in range(nq) for i in range(j, nq)]
    else:
        pairs = [(i, j) for i in range(nq) for j in range(i + 1)]
    return jnp.asarray([p[0] for p in pairs], jnp.int32), jnp.asarray([p[1] for p in pairs], jnp.int32)


def _fox_logits(q, k, ck, row0, masked):
    s = _dot(q, k, "nt") - ck
    if masked:
        s = jnp.where(_iota(s.shape, 0) + row0 >= _iota(s.shape, 1), s, NEG_BIG)
    return s


def _ones_column(rows):
    return jnp.where(_iota((rows, DH), 1) == 0, 1.0, 0.0).astype(BF16)


def _fox_fwd(pm, c_col, c_row, name):
    T = pm.shape[0]
    tq = _tile(T, ATTN_TILE)
    nq = T // tq
    rg = min(ATTN_ROWS, tq)
    qi_tab, kj_tab = _causal_pairs(nq, by_key=False)

    def body(qi_ref, kj_ref, q_ref, k_ref, v_ref, cq_ref, ck_ref, o_ref, lse_ref, m_sc, acc_sc):
        t = pl.program_id(1)
        i, j = qi_ref[t], kj_ref[t]

        @pl.when(j == 0)
        def _():
            m_sc[...] = jnp.full_like(m_sc, NEG_BIG)
            acc_sc[...] = jnp.zeros_like(acc_sc)

        def step(diag):
            m_all, acc_all = m_sc[...], acc_sc[...]
            ones = _ones_column(tq)
            ms, accs = [], []
            for r in range(tq // rg):
                rows = slice(r * rg, (r + 1) * rg)
                w = (r + 1) * rg if diag else tq
                cq = cq_ref[0, rows, :]
                s = _fox_logits(q_ref[rows, :], k_ref[:w, :], ck_ref[0, :, :w], r * rg, diag)
                m_old = m_all[rows, :]
                m_new = jnp.maximum(m_old, jnp.max(s, axis=1, keepdims=True) + cq)
                alpha = jnp.exp(m_old - m_new)
                p = jnp.exp(s - (m_new - cq)).astype(BF16)
                v_one = jnp.concatenate([v_ref[:w, :], ones[:w, :]], axis=1)
                ms.append(m_new)
                accs.append(alpha * acc_all[rows, :] + _dot(p, v_one))
            m_sc[...] = jnp.concatenate(ms, axis=0)
            acc_sc[...] = jnp.concatenate(accs, axis=0)

        @pl.when(j < i)
        def _():
            step(False)

        @pl.when(j == i)
        def _():
            step(True)
            acc = acc_sc[...]
            denom = acc[:, DH:DH + 1]
            o_ref[...] = (acc[:, :DH] / denom).astype(o_ref.dtype)
            lse_ref[0] = m_sc[...] + jnp.log(denom)

    kv = lambda off: pl.BlockSpec((tq, DH), lambda h, t, qi, kj: (kj[t], off + h))
    col = pl.BlockSpec((1, tq, 1), lambda h, t, qi, kj: (h, qi[t], 0))
    grid_spec = pltpu.PrefetchScalarGridSpec(
        num_scalar_prefetch=2, grid=(HEADS, qi_tab.shape[0]),
        in_specs=[pl.BlockSpec((tq, DH), lambda h, t, qi, kj: (qi[t], _Q_OFF + h)), kv(_K_OFF), kv(_V_OFF), col,
                  pl.BlockSpec((1, 1, tq), lambda h, t, qi, kj: (h, 0, kj[t]))],
        out_specs=[pl.BlockSpec((tq, DH), lambda h, t, qi, kj: (qi[t], h)), col],
        scratch_shapes=[pltpu.VMEM((tq, 1), F32), pltpu.VMEM((tq, 2 * DH), F32)])
    return pl.pallas_call(
        body, name=name, grid_spec=grid_spec,
        out_shape=[jax.ShapeDtypeStruct((T, HEADS * DH), BF16), jax.ShapeDtypeStruct((HEADS, T, 1), F32)],
        compiler_params=_params("parallel", "arbitrary"),
    )(qi_tab, kj_tab, pm, pm, pm, c_col, c_row)


def _fox_delta(do, o, name):
    T, D = o.shape
    tr = _tile(T, ROW_TILE)

    def body(do_ref, o_ref, d_ref):
        prod = do_ref[...].astype(F32) * o_ref[...].astype(F32)
        for h in range(HEADS):
            d_ref[h] = _rowsum(prod[:, h * DH:(h + 1) * DH])

    row = pl.BlockSpec((tr, D), lambda i: (i, 0))
    return pl.pallas_call(
        body, name=name, grid=(T // tr,), in_specs=[row, row],
        out_specs=pl.BlockSpec((HEADS, tr, 1), lambda i: (0, i, 0)),
        out_shape=jax.ShapeDtypeStruct((HEADS, T, 1), F32), compiler_params=_params("parallel"),
    )(do, o)


def _fox_bwd(pm, c_col, c_row, do, lse, delta, name):
    T = pm.shape[0]
    tq = _tile(T, ATTN_TILE)
    nq = T // tq
    rg = min(ATTN_ROWS, tq)
    qi_tab, kj_tab = _causal_pairs(nq, by_key=True)
    npairs = qi_tab.shape[0]

    def body(qi_ref, kj_ref, q_ref, k_ref, v_ref, cq_ref, ck_ref, do_ref, lse_ref, dl_ref,
             dq_ref, dk_ref, dv_ref, rsum_ref, csum_ref, dq_sc, dk_sc, dv_sc):
        t = pl.program_id(1)
        i, j = qi_ref[t], kj_ref[t]

        @pl.when(t == 0)
        def _():
            dq_sc[...] = jnp.zeros_like(dq_sc)

        @pl.when(i == j)
        def _():
            dk_sc[...] = jnp.zeros_like(dk_sc)
            dv_sc[...] = jnp.zeros_like(dv_sc)

        base = pl.multiple_of(i * tq, tq)

        def step(diag):
            ones = _ones_column(tq)
            for r in range(tq // rg):
                rows = slice(r * rg, (r + 1) * rg)
                w = (r + 1) * rg if diag else tq
                qr, dor = q_ref[rows, :], do_ref[rows, :]
                s = _fox_logits(qr, k_ref[:w, :], ck_ref[0, :, :w], r * rg, diag)
                p = jnp.exp(s - (lse_ref[0, rows, :] - cq_ref[0, rows, :]))
                dp = _dot(dor, v_ref[:w, :], "nt")
                dsb = (p * (dp - dl_ref[0, rows, :])).astype(BF16)
                dv_sc[:w, :] += _dot(p.astype(BF16), dor, "tn")
                dk_sc[:w, :] += _dot(dsb, jnp.concatenate([qr, ones[rows, :]], axis=1), "tn")
                dq_sc[pl.ds(base + r * rg, rg), :] += _dot(dsb, jnp.concatenate([k_ref[:w, :], ones[:w, :]], axis=1))

        @pl.when(i > j)
        def _():
            step(False)

        @pl.when(i == j)
        def _():
            step(True)

        @pl.when(i == nq - 1)
        def _():
            dk_ref[...] = dk_sc[:, :DH].astype(dk_ref.dtype)
            dv_ref[...] = dv_sc[...].astype(dv_ref.dtype)
            csum_ref[0] = dk_sc[:, DH:DH + 1]

        @pl.when(t == npairs - 1)
        def _():
            dq_ref[...] = dq_sc[:, :DH].astype(dq_ref.dtype)
            rsum_ref[0] = dq_sc[:, DH:DH + 1]

    col = pl.BlockSpec((1, tq, 1), lambda h, t, qi, kj: (h, qi[t], 0))
    kv = lambda off: pl.BlockSpec((tq, DH), lambda h, t, qi, kj: (kj[t], off + h))
    kv_out = pl.BlockSpec((tq, DH), lambda h, t, qi, kj: (kj[t], h))
    grid_spec = pltpu.PrefetchScalarGridSpec(
        num_scalar_prefetch=2, grid=(HEADS, npairs),
        in_specs=[pl.BlockSpec((tq, DH), lambda h, t, qi, kj: (qi[t], _Q_OFF + h)), kv(_K_OFF), kv(_V_OFF), col,
                  pl.BlockSpec((1, 1, tq), lambda h, t, qi, kj: (h, 0, kj[t])),
                  pl.BlockSpec((tq, DH), lambda h, t, qi, kj: (qi[t], h)), col, col],
        out_specs=[pl.BlockSpec((T, DH), lambda h, t, qi, kj: (0, h)), kv_out, kv_out,
                   pl.BlockSpec((1, T, 1), lambda h, t, qi, kj: (h, 0, 0)),
                   pl.BlockSpec((1, tq, 1), lambda h, t, qi, kj: (h, kj[t], 0))],
        scratch_shapes=[pltpu.VMEM((T, 2 * DH), F32), pltpu.VMEM((tq, 2 * DH), F32), pltpu.VMEM((tq, DH), F32)])
    D = HEADS * DH
    return pl.pallas_call(
        body, name=name, grid_spec=grid_spec,
        out_shape=[jax.ShapeDtypeStruct((T, D), BF16)] * 3 + [jax.ShapeDtypeStruct((HEADS, T, 1), F32)] * 2,
        compiler_params=_params("parallel", "arbitrary"),
    )(qi_tab, kj_tab, pm, pm, pm, c_col, c_row, do, lse, delta)


def _xattn_fwd(q, kv, name):
    T, D = q.shape
    M = kv.shape[0]
    dh = D // MEM_HEADS
    tq = _tile(T, XATTN_TILE)
    scale = 1.0 / math.sqrt(dh)

    def body(q_ref, kv_ref, o_ref):
        for h in range(MEM_HEADS):
            cs = slice(h * dh, (h + 1) * dh)
            s = _dot(q_ref[:, cs], kv_ref[:, cs], "nt") * scale
            p = jnp.exp(s - jnp.max(s, axis=1, keepdims=True))
            p = p / _rowsum(p)
            o_ref[:, cs] = _dot(p.astype(BF16), kv_ref[:, D + h * dh:D + (h + 1) * dh]).astype(o_ref.dtype)

    row = pl.BlockSpec((tq, D), lambda i: (i, 0))
    return pl.pallas_call(
        body, name=name, grid=(T // tq,), in_specs=[row, pl.BlockSpec((M, 2 * D), lambda i: (0, 0))],
        out_specs=row, out_shape=jax.ShapeDtypeStruct((T, D), BF16), compiler_params=_params("parallel"),
    )(q, kv)


def _xattn_bwd(q, kv, do, name):
    T, D = q.shape
    M = kv.shape[0]
    dh = D // MEM_HEADS
    tq = _tile(T, XATTN_TILE)
    scale = 1.0 / math.sqrt(dh)

    def body(q_ref, kv_ref, do_ref, dq_ref, dkv_ref):
        @pl.when(pl.program_id(0) == 0)
        def _():
            dkv_ref[...] = jnp.zeros_like(dkv_ref)

        for h in range(MEM_HEADS):
            cs = slice(h * dh, (h + 1) * dh)
            vs = slice(D + h * dh, D + (h + 1) * dh)
            s = _dot(q_ref[:, cs], kv_ref[:, cs], "nt") * scale
            p = jnp.exp(s - jnp.max(s, axis=1, keepdims=True))
            p = p / _rowsum(p)
            dp = _dot(do_ref[:, cs], kv_ref[:, vs], "nt")
            ds = (p * (dp - _rowsum(p * dp)) * scale).astype(BF16)
            dq_ref[:, cs] = _dot(ds, kv_ref[:, cs]).astype(dq_ref.dtype)
            dkv_ref[:, cs] += _dot(ds, q_ref[:, cs], "tn")
            dkv_ref[:, vs] += _dot(p.astype(BF16), do_ref[:, cs], "tn")

    row = pl.BlockSpec((tq, D), lambda i: (i, 0))
    full = pl.BlockSpec((M, 2 * D), lambda i: (0, 0))
    return pl.pallas_call(
        body, name=name, grid=(T // tq,), in_specs=[row, full, row], out_specs=[row, full],
        out_shape=[jax.ShapeDtypeStruct((T, D), BF16), jax.ShapeDtypeStruct((M, 2 * D), F32)],
        compiler_params=_params("arbitrary"),
    )(q, kv, do)


_HBM = pl.BlockSpec(memory_space=pltpu.HBM)


def _position():
    return lax.axis_index("x"), lax.axis_index("y"), lax.axis_index("c")


def _other_chips(x, y):
    return [(1 - x, y), (x, 1 - y), (1 - x, 1 - y)]


class _Exchange:
    def __init__(self, inputs, out_shapes, scratch, copies, inplace=False):
        self.inputs, self.out_shapes, self.scratch, self.copies, self.inplace = inputs, out_shapes, scratch, copies, inplace

    def start(self, in_refs, out_refs, sems):
        for cp in self.copies(in_refs, out_refs, sems, False)[0]:
            cp.start()

    def wait(self, in_refs, out_refs, sems):
        for cp, how in self.copies(in_refs, out_refs, sems, True)[1]:
            getattr(cp, how)()

    def aliases(self, first_input, first_output):
        return {first_input + w: first_output + w for w in range(len(self.inputs))} if self.inplace else {}


def _run_exchange(ex, name):
    n_in, n_out = len(ex.inputs), len(ex.out_shapes)

    def body(*refs):
        parts = refs[:n_in], refs[n_in:n_in + n_out], refs[n_in + n_out:]
        ex.start(*parts)
        ex.wait(*parts)

    return pl.pallas_call(
        body, name=name, in_specs=[_HBM] * n_in, out_specs=[_HBM] * n_out, out_shape=ex.out_shapes,
        input_output_aliases=ex.aliases(0, 0), scratch_shapes=ex.scratch,
    )(*ex.inputs)


def _chip_exchange(arrays, out_shapes, src_of, dst_of):
    n = len(arrays)

    def copies(srcs, outs, sems, waiting):
        send, recv, local = sems
        x, y, c = _position()
        q = 2 * x + y
        kept, sent, arriving = [], [], []
        for w, (s_ref, o_ref) in enumerate(zip(srcs, outs)):
            kept.append(pltpu.make_async_copy(src_of(s_ref, q, c), dst_of(o_ref, q, c), local.at[w]))
            for j, (px, py) in enumerate(_other_chips(x, y)):
                sems_j = dict(send_sem=send.at[3 * w + j], recv_sem=recv.at[3 * w + j], device_id=(px, py, c),
                              device_id_type=MESH)
                sent.append(pltpu.make_async_remote_copy(src_ref=src_of(s_ref, 2 * px + py, c),
                                                         dst_ref=dst_of(o_ref, q, c), **sems_j))
                if waiting:
                    arriving.append(pltpu.make_async_remote_copy(src_ref=src_of(s_ref, q, c),
                                                                 dst_ref=dst_of(o_ref, 2 * px + py, c), **sems_j))
        return kept + sent, ([(cp, "wait_recv") for cp in arriving] + [(cp, "wait_send") for cp in sent]
                             + [(cp, "wait") for cp in kept])

    scratch = [pltpu.SemaphoreType.DMA((3 * n,)), pltpu.SemaphoreType.DMA((3 * n,)), pltpu.SemaphoreType.DMA((n,))]
    return _Exchange(arrays, out_shapes, scratch, copies)


def _ex_ag_chips(blks):
    return _chip_exchange(blks, [jax.ShapeDtypeStruct((4, 2) + b.shape, b.dtype) for b in blks],
                          src_of=lambda r, chip, c: r, dst_of=lambda r, chip, c: r.at[chip, c])


def _ex_rs_chips(parts):
    return _chip_exchange(parts, [jax.ShapeDtypeStruct(h.shape, h.dtype) for h in parts],
                          src_of=lambda r, chip, c: r.at[chip], dst_of=lambda r, chip, c: r.at[chip])


def _ex_ag_sibling(arrs):
    n = len(arrs)

    def copies(ins, outs, sems, waiting):
        send, recv = sems
        x, y, c = _position()
        to = dict(device_id=(x, y, 1 - c), device_id_type=MESH)
        mine = [pltpu.make_async_remote_copy(src_ref=a.at[:, c], dst_ref=a.at[:, c], send_sem=send.at[w],
                                             recv_sem=recv.at[w], **to) for w, a in enumerate(outs)]
        theirs = [pltpu.make_async_remote_copy(src_ref=a.at[:, c], dst_ref=a.at[:, 1 - c], send_sem=send.at[w],
                                               recv_sem=recv.at[w], **to) for w, a in enumerate(outs if waiting else [])]
        return mine, [(cp, "wait_recv") for cp in theirs] + [(cp, "wait_send") for cp in mine]

    return _Exchange(arrs, [jax.ShapeDtypeStruct(a.shape, a.dtype) for a in arrs],
                     [pltpu.SemaphoreType.DMA((n,)), pltpu.SemaphoreType.DMA((n,))], copies, inplace=True)


def _ex_rs_sibling(blocks):
    n = len(blocks)

    def copies(srcs, outs, sems, waiting):
        send, recv = sems
        x, y, c = _position()
        cps = [pltpu.make_async_remote_copy(src_ref=b.at[:, 1 - c], dst_ref=l, send_sem=send.at[w], recv_sem=recv.at[w],
                                            device_id=(x, y, 1 - c), device_id_type=MESH)
               for w, (b, l) in enumerate(zip(srcs, outs))]
        return cps, [(cp, "wait") for cp in cps]

    return _Exchange(blocks, [jax.ShapeDtypeStruct((4,) + b.shape[2:], b.dtype) for b in blocks],
                     [pltpu.SemaphoreType.DMA((n,)), pltpu.SemaphoreType.DMA((n,))], copies)


def _row_tile(rows, pref=256):
    for t in range(min(pref, rows) // 16 * 16, 0, -16):
        if rows % t == 0:
            return t
    raise ValueError(f"no row tile for {rows}")


def _pair_add(blocks, landed, core, out_dtype, name):
    n, _, s0, s1 = blocks.shape
    tr = _row_tile(s0)

    def body(core_ref, a_ref, b_ref, o_ref):
        del core_ref
        o_ref[...] = (a_ref[...] + b_ref[...]).astype(o_ref.dtype)

    grid_spec = pltpu.PrefetchScalarGridSpec(
        num_scalar_prefetch=1, grid=(n, s0 // tr),
        in_specs=[pl.BlockSpec((1, None, tr, s1), lambda p, i, core: (p, core[0], i, 0)),
                  pl.BlockSpec((1, tr, s1), lambda p, i, core: (p, i, 0))],
        out_specs=pl.BlockSpec((1, tr, s1), lambda p, i, core: (p, i, 0)))
    return pl.pallas_call(
        body, name=name, grid_spec=grid_spec, out_shape=jax.ShapeDtypeStruct(landed.shape, out_dtype),
        compiler_params=_params("parallel", "parallel"),
    )(core, blocks, landed)


def _adamw_math(w, g, m, v):
    m = ADAM_B1 * m + (1.0 - ADAM_B1) * g
    v = ADAM_B2 * v + (1.0 - ADAM_B2) * (g * g)
    m_hat = m / (1.0 - ADAM_B1 ** ADAM_STEP)
    v_hat = v / (1.0 - ADAM_B2 ** ADAM_STEP)
    delta = -ADAM_LR * (m_hat / (jnp.sqrt(v_hat) + ADAM_EPS) + ADAM_WD * w)
    return delta, m, v


def _adamw_reduce(slots, w, m, v, name):
    n, R, C = slots.shape
    tr = _row_tile(R)

    def body(s_ref, w_ref, m_ref, v_ref, g_ref, d_ref, nm_ref, nv_ref):
        g = s_ref[0].astype(F32)
        for p in range(1, n):
            g = g + s_ref[p].astype(F32)
        g_ref[...] = g
        d_ref[...], nm_ref[...], nv_ref[...] = _adamw_math(w_ref[...], g, m_ref[...], v_ref[...])

    row = pl.BlockSpec((tr, C), lambda i: (i, 0))
    return pl.pallas_call(
        body, name=name, grid=(R // tr,), in_specs=[pl.BlockSpec((n, tr, C), lambda i: (0, i, 0)), row, row, row],
        out_specs=[row] * 4, out_shape=[jax.ShapeDtypeStruct((R, C), F32)] * 4, compiler_params=_params("parallel"),
    )(slots, w, m, v)


def _full_from_gathered(a, n):
    s0, s1 = a.shape[2:]
    blk = a.reshape(8, s0, s1)
    if n in COL_SHARDED:
        return blk.transpose(1, 0, 2).reshape(s0, 8 * s1)
    return blk.reshape(8 * s0, s1)


def _blocks_from_full(g, n, shard_shape):
    s0, s1 = shard_shape
    if n in COL_SHARDED:
        blk = g.reshape(s0, 8, s1).transpose(1, 0, 2)
    else:
        blk = g.reshape(8, s0, s1)
    return blk.reshape(4, 2, s0, s1)


def _swiglu_interleave(w):
    d, f2 = w.shape
    return w.reshape(d, 2, f2 // (2 * SWIGLU_TILE), SWIGLU_TILE).transpose(0, 2, 1, 3).reshape(d, f2)


def _swiglu_deinterleave(w):
    d, f2 = w.shape
    return w.reshape(d, f2 // (2 * SWIGLU_TILE), 2, SWIGLU_TILE).transpose(0, 2, 1, 3).reshape(d, f2)


SMALL_ROWS = 16


def _pack_small(vals, loss_row):
    rows = []
    for n in SMALL:
        flat = vals[n].reshape(-1)
        pad = (-flat.shape[0]) % PACK_COLS
        rows.append(jnp.pad(flat, (0, pad)).reshape(-1, PACK_COLS))
    rows.append(loss_row)
    out = jnp.concatenate(rows, axis=0)
    assert out.shape[0] == SMALL_ROWS, out.shape
    return out


def _unpack_small(packed, like):
    out, r = {}, 0
    for n in SMALL:
        size = like[n].size
        rows = -(-size // PACK_COLS)
        out[n] = packed[r:r + rows].reshape(-1)[:size].reshape(like[n].shape)
        r += rows
    return out


class _NoTraffic:
    def host(self, stage):
        return None

    def landed(self, stage, arrays):
        pass

    def grads_ready(self, names, gW):
        pass


def _mm_behind(traffic, stage, *args, **kwargs):
    ex = traffic.host(stage)
    if ex is None:
        return _mm(*args, **kwargs)
    out, arrays = _mm(*args, hosted=ex, **kwargs)
    traffic.landed(stage, arrays)
    return out


def _ffn_fwd(x, g_pre, W, tag, traffic, up_stage=None, down_stage=None):
    h = _rms_fwd(x, g_pre, f"{tag}_pre")
    ex = traffic.host(up_stage) if up_stage else None
    u, a, arrays = _mm_swiglu(h, W[f"{tag}_w_in"], f"{tag}_up", hosted=ex)
    if ex is not None:
        traffic.landed(up_stage, arrays)
    z = _mm_behind(traffic, down_stage, a, W[f"{tag}_w_down"], "nn", F32, f"{tag}_down", tk=1408)
    return h, u, a, z


def _ffn_bwd(saved, x, g_pre, w_in, w_down, g_post, dx_out, tag, traffic, up_dx_stage=None):
    h, u, a, z = saved
    dz, dg_post = _rms_bwd(z, g_post, dx_out, 0.5, f"{tag}_post_bwd", BF16)
    dw_down = _mm(a, dz, "tn", F32, f"{tag}_down_dw", tm=1408)
    du = _mm_swiglu_bwd(dz, w_down, u, f"{tag}_down_dx")
    dh = _mm_behind(traffic, up_dx_stage, du, w_in, "nt", BF16, f"{tag}_up_dx", tk=5632)
    dw_in = _mm(h, du, "tn", F32, f"{tag}_up_dw", tk=4096)
    dx, dg_pre = _rms_bwd(x, g_pre, dh, 1.0, f"{tag}_pre_bwd", F32, resid=dx_out)
    return dx, dg_pre, dg_post, dw_in, dw_down


def _step_local(x, mem, target, W, S, traffic=_NoTraffic()):
    T, D = x.shape
    gW, gS = {}, {}

    f1 = _ffn_fwd(x, S["ffn1_pre_g"], W, "ffn1", traffic, "gather_mixer_chips", "gather_mixer_sibling")
    x1 = _resid_rms(x, f1[3], S["ffn1_post_g"], 0.5, "ffn1_post")

    h2 = _rms_fwd(x1, S["mix_pre_g"], "mix_pre")
    pm = _mm_behind(traffic, "gather_late_chips", h2, W["w_main"], "nn", BF16, "mix_proj_main")
    pf = _mm(h2, W["w_f"], "nn", F32, "mix_proj_f")
    pg = _mm_behind(traffic, "gather_late_sibling", h2, W["w_gates"], "nn", BF16, "mix_proj_gates")
    lbl = S["hg_lb_logits"].reshape(2, HEADS, 1, DH)
    o_a, states = _hgrn_fwd(pm, lbl, "hgrn_fwd")
    oan = _hgout_fwd(o_a, pm, S["hg_norm_g"], "hgrn_out")
    bias = jnp.pad(S["fox_f_bias"], ((0, 0), (0, LANES - HEADS)))
    c = _fox_cumsum(pf, bias, "fox_cumsum")
    c_heads = c[:, :HEADS].T
    c_col, c_row = c_heads[:, :, None], c_heads[:, None, :]
    o_b, lse = _fox_fwd(pm, c_col, c_row, "fox_fwd")
    ya = _mm(oan, W["w_branch_a"], "nn", BF16, "branch_a")
    yb = _mm(o_b, W["w_branch_b"], "nn", BF16, "branch_b")
    y = _merge_fwd(ya, yb, pg, S["b_gate"], "merge")
    z2 = _mm(y, W["w_out"], "nn", F32, "mix_out")
    x2 = _resid_rms(x1, z2, S["mix_post_g"], 1.0, "mix_post")

    h3 = _rms_fwd(x2, S["mem_pre_g"], "mem_pre")
    memn = _rms_fwd(mem, S["mem_kv_g"], "mem_kv_norm")
    qm = _mm(h3, W["w_mq"], "nn", BF16, "mem_q")
    kv = _mm(memn, W["w_mkv"], "nn", BF16, "mem_kv")
    om = _xattn_fwd(qm, kv, "mem_attn")
    z3 = _mm(om, W["w_mo"], "nn", F32, "mem_o")
    x3 = _resid_rms(x2, z3, S["mem_post_g"], 1.0, "mem_post")

    f2 = _ffn_fwd(x3, S["ffn2_pre_g"], W, "ffn2", traffic)
    dx4, sq = _final_loss(x3, f2[3], S["ffn2_post_g"], 0.5, target, "loss")

    dx3, gS["ffn2_pre_g"], gS["ffn2_post_g"], gW["ffn2_w_in"], gW["ffn2_w_down"] = _ffn_bwd(
        f2, x3, S["ffn2_pre_g"], W["ffn2_w_in"], W["ffn2_w_down"], S["ffn2_post_g"], dx4, "ffn2", traffic)
    traffic.grads_ready(["ffn2_w_in", "ffn2_w_down"], gW)

    dz3, gS["mem_post_g"] = _rms_bwd(z3, S["mem_post_g"], dx3, 1.0, "mem_post_bwd", BF16)
    dom = _mm(dz3, W["w_mo"], "nt", BF16, "mem_o_dx")
    gW["w_mo"] = _mm(om, dz3, "tn", F32, "mem_o_dw")
    dqm, dkv = _xattn_bwd(qm, kv, dom, "mem_attn_bwd")
    dh3 = _mm(dqm, W["w_mq"], "nt", BF16, "mem_q_dx")
    gW["w_mq"] = _mm(h3, dqm, "tn", F32, "mem_q_dw")
    dkvb = dkv.astype(BF16)
    gW["w_mkv"] = _mm(memn, dkvb, "tn", F32, "mem_kv_dw")
    dmemn = _mm(dkvb, W["w_mkv"], "nt", F32, "mem_kv_dx")
    _, gS["mem_kv_g"] = _rms_bwd(mem, S["mem_kv_g"], dmemn, 1.0, "mem_kv_norm_bwd", BF16)
    dx2, gS["mem_pre_g"] = _rms_bwd(x2, S["mem_pre_g"], dh3, 1.0, "mem_pre_bwd", F32, resid=dx3)

    dz2, gS["mix_post_g"] = _rms_bwd(z2, S["mix_post_g"], dx2, 1.0, "mix_post_bwd", BF16)
    dy = _mm(dz2, W["w_out"], "nt", BF16, "mix_out_dx")
    gW["w_out"] = _mm(y, dz2, "tn", F32, "mix_out_dw")
    dya, dyb, dpg, gS["b_gate"] = _merge_bwd(dy, ya, yb, pg, S["b_gate"], "merge_bwd")
    doan = _mm(dya, W["w_branch_a"], "nt", BF16, "branch_a_dx")
    gW["w_branch_a"] = _mm(oan, dya, "tn", F32, "branch_a_dw")
    dob = _mm(dyb, W["w_branch_b"], "nt", BF16, "branch_b_dx")
    gW["w_branch_b"] = _mm(o_b, dyb, "tn", F32, "branch_b_dw")
    traffic.grads_ready(["w_mo", "w_mq", "w_mkv", "w_out", "w_branch_a", "w_branch_b"], gW)

    delta = _fox_delta(dob, o_b, "fox_delta")
    dq_b, dk_b, dv_b, ds_rows, ds_cols = _fox_bwd(pm, c_col, c_row, dob, lse, delta, "fox_bwd")
    dc = jnp.pad((ds_rows - ds_cols).reshape(HEADS, T).T, ((0, 0), (0, LANES - HEADS)))
    dpf, dbias = _fox_dcum(dc, pf, bias, "fox_cumsum_bwd")
    gS["fox_f_bias"] = dbias[:, :HEADS]

    do_a, dg_a, gS["hg_norm_g"] = _hgout_bwd(o_a, pm, S["hg_norm_g"], doan, "hgrn_out_bwd")
    dq_a, df_a, di_a, dlbl = _hgrn_bwd(pm, lbl, states, do_a, "hgrn_bwd")
    gS["hg_lb_logits"] = dlbl.reshape(2, HEADS, DH)

    dpm = jnp.concatenate([dq_a, df_a, di_a, dg_a, dq_b, dk_b, dv_b], axis=1)
    dpf16 = dpf.astype(BF16)
    dh2 = _mm_behind(traffic, "scatter_ffn2_chips", dpm, W["w_main"], "nt", F32, "mix_proj_main_dx")
    dh2 = _mm(dpg, W["w_gates"], "nt", F32, "mix_proj_gates_dx", add=dh2)
    dh2 = _mm(dpf16, W["w_f"], "nt", F32, "mix_proj_f_dx", add=dh2)
    gW["w_main"] = _mm_behind(traffic, "scatter_mid_chips", h2, dpm, "tn", F32, "mix_proj_main_dw")
    gW["w_gates"] = _mm(h2, dpg, "tn", F32, "mix_proj_gates_dw")
    gW["w_f"] = _mm(h2, dpf16, "tn", F32, "mix_proj_f_dw")
    traffic.grads_ready(["w_in"], gW)
    dx1, gS["mix_pre_g"] = _rms_bwd(x1, S["mix_pre_g"], dh2, 1.0, "mix_pre_bwd", F32, resid=dx2)

    dx0, gS["ffn1_pre_g"], gS["ffn1_post_g"], gW["ffn1_w_in"], gW["ffn1_w_down"] = _ffn_bwd(
        f1, x, S["ffn1_pre_g"], W["ffn1_w_in"], W["ffn1_w_down"], S["ffn1_post_g"], dx1, "ffn1", traffic,
        "scatter_w_in_chips")
    traffic.grads_ready(["ffn1_w_in", "ffn1_w_down"], gW)
    return sq, dx0, gW, gS


GATHER_FIRST = ["ffn1_w_in", "ffn1_w_down"]
GATHER_MIXER = ["w_in", "w_branch_a", "w_branch_b", "w_out"]
GATHER_LATE = ["w_mq", "w_mkv", "w_mo", "ffn2_w_in", "ffn2_w_down"]
SCATTER_BEHIND = {
    "scatter_ffn2_chips": ["ffn2_w_in", "ffn2_w_down"],
    "scatter_mid_chips": ["w_mo", "w_mq", "w_mkv", "w_out", "w_branch_a", "w_branch_b"],
    "scatter_w_in_chips": ["w_in"],
}


class _Traffic:
    def __init__(self, sent, W, shapes, core, D):
        self.sent, self.W, self.shapes, self.core, self.D = sent, W, shapes, core, D
        self.half, self.pairs, self.slots = {}, {}, {}

    def install(self, names, gathered):
        D = self.D
        for n, g in zip(names, gathered):
            full = _full_from_gathered(g, n)
            if n == "w_in":
                self.W["w_main"] = full[:, :7 * D]
                self.W["w_f"] = jnp.pad(full[:, 7 * D:7 * D + HEADS], ((0, 0), (0, LANES - HEADS)))
                self.W["w_gates"] = full[:, 7 * D + HEADS:]
            elif n in ("ffn1_w_in", "ffn2_w_in"):
                self.W[n] = _swiglu_interleave(full)
            else:
                self.W[n] = full

    def host(self, stage):
        if stage == "gather_mixer_chips":
            return _ex_ag_chips([self.sent[n] for n in GATHER_MIXER])
        if stage == "gather_late_chips":
            return _ex_ag_chips([self.sent[n] for n in GATHER_LATE])
        if stage in ("gather_mixer_sibling", "gather_late_sibling"):
            return _ex_ag_sibling(self.half[stage])
        if stage in SCATTER_BEHIND:
            return _ex_rs_chips([self.pairs[n] for n in SCATTER_BEHIND[stage]])
        return None

    def landed(self, stage, arrays):
        if stage == "gather_mixer_chips":
            self.half["gather_mixer_sibling"] = arrays
        elif stage == "gather_late_chips":
            self.half["gather_late_sibling"] = arrays
        elif stage == "gather_mixer_sibling":
            self.install(GATHER_MIXER, arrays)
        elif stage == "gather_late_sibling":
            self.install(GATHER_LATE, arrays)
        else:
            self.slots.update(zip(SCATTER_BEHIND[stage], arrays))

    def _final_grad(self, n, gW):
        D = self.D
        if n == "w_in":
            g = gW["w_main"]
            return jnp.concatenate([g[:, :4 * D], g[:, 4 * D:5 * D] * (1.0 / math.sqrt(DH)), g[:, 5 * D:],
                                    gW["w_f"][:, :HEADS], gW["w_gates"]], axis=1)
        if n in ("ffn1_w_in", "ffn2_w_in"):
            return _swiglu_deinterleave(gW[n])
        return gW[n]

    def grads_ready(self, names, gW):
        blocks = [_blocks_from_full(self._final_grad(n, gW), n, self.shapes[n]) for n in names]
        got = _run_exchange(_ex_rs_sibling(blocks), f"rs_sibling_{names[0]}")
        for n, b, l in zip(names, blocks, got):
            self.pairs[n] = _pair_add(b, l, self.core, BF16, f"rs_pair_add_{n}")

    def finish(self):
        rest = [n for n in BIG if n not in self.slots]
        got = _run_exchange(_ex_rs_chips([self.pairs[n] for n in rest]), "rs_chips_last")
        self.slots.update(zip(rest, got))
        return self.slots


def _train_step(a):
    c_idx = lax.axis_index("c")
    x, mem, target = a["x"][0], a["mem"][0], a["loss_target"][0]
    D = x.shape[1]
    shards = {n: a[n][0] for n in BIG}

    fox_scale = 1.0 / math.sqrt(DH)
    n_mine = shards["w_in"].shape[1]
    dev = 4 * lax.axis_index("x") + 2 * lax.axis_index("y") + c_idx
    cols = dev * n_mine + jnp.arange(n_mine)
    is_fox_q = (cols >= 4 * D) & (cols < 5 * D)
    sent = dict(shards, w_in=shards["w_in"] * jnp.where(is_fox_q, fox_scale, 1.0)[None, :])
    sent = {n: v.astype(BF16) for n, v in sent.items()}
    W = {}
    traffic = _Traffic(sent, W, {n: shards[n].shape for n in BIG}, c_idx.astype(jnp.int32).reshape(1), D)
    first = _run_exchange(_ex_ag_chips([sent[n] for n in GATHER_FIRST]), "ag_first_chips")
    traffic.install(GATHER_FIRST, _run_exchange(_ex_ag_sibling(first), "ag_first_sibling"))
    S = {n: a[n] for n in SMALL}

    sq, grad_x, gW, gS = _step_local(x, mem, target, W, S, traffic)
    slots = traffic.finish()
    big = {n: _adamw_reduce(slots[n], shards[n], a["m_" + n][0], a["v_" + n][0], f"adamw_{n}") for n in BIG}

    loss_row = jnp.pad(sq[:1, :1] * (0.5 / D), ((0, 0), (0, PACK_COLS - 1)))
    small_half = _run_exchange(_ex_ag_chips([_pack_small(gS, loss_row)]), "small_ag_chips")
    small_all = _run_exchange(_ex_ag_sibling(small_half), "small_ag_sibling")[0]
    small_slots = small_all.reshape(8, SMALL_ROWS, PACK_COLS)
    zero_row = jnp.zeros((1, PACK_COLS), F32)
    g_sm, d_sm, m_sm, v_sm = _adamw_reduce(
        small_slots, _pack_small({n: a[n] for n in SMALL}, zero_row),
        _pack_small({n: a["m_" + n] for n in SMALL}, zero_row),
        _pack_small({n: a["v_" + n] for n in SMALL}, zero_row), "adamw_small")

    def unpack(which, small):
        out = _unpack_small(small, {n: a[n] for n in SMALL})
        for n in BIG:
            out[n] = big[n][which][None]
        return [out[n] for n in WEIGHTS]

    loss = g_sm[SMALL_ROWS - 1, 0]
    return (loss, grad_x[None], *unpack(0, g_sm), *unpack(1, d_sm), *unpack(2, m_sm), *unpack(3, v_sm))


def kernel(x, mem, ffn1_pre_g, ffn1_w_in, ffn1_w_down, ffn1_post_g, mix_pre_g, w_in, hg_lb_logits, hg_norm_g, fox_f_bias, w_branch_a, w_branch_b, b_gate, w_out, mix_post_g, mem_pre_g, mem_kv_g, w_mq, w_mkv, w_mo, mem_post_g, ffn2_pre_g, ffn2_w_in, ffn2_w_down, ffn2_post_g, loss_target, m_ffn1_pre_g, m_ffn1_w_in, m_ffn1_w_down, m_ffn1_post_g, m_mix_pre_g, m_w_in, m_hg_lb_logits, m_hg_norm_g, m_fox_f_bias, m_w_branch_a, m_w_branch_b, m_b_gate, m_w_out, m_mix_post_g, m_mem_pre_g, m_mem_kv_g, m_w_mq, m_w_mkv, m_w_mo, m_mem_post_g, m_ffn2_pre_g, m_ffn2_w_in, m_ffn2_w_down, m_ffn2_post_g, v_ffn1_pre_g, v_ffn1_w_in, v_ffn1_w_down, v_ffn1_post_g, v_mix_pre_g, v_w_in, v_hg_lb_logits, v_hg_norm_g, v_fox_f_bias, v_w_branch_a, v_w_branch_b, v_b_gate, v_w_out, v_mix_post_g, v_mem_pre_g, v_mem_kv_g, v_w_mq, v_w_mkv, v_w_mo, v_mem_post_g, v_ffn2_pre_g, v_ffn2_w_in, v_ffn2_w_down, v_ffn2_post_g):
    return _train_step(dict(locals()))
```

```python
import functools
import math

import jax
import jax.numpy as jnp
from jax import lax
from jax.experimental import pallas as pl
from jax.experimental.pallas import tpu as pltpu

F32 = jnp.float32
BF16 = jnp.bfloat16
MESH = pl.DeviceIdType.MESH

EPS = 1e-6
HEADS = 8
DH = 128
MEM_HEADS = 4
CHUNK = 128
HALF = CHUNK // 2
SWIGLU_TILE = 256
LANES = 128
PACK_COLS = 1024
ROW_TILE = 1024
SEQ_BLOCK = 2048
CUMSUM_BLOCK = 512
XATTN_TILE = 2048
ATTN_TILE = 2048
ATTN_ROWS = 256
EXP_CLAMP = 80.0
NEG_BIG = -1e30

ADAM_LR, ADAM_B1, ADAM_B2, ADAM_EPS, ADAM_WD, ADAM_STEP = 0.001, 0.9, 0.999, 1e-08, 0.01, 10

VMEM_LIMIT = 48 * 1024 * 1024

_DN = {
    "nn": (((1,), (0,)), ((), ())),
    "nt": (((1,), (1,)), ((), ())),
    "tn": (((0,), (0,)), ((), ())),
}

BIG = ["ffn1_w_in", "ffn1_w_down", "w_in", "w_branch_a", "w_branch_b", "w_out", "w_mq", "w_mkv", "w_mo",
       "ffn2_w_in", "ffn2_w_down"]
COL_SHARDED = {"ffn1_w_in", "w_in", "w_mkv", "ffn2_w_in"}
SMALL = ["ffn1_pre_g", "ffn1_post_g", "mix_pre_g", "hg_lb_logits", "hg_norm_g", "fox_f_bias", "b_gate",
         "mix_post_g", "mem_pre_g", "mem_kv_g", "mem_post_g", "ffn2_pre_g", "ffn2_post_g"]
WEIGHTS = ["ffn1_pre_g", "ffn1_w_in", "ffn1_w_down", "ffn1_post_g", "mix_pre_g", "w_in", "hg_lb_logits",
           "hg_norm_g", "fox_f_bias", "w_branch_a", "w_branch_b", "b_gate", "w_out", "mix_post_g", "mem_pre_g",
           "mem_kv_g", "w_mq", "w_mkv", "w_mo", "mem_post_g", "ffn2_pre_g", "ffn2_w_in", "ffn2_w_down",
           "ffn2_post_g"]


def _dot(a, b, mode="nn"):
    return lax.dot_general(a, b, _DN[mode], preferred_element_type=F32)


def _sig(x):
    return 1.0 / (1.0 + jnp.exp(-x))


def _sig_approx(x):
    return pl.reciprocal(1.0 + jnp.exp(-x), approx=True)


def _params(*dims):
    return pltpu.CompilerParams(dimension_semantics=dims if dims else None, vmem_limit_bytes=VMEM_LIMIT)


def _tile(dim, pref):
    if dim <= pref:
        return dim
    t = (pref // LANES) * LANES
    while t >= LANES:
        if dim % t == 0:
            return t
        t -= LANES
    raise ValueError(f"no tile for {dim}")


def _colsum(x):
    return jnp.sum(x, axis=0, keepdims=True)


def _rowsum(x):
    return jnp.sum(x, axis=1, keepdims=True)


def _iota(shape, axis):
    return lax.broadcasted_iota(jnp.int32, shape, axis)


def _pick_row(x, r):
    return _colsum(jnp.where(_iota(x.shape, 0) == r, x, 0.0))


def _tri_dot(tri, x):
    hi = x.astype(BF16)
    r1 = x - hi.astype(F32)
    mid = r1.astype(BF16)
    lo = (r1 - mid.astype(F32)).astype(BF16)
    return _dot(tri, hi) + _dot(tri, mid) + _dot(tri, lo)


_MM_TILES = {"nn": (2048, 512, 1024), "nt": (512, 1024, 4096), "tn": (1024, 1024, 2048)}


def _host_call(body, name, grid, in_specs, out_specs, out_shape, scratch_shapes, dims, args, hosted=None):
    if hosted is None:
        results = pl.pallas_call(body, name=name, grid=grid, in_specs=in_specs, out_specs=out_specs, out_shape=out_shape,
                                 scratch_shapes=scratch_shapes, compiler_params=_params(*dims))(*args)
        return list(results), []
    n_in, n_out, n_sc = len(in_specs), len(out_specs), len(scratch_shapes)
    h_in, h_out = len(hosted.inputs), len(hosted.out_shapes)

    def wrapped(*refs):
        cut = [n_in, h_in, n_out, h_out, n_sc]
        at = [sum(cut[:i]) for i in range(len(cut) + 1)]
        ins, hin, outs, hout, scr = (refs[at[i]:at[i + 1]] for i in range(len(cut)))
        hsems = refs[at[-1]:]
        ids = [pl.program_id(d) for d in range(len(grid))]
        first = functools.reduce(jnp.logical_and, [i == 0 for i in ids])
        last = functools.reduce(jnp.logical_and, [i == g - 1 for i, g in zip(ids, grid)])

        @pl.when(first)
        def _():
            hosted.start(hin, hout, hsems)

        body(*ins, *outs, *scr)

        @pl.when(last)
        def _():
            hosted.wait(hin, hout, hsems)

    results = pl.pallas_call(
        wrapped, name=name, grid=grid, in_specs=list(in_specs) + [_HBM] * h_in,
        out_specs=list(out_specs) + [_HBM] * h_out, out_shape=list(out_shape) + list(hosted.out_shapes),
        scratch_shapes=list(scratch_shapes) + list(hosted.scratch), input_output_aliases=hosted.aliases(n_in, n_out),
        compiler_params=_params(*dims))(*args, *hosted.inputs)
    return list(results[:n_out]), list(results[n_out:])


def _mm(a, b, mode, out_dtype, name, add=None, tm=None, tn=None, tk=None, hosted=None):
    tm, tn, tk = (given or pref for given, pref in zip((tm, tn, tk), _MM_TILES[mode]))
    if mode == "nn":
        (M, K), (K2, N) = a.shape, b.shape
    elif mode == "nt":
        (M, K), (N, K2) = a.shape, b.shape
    else:
        (K, M), (K2, N) = a.shape, b.shape
    assert K == K2, (name, a.shape, b.shape)
    tm, tn, tk = _tile(M, tm), _tile(N, tn), _tile(K, tk)
    nk = K // tk
    if mode == "tn":
        a_spec = pl.BlockSpec((tk, tm), lambda i, j, k: (k, i))
    else:
        a_spec = pl.BlockSpec((tm, tk), lambda i, j, k: (i, k))
    if mode == "nt":
        b_spec = pl.BlockSpec((tn, tk), lambda i, j, k: (j, k))
    else:
        b_spec = pl.BlockSpec((tk, tn), lambda i, j, k: (k, j))
    o_spec = pl.BlockSpec((tm, tn), lambda i, j, k: (i, j))
    has_add = add is not None

    def body(*refs):
        a_ref, b_ref = refs[0], refs[1]
        c_ref = refs[2] if has_add else None
        o_ref = refs[3] if has_add else refs[2]
        part = _dot(a_ref[...], b_ref[...], mode)
        if nk == 1:
            if has_add:
                part = part + c_ref[...]
            o_ref[...] = part.astype(o_ref.dtype)
            return
        acc_ref = refs[-1]
        k = pl.program_id(2)

        @pl.when(k == 0)
        def _():
            acc_ref[...] = part + c_ref[...] if has_add else part

        @pl.when(k > 0)
        def _():
            acc_ref[...] += part

        @pl.when(k == nk - 1)
        def _():
            o_ref[...] = acc_ref[...].astype(o_ref.dtype)

    in_specs = [a_spec, b_spec] + ([o_spec] if has_add else [])
    args = (a, b) + ((add,) if has_add else ())
    (out,), landed = _host_call(
        body, name, (M // tm, N // tn, nk), in_specs, [o_spec], [jax.ShapeDtypeStruct((M, N), out_dtype)],
        [pltpu.VMEM((tm, tn), F32)] if nk > 1 else [], ("parallel", "parallel", "arbitrary"), args, hosted)
    return out if hosted is None else (out, landed)


def _rms_fwd(x, g, name, out_dtype=BF16):
    T, D = x.shape
    tr = _tile(T, ROW_TILE)

    def body(x_ref, g_ref, o_ref):
        xv = x_ref[...]
        r = lax.rsqrt(jnp.mean(xv * xv, axis=-1, keepdims=True) + EPS)
        o_ref[...] = (xv * r * g_ref[...]).astype(o_ref.dtype)

    return pl.pallas_call(
        body, name=name, grid=(T // tr,),
        in_specs=[pl.BlockSpec((tr, D), lambda i: (i, 0)), pl.BlockSpec((1, D), lambda i: (0, 0))],
        out_specs=pl.BlockSpec((tr, D), lambda i: (i, 0)),
        out_shape=jax.ShapeDtypeStruct((T, D), out_dtype), compiler_params=_params("parallel"),
    )(x, g)


def _resid_rms(x, z, g, scale, name):
    T, D = x.shape
    tr = _tile(T, ROW_TILE)

    def body(x_ref, z_ref, g_ref, o_ref):
        zv = z_ref[...]
        r = lax.rsqrt(jnp.mean(zv * zv, axis=-1, keepdims=True) + EPS)
        o_ref[...] = x_ref[...] + scale * (zv * r * g_ref[...])

    row = pl.BlockSpec((tr, D), lambda i: (i, 0))
    return pl.pallas_call(
        body, name=name, grid=(T // tr,), in_specs=[row, row, pl.BlockSpec((1, D), lambda i: (0, 0))],
        out_specs=row, out_shape=jax.ShapeDtypeStruct((T, D), F32), compiler_params=_params("parallel"),
    )(x, z, g)


def _final_loss(x, z, g, scale, target, name):
    T, D = x.shape
    tr = _tile(T, ROW_TILE)

    def body(x_ref, z_ref, g_ref, t_ref, dx_ref, acc_ref):
        @pl.when(pl.program_id(0) == 0)
        def _():
            acc_ref[...] = jnp.zeros_like(acc_ref)

        zv = z_ref[...]
        r = lax.rsqrt(jnp.mean(zv * zv, axis=-1, keepdims=True) + EPS)
        e = x_ref[...] + scale * (zv * r * g_ref[...]) - t_ref[...]
        dx_ref[...] = e * (1.0 / D)
        acc_ref[...] += _colsum(_rowsum(e * e))

    row = pl.BlockSpec((tr, D), lambda i: (i, 0))
    return pl.pallas_call(
        body, name=name, grid=(T // tr,), in_specs=[row, row, pl.BlockSpec((1, D), lambda i: (0, 0)), row],
        out_specs=[row, pl.BlockSpec((8, LANES), lambda i: (0, 0))],
        out_shape=[jax.ShapeDtypeStruct((T, D), F32), jax.ShapeDtypeStruct((8, LANES), F32)],
        compiler_params=_params("arbitrary"),
    )(x, z, g, target)


def _rms_bwd(xin, g, dy, scale, name, out_dtype, resid=None):
    T, D = xin.shape
    tr = _tile(T, ROW_TILE)
    has_resid = resid is not None

    def body(*refs):
        x_ref, g_ref, dy_ref = refs[:3]
        r_ref = refs[3] if has_resid else None
        dx_ref, dg_ref = refs[-2], refs[-1]

        @pl.when(pl.program_id(0) == 0)
        def _():
            dg_ref[...] = jnp.zeros_like(dg_ref)

        xv = x_ref[...]
        r = lax.rsqrt(jnp.mean(xv * xv, axis=-1, keepdims=True) + EPS)
        xh = xv * r
        dyv = dy_ref[...].astype(F32) * scale
        dxh = dyv * g_ref[...]
        dx = r * (dxh - xh * jnp.mean(dxh * xh, axis=-1, keepdims=True))
        if has_resid:
            dx = dx + r_ref[...]
        dx_ref[...] = dx.astype(dx_ref.dtype)
        dg_ref[...] += _colsum(dyv * xh)

    row = pl.BlockSpec((tr, D), lambda i: (i, 0))
    vec = pl.BlockSpec((1, D), lambda i: (0, 0))
    return pl.pallas_call(
        body, name=name, grid=(T // tr,), in_specs=[row, vec, row] + ([row] if has_resid else []),
        out_specs=[row, vec],
        out_shape=[jax.ShapeDtypeStruct((T, D), out_dtype), jax.ShapeDtypeStruct((1, D), F32)],
        compiler_params=_params("arbitrary"),
    )(*((xin, g, dy) + ((resid,) if has_resid else ())))


def _mm_swiglu(h, w_in, name, hosted=None):
    T, K = h.shape
    F2 = w_in.shape[1]
    tf = SWIGLU_TILE
    tm = _tile(T, _MM_TILES["nn"][0])

    def body(h_ref, w_ref, u_ref, a_ref):
        u = _dot(h_ref[...], w_ref[...])
        u_ref[...] = u.astype(u_ref.dtype)
        gate, up = u[:, :tf], u[:, tf:]
        a_ref[...] = (gate * _sig_approx(gate) * up).astype(a_ref.dtype)

    (u, a), landed = _host_call(
        body, name, (T // tm, F2 // (2 * tf)),
        [pl.BlockSpec((tm, K), lambda i, j: (i, 0)), pl.BlockSpec((K, 2 * tf), lambda i, j: (0, j))],
        [pl.BlockSpec((tm, 2 * tf), lambda i, j: (i, j)), pl.BlockSpec((tm, tf), lambda i, j: (i, j))],
        [jax.ShapeDtypeStruct((T, F2), BF16), jax.ShapeDtypeStruct((T, F2 // 2), BF16)], [],
        ("parallel", "parallel"), (h, w_in), hosted)
    return u, a, landed


def _mm_swiglu_bwd(dz, w_down, u, name):
    T, D = dz.shape
    F = w_down.shape[0]
    tf = SWIGLU_TILE
    tm = _tile(T, _MM_TILES["nn"][0])

    def body(dz_ref, w_ref, u_ref, o_ref):
        d = _dot(dz_ref[...], w_ref[...], "nt")
        gate = u_ref[:, :tf].astype(F32)
        up = u_ref[:, tf:].astype(F32)
        s = _sig_approx(gate)
        o_ref[:, :tf] = (d * up * (s * (1.0 + gate * (1.0 - s)))).astype(o_ref.dtype)
        o_ref[:, tf:] = (d * gate * s).astype(o_ref.dtype)

    return pl.pallas_call(
        body, name=name, grid=(T // tm, F // tf),
        in_specs=[pl.BlockSpec((tm, D), lambda i, j: (i, 0)), pl.BlockSpec((tf, D), lambda i, j: (j, 0)),
                  pl.BlockSpec((tm, 2 * tf), lambda i, j: (i, j))],
        out_specs=pl.BlockSpec((tm, 2 * tf), lambda i, j: (i, j)),
        out_shape=jax.ShapeDtypeStruct((T, 2 * F), BF16), compiler_params=_params("parallel", "parallel"),
    )(dz, w_down, u)


def _hgout_fwd(o_a, pm, g, name):
    T, D = o_a.shape
    tr = _tile(T, ROW_TILE)

    def body(o_ref, ga_ref, g_ref, out_ref):
        ov = o_ref[...]
        r = lax.rsqrt(jnp.mean(ov * ov, axis=-1, keepdims=True) + EPS)
        ga = ga_ref[...].astype(F32)
        out_ref[...] = (ov * r * g_ref[...] * (ga * _sig(ga))).astype(out_ref.dtype)

    row = pl.BlockSpec((tr, D), lambda i: (i, 0))
    return pl.pallas_call(
        body, name=name, grid=(T // tr,),
        in_specs=[row, pl.BlockSpec((tr, D), lambda i: (i, 3)), pl.BlockSpec((1, D), lambda i: (0, 0))],
        out_specs=row, out_shape=jax.ShapeDtypeStruct((T, D), BF16), compiler_params=_params("parallel"),
    )(o_a, pm, g)


def _hgout_bwd(o_a, pm, g, d_out, name):
    T, D = o_a.shape
    tr = _tile(T, ROW_TILE)

    def body(o_ref, ga_ref, g_ref, d_ref, do_ref, dga_ref, dg_ref):
        @pl.when(pl.program_id(0) == 0)
        def _():
            dg_ref[...] = jnp.zeros_like(dg_ref)

        ov = o_ref[...]
        r = lax.rsqrt(jnp.mean(ov * ov, axis=-1, keepdims=True) + EPS)
        oh = ov * r
        ga = ga_ref[...].astype(F32)
        s = _sig(ga)
        d = d_ref[...].astype(F32)
        dn = d * (ga * s)
        dga_ref[...] = (d * (oh * g_ref[...]) * (s * (1.0 + ga * (1.0 - s)))).astype(dga_ref.dtype)
        dxh = dn * g_ref[...]
        do_ref[...] = (r * (dxh - oh * jnp.mean(dxh * oh, axis=-1, keepdims=True))).astype(do_ref.dtype)
        dg_ref[...] += _colsum(dn * oh)

    row = pl.BlockSpec((tr, D), lambda i: (i, 0))
    vec = pl.BlockSpec((1, D), lambda i: (0, 0))
    return pl.pallas_call(
        body, name=name, grid=(T // tr,), in_specs=[row, pl.BlockSpec((tr, D), lambda i: (i, 3)), vec, row],
        out_specs=[row, row, vec],
        out_shape=[jax.ShapeDtypeStruct((T, D), BF16), jax.ShapeDtypeStruct((T, D), BF16),
                   jax.ShapeDtypeStruct((1, D), F32)],
        compiler_params=_params("arbitrary"),
    )(o_a, pm, g, d_out)


def _merge_fwd(ya, yb, pg, bg, name):
    T, D = ya.shape
    tr = _tile(T, 256)

    def body(ya_ref, yb_ref, pg_ref, bg_ref, o_ref):
        g0 = _sig(pg_ref[:, :D].astype(F32) + bg_ref[:, :D])
        g1 = _sig(pg_ref[:, D:].astype(F32) + bg_ref[:, D:])
        o_ref[...] = (g0 * ya_ref[...].astype(F32) + g1 * yb_ref[...].astype(F32)).astype(o_ref.dtype)

    row = pl.BlockSpec((tr, D), lambda i: (i, 0))
    return pl.pallas_call(
        body, name=name, grid=(T // tr,),
        in_specs=[row, row, pl.BlockSpec((tr, 2 * D), lambda i: (i, 0)), pl.BlockSpec((1, 2 * D), lambda i: (0, 0))],
        out_specs=row, out_shape=jax.ShapeDtypeStruct((T, D), BF16), compiler_params=_params("parallel"),
    )(ya, yb, pg, bg)


def _merge_bwd(dy, ya, yb, pg, bg, name):
    T, D = ya.shape
    tr = _tile(T, 256)

    def body(dy_ref, ya_ref, yb_ref, pg_ref, bg_ref, dya_ref, dyb_ref, dpg_ref, dbg_ref):
        @pl.when(pl.program_id(0) == 0)
        def _():
            dbg_ref[...] = jnp.zeros_like(dbg_ref)

        d = dy_ref[...].astype(F32)
        g0 = _sig(pg_ref[:, :D].astype(F32) + bg_ref[:, :D])
        g1 = _sig(pg_ref[:, D:].astype(F32) + bg_ref[:, D:])
        dya_ref[...] = (d * g0).astype(dya_ref.dtype)
        dyb_ref[...] = (d * g1).astype(dyb_ref.dtype)
        dg0 = d * ya_ref[...].astype(F32) * (g0 * (1.0 - g0))
        dg1 = d * yb_ref[...].astype(F32) * (g1 * (1.0 - g1))
        dpg_ref[:, :D] = dg0.astype(dpg_ref.dtype)
        dpg_ref[:, D:] = dg1.astype(dpg_ref.dtype)
        dbg_ref[:, :D] += _colsum(dg0)
        dbg_ref[:, D:] += _colsum(dg1)

    row = pl.BlockSpec((tr, D), lambda i: (i, 0))
    wide = pl.BlockSpec((tr, 2 * D), lambda i: (i, 0))
    wvec = pl.BlockSpec((1, 2 * D), lambda i: (0, 0))
    return pl.pallas_call(
        body, name=name, grid=(T // tr,), in_specs=[row, row, row, wide, wvec],
        out_specs=[row, row, wide, wvec],
        out_shape=[jax.ShapeDtypeStruct((T, D), BF16), jax.ShapeDtypeStruct((T, D), BF16),
                   jax.ShapeDtypeStruct((T, 2 * D), BF16), jax.ShapeDtypeStruct((1, 2 * D), F32)],
        compiler_params=_params("arbitrary"),
    )(dy, ya, yb, pg, bg)


def _hgrn_chunk_terms(q, fl, lb, tri):
    shape = q.shape
    row = _iota(shape, 0)
    sg = _sig(fl)
    f = lb + (1.0 - lb) * sg
    k = 1.0 - f
    b = _tri_dot(tri, jnp.log(f))
    ref1 = jnp.where(row < HALF, _pick_row(b, HALF // 2), _pick_row(b, HALF + HALF // 2))
    b_half = _pick_row(b, HALF - 1)
    b_last = _pick_row(b, CHUNK - 1)
    sq = _sig(q)
    qs = q * sq
    e_q1 = jnp.exp(jnp.minimum(b - ref1, EXP_CLAMP))
    e_k1 = jnp.exp(jnp.minimum(ref1 - b, EXP_CLAMP))
    e_q2 = jnp.exp(jnp.minimum(b - b_half, 0.0))
    e_k2 = jnp.exp(jnp.minimum(b_half - b, 0.0))
    e_b = jnp.exp(b)
    e_kd = jnp.exp(b_last - b)
    return dict(sg=sg, f=f, k=k, sq=sq, qs=qs, e_q1=e_q1, e_k1=e_k1, e_q2=e_q2, e_k2=e_k2, e_b=e_b, e_kd=e_kd,
                e_last=jnp.exp(b_last))


def _hgrn_masks():
    r = _iota((CHUNK, CHUNK), 0)
    c = _iota((CHUNK, CHUNK), 1)
    causal = r >= c
    same = (r < HALF) == (c < HALF)
    return causal, causal & same, (r >= HALF) & (c < HALF)


def _softmax_lb(lbl_ref):
    l0, l1 = lbl_ref[0, 0], lbl_ref[1, 0]
    mx = jnp.maximum(l0, l1)
    e0, e1 = jnp.exp(l0 - mx), jnp.exp(l1 - mx)
    return e0 / (e0 + e1)


def _hgrn_fwd(pm, lbl, name):
    T = pm.shape[0]
    tb = _tile(T, SEQ_BLOCK)
    nc = tb // CHUNK

    def body(q_ref, f_ref, i_ref, lbl_ref, o_ref, st_ref, s_sc):
        @pl.when(pl.program_id(1) == 0)
        def _():
            s_sc[...] = jnp.zeros_like(s_sc)

        lb = _softmax_lb(lbl_ref)
        causal, m1, m2 = _hgrn_masks()
        tri = jnp.where(causal, 1.0, 0.0).astype(BF16)
        parts = []
        for ci in range(nc):
            sl = pl.ds(ci * CHUNK, CHUNK)
            t = _hgrn_chunk_terms(q_ref[sl, :].astype(F32), f_ref[sl, :].astype(F32), lb, tri)
            iv = i_ref[sl, :]
            a1 = _dot((t["qs"] * t["e_q1"]).astype(BF16), (t["k"] * t["e_k1"]).astype(BF16), "nt")
            a2 = _dot((t["qs"] * t["e_q2"]).astype(BF16), (t["k"] * t["e_k2"]).astype(BF16), "nt")
            a = jnp.where(m1, a1, 0.0) + jnp.where(m2, a2, 0.0)
            parts.append((_dot(a.astype(BF16), iv), (t["qs"] * t["e_b"]).astype(BF16),
                          _dot(iv, (t["k"] * t["e_kd"]).astype(BF16), "tn"), t["e_last"]))
        st = s_sc[...]
        for ci, (o_intra, qi, grow, e_last) in enumerate(parts):
            st_ref[0, ci] = st
            o_ref[pl.ds(ci * CHUNK, CHUNK), :] = o_intra + _dot(qi, st.astype(BF16), "nt")
            st = e_last * st + grow
        s_sc[...] = st

    blk = lambda off: pl.BlockSpec((tb, DH), lambda h, b: (b, off + h))
    return pl.pallas_call(
        body, name=name, grid=(HEADS, T // tb),
        in_specs=[blk(0), blk(HEADS), blk(2 * HEADS), pl.BlockSpec((2, 1, 1, DH), lambda h, b: (0, h, 0, 0))],
        out_specs=[pl.BlockSpec((tb, DH), lambda h, b: (b, h)),
                   pl.BlockSpec((1, nc, DH, DH), lambda h, b: (h, b, 0, 0))],
        out_shape=[jax.ShapeDtypeStruct((T, HEADS * DH), F32),
                   jax.ShapeDtypeStruct((HEADS, T // CHUNK, DH, DH), F32)],
        scratch_shapes=[pltpu.VMEM((DH, DH), F32)],
        compiler_params=_params("parallel", "arbitrary"),
    )(pm, pm, pm, lbl)


def _hgrn_bwd(pm, lbl, states, do, name):
    T = pm.shape[0]
    tb = _tile(T, SEQ_BLOCK)
    nc = tb // CHUNK
    nb = T // tb

    def body(q_ref, f_ref, i_ref, lbl_ref, st_ref, do_ref, dq_ref, df_ref, di_ref, dl_ref, ds_sc, dlb_sc):
        @pl.when(pl.program_id(1) == 0)
        def _():
            ds_sc[...] = jnp.zeros_like(ds_sc)
            dlb_sc[...] = jnp.zeros_like(dlb_sc)

        lb = _softmax_lb(lbl_ref)
        causal, m1, m2 = _hgrn_masks()
        tri = jnp.where(causal, 1.0, 0.0).astype(BF16)
        tri_rev = jnp.where(_iota((CHUNK, CHUNK), 0) <= _iota((CHUNK, CHUNK), 1), 1.0, 0.0).astype(BF16)
        last_row = _iota((CHUNK, DH), 0) == CHUNK - 1
        dsn = ds_sc[...]
        dlb = jnp.zeros((1, DH), F32)
        for ci in reversed(range(nc)):
            sl = pl.ds(ci * CHUNK, CHUNK)
            q = q_ref[sl, :].astype(F32)
            t = _hgrn_chunk_terms(q, f_ref[sl, :].astype(F32), lb, tri)
            iv = i_ref[sl, :]
            dov = do_ref[sl, :]
            qe1, ke1 = t["qs"] * t["e_q1"], t["k"] * t["e_k1"]
            qe2, ke2 = t["qs"] * t["e_q2"], t["k"] * t["e_k2"]
            qi, kd = t["qs"] * t["e_b"], t["k"] * t["e_kd"]
            qe1b, ke1b, qe2b, ke2b = qe1.astype(BF16), ke1.astype(BF16), qe2.astype(BF16), ke2.astype(BF16)
            a = jnp.where(m1, _dot(qe1b, ke1b, "nt"), 0.0) + jnp.where(m2, _dot(qe2b, ke2b, "nt"), 0.0)
            st = st_ref[0, ci]
            dsnb = dsn.astype(BF16)
            da = _dot(dov, iv, "nt")
            da1 = jnp.where(m1, da, 0.0).astype(BF16)
            da2 = jnp.where(m2, da, 0.0).astype(BF16)
            di_ref[sl, :] = (_dot(a.astype(BF16), dov, "tn") + _dot(kd.astype(BF16), dsnb, "nt")).astype(di_ref.dtype)
            dqe1, dke1 = _dot(da1, ke1b), _dot(da1, qe1b, "tn")
            dqe2, dke2 = _dot(da2, ke2b), _dot(da2, qe2b, "tn")
            dqi = _dot(dov, st.astype(BF16))
            dkd = _dot(iv, dsnb)
            ds_before = t["e_last"] * dsn + _dot(dov, qi.astype(BF16), "tn")
            dqs =dqe1 * t["e_q1"] + dqe2 * t["e_q2"] + dqi * t["e_b"]
            dk = dke1 * t["e_k1"] + dke2 * t["e_k2"] + dkd * t["e_kd"]
            qib, kdb = qi.astype(BF16).astype(F32), kd.astype(BF16).astype(F32)
            db = (dqe1 * qe1b.astype(F32) - dke1 * ke1b.astype(F32) + dqe2 * qe2b.astype(F32)
                  - dke2 * ke2b.astype(F32) + dqi * qib - dkd * kdb)
            extra = _colsum(dkd * kdb) + t["e_last"] * _colsum(dsn * st)
            db = db + jnp.where(last_row, extra, 0.0)
            dlf = _tri_dot(tri_rev, db)
            dfv = dlf / t["f"] - dk
            sg = t["sg"]
            df_ref[sl, :] = (dfv * (1.0 - lb) * sg * (1.0 - sg)).astype(df_ref.dtype)
            dlb = dlb + _colsum(dfv * (1.0 - sg))
            sq = t["sq"]
            dq_ref[sl, :] = (dqs * (sq * (1.0 + q * (1.0 - sq)))).astype(dq_ref.dtype)
            dsn = ds_before
        ds_sc[...] = dsn
        dlb_sc[...] += dlb

        @pl.when(pl.program_id(1) == nb - 1)
        def _():
            dl0 = dlb_sc[...] * lb * (1.0 - lb)
            dl_ref[0, 0] = dl0
            dl_ref[1, 0] = -dl0

    blk = lambda off: pl.BlockSpec((tb, DH), lambda h, b: (nb - 1 - b, off + h))
    lspec = pl.BlockSpec((2, 1, 1, DH), lambda h, b: (0, h, 0, 0))
    out_blk = pl.BlockSpec((tb, DH), lambda h, b: (nb - 1 - b, h))
    D = HEADS * DH
    return pl.pallas_call(
        body, name=name, grid=(HEADS, nb),
        in_specs=[blk(0), blk(HEADS), blk(2 * HEADS), lspec,
                  pl.BlockSpec((1, nc, DH, DH), lambda h, b: (h, nb - 1 - b, 0, 0)), out_blk],
        out_specs=[out_blk, out_blk, out_blk, lspec],
        out_shape=[jax.ShapeDtypeStruct((T, D), BF16)] * 3 + [jax.ShapeDtypeStruct((2, HEADS, 1, DH), F32)],
        scratch_shapes=[pltpu.VMEM((DH, DH), F32), pltpu.VMEM((1, DH), F32)],
        compiler_params=_params("parallel", "arbitrary"),
    )(pm, pm, pm, lbl, states, do)


def _log_sigmoid(x):
    return jnp.minimum(x, 0.0) - jnp.log(1.0 + jnp.exp(-jnp.abs(x)))


def _fox_cumsum(pf, bias, name):
    T = pf.shape[0]
    tb = _tile(T, CUMSUM_BLOCK)

    def body(x_ref, b_ref, c_ref, carry):
        @pl.when(pl.program_id(0) == 0)
        def _():
            carry[...] = jnp.zeros_like(carry)

        tri = jnp.where(_iota((tb, tb), 0) >= _iota((tb, tb), 1), 1.0, 0.0).astype(BF16)
        c = _tri_dot(tri, _log_sigmoid(x_ref[...] + b_ref[...])) + carry[...]
        c_ref[...] = c
        carry[...] = _pick_row(c, tb - 1)

    row = pl.BlockSpec((tb, LANES), lambda i: (i, 0))
    return pl.pallas_call(
        body, name=name, grid=(T // tb,), in_specs=[row, pl.BlockSpec((1, LANES), lambda i: (0, 0))],
        out_specs=row, out_shape=jax.ShapeDtypeStruct((T, LANES), F32),
        scratch_shapes=[pltpu.VMEM((1, LANES), F32)], compiler_params=_params("arbitrary"),
    )(pf, bias)


def _fox_dcum(dc, pf, bias, name):
    T = pf.shape[0]
    tb = _tile(T, CUMSUM_BLOCK)
    nb = T // tb

    def body(dc_ref, x_ref, b_ref, dx_ref, db_ref, carry):
        @pl.when(pl.program_id(0) == 0)
        def _():
            carry[...] = jnp.zeros_like(carry)
            db_ref[...] = jnp.zeros_like(db_ref)

        tri_rev = jnp.where(_iota((tb, tb), 0) <= _iota((tb, tb), 1), 1.0, 0.0).astype(BF16)
        dls = _tri_dot(tri_rev, dc_ref[...]) + carry[...]
        carry[...] = _pick_row(dls, 0)
        dx = dls * (1.0 - _sig(x_ref[...] + b_ref[...]))
        dx_ref[...] = dx
        db_ref[...] += _colsum(dx)

    row = pl.BlockSpec((tb, LANES), lambda i: (nb - 1 - i, 0))
    vec = pl.BlockSpec((1, LANES), lambda i: (0, 0))
    return pl.pallas_call(
        body, name=name, grid=(nb,), in_specs=[row, row, vec], out_specs=[row, vec],
        out_shape=[jax.ShapeDtypeStruct((T, LANES), F32), jax.ShapeDtypeStruct((1, LANES), F32)],
        scratch_shapes=[pltpu.VMEM((1, LANES), F32)], compiler_params=_params("arbitrary"),
    )(dc, pf, bias)


_Q_OFF, _K_OFF, _V_OFF = 4 * HEADS, 5 * HEADS, 6 * HEADS


def _causal_pairs(nq, by_key):
    if by_key:
        pairs = [(i, j) for j in range(nq) for i in range(j, nq)]
    else:
        pairs = [(i, j) for i in range(nq) for j in range(i + 1)]
    return jnp.asarray([p[0] for p in pairs], jnp.int32), jnp.asarray([p[1] for p in pairs], jnp.int32)


def _fox_logits(q, k, ck, row0, masked):
    s = _dot(q, k, "nt") - ck
    if masked:
        s = jnp.where(_iota(s.shape, 0) + row0 >= _iota(s.shape, 1), s, NEG_BIG)
    return s


def _ones_column(rows):
    return jnp.where(_iota((rows, DH), 1) == 0, 1.0, 0.0).astype(BF16)


def _fox_fwd(pm, c_col, c_row, name):
    T = pm.shape[0]
    tq = _tile(T, ATTN_TILE)
    nq = T // tq
    rg = min(ATTN_ROWS, tq)
    qi_tab, kj_tab = _causal_pairs(nq, by_key=False)

    def body(qi_ref, kj_ref, q_ref, k_ref, v_ref, cq_ref, ck_ref, o_ref, lse_ref, m_sc, acc_sc):
        t = pl.program_id(1)
        i, j = qi_ref[t], kj_ref[t]

        @pl.when(j == 0)
        def _():
            m_sc[...] = jnp.full_like(m_sc, NEG_BIG)
            acc_sc[...] = jnp.zeros_like(acc_sc)

        def step(diag):
            m_all, acc_all = m_sc[...], acc_sc[...]
            ones = _ones_column(tq)
            ms, accs = [], []
            for r in range(tq // rg):
                rows = slice(r * rg, (r + 1) * rg)
                w = (r + 1) * rg if diag else tq
                cq = cq_ref[0, rows, :]
                s = _fox_logits(q_ref[rows, :], k_ref[:w, :], ck_ref[0, :, :w], r * rg, diag)
                m_old = m_all[rows, :]
                m_new = jnp.maximum(m_old, jnp.max(s, axis=1, keepdims=True) + cq)
                alpha = jnp.exp(m_old - m_new)
                p = jnp.exp(s - (m_new - cq)).astype(BF16)
                v_one = jnp.concatenate([v_ref[:w, :], ones[:w, :]], axis=1)
                ms.append(m_new)
                accs.append(alpha * acc_all[rows, :] + _dot(p, v_one))
            m_sc[...] = jnp.concatenate(ms, axis=0)
            acc_sc[...] = jnp.concatenate(accs, axis=0)

        @pl.when(j < i)
        def _():
            step(False)

        @pl.when(j == i)
        def _():
            step(True)
            acc = acc_sc[...]
            denom = acc[:, DH:DH + 1]
            o_ref[...] = (acc[:, :DH] / denom).astype(o_ref.dtype)
            lse_ref[0] = m_sc[...] + jnp.log(denom)

    kv = lambda off: pl.BlockSpec((tq, DH), lambda h, t, qi, kj: (kj[t], off + h))
    col = pl.BlockSpec((1, tq, 1), lambda h, t, qi, kj: (h, qi[t], 0))
    grid_spec = pltpu.PrefetchScalarGridSpec(
        num_scalar_prefetch=2, grid=(HEADS, qi_tab.shape[0]),
        in_specs=[pl.BlockSpec((tq, DH), lambda h, t, qi, kj: (qi[t], _Q_OFF + h)), kv(_K_OFF), kv(_V_OFF), col,
                  pl.BlockSpec((1, 1, tq), lambda h, t, qi, kj: (h, 0, kj[t]))],
        out_specs=[pl.BlockSpec((tq, DH), lambda h, t, qi, kj: (qi[t], h)), col],
        scratch_shapes=[pltpu.VMEM((tq, 1), F32), pltpu.VMEM((tq, 2 * DH), F32)])
    return pl.pallas_call(
        body, name=name, grid_spec=grid_spec,
        out_shape=[jax.ShapeDtypeStruct((T, HEADS * DH), BF16), jax.ShapeDtypeStruct((HEADS, T, 1), F32)],
        compiler_params=_params("parallel", "arbitrary"),
    )(qi_tab, kj_tab, pm, pm, pm, c_col, c_row)


def _fox_delta(do, o, name):
    T, D = o.shape
    tr = _tile(T, ROW_TILE)

    def body(do_ref, o_ref, d_ref):
        prod = do_ref[...].astype(F32) * o_ref[...].astype(F32)
        for h in range(HEADS):
            d_ref[h] = _rowsum(prod[:, h * DH:(h + 1) * DH])

    row = pl.BlockSpec((tr, D), lambda i: (i, 0))
    return pl.pallas_call(
        body, name=name, grid=(T // tr,), in_specs=[row, row],
        out_specs=pl.BlockSpec((HEADS, tr, 1), lambda i: (0, i, 0)),
        out_shape=jax.ShapeDtypeStruct((HEADS, T, 1), F32), compiler_params=_params("parallel"),
    )(do, o)


def _fox_bwd(pm, c_col, c_row, do, lse, delta, name):
    T = pm.shape[0]
    tq = _tile(T, ATTN_TILE)
    nq = T // tq
    rg = min(ATTN_ROWS, tq)
    qi_tab, kj_tab = _causal_pairs(nq, by_key=True)
    npairs = qi_tab.shape[0]

    def body(qi_ref, kj_ref, q_ref, k_ref, v_ref, cq_ref, ck_ref, do_ref, lse_ref, dl_ref,
             dq_ref, dk_ref, dv_ref, rsum_ref, csum_ref, dq_sc, dk_sc, dv_sc):
        t = pl.program_id(1)
        i, j = qi_ref[t], kj_ref[t]

        @pl.when(t == 0)
        def _():
            dq_sc[...] = jnp.zeros_like(dq_sc)

        @pl.when(i == j)
        def _():
            dk_sc[...] = jnp.zeros_like(dk_sc)
            dv_sc[...] = jnp.zeros_like(dv_sc)

        base = pl.multiple_of(i * tq, tq)

        def step(diag):
            ones = _ones_column(tq)
            for r in range(tq // rg):
                rows = slice(r * rg, (r + 1) * rg)
                w = (r + 1) * rg if diag else tq
                qr, dor = q_ref[rows, :], do_ref[rows, :]
                s = _fox_logits(qr, k_ref[:w, :], ck_ref[0, :, :w], r * rg, diag)
                p = jnp.exp(s - (lse_ref[0, rows, :] - cq_ref[0, rows, :]))
                dp = _dot(dor, v_ref[:w, :], "nt")
                dsb = (p * (dp - dl_ref[0, rows, :])).astype(BF16)
                dv_sc[:w, :] += _dot(p.astype(BF16), dor, "tn")
                dk_sc[:w, :] += _dot(dsb, jnp.concatenate([qr, ones[rows, :]], axis=1), "tn")
                dq_sc[pl.ds(base + r * rg, rg), :] += _dot(dsb, jnp.concatenate([k_ref[:w, :], ones[:w, :]], axis=1))

        @pl.when(i > j)
        def _():
            step(False)

        @pl.when(i == j)
        def _():
            step(True)

        @pl.when(i == nq - 1)
        def _():
            dk_ref[...] = dk_sc[:, :DH].astype(dk_ref.dtype)
            dv_ref[...] = dv_sc[...].astype(dv_ref.dtype)
            csum_ref[0] = dk_sc[:, DH:DH + 1]

        @pl.when(t == npairs - 1)
        def _():
            dq_ref[...] = dq_sc[:, :DH].astype(dq_ref.dtype)
            rsum_ref[0] = dq_sc[:, DH:DH + 1]

    col = pl.BlockSpec((1, tq, 1), lambda h, t, qi, kj: (h, qi[t], 0))
    kv = lambda off: pl.BlockSpec((tq, DH), lambda h, t, qi, kj: (kj[t], off + h))
    kv_out = pl.BlockSpec((tq, DH), lambda h, t, qi, kj: (kj[t], h))
    grid_spec = pltpu.PrefetchScalarGridSpec(
        num_scalar_prefetch=2, grid=(HEADS, npairs),
        in_specs=[pl.BlockSpec((tq, DH), lambda h, t, qi, kj: (qi[t], _Q_OFF + h)), kv(_K_OFF), kv(_V_OFF), col,
                  pl.BlockSpec((1, 1, tq), lambda h, t, qi, kj: (h, 0, kj[t])),
                  pl.BlockSpec((tq, DH), lambda h, t, qi, kj: (qi[t], h)), col, col],
        out_specs=[pl.BlockSpec((T, DH), lambda h, t, qi, kj: (0, h)), kv_out, kv_out,
                   pl.BlockSpec((1, T, 1), lambda h, t, qi, kj: (h, 0, 0)),
                   pl.BlockSpec((1, tq, 1), lambda h, t, qi, kj: (h, kj[t], 0))],
        scratch_shapes=[pltpu.VMEM((T, 2 * DH), F32), pltpu.VMEM((tq, 2 * DH), F32), pltpu.VMEM((tq, DH), F32)])
    D = HEADS * DH
    return pl.pallas_call(
        body, name=name, grid_spec=grid_spec,
        out_shape=[jax.ShapeDtypeStruct((T, D), BF16)] * 3 + [jax.ShapeDtypeStruct((HEADS, T, 1), F32)] * 2,
        compiler_params=_params("parallel", "arbitrary"),
    )(qi_tab, kj_tab, pm, pm, pm, c_col, c_row, do, lse, delta)


def _xattn_fwd(q, kv, name):
    T, D = q.shape
    M = kv.shape[0]
    dh = D // MEM_HEADS
    tq = _tile(T, XATTN_TILE)
    scale = 1.0 / math.sqrt(dh)

    def body(q_ref, kv_ref, o_ref):
        for h in range(MEM_HEADS):
            cs = slice(h * dh, (h + 1) * dh)
            s = _dot(q_ref[:, cs], kv_ref[:, cs], "nt") * scale
            p = jnp.exp(s - jnp.max(s, axis=1, keepdims=True))
            p = p / _rowsum(p)
            o_ref[:, cs] = _dot(p.astype(BF16), kv_ref[:, D + h * dh:D + (h + 1) * dh]).astype(o_ref.dtype)

    row = pl.BlockSpec((tq, D), lambda i: (i, 0))
    return pl.pallas_call(
        body, name=name, grid=(T // tq,), in_specs=[row, pl.BlockSpec((M, 2 * D), lambda i: (0, 0))],
        out_specs=row, out_shape=jax.ShapeDtypeStruct((T, D), BF16), compiler_params=_params("parallel"),
    )(q, kv)


def _xattn_bwd(q, kv, do, name):
    T, D = q.shape
    M = kv.shape[0]
    dh = D // MEM_HEADS
    tq = _tile(T, XATTN_TILE)
    scale = 1.0 / math.sqrt(dh)

    def body(q_ref, kv_ref, do_ref, dq_ref, dkv_ref):
        @pl.when(pl.program_id(0) == 0)
        def _():
            dkv_ref[...] = jnp.zeros_like(dkv_ref)

        for h in range(MEM_HEADS):
            cs = slice(h * dh, (h + 1) * dh)
            vs = slice(D + h * dh, D + (h + 1) * dh)
            s = _dot(q_ref[:, cs], kv_ref[:, cs], "nt") * scale
            p = jnp.exp(s - jnp.max(s, axis=1, keepdims=True))
            p = p / _rowsum(p)
            dp = _dot(do_ref[:, cs], kv_ref[:, vs], "nt")
            ds = (p * (dp - _rowsum(p * dp)) * scale).astype(BF16)
            dq_ref[:, cs] = _dot(ds, kv_ref[:, cs]).astype(dq_ref.dtype)
            dkv_ref[:, cs] += _dot(ds, q_ref[:, cs], "tn")
            dkv_ref[:, vs] += _dot(p.astype(BF16), do_ref[:, cs], "tn")

    row = pl.BlockSpec((tq, D), lambda i: (i, 0))
    full = pl.BlockSpec((M, 2 * D), lambda i: (0, 0))
    return pl.pallas_call(
        body, name=name, grid=(T // tq,), in_specs=[row, full, row], out_specs=[row, full],
        out_shape=[jax.ShapeDtypeStruct((T, D), BF16), jax.ShapeDtypeStruct((M, 2 * D), F32)],
        compiler_params=_params("arbitrary"),
    )(q, kv, do)


_HBM = pl.BlockSpec(memory_space=pltpu.HBM)


def _position():
    return lax.axis_index("x"), lax.axis_index("y"), lax.axis_index("c")


def _other_chips(x, y):
    return [(1 - x, y), (x, 1 - y), (1 - x, 1 - y)]


class _Exchange:
    def __init__(self, inputs, out_shapes, scratch, copies, inplace=False):
        self.inputs, self.out_shapes, self.scratch, self.copies, self.inplace = inputs, out_shapes, scratch, copies, inplace

    def start(self, in_refs, out_refs, sems):
        for cp in self.copies(in_refs, out_refs, sems, False)[0]:
            cp.start()

    def wait(self, in_refs, out_refs, sems):
        for cp, how in self.copies(in_refs, out_refs, sems, True)[1]:
            getattr(cp, how)()

    def aliases(self, first_input, first_output):
        return {first_input + w: first_output + w for w in range(len(self.inputs))} if self.inplace else {}


def _run_exchange(ex, name):
    n_in, n_out = len(ex.inputs), len(ex.out_shapes)

    def body(*refs):
        parts = refs[:n_in], refs[n_in:n_in + n_out], refs[n_in + n_out:]
        ex.start(*parts)
        ex.wait(*parts)

    return pl.pallas_call(
        body, name=name, in_specs=[_HBM] * n_in, out_specs=[_HBM] * n_out, out_shape=ex.out_shapes,
        input_output_aliases=ex.aliases(0, 0), scratch_shapes=ex.scratch,
    )(*ex.inputs)


def _chip_exchange(arrays, out_shapes, src_of, dst_of):
    n = len(arrays)

    def copies(srcs, outs, sems, waiting):
        send, recv, local = sems
        x, y, c = _position()
        q = 2 * x + y
        kept, sent, arriving = [], [], []
        for w, (s_ref, o_ref) in enumerate(zip(srcs, outs)):
            kept.append(pltpu.make_async_copy(src_of(s_ref, q, c), dst_of(o_ref, q, c), local.at[w]))
            for j, (px, py) in enumerate(_other_chips(x, y)):
                sems_j = dict(send_sem=send.at[3 * w + j], recv_sem=recv.at[3 * w + j], device_id=(px, py, c),
                              device_id_type=MESH)
                sent.append(pltpu.make_async_remote_copy(src_ref=src_of(s_ref, 2 * px + py, c),
                                                         dst_ref=dst_of(o_ref, q, c), **sems_j))
                if waiting:
                    arriving.append(pltpu.make_async_remote_copy(src_ref=src_of(s_ref, q, c),
                                                                 dst_ref=dst_of(o_ref, 2 * px + py, c), **sems_j))
        return kept + sent, ([(cp, "wait_recv") for cp in arriving] + [(cp, "wait_send") for cp in sent]
                             + [(cp, "wait") for cp in kept])

    scratch = [pltpu.SemaphoreType.DMA((3 * n,)), pltpu.SemaphoreType.DMA((3 * n,)), pltpu.SemaphoreType.DMA((n,))]
    return _Exchange(arrays, out_shapes, scratch, copies)


def _ex_ag_chips(blks):
    return _chip_exchange(blks, [jax.ShapeDtypeStruct((4, 2) + b.shape, b.dtype) for b in blks],
                          src_of=lambda r, chip, c: r, dst_of=lambda r, chip, c: r.at[chip, c])


def _ex_rs_chips(parts):
    return _chip_exchange(parts, [jax.ShapeDtypeStruct(h.shape, h.dtype) for h in parts],
                          src_of=lambda r, chip, c: r.at[chip], dst_of=lambda r, chip, c: r.at[chip])


def _ex_ag_sibling(arrs):
    n = len(arrs)

    def copies(ins, outs, sems, waiting):
        send, recv = sems
        x, y, c = _position()
        to = dict(device_id=(x, y, 1 - c), device_id_type=MESH)
        mine = [pltpu.make_async_remote_copy(src_ref=a.at[:, c], dst_ref=a.at[:, c], send_sem=send.at[w],
                                             recv_sem=recv.at[w], **to) for w, a in enumerate(outs)]
        theirs = [pltpu.make_async_remote_copy(src_ref=a.at[:, c], dst_ref=a.at[:, 1 - c], send_sem=send.at[w],
                                               recv_sem=recv.at[w], **to) for w, a in enumerate(outs if waiting else [])]
        return mine, [(cp, "wait_recv") for cp in theirs] + [(cp, "wait_send") for cp in mine]

    return _Exchange(arrs, [jax.ShapeDtypeStruct(a.shape, a.dtype) for a in arrs],
                     [pltpu.SemaphoreType.DMA((n,)), pltpu.SemaphoreType.DMA((n,))], copies, inplace=True)


def _ex_rs_sibling(blocks):
    n = len(blocks)

    def copies(srcs, outs, sems, waiting):
        send, recv = sems
        x, y, c = _position()
        cps = [pltpu.make_async_remote_copy(src_ref=b.at[:, 1 - c], dst_ref=l, send_sem=send.at[w], recv_sem=recv.at[w],
                                            device_id=(x, y, 1 - c), device_id_type=MESH)
               for w, (b, l) in enumerate(zip(srcs, outs))]
        return cps, [(cp, "wait") for cp in cps]

    return _Exchange(blocks, [jax.ShapeDtypeStruct((4,) + b.shape[2:], b.dtype) for b in blocks],
                     [pltpu.SemaphoreType.DMA((n,)), pltpu.SemaphoreType.DMA((n,))], copies)


def _row_tile(rows, pref=256):
    for t in range(min(pref, rows) // 16 * 16, 0, -16):
        if rows % t == 0:
            return t
    raise ValueError(f"no row tile for {rows}")


def _pair_add(blocks, landed, core, out_dtype, name):
    n, _, s0, s1 = blocks.shape
    tr = _row_tile(s0)

    def body(core_ref, a_ref, b_ref, o_ref):
        del core_ref
        o_ref[...] = (a_ref[...] + b_ref[...]).astype(o_ref.dtype)

    grid_spec = pltpu.PrefetchScalarGridSpec(
        num_scalar_prefetch=1, grid=(n, s0 // tr),
        in_specs=[pl.BlockSpec((1, None, tr, s1), lambda p, i, core: (p, core[0], i, 0)),
                  pl.BlockSpec((1, tr, s1), lambda p, i, core: (p, i, 0))],
        out_specs=pl.BlockSpec((1, tr, s1), lambda p, i, core: (p, i, 0)))
    return pl.pallas_call(
        body, name=name, grid_spec=grid_spec, out_shape=jax.ShapeDtypeStruct(landed.shape, out_dtype),
        compiler_params=_params("parallel", "parallel"),
    )(core, blocks, landed)


def _adamw_math(w, g, m, v):
    m = ADAM_B1 * m + (1.0 - ADAM_B1) * g
    v = ADAM_B2 * v + (1.0 - ADAM_B2) * (g * g)
    m_hat = m / (1.0 - ADAM_B1 ** ADAM_STEP)
    v_hat = v / (1.0 - ADAM_B2 ** ADAM_STEP)
    delta = -ADAM_LR * (m_hat / (jnp.sqrt(v_hat) + ADAM_EPS) + ADAM_WD * w)
    return delta, m, v


def _adamw_reduce(slots, w, m, v, name):
    n, R, C = slots.shape
    tr = _row_tile(R)

    def body(s_ref, w_ref, m_ref, v_ref, g_ref, d_ref, nm_ref, nv_ref):
        g = s_ref[0].astype(F32)
        for p in range(1, n):
            g = g + s_ref[p].astype(F32)
        g_ref[...] = g
        d_ref[...], nm_ref[...], nv_ref[...] = _adamw_math(w_ref[...], g, m_ref[...], v_ref[...])

    row = pl.BlockSpec((tr, C), lambda i: (i, 0))
    return pl.pallas_call(
        body, name=name, grid=(R // tr,), in_specs=[pl.BlockSpec((n, tr, C), lambda i: (0, i, 0)), row, row, row],
        out_specs=[row] * 4, out_shape=[jax.ShapeDtypeStruct((R, C), F32)] * 4, compiler_params=_params("parallel"),
    )(slots, w, m, v)


def _full_from_gathered(a, n):
    s0, s1 = a.shape[2:]
    blk = a.reshape(8, s0, s1)
    if n in COL_SHARDED:
        return blk.transpose(1, 0, 2).reshape(s0, 8 * s1)
    return blk.reshape(8 * s0, s1)


def _blocks_from_full(g, n, shard_shape):
    s0, s1 = shard_shape
    if n in COL_SHARDED:
        blk = g.reshape(s0, 8, s1).transpose(1, 0, 2)
    else:
        blk = g.reshape(8, s0, s1)
    return blk.reshape(4, 2, s0, s1)


def _swiglu_interleave(w):
    d, f2 = w.shape
    return w.reshape(d, 2, f2 // (2 * SWIGLU_TILE), SWIGLU_TILE).transpose(0, 2, 1, 3).reshape(d, f2)


def _swiglu_deinterleave(w):
    d, f2 = w.shape
    return w.reshape(d, f2 // (2 * SWIGLU_TILE), 2, SWIGLU_TILE).transpose(0, 2, 1, 3).reshape(d, f2)


SMALL_ROWS = 16


def _pack_small(vals, loss_row):
    rows = []
    for n in SMALL:
        flat = vals[n].reshape(-1)
        pad = (-flat.shape[0]) % PACK_COLS
        rows.append(jnp.pad(flat, (0, pad)).reshape(-1, PACK_COLS))
    rows.append(loss_row)
    out = jnp.concatenate(rows, axis=0)
    assert out.shape[0] == SMALL_ROWS, out.shape
    return out


def _unpack_small(packed, like):
    out, r = {}, 0
    for n in SMALL:
        size = like[n].size
        rows = -(-size // PACK_COLS)
        out[n] = packed[r:r + rows].reshape(-1)[:size].reshape(like[n].shape)
        r += rows
    return out


class _NoTraffic:
    def host(self, stage):
        return None

    def landed(self, stage, arrays):
        pass

    def grads_ready(self, names, gW):
        pass


def _mm_behind(traffic, stage, *args, **kwargs):
    ex = traffic.host(stage)
    if ex is None:
        return _mm(*args, **kwargs)
    out, arrays = _mm(*args, hosted=ex, **kwargs)
    traffic.landed(stage, arrays)
    return out


def _ffn_fwd(x, g_pre, W, tag, traffic, up_stage=None, down_stage=None):
    h = _rms_fwd(x, g_pre, f"{tag}_pre")
    ex = traffic.host(up_stage) if up_stage else None
    u, a, arrays = _mm_swiglu(h, W[f"{tag}_w_in"], f"{tag}_up", hosted=ex)
    if ex is not None:
        traffic.landed(up_stage, arrays)
    z = _mm_behind(traffic, down_stage, a, W[f"{tag}_w_down"], "nn", F32, f"{tag}_down", tk=1408)
    return h, u, a, z


def _ffn_bwd(saved, x, g_pre, w_in, w_down, g_post, dx_out, tag, traffic, up_dx_stage=None):
    h, u, a, z = saved
    dz, dg_post = _rms_bwd(z, g_post, dx_out, 0.5, f"{tag}_post_bwd", BF16)
    dw_down = _mm(a, dz, "tn", F32, f"{tag}_down_dw", tm=1408)
    du = _mm_swiglu_bwd(dz, w_down, u, f"{tag}_down_dx")
    dh = _mm_behind(traffic, up_dx_stage, du, w_in, "nt", BF16, f"{tag}_up_dx", tk=5632)
    dw_in = _mm(h, du, "tn", F32, f"{tag}_up_dw", tk=4096)
    dx, dg_pre = _rms_bwd(x, g_pre, dh, 1.0, f"{tag}_pre_bwd", F32, resid=dx_out)
    return dx, dg_pre, dg_post, dw_in, dw_down


def _step_local(x, mem, target, W, S, traffic=_NoTraffic()):
    T, D = x.shape
    gW, gS = {}, {}

    f1 = _ffn_fwd(x, S["ffn1_pre_g"], W, "ffn1", traffic, "gather_mixer_chips", "gather_mixer_sibling")
    x1 = _resid_rms(x, f1[3], S["ffn1_post_g"], 0.5, "ffn1_post")

    h2 = _rms_fwd(x1, S["mix_pre_g"], "mix_pre")
    pm = _mm_behind(traffic, "gather_late_chips", h2, W["w_main"], "nn", BF16, "mix_proj_main")
    pf = _mm(h2, W["w_f"], "nn", F32, "mix_proj_f")
    pg = _mm_behind(traffic, "gather_late_sibling", h2, W["w_gates"], "nn", BF16, "mix_proj_gates")
    lbl = S["hg_lb_logits"].reshape(2, HEADS, 1, DH)
    o_a, states = _hgrn_fwd(pm, lbl, "hgrn_fwd")
    oan = _hgout_fwd(o_a, pm, S["hg_norm_g"], "hgrn_out")
    bias = jnp.pad(S["fox_f_bias"], ((0, 0), (0, LANES - HEADS)))
    c = _fox_cumsum(pf, bias, "fox_cumsum")
    c_heads = c[:, :HEADS].T
    c_col, c_row = c_heads[:, :, None], c_heads[:, None, :]
    o_b, lse = _fox_fwd(pm, c_col, c_row, "fox_fwd")
    ya = _mm(oan, W["w_branch_a"], "nn", BF16, "branch_a")
    yb = _mm(o_b, W["w_branch_b"], "nn", BF16, "branch_b")
    y = _merge_fwd(ya, yb, pg, S["b_gate"], "merge")
    z2 = _mm(y, W["w_out"], "nn", F32, "mix_out")
    x2 = _resid_rms(x1, z2, S["mix_post_g"], 1.0, "mix_post")

    h3 = _rms_fwd(x2, S["mem_pre_g"], "mem_pre")
    memn = _rms_fwd(mem, S["mem_kv_g"], "mem_kv_norm")
    qm = _mm(h3, W["w_mq"], "nn", BF16, "mem_q")
    kv = _mm(memn, W["w_mkv"], "nn", BF16, "mem_kv")
    om = _xattn_fwd(qm, kv, "mem_attn")
    z3 = _mm(om, W["w_mo"], "nn", F32, "mem_o")
    x3 = _resid_rms(x2, z3, S["mem_post_g"], 1.0, "mem_post")

    f2 = _ffn_fwd(x3, S["ffn2_pre_g"], W, "ffn2", traffic)
    dx4, sq = _final_loss(x3, f2[3], S["ffn2_post_g"], 0.5, target, "loss")

    dx3, gS["ffn2_pre_g"], gS["ffn2_post_g"], gW["ffn2_w_in"], gW["ffn2_w_down"] = _ffn_bwd(
        f2, x3, S["ffn2_pre_g"], W["ffn2_w_in"], W["ffn2_w_down"], S["ffn2_post_g"], dx4, "ffn2", traffic)
    traffic.grads_ready(["ffn2_w_in", "ffn2_w_down"], gW)

    dz3, gS["mem_post_g"] = _rms_bwd(z3, S["mem_post_g"], dx3, 1.0, "mem_post_bwd", BF16)
    dom = _mm(dz3, W["w_mo"], "nt", BF16, "mem_o_dx")
    gW["w_mo"] = _mm(om, dz3, "tn", F32, "mem_o_dw")
    dqm, dkv = _xattn_bwd(qm, kv, dom, "mem_attn_bwd")
    dh3 = _mm(dqm, W["w_mq"], "nt", BF16, "mem_q_dx")
    gW["w_mq"] = _mm(h3, dqm, "tn", F32, "mem_q_dw")
    dkvb = dkv.astype(BF16)
    gW["w_mkv"] = _mm(memn, dkvb, "tn", F32, "mem_kv_dw")
    dmemn = _mm(dkvb, W["w_mkv"], "nt", F32, "mem_kv_dx")
    _, gS["mem_kv_g"] = _rms_bwd(mem, S["mem_kv_g"], dmemn, 1.0, "mem_kv_norm_bwd", BF16)
    dx2, gS["mem_pre_g"] = _rms_bwd(x2, S["mem_pre_g"], dh3, 1.0, "mem_pre_bwd", F32, resid=dx3)

    dz2, gS["mix_post_g"] = _rms_bwd(z2, S["mix_post_g"], dx2, 1.0, "mix_post_bwd", BF16)
    dy = _mm(dz2, W["w_out"], "nt", BF16, "mix_out_dx")
    gW["w_out"] = _mm(y, dz2, "tn", F32, "mix_out_dw")
    dya, dyb, dpg, gS["b_gate"] = _merge_bwd(dy, ya, yb, pg, S["b_gate"], "merge_bwd")
    doan = _mm(dya, W["w_branch_a"], "nt", BF16, "branch_a_dx")
    gW["w_branch_a"] = _mm(oan, dya, "tn", F32, "branch_a_dw")
    dob = _mm(dyb, W["w_branch_b"], "nt", BF16, "branch_b_dx")
    gW["w_branch_b"] = _mm(o_b, dyb, "tn", F32, "branch_b_dw")
    traffic.grads_ready(["w_mo", "w_mq", "w_mkv", "w_out", "w_branch_a", "w_branch_b"], gW)

    delta = _fox_delta(dob, o_b, "fox_delta")
    dq_b, dk_b, dv_b, ds_rows, ds_cols = _fox_bwd(pm, c_col, c_row, dob, lse, delta, "fox_bwd")
    dc = jnp.pad((ds_rows.reshape(HEADS, T) - ds_cols.reshape(HEADS, T)).T, ((0, 0), (0, LANES - HEADS)))
    dpf, dbias = _fox_dcum(dc, pf, bias, "fox_cumsum_bwd")
    gS["fox_f_bias"] = dbias[:, :HEADS]

    do_a, dg_a, gS["hg_norm_g"] = _hgout_bwd(o_a, pm, S["hg_norm_g"], doan, "hgrn_out_bwd")
    dq_a, df_a, di_a, dlbl = _hgrn_bwd(pm, lbl, states, do_a, "hgrn_bwd")
    gS["hg_lb_logits"] = dlbl.reshape(2, HEADS, DH)

    dpm = jnp.concatenate([dq_a, df_a, di_a, dg_a, dq_b, dk_b, dv_b], axis=1)
    dpf16 = dpf.astype(BF16)
    dh2 = _mm_behind(traffic, "scatter_ffn2_chips", dpm, W["w_main"], "nt", F32, "mix_proj_main_dx")
    dh2 = _mm(dpg, W["w_gates"], "nt", F32, "mix_proj_gates_dx", add=dh2)
    dh2 = _mm(dpf16, W["w_f"], "nt", F32, "mix_proj_f_dx", add=dh2)
    gW["w_main"] = _mm_behind(traffic, "scatter_mid_chips", h2, dpm, "tn", F32, "mix_proj_main_dw")
    gW["w_gates"] = _mm(h2, dpg, "tn", F32, "mix_proj_gates_dw")
    gW["w_f"] = _mm(h2, dpf16, "tn", F32, "mix_proj_f_dw")
    traffic.grads_ready(["w_in"], gW)
    dx1, gS["mix_pre_g"] = _rms_bwd(x1, S["mix_pre_g"], dh2, 1.0, "mix_pre_bwd", F32, resid=dx2)

    dx0, gS["ffn1_pre_g"], gS["ffn1_post_g"], gW["ffn1_w_in"], gW["ffn1_w_down"] = _ffn_bwd(
        f1, x, S["ffn1_pre_g"], W["ffn1_w_in"], W["ffn1_w_down"], S["ffn1_post_g"], dx1, "ffn1", traffic,
        "scatter_w_in_chips")
    traffic.grads_ready(["ffn1_w_in", "ffn1_w_down"], gW)
    return sq, dx0, gW, gS


GATHER_FIRST = ["ffn1_w_in", "ffn1_w_down"]
GATHER_MIXER = ["w_in", "w_branch_a", "w_branch_b", "w_out"]
GATHER_LATE = ["w_mq", "w_mkv", "w_mo", "ffn2_w_in", "ffn2_w_down"]
SCATTER_BEHIND = {
    "scatter_ffn2_chips": ["ffn2_w_in", "ffn2_w_down"],
    "scatter_mid_chips": ["w_mo", "w_mq", "w_mkv", "w_out", "w_branch_a", "w_branch_b"],
    "scatter_w_in_chips": ["w_in"],
}


class _Traffic:
    def __init__(self, sent, W, shapes, core, D):
        self.sent, self.W, self.shapes, self.core, self.D = sent, W, shapes, core, D
        self.half, self.pairs, self.slots = {}, {}, {}

    def install(self, names, gathered):
        D = self.D
        for n, g in zip(names, gathered):
            full = _full_from_gathered(g, n)
            if n == "w_in":
                self.W["w_main"] = full[:, :7 * D]
                self.W["w_f"] = jnp.pad(full[:, 7 * D:7 * D + HEADS], ((0, 0), (0, LANES - HEADS)))
                self.W["w_gates"] = full[:, 7 * D + HEADS:]
            elif n in ("ffn1_w_in", "ffn2_w_in"):
                self.W[n] = _swiglu_interleave(full)
            else:
                self.W[n] = full

    def host(self, stage):
        if stage == "gather_mixer_chips":
            return _ex_ag_chips([self.sent[n] for n in GATHER_MIXER])
        if stage == "gather_late_chips":
            return _ex_ag_chips([self.sent[n] for n in GATHER_LATE])
        if stage in ("gather_mixer_sibling", "gather_late_sibling"):
            return _ex_ag_sibling(self.half[stage])
        if stage in SCATTER_BEHIND:
            return _ex_rs_chips([self.pairs[n] for n in SCATTER_BEHIND[stage]])
        return None

    def landed(self, stage, arrays):
        if stage == "gather_mixer_chips":
            self.half["gather_mixer_sibling"] = arrays
        elif stage == "gather_late_chips":
            self.half["gather_late_sibling"] = arrays
        elif stage == "gather_mixer_sibling":
            self.install(GATHER_MIXER, arrays)
        elif stage == "gather_late_sibling":
            self.install(GATHER_LATE, arrays)
        else:
            self.slots.update(zip(SCATTER_BEHIND[stage], arrays))

    def _final_grad(self, n, gW):
        D = self.D
        if n == "w_in":
            g = gW["w_main"]
            return jnp.concatenate([g[:, :4 * D], g[:, 4 * D:5 * D] * (1.0 / math.sqrt(DH)), g[:, 5 * D:],
                                    gW["w_f"][:, :HEADS], gW["w_gates"]], axis=1)
        if n in ("ffn1_w_in", "ffn2_w_in"):
            return _swiglu_deinterleave(gW[n])
        return gW[n]

    def grads_ready(self, names, gW):
        blocks = [_blocks_from_full(self._final_grad(n, gW), n, self.shapes[n]) for n in names]
        got = _run_exchange(_ex_rs_sibling(blocks), f"rs_sibling_{names[0]}")
        for n, b, l in zip(names, blocks, got):
            self.pairs[n] = _pair_add(b, l, self.core, BF16, f"rs_pair_add_{n}")

    def finish(self):
        rest = [n for n in BIG if n not in self.slots]
        got = _run_exchange(_ex_rs_chips([self.pairs[n] for n in rest]), "rs_chips_last")
        self.slots.update(zip(rest, got))
        return self.slots


def _train_step(a):
    c_idx = lax.axis_index("c")
    x, mem, target = a["x"][0], a["mem"][0], a["loss_target"][0]
    D = x.shape[1]
    shards = {n: a[n][0] for n in BIG}

    fox_scale = 1.0 / math.sqrt(DH)
    n_mine = shards["w_in"].shape[1]
    dev = 4 * lax.axis_index("x") + 2 * lax.axis_index("y") + c_idx
    cols = dev * n_mine + jnp.arange(n_mine)
    is_fox_q = (cols >= 4 * D) & (cols < 5 * D)
    sent = dict(shards, w_in=shards["w_in"] * jnp.where(is_fox_q, fox_scale, 1.0)[None, :])
    sent = {n: v.astype(BF16) for n, v in sent.items()}
    W = {}
    traffic = _Traffic(sent, W, {n: shards[n].shape for n in BIG}, c_idx.astype(jnp.int32).reshape(1), D)
    first = _run_exchange(_ex_ag_chips([sent[n] for n in GATHER_FIRST]), "ag_first_chips")
    traffic.install(GATHER_FIRST, _run_exchange(_ex_ag_sibling(first), "ag_first_sibling"))
    S = {n: a[n] for n in SMALL}

    sq, grad_x, gW, gS = _step_local(x, mem, target, W, S, traffic)
    slots = traffic.finish()
    big = {n: _adamw_reduce(slots[n], shards[n], a["m_" + n][0], a["v_" + n][0], f"adamw_{n}") for n in BIG}

    loss_row = jnp.pad(sq[:1, :1] * (0.5 / D), ((0, 0), (0, PACK_COLS - 1)))
    small_half = _run_exchange(_ex_ag_chips([_pack_small(gS, loss_row)]), "small_ag_chips")
    small_all = _run_exchange(_ex_ag_sibling(small_half), "small_ag_sibling")[0]
    small_slots = small_all.reshape(8, SMALL_ROWS, PACK_COLS)
    zero_row = jnp.zeros((1, PACK_COLS), F32)
    g_sm, d_sm, m_sm, v_sm = _adamw_reduce(
        small_slots, _pack_small({n: a[n] for n in SMALL}, zero_row),
        _pack_small({n: a["m_" + n] for n in SMALL}, zero_row),
        _pack_small({n: a["v_" + n] for n in SMALL}, zero_row), "adamw_small")

    def unpack(which, small):
        out = _unpack_small(small, {n: a[n] for n in SMALL})
        for n in BIG:
            out[n] = big[n][which][None]
        return [out[n] for n in WEIGHTS]

    loss = g_sm[SMALL_ROWS - 1, 0]
    return (loss, grad_x[None], *unpack(0, g_sm), *unpack(1, d_sm), *unpack(2, m_sm), *unpack(3, v_sm))


def kernel(x, mem, ffn1_pre_g, ffn1_w_in, ffn1_w_down, ffn1_post_g, mix_pre_g, w_in, hg_lb_logits, hg_norm_g, fox_f_bias, w_branch_a, w_branch_b, b_gate, w_out, mix_post_g, mem_pre_g, mem_kv_g, w_mq, w_mkv, w_mo, mem_post_g, ffn2_pre_g, ffn2_w_in, ffn2_w_down, ffn2_post_g, loss_target, m_ffn1_pre_g, m_ffn1_w_in, m_ffn1_w_down, m_ffn1_post_g, m_mix_pre_g, m_w_in, m_hg_lb_logits, m_hg_norm_g, m_fox_f_bias, m_w_branch_a, m_w_branch_b, m_b_gate, m_w_out, m_mix_post_g, m_mem_pre_g, m_mem_kv_g, m_w_mq, m_w_mkv, m_w_mo, m_mem_post_g, m_ffn2_pre_g, m_ffn2_w_in, m_ffn2_w_down, m_ffn2_post_g, v_ffn1_pre_g, v_ffn1_w_in, v_ffn1_w_down, v_ffn1_post_g, v_mix_pre_g, v_w_in, v_hg_lb_logits, v_hg_norm_g, v_fox_f_bias, v_w_branch_a, v_w_branch_b, v_b_gate, v_w_out, v_mix_post_g, v_mem_pre_g, v_mem_kv_g, v_w_mq, v_w_mkv, v_w_mo, v_mem_post_g, v_ffn2_pre_g, v_ffn2_w_in, v_ffn2_w_down, v_ffn2_post_g):
    return _train_step(dict(locals()))
```

```python
import functools
import math

import jax
import jax.numpy as jnp
from jax import lax
from jax.experimental import pallas as pl
from jax.experimental.pallas import tpu as pltpu

F32 = jnp.float32
BF16 = jnp.bfloat16
MESH = pl.DeviceIdType.MESH

EPS = 1e-6
HEADS = 8
DH = 128
MEM_HEADS = 4
CHUNK = 128
HALF = CHUNK // 2
SWIGLU_TILE = 256
LANES = 128
PACK_COLS = 1024
ROW_TILE = 1024
SEQ_BLOCK = 2048
CUMSUM_BLOCK = 512
XATTN_TILE = 2048
ATTN_TILE = 2048
ATTN_ROWS = 256
EXP_CLAMP = 80.0
NEG_BIG = -1e30

ADAM_LR, ADAM_B1, ADAM_B2, ADAM_EPS, ADAM_WD, ADAM_STEP = 0.001, 0.9, 0.999, 1e-08, 0.01, 10

VMEM_LIMIT = 48 * 1024 * 1024

_DN = {
    "nn": (((1,), (0,)), ((), ())),
    "nt": (((1,), (1,)), ((), ())),
    "tn": (((0,), (0,)), ((), ())),
}

BIG = ["ffn1_w_in", "ffn1_w_down", "w_in", "w_branch_a", "w_branch_b", "w_out", "w_mq", "w_mkv", "w_mo",
       "ffn2_w_in", "ffn2_w_down"]
COL_SHARDED = {"ffn1_w_in", "w_in", "w_mkv", "ffn2_w_in"}
SMALL = ["ffn1_pre_g", "ffn1_post_g", "mix_pre_g", "hg_lb_logits", "hg_norm_g", "fox_f_bias", "b_gate",
         "mix_post_g", "mem_pre_g", "mem_kv_g", "mem_post_g", "ffn2_pre_g", "ffn2_post_g"]
WEIGHTS = ["ffn1_pre_g", "ffn1_w_in", "ffn1_w_down", "ffn1_post_g", "mix_pre_g", "w_in", "hg_lb_logits",
           "hg_norm_g", "fox_f_bias", "w_branch_a", "w_branch_b", "b_gate", "w_out", "mix_post_g", "mem_pre_g",
           "mem_kv_g", "w_mq", "w_mkv", "w_mo", "mem_post_g", "ffn2_pre_g", "ffn2_w_in", "ffn2_w_down",
           "ffn2_post_g"]


def _dot(a, b, mode="nn"):
    return lax.dot_general(a, b, _DN[mode], preferred_element_type=F32)


def _sig(x):
    return 1.0 / (1.0 + jnp.exp(-x))


def _sig_approx(x):
    return pl.reciprocal(1.0 + jnp.exp(-x), approx=True)


def _params(*dims):
    return pltpu.CompilerParams(dimension_semantics=dims if dims else None, vmem_limit_bytes=VMEM_LIMIT)


def _tile(dim, pref):
    if dim <= pref:
        return dim
    t = (pref // LANES) * LANES
    while t >= LANES:
        if dim % t == 0:
            return t
        t -= LANES
    raise ValueError(f"no tile for {dim}")


def _colsum(x):
    return jnp.sum(x, axis=0, keepdims=True)


def _rowsum(x):
    return jnp.sum(x, axis=1, keepdims=True)


def _iota(shape, axis):
    return lax.broadcasted_iota(jnp.int32, shape, axis)


def _pick_row(x, r):
    return _colsum(jnp.where(_iota(x.shape, 0) == r, x, 0.0))


def _tri_dot(tri, x):
    hi = x.astype(BF16)
    r1 = x - hi.astype(F32)
    mid = r1.astype(BF16)
    lo = (r1 - mid.astype(F32)).astype(BF16)
    return _dot(tri, hi) + _dot(tri, mid) + _dot(tri, lo)


_MM_TILES = {"nn": (2048, 512, 1024), "nt": (512, 1024, 4096), "tn": (1024, 1024, 2048)}


def _host_call(body, name, grid, in_specs, out_specs, out_shape, scratch_shapes, dims, args, hosted=None):
    if hosted is None:
        results = pl.pallas_call(body, name=name, grid=grid, in_specs=in_specs, out_specs=out_specs, out_shape=out_shape,
                                 scratch_shapes=scratch_shapes, compiler_params=_params(*dims))(*args)
        return list(results), []
    n_in, n_out, n_sc = len(in_specs), len(out_specs), len(scratch_shapes)
    h_in, h_out = len(hosted.inputs), len(hosted.out_shapes)

    def wrapped(*refs):
        cut = [n_in, h_in, n_out, h_out, n_sc]
        at = [sum(cut[:i]) for i in range(len(cut) + 1)]
        ins, hin, outs, hout, scr = (refs[at[i]:at[i + 1]] for i in range(len(cut)))
        hsems = refs[at[-1]:]
        ids = [pl.program_id(d) for d in range(len(grid))]
        first = functools.reduce(jnp.logical_and, [i == 0 for i in ids])
        last = functools.reduce(jnp.logical_and, [i == g - 1 for i, g in zip(ids, grid)])

        @pl.when(first)
        def _():
            hosted.start(hin, hout, hsems)

        body(*ins, *outs, *scr)

        @pl.when(last)
        def _():
            hosted.wait(hin, hout, hsems)

    results = pl.pallas_call(
        wrapped, name=name, grid=grid, in_specs=list(in_specs) + [_HBM] * h_in,
        out_specs=list(out_specs) + [_HBM] * h_out, out_shape=list(out_shape) + list(hosted.out_shapes),
        scratch_shapes=list(scratch_shapes) + list(hosted.scratch), input_output_aliases=hosted.aliases(n_in, n_out),
        compiler_params=_params(*dims))(*args, *hosted.inputs)
    return list(results[:n_out]), list(results[n_out:])


def _mm(a, b, mode, out_dtype, name, add=None, tm=None, tn=None, tk=None, hosted=None):
    tm, tn, tk = (given or pref for given, pref in zip((tm, tn, tk), _MM_TILES[mode]))
    if mode == "nn":
        (M, K), (K2, N) = a.shape, b.shape
    elif mode == "nt":
        (M, K), (N, K2) = a.shape, b.shape
    else:
        (K, M), (K2, N) = a.shape, b.shape
    assert K == K2, (name, a.shape, b.shape)
    tm, tn, tk = _tile(M, tm), _tile(N, tn), _tile(K, tk)
    nk = K // tk
    if mode == "tn":
        a_spec = pl.BlockSpec((tk, tm), lambda i, j, k: (k, i))
    else:
        a_spec = pl.BlockSpec((tm, tk), lambda i, j, k: (i, k))
    if mode == "nt":
        b_spec = pl.BlockSpec((tn, tk), lambda i, j, k: (j, k))
    else:
        b_spec = pl.BlockSpec((tk, tn), lambda i, j, k: (k, j))
    o_spec = pl.BlockSpec((tm, tn), lambda i, j, k: (i, j))
    has_add = add is not None

    def body(*refs):
        a_ref, b_ref = refs[0], refs[1]
        c_ref = refs[2] if has_add else None
        o_ref = refs[3] if has_add else refs[2]
        part = _dot(a_ref[...], b_ref[...], mode)
        if nk == 1:
            if has_add:
                part = part + c_ref[...]
            o_ref[...] = part.astype(o_ref.dtype)
            return
        acc_ref = refs[-1]
        k = pl.program_id(2)

        @pl.when(k == 0)
        def _():
            acc_ref[...] = part + c_ref[...] if has_add else part

        @pl.when(k > 0)
        def _():
            acc_ref[...] += part

        @pl.when(k == nk - 1)
        def _():
            o_ref[...] = acc_ref[...].astype(o_ref.dtype)

    in_specs = [a_spec, b_spec] + ([o_spec] if has_add else [])
    args = (a, b) + ((add,) if has_add else ())
    (out,), landed = _host_call(
        body, name, (M // tm, N // tn, nk), in_specs, [o_spec], [jax.ShapeDtypeStruct((M, N), out_dtype)],
        [pltpu.VMEM((tm, tn), F32)] if nk > 1 else [], ("parallel", "parallel", "arbitrary"), args, hosted)
    return out if hosted is None else (out, landed)


def _rms_fwd(x, g, name, out_dtype=BF16):
    T, D = x.shape
    tr = _tile(T, ROW_TILE)

    def body(x_ref, g_ref, o_ref):
        xv = x_ref[...]
        r = lax.rsqrt(jnp.mean(xv * xv, axis=-1, keepdims=True) + EPS)
        o_ref[...] = (xv * r * g_ref[...]).astype(o_ref.dtype)

    return pl.pallas_call(
        body, name=name, grid=(T // tr,),
        in_specs=[pl.BlockSpec((tr, D), lambda i: (i, 0)), pl.BlockSpec((1, D), lambda i: (0, 0))],
        out_specs=pl.BlockSpec((tr, D), lambda i: (i, 0)),
        out_shape=jax.ShapeDtypeStruct((T, D), out_dtype), compiler_params=_params("parallel"),
    )(x, g)


def _resid_rms(x, z, g, scale, name):
    T, D = x.shape
    tr = _tile(T, ROW_TILE)

    def body(x_ref, z_ref, g_ref, o_ref):
        zv = z_ref[...]
        r = lax.rsqrt(jnp.mean(zv * zv, axis=-1, keepdims=True) + EPS)
        o_ref[...] = x_ref[...] + scale * (zv * r * g_ref[...])

    row = pl.BlockSpec((tr, D), lambda i: (i, 0))
    return pl.pallas_call(
        body, name=name, grid=(T // tr,), in_specs=[row, row, pl.BlockSpec((1, D), lambda i: (0, 0))],
        out_specs=row, out_shape=jax.ShapeDtypeStruct((T, D), F32), compiler_params=_params("parallel"),
    )(x, z, g)


def _final_loss(x, z, g, scale, target, name):
    T, D = x.shape
    tr = _tile(T, ROW_TILE)

    def body(x_ref, z_ref, g_ref, t_ref, dx_ref, acc_ref):
        @pl.when(pl.program_id(0) == 0)
        def _():
            acc_ref[...] = jnp.zeros_like(acc_ref)

        zv = z_ref[...]
        r = lax.rsqrt(jnp.mean(zv * zv, axis=-1, keepdims=True) + EPS)
        e = x_ref[...] + scale * (zv * r * g_ref[...]) - t_ref[...]
        dx_ref[...] = e * (1.0 / D)
        acc_ref[...] += _colsum(_rowsum(e * e))

    row = pl.BlockSpec((tr, D), lambda i: (i, 0))
    return pl.pallas_call(
        body, name=name, grid=(T // tr,), in_specs=[row, row, pl.BlockSpec((1, D), lambda i: (0, 0)), row],
        out_specs=[row, pl.BlockSpec((8, LANES), lambda i: (0, 0))],
        out_shape=[jax.ShapeDtypeStruct((T, D), F32), jax.ShapeDtypeStruct((8, LANES), F32)],
        compiler_params=_params("arbitrary"),
    )(x, z, g, target)


def _rms_bwd(xin, g, dy, scale, name, out_dtype, resid=None):
    T, D = xin.shape
    tr = _tile(T, ROW_TILE)
    has_resid = resid is not None

    def body(*refs):
        x_ref, g_ref, dy_ref = refs[:3]
        r_ref = refs[3] if has_resid else None
        dx_ref, dg_ref = refs[-2], refs[-1]

        @pl.when(pl.program_id(0) == 0)
        def _():
            dg_ref[...] = jnp.zeros_like(dg_ref)

        xv = x_ref[...]
        r = lax.rsqrt(jnp.mean(xv * xv, axis=-1, keepdims=True) + EPS)
        xh = xv * r
        dyv = dy_ref[...].astype(F32) * scale
        dxh = dyv * g_ref[...]
        dx = r * (dxh - xh * jnp.mean(dxh * xh, axis=-1, keepdims=True))
        if has_resid:
            dx = dx + r_ref[...]
        dx_ref[...] = dx.astype(dx_ref.dtype)
        dg_ref[...] += _colsum(dyv * xh)

    row = pl.BlockSpec((tr, D), lambda i: (i, 0))
    vec = pl.BlockSpec((1, D), lambda i: (0, 0))
    return pl.pallas_call(
        body, name=name, grid=(T // tr,), in_specs=[row, vec, row] + ([row] if has_resid else []),
        out_specs=[row, vec],
        out_shape=[jax.ShapeDtypeStruct((T, D), out_dtype), jax.ShapeDtypeStruct((1, D), F32)],
        compiler_params=_params("arbitrary"),
    )(*((xin, g, dy) + ((resid,) if has_resid else ())))


def _mm_swiglu(h, w_in, name, hosted=None):
    T, K = h.shape
    F2 = w_in.shape[1]
    tf = SWIGLU_TILE
    tm = _tile(T, _MM_TILES["nn"][0])

    def body(h_ref, w_ref, u_ref, a_ref):
        u = _dot(h_ref[...], w_ref[...])
        u_ref[...] = u.astype(u_ref.dtype)
        gate, up = u[:, :tf], u[:, tf:]
        a_ref[...] = (gate * _sig_approx(gate) * up).astype(a_ref.dtype)

    (u, a), landed = _host_call(
        body, name, (T // tm, F2 // (2 * tf)),
        [pl.BlockSpec((tm, K), lambda i, j: (i, 0)), pl.BlockSpec((K, 2 * tf), lambda i, j: (0, j))],
        [pl.BlockSpec((tm, 2 * tf), lambda i, j: (i, j)), pl.BlockSpec((tm, tf), lambda i, j: (i, j))],
        [jax.ShapeDtypeStruct((T, F2), BF16), jax.ShapeDtypeStruct((T, F2 // 2), BF16)], [],
        ("parallel", "parallel"), (h, w_in), hosted)
    return u, a, landed


def _mm_swiglu_bwd(dz, w_down, u, name):
    T, D = dz.shape
    F = w_down.shape[0]
    tf = SWIGLU_TILE
    tm = _tile(T, _MM_TILES["nn"][0])

    def body(dz_ref, w_ref, u_ref, o_ref):
        d = _dot(dz_ref[...], w_ref[...], "nt")
        gate = u_ref[:, :tf].astype(F32)
        up = u_ref[:, tf:].astype(F32)
        s = _sig_approx(gate)
        o_ref[:, :tf] = (d * up * (s * (1.0 + gate * (1.0 - s)))).astype(o_ref.dtype)
        o_ref[:, tf:] = (d * gate * s).astype(o_ref.dtype)

    return pl.pallas_call(
        body, name=name, grid=(T // tm, F // tf),
        in_specs=[pl.BlockSpec((tm, D), lambda i, j: (i, 0)), pl.BlockSpec((tf, D), lambda i, j: (j, 0)),
                  pl.BlockSpec((tm, 2 * tf), lambda i, j: (i, j))],
        out_specs=pl.BlockSpec((tm, 2 * tf), lambda i, j: (i, j)),
        out_shape=jax.ShapeDtypeStruct((T, 2 * F), BF16), compiler_params=_params("parallel", "parallel"),
    )(dz, w_down, u)


def _hgout_fwd(o_a, pm, g, name):
    T, D = o_a.shape
    tr = _tile(T, ROW_TILE)

    def body(o_ref, ga_ref, g_ref, out_ref):
        ov = o_ref[...]
        r = lax.rsqrt(jnp.mean(ov * ov, axis=-1, keepdims=True) + EPS)
        ga = ga_ref[...].astype(F32)
        out_ref[...] = (ov * r * g_ref[...] * (ga * _sig(ga))).astype(out_ref.dtype)

    row = pl.BlockSpec((tr, D), lambda i: (i, 0))
    return pl.pallas_call(
        body, name=name, grid=(T // tr,),
        in_specs=[row, pl.BlockSpec((tr, D), lambda i: (i, 3)), pl.BlockSpec((1, D), lambda i: (0, 0))],
        out_specs=row, out_shape=jax.ShapeDtypeStruct((T, D), BF16), compiler_params=_params("parallel"),
    )(o_a, pm, g)


def _hgout_bwd(o_a, pm, g, d_out, name):
    T, D = o_a.shape
    tr = _tile(T, ROW_TILE)

    def body(o_ref, ga_ref, g_ref, d_ref, do_ref, dga_ref, dg_ref):
        @pl.when(pl.program_id(0) == 0)
        def _():
            dg_ref[...] = jnp.zeros_like(dg_ref)

        ov = o_ref[...]
        r = lax.rsqrt(jnp.mean(ov * ov, axis=-1, keepdims=True) + EPS)
        oh = ov * r
        ga = ga_ref[...].astype(F32)
        s = _sig(ga)
        d = d_ref[...].astype(F32)
        dn = d * (ga * s)
        dga_ref[...] = (d * (oh * g_ref[...]) * (s * (1.0 + ga * (1.0 - s)))).astype(dga_ref.dtype)
        dxh = dn * g_ref[...]
        do_ref[...] = (r * (dxh - oh * jnp.mean(dxh * oh, axis=-1, keepdims=True))).astype(do_ref.dtype)
        dg_ref[...] += _colsum(dn * oh)

    row = pl.BlockSpec((tr, D), lambda i: (i, 0))
    vec = pl.BlockSpec((1, D), lambda i: (0, 0))
    return pl.pallas_call(
        body, name=name, grid=(T // tr,), in_specs=[row, pl.BlockSpec((tr, D), lambda i: (i, 3)), vec, row],
        out_specs=[row, row, vec],
        out_shape=[jax.ShapeDtypeStruct((T, D), BF16), jax.ShapeDtypeStruct((T, D), BF16),
                   jax.ShapeDtypeStruct((1, D), F32)],
        compiler_params=_params("arbitrary"),
    )(o_a, pm, g, d_out)


def _merge_fwd(ya, yb, pg, bg, name):
    T, D = ya.shape
    tr = _tile(T, 256)

    def body(ya_ref, yb_ref, pg_ref, bg_ref, o_ref):
        g0 = _sig(pg_ref[:, :D].astype(F32) + bg_ref[:, :D])
        g1 = _sig(pg_ref[:, D:].astype(F32) + bg_ref[:, D:])
        o_ref[...] = (g0 * ya_ref[...].astype(F32) + g1 * yb_ref[...].astype(F32)).astype(o_ref.dtype)

    row = pl.BlockSpec((tr, D), lambda i: (i, 0))
    return pl.pallas_call(
        body, name=name, grid=(T // tr,),
        in_specs=[row, row, pl.BlockSpec((tr, 2 * D), lambda i: (i, 0)), pl.BlockSpec((1, 2 * D), lambda i: (0, 0))],
        out_specs=row, out_shape=jax.ShapeDtypeStruct((T, D), BF16), compiler_params=_params("parallel"),
    )(ya, yb, pg, bg)


def _merge_bwd(dy, ya, yb, pg, bg, name):
    T, D = ya.shape
    tr = _tile(T, 256)

    def body(dy_ref, ya_ref, yb_ref, pg_ref, bg_ref, dya_ref, dyb_ref, dpg_ref, dbg_ref):
        @pl.when(pl.program_id(0) == 0)
        def _():
            dbg_ref[...] = jnp.zeros_like(dbg_ref)

        d = dy_ref[...].astype(F32)
        g0 = _sig(pg_ref[:, :D].astype(F32) + bg_ref[:, :D])
        g1 = _sig(pg_ref[:, D:].astype(F32) + bg_ref[:, D:])
        dya_ref[...] = (d * g0).astype(dya_ref.dtype)
        dyb_ref[...] = (d * g1).astype(dyb_ref.dtype)
        dg0 = d * ya_ref[...].astype(F32) * (g0 * (1.0 - g0))
        dg1 = d * yb_ref[...].astype(F32) * (g1 * (1.0 - g1))
        dpg_ref[:, :D] = dg0.astype(dpg_ref.dtype)
        dpg_ref[:, D:] = dg1.astype(dpg_ref.dtype)
        dbg_ref[:, :D] += _colsum(dg0)
        dbg_ref[:, D:] += _colsum(dg1)

    row = pl.BlockSpec((tr, D), lambda i: (i, 0))
    wide = pl.BlockSpec((tr, 2 * D), lambda i: (i, 0))
    wvec = pl.BlockSpec((1, 2 * D), lambda i: (0, 0))
    return pl.pallas_call(
        body, name=name, grid=(T // tr,), in_specs=[row, row, row, wide, wvec],
        out_specs=[row, row, wide, wvec],
        out_shape=[jax.ShapeDtypeStruct((T, D), BF16), jax.ShapeDtypeStruct((T, D), BF16),
                   jax.ShapeDtypeStruct((T, 2 * D), BF16), jax.ShapeDtypeStruct((1, 2 * D), F32)],
        compiler_params=_params("arbitrary"),
    )(dy, ya, yb, pg, bg)


def _hgrn_chunk_terms(q, fl, lb, tri):
    shape = q.shape
    row = _iota(shape, 0)
    sg = _sig(fl)
    f = lb + (1.0 - lb) * sg
    k = 1.0 - f
    b = _tri_dot(tri, jnp.log(f))
    ref1 = jnp.where(row < HALF, _pick_row(b, HALF // 2), _pick_row(b, HALF + HALF // 2))
    b_half = _pick_row(b, HALF - 1)
    b_last = _pick_row(b, CHUNK - 1)
    sq = _sig(q)
    qs = q * sq
    e_q1 = jnp.exp(jnp.minimum(b - ref1, EXP_CLAMP))
    e_k1 = jnp.exp(jnp.minimum(ref1 - b, EXP_CLAMP))
    e_q2 = jnp.exp(jnp.minimum(b - b_half, 0.0))
    e_k2 = jnp.exp(jnp.minimum(b_half - b, 0.0))
    e_b = jnp.exp(b)
    e_kd = jnp.exp(b_last - b)
    return dict(sg=sg, f=f, k=k, sq=sq, qs=qs, e_q1=e_q1, e_k1=e_k1, e_q2=e_q2, e_k2=e_k2, e_b=e_b, e_kd=e_kd,
                e_last=jnp.exp(b_last))


def _hgrn_masks():
    r = _iota((CHUNK, CHUNK), 0)
    c = _iota((CHUNK, CHUNK), 1)
    causal = r >= c
    same = (r < HALF) == (c < HALF)
    return causal, causal & same, (r >= HALF) & (c < HALF)


def _softmax_lb(lbl_ref):
    l0, l1 = lbl_ref[0, 0], lbl_ref[1, 0]
    mx = jnp.maximum(l0, l1)
    e0, e1 = jnp.exp(l0 - mx), jnp.exp(l1 - mx)
    return e0 / (e0 + e1)


def _hgrn_fwd(pm, lbl, name):
    T = pm.shape[0]
    tb = _tile(T, SEQ_BLOCK)
    nc = tb // CHUNK

    def body(q_ref, f_ref, i_ref, lbl_ref, o_ref, st_ref, s_sc):
        @pl.when(pl.program_id(1) == 0)
        def _():
            s_sc[...] = jnp.zeros_like(s_sc)

        lb = _softmax_lb(lbl_ref)
        causal, m1, m2 = _hgrn_masks()
        tri = jnp.where(causal, 1.0, 0.0).astype(BF16)
        parts = []
        for ci in range(nc):
            sl = pl.ds(ci * CHUNK, CHUNK)
            t = _hgrn_chunk_terms(q_ref[sl, :].astype(F32), f_ref[sl, :].astype(F32), lb, tri)
            iv = i_ref[sl, :]
            a1 = _dot((t["qs"] * t["e_q1"]).astype(BF16), (t["k"] * t["e_k1"]).astype(BF16), "nt")
            a2 = _dot((t["qs"] * t["e_q2"]).astype(BF16), (t["k"] * t["e_k2"]).astype(BF16), "nt")
            a = jnp.where(m1, a1, 0.0) + jnp.where(m2, a2, 0.0)
            parts.append((_dot(a.astype(BF16), iv), (t["qs"] * t["e_b"]).astype(BF16),
                          _dot(iv, (t["k"] * t["e_kd"]).astype(BF16), "tn"), t["e_last"]))
        st = s_sc[...]
        for ci, (o_intra, qi, grow, e_last) in enumerate(parts):
            st_ref[0, ci] = st
            o_ref[pl.ds(ci * CHUNK, CHUNK), :] = o_intra + _dot(qi, st.astype(BF16), "nt")
            st = e_last * st + grow
        s_sc[...] = st

    blk = lambda off: pl.BlockSpec((tb, DH), lambda h, b: (b, off + h))
    return pl.pallas_call(
        body, name=name, grid=(HEADS, T // tb),
        in_specs=[blk(0), blk(HEADS), blk(2 * HEADS), pl.BlockSpec((2, 1, 1, DH), lambda h, b: (0, h, 0, 0))],
        out_specs=[pl.BlockSpec((tb, DH), lambda h, b: (b, h)),
                   pl.BlockSpec((1, nc, DH, DH), lambda h, b: (h, b, 0, 0))],
        out_shape=[jax.ShapeDtypeStruct((T, HEADS * DH), F32),
                   jax.ShapeDtypeStruct((HEADS, T // CHUNK, DH, DH), F32)],
        scratch_shapes=[pltpu.VMEM((DH, DH), F32)],
        compiler_params=_params("parallel", "arbitrary"),
    )(pm, pm, pm, lbl)


def _hgrn_bwd(pm, lbl, states, do, name):
    T = pm.shape[0]
    tb = _tile(T, SEQ_BLOCK)
    nc = tb // CHUNK
    nb = T // tb

    def body(q_ref, f_ref, i_ref, lbl_ref, st_ref, do_ref, dq_ref, df_ref, di_ref, dl_ref, ds_sc, dlb_sc):
        @pl.when(pl.program_id(1) == 0)
        def _():
            ds_sc[...] = jnp.zeros_like(ds_sc)
            dlb_sc[...] = jnp.zeros_like(dlb_sc)

        lb = _softmax_lb(lbl_ref)
        causal, m1, m2 = _hgrn_masks()
        tri = jnp.where(causal, 1.0, 0.0).astype(BF16)
        tri_rev = jnp.where(_iota((CHUNK, CHUNK), 0) <= _iota((CHUNK, CHUNK), 1), 1.0, 0.0).astype(BF16)
        last_row = _iota((CHUNK, DH), 0) == CHUNK - 1
        dsn = ds_sc[...]
        dlb = jnp.zeros((1, DH), F32)
        for ci in reversed(range(nc)):
            sl = pl.ds(ci * CHUNK, CHUNK)
            q = q_ref[sl, :].astype(F32)
            t = _hgrn_chunk_terms(q, f_ref[sl, :].astype(F32), lb, tri)
            iv = i_ref[sl, :]
            dov = do_ref[sl, :]
            qe1, ke1 = t["qs"] * t["e_q1"], t["k"] * t["e_k1"]
            qe2, ke2 = t["qs"] * t["e_q2"], t["k"] * t["e_k2"]
            qi, kd = t["qs"] * t["e_b"], t["k"] * t["e_kd"]
            qe1b, ke1b, qe2b, ke2b = qe1.astype(BF16), ke1.astype(BF16), qe2.astype(BF16), ke2.astype(BF16)
            a = jnp.where(m1, _dot(qe1b, ke1b, "nt"), 0.0) + jnp.where(m2, _dot(qe2b, ke2b, "nt"), 0.0)
            st = st_ref[0, ci]
            dsnb = dsn.astype(BF16)
            da = _dot(dov, iv, "nt")
            da1 = jnp.where(m1, da, 0.0).astype(BF16)
            da2 = jnp.where(m2, da, 0.0).astype(BF16)
            di_ref[sl, :] = (_dot(a.astype(BF16), dov, "tn") + _dot(kd.astype(BF16), dsnb, "nt")).astype(di_ref.dtype)
            dqe1, dke1 = _dot(da1, ke1b), _dot(da1, qe1b, "tn")
            dqe2, dke2 = _dot(da2, ke2b), _dot(da2, qe2b, "tn")
            dqi = _dot(dov, st.astype(BF16))
            dkd = _dot(iv, dsnb)
            ds_before = t["e_last"] * dsn + _dot(dov, qi.astype(BF16), "tn")
            dqs =dqe1 * t["e_q1"] + dqe2 * t["e_q2"] + dqi * t["e_b"]
            dk = dke1 * t["e_k1"] + dke2 * t["e_k2"] + dkd * t["e_kd"]
            qib, kdb = qi.astype(BF16).astype(F32), kd.astype(BF16).astype(F32)
            db = (dqe1 * qe1b.astype(F32) - dke1 * ke1b.astype(F32) + dqe2 * qe2b.astype(F32)
                  - dke2 * ke2b.astype(F32) + dqi * qib - dkd * kdb)
            extra = _colsum(dkd * kdb) + t["e_last"] * _colsum(dsn * st)
            db = db + jnp.where(last_row, extra, 0.0)
            dlf = _tri_dot(tri_rev, db)
            dfv = dlf / t["f"] - dk
            sg = t["sg"]
            df_ref[sl, :] = (dfv * (1.0 - lb) * sg * (1.0 - sg)).astype(df_ref.dtype)
            dlb = dlb + _colsum(dfv * (1.0 - sg))
            sq = t["sq"]
            dq_ref[sl, :] = (dqs * (sq * (1.0 + q * (1.0 - sq)))).astype(dq_ref.dtype)
            dsn = ds_before
        ds_sc[...] = dsn
        dlb_sc[...] += dlb

        @pl.when(pl.program_id(1) == nb - 1)
        def _():
            dl0 = dlb_sc[...] * lb * (1.0 - lb)
            dl_ref[0, 0] = dl0
            dl_ref[1, 0] = -dl0

    blk = lambda off: pl.BlockSpec((tb, DH), lambda h, b: (nb - 1 - b, off + h))
    lspec = pl.BlockSpec((2, 1, 1, DH), lambda h, b: (0, h, 0, 0))
    out_blk = pl.BlockSpec((tb, DH), lambda h, b: (nb - 1 - b, h))
    D = HEADS * DH
    return pl.pallas_call(
        body, name=name, grid=(HEADS, nb),
        in_specs=[blk(0), blk(HEADS), blk(2 * HEADS), lspec,
                  pl.BlockSpec((1, nc, DH, DH), lambda h, b: (h, nb - 1 - b, 0, 0)), out_blk],
        out_specs=[out_blk, out_blk, out_blk, lspec],
        out_shape=[jax.ShapeDtypeStruct((T, D), BF16)] * 3 + [jax.ShapeDtypeStruct((2, HEADS, 1, DH), F32)],
        scratch_shapes=[pltpu.VMEM((DH, DH), F32), pltpu.VMEM((1, DH), F32)],
        compiler_params=_params("parallel", "arbitrary"),
    )(pm, pm, pm, lbl, states, do)


def _log_sigmoid(x):
    return jnp.minimum(x, 0.0) - jnp.log(1.0 + jnp.exp(-jnp.abs(x)))


def _fox_cumsum(pf, bias, name):
    T = pf.shape[0]
    tb = _tile(T, CUMSUM_BLOCK)

    def body(x_ref, b_ref, c_ref, carry):
        @pl.when(pl.program_id(0) == 0)
        def _():
            carry[...] = jnp.zeros_like(carry)

        tri = jnp.where(_iota((tb, tb), 0) >= _iota((tb, tb), 1), 1.0, 0.0).astype(BF16)
        c = _tri_dot(tri, _log_sigmoid(x_ref[...] + b_ref[...])) + carry[...]
        c_ref[...] = c
        carry[...] = _pick_row(c, tb - 1)

    row = pl.BlockSpec((tb, LANES), lambda i: (i, 0))
    return pl.pallas_call(
        body, name=name, grid=(T // tb,), in_specs=[row, pl.BlockSpec((1, LANES), lambda i: (0, 0))],
        out_specs=row, out_shape=jax.ShapeDtypeStruct((T, LANES), F32),
        scratch_shapes=[pltpu.VMEM((1, LANES), F32)], compiler_params=_params("arbitrary"),
    )(pf, bias)


def _fox_dcum(dc, pf, bias, name):
    T = pf.shape[0]
    tb = _tile(T, CUMSUM_BLOCK)
    nb = T // tb

    def body(dc_ref, x_ref, b_ref, dx_ref, db_ref, carry):
        @pl.when(pl.program_id(0) == 0)
        def _():
            carry[...] = jnp.zeros_like(carry)
            db_ref[...] = jnp.zeros_like(db_ref)

        tri_rev = jnp.where(_iota((tb, tb), 0) <= _iota((tb, tb), 1), 1.0, 0.0).astype(BF16)
        dls = _tri_dot(tri_rev, dc_ref[...]) + carry[...]
        carry[...] = _pick_row(dls, 0)
        dx = dls * (1.0 - _sig(x_ref[...] + b_ref[...]))
        dx_ref[...] = dx
        db_ref[...] += _colsum(dx)

    row = pl.BlockSpec((tb, LANES), lambda i: (nb - 1 - i, 0))
    vec = pl.BlockSpec((1, LANES), lambda i: (0, 0))
    return pl.pallas_call(
        body, name=name, grid=(nb,), in_specs=[row, row, vec], out_specs=[row, vec],
        out_shape=[jax.ShapeDtypeStruct((T, LANES), F32), jax.ShapeDtypeStruct((1, LANES), F32)],
        scratch_shapes=[pltpu.VMEM((1, LANES), F32)], compiler_params=_params("arbitrary"),
    )(dc, pf, bias)


_Q_OFF, _K_OFF, _V_OFF = 4 * HEADS, 5 * HEADS, 6 * HEADS


def _causal_pairs(nq, by_key):
    if by_key:
        pairs = [(i, j) for j in range(nq) for i in range(j, nq)]
    else:
        pairs = [(i, j) for i in range(nq) for j in range(i + 1)]
    return jnp.asarray([p[0] for p in pairs], jnp.int32), jnp.asarray([p[1] for p in pairs], jnp.int32)


def _fox_logits(q, k, ck, row0, masked):
    s = _dot(q, k, "nt") - ck
    if masked:
        s = jnp.where(_iota(s.shape, 0) + row0 >= _iota(s.shape, 1), s, NEG_BIG)
    return s


def _ones_column(rows):
    return jnp.where(_iota((rows, DH), 1) == 0, 1.0, 0.0).astype(BF16)


def _fox_fwd(pm, c_col, c_row, name):
    T = pm.shape[0]
    tq = _tile(T, ATTN_TILE)
    nq = T // tq
    rg = min(ATTN_ROWS, tq)
    qi_tab, kj_tab = _causal_pairs(nq, by_key=False)

    def body(qi_ref, kj_ref, q_ref, k_ref, v_ref, cq_ref, ck_ref, o_ref, lse_ref, m_sc, acc_sc):
        t = pl.program_id(1)
        i, j = qi_ref[t], kj_ref[t]

        @pl.when(j == 0)
        def _():
            m_sc[...] = jnp.full_like(m_sc, NEG_BIG)
            acc_sc[...] = jnp.zeros_like(acc_sc)

        def step(diag):
            m_all, acc_all = m_sc[...], acc_sc[...]
            ones = _ones_column(tq)
            ms, accs = [], []
            for r in range(tq // rg):
                rows = slice(r * rg, (r + 1) * rg)
                w = (r + 1) * rg if diag else tq
                cq = cq_ref[0, rows, :]
                s = _fox_logits(q_ref[rows, :], k_ref[:w, :], ck_ref[0, :, :w], r * rg, diag)
                m_old = m_all[rows, :]
                m_new = jnp.maximum(m_old, jnp.max(s, axis=1, keepdims=True) + cq)
                alpha = jnp.exp(m_old - m_new)
                p = jnp.exp(s - (m_new - cq)).astype(BF16)
                v_one = jnp.concatenate([v_ref[:w, :], ones[:w, :]], axis=1)
                ms.append(m_new)
                accs.append(alpha * acc_all[rows, :] + _dot(p, v_one))
            m_sc[...] = jnp.concatenate(ms, axis=0)
            acc_sc[...] = jnp.concatenate(accs, axis=0)

        @pl.when(j < i)
        def _():
            step(False)

        @pl.when(j == i)
        def _():
            step(True)
            acc = acc_sc[...]
            denom = acc[:, DH:DH + 1]
            o_ref[...] = (acc[:, :DH] / denom).astype(o_ref.dtype)
            lse_ref[0] = m_sc[...] + jnp.log(denom)

    kv = lambda off: pl.BlockSpec((tq, DH), lambda h, t, qi, kj: (kj[t], off + h))
    col = pl.BlockSpec((1, tq, 1), lambda h, t, qi, kj: (h, qi[t], 0))
    grid_spec = pltpu.PrefetchScalarGridSpec(
        num_scalar_prefetch=2, grid=(HEADS, qi_tab.shape[0]),
        in_specs=[pl.BlockSpec((tq, DH), lambda h, t, qi, kj: (qi[t], _Q_OFF + h)), kv(_K_OFF), kv(_V_OFF), col,
                  pl.BlockSpec((1, 1, tq), lambda h, t, qi, kj: (h, 0, kj[t]))],
        out_specs=[pl.BlockSpec((tq, DH), lambda h, t, qi, kj: (qi[t], h)), col],
        scratch_shapes=[pltpu.VMEM((tq, 1), F32), pltpu.VMEM((tq, 2 * DH), F32)])
    return pl.pallas_call(
        body, name=name, grid_spec=grid_spec,
        out_shape=[jax.ShapeDtypeStruct((T, HEADS * DH), BF16), jax.ShapeDtypeStruct((HEADS, T, 1), F32)],
        compiler_params=_params("parallel", "arbitrary"),
    )(qi_tab, kj_tab, pm, pm, pm, c_col, c_row)


def _fox_delta(do, o, name):
    T, D = o.shape
    tr = _tile(T, ROW_TILE)

    def body(do_ref, o_ref, d_ref):
        prod = do_ref[...].astype(F32) * o_ref[...].astype(F32)
        for h in range(HEADS):
            d_ref[h] = _rowsum(prod[:, h * DH:(h + 1) * DH])

    row = pl.BlockSpec((tr, D), lambda i: (i, 0))
    return pl.pallas_call(
        body, name=name, grid=(T // tr,), in_specs=[row, row],
        out_specs=pl.BlockSpec((HEADS, tr, 1), lambda i: (0, i, 0)),
        out_shape=jax.ShapeDtypeStruct((HEADS, T, 1), F32), compiler_params=_params("parallel"),
    )(do, o)


def _fox_bwd(pm, c_col, c_row, do, lse, delta, name):
    T = pm.shape[0]
    tq = _tile(T, ATTN_TILE)
    nq = T // tq
    rg = min(ATTN_ROWS, tq)
    qi_tab, kj_tab = _causal_pairs(nq, by_key=True)
    npairs = qi_tab.shape[0]

    def body(qi_ref, kj_ref, q_ref, k_ref, v_ref, cq_ref, ck_ref, do_ref, lse_ref, dl_ref,
             dq_ref, dk_ref, dv_ref, rsum_ref, csum_ref, dq_sc, dk_sc, dv_sc):
        t = pl.program_id(1)
        i, j = qi_ref[t], kj_ref[t]

        @pl.when(t == 0)
        def _():
            dq_sc[...] = jnp.zeros_like(dq_sc)

        @pl.when(i == j)
        def _():
            dk_sc[...] = jnp.zeros_like(dk_sc)
            dv_sc[...] = jnp.zeros_like(dv_sc)

        base = pl.multiple_of(i * tq, tq)

        def step(diag):
            ones = _ones_column(tq)
            for r in range(tq // rg):
                rows = slice(r * rg, (r + 1) * rg)
                w = (r + 1) * rg if diag else tq
                qr, dor = q_ref[rows, :], do_ref[rows, :]
                s = _fox_logits(qr, k_ref[:w, :], ck_ref[0, :, :w], r * rg, diag)
                p = jnp.exp(s - (lse_ref[0, rows, :] - cq_ref[0, rows, :]))
                dp = _dot(dor, v_ref[:w, :], "nt")
                dsb = (p * (dp - dl_ref[0, rows, :])).astype(BF16)
                dv_sc[:w, :] += _dot(p.astype(BF16), dor, "tn")
                dk_sc[:w, :] += _dot(dsb, jnp.concatenate([qr, ones[rows, :]], axis=1), "tn")
                dq_sc[pl.ds(base + r * rg, rg), :] += _dot(dsb, jnp.concatenate([k_ref[:w, :], ones[:w, :]], axis=1))

        @pl.when(i > j)
        def _():
            step(False)

        @pl.when(i == j)
        def _():
            step(True)

        @pl.when(i == nq - 1)
        def _():
            dk_ref[...] = dk_sc[:, :DH].astype(dk_ref.dtype)
            dv_ref[...] = dv_sc[...].astype(dv_ref.dtype)
            csum_ref[0] = dk_sc[:, DH:DH + 1]

        @pl.when(t == npairs - 1)
        def _():
            dq_ref[...] = dq_sc[:, :DH].astype(dq_ref.dtype)
            rsum_ref[0] = dq_sc[:, DH:DH + 1]

    col = pl.BlockSpec((1, tq, 1), lambda h, t, qi, kj: (h, qi[t], 0))
    kv = lambda off: pl.BlockSpec((tq, DH), lambda h, t, qi, kj: (kj[t], off + h))
    kv_out = pl.BlockSpec((tq, DH), lambda h, t, qi, kj: (kj[t], h))
    grid_spec = pltpu.PrefetchScalarGridSpec(
        num_scalar_prefetch=2, grid=(HEADS, npairs),
        in_specs=[pl.BlockSpec((tq, DH), lambda h, t, qi, kj: (qi[t], _Q_OFF + h)), kv(_K_OFF), kv(_V_OFF), col,
                  pl.BlockSpec((1, 1, tq), lambda h, t, qi, kj: (h, 0, kj[t])),
                  pl.BlockSpec((tq, DH), lambda h, t, qi, kj: (qi[t], h)), col, col],
        out_specs=[pl.BlockSpec((T, DH), lambda h, t, qi, kj: (0, h)), kv_out, kv_out,
                   pl.BlockSpec((1, T, 1), lambda h, t, qi, kj: (h, 0, 0)),
                   pl.BlockSpec((1, tq, 1), lambda h, t, qi, kj: (h, kj[t], 0))],
        scratch_shapes=[pltpu.VMEM((T, 2 * DH), F32), pltpu.VMEM((tq, 2 * DH), F32), pltpu.VMEM((tq, DH), F32)])
    D = HEADS * DH
    return pl.pallas_call(
        body, name=name, grid_spec=grid_spec,
        out_shape=[jax.ShapeDtypeStruct((T, D), BF16)] * 3 + [jax.ShapeDtypeStruct((HEADS, T, 1), F32)] * 2,
        compiler_params=_params("parallel", "arbitrary"),
    )(qi_tab, kj_tab, pm, pm, pm, c_col, c_row, do, lse, delta)


def _xattn_fwd(q, kv, name):
    T, D = q.shape
    M = kv.shape[0]
    dh = D // MEM_HEADS
    tq = _tile(T, XATTN_TILE)
    scale = 1.0 / math.sqrt(dh)

    def body(q_ref, kv_ref, o_ref):
        for h in range(MEM_HEADS):
            cs = slice(h * dh, (h + 1) * dh)
            s = _dot(q_ref[:, cs], kv_ref[:, cs], "nt") * scale
            p = jnp.exp(s - jnp.max(s, axis=1, keepdims=True))
            p = p / _rowsum(p)
            o_ref[:, cs] = _dot(p.astype(BF16), kv_ref[:, D + h * dh:D + (h + 1) * dh]).astype(o_ref.dtype)

    row = pl.BlockSpec((tq, D), lambda i: (i, 0))
    return pl.pallas_call(
        body, name=name, grid=(T // tq,), in_specs=[row, pl.BlockSpec((M, 2 * D), lambda i: (0, 0))],
        out_specs=row, out_shape=jax.ShapeDtypeStruct((T, D), BF16), compiler_params=_params("parallel"),
    )(q, kv)


def _xattn_bwd(q, kv, do, name):
    T, D = q.shape
    M = kv.shape[0]
    dh = D // MEM_HEADS
    tq = _tile(T, XATTN_TILE)
    scale = 1.0 / math.sqrt(dh)

    def body(q_ref, kv_ref, do_ref, dq_ref, dkv_ref):
        @pl.when(pl.program_id(0) == 0)
        def _():
            dkv_ref[...] = jnp.zeros_like(dkv_ref)

        for h in range(MEM_HEADS):
            cs = slice(h * dh, (h + 1) * dh)
            vs = slice(D + h * dh, D + (h + 1) * dh)
            s = _dot(q_ref[:, cs], kv_ref[:, cs], "nt") * scale
            p = jnp.exp(s - jnp.max(s, axis=1, keepdims=True))
            p = p / _rowsum(p)
            dp = _dot(do_ref[:, cs], kv_ref[:, vs], "nt")
            ds = (p * (dp - _rowsum(p * dp)) * scale).astype(BF16)
            dq_ref[:, cs] = _dot(ds, kv_ref[:, cs]).astype(dq_ref.dtype)
            dkv_ref[:, cs] += _dot(ds, q_ref[:, cs], "tn")
            dkv_ref[:, vs] += _dot(p.astype(BF16), do_ref[:, cs], "tn")

    row = pl.BlockSpec((tq, D), lambda i: (i, 0))
    full = pl.BlockSpec((M, 2 * D), lambda i: (0, 0))
    return pl.pallas_call(
        body, name=name, grid=(T // tq,), in_specs=[row, full, row], out_specs=[row, full],
        out_shape=[jax.ShapeDtypeStruct((T, D), BF16), jax.ShapeDtypeStruct((M, 2 * D), F32)],
        compiler_params=_params("arbitrary"),
    )(q, kv, do)


_HBM = pl.BlockSpec(memory_space=pltpu.HBM)


def _position():
    return lax.axis_index("x"), lax.axis_index("y"), lax.axis_index("c")


def _other_chips(x, y):
    return [(1 - x, y), (x, 1 - y), (1 - x, 1 - y)]


class _Exchange:
    def __init__(self, inputs, out_shapes, scratch, copies, inplace=False):
        self.inputs, self.out_shapes, self.scratch, self.copies, self.inplace = inputs, out_shapes, scratch, copies, inplace

    def start(self, in_refs, out_refs, sems):
        for cp in self.copies(in_refs, out_refs, sems, False)[0]:
            cp.start()

    def wait(self, in_refs, out_refs, sems):
        for cp, how in self.copies(in_refs, out_refs, sems, True)[1]:
            getattr(cp, how)()

    def aliases(self, first_input, first_output):
        return {first_input + w: first_output + w for w in range(len(self.inputs))} if self.inplace else {}


def _run_exchange(ex, name):
    n_in, n_out = len(ex.inputs), len(ex.out_shapes)

    def body(*refs):
        parts = refs[:n_in], refs[n_in:n_in + n_out], refs[n_in + n_out:]
        ex.start(*parts)
        ex.wait(*parts)

    return pl.pallas_call(
        body, name=name, in_specs=[_HBM] * n_in, out_specs=[_HBM] * n_out, out_shape=ex.out_shapes,
        input_output_aliases=ex.aliases(0, 0), scratch_shapes=ex.scratch,
    )(*ex.inputs)


def _chip_exchange(arrays, out_shapes, src_of, dst_of):
    n = len(arrays)

    def copies(srcs, outs, sems, waiting):
        send, recv, local = sems
        x, y, c = _position()
        q = 2 * x + y
        kept, sent, arriving = [], [], []
        for w, (s_ref, o_ref) in enumerate(zip(srcs, outs)):
            kept.append(pltpu.make_async_copy(src_of(s_ref, q, c), dst_of(o_ref, q, c), local.at[w]))
            for j, (px, py) in enumerate(_other_chips(x, y)):
                sems_j = dict(send_sem=send.at[3 * w + j], recv_sem=recv.at[3 * w + j], device_id=(px, py, c),
                              device_id_type=MESH)
                sent.append(pltpu.make_async_remote_copy(src_ref=src_of(s_ref, 2 * px + py, c),
                                                         dst_ref=dst_of(o_ref, q, c), **sems_j))
                if waiting:
                    arriving.append(pltpu.make_async_remote_copy(src_ref=src_of(s_ref, q, c),
                                                                 dst_ref=dst_of(o_ref, 2 * px + py, c), **sems_j))
        return kept + sent, ([(cp, "wait_recv") for cp in arriving] + [(cp, "wait_send") for cp in sent]
                             + [(cp, "wait") for cp in kept])

    scratch = [pltpu.SemaphoreType.DMA((3 * n,)), pltpu.SemaphoreType.DMA((3 * n,)), pltpu.SemaphoreType.DMA((n,))]
    return _Exchange(arrays, out_shapes, scratch, copies)


def _ex_ag_chips(blks):
    return _chip_exchange(blks, [jax.ShapeDtypeStruct((4, 2) + b.shape, b.dtype) for b in blks],
                          src_of=lambda r, chip, c: r, dst_of=lambda r, chip, c: r.at[chip, c])


def _ex_rs_chips(parts):
    return _chip_exchange(parts, [jax.ShapeDtypeStruct(h.shape, h.dtype) for h in parts],
                          src_of=lambda r, chip, c: r.at[chip], dst_of=lambda r, chip, c: r.at[chip])


def _ex_ag_sibling(arrs):
    n = len(arrs)

    def copies(ins, outs, sems, waiting):
        send, recv = sems
        x, y, c = _position()
        to = dict(device_id=(x, y, 1 - c), device_id_type=MESH)
        mine = [pltpu.make_async_remote_copy(src_ref=a.at[:, c], dst_ref=a.at[:, c], send_sem=send.at[w],
                                             recv_sem=recv.at[w], **to) for w, a in enumerate(outs)]
        theirs = [pltpu.make_async_remote_copy(src_ref=a.at[:, c], dst_ref=a.at[:, 1 - c], send_sem=send.at[w],
                                               recv_sem=recv.at[w], **to) for w, a in enumerate(outs if waiting else [])]
        return mine, [(cp, "wait_recv") for cp in theirs] + [(cp, "wait_send") for cp in mine]

    return _Exchange(arrs, [jax.ShapeDtypeStruct(a.shape, a.dtype) for a in arrs],
                     [pltpu.SemaphoreType.DMA((n,)), pltpu.SemaphoreType.DMA((n,))], copies, inplace=True)


def _ex_rs_sibling(blocks):
    n = len(blocks)

    def copies(srcs, outs, sems, waiting):
        send, recv = sems
        x, y, c = _position()
        cps = [pltpu.make_async_remote_copy(src_ref=b.at[:, 1 - c], dst_ref=l, send_sem=send.at[w], recv_sem=recv.at[w],
                                            device_id=(x, y, 1 - c), device_id_type=MESH)
               for w, (b, l) in enumerate(zip(srcs, outs))]
        return cps, [(cp, "wait") for cp in cps]

    return _Exchange(blocks, [jax.ShapeDtypeStruct((4,) + b.shape[2:], b.dtype) for b in blocks],
                     [pltpu.SemaphoreType.DMA((n,)), pltpu.SemaphoreType.DMA((n,))], copies)


def _row_tile(rows, pref=256):
    for t in range(min(pref, rows) // 16 * 16, 0, -16):
        if rows % t == 0:
            return t
    raise ValueError(f"no row tile for {rows}")


def _pair_add(blocks, landed, core, out_dtype, name):
    n, _, s0, s1 = blocks.shape
    tr = _row_tile(s0)

    def body(core_ref, a_ref, b_ref, o_ref):
        del core_ref
        o_ref[...] = (a_ref[...] + b_ref[...]).astype(o_ref.dtype)

    grid_spec = pltpu.PrefetchScalarGridSpec(
        num_scalar_prefetch=1, grid=(n, s0 // tr),
        in_specs=[pl.BlockSpec((1, None, tr, s1), lambda p, i, core: (p, core[0], i, 0)),
                  pl.BlockSpec((1, tr, s1), lambda p, i, core: (p, i, 0))],
        out_specs=pl.BlockSpec((1, tr, s1), lambda p, i, core: (p, i, 0)))
    return pl.pallas_call(
        body, name=name, grid_spec=grid_spec, out_shape=jax.ShapeDtypeStruct(landed.shape, out_dtype),
        compiler_params=_params("parallel", "parallel"),
    )(core, blocks, landed)


def _adamw_math(w, g, m, v):
    m = ADAM_B1 * m + (1.0 - ADAM_B1) * g
    v = ADAM_B2 * v + (1.0 - ADAM_B2) * (g * g)
    m_hat = m / (1.0 - ADAM_B1 ** ADAM_STEP)
    v_hat = v / (1.0 - ADAM_B2 ** ADAM_STEP)
    delta = -ADAM_LR * (m_hat / (jnp.sqrt(v_hat) + ADAM_EPS) + ADAM_WD * w)
    return delta, m, v


def _adamw_reduce(slots, w, m, v, name):
    n, R, C = slots.shape
    tr = _row_tile(R)

    def body(s_ref, w_ref, m_ref, v_ref, g_ref, d_ref, nm_ref, nv_ref):
        g = s_ref[0].astype(F32)
        for p in range(1, n):
            g = g + s_ref[p].astype(F32)
        g_ref[...] = g
        d_ref[...], nm_ref[...], nv_ref[...] = _adamw_math(w_ref[...], g, m_ref[...], v_ref[...])

    row = pl.BlockSpec((tr, C), lambda i: (i, 0))
    return pl.pallas_call(
        body, name=name, grid=(R // tr,), in_specs=[pl.BlockSpec((n, tr, C), lambda i: (0, i, 0)), row, row, row],
        out_specs=[row] * 4, out_shape=[jax.ShapeDtypeStruct((R, C), F32)] * 4, compiler_params=_params("parallel"),
    )(slots, w, m, v)


def _full_from_gathered(a, n):
    s0, s1 = a.shape[2:]
    blk = a.reshape(8, s0, s1)
    if n in COL_SHARDED:
        return blk.transpose(1, 0, 2).reshape(s0, 8 * s1)
    return blk.reshape(8 * s0, s1)


def _blocks_from_full(g, n, shard_shape):
    s0, s1 = shard_shape
    if n in COL_SHARDED:
        blk = g.reshape(s0, 8, s1).transpose(1, 0, 2)
    else:
        blk = g.reshape(8, s0, s1)
    return blk.reshape(4, 2, s0, s1)


def _swiglu_interleave(w):
    d, f2 = w.shape
    return w.reshape(d, 2, f2 // (2 * SWIGLU_TILE), SWIGLU_TILE).transpose(0, 2, 1, 3).reshape(d, f2)


def _swiglu_deinterleave(w):
    d, f2 = w.shape
    return w.reshape(d, f2 // (2 * SWIGLU_TILE), 2, SWIGLU_TILE).transpose(0, 2, 1, 3).reshape(d, f2)


SMALL_ROWS = 16


def _pack_small(vals, loss_row):
    rows = []
    for n in SMALL:
        flat = vals[n].reshape(-1)
        pad = (-flat.shape[0]) % PACK_COLS
        rows.append(jnp.pad(flat, (0, pad)).reshape(-1, PACK_COLS))
    rows.append(loss_row)
    out = jnp.concatenate(rows, axis=0)
    assert out.shape[0] == SMALL_ROWS, out.shape
    return out


def _unpack_small(packed, like):
    out, r = {}, 0
    for n in SMALL:
        size = like[n].size
        rows = -(-size // PACK_COLS)
        out[n] = packed[r:r + rows].reshape(-1)[:size].reshape(like[n].shape)
        r += rows
    return out


class _NoTraffic:
    def host(self, stage):
        return None

    def landed(self, stage, arrays):
        pass

    def grads_ready(self, names, gW, behind=None):
        pass


def _mm_behind(traffic, stage, *args, **kwargs):
    ex = traffic.host(stage)
    if ex is None:
        return _mm(*args, **kwargs)
    out, arrays = _mm(*args, hosted=ex, **kwargs)
    traffic.landed(stage, arrays)
    return out


def _ffn_fwd(x, g_pre, W, tag, traffic, up_stage=None, down_stage=None):
    h = _rms_fwd(x, g_pre, f"{tag}_pre")
    ex = traffic.host(up_stage) if up_stage else None
    u, a, arrays = _mm_swiglu(h, W[f"{tag}_w_in"], f"{tag}_up", hosted=ex)
    if ex is not None:
        traffic.landed(up_stage, arrays)
    z = _mm_behind(traffic, down_stage, a, W[f"{tag}_w_down"], "nn", F32, f"{tag}_down", tk=1408)
    return h, u, a, z


def _ffn_bwd(saved, x, g_pre, w_in, w_down, g_post, dx_out, tag, traffic, down_dw_stage=None, up_dx_stage=None):
    h, u, a, z = saved
    dz, dg_post = _rms_bwd(z, g_post, dx_out, 0.5, f"{tag}_post_bwd", BF16)
    dw_down = _mm_behind(traffic, down_dw_stage, a, dz, "tn", F32, f"{tag}_down_dw", tm=1408)
    du = _mm_swiglu_bwd(dz, w_down, u, f"{tag}_down_dx")
    dh = _mm_behind(traffic, up_dx_stage, du, w_in, "nt", BF16, f"{tag}_up_dx", tk=5632)
    dw_in = _mm(h, du, "tn", F32, f"{tag}_up_dw", tk=4096)
    dx, dg_pre = _rms_bwd(x, g_pre, dh, 1.0, f"{tag}_pre_bwd", F32, resid=dx_out)
    return dx, dg_pre, dg_post, dw_in, dw_down


def _step_local(x, mem, target, W, S, traffic=_NoTraffic()):
    T, D = x.shape
    gW, gS = {}, {}

    f1 = _ffn_fwd(x, S["ffn1_pre_g"], W, "ffn1", traffic, "gather_mixer_chips", "gather_mixer_sibling")
    x1 = _resid_rms(x, f1[3], S["ffn1_post_g"], 0.5, "ffn1_post")

    h2 = _rms_fwd(x1, S["mix_pre_g"], "mix_pre")
    pm = _mm_behind(traffic, "gather_late_chips", h2, W["w_main"], "nn", BF16, "mix_proj_main")
    pf = _mm(h2, W["w_f"], "nn", F32, "mix_proj_f")
    pg = _mm_behind(traffic, "gather_late_sibling", h2, W["w_gates"], "nn", BF16, "mix_proj_gates")
    lbl = S["hg_lb_logits"].reshape(2, HEADS, 1, DH)
    o_a, states = _hgrn_fwd(pm, lbl, "hgrn_fwd")
    oan = _hgout_fwd(o_a, pm, S["hg_norm_g"], "hgrn_out")
    bias = jnp.pad(S["fox_f_bias"], ((0, 0), (0, LANES - HEADS)))
    c = _fox_cumsum(pf, bias, "fox_cumsum")
    c_heads = c[:, :HEADS].T
    c_col, c_row = c_heads[:, :, None], c_heads[:, None, :]
    o_b, lse = _fox_fwd(pm, c_col, c_row, "fox_fwd")
    ya = _mm(oan, W["w_branch_a"], "nn", BF16, "branch_a")
    yb = _mm(o_b, W["w_branch_b"], "nn", BF16, "branch_b")
    y = _merge_fwd(ya, yb, pg, S["b_gate"], "merge")
    z2 = _mm(y, W["w_out"], "nn", F32, "mix_out")
    x2 = _resid_rms(x1, z2, S["mix_post_g"], 1.0, "mix_post")

    h3 = _rms_fwd(x2, S["mem_pre_g"], "mem_pre")
    memn = _rms_fwd(mem, S["mem_kv_g"], "mem_kv_norm")
    qm = _mm(h3, W["w_mq"], "nn", BF16, "mem_q")
    kv = _mm(memn, W["w_mkv"], "nn", BF16, "mem_kv")
    om = _xattn_fwd(qm, kv, "mem_attn")
    z3 = _mm(om, W["w_mo"], "nn", F32, "mem_o")
    x3 = _resid_rms(x2, z3, S["mem_post_g"], 1.0, "mem_post")

    f2 = _ffn_fwd(x3, S["ffn2_pre_g"], W, "ffn2", traffic)
    dx4, sq = _final_loss(x3, f2[3], S["ffn2_post_g"], 0.5, target, "loss")

    dx3, gS["ffn2_pre_g"], gS["ffn2_post_g"], gW["ffn2_w_in"], gW["ffn2_w_down"] = _ffn_bwd(
        f2, x3, S["ffn2_pre_g"], W["ffn2_w_in"], W["ffn2_w_down"], S["ffn2_post_g"], dx4, "ffn2", traffic)
    traffic.grads_ready(["ffn2_w_in", "ffn2_w_down"], gW, behind="scatter_ffn2_sibling")

    dz3, gS["mem_post_g"] = _rms_bwd(z3, S["mem_post_g"], dx3, 1.0, "mem_post_bwd", BF16)
    dom = _mm_behind(traffic, "scatter_ffn2_sibling", dz3, W["w_mo"], "nt", BF16, "mem_o_dx")
    gW["w_mo"] = _mm(om, dz3, "tn", F32, "mem_o_dw")
    dqm, dkv = _xattn_bwd(qm, kv, dom, "mem_attn_bwd")
    dh3 = _mm(dqm, W["w_mq"], "nt", BF16, "mem_q_dx")
    gW["w_mq"] = _mm(h3, dqm, "tn", F32, "mem_q_dw")
    dkvb = dkv.astype(BF16)
    gW["w_mkv"] = _mm(memn, dkvb, "tn", F32, "mem_kv_dw")
    dmemn = _mm(dkvb, W["w_mkv"], "nt", F32, "mem_kv_dx")
    _, gS["mem_kv_g"] = _rms_bwd(mem, S["mem_kv_g"], dmemn, 1.0, "mem_kv_norm_bwd", BF16)
    dx2, gS["mem_pre_g"] = _rms_bwd(x2, S["mem_pre_g"], dh3, 1.0, "mem_pre_bwd", F32, resid=dx3)

    dz2, gS["mix_post_g"] = _rms_bwd(z2, S["mix_post_g"], dx2, 1.0, "mix_post_bwd", BF16)
    dy = _mm(dz2, W["w_out"], "nt", BF16, "mix_out_dx")
    gW["w_out"] = _mm(y, dz2, "tn", F32, "mix_out_dw")
    dya, dyb, dpg, gS["b_gate"] = _merge_bwd(dy, ya, yb, pg, S["b_gate"], "merge_bwd")
    doan = _mm(dya, W["w_branch_a"], "nt", BF16, "branch_a_dx")
    gW["w_branch_a"] = _mm(oan, dya, "tn", F32, "branch_a_dw")
    dob = _mm(dyb, W["w_branch_b"], "nt", BF16, "branch_b_dx")
    gW["w_branch_b"] = _mm(o_b, dyb, "tn", F32, "branch_b_dw")
    traffic.grads_ready(["w_mo", "w_mq", "w_mkv", "w_out", "w_branch_a", "w_branch_b"], gW,
                        behind="scatter_mid_sibling")

    delta = _fox_delta(dob, o_b, "fox_delta")
    dq_b, dk_b, dv_b, ds_rows, ds_cols = _fox_bwd(pm, c_col, c_row, dob, lse, delta, "fox_bwd")
    dc = jnp.pad((ds_rows.reshape(HEADS, T) - ds_cols.reshape(HEADS, T)).T, ((0, 0), (0, LANES - HEADS)))
    dpf, dbias = _fox_dcum(dc, pf, bias, "fox_cumsum_bwd")
    gS["fox_f_bias"] = dbias[:, :HEADS]

    do_a, dg_a, gS["hg_norm_g"] = _hgout_bwd(o_a, pm, S["hg_norm_g"], doan, "hgrn_out_bwd")
    dq_a, df_a, di_a, dlbl = _hgrn_bwd(pm, lbl, states, do_a, "hgrn_bwd")
    gS["hg_lb_logits"] = dlbl.reshape(2, HEADS, DH)

    dpm = jnp.concatenate([dq_a, df_a, di_a, dg_a, dq_b, dk_b, dv_b], axis=1)
    dpf16 = dpf.astype(BF16)
    dh2 = _mm_behind(traffic, "scatter_ffn2_chips", dpm, W["w_main"], "nt", F32, "mix_proj_main_dx")
    dh2 = _mm_behind(traffic, "scatter_mid_sibling", dpg, W["w_gates"], "nt", F32, "mix_proj_gates_dx", add=dh2)
    dh2 = _mm(dpf16, W["w_f"], "nt", F32, "mix_proj_f_dx", add=dh2)
    gW["w_main"] = _mm_behind(traffic, "scatter_mid_chips", h2, dpm, "tn", F32, "mix_proj_main_dw")
    gW["w_gates"] = _mm(h2, dpg, "tn", F32, "mix_proj_gates_dw")
    gW["w_f"] = _mm(h2, dpf16, "tn", F32, "mix_proj_f_dw")
    traffic.grads_ready(["w_in"], gW, behind="scatter_w_in_sibling")
    dx1, gS["mix_pre_g"] = _rms_bwd(x1, S["mix_pre_g"], dh2, 1.0, "mix_pre_bwd", F32, resid=dx2)

    dx0, gS["ffn1_pre_g"], gS["ffn1_post_g"], gW["ffn1_w_in"], gW["ffn1_w_down"] = _ffn_bwd(
        f1, x, S["ffn1_pre_g"], W["ffn1_w_in"], W["ffn1_w_down"], S["ffn1_post_g"], dx1, "ffn1", traffic,
        "scatter_w_in_sibling", "scatter_w_in_chips")
    traffic.grads_ready(["ffn1_w_in", "ffn1_w_down"], gW)
    return sq, dx0, gW, gS


GATHER_FIRST = ["ffn1_w_in", "ffn1_w_down"]
GATHER_MIXER = ["w_in", "w_branch_a", "w_branch_b", "w_out"]
GATHER_LATE = ["w_mq", "w_mkv", "w_mo", "ffn2_w_in", "ffn2_w_down"]
SCATTER_BEHIND = {
    "scatter_ffn2_chips": ["ffn2_w_in", "ffn2_w_down"],
    "scatter_mid_chips": ["w_mo", "w_mq", "w_mkv", "w_out", "w_branch_a", "w_branch_b"],
    "scatter_w_in_chips": ["w_in"],
}


class _Traffic:
    def __init__(self, sent, W, shapes, core, D):
        self.sent, self.W, self.shapes, self.core, self.D = sent, W, shapes, core, D
        self.half, self.pairs, self.slots, self.waiting = {}, {}, {}, {}

    def install(self, names, gathered):
        D = self.D
        for n, g in zip(names, gathered):
            full = _full_from_gathered(g, n)
            if n == "w_in":
                self.W["w_main"] = full[:, :7 * D]
                self.W["w_f"] = jnp.pad(full[:, 7 * D:7 * D + HEADS], ((0, 0), (0, LANES - HEADS)))
                self.W["w_gates"] = full[:, 7 * D + HEADS:]
            elif n in ("ffn1_w_in", "ffn2_w_in"):
                self.W[n] = _swiglu_interleave(full)
            else:
                self.W[n] = full

    def host(self, stage):
        if stage == "gather_mixer_chips":
            return _ex_ag_chips([self.sent[n] for n in GATHER_MIXER])
        if stage == "gather_late_chips":
            return _ex_ag_chips([self.sent[n] for n in GATHER_LATE])
        if stage in ("gather_mixer_sibling", "gather_late_sibling"):
            return _ex_ag_sibling(self.half[stage])
        if stage in SCATTER_BEHIND:
            return _ex_rs_chips([self.pairs[n] for n in SCATTER_BEHIND[stage]])
        if stage in self.waiting:
            return _ex_rs_sibling(self.waiting[stage][1])
        return None

    def landed(self, stage, arrays):
        if stage == "gather_mixer_chips":
            self.half["gather_mixer_sibling"] = arrays
        elif stage == "gather_late_chips":
            self.half["gather_late_sibling"] = arrays
        elif stage == "gather_mixer_sibling":
            self.install(GATHER_MIXER, arrays)
        elif stage == "gather_late_sibling":
            self.install(GATHER_LATE, arrays)
        elif stage in self.waiting:
            self._pair_sums(*self.waiting.pop(stage), arrays)
        else:
            self.slots.update(zip(SCATTER_BEHIND[stage], arrays))

    def _final_grad(self, n, gW):
        D = self.D
        if n == "w_in":
            g = gW["w_main"]
            return jnp.concatenate([g[:, :4 * D], g[:, 4 * D:5 * D] * (1.0 / math.sqrt(DH)), g[:, 5 * D:],
                                    gW["w_f"][:, :HEADS], gW["w_gates"]], axis=1)
        if n in ("ffn1_w_in", "ffn2_w_in"):
            return _swiglu_deinterleave(gW[n])
        return gW[n]

    def _pair_sums(self, names, blocks, got):
        for n, b, l in zip(names, blocks, got):
            self.pairs[n] = _pair_add(b, l, self.core, BF16, f"rs_pair_add_{n}")

    def grads_ready(self, names, gW, behind=None):
        blocks = [_blocks_from_full(self._final_grad(n, gW), n, self.shapes[n]) for n in names]
        if behind is None:
            self._pair_sums(names, blocks, _run_exchange(_ex_rs_sibling(blocks), f"rs_sibling_{names[0]}"))
        else:
            self.waiting[behind] = (names, blocks)

    def finish(self):
        rest = [n for n in BIG if n not in self.slots]
        got = _run_exchange(_ex_rs_chips([self.pairs[n] for n in rest]), "rs_chips_last")
        self.slots.update(zip(rest, got))
        return self.slots


def _train_step(a):
    c_idx = lax.axis_index("c")
    x, mem, target = a["x"][0], a["mem"][0], a["loss_target"][0]
    D = x.shape[1]
    shards = {n: a[n][0] for n in BIG}

    fox_scale = 1.0 / math.sqrt(DH)
    n_mine = shards["w_in"].shape[1]
    dev = 4 * lax.axis_index("x") + 2 * lax.axis_index("y") + c_idx
    cols = dev * n_mine + jnp.arange(n_mine)
    is_fox_q = (cols >= 4 * D) & (cols < 5 * D)
    sent = dict(shards, w_in=shards["w_in"] * jnp.where(is_fox_q, fox_scale, 1.0)[None, :])
    sent = {n: v.astype(BF16) for n, v in sent.items()}
    W = {}
    traffic = _Traffic(sent, W, {n: shards[n].shape for n in BIG}, c_idx.astype(jnp.int32).reshape(1), D)
    first = _run_exchange(_ex_ag_chips([sent[n] for n in GATHER_FIRST]), "ag_first_chips")
    traffic.install(GATHER_FIRST, _run_exchange(_ex_ag_sibling(first), "ag_first_sibling"))
    S = {n: a[n] for n in SMALL}

    sq, grad_x, gW, gS = _step_local(x, mem, target, W, S, traffic)
    slots = traffic.finish()
    big = {n: _adamw_reduce(slots[n], shards[n], a["m_" + n][0], a["v_" + n][0], f"adamw_{n}") for n in BIG}

    loss_row = jnp.pad(sq[:1, :1] * (0.5 / D), ((0, 0), (0, PACK_COLS - 1)))
    small_half = _run_exchange(_ex_ag_chips([_pack_small(gS, loss_row)]), "small_ag_chips")
    small_all = _run_exchange(_ex_ag_sibling(small_half), "small_ag_sibling")[0]
    small_slots = small_all.reshape(8, SMALL_ROWS, PACK_COLS)
    zero_row = jnp.zeros((1, PACK_COLS), F32)
    g_sm, d_sm, m_sm, v_sm = _adamw_reduce(
        small_slots, _pack_small({n: a[n] for n in SMALL}, zero_row),
        _pack_small({n: a["m_" + n] for n in SMALL}, zero_row),
        _pack_small({n: a["v_" + n] for n in SMALL}, zero_row), "adamw_small")

    def unpack(which, small):
        out = _unpack_small(small, {n: a[n] for n in SMALL})
        for n in BIG:
            out[n] = big[n][which][None]
        return [out[n] for n in WEIGHTS]

    loss = g_sm[SMALL_ROWS - 1, 0]
    return (loss, grad_x[None], *unpack(0, g_sm), *unpack(1, d_sm), *unpack(2, m_sm), *unpack(3, v_sm))


def kernel(x, mem, ffn1_pre_g, ffn1_w_in, ffn1_w_down, ffn1_post_g, mix_pre_g, w_in, hg_lb_logits, hg_norm_g, fox_f_bias, w_branch_a, w_branch_b, b_gate, w_out, mix_post_g, mem_pre_g, mem_kv_g, w_mq, w_mkv, w_mo, mem_post_g, ffn2_pre_g, ffn2_w_in, ffn2_w_down, ffn2_post_g, loss_target, m_ffn1_pre_g, m_ffn1_w_in, m_ffn1_w_down, m_ffn1_post_g, m_mix_pre_g, m_w_in, m_hg_lb_logits, m_hg_norm_g, m_fox_f_bias, m_w_branch_a, m_w_branch_b, m_b_gate, m_w_out, m_mix_post_g, m_mem_pre_g, m_mem_kv_g, m_w_mq, m_w_mkv, m_w_mo, m_mem_post_g, m_ffn2_pre_g, m_ffn2_w_in, m_ffn2_w_down, m_ffn2_post_g, v_ffn1_pre_g, v_ffn1_w_in, v_ffn1_w_down, v_ffn1_post_g, v_mix_pre_g, v_w_in, v_hg_lb_logits, v_hg_norm_g, v_fox_f_bias, v_w_branch_a, v_w_branch_b, v_b_gate, v_w_out, v_mix_post_g, v_mem_pre_g, v_mem_kv_g, v_w_mq, v_w_mkv, v_w_mo, v_mem_post_g, v_ffn2_pre_g, v_ffn2_w_in, v_ffn2_w_down, v_ffn2_post_g):
    return _train_step(dict(locals()))
```

```python
import functools
import math

import jax
import jax.numpy as jnp
from jax import lax
from jax.experimental import pallas as pl
from jax.experimental.pallas import tpu as pltpu

F32 = jnp.float32
BF16 = jnp.bfloat16
MESH = pl.DeviceIdType.MESH

EPS = 1e-6
HEADS = 8
DH = 128
MEM_HEADS = 4
CHUNK = 128
HALF = CHUNK // 2
SWIGLU_TILE = 256
LANES = 128
PACK_COLS = 1024
ROW_TILE = 1024
SEQ_BLOCK = 2048
CUMSUM_BLOCK = 512
XATTN_TILE = 2048
ATTN_TILE = 2048
ATTN_ROWS = 256
EXP_CLAMP = 80.0
NEG_BIG = -1e30

ADAM_LR, ADAM_B1, ADAM_B2, ADAM_EPS, ADAM_WD, ADAM_STEP = 0.001, 0.9, 0.999, 1e-08, 0.01, 10

VMEM_LIMIT = 48 * 1024 * 1024

_DN = {
    "nn": (((1,), (0,)), ((), ())),
    "nt": (((1,), (1,)), ((), ())),
    "tn": (((0,), (0,)), ((), ())),
}

BIG = ["ffn1_w_in", "ffn1_w_down", "w_in", "w_branch_a", "w_branch_b", "w_out", "w_mq", "w_mkv", "w_mo",
       "ffn2_w_in", "ffn2_w_down"]
COL_SHARDED = {"ffn1_w_in", "w_in", "w_mkv", "ffn2_w_in"}
SMALL = ["ffn1_pre_g", "ffn1_post_g", "mix_pre_g", "hg_lb_logits", "hg_norm_g", "fox_f_bias", "b_gate",
         "mix_post_g", "mem_pre_g", "mem_kv_g", "mem_post_g", "ffn2_pre_g", "ffn2_post_g"]
WEIGHTS = ["ffn1_pre_g", "ffn1_w_in", "ffn1_w_down", "ffn1_post_g", "mix_pre_g", "w_in", "hg_lb_logits",
           "hg_norm_g", "fox_f_bias", "w_branch_a", "w_branch_b", "b_gate", "w_out", "mix_post_g", "mem_pre_g",
           "mem_kv_g", "w_mq", "w_mkv", "w_mo", "mem_post_g", "ffn2_pre_g", "ffn2_w_in", "ffn2_w_down",
           "ffn2_post_g"]


def _dot(a, b, mode="nn"):
    return lax.dot_general(a, b, _DN[mode], preferred_element_type=F32)


def _sig(x):
    return 1.0 / (1.0 + jnp.exp(-x))


def _sig_approx(x):
    return pl.reciprocal(1.0 + jnp.exp(-x), approx=True)


def _params(*dims):
    return pltpu.CompilerParams(dimension_semantics=dims if dims else None, vmem_limit_bytes=VMEM_LIMIT)


def _tile(dim, pref):
    if dim <= pref:
        return dim
    t = (pref // LANES) * LANES
    while t >= LANES:
        if dim % t == 0:
            return t
        t -= LANES
    raise ValueError(f"no tile for {dim}")


def _colsum(x):
    return jnp.sum(x, axis=0, keepdims=True)


def _rowsum(x):
    return jnp.sum(x, axis=1, keepdims=True)


def _iota(shape, axis):
    return lax.broadcasted_iota(jnp.int32, shape, axis)


def _pick_row(x, r):
    return _colsum(jnp.where(_iota(x.shape, 0) == r, x, 0.0))


def _tri_dot(tri, x):
    hi = x.astype(BF16)
    r1 = x - hi.astype(F32)
    mid = r1.astype(BF16)
    lo = (r1 - mid.astype(F32)).astype(BF16)
    return _dot(tri, hi) + _dot(tri, mid) + _dot(tri, lo)


_MM_TILES = {"nn": (2048, 512, 1024), "nt": (512, 1024, 4096), "tn": (1024, 1024, 2048)}


def _host_call(body, name, grid, in_specs, out_specs, out_shape, scratch_shapes, dims, args, hosted=None):
    if hosted is None:
        results = pl.pallas_call(body, name=name, grid=grid, in_specs=in_specs, out_specs=out_specs, out_shape=out_shape,
                                 scratch_shapes=scratch_shapes, compiler_params=_params(*dims))(*args)
        return list(results), []
    n_in, n_out, n_sc = len(in_specs), len(out_specs), len(scratch_shapes)
    h_in, h_out = len(hosted.inputs), len(hosted.out_shapes)

    def wrapped(*refs):
        cut = [n_in, h_in, n_out, h_out, n_sc]
        at = [sum(cut[:i]) for i in range(len(cut) + 1)]
        ins, hin, outs, hout, scr = (refs[at[i]:at[i + 1]] for i in range(len(cut)))
        hsems = refs[at[-1]:]
        ids = [pl.program_id(d) for d in range(len(grid))]
        first = functools.reduce(jnp.logical_and, [i == 0 for i in ids])
        last = functools.reduce(jnp.logical_and, [i == g - 1 for i, g in zip(ids, grid)])

        @pl.when(first)
        def _():
            hosted.start(hin, hout, hsems)

        body(*ins, *outs, *scr)

        @pl.when(last)
        def _():
            hosted.wait(hin, hout, hsems)

    results = pl.pallas_call(
        wrapped, name=name, grid=grid, in_specs=list(in_specs) + [_HBM] * h_in,
        out_specs=list(out_specs) + [_HBM] * h_out, out_shape=list(out_shape) + list(hosted.out_shapes),
        scratch_shapes=list(scratch_shapes) + list(hosted.scratch), input_output_aliases=hosted.aliases(n_in, n_out),
        compiler_params=_params(*dims))(*args, *hosted.inputs)
    return list(results[:n_out]), list(results[n_out:])


def _mm(a, b, mode, out_dtype, name, add=None, tm=None, tn=None, tk=None, hosted=None):
    tm, tn, tk = (given or pref for given, pref in zip((tm, tn, tk), _MM_TILES[mode]))
    if mode == "nn":
        (M, K), (K2, N) = a.shape, b.shape
    elif mode == "nt":
        (M, K), (N, K2) = a.shape, b.shape
    else:
        (K, M), (K2, N) = a.shape, b.shape
    assert K == K2, (name, a.shape, b.shape)
    tm, tn, tk = _tile(M, tm), _tile(N, tn), _tile(K, tk)
    nk = K // tk
    if mode == "tn":
        a_spec = pl.BlockSpec((tk, tm), lambda i, j, k: (k, i))
    else:
        a_spec = pl.BlockSpec((tm, tk), lambda i, j, k: (i, k))
    if mode == "nt":
        b_spec = pl.BlockSpec((tn, tk), lambda i, j, k: (j, k))
    else:
        b_spec = pl.BlockSpec((tk, tn), lambda i, j, k: (k, j))
    o_spec = pl.BlockSpec((tm, tn), lambda i, j, k: (i, j))
    has_add = add is not None

    def body(*refs):
        a_ref, b_ref = refs[0], refs[1]
        c_ref = refs[2] if has_add else None
        o_ref = refs[3] if has_add else refs[2]
        part = _dot(a_ref[...], b_ref[...], mode)
        if nk == 1:
            if has_add:
                part = part + c_ref[...]
            o_ref[...] = part.astype(o_ref.dtype)
            return
        acc_ref = refs[-1]
        k = pl.program_id(2)

        @pl.when(k == 0)
        def _():
            acc_ref[...] = part + c_ref[...] if has_add else part

        @pl.when(k > 0)
        def _():
            acc_ref[...] += part

        @pl.when(k == nk - 1)
        def _():
            o_ref[...] = acc_ref[...].astype(o_ref.dtype)

    in_specs = [a_spec, b_spec] + ([o_spec] if has_add else [])
    args = (a, b) + ((add,) if has_add else ())
    (out,), landed = _host_call(
        body, name, (M // tm, N // tn, nk), in_specs, [o_spec], [jax.ShapeDtypeStruct((M, N), out_dtype)],
        [pltpu.VMEM((tm, tn), F32)] if nk > 1 else [], ("parallel", "parallel", "arbitrary"), args, hosted)
    return out if hosted is None else (out, landed)


def _rms_fwd(x, g, name, out_dtype=BF16):
    T, D = x.shape
    tr = _tile(T, ROW_TILE)

    def body(x_ref, g_ref, o_ref):
        xv = x_ref[...]
        r = lax.rsqrt(jnp.mean(xv * xv, axis=-1, keepdims=True) + EPS)
        o_ref[...] = (xv * r * g_ref[...]).astype(o_ref.dtype)

    return pl.pallas_call(
        body, name=name, grid=(T // tr,),
        in_specs=[pl.BlockSpec((tr, D), lambda i: (i, 0)), pl.BlockSpec((1, D), lambda i: (0, 0))],
        out_specs=pl.BlockSpec((tr, D), lambda i: (i, 0)),
        out_shape=jax.ShapeDtypeStruct((T, D), out_dtype), compiler_params=_params("parallel"),
    )(x, g)


def _resid_rms(x, z, g, scale, name):
    T, D = x.shape
    tr = _tile(T, ROW_TILE)

    def body(x_ref, z_ref, g_ref, o_ref):
        zv = z_ref[...]
        r = lax.rsqrt(jnp.mean(zv * zv, axis=-1, keepdims=True) + EPS)
        o_ref[...] = x_ref[...] + scale * (zv * r * g_ref[...])

    row = pl.BlockSpec((tr, D), lambda i: (i, 0))
    return pl.pallas_call(
        body, name=name, grid=(T // tr,), in_specs=[row, row, pl.BlockSpec((1, D), lambda i: (0, 0))],
        out_specs=row, out_shape=jax.ShapeDtypeStruct((T, D), F32), compiler_params=_params("parallel"),
    )(x, z, g)


def _final_loss(x, z, g, scale, target, name):
    T, D = x.shape
    tr = _tile(T, ROW_TILE)

    def body(x_ref, z_ref, g_ref, t_ref, dx_ref, acc_ref):
        @pl.when(pl.program_id(0) == 0)
        def _():
            acc_ref[...] = jnp.zeros_like(acc_ref)

        zv = z_ref[...]
        r = lax.rsqrt(jnp.mean(zv * zv, axis=-1, keepdims=True) + EPS)
        e = x_ref[...] + scale * (zv * r * g_ref[...]) - t_ref[...]
        dx_ref[...] = e * (1.0 / D)
        acc_ref[...] += _colsum(_rowsum(e * e))

    row = pl.BlockSpec((tr, D), lambda i: (i, 0))
    return pl.pallas_call(
        body, name=name, grid=(T // tr,), in_specs=[row, row, pl.BlockSpec((1, D), lambda i: (0, 0)), row],
        out_specs=[row, pl.BlockSpec((8, LANES), lambda i: (0, 0))],
        out_shape=[jax.ShapeDtypeStruct((T, D), F32), jax.ShapeDtypeStruct((8, LANES), F32)],
        compiler_params=_params("arbitrary"),
    )(x, z, g, target)


def _rms_bwd(xin, g, dy, scale, name, out_dtype, resid=None):
    T, D = xin.shape
    tr = _tile(T, ROW_TILE)
    has_resid = resid is not None

    def body(*refs):
        x_ref, g_ref, dy_ref = refs[:3]
        r_ref = refs[3] if has_resid else None
        dx_ref, dg_ref = refs[-2], refs[-1]

        @pl.when(pl.program_id(0) == 0)
        def _():
            dg_ref[...] = jnp.zeros_like(dg_ref)

        xv = x_ref[...]
        r = lax.rsqrt(jnp.mean(xv * xv, axis=-1, keepdims=True) + EPS)
        xh = xv * r
        dyv = dy_ref[...].astype(F32) * scale
        dxh = dyv * g_ref[...]
        dx = r * (dxh - xh * jnp.mean(dxh * xh, axis=-1, keepdims=True))
        if has_resid:
            dx = dx + r_ref[...]
        dx_ref[...] = dx.astype(dx_ref.dtype)
        dg_ref[...] += _colsum(dyv * xh)

    row = pl.BlockSpec((tr, D), lambda i: (i, 0))
    vec = pl.BlockSpec((1, D), lambda i: (0, 0))
    return pl.pallas_call(
        body, name=name, grid=(T // tr,), in_specs=[row, vec, row] + ([row] if has_resid else []),
        out_specs=[row, vec],
        out_shape=[jax.ShapeDtypeStruct((T, D), out_dtype), jax.ShapeDtypeStruct((1, D), F32)],
        compiler_params=_params("arbitrary"),
    )(*((xin, g, dy) + ((resid,) if has_resid else ())))


def _mm_swiglu(h, w_in, name, hosted=None):
    T, K = h.shape
    F2 = w_in.shape[1]
    tf = SWIGLU_TILE
    tm = _tile(T, _MM_TILES["nn"][0])

    def body(h_ref, w_ref, u_ref, a_ref):
        u = _dot(h_ref[...], w_ref[...])
        u_ref[...] = u.astype(u_ref.dtype)
        gate, up = u[:, :tf], u[:, tf:]
        a_ref[...] = (gate * _sig_approx(gate) * up).astype(a_ref.dtype)

    (u, a), landed = _host_call(
        body, name, (T // tm, F2 // (2 * tf)),
        [pl.BlockSpec((tm, K), lambda i, j: (i, 0)), pl.BlockSpec((K, 2 * tf), lambda i, j: (0, j))],
        [pl.BlockSpec((tm, 2 * tf), lambda i, j: (i, j)), pl.BlockSpec((tm, tf), lambda i, j: (i, j))],
        [jax.ShapeDtypeStruct((T, F2), BF16), jax.ShapeDtypeStruct((T, F2 // 2), BF16)], [],
        ("parallel", "parallel"), (h, w_in), hosted)
    return u, a, landed


def _mm_swiglu_bwd(dz, w_down, u, name):
    T, D = dz.shape
    F = w_down.shape[0]
    tf = SWIGLU_TILE
    tm = _tile(T, _MM_TILES["nn"][0])

    def body(dz_ref, w_ref, u_ref, o_ref):
        d = _dot(dz_ref[...], w_ref[...], "nt")
        gate = u_ref[:, :tf].astype(F32)
        up = u_ref[:, tf:].astype(F32)
        s = _sig_approx(gate)
        o_ref[:, :tf] = (d * up * (s * (1.0 + gate * (1.0 - s)))).astype(o_ref.dtype)
        o_ref[:, tf:] = (d * gate * s).astype(o_ref.dtype)

    return pl.pallas_call(
        body, name=name, grid=(T // tm, F // tf),
        in_specs=[pl.BlockSpec((tm, D), lambda i, j: (i, 0)), pl.BlockSpec((tf, D), lambda i, j: (j, 0)),
                  pl.BlockSpec((tm, 2 * tf), lambda i, j: (i, j))],
        out_specs=pl.BlockSpec((tm, 2 * tf), lambda i, j: (i, j)),
        out_shape=jax.ShapeDtypeStruct((T, 2 * F), BF16), compiler_params=_params("parallel", "parallel"),
    )(dz, w_down, u)


def _hgout_fwd(o_a, pm, g, name):
    T, D = o_a.shape
    tr = _tile(T, ROW_TILE)

    def body(o_ref, ga_ref, g_ref, out_ref):
        ov = o_ref[...]
        r = lax.rsqrt(jnp.mean(ov * ov, axis=-1, keepdims=True) + EPS)
        ga = ga_ref[...].astype(F32)
        out_ref[...] = (ov * r * g_ref[...] * (ga * _sig(ga))).astype(out_ref.dtype)

    row = pl.BlockSpec((tr, D), lambda i: (i, 0))
    return pl.pallas_call(
        body, name=name, grid=(T // tr,),
        in_specs=[row, pl.BlockSpec((tr, D), lambda i: (i, 3)), pl.BlockSpec((1, D), lambda i: (0, 0))],
        out_specs=row, out_shape=jax.ShapeDtypeStruct((T, D), BF16), compiler_params=_params("parallel"),
    )(o_a, pm, g)


def _hgout_bwd(o_a, pm, g, d_out, name):
    T, D = o_a.shape
    tr = _tile(T, ROW_TILE)

    def body(o_ref, ga_ref, g_ref, d_ref, do_ref, dga_ref, dg_ref):
        @pl.when(pl.program_id(0) == 0)
        def _():
            dg_ref[...] = jnp.zeros_like(dg_ref)

        ov = o_ref[...]
        r = lax.rsqrt(jnp.mean(ov * ov, axis=-1, keepdims=True) + EPS)
        oh = ov * r
        ga = ga_ref[...].astype(F32)
        s = _sig(ga)
        d = d_ref[...].astype(F32)
        dn = d * (ga * s)
        dga_ref[...] = (d * (oh * g_ref[...]) * (s * (1.0 + ga * (1.0 - s)))).astype(dga_ref.dtype)
        dxh = dn * g_ref[...]
        do_ref[...] = (r * (dxh - oh * jnp.mean(dxh * oh, axis=-1, keepdims=True))).astype(do_ref.dtype)
        dg_ref[...] += _colsum(dn * oh)

    row = pl.BlockSpec((tr, D), lambda i: (i, 0))
    vec = pl.BlockSpec((1, D), lambda i: (0, 0))
    return pl.pallas_call(
        body, name=name, grid=(T // tr,), in_specs=[row, pl.BlockSpec((tr, D), lambda i: (i, 3)), vec, row],
        out_specs=[row, row, vec],
        out_shape=[jax.ShapeDtypeStruct((T, D), BF16), jax.ShapeDtypeStruct((T, D), BF16),
                   jax.ShapeDtypeStruct((1, D), F32)],
        compiler_params=_params("arbitrary"),
    )(o_a, pm, g, d_out)


def _merge_fwd(ya, yb, pg, bg, name):
    T, D = ya.shape
    tr = _tile(T, 256)

    def body(ya_ref, yb_ref, pg_ref, bg_ref, o_ref):
        g0 = _sig(pg_ref[:, :D].astype(F32) + bg_ref[:, :D])
        g1 = _sig(pg_ref[:, D:].astype(F32) + bg_ref[:, D:])
        o_ref[...] = (g0 * ya_ref[...].astype(F32) + g1 * yb_ref[...].astype(F32)).astype(o_ref.dtype)

    row = pl.BlockSpec((tr, D), lambda i: (i, 0))
    return pl.pallas_call(
        body, name=name, grid=(T // tr,),
        in_specs=[row, row, pl.BlockSpec((tr, 2 * D), lambda i: (i, 0)), pl.BlockSpec((1, 2 * D), lambda i: (0, 0))],
        out_specs=row, out_shape=jax.ShapeDtypeStruct((T, D), BF16), compiler_params=_params("parallel"),
    )(ya, yb, pg, bg)


def _merge_bwd(dy, ya, yb, pg, bg, name):
    T, D = ya.shape
    tr = _tile(T, 256)

    def body(dy_ref, ya_ref, yb_ref, pg_ref, bg_ref, dya_ref, dyb_ref, dpg_ref, dbg_ref):
        @pl.when(pl.program_id(0) == 0)
        def _():
            dbg_ref[...] = jnp.zeros_like(dbg_ref)

        d = dy_ref[...].astype(F32)
        g0 = _sig(pg_ref[:, :D].astype(F32) + bg_ref[:, :D])
        g1 = _sig(pg_ref[:, D:].astype(F32) + bg_ref[:, D:])
        dya_ref[...] = (d * g0).astype(dya_ref.dtype)
        dyb_ref[...] = (d * g1).astype(dyb_ref.dtype)
        dg0 = d * ya_ref[...].astype(F32) * (g0 * (1.0 - g0))
        dg1 = d * yb_ref[...].astype(F32) * (g1 * (1.0 - g1))
        dpg_ref[:, :D] = dg0.astype(dpg_ref.dtype)
        dpg_ref[:, D:] = dg1.astype(dpg_ref.dtype)
        dbg_ref[:, :D] += _colsum(dg0)
        dbg_ref[:, D:] += _colsum(dg1)

    row = pl.BlockSpec((tr, D), lambda i: (i, 0))
    wide = pl.BlockSpec((tr, 2 * D), lambda i: (i, 0))
    wvec = pl.BlockSpec((1, 2 * D), lambda i: (0, 0))
    return pl.pallas_call(
        body, name=name, grid=(T // tr,), in_specs=[row, row, row, wide, wvec],
        out_specs=[row, row, wide, wvec],
        out_shape=[jax.ShapeDtypeStruct((T, D), BF16), jax.ShapeDtypeStruct((T, D), BF16),
                   jax.ShapeDtypeStruct((T, 2 * D), BF16), jax.ShapeDtypeStruct((1, 2 * D), F32)],
        compiler_params=_params("arbitrary"),
    )(dy, ya, yb, pg, bg)


def _hgrn_chunk_terms(q, fl, lb, tri):
    shape = q.shape
    row = _iota(shape, 0)
    sg = _sig(fl)
    f = lb + (1.0 - lb) * sg
    k = 1.0 - f
    b = _tri_dot(tri, jnp.log(f))
    ref1 = jnp.where(row < HALF, _pick_row(b, HALF // 2), _pick_row(b, HALF + HALF // 2))
    b_half = _pick_row(b, HALF - 1)
    b_last = _pick_row(b, CHUNK - 1)
    sq = _sig(q)
    qs = q * sq
    e_q1 = jnp.exp(jnp.minimum(b - ref1, EXP_CLAMP))
    e_k1 = jnp.exp(jnp.minimum(ref1 - b, EXP_CLAMP))
    e_q2 = jnp.exp(jnp.minimum(b - b_half, 0.0))
    e_k2 = jnp.exp(jnp.minimum(b_half - b, 0.0))
    e_b = jnp.exp(b)
    e_kd = jnp.exp(b_last - b)
    return dict(sg=sg, f=f, k=k, sq=sq, qs=qs, e_q1=e_q1, e_k1=e_k1, e_q2=e_q2, e_k2=e_k2, e_b=e_b, e_kd=e_kd,
                e_last=jnp.exp(b_last))


def _hgrn_masks():
    r = _iota((CHUNK, CHUNK), 0)
    c = _iota((CHUNK, CHUNK), 1)
    causal = r >= c
    same = (r < HALF) == (c < HALF)
    return causal, causal & same, (r >= HALF) & (c < HALF)


def _softmax_lb(lbl_ref):
    l0, l1 = lbl_ref[0, 0], lbl_ref[1, 0]
    mx = jnp.maximum(l0, l1)
    e0, e1 = jnp.exp(l0 - mx), jnp.exp(l1 - mx)
    return e0 / (e0 + e1)


def _hgrn_fwd(pm, lbl, name):
    T = pm.shape[0]
    tb = _tile(T, SEQ_BLOCK)
    nc = tb // CHUNK

    def body(q_ref, f_ref, i_ref, lbl_ref, o_ref, st_ref, s_sc):
        @pl.when(pl.program_id(1) == 0)
        def _():
            s_sc[...] = jnp.zeros_like(s_sc)

        lb = _softmax_lb(lbl_ref)
        causal, m1, m2 = _hgrn_masks()
        tri = jnp.where(causal, 1.0, 0.0).astype(BF16)
        parts = []
        for ci in range(nc):
            sl = pl.ds(ci * CHUNK, CHUNK)
            t = _hgrn_chunk_terms(q_ref[sl, :].astype(F32), f_ref[sl, :].astype(F32), lb, tri)
            iv = i_ref[sl, :]
            a1 = _dot((t["qs"] * t["e_q1"]).astype(BF16), (t["k"] * t["e_k1"]).astype(BF16), "nt")
            a2 = _dot((t["qs"] * t["e_q2"]).astype(BF16), (t["k"] * t["e_k2"]).astype(BF16), "nt")
            a = jnp.where(m1, a1, 0.0) + jnp.where(m2, a2, 0.0)
            parts.append((_dot(a.astype(BF16), iv), (t["qs"] * t["e_b"]).astype(BF16),
                          _dot(iv, (t["k"] * t["e_kd"]).astype(BF16), "tn"), t["e_last"]))
        st = s_sc[...]
        for ci, (o_intra, qi, grow, e_last) in enumerate(parts):
            st_ref[0, ci] = st
            o_ref[pl.ds(ci * CHUNK, CHUNK), :] = o_intra + _dot(qi, st.astype(BF16), "nt")
            st = e_last * st + grow
        s_sc[...] = st

    blk = lambda off: pl.BlockSpec((tb, DH), lambda h, b: (b, off + h))
    return pl.pallas_call(
        body, name=name, grid=(HEADS, T // tb),
        in_specs=[blk(0), blk(HEADS), blk(2 * HEADS), pl.BlockSpec((2, 1, 1, DH), lambda h, b: (0, h, 0, 0))],
        out_specs=[pl.BlockSpec((tb, DH), lambda h, b: (b, h)),
                   pl.BlockSpec((1, nc, DH, DH), lambda h, b: (h, b, 0, 0))],
        out_shape=[jax.ShapeDtypeStruct((T, HEADS * DH), F32),
                   jax.ShapeDtypeStruct((HEADS, T // CHUNK, DH, DH), F32)],
        scratch_shapes=[pltpu.VMEM((DH, DH), F32)],
        compiler_params=_params("parallel", "arbitrary"),
    )(pm, pm, pm, lbl)


def _hgrn_bwd(pm, lbl, states, do, name):
    T = pm.shape[0]
    tb = _tile(T, SEQ_BLOCK)
    nc = tb // CHUNK
    nb = T // tb

    def body(q_ref, f_ref, i_ref, lbl_ref, st_ref, do_ref, dq_ref, df_ref, di_ref, dl_ref, ds_sc, dlb_sc):
        @pl.when(pl.program_id(1) == 0)
        def _():
            ds_sc[...] = jnp.zeros_like(ds_sc)
            dlb_sc[...] = jnp.zeros_like(dlb_sc)

        lb = _softmax_lb(lbl_ref)
        causal, m1, m2 = _hgrn_masks()
        tri = jnp.where(causal, 1.0, 0.0).astype(BF16)
        tri_rev = jnp.where(_iota((CHUNK, CHUNK), 0) <= _iota((CHUNK, CHUNK), 1), 1.0, 0.0).astype(BF16)
        last_row = _iota((CHUNK, DH), 0) == CHUNK - 1
        dsn = ds_sc[...]
        dlb = jnp.zeros((1, DH), F32)
        for ci in reversed(range(nc)):
            sl = pl.ds(ci * CHUNK, CHUNK)
            q = q_ref[sl, :].astype(F32)
            t = _hgrn_chunk_terms(q, f_ref[sl, :].astype(F32), lb, tri)
            iv = i_ref[sl, :]
            dov = do_ref[sl, :]
            qe1, ke1 = t["qs"] * t["e_q1"], t["k"] * t["e_k1"]
            qe2, ke2 = t["qs"] * t["e_q2"], t["k"] * t["e_k2"]
            qi, kd = t["qs"] * t["e_b"], t["k"] * t["e_kd"]
            qe1b, ke1b, qe2b, ke2b = qe1.astype(BF16), ke1.astype(BF16), qe2.astype(BF16), ke2.astype(BF16)
            a = jnp.where(m1, _dot(qe1b, ke1b, "nt"), 0.0) + jnp.where(m2, _dot(qe2b, ke2b, "nt"), 0.0)
            st = st_ref[0, ci]
            dsnb = dsn.astype(BF16)
            da = _dot(dov, iv, "nt")
            da1 = jnp.where(m1, da, 0.0).astype(BF16)
            da2 = jnp.where(m2, da, 0.0).astype(BF16)
            di_ref[sl, :] = (_dot(a.astype(BF16), dov, "tn") + _dot(kd.astype(BF16), dsnb, "nt")).astype(di_ref.dtype)
            dqe1, dke1 = _dot(da1, ke1b), _dot(da1, qe1b, "tn")
            dqe2, dke2 = _dot(da2, ke2b), _dot(da2, qe2b, "tn")
            dqi = _dot(dov, st.astype(BF16))
            dkd = _dot(iv, dsnb)
            ds_before = t["e_last"] * dsn + _dot(dov, qi.astype(BF16), "tn")
            dqs =dqe1 * t["e_q1"] + dqe2 * t["e_q2"] + dqi * t["e_b"]
            dk = dke1 * t["e_k1"] + dke2 * t["e_k2"] + dkd * t["e_kd"]
            qib, kdb = qi.astype(BF16).astype(F32), kd.astype(BF16).astype(F32)
            db = (dqe1 * qe1b.astype(F32) - dke1 * ke1b.astype(F32) + dqe2 * qe2b.astype(F32)
                  - dke2 * ke2b.astype(F32) + dqi * qib - dkd * kdb)
            extra = _colsum(dkd * kdb) + t["e_last"] * _colsum(dsn * st)
            db = db + jnp.where(last_row, extra, 0.0)
            dlf = _tri_dot(tri_rev, db)
            dfv = dlf / t["f"] - dk
            sg = t["sg"]
            df_ref[sl, :] = (dfv * (1.0 - lb) * sg * (1.0 - sg)).astype(df_ref.dtype)
            dlb = dlb + _colsum(dfv * (1.0 - sg))
            sq = t["sq"]
            dq_ref[sl, :] = (dqs * (sq * (1.0 + q * (1.0 - sq)))).astype(dq_ref.dtype)
            dsn = ds_before
        ds_sc[...] = dsn
        dlb_sc[...] += dlb

        @pl.when(pl.program_id(1) == nb - 1)
        def _():
            dl0 = dlb_sc[...] * lb * (1.0 - lb)
            dl_ref[0, 0] = dl0
            dl_ref[1, 0] = -dl0

    blk = lambda off: pl.BlockSpec((tb, DH), lambda h, b: (nb - 1 - b, off + h))
    lspec = pl.BlockSpec((2, 1, 1, DH), lambda h, b: (0, h, 0, 0))
    out_blk = pl.BlockSpec((tb, DH), lambda h, b: (nb - 1 - b, h))
    D = HEADS * DH
    return pl.pallas_call(
        body, name=name, grid=(HEADS, nb),
        in_specs=[blk(0), blk(HEADS), blk(2 * HEADS), lspec,
                  pl.BlockSpec((1, nc, DH, DH), lambda h, b: (h, nb - 1 - b, 0, 0)), out_blk],
        out_specs=[out_blk, out_blk, out_blk, lspec],
        out_shape=[jax.ShapeDtypeStruct((T, D), BF16)] * 3 + [jax.ShapeDtypeStruct((2, HEADS, 1, DH), F32)],
        scratch_shapes=[pltpu.VMEM((DH, DH), F32), pltpu.VMEM((1, DH), F32)],
        compiler_params=_params("parallel", "arbitrary"),
    )(pm, pm, pm, lbl, states, do)


def _log_sigmoid(x):
    return jnp.minimum(x, 0.0) - jnp.log(1.0 + jnp.exp(-jnp.abs(x)))


def _fox_cumsum(pf, bias, name):
    T = pf.shape[0]
    tb = _tile(T, CUMSUM_BLOCK)

    def body(x_ref, b_ref, c_ref, carry):
        @pl.when(pl.program_id(0) == 0)
        def _():
            carry[...] = jnp.zeros_like(carry)

        tri = jnp.where(_iota((tb, tb), 0) >= _iota((tb, tb), 1), 1.0, 0.0).astype(BF16)
        c = _tri_dot(tri, _log_sigmoid(x_ref[...] + b_ref[...])) + carry[...]
        c_ref[...] = c
        carry[...] = _pick_row(c, tb - 1)

    row = pl.BlockSpec((tb, LANES), lambda i: (i, 0))
    return pl.pallas_call(
        body, name=name, grid=(T // tb,), in_specs=[row, pl.BlockSpec((1, LANES), lambda i: (0, 0))],
        out_specs=row, out_shape=jax.ShapeDtypeStruct((T, LANES), F32),
        scratch_shapes=[pltpu.VMEM((1, LANES), F32)], compiler_params=_params("arbitrary"),
    )(pf, bias)


def _fox_dcum(dc, pf, bias, name):
    T = pf.shape[0]
    tb = _tile(T, CUMSUM_BLOCK)
    nb = T // tb

    def body(dc_ref, x_ref, b_ref, dx_ref, db_ref, carry):
        @pl.when(pl.program_id(0) == 0)
        def _():
            carry[...] = jnp.zeros_like(carry)
            db_ref[...] = jnp.zeros_like(db_ref)

        tri_rev = jnp.where(_iota((tb, tb), 0) <= _iota((tb, tb), 1), 1.0, 0.0).astype(BF16)
        dls = _tri_dot(tri_rev, dc_ref[...]) + carry[...]
        carry[...] = _pick_row(dls, 0)
        dx = dls * (1.0 - _sig(x_ref[...] + b_ref[...]))
        dx_ref[...] = dx
        db_ref[...] += _colsum(dx)

    row = pl.BlockSpec((tb, LANES), lambda i: (nb - 1 - i, 0))
    vec = pl.BlockSpec((1, LANES), lambda i: (0, 0))
    return pl.pallas_call(
        body, name=name, grid=(nb,), in_specs=[row, row, vec], out_specs=[row, vec],
        out_shape=[jax.ShapeDtypeStruct((T, LANES), F32), jax.ShapeDtypeStruct((1, LANES), F32)],
        scratch_shapes=[pltpu.VMEM((1, LANES), F32)], compiler_params=_params("arbitrary"),
    )(dc, pf, bias)


_Q_OFF, _K_OFF, _V_OFF = 4 * HEADS, 5 * HEADS, 6 * HEADS


def _causal_pairs(nq, by_key):
    if by_key:
        pairs = [(i, j) for j in range(nq) for i in range(j, nq)]
    else:
        pairs = [(i, j) for i in range(nq) for j in range(i + 1)]
    return jnp.asarray([p[0] for p in pairs], jnp.int32), jnp.asarray([p[1] for p in pairs], jnp.int32)


def _fox_logits(q, k, ck, row0, masked):
    s = _dot(q, k, "nt") - ck
    if masked:
        s = jnp.where(_iota(s.shape, 0) + row0 >= _iota(s.shape, 1), s, NEG_BIG)
    return s


def _ones_column(rows):
    return jnp.where(_iota((rows, DH), 1) == 0, 1.0, 0.0).astype(BF16)


def _fox_fwd(pm, c_col, c_row, name):
    T = pm.shape[0]
    tq = _tile(T, ATTN_TILE)
    nq = T // tq
    rg = min(ATTN_ROWS, tq)
    qi_tab, kj_tab = _causal_pairs(nq, by_key=False)

    def body(qi_ref, kj_ref, q_ref, k_ref, v_ref, cq_ref, ck_ref, o_ref, lse_ref, m_sc, acc_sc):
        t = pl.program_id(1)
        i, j = qi_ref[t], kj_ref[t]

        @pl.when(j == 0)
        def _():
            m_sc[...] = jnp.full_like(m_sc, NEG_BIG)
            acc_sc[...] = jnp.zeros_like(acc_sc)

        def step(diag):
            m_all, acc_all = m_sc[...], acc_sc[...]
            ones = _ones_column(tq)
            ms, accs = [], []
            for r in range(tq // rg):
                rows = slice(r * rg, (r + 1) * rg)
                w = (r + 1) * rg if diag else tq
                cq = cq_ref[0, rows, :]
                s = _fox_logits(q_ref[rows, :], k_ref[:w, :], ck_ref[0, :, :w], r * rg, diag)
                m_old = m_all[rows, :]
                m_new = jnp.maximum(m_old, jnp.max(s, axis=1, keepdims=True) + cq)
                alpha = jnp.exp(m_old - m_new)
                p = jnp.exp((s - (m_new - cq)).astype(BF16))
                v_one = jnp.concatenate([v_ref[:w, :], ones[:w, :]], axis=1)
                ms.append(m_new)
                accs.append(alpha * acc_all[rows, :] + _dot(p, v_one))
            m_sc[...] = jnp.concatenate(ms, axis=0)
            acc_sc[...] = jnp.concatenate(accs, axis=0)

        @pl.when(j < i)
        def _():
            step(False)

        @pl.when(j == i)
        def _():
            step(True)
            acc = acc_sc[...]
            denom = acc[:, DH:DH + 1]
            o_ref[...] = (acc[:, :DH] / denom).astype(o_ref.dtype)
            lse_ref[0] = m_sc[...] + jnp.log(denom)

    kv = lambda off: pl.BlockSpec((tq, DH), lambda h, t, qi, kj: (kj[t], off + h))
    col = pl.BlockSpec((1, tq, 1), lambda h, t, qi, kj: (h, qi[t], 0))
    grid_spec = pltpu.PrefetchScalarGridSpec(
        num_scalar_prefetch=2, grid=(HEADS, qi_tab.shape[0]),
        in_specs=[pl.BlockSpec((tq, DH), lambda h, t, qi, kj: (qi[t], _Q_OFF + h)), kv(_K_OFF), kv(_V_OFF), col,
                  pl.BlockSpec((1, 1, tq), lambda h, t, qi, kj: (h, 0, kj[t]))],
        out_specs=[pl.BlockSpec((tq, DH), lambda h, t, qi, kj: (qi[t], h)), col],
        scratch_shapes=[pltpu.VMEM((tq, 1), F32), pltpu.VMEM((tq, 2 * DH), F32)])
    return pl.pallas_call(
        body, name=name, grid_spec=grid_spec,
        out_shape=[jax.ShapeDtypeStruct((T, HEADS * DH), BF16), jax.ShapeDtypeStruct((HEADS, T, 1), F32)],
        compiler_params=_params("parallel", "arbitrary"),
    )(qi_tab, kj_tab, pm, pm, pm, c_col, c_row)


def _fox_delta(do, o, name):
    T, D = o.shape
    tr = _tile(T, ROW_TILE)

    def body(do_ref, o_ref, d_ref):
        prod = do_ref[...].astype(F32) * o_ref[...].astype(F32)
        for h in range(HEADS):
            d_ref[h] = _rowsum(prod[:, h * DH:(h + 1) * DH])

    row = pl.BlockSpec((tr, D), lambda i: (i, 0))
    return pl.pallas_call(
        body, name=name, grid=(T // tr,), in_specs=[row, row],
        out_specs=pl.BlockSpec((HEADS, tr, 1), lambda i: (0, i, 0)),
        out_shape=jax.ShapeDtypeStruct((HEADS, T, 1), F32), compiler_params=_params("parallel"),
    )(do, o)


def _fox_bwd(pm, c_col, c_row, do, lse, delta, name):
    T = pm.shape[0]
    tq = _tile(T, ATTN_TILE)
    nq = T // tq
    rg = min(ATTN_ROWS, tq)
    qi_tab, kj_tab = _causal_pairs(nq, by_key=True)
    npairs = qi_tab.shape[0]

    def body(qi_ref, kj_ref, q_ref, k_ref, v_ref, cq_ref, ck_ref, do_ref, lse_ref, dl_ref,
             dq_ref, dk_ref, dv_ref, rsum_ref, csum_ref, dq_sc, dk_sc, dv_sc):
        t = pl.program_id(1)
        i, j = qi_ref[t], kj_ref[t]

        @pl.when(t == 0)
        def _():
            dq_sc[...] = jnp.zeros_like(dq_sc)

        @pl.when(i == j)
        def _():
            dk_sc[...] = jnp.zeros_like(dk_sc)
            dv_sc[...] = jnp.zeros_like(dv_sc)

        base = pl.multiple_of(i * tq, tq)

        def step(diag):
            ones = _ones_column(tq)
            for r in range(tq // rg):
                rows = slice(r * rg, (r + 1) * rg)
                w = (r + 1) * rg if diag else tq
                qr, dor = q_ref[rows, :], do_ref[rows, :]
                s = _fox_logits(qr, k_ref[:w, :], ck_ref[0, :, :w], r * rg, diag)
                p = jnp.exp((s - (lse_ref[0, rows, :] - cq_ref[0, rows, :])).astype(BF16))
                dp = _dot(dor, v_ref[:w, :], "nt")
                dsb = (p.astype(F32) * (dp - dl_ref[0, rows, :])).astype(BF16)
                dv_sc[:w, :] += _dot(p, dor, "tn")
                dk_sc[:w, :] += _dot(dsb, jnp.concatenate([qr, ones[rows, :]], axis=1), "tn")
                dq_sc[pl.ds(base + r * rg, rg), :] += _dot(dsb, jnp.concatenate([k_ref[:w, :], ones[:w, :]], axis=1))

        @pl.when(i > j)
        def _():
            step(False)

        @pl.when(i == j)
        def _():
            step(True)

        @pl.when(i == nq - 1)
        def _():
            dk_ref[...] = dk_sc[:, :DH].astype(dk_ref.dtype)
            dv_ref[...] = dv_sc[...].astype(dv_ref.dtype)
            csum_ref[0] = dk_sc[:, DH:DH + 1]

        @pl.when(t == npairs - 1)
        def _():
            dq_ref[...] = dq_sc[:, :DH].astype(dq_ref.dtype)
            rsum_ref[0] = dq_sc[:, DH:DH + 1]

    col = pl.BlockSpec((1, tq, 1), lambda h, t, qi, kj: (h, qi[t], 0))
    kv = lambda off: pl.BlockSpec((tq, DH), lambda h, t, qi, kj: (kj[t], off + h))
    kv_out = pl.BlockSpec((tq, DH), lambda h, t, qi, kj: (kj[t], h))
    grid_spec = pltpu.PrefetchScalarGridSpec(
        num_scalar_prefetch=2, grid=(HEADS, npairs),
        in_specs=[pl.BlockSpec((tq, DH), lambda h, t, qi, kj: (qi[t], _Q_OFF + h)), kv(_K_OFF), kv(_V_OFF), col,
                  pl.BlockSpec((1, 1, tq), lambda h, t, qi, kj: (h, 0, kj[t])),
                  pl.BlockSpec((tq, DH), lambda h, t, qi, kj: (qi[t], h)), col, col],
        out_specs=[pl.BlockSpec((T, DH), lambda h, t, qi, kj: (0, h)), kv_out, kv_out,
                   pl.BlockSpec((1, T, 1), lambda h, t, qi, kj: (h, 0, 0)),
                   pl.BlockSpec((1, tq, 1), lambda h, t, qi, kj: (h, kj[t], 0))],
        scratch_shapes=[pltpu.VMEM((T, 2 * DH), F32), pltpu.VMEM((tq, 2 * DH), F32), pltpu.VMEM((tq, DH), F32)])
    D = HEADS * DH
    return pl.pallas_call(
        body, name=name, grid_spec=grid_spec,
        out_shape=[jax.ShapeDtypeStruct((T, D), BF16)] * 3 + [jax.ShapeDtypeStruct((HEADS, T, 1), F32)] * 2,
        compiler_params=_params("parallel", "arbitrary"),
    )(qi_tab, kj_tab, pm, pm, pm, c_col, c_row, do, lse, delta)


def _xattn_fwd(q, kv, name):
    T, D = q.shape
    M = kv.shape[0]
    dh = D // MEM_HEADS
    tq = _tile(T, XATTN_TILE)
    scale = 1.0 / math.sqrt(dh)

    def body(q_ref, kv_ref, o_ref):
        for h in range(MEM_HEADS):
            cs = slice(h * dh, (h + 1) * dh)
            s = _dot(q_ref[:, cs], kv_ref[:, cs], "nt") * scale
            p = jnp.exp(s - jnp.max(s, axis=1, keepdims=True))
            p = p / _rowsum(p)
            o_ref[:, cs] = _dot(p.astype(BF16), kv_ref[:, D + h * dh:D + (h + 1) * dh]).astype(o_ref.dtype)

    row = pl.BlockSpec((tq, D), lambda i: (i, 0))
    return pl.pallas_call(
        body, name=name, grid=(T // tq,), in_specs=[row, pl.BlockSpec((M, 2 * D), lambda i: (0, 0))],
        out_specs=row, out_shape=jax.ShapeDtypeStruct((T, D), BF16), compiler_params=_params("parallel"),
    )(q, kv)


def _xattn_bwd(q, kv, do, name):
    T, D = q.shape
    M = kv.shape[0]
    dh = D // MEM_HEADS
    tq = _tile(T, XATTN_TILE)
    scale = 1.0 / math.sqrt(dh)

    def body(q_ref, kv_ref, do_ref, dq_ref, dkv_ref):
        @pl.when(pl.program_id(0) == 0)
        def _():
            dkv_ref[...] = jnp.zeros_like(dkv_ref)

        for h in range(MEM_HEADS):
            cs = slice(h * dh, (h + 1) * dh)
            vs = slice(D + h * dh, D + (h + 1) * dh)
            s = _dot(q_ref[:, cs], kv_ref[:, cs], "nt") * scale
            p = jnp.exp(s - jnp.max(s, axis=1, keepdims=True))
            p = p / _rowsum(p)
            dp = _dot(do_ref[:, cs], kv_ref[:, vs], "nt")
            ds = (p * (dp - _rowsum(p * dp)) * scale).astype(BF16)
            dq_ref[:, cs] = _dot(ds, kv_ref[:, cs]).astype(dq_ref.dtype)
            dkv_ref[:, cs] += _dot(ds, q_ref[:, cs], "tn")
            dkv_ref[:, vs] += _dot(p.astype(BF16), do_ref[:, cs], "tn")

    row = pl.BlockSpec((tq, D), lambda i: (i, 0))
    full = pl.BlockSpec((M, 2 * D), lambda i: (0, 0))
    return pl.pallas_call(
        body, name=name, grid=(T // tq,), in_specs=[row, full, row], out_specs=[row, full],
        out_shape=[jax.ShapeDtypeStruct((T, D), BF16), jax.ShapeDtypeStruct((M, 2 * D), F32)],
        compiler_params=_params("arbitrary"),
    )(q, kv, do)


_HBM = pl.BlockSpec(memory_space=pltpu.HBM)


def _position():
    return lax.axis_index("x"), lax.axis_index("y"), lax.axis_index("c")


def _other_chips(x, y):
    return [(1 - x, y), (x, 1 - y), (1 - x, 1 - y)]


class _Exchange:
    def __init__(self, inputs, out_shapes, scratch, copies, inplace=False):
        self.inputs, self.out_shapes, self.scratch, self.copies, self.inplace = inputs, out_shapes, scratch, copies, inplace

    def start(self, in_refs, out_refs, sems):
        for cp in self.copies(in_refs, out_refs, sems, False)[0]:
            cp.start()

    def wait(self, in_refs, out_refs, sems):
        for cp, how in self.copies(in_refs, out_refs, sems, True)[1]:
            getattr(cp, how)()

    def aliases(self, first_input, first_output):
        return {first_input + w: first_output + w for w in range(len(self.inputs))} if self.inplace else {}


def _run_exchange(ex, name):
    n_in, n_out = len(ex.inputs), len(ex.out_shapes)

    def body(*refs):
        parts = refs[:n_in], refs[n_in:n_in + n_out], refs[n_in + n_out:]
        ex.start(*parts)
        ex.wait(*parts)

    return pl.pallas_call(
        body, name=name, in_specs=[_HBM] * n_in, out_specs=[_HBM] * n_out, out_shape=ex.out_shapes,
        input_output_aliases=ex.aliases(0, 0), scratch_shapes=ex.scratch,
    )(*ex.inputs)


def _chip_exchange(arrays, out_shapes, src_of, dst_of):
    n = len(arrays)

    def copies(srcs, outs, sems, waiting):
        send, recv, local = sems
        x, y, c = _position()
        q = 2 * x + y
        kept, sent, arriving = [], [], []
        for w, (s_ref, o_ref) in enumerate(zip(srcs, outs)):
            kept.append(pltpu.make_async_copy(src_of(s_ref, q, c), dst_of(o_ref, q, c), local.at[w]))
            for j, (px, py) in enumerate(_other_chips(x, y)):
                sems_j = dict(send_sem=send.at[3 * w + j], recv_sem=recv.at[3 * w + j], device_id=(px, py, c),
                              device_id_type=MESH)
                sent.append(pltpu.make_async_remote_copy(src_ref=src_of(s_ref, 2 * px + py, c),
                                                         dst_ref=dst_of(o_ref, q, c), **sems_j))
                if waiting:
                    arriving.append(pltpu.make_async_remote_copy(src_ref=src_of(s_ref, q, c),
                                                                 dst_ref=dst_of(o_ref, 2 * px + py, c), **sems_j))
        return kept + sent, ([(cp, "wait_recv") for cp in arriving] + [(cp, "wait_send") for cp in sent]
                             + [(cp, "wait") for cp in kept])

    scratch = [pltpu.SemaphoreType.DMA((3 * n,)), pltpu.SemaphoreType.DMA((3 * n,)), pltpu.SemaphoreType.DMA((n,))]
    return _Exchange(arrays, out_shapes, scratch, copies)


def _ex_ag_chips(blks):
    return _chip_exchange(blks, [jax.ShapeDtypeStruct((4, 2) + b.shape, b.dtype) for b in blks],
                          src_of=lambda r, chip, c: r, dst_of=lambda r, chip, c: r.at[chip, c])


def _ex_rs_chips(parts):
    return _chip_exchange(parts, [jax.ShapeDtypeStruct(h.shape, h.dtype) for h in parts],
                          src_of=lambda r, chip, c: r.at[chip], dst_of=lambda r, chip, c: r.at[chip])


def _ex_ag_sibling(arrs):
    n = len(arrs)

    def copies(ins, outs, sems, waiting):
        send, recv = sems
        x, y, c = _position()
        to = dict(device_id=(x, y, 1 - c), device_id_type=MESH)
        mine = [pltpu.make_async_remote_copy(src_ref=a.at[:, c], dst_ref=a.at[:, c], send_sem=send.at[w],
                                             recv_sem=recv.at[w], **to) for w, a in enumerate(outs)]
        theirs = [pltpu.make_async_remote_copy(src_ref=a.at[:, c], dst_ref=a.at[:, 1 - c], send_sem=send.at[w],
                                               recv_sem=recv.at[w], **to) for w, a in enumerate(outs if waiting else [])]
        return mine, [(cp, "wait_recv") for cp in theirs] + [(cp, "wait_send") for cp in mine]

    return _Exchange(arrs, [jax.ShapeDtypeStruct(a.shape, a.dtype) for a in arrs],
                     [pltpu.SemaphoreType.DMA((n,)), pltpu.SemaphoreType.DMA((n,))], copies, inplace=True)


def _ex_rs_sibling(blocks):
    n = len(blocks)

    def copies(srcs, outs, sems, waiting):
        send, recv = sems
        x, y, c = _position()
        cps = [pltpu.make_async_remote_copy(src_ref=b.at[:, 1 - c], dst_ref=l, send_sem=send.at[w], recv_sem=recv.at[w],
                                            device_id=(x, y, 1 - c), device_id_type=MESH)
               for w, (b, l) in enumerate(zip(srcs, outs))]
        return cps, [(cp, "wait") for cp in cps]

    return _Exchange(blocks, [jax.ShapeDtypeStruct((4,) + b.shape[2:], b.dtype) for b in blocks],
                     [pltpu.SemaphoreType.DMA((n,)), pltpu.SemaphoreType.DMA((n,))], copies)


def _row_tile(rows, pref=256):
    for t in range(min(pref, rows) // 16 * 16, 0, -16):
        if rows % t == 0:
            return t
    raise ValueError(f"no row tile for {rows}")


def _pair_add(blocks, landed, core, out_dtype, name):
    n, _, s0, s1 = blocks.shape
    tr = _row_tile(s0)

    def body(core_ref, a_ref, b_ref, o_ref):
        del core_ref
        o_ref[...] = (a_ref[...] + b_ref[...]).astype(o_ref.dtype)

    grid_spec = pltpu.PrefetchScalarGridSpec(
        num_scalar_prefetch=1, grid=(n, s0 // tr),
        in_specs=[pl.BlockSpec((1, None, tr, s1), lambda p, i, core: (p, core[0], i, 0)),
                  pl.BlockSpec((1, tr, s1), lambda p, i, core: (p, i, 0))],
        out_specs=pl.BlockSpec((1, tr, s1), lambda p, i, core: (p, i, 0)))
    return pl.pallas_call(
        body, name=name, grid_spec=grid_spec, out_shape=jax.ShapeDtypeStruct(landed.shape, out_dtype),
        compiler_params=_params("parallel", "parallel"),
    )(core, blocks, landed)


def _adamw_math(w, g, m, v):
    m = ADAM_B1 * m + (1.0 - ADAM_B1) * g
    v = ADAM_B2 * v + (1.0 - ADAM_B2) * (g * g)
    m_hat = m / (1.0 - ADAM_B1 ** ADAM_STEP)
    v_hat = v / (1.0 - ADAM_B2 ** ADAM_STEP)
    delta = -ADAM_LR * (m_hat / (jnp.sqrt(v_hat) + ADAM_EPS) + ADAM_WD * w)
    return delta, m, v


def _adamw_reduce(slots, w, m, v, name):
    n, R, C = slots.shape
    tr = _row_tile(R)

    def body(s_ref, w_ref, m_ref, v_ref, g_ref, d_ref, nm_ref, nv_ref):
        g = s_ref[0].astype(F32)
        for p in range(1, n):
            g = g + s_ref[p].astype(F32)
        g_ref[...] = g
        d_ref[...], nm_ref[...], nv_ref[...] = _adamw_math(w_ref[...], g, m_ref[...], v_ref[...])

    row = pl.BlockSpec((tr, C), lambda i: (i, 0))
    return pl.pallas_call(
        body, name=name, grid=(R // tr,), in_specs=[pl.BlockSpec((n, tr, C), lambda i: (0, i, 0)), row, row, row],
        out_specs=[row] * 4, out_shape=[jax.ShapeDtypeStruct((R, C), F32)] * 4, compiler_params=_params("parallel"),
    )(slots, w, m, v)


def _full_from_gathered(a, n):
    s0, s1 = a.shape[2:]
    blk = a.reshape(8, s0, s1)
    if n in COL_SHARDED:
        return blk.transpose(1, 0, 2).reshape(s0, 8 * s1)
    return blk.reshape(8 * s0, s1)


def _blocks_from_full(g, n, shard_shape):
    s0, s1 = shard_shape
    if n in COL_SHARDED:
        blk = g.reshape(s0, 8, s1).transpose(1, 0, 2)
    else:
        blk = g.reshape(8, s0, s1)
    return blk.reshape(4, 2, s0, s1)


def _swiglu_interleave(w):
    d, f2 = w.shape
    return w.reshape(d, 2, f2 // (2 * SWIGLU_TILE), SWIGLU_TILE).transpose(0, 2, 1, 3).reshape(d, f2)


def _swiglu_deinterleave(w):
    d, f2 = w.shape
    return w.reshape(d, f2 // (2 * SWIGLU_TILE), 2, SWIGLU_TILE).transpose(0, 2, 1, 3).reshape(d, f2)


SMALL_ROWS = 16


def _pack_small(vals, loss_row):
    rows = []
    for n in SMALL:
        flat = vals[n].reshape(-1)
        pad = (-flat.shape[0]) % PACK_COLS
        rows.append(jnp.pad(flat, (0, pad)).reshape(-1, PACK_COLS))
    rows.append(loss_row)
    out = jnp.concatenate(rows, axis=0)
    assert out.shape[0] == SMALL_ROWS, out.shape
    return out


def _unpack_small(packed, like):
    out, r = {}, 0
    for n in SMALL:
        size = like[n].size
        rows = -(-size // PACK_COLS)
        out[n] = packed[r:r + rows].reshape(-1)[:size].reshape(like[n].shape)
        r += rows
    return out


class _NoTraffic:
    def host(self, stage):
        return None

    def landed(self, stage, arrays):
        pass

    def grads_ready(self, names, gW, behind=None):
        pass


def _mm_behind(traffic, stage, *args, **kwargs):
    ex = traffic.host(stage)
    if ex is None:
        return _mm(*args, **kwargs)
    out, arrays = _mm(*args, hosted=ex, **kwargs)
    traffic.landed(stage, arrays)
    return out


def _ffn_fwd(x, g_pre, W, tag, traffic, up_stage=None, down_stage=None):
    h = _rms_fwd(x, g_pre, f"{tag}_pre")
    ex = traffic.host(up_stage) if up_stage else None
    u, a, arrays = _mm_swiglu(h, W[f"{tag}_w_in"], f"{tag}_up", hosted=ex)
    if ex is not None:
        traffic.landed(up_stage, arrays)
    z = _mm_behind(traffic, down_stage, a, W[f"{tag}_w_down"], "nn", F32, f"{tag}_down", tk=1408)
    return h, u, a, z


def _ffn_bwd(saved, x, g_pre, w_in, w_down, g_post, dx_out, tag, traffic, down_dw_stage=None, up_dx_stage=None):
    h, u, a, z = saved
    dz, dg_post = _rms_bwd(z, g_post, dx_out, 0.5, f"{tag}_post_bwd", BF16)
    dw_down = _mm_behind(traffic, down_dw_stage, a, dz, "tn", F32, f"{tag}_down_dw", tm=1408)
    du = _mm_swiglu_bwd(dz, w_down, u, f"{tag}_down_dx")
    dh = _mm_behind(traffic, up_dx_stage, du, w_in, "nt", BF16, f"{tag}_up_dx", tk=5632)
    dw_in = _mm(h, du, "tn", F32, f"{tag}_up_dw", tk=4096)
    dx, dg_pre = _rms_bwd(x, g_pre, dh, 1.0, f"{tag}_pre_bwd", F32, resid=dx_out)
    return dx, dg_pre, dg_post, dw_in, dw_down


def _step_local(x, mem, target, W, S, traffic=_NoTraffic()):
    T, D = x.shape
    gW, gS = {}, {}

    f1 = _ffn_fwd(x, S["ffn1_pre_g"], W, "ffn1", traffic, "gather_mixer_chips", "gather_mixer_sibling")
    x1 = _resid_rms(x, f1[3], S["ffn1_post_g"], 0.5, "ffn1_post")

    h2 = _rms_fwd(x1, S["mix_pre_g"], "mix_pre")
    pm = _mm_behind(traffic, "gather_late_chips", h2, W["w_main"], "nn", BF16, "mix_proj_main")
    pf = _mm(h2, W["w_f"], "nn", F32, "mix_proj_f")
    pg = _mm_behind(traffic, "gather_late_sibling", h2, W["w_gates"], "nn", BF16, "mix_proj_gates")
    lbl = S["hg_lb_logits"].reshape(2, HEADS, 1, DH)
    o_a, states = _hgrn_fwd(pm, lbl, "hgrn_fwd")
    oan = _hgout_fwd(o_a, pm, S["hg_norm_g"], "hgrn_out")
    bias = jnp.pad(S["fox_f_bias"], ((0, 0), (0, LANES - HEADS)))
    c = _fox_cumsum(pf, bias, "fox_cumsum")
    c_heads = c[:, :HEADS].T
    c_col, c_row = c_heads[:, :, None], c_heads[:, None, :]
    o_b, lse = _fox_fwd(pm, c_col, c_row, "fox_fwd")
    ya = _mm(oan, W["w_branch_a"], "nn", BF16, "branch_a")
    yb = _mm(o_b, W["w_branch_b"], "nn", BF16, "branch_b")
    y = _merge_fwd(ya, yb, pg, S["b_gate"], "merge")
    z2 = _mm(y, W["w_out"], "nn", F32, "mix_out")
    x2 = _resid_rms(x1, z2, S["mix_post_g"], 1.0, "mix_post")

    h3 = _rms_fwd(x2, S["mem_pre_g"], "mem_pre")
    memn = _rms_fwd(mem, S["mem_kv_g"], "mem_kv_norm")
    qm = _mm(h3, W["w_mq"], "nn", BF16, "mem_q")
    kv = _mm(memn, W["w_mkv"], "nn", BF16, "mem_kv")
    om = _xattn_fwd(qm, kv, "mem_attn")
    z3 = _mm(om, W["w_mo"], "nn", F32, "mem_o")
    x3 = _resid_rms(x2, z3, S["mem_post_g"], 1.0, "mem_post")

    f2 = _ffn_fwd(x3, S["ffn2_pre_g"], W, "ffn2", traffic)
    dx4, sq = _final_loss(x3, f2[3], S["ffn2_post_g"], 0.5, target, "loss")

    dx3, gS["ffn2_pre_g"], gS["ffn2_post_g"], gW["ffn2_w_in"], gW["ffn2_w_down"] = _ffn_bwd(
        f2, x3, S["ffn2_pre_g"], W["ffn2_w_in"], W["ffn2_w_down"], S["ffn2_post_g"], dx4, "ffn2", traffic)
    traffic.grads_ready(["ffn2_w_in", "ffn2_w_down"], gW, behind="scatter_ffn2_sibling")

    dz3, gS["mem_post_g"] = _rms_bwd(z3, S["mem_post_g"], dx3, 1.0, "mem_post_bwd", BF16)
    dom = _mm_behind(traffic, "scatter_ffn2_sibling", dz3, W["w_mo"], "nt", BF16, "mem_o_dx")
    gW["w_mo"] = _mm(om, dz3, "tn", F32, "mem_o_dw")
    dqm, dkv = _xattn_bwd(qm, kv, dom, "mem_attn_bwd")
    dh3 = _mm(dqm, W["w_mq"], "nt", BF16, "mem_q_dx")
    gW["w_mq"] = _mm(h3, dqm, "tn", F32, "mem_q_dw")
    dkvb = dkv.astype(BF16)
    gW["w_mkv"] = _mm(memn, dkvb, "tn", F32, "mem_kv_dw")
    dmemn = _mm(dkvb, W["w_mkv"], "nt", F32, "mem_kv_dx")
    _, gS["mem_kv_g"] = _rms_bwd(mem, S["mem_kv_g"], dmemn, 1.0, "mem_kv_norm_bwd", BF16)
    dx2, gS["mem_pre_g"] = _rms_bwd(x2, S["mem_pre_g"], dh3, 1.0, "mem_pre_bwd", F32, resid=dx3)

    dz2, gS["mix_post_g"] = _rms_bwd(z2, S["mix_post_g"], dx2, 1.0, "mix_post_bwd", BF16)
    dy = _mm(dz2, W["w_out"], "nt", BF16, "mix_out_dx")
    gW["w_out"] = _mm(y, dz2, "tn", F32, "mix_out_dw")
    dya, dyb, dpg, gS["b_gate"] = _merge_bwd(dy, ya, yb, pg, S["b_gate"], "merge_bwd")
    doan = _mm(dya, W["w_branch_a"], "nt", BF16, "branch_a_dx")
    gW["w_branch_a"] = _mm(oan, dya, "tn", F32, "branch_a_dw")
    dob = _mm(dyb, W["w_branch_b"], "nt", BF16, "branch_b_dx")
    gW["w_branch_b"] = _mm(o_b, dyb, "tn", F32, "branch_b_dw")
    traffic.grads_ready(["w_mo", "w_mq", "w_mkv", "w_out", "w_branch_a", "w_branch_b"], gW,
                        behind="scatter_mid_sibling")

    delta = _fox_delta(dob, o_b, "fox_delta")
    dq_b, dk_b, dv_b, ds_rows, ds_cols = _fox_bwd(pm, c_col, c_row, dob, lse, delta, "fox_bwd")
    dc = jnp.pad((ds_rows.reshape(HEADS, T) - ds_cols.reshape(HEADS, T)).T, ((0, 0), (0, LANES - HEADS)))
    dpf, dbias = _fox_dcum(dc, pf, bias, "fox_cumsum_bwd")
    gS["fox_f_bias"] = dbias[:, :HEADS]

    do_a, dg_a, gS["hg_norm_g"] = _hgout_bwd(o_a, pm, S["hg_norm_g"], doan, "hgrn_out_bwd")
    dq_a, df_a, di_a, dlbl = _hgrn_bwd(pm, lbl, states, do_a, "hgrn_bwd")
    gS["hg_lb_logits"] = dlbl.reshape(2, HEADS, DH)

    dpm = jnp.concatenate([dq_a, df_a, di_a, dg_a, dq_b, dk_b, dv_b], axis=1)
    dpf16 = dpf.astype(BF16)
    dh2 = _mm_behind(traffic, "scatter_ffn2_chips", dpm, W["w_main"], "nt", F32, "mix_proj_main_dx")
    dh2 = _mm_behind(traffic, "scatter_mid_sibling", dpg, W["w_gates"], "nt", F32, "mix_proj_gates_dx", add=dh2)
    dh2 = _mm(dpf16, W["w_f"], "nt", F32, "mix_proj_f_dx", add=dh2)
    gW["w_main"] = _mm_behind(traffic, "scatter_mid_chips", h2, dpm, "tn", F32, "mix_proj_main_dw")
    gW["w_gates"] = _mm(h2, dpg, "tn", F32, "mix_proj_gates_dw")
    gW["w_f"] = _mm(h2, dpf16, "tn", F32, "mix_proj_f_dw")
    traffic.grads_ready(["w_in"], gW, behind="scatter_w_in_sibling")
    dx1, gS["mix_pre_g"] = _rms_bwd(x1, S["mix_pre_g"], dh2, 1.0, "mix_pre_bwd", F32, resid=dx2)

    dx0, gS["ffn1_pre_g"], gS["ffn1_post_g"], gW["ffn1_w_in"], gW["ffn1_w_down"] = _ffn_bwd(
        f1, x, S["ffn1_pre_g"], W["ffn1_w_in"], W["ffn1_w_down"], S["ffn1_post_g"], dx1, "ffn1", traffic,
        "scatter_w_in_sibling", "scatter_w_in_chips")
    traffic.grads_ready(["ffn1_w_in", "ffn1_w_down"], gW)
    return sq, dx0, gW, gS


GATHER_FIRST = ["ffn1_w_in", "ffn1_w_down"]
GATHER_MIXER = ["w_in", "w_branch_a", "w_branch_b", "w_out"]
GATHER_LATE = ["w_mq", "w_mkv", "w_mo", "ffn2_w_in", "ffn2_w_down"]
SCATTER_BEHIND = {
    "scatter_ffn2_chips": ["ffn2_w_in", "ffn2_w_down"],
    "scatter_mid_chips": ["w_mo", "w_mq", "w_mkv", "w_out", "w_branch_a", "w_branch_b"],
    "scatter_w_in_chips": ["w_in"],
}


class _Traffic:
    def __init__(self, sent, W, shapes, core, D):
        self.sent, self.W, self.shapes, self.core, self.D = sent, W, shapes, core, D
        self.half, self.pairs, self.slots, self.waiting = {}, {}, {}, {}

    def install(self, names, gathered):
        D = self.D
        for n, g in zip(names, gathered):
            full = _full_from_gathered(g, n)
            if n == "w_in":
                self.W["w_main"] = full[:, :7 * D]
                self.W["w_f"] = jnp.pad(full[:, 7 * D:7 * D + HEADS], ((0, 0), (0, LANES - HEADS)))
                self.W["w_gates"] = full[:, 7 * D + HEADS:]
            elif n in ("ffn1_w_in", "ffn2_w_in"):
                self.W[n] = _swiglu_interleave(full)
            else:
                self.W[n] = full

    def host(self, stage):
        if stage == "gather_mixer_chips":
            return _ex_ag_chips([self.sent[n] for n in GATHER_MIXER])
        if stage == "gather_late_chips":
            return _ex_ag_chips([self.sent[n] for n in GATHER_LATE])
        if stage in ("gather_mixer_sibling", "gather_late_sibling"):
            return _ex_ag_sibling(self.half[stage])
        if stage in SCATTER_BEHIND:
            return _ex_rs_chips([self.pairs[n] for n in SCATTER_BEHIND[stage]])
        if stage in self.waiting:
            return _ex_rs_sibling(self.waiting[stage][1])
        return None

    def landed(self, stage, arrays):
        if stage == "gather_mixer_chips":
            self.half["gather_mixer_sibling"] = arrays
        elif stage == "gather_late_chips":
            self.half["gather_late_sibling"] = arrays
        elif stage == "gather_mixer_sibling":
            self.install(GATHER_MIXER, arrays)
        elif stage == "gather_late_sibling":
            self.install(GATHER_LATE, arrays)
        elif stage in self.waiting:
            self._pair_sums(*self.waiting.pop(stage), arrays)
        else:
            self.slots.update(zip(SCATTER_BEHIND[stage], arrays))

    def _final_grad(self, n, gW):
        D = self.D
        if n == "w_in":
            g = gW["w_main"]
            return jnp.concatenate([g[:, :4 * D], g[:, 4 * D:5 * D] * (1.0 / math.sqrt(DH)), g[:, 5 * D:],
                                    gW["w_f"][:, :HEADS], gW["w_gates"]], axis=1)
        if n in ("ffn1_w_in", "ffn2_w_in"):
            return _swiglu_deinterleave(gW[n])
        return gW[n]

    def _pair_sums(self, names, blocks, got):
        for n, b, l in zip(names, blocks, got):
            self.pairs[n] = _pair_add(b, l, self.core, BF16, f"rs_pair_add_{n}")

    def grads_ready(self, names, gW, behind=None):
        blocks = [_blocks_from_full(self._final_grad(n, gW), n, self.shapes[n]) for n in names]
        if behind is None:
            self._pair_sums(names, blocks, _run_exchange(_ex_rs_sibling(blocks), f"rs_sibling_{names[0]}"))
        else:
            self.waiting[behind] = (names, blocks)

    def finish(self):
        rest = [n for n in BIG if n not in self.slots]
        got = _run_exchange(_ex_rs_chips([self.pairs[n] for n in rest]), "rs_chips_last")
        self.slots.update(zip(rest, got))
        return self.slots


def _train_step(a):
    c_idx = lax.axis_index("c")
    x, mem, target = a["x"][0], a["mem"][0], a["loss_target"][0]
    D = x.shape[1]
    shards = {n: a[n][0] for n in BIG}

    fox_scale = 1.0 / math.sqrt(DH)
    n_mine = shards["w_in"].shape[1]
    dev = 4 * lax.axis_index("x") + 2 * lax.axis_index("y") + c_idx
    cols = dev * n_mine + jnp.arange(n_mine)
    is_fox_q = (cols >= 4 * D) & (cols < 5 * D)
    sent = dict(shards, w_in=shards["w_in"] * jnp.where(is_fox_q, fox_scale, 1.0)[None, :])
    sent = {n: v.astype(BF16) for n, v in sent.items()}
    W = {}
    traffic = _Traffic(sent, W, {n: shards[n].shape for n in BIG}, c_idx.astype(jnp.int32).reshape(1), D)
    first = _run_exchange(_ex_ag_chips([sent[n] for n in GATHER_FIRST]), "ag_first_chips")
    traffic.install(GATHER_FIRST, _run_exchange(_ex_ag_sibling(first), "ag_first_sibling"))
    S = {n: a[n] for n in SMALL}

    sq, grad_x, gW, gS = _step_local(x, mem, target, W, S, traffic)
    slots = traffic.finish()
    big = {n: _adamw_reduce(slots[n], shards[n], a["m_" + n][0], a["v_" + n][0], f"adamw_{n}") for n in BIG}

    loss_row = jnp.pad(sq[:1, :1] * (0.5 / D), ((0, 0), (0, PACK_COLS - 1)))
    small_half = _run_exchange(_ex_ag_chips([_pack_small(gS, loss_row)]), "small_ag_chips")
    small_all = _run_exchange(_ex_ag_sibling(small_half), "small_ag_sibling")[0]
    small_slots = small_all.reshape(8, SMALL_ROWS, PACK_COLS)
    zero_row = jnp.zeros((1, PACK_COLS), F32)
    g_sm, d_sm, m_sm, v_sm = _adamw_reduce(
        small_slots, _pack_small({n: a[n] for n in SMALL}, zero_row),
        _pack_small({n: a["m_" + n] for n in SMALL}, zero_row),
        _pack_small({n: a["v_" + n] for n in SMALL}, zero_row), "adamw_small")

    def unpack(which, small):
        out = _unpack_small(small, {n: a[n] for n in SMALL})
        for n in BIG:
            out[n] = big[n][which][None]
        return [out[n] for n in WEIGHTS]

    loss = g_sm[SMALL_ROWS - 1, 0]
    return (loss, grad_x[None], *unpack(0, g_sm), *unpack(1, d_sm), *unpack(2, m_sm), *unpack(3, v_sm))


def kernel(x, mem, ffn1_pre_g, ffn1_w_in, ffn1_w_down, ffn1_post_g, mix_pre_g, w_in, hg_lb_logits, hg_norm_g, fox_f_bias, w_branch_a, w_branch_b, b_gate, w_out, mix_post_g, mem_pre_g, mem_kv_g, w_mq, w_mkv, w_mo, mem_post_g, ffn2_pre_g, ffn2_w_in, ffn2_w_down, ffn2_post_g, loss_target, m_ffn1_pre_g, m_ffn1_w_in, m_ffn1_w_down, m_ffn1_post_g, m_mix_pre_g, m_w_in, m_hg_lb_logits, m_hg_norm_g, m_fox_f_bias, m_w_branch_a, m_w_branch_b, m_b_gate, m_w_out, m_mix_post_g, m_mem_pre_g, m_mem_kv_g, m_w_mq, m_w_mkv, m_w_mo, m_mem_post_g, m_ffn2_pre_g, m_ffn2_w_in, m_ffn2_w_down, m_ffn2_post_g, v_ffn1_pre_g, v_ffn1_w_in, v_ffn1_w_down, v_ffn1_post_g, v_mix_pre_g, v_w_in, v_hg_lb_logits, v_hg_norm_g, v_fox_f_bias, v_w_branch_a, v_w_branch_b, v_b_gate, v_w_out, v_mix_post_g, v_mem_pre_g, v_mem_kv_g, v_w_mq, v_w_mkv, v_w_mo, v_mem_post_g, v_ffn2_pre_g, v_ffn2_w_in, v_ffn2_w_down, v_ffn2_post_g):
    return _train_step(dict(locals()))
```

```python
import functools
import math

import jax
import jax.numpy as jnp
from jax import lax
from jax.experimental import pallas as pl
from jax.experimental.pallas import tpu as pltpu

F32 = jnp.float32
BF16 = jnp.bfloat16
MESH = pl.DeviceIdType.MESH

EPS = 1e-6
HEADS = 8
DH = 128
MEM_HEADS = 4
CHUNK = 128
HALF = CHUNK // 2
SWIGLU_TILE = 256
LANES = 128
PACK_COLS = 1024
ROW_TILE = 1024
SEQ_BLOCK = 2048
CUMSUM_BLOCK = 512
XATTN_TILE = 2048
ATTN_TILE = 2048
ATTN_ROWS = 256
EXP_CLAMP = 80.0
NEG_BIG = -1e30

ADAM_LR, ADAM_B1, ADAM_B2, ADAM_EPS, ADAM_WD, ADAM_STEP = 0.001, 0.9, 0.999, 1e-08, 0.01, 10

VMEM_LIMIT = 48 * 1024 * 1024

_DN = {
    "nn": (((1,), (0,)), ((), ())),
    "nt": (((1,), (1,)), ((), ())),
    "tn": (((0,), (0,)), ((), ())),
}

BIG = ["ffn1_w_in", "ffn1_w_down", "w_in", "w_branch_a", "w_branch_b", "w_out", "w_mq", "w_mkv", "w_mo",
       "ffn2_w_in", "ffn2_w_down"]
COL_SHARDED = {"ffn1_w_in", "w_in", "w_mkv", "ffn2_w_in"}
SMALL = ["ffn1_pre_g", "ffn1_post_g", "mix_pre_g", "hg_lb_logits", "hg_norm_g", "fox_f_bias", "b_gate",
         "mix_post_g", "mem_pre_g", "mem_kv_g", "mem_post_g", "ffn2_pre_g", "ffn2_post_g"]
WEIGHTS = ["ffn1_pre_g", "ffn1_w_in", "ffn1_w_down", "ffn1_post_g", "mix_pre_g", "w_in", "hg_lb_logits",
           "hg_norm_g", "fox_f_bias", "w_branch_a", "w_branch_b", "b_gate", "w_out", "mix_post_g", "mem_pre_g",
           "mem_kv_g", "w_mq", "w_mkv", "w_mo", "mem_post_g", "ffn2_pre_g", "ffn2_w_in", "ffn2_w_down",
           "ffn2_post_g"]


def _dot(a, b, mode="nn"):
    return lax.dot_general(a, b, _DN[mode], preferred_element_type=F32)


def _sig(x):
    return 1.0 / (1.0 + jnp.exp(-x))


def _sig_approx(x):
    return pl.reciprocal(1.0 + jnp.exp(-x), approx=True)


def _params(*dims):
    return pltpu.CompilerParams(dimension_semantics=dims if dims else None, vmem_limit_bytes=VMEM_LIMIT)


def _tile(dim, pref):
    if dim <= pref:
        return dim
    t = (pref // LANES) * LANES
    while t >= LANES:
        if dim % t == 0:
            return t
        t -= LANES
    raise ValueError(f"no tile for {dim}")


def _colsum(x):
    return jnp.sum(x, axis=0, keepdims=True)


def _rowsum(x):
    return jnp.sum(x, axis=1, keepdims=True)


def _iota(shape, axis):
    return lax.broadcasted_iota(jnp.int32, shape, axis)


def _pick_row(x, r):
    return _colsum(jnp.where(_iota(x.shape, 0) == r, x, 0.0))


def _tri_dot(tri, x):
    hi = x.astype(BF16)
    r1 = x - hi.astype(F32)
    mid = r1.astype(BF16)
    lo = (r1 - mid.astype(F32)).astype(BF16)
    return _dot(tri, hi) + _dot(tri, mid) + _dot(tri, lo)


_MM_TILES = {"nn": (2048, 512, 1024), "nt": (512, 1024, 4096), "tn": (1024, 1024, 2048)}


def _host_call(body, name, grid, in_specs, out_specs, out_shape, scratch_shapes, dims, args, hosted=None):
    if hosted is None:
        results = pl.pallas_call(body, name=name, grid=grid, in_specs=in_specs, out_specs=out_specs, out_shape=out_shape,
                                 scratch_shapes=scratch_shapes, compiler_params=_params(*dims))(*args)
        return list(results), []
    n_in, n_out, n_sc = len(in_specs), len(out_specs), len(scratch_shapes)
    h_in, h_out = len(hosted.inputs), len(hosted.out_shapes)

    def wrapped(*refs):
        cut = [n_in, h_in, n_out, h_out, n_sc]
        at = [sum(cut[:i]) for i in range(len(cut) + 1)]
        ins, hin, outs, hout, scr = (refs[at[i]:at[i + 1]] for i in range(len(cut)))
        hsems = refs[at[-1]:]
        ids = [pl.program_id(d) for d in range(len(grid))]
        first = functools.reduce(jnp.logical_and, [i == 0 for i in ids])
        last = functools.reduce(jnp.logical_and, [i == g - 1 for i, g in zip(ids, grid)])

        @pl.when(first)
        def _():
            hosted.start(hin, hout, hsems)

        body(*ins, *outs, *scr)

        @pl.when(last)
        def _():
            hosted.wait(hin, hout, hsems)

    results = pl.pallas_call(
        wrapped, name=name, grid=grid, in_specs=list(in_specs) + [_HBM] * h_in,
        out_specs=list(out_specs) + [_HBM] * h_out, out_shape=list(out_shape) + list(hosted.out_shapes),
        scratch_shapes=list(scratch_shapes) + list(hosted.scratch), input_output_aliases=hosted.aliases(n_in, n_out),
        compiler_params=_params(*dims))(*args, *hosted.inputs)
    return list(results[:n_out]), list(results[n_out:])


def _mm(a, b, mode, out_dtype, name, add=None, tm=None, tn=None, tk=None, hosted=None):
    tm, tn, tk = (given or pref for given, pref in zip((tm, tn, tk), _MM_TILES[mode]))
    if mode == "nn":
        (M, K), (K2, N) = a.shape, b.shape
    elif mode == "nt":
        (M, K), (N, K2) = a.shape, b.shape
    else:
        (K, M), (K2, N) = a.shape, b.shape
    assert K == K2, (name, a.shape, b.shape)
    tm, tn, tk = _tile(M, tm), _tile(N, tn), _tile(K, tk)
    nk = K // tk
    if mode == "tn":
        a_spec = pl.BlockSpec((tk, tm), lambda i, j, k: (k, i))
    else:
        a_spec = pl.BlockSpec((tm, tk), lambda i, j, k: (i, k))
    if mode == "nt":
        b_spec = pl.BlockSpec((tn, tk), lambda i, j, k: (j, k))
    else:
        b_spec = pl.BlockSpec((tk, tn), lambda i, j, k: (k, j))
    o_spec = pl.BlockSpec((tm, tn), lambda i, j, k: (i, j))
    has_add = add is not None

    def body(*refs):
        a_ref, b_ref = refs[0], refs[1]
        c_ref = refs[2] if has_add else None
        o_ref = refs[3] if has_add else refs[2]
        part = _dot(a_ref[...], b_ref[...], mode)
        if nk == 1:
            if has_add:
                part = part + c_ref[...]
            o_ref[...] = part.astype(o_ref.dtype)
            return
        acc_ref = refs[-1]
        k = pl.program_id(2)

        @pl.when(k == 0)
        def _():
            acc_ref[...] = part + c_ref[...] if has_add else part

        @pl.when(k > 0)
        def _():
            acc_ref[...] += part

        @pl.when(k == nk - 1)
        def _():
            o_ref[...] = acc_ref[...].astype(o_ref.dtype)

    in_specs = [a_spec, b_spec] + ([o_spec] if has_add else [])
    args = (a, b) + ((add,) if has_add else ())
    (out,), landed = _host_call(
        body, name, (M // tm, N // tn, nk), in_specs, [o_spec], [jax.ShapeDtypeStruct((M, N), out_dtype)],
        [pltpu.VMEM((tm, tn), F32)] if nk > 1 else [], ("parallel", "parallel", "arbitrary"), args, hosted)
    return out if hosted is None else (out, landed)


def _rms_fwd(x, g, name, out_dtype=BF16):
    T, D = x.shape
    tr = _tile(T, ROW_TILE)

    def body(x_ref, g_ref, o_ref):
        xv = x_ref[...]
        r = lax.rsqrt(jnp.mean(xv * xv, axis=-1, keepdims=True) + EPS)
        o_ref[...] = (xv * r * g_ref[...]).astype(o_ref.dtype)

    return pl.pallas_call(
        body, name=name, grid=(T // tr,),
        in_specs=[pl.BlockSpec((tr, D), lambda i: (i, 0)), pl.BlockSpec((1, D), lambda i: (0, 0))],
        out_specs=pl.BlockSpec((tr, D), lambda i: (i, 0)),
        out_shape=jax.ShapeDtypeStruct((T, D), out_dtype), compiler_params=_params("parallel"),
    )(x, g)


def _resid_rms(x, z, g, scale, name):
    T, D = x.shape
    tr = _tile(T, ROW_TILE)

    def body(x_ref, z_ref, g_ref, o_ref):
        zv = z_ref[...]
        r = lax.rsqrt(jnp.mean(zv * zv, axis=-1, keepdims=True) + EPS)
        o_ref[...] = x_ref[...] + scale * (zv * r * g_ref[...])

    row = pl.BlockSpec((tr, D), lambda i: (i, 0))
    return pl.pallas_call(
        body, name=name, grid=(T // tr,), in_specs=[row, row, pl.BlockSpec((1, D), lambda i: (0, 0))],
        out_specs=row, out_shape=jax.ShapeDtypeStruct((T, D), F32), compiler_params=_params("parallel"),
    )(x, z, g)


def _mm_post_norm(a, w, x, g, scale, name):
    T, K = a.shape
    D = w.shape[1]
    tm = _tile(T, 1024)

    def body(a_ref, w_ref, x_ref, g_ref, z_ref, o_ref):
        z = _dot(a_ref[...], w_ref[...])
        z_ref[...] = z
        r = lax.rsqrt(jnp.mean(z * z, axis=-1, keepdims=True) + EPS)
        o_ref[...] = x_ref[...] + scale * (z * r * g_ref[...])

    row = pl.BlockSpec((tm, D), lambda i: (i, 0))
    return pl.pallas_call(
        body, name=name, grid=(T // tm,),
        in_specs=[pl.BlockSpec((tm, K), lambda i: (i, 0)), pl.BlockSpec((K, D), lambda i: (0, 0)), row,
                  pl.BlockSpec((1, D), lambda i: (0, 0))],
        out_specs=[row, row], out_shape=[jax.ShapeDtypeStruct((T, D), F32)] * 2, compiler_params=_params("parallel"),
    )(a, w, x, g)


def _final_loss(x, z, g, scale, target, name):
    T, D = x.shape
    tr = _tile(T, ROW_TILE)

    def body(x_ref, z_ref, g_ref, t_ref, dx_ref, acc_ref):
        @pl.when(pl.program_id(0) == 0)
        def _():
            acc_ref[...] = jnp.zeros_like(acc_ref)

        zv = z_ref[...]
        r = lax.rsqrt(jnp.mean(zv * zv, axis=-1, keepdims=True) + EPS)
        e = x_ref[...] + scale * (zv * r * g_ref[...]) - t_ref[...]
        dx_ref[...] = e * (1.0 / D)
        acc_ref[...] += _colsum(_rowsum(e * e))

    row = pl.BlockSpec((tr, D), lambda i: (i, 0))
    return pl.pallas_call(
        body, name=name, grid=(T // tr,), in_specs=[row, row, pl.BlockSpec((1, D), lambda i: (0, 0)), row],
        out_specs=[row, pl.BlockSpec((8, LANES), lambda i: (0, 0))],
        out_shape=[jax.ShapeDtypeStruct((T, D), F32), jax.ShapeDtypeStruct((8, LANES), F32)],
        compiler_params=_params("arbitrary"),
    )(x, z, g, target)


def _rms_bwd(xin, g, dy, scale, name, out_dtype, resid=None):
    T, D = xin.shape
    tr = _tile(T, ROW_TILE)
    has_resid = resid is not None

    def body(*refs):
        x_ref, g_ref, dy_ref = refs[:3]
        r_ref = refs[3] if has_resid else None
        dx_ref, dg_ref = refs[-2], refs[-1]

        @pl.when(pl.program_id(0) == 0)
        def _():
            dg_ref[...] = jnp.zeros_like(dg_ref)

        xv = x_ref[...]
        r = lax.rsqrt(jnp.mean(xv * xv, axis=-1, keepdims=True) + EPS)
        xh = xv * r
        dyv = dy_ref[...].astype(F32) * scale
        dxh = dyv * g_ref[...]
        dx = r * (dxh - xh * jnp.mean(dxh * xh, axis=-1, keepdims=True))
        if has_resid:
            dx = dx + r_ref[...]
        dx_ref[...] = dx.astype(dx_ref.dtype)
        dg_ref[...] += _colsum(dyv * xh)

    row = pl.BlockSpec((tr, D), lambda i: (i, 0))
    vec = pl.BlockSpec((1, D), lambda i: (0, 0))
    return pl.pallas_call(
        body, name=name, grid=(T // tr,), in_specs=[row, vec, row] + ([row] if has_resid else []),
        out_specs=[row, vec],
        out_shape=[jax.ShapeDtypeStruct((T, D), out_dtype), jax.ShapeDtypeStruct((1, D), F32)],
        compiler_params=_params("arbitrary"),
    )(*((xin, g, dy) + ((resid,) if has_resid else ())))


def _mm_swiglu(h, w_in, name, hosted=None):
    T, K = h.shape
    F2 = w_in.shape[1]
    tf = SWIGLU_TILE
    tm = _tile(T, _MM_TILES["nn"][0])

    def body(h_ref, w_ref, u_ref, a_ref):
        u = _dot(h_ref[...], w_ref[...])
        u_ref[...] = u.astype(u_ref.dtype)
        gate, up = u[:, :tf], u[:, tf:]
        a_ref[...] = (gate * _sig_approx(gate) * up).astype(a_ref.dtype)

    (u, a), landed = _host_call(
        body, name, (T // tm, F2 // (2 * tf)),
        [pl.BlockSpec((tm, K), lambda i, j: (i, 0)), pl.BlockSpec((K, 2 * tf), lambda i, j: (0, j))],
        [pl.BlockSpec((tm, 2 * tf), lambda i, j: (i, j)), pl.BlockSpec((tm, tf), lambda i, j: (i, j))],
        [jax.ShapeDtypeStruct((T, F2), BF16), jax.ShapeDtypeStruct((T, F2 // 2), BF16)], [],
        ("parallel", "parallel"), (h, w_in), hosted)
    return u, a, landed


def _mm_swiglu_bwd(dz, w_down, u, name):
    T, D = dz.shape
    F = w_down.shape[0]
    tf = SWIGLU_TILE
    tm = _tile(T, _MM_TILES["nn"][0])

    def body(dz_ref, w_ref, u_ref, o_ref):
        d = _dot(dz_ref[...], w_ref[...], "nt")
        gate = u_ref[:, :tf].astype(F32)
        up = u_ref[:, tf:].astype(F32)
        s = _sig_approx(gate)
        o_ref[:, :tf] = (d * up * (s * (1.0 + gate * (1.0 - s)))).astype(o_ref.dtype)
        o_ref[:, tf:] = (d * gate * s).astype(o_ref.dtype)

    return pl.pallas_call(
        body, name=name, grid=(T // tm, F // tf),
        in_specs=[pl.BlockSpec((tm, D), lambda i, j: (i, 0)), pl.BlockSpec((tf, D), lambda i, j: (j, 0)),
                  pl.BlockSpec((tm, 2 * tf), lambda i, j: (i, j))],
        out_specs=pl.BlockSpec((tm, 2 * tf), lambda i, j: (i, j)),
        out_shape=jax.ShapeDtypeStruct((T, 2 * F), BF16), compiler_params=_params("parallel", "parallel"),
    )(dz, w_down, u)


def _hgout_fwd(o_a, pm, g, name):
    T, D = o_a.shape
    tr = _tile(T, ROW_TILE)

    def body(o_ref, ga_ref, g_ref, out_ref):
        ov = o_ref[...]
        r = lax.rsqrt(jnp.mean(ov * ov, axis=-1, keepdims=True) + EPS)
        ga = ga_ref[...].astype(F32)
        out_ref[...] = (ov * r * g_ref[...] * (ga * _sig(ga))).astype(out_ref.dtype)

    row = pl.BlockSpec((tr, D), lambda i: (i, 0))
    return pl.pallas_call(
        body, name=name, grid=(T // tr,),
        in_specs=[row, pl.BlockSpec((tr, D), lambda i: (i, 3)), pl.BlockSpec((1, D), lambda i: (0, 0))],
        out_specs=row, out_shape=jax.ShapeDtypeStruct((T, D), BF16), compiler_params=_params("parallel"),
    )(o_a, pm, g)


def _hgout_bwd(o_a, pm, g, d_out, name):
    T, D = o_a.shape
    tr = _tile(T, ROW_TILE)

    def body(o_ref, ga_ref, g_ref, d_ref, do_ref, dga_ref, dg_ref):
        @pl.when(pl.program_id(0) == 0)
        def _():
            dg_ref[...] = jnp.zeros_like(dg_ref)

        ov = o_ref[...]
        r = lax.rsqrt(jnp.mean(ov * ov, axis=-1, keepdims=True) + EPS)
        oh = ov * r
        ga = ga_ref[...].astype(F32)
        s = _sig(ga)
        d = d_ref[...].astype(F32)
        dn = d * (ga * s)
        dga_ref[...] = (d * (oh * g_ref[...]) * (s * (1.0 + ga * (1.0 - s)))).astype(dga_ref.dtype)
        dxh = dn * g_ref[...]
        do_ref[...] = (r * (dxh - oh * jnp.mean(dxh * oh, axis=-1, keepdims=True))).astype(do_ref.dtype)
        dg_ref[...] += _colsum(dn * oh)

    row = pl.BlockSpec((tr, D), lambda i: (i, 0))
    vec = pl.BlockSpec((1, D), lambda i: (0, 0))
    return pl.pallas_call(
        body, name=name, grid=(T // tr,), in_specs=[row, pl.BlockSpec((tr, D), lambda i: (i, 3)), vec, row],
        out_specs=[row, row, vec],
        out_shape=[jax.ShapeDtypeStruct((T, D), BF16), jax.ShapeDtypeStruct((T, D), BF16),
                   jax.ShapeDtypeStruct((1, D), F32)],
        compiler_params=_params("arbitrary"),
    )(o_a, pm, g, d_out)


def _merge_fwd(ya, yb, pg, bg, name):
    T, D = ya.shape
    tr = _tile(T, 256)

    def body(ya_ref, yb_ref, pg_ref, bg_ref, o_ref):
        g0 = _sig(pg_ref[:, :D].astype(F32) + bg_ref[:, :D])
        g1 = _sig(pg_ref[:, D:].astype(F32) + bg_ref[:, D:])
        o_ref[...] = (g0 * ya_ref[...].astype(F32) + g1 * yb_ref[...].astype(F32)).astype(o_ref.dtype)

    row = pl.BlockSpec((tr, D), lambda i: (i, 0))
    return pl.pallas_call(
        body, name=name, grid=(T // tr,),
        in_specs=[row, row, pl.BlockSpec((tr, 2 * D), lambda i: (i, 0)), pl.BlockSpec((1, 2 * D), lambda i: (0, 0))],
        out_specs=row, out_shape=jax.ShapeDtypeStruct((T, D), BF16), compiler_params=_params("parallel"),
    )(ya, yb, pg, bg)


def _merge_bwd(dy, ya, yb, pg, bg, name):
    T, D = ya.shape
    tr = _tile(T, 256)

    def body(dy_ref, ya_ref, yb_ref, pg_ref, bg_ref, dya_ref, dyb_ref, dpg_ref, dbg_ref):
        @pl.when(pl.program_id(0) == 0)
        def _():
            dbg_ref[...] = jnp.zeros_like(dbg_ref)

        d = dy_ref[...].astype(F32)
        g0 = _sig(pg_ref[:, :D].astype(F32) + bg_ref[:, :D])
        g1 = _sig(pg_ref[:, D:].astype(F32) + bg_ref[:, D:])
        dya_ref[...] = (d * g0).astype(dya_ref.dtype)
        dyb_ref[...] = (d * g1).astype(dyb_ref.dtype)
        dg0 = d * ya_ref[...].astype(F32) * (g0 * (1.0 - g0))
        dg1 = d * yb_ref[...].astype(F32) * (g1 * (1.0 - g1))
        dpg_ref[:, :D] = dg0.astype(dpg_ref.dtype)
        dpg_ref[:, D:] = dg1.astype(dpg_ref.dtype)
        dbg_ref[:, :D] += _colsum(dg0)
        dbg_ref[:, D:] += _colsum(dg1)

    row = pl.BlockSpec((tr, D), lambda i: (i, 0))
    wide = pl.BlockSpec((tr, 2 * D), lambda i: (i, 0))
    wvec = pl.BlockSpec((1, 2 * D), lambda i: (0, 0))
    return pl.pallas_call(
        body, name=name, grid=(T // tr,), in_specs=[row, row, row, wide, wvec],
        out_specs=[row, row, wide, wvec],
        out_shape=[jax.ShapeDtypeStruct((T, D), BF16), jax.ShapeDtypeStruct((T, D), BF16),
                   jax.ShapeDtypeStruct((T, 2 * D), BF16), jax.ShapeDtypeStruct((1, 2 * D), F32)],
        compiler_params=_params("arbitrary"),
    )(dy, ya, yb, pg, bg)


def _hgrn_chunk_terms(q, fl, lb, tri):
    shape = q.shape
    row = _iota(shape, 0)
    sg = _sig(fl)
    f = lb + (1.0 - lb) * sg
    k = 1.0 - f
    b = _tri_dot(tri, jnp.log(f))
    ref1 = jnp.where(row < HALF, _pick_row(b, HALF // 2), _pick_row(b, HALF + HALF // 2))
    b_half = _pick_row(b, HALF - 1)
    b_last = _pick_row(b, CHUNK - 1)
    sq = _sig(q)
    qs = q * sq
    e_q1 = jnp.exp(jnp.minimum(b - ref1, EXP_CLAMP))
    e_k1 = jnp.exp(jnp.minimum(ref1 - b, EXP_CLAMP))
    e_q2 = jnp.exp(jnp.minimum(b - b_half, 0.0))
    e_k2 = jnp.exp(jnp.minimum(b_half - b, 0.0))
    e_b = jnp.exp(b)
    e_kd = jnp.exp(b_last - b)
    return dict(sg=sg, f=f, k=k, sq=sq, qs=qs, e_q1=e_q1, e_k1=e_k1, e_q2=e_q2, e_k2=e_k2, e_b=e_b, e_kd=e_kd,
                e_last=jnp.exp(b_last))


def _hgrn_masks():
    r = _iota((CHUNK, CHUNK), 0)
    c = _iota((CHUNK, CHUNK), 1)
    causal = r >= c
    same = (r < HALF) == (c < HALF)
    return causal, causal & same, (r >= HALF) & (c < HALF)


def _softmax_lb(lbl_ref):
    l0, l1 = lbl_ref[0, 0], lbl_ref[1, 0]
    mx = jnp.maximum(l0, l1)
    e0, e1 = jnp.exp(l0 - mx), jnp.exp(l1 - mx)
    return e0 / (e0 + e1)


def _hgrn_fwd(pm, lbl, name):
    T = pm.shape[0]
    tb = _tile(T, SEQ_BLOCK)
    nc = tb // CHUNK

    def body(q_ref, f_ref, i_ref, lbl_ref, o_ref, st_ref, s_sc):
        @pl.when(pl.program_id(1) == 0)
        def _():
            s_sc[...] = jnp.zeros_like(s_sc)

        lb = _softmax_lb(lbl_ref)
        causal, m1, m2 = _hgrn_masks()
        tri = jnp.where(causal, 1.0, 0.0).astype(BF16)
        parts = []
        for ci in range(nc):
            sl = pl.ds(ci * CHUNK, CHUNK)
            t = _hgrn_chunk_terms(q_ref[sl, :].astype(F32), f_ref[sl, :].astype(F32), lb, tri)
            iv = i_ref[sl, :]
            a1 = _dot((t["qs"] * t["e_q1"]).astype(BF16), (t["k"] * t["e_k1"]).astype(BF16), "nt")
            a2 = _dot((t["qs"] * t["e_q2"]).astype(BF16), (t["k"] * t["e_k2"]).astype(BF16), "nt")
            a = jnp.where(m1, a1, 0.0) + jnp.where(m2, a2, 0.0)
            parts.append((_dot(a.astype(BF16), iv), (t["qs"] * t["e_b"]).astype(BF16),
                          _dot(iv, (t["k"] * t["e_kd"]).astype(BF16), "tn"), t["e_last"]))
        st = s_sc[...]
        for ci, (o_intra, qi, grow, e_last) in enumerate(parts):
            st_ref[0, ci] = st
            o_ref[pl.ds(ci * CHUNK, CHUNK), :] = o_intra + _dot(qi, st.astype(BF16), "nt")
            st = e_last * st + grow
        s_sc[...] = st

    blk = lambda off: pl.BlockSpec((tb, DH), lambda h, b: (b, off + h))
    return pl.pallas_call(
        body, name=name, grid=(HEADS, T // tb),
        in_specs=[blk(0), blk(HEADS), blk(2 * HEADS), pl.BlockSpec((2, 1, 1, DH), lambda h, b: (0, h, 0, 0))],
        out_specs=[pl.BlockSpec((tb, DH), lambda h, b: (b, h)),
                   pl.BlockSpec((1, nc, DH, DH), lambda h, b: (h, b, 0, 0))],
        out_shape=[jax.ShapeDtypeStruct((T, HEADS * DH), F32),
                   jax.ShapeDtypeStruct((HEADS, T // CHUNK, DH, DH), F32)],
        scratch_shapes=[pltpu.VMEM((DH, DH), F32)],
        compiler_params=_params("parallel", "arbitrary"),
    )(pm, pm, pm, lbl)


def _hgrn_bwd(pm, lbl, states, do, name):
    T = pm.shape[0]
    tb = _tile(T, SEQ_BLOCK)
    nc = tb // CHUNK
    nb = T // tb

    def body(q_ref, f_ref, i_ref, lbl_ref, st_ref, do_ref, dq_ref, df_ref, di_ref, dl_ref, ds_sc, dlb_sc):
        @pl.when(pl.program_id(1) == 0)
        def _():
            ds_sc[...] = jnp.zeros_like(ds_sc)
            dlb_sc[...] = jnp.zeros_like(dlb_sc)

        lb = _softmax_lb(lbl_ref)
        causal, m1, m2 = _hgrn_masks()
        tri = jnp.where(causal, 1.0, 0.0).astype(BF16)
        tri_rev = jnp.where(_iota((CHUNK, CHUNK), 0) <= _iota((CHUNK, CHUNK), 1), 1.0, 0.0).astype(BF16)
        last_row = _iota((CHUNK, DH), 0) == CHUNK - 1
        dsn = ds_sc[...]
        dlb = jnp.zeros((1, DH), F32)
        for ci in reversed(range(nc)):
            sl = pl.ds(ci * CHUNK, CHUNK)
            q = q_ref[sl, :].astype(F32)
            t = _hgrn_chunk_terms(q, f_ref[sl, :].astype(F32), lb, tri)
            iv = i_ref[sl, :]
            dov = do_ref[sl, :]
            qe1, ke1 = t["qs"] * t["e_q1"], t["k"] * t["e_k1"]
            qe2, ke2 = t["qs"] * t["e_q2"], t["k"] * t["e_k2"]
            qi, kd = t["qs"] * t["e_b"], t["k"] * t["e_kd"]
            qe1b, ke1b, qe2b, ke2b = qe1.astype(BF16), ke1.astype(BF16), qe2.astype(BF16), ke2.astype(BF16)
            a = jnp.where(m1, _dot(qe1b, ke1b, "nt"), 0.0) + jnp.where(m2, _dot(qe2b, ke2b, "nt"), 0.0)
            st = st_ref[0, ci]
            dsnb = dsn.astype(BF16)
            da = _dot(dov, iv, "nt")
            da1 = jnp.where(m1, da, 0.0).astype(BF16)
            da2 = jnp.where(m2, da, 0.0).astype(BF16)
            di_ref[sl, :] = (_dot(a.astype(BF16), dov, "tn") + _dot(kd.astype(BF16), dsnb, "nt")).astype(di_ref.dtype)
            dqe1, dke1 = _dot(da1, ke1b), _dot(da1, qe1b, "tn")
            dqe2, dke2 = _dot(da2, ke2b), _dot(da2, qe2b, "tn")
            dqi = _dot(dov, st.astype(BF16))
            dkd = _dot(iv, dsnb)
            ds_before = t["e_last"] * dsn + _dot(dov, qi.astype(BF16), "tn")
            dqs =dqe1 * t["e_q1"] + dqe2 * t["e_q2"] + dqi * t["e_b"]
            dk = dke1 * t["e_k1"] + dke2 * t["e_k2"] + dkd * t["e_kd"]
            qib, kdb = qi.astype(BF16).astype(F32), kd.astype(BF16).astype(F32)
            db = (dqe1 * qe1b.astype(F32) - dke1 * ke1b.astype(F32) + dqe2 * qe2b.astype(F32)
                  - dke2 * ke2b.astype(F32) + dqi * qib - dkd * kdb)
            extra = _colsum(dkd * kdb) + t["e_last"] * _colsum(dsn * st)
            db = db + jnp.where(last_row, extra, 0.0)
            dlf = _tri_dot(tri_rev, db)
            dfv = dlf / t["f"] - dk
            sg = t["sg"]
            df_ref[sl, :] = (dfv * (1.0 - lb) * sg * (1.0 - sg)).astype(df_ref.dtype)
            dlb = dlb + _colsum(dfv * (1.0 - sg))
            sq = t["sq"]
            dq_ref[sl, :] = (dqs * (sq * (1.0 + q * (1.0 - sq)))).astype(dq_ref.dtype)
            dsn = ds_before
        ds_sc[...] = dsn
        dlb_sc[...] += dlb

        @pl.when(pl.program_id(1) == nb - 1)
        def _():
            dl0 = dlb_sc[...] * lb * (1.0 - lb)
            dl_ref[0, 0] = dl0
            dl_ref[1, 0] = -dl0

    blk = lambda off: pl.BlockSpec((tb, DH), lambda h, b: (nb - 1 - b, off + h))
    lspec = pl.BlockSpec((2, 1, 1, DH), lambda h, b: (0, h, 0, 0))
    out_blk = pl.BlockSpec((tb, DH), lambda h, b: (nb - 1 - b, h))
    D = HEADS * DH
    return pl.pallas_call(
        body, name=name, grid=(HEADS, nb),
        in_specs=[blk(0), blk(HEADS), blk(2 * HEADS), lspec,
                  pl.BlockSpec((1, nc, DH, DH), lambda h, b: (h, nb - 1 - b, 0, 0)), out_blk],
        out_specs=[out_blk, out_blk, out_blk, lspec],
        out_shape=[jax.ShapeDtypeStruct((T, D), BF16)] * 3 + [jax.ShapeDtypeStruct((2, HEADS, 1, DH), F32)],
        scratch_shapes=[pltpu.VMEM((DH, DH), F32), pltpu.VMEM((1, DH), F32)],
        compiler_params=_params("parallel", "arbitrary"),
    )(pm, pm, pm, lbl, states, do)


def _log_sigmoid(x):
    return jnp.minimum(x, 0.0) - jnp.log(1.0 + jnp.exp(-jnp.abs(x)))


def _fox_cumsum(pf, bias, name):
    T = pf.shape[0]
    tb = _tile(T, CUMSUM_BLOCK)

    def body(x_ref, b_ref, c_ref, carry):
        @pl.when(pl.program_id(0) == 0)
        def _():
            carry[...] = jnp.zeros_like(carry)

        tri = jnp.where(_iota((tb, tb), 0) >= _iota((tb, tb), 1), 1.0, 0.0).astype(BF16)
        c = _tri_dot(tri, _log_sigmoid(x_ref[...] + b_ref[...])) + carry[...]
        c_ref[...] = c
        carry[...] = _pick_row(c, tb - 1)

    row = pl.BlockSpec((tb, LANES), lambda i: (i, 0))
    return pl.pallas_call(
        body, name=name, grid=(T // tb,), in_specs=[row, pl.BlockSpec((1, LANES), lambda i: (0, 0))],
        out_specs=row, out_shape=jax.ShapeDtypeStruct((T, LANES), F32),
        scratch_shapes=[pltpu.VMEM((1, LANES), F32)], compiler_params=_params("arbitrary"),
    )(pf, bias)


def _fox_dcum(dc, pf, bias, name):
    T = pf.shape[0]
    tb = _tile(T, CUMSUM_BLOCK)
    nb = T // tb

    def body(dc_ref, x_ref, b_ref, dx_ref, db_ref, carry):
        @pl.when(pl.program_id(0) == 0)
        def _():
            carry[...] = jnp.zeros_like(carry)
            db_ref[...] = jnp.zeros_like(db_ref)

        tri_rev = jnp.where(_iota((tb, tb), 0) <= _iota((tb, tb), 1), 1.0, 0.0).astype(BF16)
        dls = _tri_dot(tri_rev, dc_ref[...]) + carry[...]
        carry[...] = _pick_row(dls, 0)
        dx = dls * (1.0 - _sig(x_ref[...] + b_ref[...]))
        dx_ref[...] = dx
        db_ref[...] += _colsum(dx)

    row = pl.BlockSpec((tb, LANES), lambda i: (nb - 1 - i, 0))
    vec = pl.BlockSpec((1, LANES), lambda i: (0, 0))
    return pl.pallas_call(
        body, name=name, grid=(nb,), in_specs=[row, row, vec], out_specs=[row, vec],
        out_shape=[jax.ShapeDtypeStruct((T, LANES), F32), jax.ShapeDtypeStruct((1, LANES), F32)],
        scratch_shapes=[pltpu.VMEM((1, LANES), F32)], compiler_params=_params("arbitrary"),
    )(dc, pf, bias)


_Q_OFF, _K_OFF, _V_OFF = 4 * HEADS, 5 * HEADS, 6 * HEADS


def _causal_pairs(nq, by_key):
    if by_key:
        pairs = [(i, j) for j in range(nq) for i in range(j, nq)]
    else:
        pairs = [(i, j) for i in range(nq) for j in range(i + 1)]
    return jnp.asarray([p[0] for p in pairs], jnp.int32), jnp.asarray([p[1] for p in pairs], jnp.int32)


def _fox_logits(q, k, ck, row0, masked):
    s = _dot(q, k, "nt") - ck
    if masked:
        s = jnp.where(_iota(s.shape, 0) + row0 >= _iota(s.shape, 1), s, NEG_BIG)
    return s


def _ones_column(rows):
    return jnp.where(_iota((rows, DH), 1) == 0, 1.0, 0.0).astype(BF16)


def _fox_fwd(pm, c_col, c_row, name):
    T = pm.shape[0]
    tq = _tile(T, ATTN_TILE)
    nq = T // tq
    rg = min(ATTN_ROWS, tq)
    qi_tab, kj_tab = _causal_pairs(nq, by_key=False)

    def body(qi_ref, kj_ref, q_ref, k_ref, v_ref, cq_ref, ck_ref, o_ref, lse_ref, m_sc, acc_sc):
        t = pl.program_id(1)
        i, j = qi_ref[t], kj_ref[t]

        @pl.when(j == 0)
        def _():
            m_sc[...] = jnp.full_like(m_sc, NEG_BIG)
            acc_sc[...] = jnp.zeros_like(acc_sc)

        def step(diag):
            m_all, acc_all = m_sc[...], acc_sc[...]
            ones = _ones_column(tq)
            ms, accs = [], []
            for r in range(tq // rg):
                rows = slice(r * rg, (r + 1) * rg)
                w = (r + 1) * rg if diag else tq
                cq = cq_ref[0, rows, :]
                s = _fox_logits(q_ref[rows, :], k_ref[:w, :], ck_ref[0, :, :w], r * rg, diag)
                m_old = m_all[rows, :]
                m_new = jnp.maximum(m_old, jnp.max(s, axis=1, keepdims=True) + cq)
                alpha = jnp.exp(m_old - m_new)
                p = jnp.exp(s - (m_new - cq)).astype(BF16)
                v_one = jnp.concatenate([v_ref[:w, :], ones[:w, :]], axis=1)
                ms.append(m_new)
                accs.append(alpha * acc_all[rows, :] + _dot(p, v_one))
            m_sc[...] = jnp.concatenate(ms, axis=0)
            acc_sc[...] = jnp.concatenate(accs, axis=0)

        @pl.when(j < i)
        def _():
            step(False)

        @pl.when(j == i)
        def _():
            step(True)
            acc = acc_sc[...]
            denom = acc[:, DH:DH + 1]
            o_ref[...] = (acc[:, :DH] / denom).astype(o_ref.dtype)
            lse_ref[0] = m_sc[...] + jnp.log(denom)

    kv = lambda off: pl.BlockSpec((tq, DH), lambda h, t, qi, kj: (kj[t], off + h))
    col = pl.BlockSpec((1, tq, 1), lambda h, t, qi, kj: (h, qi[t], 0))
    grid_spec = pltpu.PrefetchScalarGridSpec(
        num_scalar_prefetch=2, grid=(HEADS, qi_tab.shape[0]),
        in_specs=[pl.BlockSpec((tq, DH), lambda h, t, qi, kj: (qi[t], _Q_OFF + h)), kv(_K_OFF), kv(_V_OFF), col,
                  pl.BlockSpec((1, 1, tq), lambda h, t, qi, kj: (h, 0, kj[t]))],
        out_specs=[pl.BlockSpec((tq, DH), lambda h, t, qi, kj: (qi[t], h)), col],
        scratch_shapes=[pltpu.VMEM((tq, 1), F32), pltpu.VMEM((tq, 2 * DH), F32)])
    return pl.pallas_call(
        body, name=name, grid_spec=grid_spec,
        out_shape=[jax.ShapeDtypeStruct((T, HEADS * DH), BF16), jax.ShapeDtypeStruct((HEADS, T, 1), F32)],
        compiler_params=_params("parallel", "arbitrary"),
    )(qi_tab, kj_tab, pm, pm, pm, c_col, c_row)


def _fox_delta(do, o, name):
    T, D = o.shape
    tr = _tile(T, ROW_TILE)

    def body(do_ref, o_ref, d_ref):
        prod = do_ref[...].astype(F32) * o_ref[...].astype(F32)
        for h in range(HEADS):
            d_ref[h] = _rowsum(prod[:, h * DH:(h + 1) * DH])

    row = pl.BlockSpec((tr, D), lambda i: (i, 0))
    return pl.pallas_call(
        body, name=name, grid=(T // tr,), in_specs=[row, row],
        out_specs=pl.BlockSpec((HEADS, tr, 1), lambda i: (0, i, 0)),
        out_shape=jax.ShapeDtypeStruct((HEADS, T, 1), F32), compiler_params=_params("parallel"),
    )(do, o)


def _fox_bwd(pm, c_col, c_row, do, lse, delta, name):
    T = pm.shape[0]
    tq = _tile(T, ATTN_TILE)
    nq = T // tq
    rg = min(ATTN_ROWS, tq)
    qi_tab, kj_tab = _causal_pairs(nq, by_key=True)
    npairs = qi_tab.shape[0]

    def body(qi_ref, kj_ref, q_ref, k_ref, v_ref, cq_ref, ck_ref, do_ref, lse_ref, dl_ref,
             dq_ref, dk_ref, dv_ref, rsum_ref, csum_ref, dq_sc, dk_sc, dv_sc):
        t = pl.program_id(1)
        i, j = qi_ref[t], kj_ref[t]

        @pl.when(t == 0)
        def _():
            dq_sc[...] = jnp.zeros_like(dq_sc)

        @pl.when(i == j)
        def _():
            dk_sc[...] = jnp.zeros_like(dk_sc)
            dv_sc[...] = jnp.zeros_like(dv_sc)

        base = pl.multiple_of(i * tq, tq)

        def step(diag):
            ones = _ones_column(tq)
            for r in range(tq // rg):
                rows = slice(r * rg, (r + 1) * rg)
                w = (r + 1) * rg if diag else tq
                qr, dor = q_ref[rows, :], do_ref[rows, :]
                s = _fox_logits(qr, k_ref[:w, :], ck_ref[0, :, :w], r * rg, diag)
                p = jnp.exp(s - (lse_ref[0, rows, :] - cq_ref[0, rows, :]))
                dp = _dot(dor, v_ref[:w, :], "nt")
                dsb = (p * (dp - dl_ref[0, rows, :])).astype(BF16)
                dv_sc[:w, :] += _dot(p.astype(BF16), dor, "tn")
                dk_sc[:w, :] += _dot(dsb, jnp.concatenate([qr, ones[rows, :]], axis=1), "tn")
                dq_sc[pl.ds(base + r * rg, rg), :] += _dot(dsb, jnp.concatenate([k_ref[:w, :], ones[:w, :]], axis=1))

        @pl.when(i > j)
        def _():
            step(False)

        @pl.when(i == j)
        def _():
            step(True)

        @pl.when(i == nq - 1)
        def _():
            dk_ref[...] = dk_sc[:, :DH].astype(dk_ref.dtype)
            dv_ref[...] = dv_sc[...].astype(dv_ref.dtype)
            csum_ref[0] = dk_sc[:, DH:DH + 1]

        @pl.when(t == npairs - 1)
        def _():
            dq_ref[...] = dq_sc[:, :DH].astype(dq_ref.dtype)
            rsum_ref[0] = dq_sc[:, DH:DH + 1]

    col = pl.BlockSpec((1, tq, 1), lambda h, t, qi, kj: (h, qi[t], 0))
    kv = lambda off: pl.BlockSpec((tq, DH), lambda h, t, qi, kj: (kj[t], off + h))
    kv_out = pl.BlockSpec((tq, DH), lambda h, t, qi, kj: (kj[t], h))
    grid_spec = pltpu.PrefetchScalarGridSpec(
        num_scalar_prefetch=2, grid=(HEADS, npairs),
        in_specs=[pl.BlockSpec((tq, DH), lambda h, t, qi, kj: (qi[t], _Q_OFF + h)), kv(_K_OFF), kv(_V_OFF), col,
                  pl.BlockSpec((1, 1, tq), lambda h, t, qi, kj: (h, 0, kj[t])),
                  pl.BlockSpec((tq, DH), lambda h, t, qi, kj: (qi[t], h)), col, col],
        out_specs=[pl.BlockSpec((T, DH), lambda h, t, qi, kj: (0, h)), kv_out, kv_out,
                   pl.BlockSpec((1, T, 1), lambda h, t, qi, kj: (h, 0, 0)),
                   pl.BlockSpec((1, tq, 1), lambda h, t, qi, kj: (h, kj[t], 0))],
        scratch_shapes=[pltpu.VMEM((T, 2 * DH), F32), pltpu.VMEM((tq, 2 * DH), F32), pltpu.VMEM((tq, DH), F32)])
    D = HEADS * DH
    return pl.pallas_call(
        body, name=name, grid_spec=grid_spec,
        out_shape=[jax.ShapeDtypeStruct((T, D), BF16)] * 3 + [jax.ShapeDtypeStruct((HEADS, T, 1), F32)] * 2,
        compiler_params=_params("parallel", "arbitrary"),
    )(qi_tab, kj_tab, pm, pm, pm, c_col, c_row, do, lse, delta)


def _xattn_fwd(q, kv, name):
    T, D = q.shape
    M = kv.shape[0]
    dh = D // MEM_HEADS
    tq = _tile(T, XATTN_TILE)
    scale = 1.0 / math.sqrt(dh)

    def body(q_ref, kv_ref, o_ref):
        for h in range(MEM_HEADS):
            cs = slice(h * dh, (h + 1) * dh)
            s = _dot(q_ref[:, cs], kv_ref[:, cs], "nt") * scale
            p = jnp.exp(s - jnp.max(s, axis=1, keepdims=True))
            p = p / _rowsum(p)
            o_ref[:, cs] = _dot(p.astype(BF16), kv_ref[:, D + h * dh:D + (h + 1) * dh]).astype(o_ref.dtype)

    row = pl.BlockSpec((tq, D), lambda i: (i, 0))
    return pl.pallas_call(
        body, name=name, grid=(T // tq,), in_specs=[row, pl.BlockSpec((M, 2 * D), lambda i: (0, 0))],
        out_specs=row, out_shape=jax.ShapeDtypeStruct((T, D), BF16), compiler_params=_params("parallel"),
    )(q, kv)


def _xattn_bwd(q, kv, do, name):
    T, D = q.shape
    M = kv.shape[0]
    dh = D // MEM_HEADS
    tq = _tile(T, XATTN_TILE)
    scale = 1.0 / math.sqrt(dh)

    def body(q_ref, kv_ref, do_ref, dq_ref, dkv_ref):
        @pl.when(pl.program_id(0) == 0)
        def _():
            dkv_ref[...] = jnp.zeros_like(dkv_ref)

        for h in range(MEM_HEADS):
            cs = slice(h * dh, (h + 1) * dh)
            vs = slice(D + h * dh, D + (h + 1) * dh)
            s = _dot(q_ref[:, cs], kv_ref[:, cs], "nt") * scale
            p = jnp.exp(s - jnp.max(s, axis=1, keepdims=True))
            p = p / _rowsum(p)
            dp = _dot(do_ref[:, cs], kv_ref[:, vs], "nt")
            ds = (p * (dp - _rowsum(p * dp)) * scale).astype(BF16)
            dq_ref[:, cs] = _dot(ds, kv_ref[:, cs]).astype(dq_ref.dtype)
            dkv_ref[:, cs] += _dot(ds, q_ref[:, cs], "tn")
            dkv_ref[:, vs] += _dot(p.astype(BF16), do_ref[:, cs], "tn")

    row = pl.BlockSpec((tq, D), lambda i: (i, 0))
    full = pl.BlockSpec((M, 2 * D), lambda i: (0, 0))
    return pl.pallas_call(
        body, name=name, grid=(T // tq,), in_specs=[row, full, row], out_specs=[row, full],
        out_shape=[jax.ShapeDtypeStruct((T, D), BF16), jax.ShapeDtypeStruct((M, 2 * D), F32)],
        compiler_params=_params("arbitrary"),
    )(q, kv, do)


_HBM = pl.BlockSpec(memory_space=pltpu.HBM)


def _position():
    return lax.axis_index("x"), lax.axis_index("y"), lax.axis_index("c")


def _other_chips(x, y):
    return [(1 - x, y), (x, 1 - y), (1 - x, 1 - y)]


class _Exchange:
    def __init__(self, inputs, out_shapes, scratch, copies, inplace=False):
        self.inputs, self.out_shapes, self.scratch, self.copies, self.inplace = inputs, out_shapes, scratch, copies, inplace

    def start(self, in_refs, out_refs, sems):
        for cp in self.copies(in_refs, out_refs, sems, False)[0]:
            cp.start()

    def wait(self, in_refs, out_refs, sems):
        for cp, how in self.copies(in_refs, out_refs, sems, True)[1]:
            getattr(cp, how)()

    def aliases(self, first_input, first_output):
        return {first_input + w: first_output + w for w in range(len(self.inputs))} if self.inplace else {}


def _run_exchange(ex, name):
    n_in, n_out = len(ex.inputs), len(ex.out_shapes)

    def body(*refs):
        parts = refs[:n_in], refs[n_in:n_in + n_out], refs[n_in + n_out:]
        ex.start(*parts)
        ex.wait(*parts)

    return pl.pallas_call(
        body, name=name, in_specs=[_HBM] * n_in, out_specs=[_HBM] * n_out, out_shape=ex.out_shapes,
        input_output_aliases=ex.aliases(0, 0), scratch_shapes=ex.scratch,
    )(*ex.inputs)


def _chip_exchange(arrays, out_shapes, src_of, dst_of):
    n = len(arrays)

    def copies(srcs, outs, sems, waiting):
        send, recv, local = sems
        x, y, c = _position()
        q = 2 * x + y
        kept, sent, arriving = [], [], []
        for w, (s_ref, o_ref) in enumerate(zip(srcs, outs)):
            kept.append(pltpu.make_async_copy(src_of(s_ref, q, c), dst_of(o_ref, q, c), local.at[w]))
            for j, (px, py) in enumerate(_other_chips(x, y)):
                sems_j = dict(send_sem=send.at[3 * w + j], recv_sem=recv.at[3 * w + j], device_id=(px, py, c),
                              device_id_type=MESH)
                sent.append(pltpu.make_async_remote_copy(src_ref=src_of(s_ref, 2 * px + py, c),
                                                         dst_ref=dst_of(o_ref, q, c), **sems_j))
                if waiting:
                    arriving.append(pltpu.make_async_remote_copy(src_ref=src_of(s_ref, q, c),
                                                                 dst_ref=dst_of(o_ref, 2 * px + py, c), **sems_j))
        return kept + sent, ([(cp, "wait_recv") for cp in arriving] + [(cp, "wait_send") for cp in sent]
                             + [(cp, "wait") for cp in kept])

    scratch = [pltpu.SemaphoreType.DMA((3 * n,)), pltpu.SemaphoreType.DMA((3 * n,)), pltpu.SemaphoreType.DMA((n,))]
    return _Exchange(arrays, out_shapes, scratch, copies)


def _ex_ag_chips(blks):
    return _chip_exchange(blks, [jax.ShapeDtypeStruct((4, 2) + b.shape, b.dtype) for b in blks],
                          src_of=lambda r, chip, c: r, dst_of=lambda r, chip, c: r.at[chip, c])


def _ex_rs_chips(parts):
    return _chip_exchange(parts, [jax.ShapeDtypeStruct(h.shape, h.dtype) for h in parts],
                          src_of=lambda r, chip, c: r.at[chip], dst_of=lambda r, chip, c: r.at[chip])


def _ex_ag_sibling(arrs):
    n = len(arrs)

    def copies(ins, outs, sems, waiting):
        send, recv = sems
        x, y, c = _position()
        to = dict(device_id=(x, y, 1 - c), device_id_type=MESH)
        mine = [pltpu.make_async_remote_copy(src_ref=a.at[:, c], dst_ref=a.at[:, c], send_sem=send.at[w],
                                             recv_sem=recv.at[w], **to) for w, a in enumerate(outs)]
        theirs = [pltpu.make_async_remote_copy(src_ref=a.at[:, c], dst_ref=a.at[:, 1 - c], send_sem=send.at[w],
                                               recv_sem=recv.at[w], **to) for w, a in enumerate(outs if waiting else [])]
        return mine, [(cp, "wait_recv") for cp in theirs] + [(cp, "wait_send") for cp in mine]

    return _Exchange(arrs, [jax.ShapeDtypeStruct(a.shape, a.dtype) for a in arrs],
                     [pltpu.SemaphoreType.DMA((n,)), pltpu.SemaphoreType.DMA((n,))], copies, inplace=True)


def _ex_rs_sibling(blocks):
    n = len(blocks)

    def copies(srcs, outs, sems, waiting):
        send, recv = sems
        x, y, c = _position()
        cps = [pltpu.make_async_remote_copy(src_ref=b.at[:, 1 - c], dst_ref=l, send_sem=send.at[w], recv_sem=recv.at[w],
                                            device_id=(x, y, 1 - c), device_id_type=MESH)
               for w, (b, l) in enumerate(zip(srcs, outs))]
        return cps, [(cp, "wait") for cp in cps]

    return _Exchange(blocks, [jax.ShapeDtypeStruct((4,) + b.shape[2:], b.dtype) for b in blocks],
                     [pltpu.SemaphoreType.DMA((n,)), pltpu.SemaphoreType.DMA((n,))], copies)


def _row_tile(rows, pref=256):
    for t in range(min(pref, rows) // 16 * 16, 0, -16):
        if rows % t == 0:
            return t
    raise ValueError(f"no row tile for {rows}")


def _pair_add(blocks, landed, core, out_dtype, name):
    n, _, s0, s1 = blocks.shape
    tr = _row_tile(s0)

    def body(core_ref, a_ref, b_ref, o_ref):
        del core_ref
        o_ref[...] = (a_ref[...] + b_ref[...]).astype(o_ref.dtype)

    grid_spec = pltpu.PrefetchScalarGridSpec(
        num_scalar_prefetch=1, grid=(n, s0 // tr),
        in_specs=[pl.BlockSpec((1, None, tr, s1), lambda p, i, core: (p, core[0], i, 0)),
                  pl.BlockSpec((1, tr, s1), lambda p, i, core: (p, i, 0))],
        out_specs=pl.BlockSpec((1, tr, s1), lambda p, i, core: (p, i, 0)))
    return pl.pallas_call(
        body, name=name, grid_spec=grid_spec, out_shape=jax.ShapeDtypeStruct(landed.shape, out_dtype),
        compiler_params=_params("parallel", "parallel"),
    )(core, blocks, landed)


def _adamw_math(w, g, m, v):
    m = ADAM_B1 * m + (1.0 - ADAM_B1) * g
    v = ADAM_B2 * v + (1.0 - ADAM_B2) * (g * g)
    m_hat = m / (1.0 - ADAM_B1 ** ADAM_STEP)
    v_hat = v / (1.0 - ADAM_B2 ** ADAM_STEP)
    delta = -ADAM_LR * (m_hat / (jnp.sqrt(v_hat) + ADAM_EPS) + ADAM_WD * w)
    return delta, m, v


def _adamw_reduce(slots, w, m, v, name):
    n, R, C = slots.shape
    tr = _row_tile(R)

    def body(s_ref, w_ref, m_ref, v_ref, g_ref, d_ref, nm_ref, nv_ref):
        g = s_ref[0].astype(F32)
        for p in range(1, n):
            g = g + s_ref[p].astype(F32)
        g_ref[...] = g
        d_ref[...], nm_ref[...], nv_ref[...] = _adamw_math(w_ref[...], g, m_ref[...], v_ref[...])

    row = pl.BlockSpec((tr, C), lambda i: (i, 0))
    return pl.pallas_call(
        body, name=name, grid=(R // tr,), in_specs=[pl.BlockSpec((n, tr, C), lambda i: (0, i, 0)), row, row, row],
        out_specs=[row] * 4, out_shape=[jax.ShapeDtypeStruct((R, C), F32)] * 4, compiler_params=_params("parallel"),
    )(slots, w, m, v)


def _full_from_gathered(a, n):
    s0, s1 = a.shape[2:]
    blk = a.reshape(8, s0, s1)
    if n in COL_SHARDED:
        return blk.transpose(1, 0, 2).reshape(s0, 8 * s1)
    return blk.reshape(8 * s0, s1)


def _blocks_from_full(g, n, shard_shape):
    s0, s1 = shard_shape
    if n in COL_SHARDED:
        blk = g.reshape(s0, 8, s1).transpose(1, 0, 2)
    else:
        blk = g.reshape(8, s0, s1)
    return blk.reshape(4, 2, s0, s1)


def _swiglu_interleave(w):
    d, f2 = w.shape
    return w.reshape(d, 2, f2 // (2 * SWIGLU_TILE), SWIGLU_TILE).transpose(0, 2, 1, 3).reshape(d, f2)


def _swiglu_deinterleave(w):
    d, f2 = w.shape
    return w.reshape(d, f2 // (2 * SWIGLU_TILE), 2, SWIGLU_TILE).transpose(0, 2, 1, 3).reshape(d, f2)


SMALL_ROWS = 16


def _pack_small(vals, loss_row):
    rows = []
    for n in SMALL:
        flat = vals[n].reshape(-1)
        pad = (-flat.shape[0]) % PACK_COLS
        rows.append(jnp.pad(flat, (0, pad)).reshape(-1, PACK_COLS))
    rows.append(loss_row)
    out = jnp.concatenate(rows, axis=0)
    assert out.shape[0] == SMALL_ROWS, out.shape
    return out


def _unpack_small(packed, like):
    out, r = {}, 0
    for n in SMALL:
        size = like[n].size
        rows = -(-size // PACK_COLS)
        out[n] = packed[r:r + rows].reshape(-1)[:size].reshape(like[n].shape)
        r += rows
    return out


class _NoTraffic:
    def host(self, stage):
        return None

    def landed(self, stage, arrays):
        pass

    def grads_ready(self, names, gW, behind=None):
        pass


def _mm_behind(traffic, stage, *args, **kwargs):
    ex = traffic.host(stage)
    if ex is None:
        return _mm(*args, **kwargs)
    out, arrays = _mm(*args, hosted=ex, **kwargs)
    traffic.landed(stage, arrays)
    return out


def _ffn_fwd(x, g_pre, W, tag, traffic, up_stage=None, down_stage=None):
    h = _rms_fwd(x, g_pre, f"{tag}_pre")
    ex = traffic.host(up_stage) if up_stage else None
    u, a, arrays = _mm_swiglu(h, W[f"{tag}_w_in"], f"{tag}_up", hosted=ex)
    if ex is not None:
        traffic.landed(up_stage, arrays)
    z = _mm_behind(traffic, down_stage, a, W[f"{tag}_w_down"], "nn", F32, f"{tag}_down", tk=1408)
    return h, u, a, z


def _ffn_bwd(saved, x, g_pre, w_in, w_down, g_post, dx_out, tag, traffic, down_dw_stage=None, up_dx_stage=None):
    h, u, a, z = saved
    dz, dg_post = _rms_bwd(z, g_post, dx_out, 0.5, f"{tag}_post_bwd", BF16)
    dw_down = _mm_behind(traffic, down_dw_stage, a, dz, "tn", F32, f"{tag}_down_dw", tm=1408)
    du = _mm_swiglu_bwd(dz, w_down, u, f"{tag}_down_dx")
    dh = _mm_behind(traffic, up_dx_stage, du, w_in, "nt", BF16, f"{tag}_up_dx", tk=5632)
    dw_in = _mm(h, du, "tn", F32, f"{tag}_up_dw", tk=4096)
    dx, dg_pre = _rms_bwd(x, g_pre, dh, 1.0, f"{tag}_pre_bwd", F32, resid=dx_out)
    return dx, dg_pre, dg_post, dw_in, dw_down


def _step_local(x, mem, target, W, S, traffic=_NoTraffic()):
    T, D = x.shape
    gW, gS = {}, {}

    f1 = _ffn_fwd(x, S["ffn1_pre_g"], W, "ffn1", traffic, "gather_mixer_chips", "gather_mixer_sibling")
    x1 = _resid_rms(x, f1[3], S["ffn1_post_g"], 0.5, "ffn1_post")

    h2 = _rms_fwd(x1, S["mix_pre_g"], "mix_pre")
    pm = _mm_behind(traffic, "gather_late_chips", h2, W["w_main"], "nn", BF16, "mix_proj_main")
    pf = _mm(h2, W["w_f"], "nn", F32, "mix_proj_f")
    pg = _mm_behind(traffic, "gather_late_sibling", h2, W["w_gates"], "nn", BF16, "mix_proj_gates")
    lbl = S["hg_lb_logits"].reshape(2, HEADS, 1, DH)
    o_a, states = _hgrn_fwd(pm, lbl, "hgrn_fwd")
    oan = _hgout_fwd(o_a, pm, S["hg_norm_g"], "hgrn_out")
    bias = jnp.pad(S["fox_f_bias"], ((0, 0), (0, LANES - HEADS)))
    c = _fox_cumsum(pf, bias, "fox_cumsum")
    c_heads = c[:, :HEADS].T
    c_col, c_row = c_heads[:, :, None], c_heads[:, None, :]
    o_b, lse = _fox_fwd(pm, c_col, c_row, "fox_fwd")
    ya = _mm(oan, W["w_branch_a"], "nn", BF16, "branch_a")
    yb = _mm(o_b, W["w_branch_b"], "nn", BF16, "branch_b")
    y = _merge_fwd(ya, yb, pg, S["b_gate"], "merge")
    z2, x2 = _mm_post_norm(y, W["w_out"], x1, S["mix_post_g"], 1.0, "mix_out")

    h3 = _rms_fwd(x2, S["mem_pre_g"], "mem_pre")
    memn = _rms_fwd(mem, S["mem_kv_g"], "mem_kv_norm")
    qm = _mm(h3, W["w_mq"], "nn", BF16, "mem_q")
    kv = _mm(memn, W["w_mkv"], "nn", BF16, "mem_kv")
    om = _xattn_fwd(qm, kv, "mem_attn")
    z3, x3 = _mm_post_norm(om, W["w_mo"], x2, S["mem_post_g"], 1.0, "mem_o")

    f2 = _ffn_fwd(x3, S["ffn2_pre_g"], W, "ffn2", traffic)
    dx4, sq = _final_loss(x3, f2[3], S["ffn2_post_g"], 0.5, target, "loss")

    dx3, gS["ffn2_pre_g"], gS["ffn2_post_g"], gW["ffn2_w_in"], gW["ffn2_w_down"] = _ffn_bwd(
        f2, x3, S["ffn2_pre_g"], W["ffn2_w_in"], W["ffn2_w_down"], S["ffn2_post_g"], dx4, "ffn2", traffic)
    traffic.grads_ready(["ffn2_w_in", "ffn2_w_down"], gW, behind="scatter_ffn2_sibling")

    dz3, gS["mem_post_g"] = _rms_bwd(z3, S["mem_post_g"], dx3, 1.0, "mem_post_bwd", BF16)
    dom = _mm_behind(traffic, "scatter_ffn2_sibling", dz3, W["w_mo"], "nt", BF16, "mem_o_dx")
    gW["w_mo"] = _mm(om, dz3, "tn", F32, "mem_o_dw")
    dqm, dkv = _xattn_bwd(qm, kv, dom, "mem_attn_bwd")
    dh3 = _mm(dqm, W["w_mq"], "nt", BF16, "mem_q_dx")
    gW["w_mq"] = _mm(h3, dqm, "tn", F32, "mem_q_dw")
    dkvb = dkv.astype(BF16)
    gW["w_mkv"] = _mm(memn, dkvb, "tn", F32, "mem_kv_dw")
    dmemn = _mm(dkvb, W["w_mkv"], "nt", F32, "mem_kv_dx")
    _, gS["mem_kv_g"] = _rms_bwd(mem, S["mem_kv_g"], dmemn, 1.0, "mem_kv_norm_bwd", BF16)
    dx2, gS["mem_pre_g"] = _rms_bwd(x2, S["mem_pre_g"], dh3, 1.0, "mem_pre_bwd", F32, resid=dx3)

    dz2, gS["mix_post_g"] = _rms_bwd(z2, S["mix_post_g"], dx2, 1.0, "mix_post_bwd", BF16)
    dy = _mm(dz2, W["w_out"], "nt", BF16, "mix_out_dx")
    gW["w_out"] = _mm(y, dz2, "tn", F32, "mix_out_dw")
    dya, dyb, dpg, gS["b_gate"] = _merge_bwd(dy, ya, yb, pg, S["b_gate"], "merge_bwd")
    doan = _mm(dya, W["w_branch_a"], "nt", BF16, "branch_a_dx")
    gW["w_branch_a"] = _mm(oan, dya, "tn", F32, "branch_a_dw")
    dob = _mm(dyb, W["w_branch_b"], "nt", BF16, "branch_b_dx")
    gW["w_branch_b"] = _mm(o_b, dyb, "tn", F32, "branch_b_dw")
    traffic.grads_ready(["w_mo", "w_mq", "w_mkv", "w_out", "w_branch_a", "w_branch_b"], gW,
                        behind="scatter_mid_sibling")

    delta = _fox_delta(dob, o_b, "fox_delta")
    dq_b, dk_b, dv_b, ds_rows, ds_cols = _fox_bwd(pm, c_col, c_row, dob, lse, delta, "fox_bwd")
    dc = jnp.pad((ds_rows.reshape(HEADS, T) - ds_cols.reshape(HEADS, T)).T, ((0, 0), (0, LANES - HEADS)))
    dpf, dbias = _fox_dcum(dc, pf, bias, "fox_cumsum_bwd")
    gS["fox_f_bias"] = dbias[:, :HEADS]

    do_a, dg_a, gS["hg_norm_g"] = _hgout_bwd(o_a, pm, S["hg_norm_g"], doan, "hgrn_out_bwd")
    dq_a, df_a, di_a, dlbl = _hgrn_bwd(pm, lbl, states, do_a, "hgrn_bwd")
    gS["hg_lb_logits"] = dlbl.reshape(2, HEADS, DH)

    dpm = jnp.concatenate([dq_a, df_a, di_a, dg_a, dq_b, dk_b, dv_b], axis=1)
    dpf16 = dpf.astype(BF16)
    dh2 = _mm_behind(traffic, "scatter_ffn2_chips", dpm, W["w_main"], "nt", F32, "mix_proj_main_dx")
    dh2 = _mm_behind(traffic, "scatter_mid_sibling", dpg, W["w_gates"], "nt", F32, "mix_proj_gates_dx", add=dh2)
    dh2 = _mm(dpf16, W["w_f"], "nt", F32, "mix_proj_f_dx", add=dh2)
    gW["w_main"] = _mm_behind(traffic, "scatter_mid_chips", h2, dpm, "tn", F32, "mix_proj_main_dw")
    gW["w_gates"] = _mm(h2, dpg, "tn", F32, "mix_proj_gates_dw")
    gW["w_f"] = _mm(h2, dpf16, "tn", F32, "mix_proj_f_dw")
    traffic.grads_ready(["w_in"], gW, behind="scatter_w_in_sibling")
    dx1, gS["mix_pre_g"] = _rms_bwd(x1, S["mix_pre_g"], dh2, 1.0, "mix_pre_bwd", F32, resid=dx2)

    dx0, gS["ffn1_pre_g"], gS["ffn1_post_g"], gW["ffn1_w_in"], gW["ffn1_w_down"] = _ffn_bwd(
        f1, x, S["ffn1_pre_g"], W["ffn1_w_in"], W["ffn1_w_down"], S["ffn1_post_g"], dx1, "ffn1", traffic,
        "scatter_w_in_sibling", "scatter_w_in_chips")
    traffic.grads_ready(["ffn1_w_in", "ffn1_w_down"], gW)
    return sq, dx0, gW, gS


GATHER_FIRST = ["ffn1_w_in", "ffn1_w_down"]
GATHER_MIXER = ["w_in", "w_branch_a", "w_branch_b", "w_out"]
GATHER_LATE = ["w_mq", "w_mkv", "w_mo", "ffn2_w_in", "ffn2_w_down"]
SCATTER_BEHIND = {
    "scatter_ffn2_chips": ["ffn2_w_in", "ffn2_w_down"],
    "scatter_mid_chips": ["w_mo", "w_mq", "w_mkv", "w_out", "w_branch_a", "w_branch_b"],
    "scatter_w_in_chips": ["w_in"],
}


class _Traffic:
    def __init__(self, sent, W, shapes, core, D):
        self.sent, self.W, self.shapes, self.core, self.D = sent, W, shapes, core, D
        self.half, self.pairs, self.slots, self.waiting = {}, {}, {}, {}

    def install(self, names, gathered):
        D = self.D
        for n, g in zip(names, gathered):
            full = _full_from_gathered(g, n)
            if n == "w_in":
                self.W["w_main"] = full[:, :7 * D]
                self.W["w_f"] = jnp.pad(full[:, 7 * D:7 * D + HEADS], ((0, 0), (0, LANES - HEADS)))
                self.W["w_gates"] = full[:, 7 * D + HEADS:]
            elif n in ("ffn1_w_in", "ffn2_w_in"):
                self.W[n] = _swiglu_interleave(full)
            else:
                self.W[n] = full

    def host(self, stage):
        if stage == "gather_mixer_chips":
            return _ex_ag_chips([self.sent[n] for n in GATHER_MIXER])
        if stage == "gather_late_chips":
            return _ex_ag_chips([self.sent[n] for n in GATHER_LATE])
        if stage in ("gather_mixer_sibling", "gather_late_sibling"):
            return _ex_ag_sibling(self.half[stage])
        if stage in SCATTER_BEHIND:
            return _ex_rs_chips([self.pairs[n] for n in SCATTER_BEHIND[stage]])
        if stage in self.waiting:
            return _ex_rs_sibling(self.waiting[stage][1])
        return None

    def landed(self, stage, arrays):
        if stage == "gather_mixer_chips":
            self.half["gather_mixer_sibling"] = arrays
        elif stage == "gather_late_chips":
            self.half["gather_late_sibling"] = arrays
        elif stage == "gather_mixer_sibling":
            self.install(GATHER_MIXER, arrays)
        elif stage == "gather_late_sibling":
            self.install(GATHER_LATE, arrays)
        elif stage in self.waiting:
            self._pair_sums(*self.waiting.pop(stage), arrays)
        else:
            self.slots.update(zip(SCATTER_BEHIND[stage], arrays))

    def _final_grad(self, n, gW):
        D = self.D
        if n == "w_in":
            g = gW["w_main"]
            return jnp.concatenate([g[:, :4 * D], g[:, 4 * D:5 * D] * (1.0 / math.sqrt(DH)), g[:, 5 * D:],
                                    gW["w_f"][:, :HEADS], gW["w_gates"]], axis=1)
        if n in ("ffn1_w_in", "ffn2_w_in"):
            return _swiglu_deinterleave(gW[n])
        return gW[n]

    def _pair_sums(self, names, blocks, got):
        for n, b, l in zip(names, blocks, got):
            self.pairs[n] = _pair_add(b, l, self.core, BF16, f"rs_pair_add_{n}")

    def grads_ready(self, names, gW, behind=None):
        blocks = [_blocks_from_full(self._final_grad(n, gW), n, self.shapes[n]) for n in names]
        if behind is None:
            self._pair_sums(names, blocks, _run_exchange(_ex_rs_sibling(blocks), f"rs_sibling_{names[0]}"))
        else:
            self.waiting[behind] = (names, blocks)

    def finish(self):
        rest = [n for n in BIG if n not in self.slots]
        got = _run_exchange(_ex_rs_chips([self.pairs[n] for n in rest]), "rs_chips_last")
        self.slots.update(zip(rest, got))
        return self.slots


def _train_step(a):
    c_idx = lax.axis_index("c")
    x, mem, target = a["x"][0], a["mem"][0], a["loss_target"][0]
    D = x.shape[1]
    shards = {n: a[n][0] for n in BIG}

    fox_scale = 1.0 / math.sqrt(DH)
    n_mine = shards["w_in"].shape[1]
    dev = 4 * lax.axis_index("x") + 2 * lax.axis_index("y") + c_idx
    cols = dev * n_mine + jnp.arange(n_mine)
    is_fox_q = (cols >= 4 * D) & (cols < 5 * D)
    sent = dict(shards, w_in=shards["w_in"] * jnp.where(is_fox_q, fox_scale, 1.0)[None, :])
    sent = {n: v.astype(BF16) for n, v in sent.items()}
    W = {}
    traffic = _Traffic(sent, W, {n: shards[n].shape for n in BIG}, c_idx.astype(jnp.int32).reshape(1), D)
    first = _run_exchange(_ex_ag_chips([sent[n] for n in GATHER_FIRST]), "ag_first_chips")
    traffic.install(GATHER_FIRST, _run_exchange(_ex_ag_sibling(first), "ag_first_sibling"))
    S = {n: a[n] for n in SMALL}

    sq, grad_x, gW, gS = _step_local(x, mem, target, W, S, traffic)
    slots = traffic.finish()
    big = {n: _adamw_reduce(slots[n], shards[n], a["m_" + n][0], a["v_" + n][0], f"adamw_{n}") for n in BIG}

    loss_row = jnp.pad(sq[:1, :1] * (0.5 / D), ((0, 0), (0, PACK_COLS - 1)))
    small_half = _run_exchange(_ex_ag_chips([_pack_small(gS, loss_row)]), "small_ag_chips")
    small_all = _run_exchange(_ex_ag_sibling(small_half), "small_ag_sibling")[0]
    small_slots = small_all.reshape(8, SMALL_ROWS, PACK_COLS)
    zero_row = jnp.zeros((1, PACK_COLS), F32)
    g_sm, d_sm, m_sm, v_sm = _adamw_reduce(
        small_slots, _pack_small({n: a[n] for n in SMALL}, zero_row),
        _pack_small({n: a["m_" + n] for n in SMALL}, zero_row),
        _pack_small({n: a["v_" + n] for n in SMALL}, zero_row), "adamw_small")

    def unpack(which, small):
        out = _unpack_small(small, {n: a[n] for n in SMALL})
        for n in BIG:
            out[n] = big[n][which][None]
        return [out[n] for n in WEIGHTS]

    loss = g_sm[SMALL_ROWS - 1, 0]
    return (loss, grad_x[None], *unpack(0, g_sm), *unpack(1, d_sm), *unpack(2, m_sm), *unpack(3, v_sm))


def kernel(x, mem, ffn1_pre_g, ffn1_w_in, ffn1_w_down, ffn1_post_g, mix_pre_g, w_in, hg_lb_logits, hg_norm_g, fox_f_bias, w_branch_a, w_branch_b, b_gate, w_out, mix_post_g, mem_pre_g, mem_kv_g, w_mq, w_mkv, w_mo, mem_post_g, ffn2_pre_g, ffn2_w_in, ffn2_w_down, ffn2_post_g, loss_target, m_ffn1_pre_g, m_ffn1_w_in, m_ffn1_w_down, m_ffn1_post_g, m_mix_pre_g, m_w_in, m_hg_lb_logits, m_hg_norm_g, m_fox_f_bias, m_w_branch_a, m_w_branch_b, m_b_gate, m_w_out, m_mix_post_g, m_mem_pre_g, m_mem_kv_g, m_w_mq, m_w_mkv, m_w_mo, m_mem_post_g, m_ffn2_pre_g, m_ffn2_w_in, m_ffn2_w_down, m_ffn2_post_g, v_ffn1_pre_g, v_ffn1_w_in, v_ffn1_w_down, v_ffn1_post_g, v_mix_pre_g, v_w_in, v_hg_lb_logits, v_hg_norm_g, v_fox_f_bias, v_w_branch_a, v_w_branch_b, v_b_gate, v_w_out, v_mix_post_g, v_mem_pre_g, v_mem_kv_g, v_w_mq, v_w_mkv, v_w_mo, v_mem_post_g, v_ffn2_pre_g, v_ffn2_w_in, v_ffn2_w_down, v_ffn2_post_g):
    return _train_step(dict(locals()))
```

```python
import functools
import math

import jax
import jax.numpy as jnp
from jax import lax
from jax.experimental import pallas as pl
from jax.experimental.pallas import tpu as pltpu

F32 = jnp.float32
BF16 = jnp.bfloat16
MESH = pl.DeviceIdType.MESH

EPS = 1e-6
HEADS = 8
DH = 128
MEM_HEADS = 4
CHUNK = 128
HALF = CHUNK // 2
SWIGLU_TILE = 256
LANES = 128
PACK_COLS = 1024
ROW_TILE = 1024
SEQ_BLOCK = 4096
CUMSUM_BLOCK = 512
XATTN_TILE = 2048
ATTN_TILE = 2048
ATTN_ROWS = 256
EXP_CLAMP = 80.0
NEG_BIG = -1e30

ADAM_LR, ADAM_B1, ADAM_B2, ADAM_EPS, ADAM_WD, ADAM_STEP = 0.001, 0.9, 0.999, 1e-08, 0.01, 10

VMEM_LIMIT = 48 * 1024 * 1024

_DN = {
    "nn": (((1,), (0,)), ((), ())),
    "nt": (((1,), (1,)), ((), ())),
    "tn": (((0,), (0,)), ((), ())),
}

BIG = ["ffn1_w_in", "ffn1_w_down", "w_in", "w_branch_a", "w_branch_b", "w_out", "w_mq", "w_mkv", "w_mo",
       "ffn2_w_in", "ffn2_w_down"]
COL_SHARDED = {"ffn1_w_in", "w_in", "w_mkv", "ffn2_w_in"}
SMALL = ["ffn1_pre_g", "ffn1_post_g", "mix_pre_g", "hg_lb_logits", "hg_norm_g", "fox_f_bias", "b_gate",
         "mix_post_g", "mem_pre_g", "mem_kv_g", "mem_post_g", "ffn2_pre_g", "ffn2_post_g"]
WEIGHTS = ["ffn1_pre_g", "ffn1_w_in", "ffn1_w_down", "ffn1_post_g", "mix_pre_g", "w_in", "hg_lb_logits",
           "hg_norm_g", "fox_f_bias", "w_branch_a", "w_branch_b", "b_gate", "w_out", "mix_post_g", "mem_pre_g",
           "mem_kv_g", "w_mq", "w_mkv", "w_mo", "mem_post_g", "ffn2_pre_g", "ffn2_w_in", "ffn2_w_down",
           "ffn2_post_g"]


def _dot(a, b, mode="nn"):
    return lax.dot_general(a, b, _DN[mode], preferred_element_type=F32)


def _sig(x):
    return 1.0 / (1.0 + jnp.exp(-x))


def _sig_approx(x):
    return pl.reciprocal(1.0 + jnp.exp(-x), approx=True)


def _params(*dims):
    return pltpu.CompilerParams(dimension_semantics=dims if dims else None, vmem_limit_bytes=VMEM_LIMIT)


def _tile(dim, pref):
    if dim <= pref:
        return dim
    t = (pref // LANES) * LANES
    while t >= LANES:
        if dim % t == 0:
            return t
        t -= LANES
    raise ValueError(f"no tile for {dim}")


def _colsum(x):
    return jnp.sum(x, axis=0, keepdims=True)


def _rowsum(x):
    return jnp.sum(x, axis=1, keepdims=True)


def _iota(shape, axis):
    return lax.broadcasted_iota(jnp.int32, shape, axis)


def _pick_row(x, r):
    return _colsum(jnp.where(_iota(x.shape, 0) == r, x, 0.0))


def _tri_dot(tri, x):
    hi = x.astype(BF16)
    r1 = x - hi.astype(F32)
    mid = r1.astype(BF16)
    lo = (r1 - mid.astype(F32)).astype(BF16)
    return _dot(tri, hi) + _dot(tri, mid) + _dot(tri, lo)


_MM_TILES = {"nn": (2048, 512, 1024), "nt": (512, 1024, 4096), "tn": (1024, 1024, 2048)}


def _host_call(body, name, grid, in_specs, out_specs, out_shape, scratch_shapes, dims, args, hosted=None):
    if hosted is None:
        results = pl.pallas_call(body, name=name, grid=grid, in_specs=in_specs, out_specs=out_specs, out_shape=out_shape,
                                 scratch_shapes=scratch_shapes, compiler_params=_params(*dims))(*args)
        return list(results), []
    n_in, n_out, n_sc = len(in_specs), len(out_specs), len(scratch_shapes)
    h_in, h_out = len(hosted.inputs), len(hosted.out_shapes)

    def wrapped(*refs):
        cut = [n_in, h_in, n_out, h_out, n_sc]
        at = [sum(cut[:i]) for i in range(len(cut) + 1)]
        ins, hin, outs, hout, scr = (refs[at[i]:at[i + 1]] for i in range(len(cut)))
        hsems = refs[at[-1]:]
        ids = [pl.program_id(d) for d in range(len(grid))]
        first = functools.reduce(jnp.logical_and, [i == 0 for i in ids])
        last = functools.reduce(jnp.logical_and, [i == g - 1 for i, g in zip(ids, grid)])

        @pl.when(first)
        def _():
            hosted.start(hin, hout, hsems)

        body(*ins, *outs, *scr)

        @pl.when(last)
        def _():
            hosted.wait(hin, hout, hsems)

    results = pl.pallas_call(
        wrapped, name=name, grid=grid, in_specs=list(in_specs) + [_HBM] * h_in,
        out_specs=list(out_specs) + [_HBM] * h_out, out_shape=list(out_shape) + list(hosted.out_shapes),
        scratch_shapes=list(scratch_shapes) + list(hosted.scratch), input_output_aliases=hosted.aliases(n_in, n_out),
        compiler_params=_params(*dims))(*args, *hosted.inputs)
    return list(results[:n_out]), list(results[n_out:])


def _mm(a, b, mode, out_dtype, name, add=None, tm=None, tn=None, tk=None, hosted=None):
    tm, tn, tk = (given or pref for given, pref in zip((tm, tn, tk), _MM_TILES[mode]))
    if mode == "nn":
        (M, K), (K2, N) = a.shape, b.shape
    elif mode == "nt":
        (M, K), (N, K2) = a.shape, b.shape
    else:
        (K, M), (K2, N) = a.shape, b.shape
    assert K == K2, (name, a.shape, b.shape)
    tm, tn, tk = _tile(M, tm), _tile(N, tn), _tile(K, tk)
    nk = K // tk
    if mode == "tn":
        a_spec = pl.BlockSpec((tk, tm), lambda i, j, k: (k, i))
    else:
        a_spec = pl.BlockSpec((tm, tk), lambda i, j, k: (i, k))
    if mode == "nt":
        b_spec = pl.BlockSpec((tn, tk), lambda i, j, k: (j, k))
    else:
        b_spec = pl.BlockSpec((tk, tn), lambda i, j, k: (k, j))
    o_spec = pl.BlockSpec((tm, tn), lambda i, j, k: (i, j))
    has_add = add is not None

    def body(*refs):
        a_ref, b_ref = refs[0], refs[1]
        c_ref = refs[2] if has_add else None
        o_ref = refs[3] if has_add else refs[2]
        part = _dot(a_ref[...], b_ref[...], mode)
        if nk == 1:
            if has_add:
                part = part + c_ref[...]
            o_ref[...] = part.astype(o_ref.dtype)
            return
        acc_ref = refs[-1]
        k = pl.program_id(2)

        @pl.when(k == 0)
        def _():
            acc_ref[...] = part + c_ref[...] if has_add else part

        @pl.when(k > 0)
        def _():
            acc_ref[...] += part

        @pl.when(k == nk - 1)
        def _():
            o_ref[...] = acc_ref[...].astype(o_ref.dtype)

    in_specs = [a_spec, b_spec] + ([o_spec] if has_add else [])
    args = (a, b) + ((add,) if has_add else ())
    (out,), landed = _host_call(
        body, name, (M // tm, N // tn, nk), in_specs, [o_spec], [jax.ShapeDtypeStruct((M, N), out_dtype)],
        [pltpu.VMEM((tm, tn), F32)] if nk > 1 else [], ("parallel", "parallel", "arbitrary"), args, hosted)
    return out if hosted is None else (out, landed)


def _rms_fwd(x, g, name, out_dtype=BF16):
    T, D = x.shape
    tr = _tile(T, ROW_TILE)

    def body(x_ref, g_ref, o_ref):
        xv = x_ref[...]
        r = lax.rsqrt(jnp.mean(xv * xv, axis=-1, keepdims=True) + EPS)
        o_ref[...] = (xv * r * g_ref[...]).astype(o_ref.dtype)

    return pl.pallas_call(
        body, name=name, grid=(T // tr,),
        in_specs=[pl.BlockSpec((tr, D), lambda i: (i, 0)), pl.BlockSpec((1, D), lambda i: (0, 0))],
        out_specs=pl.BlockSpec((tr, D), lambda i: (i, 0)),
        out_shape=jax.ShapeDtypeStruct((T, D), out_dtype), compiler_params=_params("parallel"),
    )(x, g)


def _resid_rms(x, z, g, scale, name):
    T, D = x.shape
    tr = _tile(T, ROW_TILE)

    def body(x_ref, z_ref, g_ref, o_ref):
        zv = z_ref[...]
        r = lax.rsqrt(jnp.mean(zv * zv, axis=-1, keepdims=True) + EPS)
        o_ref[...] = x_ref[...] + scale * (zv * r * g_ref[...])

    row = pl.BlockSpec((tr, D), lambda i: (i, 0))
    return pl.pallas_call(
        body, name=name, grid=(T // tr,), in_specs=[row, row, pl.BlockSpec((1, D), lambda i: (0, 0))],
        out_specs=row, out_shape=jax.ShapeDtypeStruct((T, D), F32), compiler_params=_params("parallel"),
    )(x, z, g)


def _mm_post_norm(a, w, x, g, scale, name):
    T, K = a.shape
    D = w.shape[1]
    tm = _tile(T, 1024)

    def body(a_ref, w_ref, x_ref, g_ref, z_ref, o_ref):
        z = _dot(a_ref[...], w_ref[...])
        z_ref[...] = z
        r = lax.rsqrt(jnp.mean(z * z, axis=-1, keepdims=True) + EPS)
        o_ref[...] = x_ref[...] + scale * (z * r * g_ref[...])

    row = pl.BlockSpec((tm, D), lambda i: (i, 0))
    return pl.pallas_call(
        body, name=name, grid=(T // tm,),
        in_specs=[pl.BlockSpec((tm, K), lambda i: (i, 0)), pl.BlockSpec((K, D), lambda i: (0, 0)), row,
                  pl.BlockSpec((1, D), lambda i: (0, 0))],
        out_specs=[row, row], out_shape=[jax.ShapeDtypeStruct((T, D), F32)] * 2, compiler_params=_params("parallel"),
    )(a, w, x, g)


def _final_loss(x, z, g, scale, target, name):
    T, D = x.shape
    tr = _tile(T, ROW_TILE)

    def body(x_ref, z_ref, g_ref, t_ref, dx_ref, acc_ref):
        @pl.when(pl.program_id(0) == 0)
        def _():
            acc_ref[...] = jnp.zeros_like(acc_ref)

        zv = z_ref[...]
        r = lax.rsqrt(jnp.mean(zv * zv, axis=-1, keepdims=True) + EPS)
        e = x_ref[...] + scale * (zv * r * g_ref[...]) - t_ref[...]
        dx_ref[...] = e * (1.0 / D)
        acc_ref[...] += _colsum(_rowsum(e * e))

    row = pl.BlockSpec((tr, D), lambda i: (i, 0))
    return pl.pallas_call(
        body, name=name, grid=(T // tr,), in_specs=[row, row, pl.BlockSpec((1, D), lambda i: (0, 0)), row],
        out_specs=[row, pl.BlockSpec((8, LANES), lambda i: (0, 0))],
        out_shape=[jax.ShapeDtypeStruct((T, D), F32), jax.ShapeDtypeStruct((8, LANES), F32)],
        compiler_params=_params("arbitrary"),
    )(x, z, g, target)


def _rms_bwd(xin, g, dy, scale, name, out_dtype, resid=None):
    T, D = xin.shape
    tr = _tile(T, ROW_TILE)
    has_resid = resid is not None

    def body(*refs):
        x_ref, g_ref, dy_ref = refs[:3]
        r_ref = refs[3] if has_resid else None
        dx_ref, dg_ref = refs[-2], refs[-1]

        @pl.when(pl.program_id(0) == 0)
        def _():
            dg_ref[...] = jnp.zeros_like(dg_ref)

        xv = x_ref[...]
        r = lax.rsqrt(jnp.mean(xv * xv, axis=-1, keepdims=True) + EPS)
        xh = xv * r
        dyv = dy_ref[...].astype(F32) * scale
        dxh = dyv * g_ref[...]
        dx = r * (dxh - xh * jnp.mean(dxh * xh, axis=-1, keepdims=True))
        if has_resid:
            dx = dx + r_ref[...]
        dx_ref[...] = dx.astype(dx_ref.dtype)
        dg_ref[...] += _colsum(dyv * xh)

    row = pl.BlockSpec((tr, D), lambda i: (i, 0))
    vec = pl.BlockSpec((1, D), lambda i: (0, 0))
    return pl.pallas_call(
        body, name=name, grid=(T // tr,), in_specs=[row, vec, row] + ([row] if has_resid else []),
        out_specs=[row, vec],
        out_shape=[jax.ShapeDtypeStruct((T, D), out_dtype), jax.ShapeDtypeStruct((1, D), F32)],
        compiler_params=_params("arbitrary"),
    )(*((xin, g, dy) + ((resid,) if has_resid else ())))


def _mm_swiglu(h, w_in, name, hosted=None):
    T, K = h.shape
    F2 = w_in.shape[1]
    tf = SWIGLU_TILE
    tm = _tile(T, _MM_TILES["nn"][0])

    def body(h_ref, w_ref, u_ref, a_ref):
        u = _dot(h_ref[...], w_ref[...])
        u_ref[...] = u.astype(u_ref.dtype)
        gate, up = u[:, :tf], u[:, tf:]
        a_ref[...] = (gate * _sig_approx(gate) * up).astype(a_ref.dtype)

    (u, a), landed = _host_call(
        body, name, (T // tm, F2 // (2 * tf)),
        [pl.BlockSpec((tm, K), lambda i, j: (i, 0)), pl.BlockSpec((K, 2 * tf), lambda i, j: (0, j))],
        [pl.BlockSpec((tm, 2 * tf), lambda i, j: (i, j)), pl.BlockSpec((tm, tf), lambda i, j: (i, j))],
        [jax.ShapeDtypeStruct((T, F2), BF16), jax.ShapeDtypeStruct((T, F2 // 2), BF16)], [],
        ("parallel", "parallel"), (h, w_in), hosted)
    return u, a, landed


def _mm_swiglu_bwd(dz, w_down, u, name):
    T, D = dz.shape
    F = w_down.shape[0]
    tf = SWIGLU_TILE
    tm = _tile(T, _MM_TILES["nn"][0])

    def body(dz_ref, w_ref, u_ref, o_ref):
        d = _dot(dz_ref[...], w_ref[...], "nt")
        gate = u_ref[:, :tf].astype(F32)
        up = u_ref[:, tf:].astype(F32)
        s = _sig_approx(gate)
        o_ref[:, :tf] = (d * up * (s * (1.0 + gate * (1.0 - s)))).astype(o_ref.dtype)
        o_ref[:, tf:] = (d * gate * s).astype(o_ref.dtype)

    return pl.pallas_call(
        body, name=name, grid=(T // tm, F // tf),
        in_specs=[pl.BlockSpec((tm, D), lambda i, j: (i, 0)), pl.BlockSpec((tf, D), lambda i, j: (j, 0)),
                  pl.BlockSpec((tm, 2 * tf), lambda i, j: (i, j))],
        out_specs=pl.BlockSpec((tm, 2 * tf), lambda i, j: (i, j)),
        out_shape=jax.ShapeDtypeStruct((T, 2 * F), BF16), compiler_params=_params("parallel", "parallel"),
    )(dz, w_down, u)


def _hgout_fwd(o_a, pm, g, name):
    T, D = o_a.shape
    tr = _tile(T, ROW_TILE)

    def body(o_ref, ga_ref, g_ref, out_ref):
        ov = o_ref[...]
        r = lax.rsqrt(jnp.mean(ov * ov, axis=-1, keepdims=True) + EPS)
        ga = ga_ref[...].astype(F32)
        out_ref[...] = (ov * r * g_ref[...] * (ga * _sig(ga))).astype(out_ref.dtype)

    row = pl.BlockSpec((tr, D), lambda i: (i, 0))
    return pl.pallas_call(
        body, name=name, grid=(T // tr,),
        in_specs=[row, pl.BlockSpec((tr, D), lambda i: (i, 3)), pl.BlockSpec((1, D), lambda i: (0, 0))],
        out_specs=row, out_shape=jax.ShapeDtypeStruct((T, D), BF16), compiler_params=_params("parallel"),
    )(o_a, pm, g)


def _hgout_bwd(o_a, pm, g, d_out, name):
    T, D = o_a.shape
    tr = _tile(T, ROW_TILE)

    def body(o_ref, ga_ref, g_ref, d_ref, do_ref, dga_ref, dg_ref):
        @pl.when(pl.program_id(0) == 0)
        def _():
            dg_ref[...] = jnp.zeros_like(dg_ref)

        ov = o_ref[...]
        r = lax.rsqrt(jnp.mean(ov * ov, axis=-1, keepdims=True) + EPS)
        oh = ov * r
        ga = ga_ref[...].astype(F32)
        s = _sig(ga)
        d = d_ref[...].astype(F32)
        dn = d * (ga * s)
        dga_ref[...] = (d * (oh * g_ref[...]) * (s * (1.0 + ga * (1.0 - s)))).astype(dga_ref.dtype)
        dxh = dn * g_ref[...]
        do_ref[...] = (r * (dxh - oh * jnp.mean(dxh * oh, axis=-1, keepdims=True))).astype(do_ref.dtype)
        dg_ref[...] += _colsum(dn * oh)

    row = pl.BlockSpec((tr, D), lambda i: (i, 0))
    vec = pl.BlockSpec((1, D), lambda i: (0, 0))
    return pl.pallas_call(
        body, name=name, grid=(T // tr,), in_specs=[row, pl.BlockSpec((tr, D), lambda i: (i, 3)), vec, row],
        out_specs=[row, row, vec],
        out_shape=[jax.ShapeDtypeStruct((T, D), BF16), jax.ShapeDtypeStruct((T, D), BF16),
                   jax.ShapeDtypeStruct((1, D), F32)],
        compiler_params=_params("arbitrary"),
    )(o_a, pm, g, d_out)


def _merge_fwd(ya, yb, pg, bg, name):
    T, D = ya.shape
    tr = _tile(T, 256)

    def body(ya_ref, yb_ref, pg_ref, bg_ref, o_ref):
        g0 = _sig(pg_ref[:, :D].astype(F32) + bg_ref[:, :D])
        g1 = _sig(pg_ref[:, D:].astype(F32) + bg_ref[:, D:])
        o_ref[...] = (g0 * ya_ref[...].astype(F32) + g1 * yb_ref[...].astype(F32)).astype(o_ref.dtype)

    row = pl.BlockSpec((tr, D), lambda i: (i, 0))
    return pl.pallas_call(
        body, name=name, grid=(T // tr,),
        in_specs=[row, row, pl.BlockSpec((tr, 2 * D), lambda i: (i, 0)), pl.BlockSpec((1, 2 * D), lambda i: (0, 0))],
        out_specs=row, out_shape=jax.ShapeDtypeStruct((T, D), BF16), compiler_params=_params("parallel"),
    )(ya, yb, pg, bg)


def _merge_bwd(dy, ya, yb, pg, bg, name):
    T, D = ya.shape
    tr = _tile(T, 256)

    def body(dy_ref, ya_ref, yb_ref, pg_ref, bg_ref, dya_ref, dyb_ref, dpg_ref, dbg_ref):
        @pl.when(pl.program_id(0) == 0)
        def _():
            dbg_ref[...] = jnp.zeros_like(dbg_ref)

        d = dy_ref[...].astype(F32)
        g0 = _sig(pg_ref[:, :D].astype(F32) + bg_ref[:, :D])
        g1 = _sig(pg_ref[:, D:].astype(F32) + bg_ref[:, D:])
        dya_ref[...] = (d * g0).astype(dya_ref.dtype)
        dyb_ref[...] = (d * g1).astype(dyb_ref.dtype)
        dg0 = d * ya_ref[...].astype(F32) * (g0 * (1.0 - g0))
        dg1 = d * yb_ref[...].astype(F32) * (g1 * (1.0 - g1))
        dpg_ref[:, :D] = dg0.astype(dpg_ref.dtype)
        dpg_ref[:, D:] = dg1.astype(dpg_ref.dtype)
        dbg_ref[:, :D] += _colsum(dg0)
        dbg_ref[:, D:] += _colsum(dg1)

    row = pl.BlockSpec((tr, D), lambda i: (i, 0))
    wide = pl.BlockSpec((tr, 2 * D), lambda i: (i, 0))
    wvec = pl.BlockSpec((1, 2 * D), lambda i: (0, 0))
    return pl.pallas_call(
        body, name=name, grid=(T // tr,), in_specs=[row, row, row, wide, wvec],
        out_specs=[row, row, wide, wvec],
        out_shape=[jax.ShapeDtypeStruct((T, D), BF16), jax.ShapeDtypeStruct((T, D), BF16),
                   jax.ShapeDtypeStruct((T, 2 * D), BF16), jax.ShapeDtypeStruct((1, 2 * D), F32)],
        compiler_params=_params("arbitrary"),
    )(dy, ya, yb, pg, bg)


def _hgrn_chunk_terms(q, fl, lb, tri):
    shape = q.shape
    row = _iota(shape, 0)
    sg = _sig(fl)
    f = lb + (1.0 - lb) * sg
    k = 1.0 - f
    b = _tri_dot(tri, jnp.log(f))
    ref1 = jnp.where(row < HALF, _pick_row(b, HALF // 2), _pick_row(b, HALF + HALF // 2))
    b_half = _pick_row(b, HALF - 1)
    b_last = _pick_row(b, CHUNK - 1)
    sq = _sig(q)
    qs = q * sq
    e_q1 = jnp.exp(jnp.minimum(b - ref1, EXP_CLAMP))
    e_k1 = jnp.exp(jnp.minimum(ref1 - b, EXP_CLAMP))
    e_q2 = jnp.exp(jnp.minimum(b - b_half, 0.0))
    e_k2 = jnp.exp(jnp.minimum(b_half - b, 0.0))
    e_b = jnp.exp(b)
    e_kd = jnp.exp(b_last - b)
    return dict(sg=sg, f=f, k=k, sq=sq, qs=qs, e_q1=e_q1, e_k1=e_k1, e_q2=e_q2, e_k2=e_k2, e_b=e_b, e_kd=e_kd,
                e_last=jnp.exp(b_last))


def _hgrn_masks():
    r = _iota((CHUNK, CHUNK), 0)
    c = _iota((CHUNK, CHUNK), 1)
    causal = r >= c
    same = (r < HALF) == (c < HALF)
    return causal, causal & same, (r >= HALF) & (c < HALF)


def _softmax_lb(lbl_ref):
    l0, l1 = lbl_ref[0, 0], lbl_ref[1, 0]
    mx = jnp.maximum(l0, l1)
    e0, e1 = jnp.exp(l0 - mx), jnp.exp(l1 - mx)
    return e0 / (e0 + e1)


def _hgrn_fwd(pm, lbl, name):
    T = pm.shape[0]
    tb = _tile(T, SEQ_BLOCK)
    nc = tb // CHUNK

    def body(q_ref, f_ref, i_ref, lbl_ref, o_ref, st_ref, s_sc):
        @pl.when(pl.program_id(1) == 0)
        def _():
            s_sc[...] = jnp.zeros_like(s_sc)

        lb = _softmax_lb(lbl_ref)
        causal, m1, m2 = _hgrn_masks()
        tri = jnp.where(causal, 1.0, 0.0).astype(BF16)
        parts = []
        for ci in range(nc):
            sl = pl.ds(ci * CHUNK, CHUNK)
            t = _hgrn_chunk_terms(q_ref[sl, :].astype(F32), f_ref[sl, :].astype(F32), lb, tri)
            iv = i_ref[sl, :]
            a1 = _dot((t["qs"] * t["e_q1"]).astype(BF16), (t["k"] * t["e_k1"]).astype(BF16), "nt")
            a2 = _dot((t["qs"] * t["e_q2"]).astype(BF16), (t["k"] * t["e_k2"]).astype(BF16), "nt")
            a = jnp.where(m1, a1, 0.0) + jnp.where(m2, a2, 0.0)
            parts.append((_dot(a.astype(BF16), iv), (t["qs"] * t["e_b"]).astype(BF16),
                          _dot(iv, (t["k"] * t["e_kd"]).astype(BF16), "tn"), t["e_last"]))
        st = s_sc[...]
        for ci, (o_intra, qi, grow, e_last) in enumerate(parts):
            st_ref[0, ci] = st
            o_ref[pl.ds(ci * CHUNK, CHUNK), :] = o_intra + _dot(qi, st.astype(BF16), "nt")
            st = e_last * st + grow
        s_sc[...] = st

    blk = lambda off: pl.BlockSpec((tb, DH), lambda h, b: (b, off + h))
    return pl.pallas_call(
        body, name=name, grid=(HEADS, T // tb),
        in_specs=[blk(0), blk(HEADS), blk(2 * HEADS), pl.BlockSpec((2, 1, 1, DH), lambda h, b: (0, h, 0, 0))],
        out_specs=[pl.BlockSpec((tb, DH), lambda h, b: (b, h)),
                   pl.BlockSpec((1, nc, DH, DH), lambda h, b: (h, b, 0, 0))],
        out_shape=[jax.ShapeDtypeStruct((T, HEADS * DH), F32),
                   jax.ShapeDtypeStruct((HEADS, T // CHUNK, DH, DH), F32)],
        scratch_shapes=[pltpu.VMEM((DH, DH), F32)],
        compiler_params=_params("parallel", "arbitrary"),
    )(pm, pm, pm, lbl)


def _hgrn_bwd(pm, lbl, states, do, name):
    T = pm.shape[0]
    tb = _tile(T, SEQ_BLOCK)
    nc = tb // CHUNK
    nb = T // tb

    def body(q_ref, f_ref, i_ref, lbl_ref, st_ref, do_ref, dq_ref, df_ref, di_ref, dl_ref, ds_sc, dlb_sc):
        @pl.when(pl.program_id(1) == 0)
        def _():
            ds_sc[...] = jnp.zeros_like(ds_sc)
            dlb_sc[...] = jnp.zeros_like(dlb_sc)

        lb = _softmax_lb(lbl_ref)
        causal, m1, m2 = _hgrn_masks()
        tri = jnp.where(causal, 1.0, 0.0).astype(BF16)
        tri_rev = jnp.where(_iota((CHUNK, CHUNK), 0) <= _iota((CHUNK, CHUNK), 1), 1.0, 0.0).astype(BF16)
        last_row = _iota((CHUNK, DH), 0) == CHUNK - 1
        dsn = ds_sc[...]
        dlb = jnp.zeros((1, DH), F32)
        for ci in reversed(range(nc)):
            sl = pl.ds(ci * CHUNK, CHUNK)
            q = q_ref[sl, :].astype(F32)
            t = _hgrn_chunk_terms(q, f_ref[sl, :].astype(F32), lb, tri)
            iv = i_ref[sl, :]
            dov = do_ref[sl, :]
            qe1, ke1 = t["qs"] * t["e_q1"], t["k"] * t["e_k1"]
            qe2, ke2 = t["qs"] * t["e_q2"], t["k"] * t["e_k2"]
            qi, kd = t["qs"] * t["e_b"], t["k"] * t["e_kd"]
            qe1b, ke1b, qe2b, ke2b = qe1.astype(BF16), ke1.astype(BF16), qe2.astype(BF16), ke2.astype(BF16)
            a = jnp.where(m1, _dot(qe1b, ke1b, "nt"), 0.0) + jnp.where(m2, _dot(qe2b, ke2b, "nt"), 0.0)
            st = st_ref[0, ci]
            dsnb = dsn.astype(BF16)
            da = _dot(dov, iv, "nt")
            da1 = jnp.where(m1, da, 0.0).astype(BF16)
            da2 = jnp.where(m2, da, 0.0).astype(BF16)
            di_ref[sl, :] = (_dot(a.astype(BF16), dov, "tn") + _dot(kd.astype(BF16), dsnb, "nt")).astype(di_ref.dtype)
            dqe1, dke1 = _dot(da1, ke1b), _dot(da1, qe1b, "tn")
            dqe2, dke2 = _dot(da2, ke2b), _dot(da2, qe2b, "tn")
            dqi = _dot(dov, st.astype(BF16))
            dkd = _dot(iv, dsnb)
            ds_before = t["e_last"] * dsn + _dot(dov, qi.astype(BF16), "tn")
            dqs =dqe1 * t["e_q1"] + dqe2 * t["e_q2"] + dqi * t["e_b"]
            dk = dke1 * t["e_k1"] + dke2 * t["e_k2"] + dkd * t["e_kd"]
            qib, kdb = qi.astype(BF16).astype(F32), kd.astype(BF16).astype(F32)
            db = (dqe1 * qe1b.astype(F32) - dke1 * ke1b.astype(F32) + dqe2 * qe2b.astype(F32)
                  - dke2 * ke2b.astype(F32) + dqi * qib - dkd * kdb)
            extra = _colsum(dkd * kdb) + t["e_last"] * _colsum(dsn * st)
            db = db + jnp.where(last_row, extra, 0.0)
            dlf = _tri_dot(tri_rev, db)
            dfv = dlf / t["f"] - dk
            sg = t["sg"]
            df_ref[sl, :] = (dfv * (1.0 - lb) * sg * (1.0 - sg)).astype(df_ref.dtype)
            dlb = dlb + _colsum(dfv * (1.0 - sg))
            sq = t["sq"]
            dq_ref[sl, :] = (dqs * (sq * (1.0 + q * (1.0 - sq)))).astype(dq_ref.dtype)
            dsn = ds_before
        ds_sc[...] = dsn
        dlb_sc[...] += dlb

        @pl.when(pl.program_id(1) == nb - 1)
        def _():
            dl0 = dlb_sc[...] * lb * (1.0 - lb)
            dl_ref[0, 0] = dl0
            dl_ref[1, 0] = -dl0

    blk = lambda off: pl.BlockSpec((tb, DH), lambda h, b: (nb - 1 - b, off + h))
    lspec = pl.BlockSpec((2, 1, 1, DH), lambda h, b: (0, h, 0, 0))
    out_blk = pl.BlockSpec((tb, DH), lambda h, b: (nb - 1 - b, h))
    D = HEADS * DH
    return pl.pallas_call(
        body, name=name, grid=(HEADS, nb),
        in_specs=[blk(0), blk(HEADS), blk(2 * HEADS), lspec,
                  pl.BlockSpec((1, nc, DH, DH), lambda h, b: (h, nb - 1 - b, 0, 0)), out_blk],
        out_specs=[out_blk, out_blk, out_blk, lspec],
        out_shape=[jax.ShapeDtypeStruct((T, D), BF16)] * 3 + [jax.ShapeDtypeStruct((2, HEADS, 1, DH), F32)],
        scratch_shapes=[pltpu.VMEM((DH, DH), F32), pltpu.VMEM((1, DH), F32)],
        compiler_params=_params("parallel", "arbitrary"),
    )(pm, pm, pm, lbl, states, do)


def _log_sigmoid(x):
    return jnp.minimum(x, 0.0) - jnp.log(1.0 + jnp.exp(-jnp.abs(x)))


def _fox_cumsum(pf, bias, name):
    T = pf.shape[0]
    tb = _tile(T, CUMSUM_BLOCK)

    def body(x_ref, b_ref, c_ref, carry):
        @pl.when(pl.program_id(0) == 0)
        def _():
            carry[...] = jnp.zeros_like(carry)

        tri = jnp.where(_iota((tb, tb), 0) >= _iota((tb, tb), 1), 1.0, 0.0).astype(BF16)
        c = _tri_dot(tri, _log_sigmoid(x_ref[...] + b_ref[...])) + carry[...]
        c_ref[...] = c
        carry[...] = _pick_row(c, tb - 1)

    row = pl.BlockSpec((tb, LANES), lambda i: (i, 0))
    return pl.pallas_call(
        body, name=name, grid=(T // tb,), in_specs=[row, pl.BlockSpec((1, LANES), lambda i: (0, 0))],
        out_specs=row, out_shape=jax.ShapeDtypeStruct((T, LANES), F32),
        scratch_shapes=[pltpu.VMEM((1, LANES), F32)], compiler_params=_params("arbitrary"),
    )(pf, bias)


def _fox_dcum(dc, pf, bias, name):
    T = pf.shape[0]
    tb = _tile(T, CUMSUM_BLOCK)
    nb = T // tb

    def body(dc_ref, x_ref, b_ref, dx_ref, db_ref, carry):
        @pl.when(pl.program_id(0) == 0)
        def _():
            carry[...] = jnp.zeros_like(carry)
            db_ref[...] = jnp.zeros_like(db_ref)

        tri_rev = jnp.where(_iota((tb, tb), 0) <= _iota((tb, tb), 1), 1.0, 0.0).astype(BF16)
        dls = _tri_dot(tri_rev, dc_ref[...]) + carry[...]
        carry[...] = _pick_row(dls, 0)
        dx = dls * (1.0 - _sig(x_ref[...] + b_ref[...]))
        dx_ref[...] = dx
        db_ref[...] += _colsum(dx)

    row = pl.BlockSpec((tb, LANES), lambda i: (nb - 1 - i, 0))
    vec = pl.BlockSpec((1, LANES), lambda i: (0, 0))
    return pl.pallas_call(
        body, name=name, grid=(nb,), in_specs=[row, row, vec], out_specs=[row, vec],
        out_shape=[jax.ShapeDtypeStruct((T, LANES), F32), jax.ShapeDtypeStruct((1, LANES), F32)],
        scratch_shapes=[pltpu.VMEM((1, LANES), F32)], compiler_params=_params("arbitrary"),
    )(dc, pf, bias)


_Q_OFF, _K_OFF, _V_OFF = 4 * HEADS, 5 * HEADS, 6 * HEADS


def _causal_pairs(nq, by_key):
    if by_key:
        pairs = [(i, j) for j in range(nq) for i in range(j, nq)]
    else:
        pairs = [(i, j) for i in range(nq) for j in range(i + 1)]
    return jnp.asarray([p[0] for p in pairs], jnp.int32), jnp.asarray([p[1] for p in pairs], jnp.int32)


def _fox_logits(q, k, ck, row0, masked):
    s = _dot(q, k, "nt") - ck
    if masked:
        s = jnp.where(_iota(s.shape, 0) + row0 >= _iota(s.shape, 1), s, NEG_BIG)
    return s


def _ones_column(rows):
    return jnp.where(_iota((rows, DH), 1) == 0, 1.0, 0.0).astype(BF16)


def _fox_fwd(pm, c_col, c_row, name):
    T = pm.shape[0]
    tq = _tile(T, ATTN_TILE)
    nq = T // tq
    rg = min(ATTN_ROWS, tq)
    qi_tab, kj_tab = _causal_pairs(nq, by_key=False)

    def body(qi_ref, kj_ref, q_ref, k_ref, v_ref, cq_ref, ck_ref, o_ref, lse_ref, m_sc, acc_sc):
        t = pl.program_id(1)
        i, j = qi_ref[t], kj_ref[t]

        @pl.when(j == 0)
        def _():
            m_sc[...] = jnp.full_like(m_sc, NEG_BIG)
            acc_sc[...] = jnp.zeros_like(acc_sc)

        def step(diag):
            m_all, acc_all = m_sc[...], acc_sc[...]
            ones = _ones_column(tq)
            ms, accs = [], []
            for r in range(tq // rg):
                rows = slice(r * rg, (r + 1) * rg)
                w = (r + 1) * rg if diag else tq
                cq = cq_ref[0, rows, :]
                s = _fox_logits(q_ref[rows, :], k_ref[:w, :], ck_ref[0, :, :w], r * rg, diag)
                m_old = m_all[rows, :]
                m_new = jnp.maximum(m_old, jnp.max(s, axis=1, keepdims=True) + cq)
                alpha = jnp.exp(m_old - m_new)
                p = jnp.exp(s - (m_new - cq)).astype(BF16)
                v_one = jnp.concatenate([v_ref[:w, :], ones[:w, :]], axis=1)
                ms.append(m_new)
                accs.append(alpha * acc_all[rows, :] + _dot(p, v_one))
            m_sc[...] = jnp.concatenate(ms, axis=0)
            acc_sc[...] = jnp.concatenate(accs, axis=0)

        @pl.when(j < i)
        def _():
            step(False)

        @pl.when(j == i)
        def _():
            step(True)
            acc = acc_sc[...]
            denom = acc[:, DH:DH + 1]
            o_ref[...] = (acc[:, :DH] / denom).astype(o_ref.dtype)
            lse_ref[0] = m_sc[...] + jnp.log(denom)

    kv = lambda off: pl.BlockSpec((tq, DH), lambda h, t, qi, kj: (kj[t], off + h))
    col = pl.BlockSpec((1, tq, 1), lambda h, t, qi, kj: (h, qi[t], 0))
    grid_spec = pltpu.PrefetchScalarGridSpec(
        num_scalar_prefetch=2, grid=(HEADS, qi_tab.shape[0]),
        in_specs=[pl.BlockSpec((tq, DH), lambda h, t, qi, kj: (qi[t], _Q_OFF + h)), kv(_K_OFF), kv(_V_OFF), col,
                  pl.BlockSpec((1, 1, tq), lambda h, t, qi, kj: (h, 0, kj[t]))],
        out_specs=[pl.BlockSpec((tq, DH), lambda h, t, qi, kj: (qi[t], h)), col],
        scratch_shapes=[pltpu.VMEM((tq, 1), F32), pltpu.VMEM((tq, 2 * DH), F32)])
    return pl.pallas_call(
        body, name=name, grid_spec=grid_spec,
        out_shape=[jax.ShapeDtypeStruct((T, HEADS * DH), BF16), jax.ShapeDtypeStruct((HEADS, T, 1), F32)],
        compiler_params=_params("parallel", "arbitrary"),
    )(qi_tab, kj_tab, pm, pm, pm, c_col, c_row)


def _fox_delta(do, o, name):
    T, D = o.shape
    tr = _tile(T, ROW_TILE)

    def body(do_ref, o_ref, d_ref):
        prod = do_ref[...].astype(F32) * o_ref[...].astype(F32)
        for h in range(HEADS):
            d_ref[h] = _rowsum(prod[:, h * DH:(h + 1) * DH])

    row = pl.BlockSpec((tr, D), lambda i: (i, 0))
    return pl.pallas_call(
        body, name=name, grid=(T // tr,), in_specs=[row, row],
        out_specs=pl.BlockSpec((HEADS, tr, 1), lambda i: (0, i, 0)),
        out_shape=jax.ShapeDtypeStruct((HEADS, T, 1), F32), compiler_params=_params("parallel"),
    )(do, o)


def _fox_bwd(pm, c_col, c_row, do, lse, delta, name):
    T = pm.shape[0]
    tq = _tile(T, ATTN_TILE)
    nq = T // tq
    rg = min(ATTN_ROWS, tq)
    qi_tab, kj_tab = _causal_pairs(nq, by_key=True)
    npairs = qi_tab.shape[0]

    def body(qi_ref, kj_ref, q_ref, k_ref, v_ref, cq_ref, ck_ref, do_ref, lse_ref, dl_ref,
             dq_ref, dk_ref, dv_ref, rsum_ref, csum_ref, dq_sc, dk_sc, dv_sc):
        t = pl.program_id(1)
        i, j = qi_ref[t], kj_ref[t]

        @pl.when(t == 0)
        def _():
            dq_sc[...] = jnp.zeros_like(dq_sc)

        @pl.when(i == j)
        def _():
            dk_sc[...] = jnp.zeros_like(dk_sc)
            dv_sc[...] = jnp.zeros_like(dv_sc)

        base = pl.multiple_of(i * tq, tq)

        def step(diag):
            ones = _ones_column(tq)
            for r in range(tq // rg):
                rows = slice(r * rg, (r + 1) * rg)
                w = (r + 1) * rg if diag else tq
                qr, dor = q_ref[rows, :], do_ref[rows, :]
                s = _fox_logits(qr, k_ref[:w, :], ck_ref[0, :, :w], r * rg, diag)
                p = jnp.exp(s - (lse_ref[0, rows, :] - cq_ref[0, rows, :]))
                dp = _dot(dor, v_ref[:w, :], "nt")
                dsb = (p * (dp - dl_ref[0, rows, :])).astype(BF16)
                dv_sc[:w, :] += _dot(p.astype(BF16), dor, "tn")
                dk_sc[:w, :] += _dot(dsb, jnp.concatenate([qr, ones[rows, :]], axis=1), "tn")
                dq_sc[pl.ds(base + r * rg, rg), :] += _dot(dsb, jnp.concatenate([k_ref[:w, :], ones[:w, :]], axis=1))

        @pl.when(i > j)
        def _():
            step(False)

        @pl.when(i == j)
        def _():
            step(True)

        @pl.when(i == nq - 1)
        def _():
            dk_ref[...] = dk_sc[:, :DH].astype(dk_ref.dtype)
            dv_ref[...] = dv_sc[...].astype(dv_ref.dtype)
            csum_ref[0] = dk_sc[:, DH:DH + 1]

        @pl.when(t == npairs - 1)
        def _():
            dq_ref[...] = dq_sc[:, :DH].astype(dq_ref.dtype)
            rsum_ref[0] = dq_sc[:, DH:DH + 1]

    col = pl.BlockSpec((1, tq, 1), lambda h, t, qi, kj: (h, qi[t], 0))
    kv = lambda off: pl.BlockSpec((tq, DH), lambda h, t, qi, kj: (kj[t], off + h))
    kv_out = pl.BlockSpec((tq, DH), lambda h, t, qi, kj: (kj[t], h))
    grid_spec = pltpu.PrefetchScalarGridSpec(
        num_scalar_prefetch=2, grid=(HEADS, npairs),
        in_specs=[pl.BlockSpec((tq, DH), lambda h, t, qi, kj: (qi[t], _Q_OFF + h)), kv(_K_OFF), kv(_V_OFF), col,
                  pl.BlockSpec((1, 1, tq), lambda h, t, qi, kj: (h, 0, kj[t])),
                  pl.BlockSpec((tq, DH), lambda h, t, qi, kj: (qi[t], h)), col, col],
        out_specs=[pl.BlockSpec((T, DH), lambda h, t, qi, kj: (0, h)), kv_out, kv_out,
                   pl.BlockSpec((1, T, 1), lambda h, t, qi, kj: (h, 0, 0)),
                   pl.BlockSpec((1, tq, 1), lambda h, t, qi, kj: (h, kj[t], 0))],
        scratch_shapes=[pltpu.VMEM((T, 2 * DH), F32), pltpu.VMEM((tq, 2 * DH), F32), pltpu.VMEM((tq, DH), F32)])
    D = HEADS * DH
    return pl.pallas_call(
        body, name=name, grid_spec=grid_spec,
        out_shape=[jax.ShapeDtypeStruct((T, D), BF16)] * 3 + [jax.ShapeDtypeStruct((HEADS, T, 1), F32)] * 2,
        compiler_params=_params("parallel", "arbitrary"),
    )(qi_tab, kj_tab, pm, pm, pm, c_col, c_row, do, lse, delta)


def _xattn_fwd(q, kv, name):
    T, D = q.shape
    M = kv.shape[0]
    dh = D // MEM_HEADS
    tq = _tile(T, XATTN_TILE)
    scale = 1.0 / math.sqrt(dh)

    def body(q_ref, kv_ref, o_ref):
        for h in range(MEM_HEADS):
            cs = slice(h * dh, (h + 1) * dh)
            s = _dot(q_ref[:, cs], kv_ref[:, cs], "nt") * scale
            p = jnp.exp(s - jnp.max(s, axis=1, keepdims=True))
            p = p / _rowsum(p)
            o_ref[:, cs] = _dot(p.astype(BF16), kv_ref[:, D + h * dh:D + (h + 1) * dh]).astype(o_ref.dtype)

    row = pl.BlockSpec((tq, D), lambda i: (i, 0))
    return pl.pallas_call(
        body, name=name, grid=(T // tq,), in_specs=[row, pl.BlockSpec((M, 2 * D), lambda i: (0, 0))],
        out_specs=row, out_shape=jax.ShapeDtypeStruct((T, D), BF16), compiler_params=_params("parallel"),
    )(q, kv)


def _xattn_bwd(q, kv, do, name):
    T, D = q.shape
    M = kv.shape[0]
    dh = D // MEM_HEADS
    tq = _tile(T, XATTN_TILE)
    scale = 1.0 / math.sqrt(dh)

    def body(q_ref, kv_ref, do_ref, dq_ref, dkv_ref):
        @pl.when(pl.program_id(0) == 0)
        def _():
            dkv_ref[...] = jnp.zeros_like(dkv_ref)

        for h in range(MEM_HEADS):
            cs = slice(h * dh, (h + 1) * dh)
            vs = slice(D + h * dh, D + (h + 1) * dh)
            s = _dot(q_ref[:, cs], kv_ref[:, cs], "nt") * scale
            p = jnp.exp(s - jnp.max(s, axis=1, keepdims=True))
            p = p / _rowsum(p)
            dp = _dot(do_ref[:, cs], kv_ref[:, vs], "nt")
            ds = (p * (dp - _rowsum(p * dp)) * scale).astype(BF16)
            dq_ref[:, cs] = _dot(ds, kv_ref[:, cs]).astype(dq_ref.dtype)
            dkv_ref[:, cs] += _dot(ds, q_ref[:, cs], "tn")
            dkv_ref[:, vs] += _dot(p.astype(BF16), do_ref[:, cs], "tn")

    row = pl.BlockSpec((tq, D), lambda i: (i, 0))
    full = pl.BlockSpec((M, 2 * D), lambda i: (0, 0))
    return pl.pallas_call(
        body, name=name, grid=(T // tq,), in_specs=[row, full, row], out_specs=[row, full],
        out_shape=[jax.ShapeDtypeStruct((T, D), BF16), jax.ShapeDtypeStruct((M, 2 * D), F32)],
        compiler_params=_params("arbitrary"),
    )(q, kv, do)


_HBM = pl.BlockSpec(memory_space=pltpu.HBM)


def _position():
    return lax.axis_index("x"), lax.axis_index("y"), lax.axis_index("c")


def _other_chips(x, y):
    return [(1 - x, y), (x, 1 - y), (1 - x, 1 - y)]


class _Exchange:
    def __init__(self, inputs, out_shapes, scratch, copies, inplace=False):
        self.inputs, self.out_shapes, self.scratch, self.copies, self.inplace = inputs, out_shapes, scratch, copies, inplace

    def start(self, in_refs, out_refs, sems):
        for cp in self.copies(in_refs, out_refs, sems, False)[0]:
            cp.start()

    def wait(self, in_refs, out_refs, sems):
        for cp, how in self.copies(in_refs, out_refs, sems, True)[1]:
            getattr(cp, how)()

    def aliases(self, first_input, first_output):
        return {first_input + w: first_output + w for w in range(len(self.inputs))} if self.inplace else {}


def _run_exchange(ex, name):
    n_in, n_out = len(ex.inputs), len(ex.out_shapes)

    def body(*refs):
        parts = refs[:n_in], refs[n_in:n_in + n_out], refs[n_in + n_out:]
        ex.start(*parts)
        ex.wait(*parts)

    return pl.pallas_call(
        body, name=name, in_specs=[_HBM] * n_in, out_specs=[_HBM] * n_out, out_shape=ex.out_shapes,
        input_output_aliases=ex.aliases(0, 0), scratch_shapes=ex.scratch,
    )(*ex.inputs)


def _chip_exchange(arrays, out_shapes, src_of, dst_of):
    n = len(arrays)

    def copies(srcs, outs, sems, waiting):
        send, recv, local = sems
        x, y, c = _position()
        q = 2 * x + y
        kept, sent, arriving = [], [], []
        for w, (s_ref, o_ref) in enumerate(zip(srcs, outs)):
            kept.append(pltpu.make_async_copy(src_of(s_ref, q, c), dst_of(o_ref, q, c), local.at[w]))
            for j, (px, py) in enumerate(_other_chips(x, y)):
                sems_j = dict(send_sem=send.at[3 * w + j], recv_sem=recv.at[3 * w + j], device_id=(px, py, c),
                              device_id_type=MESH)
                sent.append(pltpu.make_async_remote_copy(src_ref=src_of(s_ref, 2 * px + py, c),
                                                         dst_ref=dst_of(o_ref, q, c), **sems_j))
                if waiting:
                    arriving.append(pltpu.make_async_remote_copy(src_ref=src_of(s_ref, q, c),
                                                                 dst_ref=dst_of(o_ref, 2 * px + py, c), **sems_j))
        return kept + sent, ([(cp, "wait_recv") for cp in arriving] + [(cp, "wait_send") for cp in sent]
                             + [(cp, "wait") for cp in kept])

    scratch = [pltpu.SemaphoreType.DMA((3 * n,)), pltpu.SemaphoreType.DMA((3 * n,)), pltpu.SemaphoreType.DMA((n,))]
    return _Exchange(arrays, out_shapes, scratch, copies)


def _ex_ag_chips(blks):
    return _chip_exchange(blks, [jax.ShapeDtypeStruct((4, 2) + b.shape, b.dtype) for b in blks],
                          src_of=lambda r, chip, c: r, dst_of=lambda r, chip, c: r.at[chip, c])


def _ex_rs_chips(parts):
    return _chip_exchange(parts, [jax.ShapeDtypeStruct(h.shape, h.dtype) for h in parts],
                          src_of=lambda r, chip, c: r.at[chip], dst_of=lambda r, chip, c: r.at[chip])


def _ex_ag_sibling(arrs):
    n = len(arrs)

    def copies(ins, outs, sems, waiting):
        send, recv = sems
        x, y, c = _position()
        to = dict(device_id=(x, y, 1 - c), device_id_type=MESH)
        mine = [pltpu.make_async_remote_copy(src_ref=a.at[:, c], dst_ref=a.at[:, c], send_sem=send.at[w],
                                             recv_sem=recv.at[w], **to) for w, a in enumerate(outs)]
        theirs = [pltpu.make_async_remote_copy(src_ref=a.at[:, c], dst_ref=a.at[:, 1 - c], send_sem=send.at[w],
                                               recv_sem=recv.at[w], **to) for w, a in enumerate(outs if waiting else [])]
        return mine, [(cp, "wait_recv") for cp in theirs] + [(cp, "wait_send") for cp in mine]

    return _Exchange(arrs, [jax.ShapeDtypeStruct(a.shape, a.dtype) for a in arrs],
                     [pltpu.SemaphoreType.DMA((n,)), pltpu.SemaphoreType.DMA((n,))], copies, inplace=True)


def _ex_rs_sibling(blocks):
    n = len(blocks)

    def copies(srcs, outs, sems, waiting):
        send, recv = sems
        x, y, c = _position()
        cps = [pltpu.make_async_remote_copy(src_ref=b.at[:, 1 - c], dst_ref=l, send_sem=send.at[w], recv_sem=recv.at[w],
                                            device_id=(x, y, 1 - c), device_id_type=MESH)
               for w, (b, l) in enumerate(zip(srcs, outs))]
        return cps, [(cp, "wait") for cp in cps]

    return _Exchange(blocks, [jax.ShapeDtypeStruct((4,) + b.shape[2:], b.dtype) for b in blocks],
                     [pltpu.SemaphoreType.DMA((n,)), pltpu.SemaphoreType.DMA((n,))], copies)


def _row_tile(rows, pref=256):
    for t in range(min(pref, rows) // 16 * 16, 0, -16):
        if rows % t == 0:
            return t
    raise ValueError(f"no row tile for {rows}")


def _pair_add(blocks, landed, core, out_dtype, name):
    n, _, s0, s1 = blocks.shape
    tr = _row_tile(s0)

    def body(core_ref, a_ref, b_ref, o_ref):
        del core_ref
        o_ref[...] = (a_ref[...] + b_ref[...]).astype(o_ref.dtype)

    grid_spec = pltpu.PrefetchScalarGridSpec(
        num_scalar_prefetch=1, grid=(n, s0 // tr),
        in_specs=[pl.BlockSpec((1, None, tr, s1), lambda p, i, core: (p, core[0], i, 0)),
                  pl.BlockSpec((1, tr, s1), lambda p, i, core: (p, i, 0))],
        out_specs=pl.BlockSpec((1, tr, s1), lambda p, i, core: (p, i, 0)))
    return pl.pallas_call(
        body, name=name, grid_spec=grid_spec, out_shape=jax.ShapeDtypeStruct(landed.shape, out_dtype),
        compiler_params=_params("parallel", "parallel"),
    )(core, blocks, landed)


def _adamw_math(w, g, m, v):
    m = ADAM_B1 * m + (1.0 - ADAM_B1) * g
    v = ADAM_B2 * v + (1.0 - ADAM_B2) * (g * g)
    m_hat = m / (1.0 - ADAM_B1 ** ADAM_STEP)
    v_hat = v / (1.0 - ADAM_B2 ** ADAM_STEP)
    delta = -ADAM_LR * (m_hat / (jnp.sqrt(v_hat) + ADAM_EPS) + ADAM_WD * w)
    return delta, m, v


def _adamw_reduce(slots, w, m, v, name):
    n, R, C = slots.shape
    tr = _row_tile(R)

    def body(s_ref, w_ref, m_ref, v_ref, g_ref, d_ref, nm_ref, nv_ref):
        g = s_ref[0].astype(F32)
        for p in range(1, n):
            g = g + s_ref[p].astype(F32)
        g_ref[...] = g
        d_ref[...], nm_ref[...], nv_ref[...] = _adamw_math(w_ref[...], g, m_ref[...], v_ref[...])

    row = pl.BlockSpec((tr, C), lambda i: (i, 0))
    return pl.pallas_call(
        body, name=name, grid=(R // tr,), in_specs=[pl.BlockSpec((n, tr, C), lambda i: (0, i, 0)), row, row, row],
        out_specs=[row] * 4, out_shape=[jax.ShapeDtypeStruct((R, C), F32)] * 4, compiler_params=_params("parallel"),
    )(slots, w, m, v)


def _full_from_gathered(a, n):
    s0, s1 = a.shape[2:]
    blk = a.reshape(8, s0, s1)
    if n in COL_SHARDED:
        return blk.transpose(1, 0, 2).reshape(s0, 8 * s1)
    return blk.reshape(8 * s0, s1)


def _blocks_from_full(g, n, shard_shape):
    s0, s1 = shard_shape
    if n in COL_SHARDED:
        blk = g.reshape(s0, 8, s1).transpose(1, 0, 2)
    else:
        blk = g.reshape(8, s0, s1)
    return blk.reshape(4, 2, s0, s1)


def _swiglu_interleave(w):
    d, f2 = w.shape
    return w.reshape(d, 2, f2 // (2 * SWIGLU_TILE), SWIGLU_TILE).transpose(0, 2, 1, 3).reshape(d, f2)


def _swiglu_deinterleave(w):
    d, f2 = w.shape
    return w.reshape(d, f2 // (2 * SWIGLU_TILE), 2, SWIGLU_TILE).transpose(0, 2, 1, 3).reshape(d, f2)


SMALL_ROWS = 16


def _pack_small(vals, loss_row):
    rows = []
    for n in SMALL:
        flat = vals[n].reshape(-1)
        pad = (-flat.shape[0]) % PACK_COLS
        rows.append(jnp.pad(flat, (0, pad)).reshape(-1, PACK_COLS))
    rows.append(loss_row)
    out = jnp.concatenate(rows, axis=0)
    assert out.shape[0] == SMALL_ROWS, out.shape
    return out


def _unpack_small(packed, like):
    out, r = {}, 0
    for n in SMALL:
        size = like[n].size
        rows = -(-size // PACK_COLS)
        out[n] = packed[r:r + rows].reshape(-1)[:size].reshape(like[n].shape)
        r += rows
    return out


class _NoTraffic:
    def host(self, stage):
        return None

    def landed(self, stage, arrays):
        pass

    def grads_ready(self, names, gW, behind=None):
        pass


def _mm_behind(traffic, stage, *args, **kwargs):
    ex = traffic.host(stage)
    if ex is None:
        return _mm(*args, **kwargs)
    out, arrays = _mm(*args, hosted=ex, **kwargs)
    traffic.landed(stage, arrays)
    return out


def _ffn_fwd(x, g_pre, W, tag, traffic, up_stage=None, down_stage=None):
    h = _rms_fwd(x, g_pre, f"{tag}_pre")
    ex = traffic.host(up_stage) if up_stage else None
    u, a, arrays = _mm_swiglu(h, W[f"{tag}_w_in"], f"{tag}_up", hosted=ex)
    if ex is not None:
        traffic.landed(up_stage, arrays)
    z = _mm_behind(traffic, down_stage, a, W[f"{tag}_w_down"], "nn", F32, f"{tag}_down", tk=1408)
    return h, u, a, z


def _ffn_bwd(saved, x, g_pre, w_in, w_down, g_post, dx_out, tag, traffic, down_dw_stage=None, up_dx_stage=None):
    h, u, a, z = saved
    dz, dg_post = _rms_bwd(z, g_post, dx_out, 0.5, f"{tag}_post_bwd", BF16)
    dw_down = _mm_behind(traffic, down_dw_stage, a, dz, "tn", F32, f"{tag}_down_dw", tm=1408)
    du = _mm_swiglu_bwd(dz, w_down, u, f"{tag}_down_dx")
    dh = _mm_behind(traffic, up_dx_stage, du, w_in, "nt", BF16, f"{tag}_up_dx", tk=5632)
    dw_in = _mm(h, du, "tn", F32, f"{tag}_up_dw", tk=4096)
    dx, dg_pre = _rms_bwd(x, g_pre, dh, 1.0, f"{tag}_pre_bwd", F32, resid=dx_out)
    return dx, dg_pre, dg_post, dw_in, dw_down


def _step_local(x, mem, target, W, S, traffic=_NoTraffic()):
    T, D = x.shape
    gW, gS = {}, {}

    f1 = _ffn_fwd(x, S["ffn1_pre_g"], W, "ffn1", traffic, "gather_mixer_chips", "gather_mixer_sibling")
    x1 = _resid_rms(x, f1[3], S["ffn1_post_g"], 0.5, "ffn1_post")

    h2 = _rms_fwd(x1, S["mix_pre_g"], "mix_pre")
    pm = _mm_behind(traffic, "gather_late_chips", h2, W["w_main"], "nn", BF16, "mix_proj_main")
    pf = _mm(h2, W["w_f"], "nn", F32, "mix_proj_f")
    pg = _mm_behind(traffic, "gather_late_sibling", h2, W["w_gates"], "nn", BF16, "mix_proj_gates")
    lbl = S["hg_lb_logits"].reshape(2, HEADS, 1, DH)
    o_a, states = _hgrn_fwd(pm, lbl, "hgrn_fwd")
    oan = _hgout_fwd(o_a, pm, S["hg_norm_g"], "hgrn_out")
    bias = jnp.pad(S["fox_f_bias"], ((0, 0), (0, LANES - HEADS)))
    c = _fox_cumsum(pf, bias, "fox_cumsum")
    c_heads = c[:, :HEADS].T
    c_col, c_row = c_heads[:, :, None], c_heads[:, None, :]
    o_b, lse = _fox_fwd(pm, c_col, c_row, "fox_fwd")
    ya = _mm(oan, W["w_branch_a"], "nn", BF16, "branch_a")
    yb = _mm(o_b, W["w_branch_b"], "nn", BF16, "branch_b")
    y = _merge_fwd(ya, yb, pg, S["b_gate"], "merge")
    z2, x2 = _mm_post_norm(y, W["w_out"], x1, S["mix_post_g"], 1.0, "mix_out")

    h3 = _rms_fwd(x2, S["mem_pre_g"], "mem_pre")
    memn = _rms_fwd(mem, S["mem_kv_g"], "mem_kv_norm")
    qm = _mm(h3, W["w_mq"], "nn", BF16, "mem_q")
    kv = _mm(memn, W["w_mkv"], "nn", BF16, "mem_kv")
    om = _xattn_fwd(qm, kv, "mem_attn")
    z3, x3 = _mm_post_norm(om, W["w_mo"], x2, S["mem_post_g"], 1.0, "mem_o")

    f2 = _ffn_fwd(x3, S["ffn2_pre_g"], W, "ffn2", traffic)
    dx4, sq = _final_loss(x3, f2[3], S["ffn2_post_g"], 0.5, target, "loss")

    dx3, gS["ffn2_pre_g"], gS["ffn2_post_g"], gW["ffn2_w_in"], gW["ffn2_w_down"] = _ffn_bwd(
        f2, x3, S["ffn2_pre_g"], W["ffn2_w_in"], W["ffn2_w_down"], S["ffn2_post_g"], dx4, "ffn2", traffic)
    traffic.grads_ready(["ffn2_w_in", "ffn2_w_down"], gW, behind="scatter_ffn2_sibling")

    dz3, gS["mem_post_g"] = _rms_bwd(z3, S["mem_post_g"], dx3, 1.0, "mem_post_bwd", BF16)
    dom = _mm_behind(traffic, "scatter_ffn2_sibling", dz3, W["w_mo"], "nt", BF16, "mem_o_dx")
    gW["w_mo"] = _mm(om, dz3, "tn", F32, "mem_o_dw")
    dqm, dkv = _xattn_bwd(qm, kv, dom, "mem_attn_bwd")
    dh3 = _mm(dqm, W["w_mq"], "nt", BF16, "mem_q_dx")
    gW["w_mq"] = _mm(h3, dqm, "tn", F32, "mem_q_dw")
    dkvb = dkv.astype(BF16)
    gW["w_mkv"] = _mm(memn, dkvb, "tn", F32, "mem_kv_dw")
    dmemn = _mm(dkvb, W["w_mkv"], "nt", F32, "mem_kv_dx")
    _, gS["mem_kv_g"] = _rms_bwd(mem, S["mem_kv_g"], dmemn, 1.0, "mem_kv_norm_bwd", BF16)
    dx2, gS["mem_pre_g"] = _rms_bwd(x2, S["mem_pre_g"], dh3, 1.0, "mem_pre_bwd", F32, resid=dx3)

    dz2, gS["mix_post_g"] = _rms_bwd(z2, S["mix_post_g"], dx2, 1.0, "mix_post_bwd", BF16)
    dy = _mm(dz2, W["w_out"], "nt", BF16, "mix_out_dx")
    gW["w_out"] = _mm(y, dz2, "tn", F32, "mix_out_dw")
    dya, dyb, dpg, gS["b_gate"] = _merge_bwd(dy, ya, yb, pg, S["b_gate"], "merge_bwd")
    doan = _mm(dya, W["w_branch_a"], "nt", BF16, "branch_a_dx")
    gW["w_branch_a"] = _mm(oan, dya, "tn", F32, "branch_a_dw")
    dob = _mm(dyb, W["w_branch_b"], "nt", BF16, "branch_b_dx")
    gW["w_branch_b"] = _mm(o_b, dyb, "tn", F32, "branch_b_dw")
    traffic.grads_ready(["w_mo", "w_mq", "w_mkv", "w_out", "w_branch_a", "w_branch_b"], gW,
                        behind="scatter_mid_sibling")

    delta = _fox_delta(dob, o_b, "fox_delta")
    dq_b, dk_b, dv_b, ds_rows, ds_cols = _fox_bwd(pm, c_col, c_row, dob, lse, delta, "fox_bwd")
    dc = jnp.pad((ds_rows.reshape(HEADS, T) - ds_cols.reshape(HEADS, T)).T, ((0, 0), (0, LANES - HEADS)))
    dpf, dbias = _fox_dcum(dc, pf, bias, "fox_cumsum_bwd")
    gS["fox_f_bias"] = dbias[:, :HEADS]

    do_a, dg_a, gS["hg_norm_g"] = _hgout_bwd(o_a, pm, S["hg_norm_g"], doan, "hgrn_out_bwd")
    dq_a, df_a, di_a, dlbl = _hgrn_bwd(pm, lbl, states, do_a, "hgrn_bwd")
    gS["hg_lb_logits"] = dlbl.reshape(2, HEADS, DH)

    dpm = jnp.concatenate([dq_a, df_a, di_a, dg_a, dq_b, dk_b, dv_b], axis=1)
    dpf16 = dpf.astype(BF16)
    dh2 = _mm_behind(traffic, "scatter_ffn2_chips", dpm, W["w_main"], "nt", F32, "mix_proj_main_dx")
    dh2 = _mm_behind(traffic, "scatter_mid_sibling", dpg, W["w_gates"], "nt", F32, "mix_proj_gates_dx", add=dh2)
    dh2 = _mm(dpf16, W["w_f"], "nt", F32, "mix_proj_f_dx", add=dh2)
    gW["w_main"] = _mm_behind(traffic, "scatter_mid_chips", h2, dpm, "tn", F32, "mix_proj_main_dw")
    gW["w_gates"] = _mm(h2, dpg, "tn", F32, "mix_proj_gates_dw")
    gW["w_f"] = _mm(h2, dpf16, "tn", F32, "mix_proj_f_dw")
    traffic.grads_ready(["w_in"], gW, behind="scatter_w_in_sibling")
    dx1, gS["mix_pre_g"] = _rms_bwd(x1, S["mix_pre_g"], dh2, 1.0, "mix_pre_bwd", F32, resid=dx2)

    dx0, gS["ffn1_pre_g"], gS["ffn1_post_g"], gW["ffn1_w_in"], gW["ffn1_w_down"] = _ffn_bwd(
        f1, x, S["ffn1_pre_g"], W["ffn1_w_in"], W["ffn1_w_down"], S["ffn1_post_g"], dx1, "ffn1", traffic,
        "scatter_w_in_sibling", "scatter_w_in_chips")
    traffic.grads_ready(["ffn1_w_in", "ffn1_w_down"], gW)
    return sq, dx0, gW, gS


GATHER_FIRST = ["ffn1_w_in", "ffn1_w_down"]
GATHER_MIXER = ["w_in", "w_branch_a", "w_branch_b", "w_out"]
GATHER_LATE = ["w_mq", "w_mkv", "w_mo", "ffn2_w_in", "ffn2_w_down"]
SCATTER_BEHIND = {
    "scatter_ffn2_chips": ["ffn2_w_in", "ffn2_w_down"],
    "scatter_mid_chips": ["w_mo", "w_mq", "w_mkv", "w_out", "w_branch_a", "w_branch_b"],
    "scatter_w_in_chips": ["w_in"],
}


class _Traffic:
    def __init__(self, sent, W, shapes, core, D):
        self.sent, self.W, self.shapes, self.core, self.D = sent, W, shapes, core, D
        self.half, self.pairs, self.slots, self.waiting = {}, {}, {}, {}

    def install(self, names, gathered):
        D = self.D
        for n, g in zip(names, gathered):
            full = _full_from_gathered(g, n)
            if n == "w_in":
                self.W["w_main"] = full[:, :7 * D]
                self.W["w_f"] = jnp.pad(full[:, 7 * D:7 * D + HEADS], ((0, 0), (0, LANES - HEADS)))
                self.W["w_gates"] = full[:, 7 * D + HEADS:]
            elif n in ("ffn1_w_in", "ffn2_w_in"):
                self.W[n] = _swiglu_interleave(full)
            else:
                self.W[n] = full

    def host(self, stage):
        if stage == "gather_mixer_chips":
            return _ex_ag_chips([self.sent[n] for n in GATHER_MIXER])
        if stage == "gather_late_chips":
            return _ex_ag_chips([self.sent[n] for n in GATHER_LATE])
        if stage in ("gather_mixer_sibling", "gather_late_sibling"):
            return _ex_ag_sibling(self.half[stage])
        if stage in SCATTER_BEHIND:
            return _ex_rs_chips([self.pairs[n] for n in SCATTER_BEHIND[stage]])
        if stage in self.waiting:
            return _ex_rs_sibling(self.waiting[stage][1])
        return None

    def landed(self, stage, arrays):
        if stage == "gather_mixer_chips":
            self.half["gather_mixer_sibling"] = arrays
        elif stage == "gather_late_chips":
            self.half["gather_late_sibling"] = arrays
        elif stage == "gather_mixer_sibling":
            self.install(GATHER_MIXER, arrays)
        elif stage == "gather_late_sibling":
            self.install(GATHER_LATE, arrays)
        elif stage in self.waiting:
            self._pair_sums(*self.waiting.pop(stage), arrays)
        else:
            self.slots.update(zip(SCATTER_BEHIND[stage], arrays))

    def _final_grad(self, n, gW):
        D = self.D
        if n == "w_in":
            g = gW["w_main"]
            return jnp.concatenate([g[:, :4 * D], g[:, 4 * D:5 * D] * (1.0 / math.sqrt(DH)), g[:, 5 * D:],
                                    gW["w_f"][:, :HEADS], gW["w_gates"]], axis=1)
        if n in ("ffn1_w_in", "ffn2_w_in"):
            return _swiglu_deinterleave(gW[n])
        return gW[n]

    def _pair_sums(self, names, blocks, got):
        for n, b, l in zip(names, blocks, got):
            self.pairs[n] = _pair_add(b, l, self.core, BF16, f"rs_pair_add_{n}")

    def grads_ready(self, names, gW, behind=None):
        blocks = [_blocks_from_full(self._final_grad(n, gW), n, self.shapes[n]) for n in names]
        if behind is None:
            self._pair_sums(names, blocks, _run_exchange(_ex_rs_sibling(blocks), f"rs_sibling_{names[0]}"))
        else:
            self.waiting[behind] = (names, blocks)

    def finish(self):
        rest = [n for n in BIG if n not in self.slots]
        got = _run_exchange(_ex_rs_chips([self.pairs[n] for n in rest]), "rs_chips_last")
        self.slots.update(zip(rest, got))
        return self.slots


def _train_step(a):
    c_idx = lax.axis_index("c")
    x, mem, target = a["x"][0], a["mem"][0], a["loss_target"][0]
    D = x.shape[1]
    shards = {n: a[n][0] for n in BIG}

    fox_scale = 1.0 / math.sqrt(DH)
    n_mine = shards["w_in"].shape[1]
    dev = 4 * lax.axis_index("x") + 2 * lax.axis_index("y") + c_idx
    cols = dev * n_mine + jnp.arange(n_mine)
    is_fox_q = (cols >= 4 * D) & (cols < 5 * D)
    sent = dict(shards, w_in=shards["w_in"] * jnp.where(is_fox_q, fox_scale, 1.0)[None, :])
    sent = {n: v.astype(BF16) for n, v in sent.items()}
    W = {}
    traffic = _Traffic(sent, W, {n: shards[n].shape for n in BIG}, c_idx.astype(jnp.int32).reshape(1), D)
    first = _run_exchange(_ex_ag_chips([sent[n] for n in GATHER_FIRST]), "ag_first_chips")
    traffic.install(GATHER_FIRST, _run_exchange(_ex_ag_sibling(first), "ag_first_sibling"))
    S = {n: a[n] for n in SMALL}

    sq, grad_x, gW, gS = _step_local(x, mem, target, W, S, traffic)
    slots = traffic.finish()
    big = {n: _adamw_reduce(slots[n], shards[n], a["m_" + n][0], a["v_" + n][0], f"adamw_{n}") for n in BIG}

    loss_row = jnp.pad(sq[:1, :1] * (0.5 / D), ((0, 0), (0, PACK_COLS - 1)))
    small_half = _run_exchange(_ex_ag_chips([_pack_small(gS, loss_row)]), "small_ag_chips")
    small_all = _run_exchange(_ex_ag_sibling(small_half), "small_ag_sibling")[0]
    small_slots = small_all.reshape(8, SMALL_ROWS, PACK_COLS)
    zero_row = jnp.zeros((1, PACK_COLS), F32)
    g_sm, d_sm, m_sm, v_sm = _adamw_reduce(
        small_slots, _pack_small({n: a[n] for n in SMALL}, zero_row),
        _pack_small({n: a["m_" + n] for n in SMALL}, zero_row),
        _pack_small({n: a["v_" + n] for n in SMALL}, zero_row), "adamw_small")

    def unpack(which, small):
        out = _unpack_small(small, {n: a[n] for n in SMALL})
        for n in BIG:
            out[n] = big[n][which][None]
        return [out[n] for n in WEIGHTS]

    loss = g_sm[SMALL_ROWS - 1, 0]
    return (loss, grad_x[None], *unpack(0, g_sm), *unpack(1, d_sm), *unpack(2, m_sm), *unpack(3, v_sm))


def kernel(x, mem, ffn1_pre_g, ffn1_w_in, ffn1_w_down, ffn1_post_g, mix_pre_g, w_in, hg_lb_logits, hg_norm_g, fox_f_bias, w_branch_a, w_branch_b, b_gate, w_out, mix_post_g, mem_pre_g, mem_kv_g, w_mq, w_mkv, w_mo, mem_post_g, ffn2_pre_g, ffn2_w_in, ffn2_w_down, ffn2_post_g, loss_target, m_ffn1_pre_g, m_ffn1_w_in, m_ffn1_w_down, m_ffn1_post_g, m_mix_pre_g, m_w_in, m_hg_lb_logits, m_hg_norm_g, m_fox_f_bias, m_w_branch_a, m_w_branch_b, m_b_gate, m_w_out, m_mix_post_g, m_mem_pre_g, m_mem_kv_g, m_w_mq, m_w_mkv, m_w_mo, m_mem_post_g, m_ffn2_pre_g, m_ffn2_w_in, m_ffn2_w_down, m_ffn2_post_g, v_ffn1_pre_g, v_ffn1_w_in, v_ffn1_w_down, v_ffn1_post_g, v_mix_pre_g, v_w_in, v_hg_lb_logits, v_hg_norm_g, v_fox_f_bias, v_w_branch_a, v_w_branch_b, v_b_gate, v_w_out, v_mix_post_g, v_mem_pre_g, v_mem_kv_g, v_w_mq, v_w_mkv, v_w_mo, v_mem_post_g, v_ffn2_pre_g, v_ffn2_w_in, v_ffn2_w_down, v_ffn2_post_g):
    return _train_step(dict(locals()))
```
